```python
import jax, jax.numpy as jnp
from jax import lax
import numpy as np

D_MODEL = 1024
BATCH = 16
SEQ = 4096
DEPTH = 4

N_MIXERS = 4
EPS = 1e-6
GLA_HEADS = 4
GLA_DK = D_MODEL // 2 // GLA_HEADS
GLA_DV = D_MODEL // GLA_HEADS
GLA_GATE_RANK = 16
GLA_GATE_NORM = 16.0
GLA_CHUNK = 64
CONV_WIDTH = 31
SGU_CHUNK = 128
SGU_GROUPS = 8
SGU_DIM = D_MODEL
HGRN_EXPAND = 128
HGRN_HEADS = D_MODEL // HGRN_EXPAND
HGRN_CHUNK = 64
FFN_HIDDEN = 2816
FFN_CONV_WIDTH = 3

kernel_name = "hybrid_gla_conformer_sgu_hgrn2_trunk"


def n_uses(m):
    return (DEPTH - m + N_MIXERS - 1) // N_MIXERS


def rms_norm(x, g):
    x32 = x.astype(jnp.float32)
    y = x32 * lax.rsqrt(jnp.mean(x32 * x32, axis=-1, keepdims=True) + EPS)
    return (y * g.astype(jnp.float32)).astype(x.dtype)


def layer_norm(x, g, b):
    x32 = x.astype(jnp.float32)
    mu = jnp.mean(x32, axis=-1, keepdims=True)
    xc = x32 - mu
    y = xc * lax.rsqrt(jnp.mean(xc * xc, axis=-1, keepdims=True) + EPS)
    return (y * g.astype(jnp.float32) + b.astype(jnp.float32)).astype(x.dtype)


def causal_depthwise_conv(x, w):
    k_width, ch = w.shape
    return lax.conv_general_dilated(
        x, w.astype(x.dtype)[:, None, :], window_strides=(1,),
        padding=[(k_width - 1, 0)], dimension_numbers=("NWC", "WIO", "NWC"),
        feature_group_count=ch)


def chunk_gated_linear_attention(q, k, v, log_g, chunk):
    f32 = jnp.float32
    bsz, seq, heads, dk = q.shape
    dv = v.shape[-1]
    n = seq // chunk
    qc = q.astype(f32).reshape(bsz, n, chunk, heads, dk)
    kc = k.astype(f32).reshape(bsz, n, chunk, heads, dk)
    vc = v.astype(f32).reshape(bsz, n, chunk, heads, dv)
    b = jnp.cumsum(log_g.astype(f32).reshape(bsz, n, chunk, heads, dk), axis=2)
    b_mid = b[:, :, chunk // 2:chunk // 2 + 1]
    b_last = b[:, :, -1:]
    scores = jnp.einsum('bnihd,bnjhd->bnhij', qc * jnp.exp(b - b_mid), kc * jnp.exp(b_mid - b))
    causal = jnp.tril(jnp.ones((chunk, chunk), dtype=bool))
    scores = jnp.where(causal, scores, 0.0)
    o_intra = jnp.einsum('bnhij,bnjhe->bnihe', scores, vc)
    q_dec = qc * jnp.exp(b)
    k_dec = kc * jnp.exp(b_last - b)
    g_chunk = jnp.exp(b_last[:, :, 0])

    def step(state, xs):
        q_n, k_n, v_n, g_n = xs
        o_n = jnp.einsum('bihd,bhde->bihe', q_n, state)
        state = g_n[..., None] * state + jnp.einsum('bjhd,bjhe->bhde', k_n, v_n)
        return state, o_n

    s0 = jnp.zeros((bsz, heads, dk, dv), f32)
    _, o_inter = lax.scan(step, s0, (jnp.moveaxis(q_dec, 1, 0), jnp.moveaxis(k_dec, 1, 0),
                                     jnp.moveaxis(vc, 1, 0), jnp.moveaxis(g_chunk, 1, 0)))
    o = o_intra + jnp.moveaxis(o_inter, 0, 1)
    return o.reshape(bsz, seq, heads, dv)


def gla_mixer(h, w_in, w_g2, b_g2, norm_g, w_out):
    bsz, seq, _ = h.shape
    dk_t, dv_t = GLA_HEADS * GLA_DK, GLA_HEADS * GLA_DV
    proj = h @ w_in
    q, k, v, r, g_lr = jnp.split(proj, [dk_t, 2 * dk_t, 2 * dk_t + dv_t, 2 * dk_t + 2 * dv_t], axis=-1)
    log_g = jax.nn.log_sigmoid((g_lr @ w_g2 + b_g2).astype(jnp.float32)) / GLA_GATE_NORM
    o = chunk_gated_linear_attention(
        (q * GLA_DK ** -0.5).reshape(bsz, seq, GLA_HEADS, GLA_DK),
        k.reshape(bsz, seq, GLA_HEADS, GLA_DK),
        v.reshape(bsz, seq, GLA_HEADS, GLA_DV),
        log_g.reshape(bsz, seq, GLA_HEADS, GLA_DK), GLA_CHUNK)
    o = rms_norm(o, norm_g).reshape(bsz, seq, dv_t) * jax.nn.silu(r.astype(jnp.float32))
    return o.astype(h.dtype) @ w_out


def conformer_conv_mixer(h, w_in, b_in, w_dw, b_dw, ln_g, ln_b, w_out, b_out):
    a, gate = jnp.split(h @ w_in + b_in, 2, axis=-1)
    y = a * jax.nn.sigmoid(gate)
    y = causal_depthwise_conv(y, w_dw) + b_dw
    y = jax.nn.silu(layer_norm(y, ln_g, ln_b))
    return y @ w_out + b_out


def sgu_mixer(h, w_in, b_in, ln_g, ln_b, w_s, b_s, w_out, b_out):
    bsz, seq, _ = h.shape
    u, v = jnp.split(jax.nn.gelu(h @ w_in + b_in), 2, axis=-1)
    v = layer_norm(v, ln_g, ln_b)
    n = seq // SGU_CHUNK
    v = v.reshape(bsz, n, SGU_CHUNK, SGU_GROUPS, SGU_DIM // SGU_GROUPS)
    causal = jnp.tril(jnp.ones((SGU_CHUNK, SGU_CHUNK), dtype=bool))
    w_causal = jnp.where(causal, w_s, 0.0).astype(v.dtype)
    s = jnp.einsum('gij,bnjgc->bnigc', w_causal, v) + b_s.T.astype(v.dtype)[:, :, None]
    return (u * s.reshape(bsz, seq, SGU_DIM)) @ w_out + b_out


def hgrn2_mixer(h, w_in, lb, norm_g, w_out):
    bsz, seq, _ = h.shape
    q, f, i, g = jnp.split(h @ w_in, 4, axis=-1)
    lb = lb.astype(jnp.float32)
    log_f = jnp.logaddexp(jnp.log(lb), jnp.log1p(-lb) + jax.nn.log_sigmoid(f.astype(jnp.float32)))
    k = -jnp.expm1(log_f)
    shp = (bsz, seq, HGRN_HEADS, HGRN_EXPAND)
    o = chunk_gated_linear_attention(jax.nn.silu(q).reshape(shp), k.reshape(shp),
                                     i.reshape(shp), log_f.reshape(shp), HGRN_CHUNK)
    o = rms_norm(o, norm_g).reshape(bsz, seq, D_MODEL) * jax.nn.silu(g.astype(jnp.float32))
    return o.astype(h.dtype) @ w_out


def conv_ffn(h, w_up, w_dw, w_down):
    z = causal_depthwise_conv(h @ w_up, w_dw)
    gate, val = jnp.split(z, 2, axis=-1)
    return (jax.nn.silu(gate) * val) @ w_down


def _fwd_setup_inputs(seed: int = 0) -> dict:
    key = jax.random.key(seed)
    ks = iter(jax.random.split(key, 48))
    D = D_MODEL
    na, nb, nc, nd = n_uses(0), n_uses(1), n_uses(2), n_uses(3)

    def nrm(shape, scale):
        return scale * jax.random.normal(next(ks), shape, jnp.float32)

    def gain(shape):
        return 1.0 + nrm(shape, 0.02)

    gla_in = 2 * GLA_HEADS * GLA_DK + 2 * GLA_HEADS * GLA_DV + GLA_GATE_RANK
    return {
        "x": nrm((BATCH, SEQ, D), 1.0),
        "norm_mix": gain((DEPTH, D)),
        "norm_ffn": gain((DEPTH, D)),
        "norm_final": gain((D,)),
        "gla_w_in": nrm((na, D, gla_in), D ** -0.5),
        "gla_w_g2": nrm((na, GLA_GATE_RANK, GLA_HEADS * GLA_DK), GLA_GATE_RANK ** -0.5),
        "gla_b_g2": nrm((na, GLA_HEADS * GLA_DK), 0.01),
        "gla_norm": gain((na, GLA_DV)),
        "gla_w_out": nrm((na, GLA_HEADS * GLA_DV, D), (GLA_HEADS * GLA_DV) ** -0.5),
        "cv_w_in": nrm((nb, D, 2 * D), D ** -0.5),
        "cv_b_in": nrm((nb, 2 * D), 0.01),
        "cv_w_dw": nrm((nb, CONV_WIDTH, D), CONV_WIDTH ** -0.5),
        "cv_b_dw": nrm((nb, D), 0.01),
        "cv_ln_g": gain((nb, D)),
        "cv_ln_b": nrm((nb, D), 0.01),
        "cv_w_out": nrm((nb, D, D), D ** -0.5),
        "cv_b_out": nrm((nb, D), 0.01),
        "sg_w_in": nrm((nc, D, 2 * SGU_DIM), D ** -0.5),
        "sg_b_in": nrm((nc, 2 * SGU_DIM), 0.01),
        "sg_ln_g": gain((nc, SGU_DIM)),
        "sg_ln_b": nrm((nc, SGU_DIM), 0.01),
        "sg_w_s": nrm((nc, SGU_GROUPS, SGU_CHUNK, SGU_CHUNK), SGU_CHUNK ** -0.5),
        "sg_b_s": gain((nc, SGU_GROUPS, SGU_CHUNK)),
        "sg_w_out": nrm((nc, SGU_DIM, D), SGU_DIM ** -0.5),
        "sg_b_out": nrm((nc, D), 0.01),
        "hg_w_in": nrm((nd, D, 4 * D), D ** -0.5),
        "hg_lb_table": nrm((DEPTH, D), 0.1),
        "hg_norm": gain((nd, HGRN_EXPAND)),
        "hg_w_out": nrm((nd, D, D), D ** -0.5),
        "ffn_w_up": nrm((DEPTH, D, 2 * FFN_HIDDEN), D ** -0.5),
        "ffn_w_dw": nrm((DEPTH, FFN_CONV_WIDTH, 2 * FFN_HIDDEN), FFN_CONV_WIDTH ** -0.5),
        "ffn_w_down": nrm((DEPTH, FFN_HIDDEN, D), FFN_HIDDEN ** -0.5),
    }


def _fwd_reference(x, norm_mix, norm_ffn, norm_final,
              gla_w_in, gla_w_g2, gla_b_g2, gla_norm, gla_w_out,
              cv_w_in, cv_b_in, cv_w_dw, cv_b_dw, cv_ln_g, cv_ln_b, cv_w_out, cv_b_out,
              sg_w_in, sg_b_in, sg_ln_g, sg_ln_b, sg_w_s, sg_b_s, sg_w_out, sg_b_out,
              hg_w_in, hg_lb_table, hg_norm, hg_w_out,
              ffn_w_up, ffn_w_dw, ffn_w_down):
    lb_cum = jnp.cumsum(jax.nn.softmax(hg_lb_table.astype(jnp.float32), axis=0), axis=0)
    lower_bounds = lb_cum - lb_cum[0]
    for layer in range(DEPTH):
        m, j = layer % N_MIXERS, layer // N_MIXERS
        h = rms_norm(x, norm_mix[layer])
        if m == 0:
            y = gla_mixer(h, gla_w_in[j], gla_w_g2[j], gla_b_g2[j], gla_norm[j], gla_w_out[j])
        elif m == 1:
            y = conformer_conv_mixer(h, cv_w_in[j], cv_b_in[j], cv_w_dw[j], cv_b_dw[j],
                                     cv_ln_g[j], cv_ln_b[j], cv_w_out[j], cv_b_out[j])
        elif m == 2:
            y = sgu_mixer(h, sg_w_in[j], sg_b_in[j], sg_ln_g[j], sg_ln_b[j],
                          sg_w_s[j], sg_b_s[j], sg_w_out[j], sg_b_out[j])
        else:
            y = hgrn2_mixer(h, hg_w_in[j], lower_bounds[layer], hg_norm[j], hg_w_out[j])
        x = x + y
        x = x + conv_ffn(rms_norm(x, norm_ffn[layer]), ffn_w_up[layer], ffn_w_dw[layer], ffn_w_down[layer])
    return rms_norm(x, norm_final)


import jax as _jax
import jax.numpy as _jnp

TWIN_FORMAT = 'train_step'
FWD_PARAMS = ['x', 'norm_mix', 'norm_ffn', 'norm_final', 'gla_w_in', 'gla_w_g2', 'gla_b_g2', 'gla_norm', 'gla_w_out', 'cv_w_in', 'cv_b_in', 'cv_w_dw', 'cv_b_dw', 'cv_ln_g', 'cv_ln_b', 'cv_w_out', 'cv_b_out', 'sg_w_in', 'sg_b_in', 'sg_ln_g', 'sg_ln_b', 'sg_w_s', 'sg_b_s', 'sg_w_out', 'sg_b_out', 'hg_w_in', 'hg_lb_table', 'hg_norm', 'hg_w_out', 'ffn_w_up', 'ffn_w_dw', 'ffn_w_down']
TWIN_WEIGHTS = ['norm_mix', 'norm_ffn', 'norm_final', 'gla_w_in', 'gla_w_g2', 'gla_b_g2', 'gla_norm', 'gla_w_out', 'cv_w_in', 'cv_b_in', 'cv_w_dw', 'cv_b_dw', 'cv_ln_g', 'cv_ln_b', 'cv_w_out', 'cv_b_out', 'sg_w_in', 'sg_b_in', 'sg_ln_g', 'sg_ln_b', 'sg_w_s', 'sg_b_s', 'sg_w_out', 'sg_b_out', 'hg_w_in', 'hg_lb_table', 'hg_norm', 'hg_w_out', 'ffn_w_up', 'ffn_w_dw', 'ffn_w_down']
TWIN_DIFF_INPUT = 'x'
TWIN_INPUTS = ['x', 'norm_mix', 'norm_ffn', 'norm_final', 'gla_w_in', 'gla_w_g2', 'gla_b_g2', 'gla_norm', 'gla_w_out', 'cv_w_in', 'cv_b_in', 'cv_w_dw', 'cv_b_dw', 'cv_ln_g', 'cv_ln_b', 'cv_w_out', 'cv_b_out', 'sg_w_in', 'sg_b_in', 'sg_ln_g', 'sg_ln_b', 'sg_w_s', 'sg_b_s', 'sg_w_out', 'sg_b_out', 'hg_w_in', 'hg_lb_table', 'hg_norm', 'hg_w_out', 'ffn_w_up', 'ffn_w_dw', 'ffn_w_down', 'loss_target', 'm_norm_mix', 'm_norm_ffn', 'm_norm_final', 'm_gla_w_in', 'm_gla_w_g2', 'm_gla_b_g2', 'm_gla_norm', 'm_gla_w_out', 'm_cv_w_in', 'm_cv_b_in', 'm_cv_w_dw', 'm_cv_b_dw', 'm_cv_ln_g', 'm_cv_ln_b', 'm_cv_w_out', 'm_cv_b_out', 'm_sg_w_in', 'm_sg_b_in', 'm_sg_ln_g', 'm_sg_ln_b', 'm_sg_w_s', 'm_sg_b_s', 'm_sg_w_out', 'm_sg_b_out', 'm_hg_w_in', 'm_hg_lb_table', 'm_hg_norm', 'm_hg_w_out', 'm_ffn_w_up', 'm_ffn_w_dw', 'm_ffn_w_down', 'v_norm_mix', 'v_norm_ffn', 'v_norm_final', 'v_gla_w_in', 'v_gla_w_g2', 'v_gla_b_g2', 'v_gla_norm', 'v_gla_w_out', 'v_cv_w_in', 'v_cv_b_in', 'v_cv_w_dw', 'v_cv_b_dw', 'v_cv_ln_g', 'v_cv_ln_b', 'v_cv_w_out', 'v_cv_b_out', 'v_sg_w_in', 'v_sg_b_in', 'v_sg_ln_g', 'v_sg_ln_b', 'v_sg_w_s', 'v_sg_b_s', 'v_sg_w_out', 'v_sg_b_out', 'v_hg_w_in', 'v_hg_lb_table', 'v_hg_norm', 'v_hg_w_out', 'v_ffn_w_up', 'v_ffn_w_dw', 'v_ffn_w_down']
TWIN_OUTPUTS = ['loss', 'grad_x', 'grad_norm_mix', 'grad_norm_ffn', 'grad_norm_final', 'grad_gla_w_in', 'grad_gla_w_g2', 'grad_gla_b_g2', 'grad_gla_norm', 'grad_gla_w_out', 'grad_cv_w_in', 'grad_cv_b_in', 'grad_cv_w_dw', 'grad_cv_b_dw', 'grad_cv_ln_g', 'grad_cv_ln_b', 'grad_cv_w_out', 'grad_cv_b_out', 'grad_sg_w_in', 'grad_sg_b_in', 'grad_sg_ln_g', 'grad_sg_ln_b', 'grad_sg_w_s', 'grad_sg_b_s', 'grad_sg_w_out', 'grad_sg_b_out', 'grad_hg_w_in', 'grad_hg_lb_table', 'grad_hg_norm', 'grad_hg_w_out', 'grad_ffn_w_up', 'grad_ffn_w_dw', 'grad_ffn_w_down', 'delta_norm_mix', 'delta_norm_ffn', 'delta_norm_final', 'delta_gla_w_in', 'delta_gla_w_g2', 'delta_gla_b_g2', 'delta_gla_norm', 'delta_gla_w_out', 'delta_cv_w_in', 'delta_cv_b_in', 'delta_cv_w_dw', 'delta_cv_b_dw', 'delta_cv_ln_g', 'delta_cv_ln_b', 'delta_cv_w_out', 'delta_cv_b_out', 'delta_sg_w_in', 'delta_sg_b_in', 'delta_sg_ln_g', 'delta_sg_ln_b', 'delta_sg_w_s', 'delta_sg_b_s', 'delta_sg_w_out', 'delta_sg_b_out', 'delta_hg_w_in', 'delta_hg_lb_table', 'delta_hg_norm', 'delta_hg_w_out', 'delta_ffn_w_up', 'delta_ffn_w_dw', 'delta_ffn_w_down', 'new_m_norm_mix', 'new_m_norm_ffn', 'new_m_norm_final', 'new_m_gla_w_in', 'new_m_gla_w_g2', 'new_m_gla_b_g2', 'new_m_gla_norm', 'new_m_gla_w_out', 'new_m_cv_w_in', 'new_m_cv_b_in', 'new_m_cv_w_dw', 'new_m_cv_b_dw', 'new_m_cv_ln_g', 'new_m_cv_ln_b', 'new_m_cv_w_out', 'new_m_cv_b_out', 'new_m_sg_w_in', 'new_m_sg_b_in', 'new_m_sg_ln_g', 'new_m_sg_ln_b', 'new_m_sg_w_s', 'new_m_sg_b_s', 'new_m_sg_w_out', 'new_m_sg_b_out', 'new_m_hg_w_in', 'new_m_hg_lb_table', 'new_m_hg_norm', 'new_m_hg_w_out', 'new_m_ffn_w_up', 'new_m_ffn_w_dw', 'new_m_ffn_w_down', 'new_v_norm_mix', 'new_v_norm_ffn', 'new_v_norm_final', 'new_v_gla_w_in', 'new_v_gla_w_g2', 'new_v_gla_b_g2', 'new_v_gla_norm', 'new_v_gla_w_out', 'new_v_cv_w_in', 'new_v_cv_b_in', 'new_v_cv_w_dw', 'new_v_cv_b_dw', 'new_v_cv_ln_g', 'new_v_cv_ln_b', 'new_v_cv_w_out', 'new_v_cv_b_out', 'new_v_sg_w_in', 'new_v_sg_b_in', 'new_v_sg_ln_g', 'new_v_sg_ln_b', 'new_v_sg_w_s', 'new_v_sg_b_s', 'new_v_sg_w_out', 'new_v_sg_b_out', 'new_v_hg_w_in', 'new_v_hg_lb_table', 'new_v_hg_norm', 'new_v_hg_w_out', 'new_v_ffn_w_up', 'new_v_ffn_w_dw', 'new_v_ffn_w_down']
TWIN_LEAF_KINDS = {'loss': 'loss', 'grad_x': 'grad_x', 'grad_norm_mix': 'grad_w', 'grad_norm_ffn': 'grad_w', 'grad_norm_final': 'grad_w', 'grad_gla_w_in': 'grad_w', 'grad_gla_w_g2': 'grad_w', 'grad_gla_b_g2': 'grad_w', 'grad_gla_norm': 'grad_w', 'grad_gla_w_out': 'grad_w', 'grad_cv_w_in': 'grad_w', 'grad_cv_b_in': 'grad_w', 'grad_cv_w_dw': 'grad_w', 'grad_cv_b_dw': 'grad_w', 'grad_cv_ln_g': 'grad_w', 'grad_cv_ln_b': 'grad_w', 'grad_cv_w_out': 'grad_w', 'grad_cv_b_out': 'grad_w', 'grad_sg_w_in': 'grad_w', 'grad_sg_b_in': 'grad_w', 'grad_sg_ln_g': 'grad_w', 'grad_sg_ln_b': 'grad_w', 'grad_sg_w_s': 'grad_w', 'grad_sg_b_s': 'grad_w', 'grad_sg_w_out': 'grad_w', 'grad_sg_b_out': 'grad_w', 'grad_hg_w_in': 'grad_w', 'grad_hg_lb_table': 'grad_w', 'grad_hg_norm': 'grad_w', 'grad_hg_w_out': 'grad_w', 'grad_ffn_w_up': 'grad_w', 'grad_ffn_w_dw': 'grad_w', 'grad_ffn_w_down': 'grad_w', 'delta_norm_mix': 'delta_w', 'delta_norm_ffn': 'delta_w', 'delta_norm_final': 'delta_w', 'delta_gla_w_in': 'delta_w', 'delta_gla_w_g2': 'delta_w', 'delta_gla_b_g2': 'delta_w', 'delta_gla_norm': 'delta_w', 'delta_gla_w_out': 'delta_w', 'delta_cv_w_in': 'delta_w', 'delta_cv_b_in': 'delta_w', 'delta_cv_w_dw': 'delta_w', 'delta_cv_b_dw': 'delta_w', 'delta_cv_ln_g': 'delta_w', 'delta_cv_ln_b': 'delta_w', 'delta_cv_w_out': 'delta_w', 'delta_cv_b_out': 'delta_w', 'delta_sg_w_in': 'delta_w', 'delta_sg_b_in': 'delta_w', 'delta_sg_ln_g': 'delta_w', 'delta_sg_ln_b': 'delta_w', 'delta_sg_w_s': 'delta_w', 'delta_sg_b_s': 'delta_w', 'delta_sg_w_out': 'delta_w', 'delta_sg_b_out': 'delta_w', 'delta_hg_w_in': 'delta_w', 'delta_hg_lb_table': 'delta_w', 'delta_hg_norm': 'delta_w', 'delta_hg_w_out': 'delta_w', 'delta_ffn_w_up': 'delta_w', 'delta_ffn_w_dw': 'delta_w', 'delta_ffn_w_down': 'delta_w', 'new_m_norm_mix': 'new_m', 'new_m_norm_ffn': 'new_m', 'new_m_norm_final': 'new_m', 'new_m_gla_w_in': 'new_m', 'new_m_gla_w_g2': 'new_m', 'new_m_gla_b_g2': 'new_m', 'new_m_gla_norm': 'new_m', 'new_m_gla_w_out': 'new_m', 'new_m_cv_w_in': 'new_m', 'new_m_cv_b_in': 'new_m', 'new_m_cv_w_dw': 'new_m', 'new_m_cv_b_dw': 'new_m', 'new_m_cv_ln_g': 'new_m', 'new_m_cv_ln_b': 'new_m', 'new_m_cv_w_out': 'new_m', 'new_m_cv_b_out': 'new_m', 'new_m_sg_w_in': 'new_m', 'new_m_sg_b_in': 'new_m', 'new_m_sg_ln_g': 'new_m', 'new_m_sg_ln_b': 'new_m', 'new_m_sg_w_s': 'new_m', 'new_m_sg_b_s': 'new_m', 'new_m_sg_w_out': 'new_m', 'new_m_sg_b_out': 'new_m', 'new_m_hg_w_in': 'new_m', 'new_m_hg_lb_table': 'new_m', 'new_m_hg_norm': 'new_m', 'new_m_hg_w_out': 'new_m', 'new_m_ffn_w_up': 'new_m', 'new_m_ffn_w_dw': 'new_m', 'new_m_ffn_w_down': 'new_m', 'new_v_norm_mix': 'new_v', 'new_v_norm_ffn': 'new_v', 'new_v_norm_final': 'new_v', 'new_v_gla_w_in': 'new_v', 'new_v_gla_w_g2': 'new_v', 'new_v_gla_b_g2': 'new_v', 'new_v_gla_norm': 'new_v', 'new_v_gla_w_out': 'new_v', 'new_v_cv_w_in': 'new_v', 'new_v_cv_b_in': 'new_v', 'new_v_cv_w_dw': 'new_v', 'new_v_cv_b_dw': 'new_v', 'new_v_cv_ln_g': 'new_v', 'new_v_cv_ln_b': 'new_v', 'new_v_cv_w_out': 'new_v', 'new_v_cv_b_out': 'new_v', 'new_v_sg_w_in': 'new_v', 'new_v_sg_b_in': 'new_v', 'new_v_sg_ln_g': 'new_v', 'new_v_sg_ln_b': 'new_v', 'new_v_sg_w_s': 'new_v', 'new_v_sg_b_s': 'new_v', 'new_v_sg_w_out': 'new_v', 'new_v_sg_b_out': 'new_v', 'new_v_hg_w_in': 'new_v', 'new_v_hg_lb_table': 'new_v', 'new_v_hg_norm': 'new_v', 'new_v_hg_w_out': 'new_v', 'new_v_ffn_w_up': 'new_v', 'new_v_ffn_w_dw': 'new_v', 'new_v_ffn_w_down': 'new_v'}


def _forward(args):
    return _fwd_reference(*[args[k] for k in FWD_PARAMS])


def _output_shape():
    out = _jax.eval_shape(lambda: _forward(_fwd_setup_inputs(0)))
    return out.shape, out.dtype

N_MICROBATCH = 1
ADAM_LR = 0.001
ADAM_B1 = 0.9
ADAM_B2 = 0.999
ADAM_EPS = 1e-08
ADAM_WD = 0.01
ADAM_STEP = 10
PER_EXAMPLE_BATCH_AXIS = {'x': 0, 'loss_target': 0}
SHARED_INPUTS = []
_WEIGHT_DTYPES = {'norm_mix': _jnp.float32, 'norm_ffn': _jnp.float32, 'norm_final': _jnp.float32, 'gla_w_in': _jnp.float32, 'gla_w_g2': _jnp.float32, 'gla_b_g2': _jnp.float32, 'gla_norm': _jnp.float32, 'gla_w_out': _jnp.float32, 'cv_w_in': _jnp.float32, 'cv_b_in': _jnp.float32, 'cv_w_dw': _jnp.float32, 'cv_b_dw': _jnp.float32, 'cv_ln_g': _jnp.float32, 'cv_ln_b': _jnp.float32, 'cv_w_out': _jnp.float32, 'cv_b_out': _jnp.float32, 'sg_w_in': _jnp.float32, 'sg_b_in': _jnp.float32, 'sg_ln_g': _jnp.float32, 'sg_ln_b': _jnp.float32, 'sg_w_s': _jnp.float32, 'sg_b_s': _jnp.float32, 'sg_w_out': _jnp.float32, 'sg_b_out': _jnp.float32, 'hg_w_in': _jnp.float32, 'hg_lb_table': _jnp.float32, 'hg_norm': _jnp.float32, 'hg_w_out': _jnp.float32, 'ffn_w_up': _jnp.float32, 'ffn_w_dw': _jnp.float32, 'ffn_w_down': _jnp.float32}
MOMENT_SCALE = {'norm_mix': 2.337922e-01, 'norm_ffn': 1.642653e-01, 'norm_final': 6.403752e+01, 'gla_w_in': 2.224602e-01, 'gla_w_g2': 3.354887e-02, 'gla_b_g2': 1.496918e-01, 'gla_norm': 3.800733e-01, 'gla_w_out': 1.884635e-01, 'cv_w_in': 1.060298e-01, 'cv_b_in': 1.144670e-01, 'cv_w_dw': 1.395641e-01, 'cv_b_dw': 3.313059e-01, 'cv_ln_g': 1.695878e-01, 'cv_ln_b': 1.600022e-01, 'cv_w_out': 1.386937e-01, 'cv_b_out': 2.801845e-01, 'sg_w_in': 1.130580e-01, 'sg_b_in': 1.217993e-01, 'sg_ln_g': 7.699766e-02, 'sg_ln_b': 7.574841e-02, 'sg_w_s': 7.524235e-02, 'sg_b_s': 1.113623e-01, 'sg_w_out': 1.384209e-01, 'sg_b_out': 2.280549e-01, 'hg_w_in': 5.735456e-02, 'hg_lb_table': 5.339663e-03, 'hg_norm': 2.380295e-01, 'hg_w_out': 8.021508e-02, 'ffn_w_up': 6.917079e-02, 'ffn_w_dw': 6.888201e-02, 'ffn_w_down': 1.126146e-01}


def _to_microbatches(a, axis):
    t = _jnp.moveaxis(a, axis, 0)
    t = t.reshape((N_MICROBATCH, t.shape[0] // N_MICROBATCH) + t.shape[1:])
    return _jnp.moveaxis(t, 1, axis + 1)


def setup_inputs(seed: int = 0) -> dict:
    inp = _fwd_setup_inputs(seed)
    key = _jax.random.fold_in(_jax.random.key(seed), 7919)
    shape, _ = _output_shape()
    out = dict(inp)
    out["loss_target"] = _jax.random.normal(_jax.random.fold_in(key, 0), shape, _jnp.float32)
    for i, name in enumerate(TWIN_WEIGHTS):
        w = inp[name].astype(_jnp.float32)
        if MOMENT_SCALE is None:
            s = _jnp.sqrt(_jnp.mean(_jnp.square(w)) + 1e-30)
        else:
            s = MOMENT_SCALE[name]
        km, kv = _jax.random.split(_jax.random.fold_in(key, i + 1))
        out[name] = w
        out["m_" + name] = s * _jax.random.normal(km, w.shape, _jnp.float32)
        out["v_" + name] = (s * s) * _jax.random.uniform(kv, w.shape, _jnp.float32, 0.5, 1.5)
    if N_MICROBATCH > 1:
        for name, axis in PER_EXAMPLE_BATCH_AXIS.items():
            out[name] = _to_microbatches(out[name], axis)
    return {'x': out['x'], 'norm_mix': out['norm_mix'], 'norm_ffn': out['norm_ffn'], 'norm_final': out['norm_final'], 'gla_w_in': out['gla_w_in'], 'gla_w_g2': out['gla_w_g2'], 'gla_b_g2': out['gla_b_g2'], 'gla_norm': out['gla_norm'], 'gla_w_out': out['gla_w_out'], 'cv_w_in': out['cv_w_in'], 'cv_b_in': out['cv_b_in'], 'cv_w_dw': out['cv_w_dw'], 'cv_b_dw': out['cv_b_dw'], 'cv_ln_g': out['cv_ln_g'], 'cv_ln_b': out['cv_ln_b'], 'cv_w_out': out['cv_w_out'], 'cv_b_out': out['cv_b_out'], 'sg_w_in': out['sg_w_in'], 'sg_b_in': out['sg_b_in'], 'sg_ln_g': out['sg_ln_g'], 'sg_ln_b': out['sg_ln_b'], 'sg_w_s': out['sg_w_s'], 'sg_b_s': out['sg_b_s'], 'sg_w_out': out['sg_w_out'], 'sg_b_out': out['sg_b_out'], 'hg_w_in': out['hg_w_in'], 'hg_lb_table': out['hg_lb_table'], 'hg_norm': out['hg_norm'], 'hg_w_out': out['hg_w_out'], 'ffn_w_up': out['ffn_w_up'], 'ffn_w_dw': out['ffn_w_dw'], 'ffn_w_down': out['ffn_w_down'], 'loss_target': out['loss_target'], 'm_norm_mix': out['m_norm_mix'], 'm_norm_ffn': out['m_norm_ffn'], 'm_norm_final': out['m_norm_final'], 'm_gla_w_in': out['m_gla_w_in'], 'm_gla_w_g2': out['m_gla_w_g2'], 'm_gla_b_g2': out['m_gla_b_g2'], 'm_gla_norm': out['m_gla_norm'], 'm_gla_w_out': out['m_gla_w_out'], 'm_cv_w_in': out['m_cv_w_in'], 'm_cv_b_in': out['m_cv_b_in'], 'm_cv_w_dw': out['m_cv_w_dw'], 'm_cv_b_dw': out['m_cv_b_dw'], 'm_cv_ln_g': out['m_cv_ln_g'], 'm_cv_ln_b': out['m_cv_ln_b'], 'm_cv_w_out': out['m_cv_w_out'], 'm_cv_b_out': out['m_cv_b_out'], 'm_sg_w_in': out['m_sg_w_in'], 'm_sg_b_in': out['m_sg_b_in'], 'm_sg_ln_g': out['m_sg_ln_g'], 'm_sg_ln_b': out['m_sg_ln_b'], 'm_sg_w_s': out['m_sg_w_s'], 'm_sg_b_s': out['m_sg_b_s'], 'm_sg_w_out': out['m_sg_w_out'], 'm_sg_b_out': out['m_sg_b_out'], 'm_hg_w_in': out['m_hg_w_in'], 'm_hg_lb_table': out['m_hg_lb_table'], 'm_hg_norm': out['m_hg_norm'], 'm_hg_w_out': out['m_hg_w_out'], 'm_ffn_w_up': out['m_ffn_w_up'], 'm_ffn_w_dw': out['m_ffn_w_dw'], 'm_ffn_w_down': out['m_ffn_w_down'], 'v_norm_mix': out['v_norm_mix'], 'v_norm_ffn': out['v_norm_ffn'], 'v_norm_final': out['v_norm_final'], 'v_gla_w_in': out['v_gla_w_in'], 'v_gla_w_g2': out['v_gla_w_g2'], 'v_gla_b_g2': out['v_gla_b_g2'], 'v_gla_norm': out['v_gla_norm'], 'v_gla_w_out': out['v_gla_w_out'], 'v_cv_w_in': out['v_cv_w_in'], 'v_cv_b_in': out['v_cv_b_in'], 'v_cv_w_dw': out['v_cv_w_dw'], 'v_cv_b_dw': out['v_cv_b_dw'], 'v_cv_ln_g': out['v_cv_ln_g'], 'v_cv_ln_b': out['v_cv_ln_b'], 'v_cv_w_out': out['v_cv_w_out'], 'v_cv_b_out': out['v_cv_b_out'], 'v_sg_w_in': out['v_sg_w_in'], 'v_sg_b_in': out['v_sg_b_in'], 'v_sg_ln_g': out['v_sg_ln_g'], 'v_sg_ln_b': out['v_sg_ln_b'], 'v_sg_w_s': out['v_sg_w_s'], 'v_sg_b_s': out['v_sg_b_s'], 'v_sg_w_out': out['v_sg_w_out'], 'v_sg_b_out': out['v_sg_b_out'], 'v_hg_w_in': out['v_hg_w_in'], 'v_hg_lb_table': out['v_hg_lb_table'], 'v_hg_norm': out['v_hg_norm'], 'v_hg_w_out': out['v_hg_w_out'], 'v_ffn_w_up': out['v_ffn_w_up'], 'v_ffn_w_dw': out['v_ffn_w_dw'], 'v_ffn_w_down': out['v_ffn_w_down']}


def _loss(weights, diff, rest, loss_target):
    with _jax.named_scope("forward"):
        args = {**rest, TWIN_DIFF_INPUT: diff, **{k: w.astype(_WEIGHT_DTYPES[k]) for k, w in weights.items()}}
        y = _forward(args)
    with _jax.named_scope("loss_head"):
        err = _jnp.square(y.astype(_jnp.float32) - loss_target)
        return 0.5 * _jnp.sum(_jnp.mean(err, axis=-1)) if err.ndim else 0.5 * err


def _adamw(w, g, m, v):
    m = ADAM_B1 * m + (1.0 - ADAM_B1) * g
    v = ADAM_B2 * v + (1.0 - ADAM_B2) * _jnp.square(g)
    m_hat = m / (1.0 - ADAM_B1 ** ADAM_STEP)
    v_hat = v / (1.0 - ADAM_B2 ** ADAM_STEP)
    delta = -ADAM_LR * (m_hat / (_jnp.sqrt(v_hat) + ADAM_EPS) + ADAM_WD * w)
    return delta, m, v


def reference(x, norm_mix, norm_ffn, norm_final, gla_w_in, gla_w_g2, gla_b_g2, gla_norm, gla_w_out, cv_w_in, cv_b_in, cv_w_dw, cv_b_dw, cv_ln_g, cv_ln_b, cv_w_out, cv_b_out, sg_w_in, sg_b_in, sg_ln_g, sg_ln_b, sg_w_s, sg_b_s, sg_w_out, sg_b_out, hg_w_in, hg_lb_table, hg_norm, hg_w_out, ffn_w_up, ffn_w_dw, ffn_w_down, loss_target, m_norm_mix, m_norm_ffn, m_norm_final, m_gla_w_in, m_gla_w_g2, m_gla_b_g2, m_gla_norm, m_gla_w_out, m_cv_w_in, m_cv_b_in, m_cv_w_dw, m_cv_b_dw, m_cv_ln_g, m_cv_ln_b, m_cv_w_out, m_cv_b_out, m_sg_w_in, m_sg_b_in, m_sg_ln_g, m_sg_ln_b, m_sg_w_s, m_sg_b_s, m_sg_w_out, m_sg_b_out, m_hg_w_in, m_hg_lb_table, m_hg_norm, m_hg_w_out, m_ffn_w_up, m_ffn_w_dw, m_ffn_w_down, v_norm_mix, v_norm_ffn, v_norm_final, v_gla_w_in, v_gla_w_g2, v_gla_b_g2, v_gla_norm, v_gla_w_out, v_cv_w_in, v_cv_b_in, v_cv_w_dw, v_cv_b_dw, v_cv_ln_g, v_cv_ln_b, v_cv_w_out, v_cv_b_out, v_sg_w_in, v_sg_b_in, v_sg_ln_g, v_sg_ln_b, v_sg_w_s, v_sg_b_s, v_sg_w_out, v_sg_b_out, v_hg_w_in, v_hg_lb_table, v_hg_norm, v_hg_w_out, v_ffn_w_up, v_ffn_w_dw, v_ffn_w_down):
    given = dict(x=x, norm_mix=norm_mix, norm_ffn=norm_ffn, norm_final=norm_final, gla_w_in=gla_w_in, gla_w_g2=gla_w_g2, gla_b_g2=gla_b_g2, gla_norm=gla_norm, gla_w_out=gla_w_out, cv_w_in=cv_w_in, cv_b_in=cv_b_in, cv_w_dw=cv_w_dw, cv_b_dw=cv_b_dw, cv_ln_g=cv_ln_g, cv_ln_b=cv_ln_b, cv_w_out=cv_w_out, cv_b_out=cv_b_out, sg_w_in=sg_w_in, sg_b_in=sg_b_in, sg_ln_g=sg_ln_g, sg_ln_b=sg_ln_b, sg_w_s=sg_w_s, sg_b_s=sg_b_s, sg_w_out=sg_w_out, sg_b_out=sg_b_out, hg_w_in=hg_w_in, hg_lb_table=hg_lb_table, hg_norm=hg_norm, hg_w_out=hg_w_out, ffn_w_up=ffn_w_up, ffn_w_dw=ffn_w_dw, ffn_w_down=ffn_w_down, loss_target=loss_target, m_norm_mix=m_norm_mix, m_norm_ffn=m_norm_ffn, m_norm_final=m_norm_final, m_gla_w_in=m_gla_w_in, m_gla_w_g2=m_gla_w_g2, m_gla_b_g2=m_gla_b_g2, m_gla_norm=m_gla_norm, m_gla_w_out=m_gla_w_out, m_cv_w_in=m_cv_w_in, m_cv_b_in=m_cv_b_in, m_cv_w_dw=m_cv_w_dw, m_cv_b_dw=m_cv_b_dw, m_cv_ln_g=m_cv_ln_g, m_cv_ln_b=m_cv_ln_b, m_cv_w_out=m_cv_w_out, m_cv_b_out=m_cv_b_out, m_sg_w_in=m_sg_w_in, m_sg_b_in=m_sg_b_in, m_sg_ln_g=m_sg_ln_g, m_sg_ln_b=m_sg_ln_b, m_sg_w_s=m_sg_w_s, m_sg_b_s=m_sg_b_s, m_sg_w_out=m_sg_w_out, m_sg_b_out=m_sg_b_out, m_hg_w_in=m_hg_w_in, m_hg_lb_table=m_hg_lb_table, m_hg_norm=m_hg_norm, m_hg_w_out=m_hg_w_out, m_ffn_w_up=m_ffn_w_up, m_ffn_w_dw=m_ffn_w_dw, m_ffn_w_down=m_ffn_w_down, v_norm_mix=v_norm_mix, v_norm_ffn=v_norm_ffn, v_norm_final=v_norm_final, v_gla_w_in=v_gla_w_in, v_gla_w_g2=v_gla_w_g2, v_gla_b_g2=v_gla_b_g2, v_gla_norm=v_gla_norm, v_gla_w_out=v_gla_w_out, v_cv_w_in=v_cv_w_in, v_cv_b_in=v_cv_b_in, v_cv_w_dw=v_cv_w_dw, v_cv_b_dw=v_cv_b_dw, v_cv_ln_g=v_cv_ln_g, v_cv_ln_b=v_cv_ln_b, v_cv_w_out=v_cv_w_out, v_cv_b_out=v_cv_b_out, v_sg_w_in=v_sg_w_in, v_sg_b_in=v_sg_b_in, v_sg_ln_g=v_sg_ln_g, v_sg_ln_b=v_sg_ln_b, v_sg_w_s=v_sg_w_s, v_sg_b_s=v_sg_b_s, v_sg_w_out=v_sg_w_out, v_sg_b_out=v_sg_b_out, v_hg_w_in=v_hg_w_in, v_hg_lb_table=v_hg_lb_table, v_hg_norm=v_hg_norm, v_hg_w_out=v_hg_w_out, v_ffn_w_up=v_ffn_w_up, v_ffn_w_dw=v_ffn_w_dw, v_ffn_w_down=v_ffn_w_down)
    weights = {n: given[n] for n in TWIN_WEIGHTS}
    shared = {n: given[n] for n in SHARED_INPUTS}
    per_example = {n: given[n] for n in ['x']}
    grad_fn = _jax.value_and_grad(_loss, argnums=(0, 1))

    def one_microbatch(ex, loss_target):
        ex = dict(ex)
        diff = ex.pop(TWIN_DIFF_INPUT)
        return grad_fn(weights, diff, {**shared, **ex}, loss_target)

    if N_MICROBATCH == 1:
        loss, (grad_w, grad_x) = one_microbatch(per_example, given["loss_target"])
    else:
        def body(carry, xs):
            loss_sum, grad_sum = carry
            l_k, (gw_k, gx_k) = one_microbatch(xs[0], xs[1])
            with _jax.named_scope("update"):
                return (loss_sum + l_k, _jax.tree.map(_jnp.add, grad_sum, gw_k)), gx_k

        init = (_jnp.zeros((), _jnp.float32), _jax.tree.map(_jnp.zeros_like, weights))
        (loss, grad_w), grad_x = _jax.lax.scan(body, init, (per_example, given["loss_target"]))
    with _jax.named_scope("update"):
        delta_w, new_m, new_v = {}, {}, {}
        for n in TWIN_WEIGHTS:
            delta_w[n], new_m[n], new_v[n] = _adamw(weights[n], grad_w[n], given["m_" + n], given["v_" + n])
    return (loss, grad_x, *[grad_w[n] for n in TWIN_WEIGHTS], *[delta_w[n] for n in TWIN_WEIGHTS],
            *[new_m[n] for n in TWIN_WEIGHTS], *[new_v[n] for n in TWIN_WEIGHTS])
```

```python
import functools
import math

import jax
import jax.numpy as jnp
import numpy as np
from jax import lax
from jax.experimental import pallas as pl
from jax.experimental.pallas import tpu as pltpu

F32 = jnp.float32
BF16 = jnp.bfloat16
EPS = 1e-6
N_DEV = 8
LANES = 128
SUBLANES_BF16 = 16
HALO = 32
GLA_HEADS, GLA_RANK, GLA_GATE_NORM, GLA_CHUNK = 4, 16, 16.0, 64
SGU_CHUNK, SGU_GROUPS = 128, 8
HGRN_EXPAND, HGRN_CHUNK = 128, 64
CONV_WIDTH, FFN_CONV_WIDTH = 31, 3
ADAM_LR, ADAM_B1, ADAM_B2, ADAM_EPS, ADAM_WD, ADAM_STEP = 0.001, 0.9, 0.999, 1e-08, 0.01, 10
MESH = pl.DeviceIdType.MESH


def _sigmoid(x):
    return 0.5 * (jnp.tanh(0.5 * x) + 1.0)


def _silu(x):
    return x * _sigmoid(x)


def _log_sigmoid(x):
    return jnp.minimum(x, 0.0) - jnp.log(1.0 + jnp.exp(-jnp.abs(x)))


def _gelu(x):
    return 0.5 * x * (1.0 + jnp.tanh(math.sqrt(2.0 / math.pi) * (x + 0.044715 * (x * x * x))))


def _rms(x, g):
    return x * lax.rsqrt(jnp.mean(x * x, axis=-1, keepdims=True) + EPS) * g


def _layer_norm(x, g, b):
    xc = x - jnp.mean(x, axis=-1, keepdims=True)
    return xc * lax.rsqrt(jnp.mean(xc * xc, axis=-1, keepdims=True) + EPS) * g + b


def _dot_raw(a, b, dims):
    return lax.dot_general(a.astype(BF16), b.astype(BF16), (dims, ((), ())), preferred_element_type=F32)


@jax.custom_vjp
def _bdot(a, b):
    return _dot_raw(a, b, ((1,), (0,)))


@jax.custom_vjp
def _bdot_nt(a, b):
    return _dot_raw(a, b, ((1,), (1,)))


@jax.custom_vjp
def _bdot_tn(a, b):
    return _dot_raw(a, b, ((0,), (0,)))


_bdot.defvjp(lambda a, b: (_bdot(a, b), (a, b)), lambda r, g: (_bdot_nt(g, r[1]), _bdot_tn(r[0], g)))
_bdot_nt.defvjp(lambda a, b: (_bdot_nt(a, b), (a, b)), lambda r, g: (_bdot(g, r[1]), _bdot_tn(g, r[0])))
_bdot_tn.defvjp(lambda a, b: (_bdot_tn(a, b), (a, b)), lambda r, g: (_bdot_nt(r[1], g), _bdot(r[0], g)))


def _hdot(a, b):
    return jnp.dot(a, b, precision=lax.Precision.HIGHEST, preferred_element_type=F32)


def _divisor_tile(n, cap, unit):
    if n <= cap:
        return n
    best = None
    for t in range(unit, cap + 1, unit):
        if n % t == 0:
            best = t
    assert best is not None, (n, cap, unit)
    return best


def _const_map(nd):
    return lambda *_: (0,) * nd


def _mm(a, b, *, add=None, bias=None, out_dtype=F32, name):
    m, k = a.shape
    k2, n = b.shape
    assert k == k2
    tn = _divisor_tile(n, max(LANES, min(1408, (6 << 20) // (2 * k) // LANES * LANES)), LANES)
    tm = _divisor_tile(m, max(256, min(1024, (4 << 20) // (a.dtype.itemsize * k) // 256 * 256)), 8)
    has_bias, has_add = bias is not None, add is not None

    def body(*refs):
        a_ref, b_ref = refs[0], refs[1]
        o_ref = refs[-1]
        acc = jnp.dot(a_ref[...].astype(BF16), b_ref[...], preferred_element_type=F32)
        pos = 2
        if has_bias:
            acc = acc + refs[pos][...]
            pos += 1
        if has_add:
            acc = acc + refs[pos][...].astype(F32)
        o_ref[...] = acc.astype(o_ref.dtype)

    in_specs = [pl.BlockSpec((tm, k), lambda i, j: (i, 0)), pl.BlockSpec((k, tn), lambda i, j: (0, j))]
    args = [a, b]
    if has_bias:
        in_specs.append(pl.BlockSpec((1, tn), lambda i, j: (0, j)))
        args.append(bias)
    if has_add:
        in_specs.append(pl.BlockSpec((tm, tn), lambda i, j: (i, j)))
        args.append(add)
    return pl.pallas_call(
        body, name=name, grid=(m // tm, n // tn), in_specs=in_specs,
        out_specs=pl.BlockSpec((tm, tn), lambda i, j: (i, j)),
        out_shape=jax.ShapeDtypeStruct((m, n), out_dtype),
        compiler_params=pltpu.CompilerParams(dimension_semantics=("parallel", "parallel")),
    )(*args)


def _mm_tn(a, g, *, name):
    m, k = a.shape
    m2, n = g.shape
    assert m == m2
    tk = _divisor_tile(k, 1408, LANES)
    tn = _divisor_tile(n, 1408, LANES)
    tm = _divisor_tile(m, 1024, 8)

    def body(a_ref, g_ref, o_ref):
        @pl.when(pl.program_id(2) == 0)
        def _():
            o_ref[...] = jnp.zeros_like(o_ref)

        o_ref[...] += _dot_raw(a_ref[...], g_ref[...], ((0,), (0,)))

    return pl.pallas_call(
        body, name=name, grid=(k // tk, n // tn, m // tm),
        in_specs=[pl.BlockSpec((tm, tk), lambda i, j, t: (t, i)), pl.BlockSpec((tm, tn), lambda i, j, t: (t, j))],
        out_specs=pl.BlockSpec((tk, tn), lambda i, j, t: (i, j)),
        out_shape=jax.ShapeDtypeStruct((k, n), F32),
        compiler_params=pltpu.CompilerParams(dimension_semantics=("parallel", "parallel", "arbitrary")),
    )(a, g)


def _tile_call(name, fn, tiled, params, out_tiled, out_acc, tile):
    tiled = [t if isinstance(t, tuple) else (t, t.shape[1], 0) for t in tiled]
    t_rows = tiled[0][0].shape[0]
    tile = min(tile, t_rows)
    assert t_rows % tile == 0
    n_t, n_p, n_o = len(tiled), len(params), len(out_tiled)

    def body(*refs):
        vals = [r[...] for r in refs[: n_t + n_p]]
        touts, aouts = fn(*vals)
        for r, v in zip(refs[n_t + n_p: n_t + n_p + n_o], touts):
            r[...] = v.astype(r.dtype)
        acc_refs = refs[n_t + n_p + n_o:]
        if acc_refs:
            @pl.when(pl.program_id(0) == 0)
            def _():
                for r in acc_refs:
                    r[...] = jnp.zeros_like(r)

            for r, v in zip(acc_refs, aouts):
                r[...] += v

    in_specs = [pl.BlockSpec((tile, w), lambda i, cb=cb: (i, cb)) for _, w, cb in tiled]
    in_specs += [pl.BlockSpec(p.shape, _const_map(p.ndim)) for p in params]
    out_specs = [pl.BlockSpec((tile, w), lambda i: (i, 0)) for w, _ in out_tiled]
    out_specs += [pl.BlockSpec(s, _const_map(len(s))) for s in out_acc]
    out_shape = [jax.ShapeDtypeStruct((t_rows, w), dt) for w, dt in out_tiled]
    out_shape += [jax.ShapeDtypeStruct(s, F32) for s in out_acc]
    res = pl.pallas_call(
        body, name=name, grid=(t_rows // tile,), in_specs=in_specs, out_specs=out_specs, out_shape=out_shape,
        compiler_params=pltpu.CompilerParams(dimension_semantics=("arbitrary" if out_acc else "parallel",)),
    )(*[t[0] for t in tiled], *params)
    return res[:n_o], res[n_o:]


def _f32(vals):
    return [v.astype(F32) for v in vals]


def _rms_fwd(x, g, name):
    (h,), _ = _tile_call(name, lambda xv, gv: ([_rms(xv, gv)], []), [x], [g], [(x.shape[1], BF16)], [], 512)
    return h


def _rms_bwd(x, g, dh, dres, name):
    def fn(xv, dhv, drv, gv):
        _, vjp = jax.vjp(_rms, xv, gv)
        dx, dg = vjp(dhv.astype(F32))
        return [drv + dx], [dg]

    (dx,), (dg,) = _tile_call(name, fn, [x, dh, dres], [g], [(x.shape[1], F32)], [g.shape], 512)
    return dx, dg


def _colsum(x, name):
    _, (s,) = _tile_call(name, lambda xv: ([], [jnp.sum(xv.astype(F32), axis=0, keepdims=True)]), [x], [], [],
                         [(1, x.shape[1])], 512)
    return s


def _seq_flags(i, tiles_per_seq):
    pos = i % tiles_per_seq
    return pos == 0, pos == tiles_per_seq - 1


def _dwconv_fwd(x, w, b, seq, name):
    t_rows, ch = x.shape
    kw = w.shape[0]
    tile = min(512, seq)
    cb = _divisor_tile(ch, 256, LANES)
    tps, hb = seq // tile, tile // HALO

    def body(x_ref, halo_ref, w_ref, b_ref, y_ref, pad_ref):
        first, _ = _seq_flags(pl.program_id(0), tps)
        pad_ref[0:HALO, :] = jnp.where(first, 0.0, halo_ref[...])
        pad_ref[HALO:HALO + tile, :] = x_ref[...]
        acc = jnp.broadcast_to(b_ref[...], (tile, cb))
        for k in range(kw):
            acc = acc + pad_ref[pl.ds(HALO - (kw - 1) + k, tile), :] * w_ref[k:k + 1, :]
        y_ref[...] = acc

    return pl.pallas_call(
        body, name=name, grid=(t_rows // tile, ch // cb),
        in_specs=[pl.BlockSpec((tile, cb), lambda i, j: (i, j)),
                  pl.BlockSpec((HALO, cb), lambda i, j: (jnp.maximum(i * hb - 1, 0), j)),
                  pl.BlockSpec((kw, cb), lambda i, j: (0, j)), pl.BlockSpec((1, cb), lambda i, j: (0, j))],
        out_specs=pl.BlockSpec((tile, cb), lambda i, j: (i, j)),
        out_shape=jax.ShapeDtypeStruct((t_rows, ch), F32),
        scratch_shapes=[pltpu.VMEM((HALO + tile, cb), F32)],
        compiler_params=pltpu.CompilerParams(dimension_semantics=("parallel", "parallel")),
    )(x, x, w, b)


def _dwconv_bwd(x, dy, w, seq, name):
    t_rows, ch = x.shape
    kw = w.shape[0]
    tile = min(512, seq)
    cb = _divisor_tile(ch, 256, LANES)
    tps, hb, n_hb = seq // tile, tile // HALO, t_rows // HALO

    def body(x_ref, xh_ref, dy_ref, dyh_ref, w_ref, dx_ref, dw_ref, db_ref, xpad, dypad):
        i = pl.program_id(1)
        first, last = _seq_flags(i, tps)

        @pl.when(i == 0)
        def _():
            dw_ref[...] = jnp.zeros_like(dw_ref)
            db_ref[...] = jnp.zeros_like(db_ref)

        xpad[0:HALO, :] = jnp.where(first, 0.0, xh_ref[...])
        xpad[HALO:HALO + tile, :] = x_ref[...]
        dyv = dy_ref[...]
        dypad[0:tile, :] = dyv
        dypad[tile:tile + HALO, :] = jnp.where(last, 0.0, dyh_ref[...])
        acc = jnp.zeros((tile, cb), F32)
        for k in range(kw):
            acc = acc + dypad[pl.ds(kw - 1 - k, tile), :] * w_ref[k:k + 1, :]
            dw_ref[k:k + 1, :] += jnp.sum(dyv * xpad[pl.ds(HALO - (kw - 1) + k, tile), :], axis=0, keepdims=True)
        dx_ref[...] = acc
        db_ref[...] += jnp.sum(dyv, axis=0, keepdims=True)

    return pl.pallas_call(
        body, name=name, grid=(ch // cb, t_rows // tile),
        in_specs=[pl.BlockSpec((tile, cb), lambda j, i: (i, j)),
                  pl.BlockSpec((HALO, cb), lambda j, i: (jnp.maximum(i * hb - 1, 0), j)),
                  pl.BlockSpec((tile, cb), lambda j, i: (i, j)),
                  pl.BlockSpec((HALO, cb), lambda j, i: (jnp.minimum((i + 1) * hb, n_hb - 1), j)),
                  pl.BlockSpec((kw, cb), lambda j, i: (0, j))],
        out_specs=[pl.BlockSpec((tile, cb), lambda j, i: (i, j)), pl.BlockSpec((kw, cb), lambda j, i: (0, j)),
                   pl.BlockSpec((1, cb), lambda j, i: (0, j))],
        out_shape=[jax.ShapeDtypeStruct((t_rows, ch), F32), jax.ShapeDtypeStruct((kw, ch), F32),
                   jax.ShapeDtypeStruct((1, ch), F32)],
        scratch_shapes=[pltpu.VMEM((HALO + tile, cb), F32), pltpu.VMEM((tile + HALO, cb), F32)],
        compiler_params=pltpu.CompilerParams(dimension_semantics=("parallel", "arbitrary")),
    )(x, x, dy, dy, w)


def _ffn_mid_fwd(u, w, seq, name):
    t_rows, f2 = u.shape
    f = f2 // 2
    tile = min(256, seq)
    cb = _divisor_tile(f, 1408, LANES)
    nj, tps, hb, hl = f // cb, seq // tile, tile // SUBLANES_BF16, SUBLANES_BF16

    def body(ug_ref, uv_ref, hg_ref, hv_ref, wg_ref, wv_ref, a_ref, gpad, vpad):
        first, _ = _seq_flags(pl.program_id(0), tps)

        def conv(t_ref, h_ref, w_ref, pad):
            pad[0:hl, :] = jnp.where(first, 0.0, h_ref[...].astype(F32))
            pad[hl:hl + tile, :] = t_ref[...].astype(F32)
            z = pad[pl.ds(hl, tile), :] * w_ref[2:3, :]
            z = z + pad[pl.ds(hl - 1, tile), :] * w_ref[1:2, :]
            return z + pad[pl.ds(hl - 2, tile), :] * w_ref[0:1, :]

        zg = conv(ug_ref, hg_ref, wg_ref, gpad)
        zv = conv(uv_ref, hv_ref, wv_ref, vpad)
        a_ref[...] = (_silu(zg) * zv).astype(a_ref.dtype)

    halo_map = lambda off: (lambda i, j: (jnp.maximum(i * hb - 1, 0), j + off))
    return pl.pallas_call(
        body, name=name, grid=(t_rows // tile, nj),
        in_specs=[pl.BlockSpec((tile, cb), lambda i, j: (i, j)), pl.BlockSpec((tile, cb), lambda i, j: (i, j + nj)),
                  pl.BlockSpec((hl, cb), halo_map(0)), pl.BlockSpec((hl, cb), halo_map(nj)),
                  pl.BlockSpec((3, cb), lambda i, j: (0, j)), pl.BlockSpec((3, cb), lambda i, j: (0, j + nj))],
        out_specs=pl.BlockSpec((tile, cb), lambda i, j: (i, j)),
        out_shape=jax.ShapeDtypeStruct((t_rows, f), BF16),
        scratch_shapes=[pltpu.VMEM((hl + tile, cb), F32), pltpu.VMEM((hl + tile, cb), F32)],
        compiler_params=pltpu.CompilerParams(dimension_semantics=("parallel", "parallel")),
    )(u, u, u, u, w, w)


def _ffn_mid_bwd(u, da, w, seq, name):
    t_rows, f2 = u.shape
    f = f2 // 2
    tile = min(256, seq)
    cb = _divisor_tile(f, 1408, LANES)
    hl = SUBLANES_BF16
    nj, tps, hb, n_hb, ext = f // cb, seq // tile, tile // hl, t_rows // hl, tile + hl

    def body(ug_ref, uv_ref, pg_ref, pv_ref, ng_ref, nv_ref, da_ref, dan_ref, wg_ref, wv_ref,
             dug_ref, duv_ref, dwg_ref, dwv_ref, gpad, vpad, dzg, dzv):
        i = pl.program_id(1)
        first, last = _seq_flags(i, tps)

        @pl.when(i == 0)
        def _():
            dwg_ref[...] = jnp.zeros_like(dwg_ref)
            dwv_ref[...] = jnp.zeros_like(dwv_ref)

        def conv(t_ref, p_ref, n_ref, w_ref, pad):
            pad[0:hl, :] = jnp.where(first, 0.0, p_ref[...].astype(F32))
            pad[hl:hl + tile, :] = t_ref[...].astype(F32)
            pad[hl + tile:hl + ext, :] = jnp.where(last, 0.0, n_ref[...].astype(F32))
            z = pad[pl.ds(hl, ext), :] * w_ref[2:3, :]
            z = z + pad[pl.ds(hl - 1, ext), :] * w_ref[1:2, :]
            return z + pad[pl.ds(hl - 2, ext), :] * w_ref[0:1, :]

        zg = conv(ug_ref, pg_ref, ng_ref, wg_ref, gpad)
        zv = conv(uv_ref, pv_ref, nv_ref, wv_ref, vpad)
        da_ext = jnp.concatenate(
            [da_ref[...].astype(F32), jnp.where(last, 0.0, dan_ref[...].astype(F32))], axis=0)
        sg = _sigmoid(zg)
        dzg[...] = da_ext * zv * (sg * (1.0 + zg * (1.0 - sg)))
        dzv[...] = da_ext * (zg * sg)

        def back(dz, w_ref, pad, du_ref, dw_ref):
            du = dz[pl.ds(2, tile), :] * w_ref[0:1, :]
            du = du + dz[pl.ds(1, tile), :] * w_ref[1:2, :]
            du_ref[...] = (du + dz[pl.ds(0, tile), :] * w_ref[2:3, :]).astype(du_ref.dtype)
            dzt = dz[pl.ds(0, tile), :]
            for k in range(3):
                dw_ref[k:k + 1, :] += jnp.sum(dzt * pad[pl.ds(hl - 2 + k, tile), :], axis=0, keepdims=True)

        back(dzg, wg_ref, gpad, dug_ref, dwg_ref)
        back(dzv, wv_ref, vpad, duv_ref, dwv_ref)

    prev_map = lambda off: (lambda j, i: (jnp.maximum(i * hb - 1, 0), j + off))
    next_map = lambda off: (lambda j, i: (jnp.minimum((i + 1) * hb, n_hb - 1), j + off))
    tile_spec = lambda off: pl.BlockSpec((tile, cb), lambda j, i: (i, j + off))
    w_spec = lambda off: pl.BlockSpec((3, cb), lambda j, i: (0, j + off))
    return pl.pallas_call(
        body, name=name, grid=(nj, t_rows // tile),
        in_specs=[tile_spec(0), tile_spec(nj), pl.BlockSpec((hl, cb), prev_map(0)), pl.BlockSpec((hl, cb), prev_map(nj)),
                  pl.BlockSpec((hl, cb), next_map(0)), pl.BlockSpec((hl, cb), next_map(nj)),
                  tile_spec(0), pl.BlockSpec((hl, cb), next_map(0)), w_spec(0), w_spec(nj)],
        out_specs=[tile_spec(0), tile_spec(0), w_spec(0), w_spec(0)],
        out_shape=[jax.ShapeDtypeStruct((t_rows, f), BF16), jax.ShapeDtypeStruct((t_rows, f), BF16),
                   jax.ShapeDtypeStruct((3, f), F32), jax.ShapeDtypeStruct((3, f), F32)],
        scratch_shapes=[pltpu.VMEM((hl + ext, cb), F32), pltpu.VMEM((hl + ext, cb), F32),
                        pltpu.VMEM((ext, cb), F32), pltpu.VMEM((ext, cb), F32)],
        compiler_params=pltpu.CompilerParams(dimension_semantics=("parallel", "arbitrary")),
    )(u, u, u, u, u, u, da, da, w, w)


def _gla_chunk(q, k, v, lg, st, *, scale, chunk):
    row = lax.broadcasted_iota(jnp.int32, (chunk, chunk), 0)
    col = lax.broadcasted_iota(jnp.int32, (chunk, chunk), 1)
    causal = col <= row
    b = _hdot(causal.astype(F32), lg)
    upto_mid = lax.broadcasted_iota(jnp.int32, lg.shape, 0) <= chunk // 2
    b_mid = jnp.sum(jnp.where(upto_mid, lg, 0.0), axis=0, keepdims=True)
    b_last = jnp.sum(lg, axis=0, keepdims=True)
    qs = q * scale
    scores = _bdot_nt(qs * jnp.exp(b - b_mid), k * jnp.exp(b_mid - b))
    o = _bdot(jnp.where(causal, scores, 0.0), v)
    o = o + _bdot_nt(qs * jnp.exp(b), st)
    st_new = st * jnp.exp(b_last) + _bdot_tn(v, k * jnp.exp(b_last - b))
    return o, st_new


def _gla_specs(specs, chunk, n_chunks, reverse):
    if reverse:
        row = lambda bi, ci: bi * n_chunks + (n_chunks - 1 - ci)
    else:
        row = lambda bi, ci: bi * n_chunks + ci
    return [pl.BlockSpec((chunk, w), lambda bi, ci, cb=cb: (row(bi, ci), cb)) for _, w, cb in specs], row


def _gla_fwd(q, k, v, lg, *, heads, dk, dv, scale, chunk, seq, name):
    t_rows = q[0].shape[0]
    n_chunks = seq // chunk
    fn = functools.partial(_gla_chunk, scale=scale, chunk=chunk)

    def body(q_ref, k_ref, v_ref, lg_ref, o_ref, sts_ref, st_ref):
        @pl.when(pl.program_id(1) == 0)
        def _():
            st_ref[...] = jnp.zeros_like(st_ref)

        sts_ref[0] = st_ref[...]
        for h in range(heads):
            ks, vs = slice(h * dk, (h + 1) * dk), slice(h * dv, (h + 1) * dv)
            o, st = fn(q_ref[:, ks].astype(F32), k_ref[:, ks].astype(F32), v_ref[:, vs].astype(F32),
                       lg_ref[:, ks], st_ref[vs, :])
            o_ref[:, vs] = o
            st_ref[vs, :] = st

    in_specs, row = _gla_specs([q, k, v, lg], chunk, n_chunks, False)
    return pl.pallas_call(
        body, name=name, grid=(t_rows // seq, n_chunks), in_specs=in_specs,
        out_specs=[pl.BlockSpec((chunk, heads * dv), lambda bi, ci: (row(bi, ci), 0)),
                   pl.BlockSpec((1, heads * dv, dk), lambda bi, ci: (row(bi, ci), 0, 0))],
        out_shape=[jax.ShapeDtypeStruct((t_rows, heads * dv), F32),
                   jax.ShapeDtypeStruct((t_rows // chunk, heads * dv, dk), F32)],
        scratch_shapes=[pltpu.VMEM((heads * dv, dk), F32)],
        compiler_params=pltpu.CompilerParams(dimension_semantics=("arbitrary", "arbitrary")),
    )(q[0], k[0], v[0], lg[0])


def _gla_bwd(q, k, v, lg, states, do, *, heads, dk, dv, scale, chunk, seq, out_dtypes, name):
    t_rows = q[0].shape[0]
    n_chunks = seq // chunk
    fn = functools.partial(_gla_chunk, scale=scale, chunk=chunk)

    def body(q_ref, k_ref, v_ref, lg_ref, do_ref, sts_ref, dq_ref, dk_ref, dv_ref, dlg_ref, dst_ref):
        @pl.when(pl.program_id(1) == 0)
        def _():
            dst_ref[...] = jnp.zeros_like(dst_ref)

        for h in range(heads):
            ks, vs = slice(h * dk, (h + 1) * dk), slice(h * dv, (h + 1) * dv)
            _, vjp = jax.vjp(fn, q_ref[:, ks].astype(F32), k_ref[:, ks].astype(F32), v_ref[:, vs].astype(F32),
                             lg_ref[:, ks], sts_ref[0, vs, :])
            dq, dkk, dvv, dlg, dst = vjp((do_ref[:, vs].astype(F32), dst_ref[vs, :]))
            dq_ref[:, ks] = dq.astype(dq_ref.dtype)
            dk_ref[:, ks] = dkk.astype(dk_ref.dtype)
            dv_ref[:, vs] = dvv.astype(dv_ref.dtype)
            dlg_ref[:, ks] = dlg
            dst_ref[vs, :] = dst

    do_view = (do, heads * dv, 0)
    in_specs, row = _gla_specs([q, k, v, lg, do_view], chunk, n_chunks, True)
    in_specs.append(pl.BlockSpec((1, heads * dv, dk), lambda bi, ci: (row(bi, ci), 0, 0)))
    wide = lambda w: pl.BlockSpec((chunk, w), lambda bi, ci: (row(bi, ci), 0))
    return pl.pallas_call(
        body, name=name, grid=(t_rows // seq, n_chunks), in_specs=in_specs,
        out_specs=[wide(heads * dk), wide(heads * dk), wide(heads * dv), wide(heads * dk)],
        out_shape=[jax.ShapeDtypeStruct((t_rows, heads * dk), out_dtypes[0]),
                   jax.ShapeDtypeStruct((t_rows, heads * dk), out_dtypes[1]),
                   jax.ShapeDtypeStruct((t_rows, heads * dv), out_dtypes[2]),
                   jax.ShapeDtypeStruct((t_rows, heads * dk), F32)],
        scratch_shapes=[pltpu.VMEM((heads * dv, dk), F32)],
        compiler_params=pltpu.CompilerParams(dimension_semantics=("arbitrary", "arbitrary")),
    )(q[0], k[0], v[0], lg[0], do, states)


def _head_rms_gate(o, r, g, heads):
    d = o.shape[1] // heads
    parts = [_rms(o[:, h * d:(h + 1) * d], g) for h in range(heads)]
    return jnp.concatenate(parts, axis=1) * _silu(r)


def _gla_gate(glr, w_g2p, b_g2):
    return _log_sigmoid(_bdot(glr, w_g2p) + b_g2) * (1.0 / GLA_GATE_NORM)


def _glu(a, gate, b_in):
    d = a.shape[1]
    return (a + b_in[:, :d]) * _sigmoid(gate + b_in[:, d:])


def _ln_silu(y, g, b):
    return _silu(_layer_norm(y, g, b))


def _sgu(pre, b_in, ln_g, ln_b, w_s, b_st):
    d = pre.shape[1] // 2
    gd = d // SGU_GROUPS
    uv = _gelu(pre + b_in)
    u, v = uv[:, :d], _layer_norm(uv[:, d:], ln_g, ln_b)
    row = lax.broadcasted_iota(jnp.int32, (SGU_CHUNK, SGU_CHUNK), 0)
    col = lax.broadcasted_iota(jnp.int32, (SGU_CHUNK, SGU_CHUNK), 1)
    lane = lax.broadcasted_iota(jnp.int32, b_st.shape, 1)
    rows = []
    for c in range(pre.shape[0] // SGU_CHUNK):
        rs = slice(c * SGU_CHUNK, (c + 1) * SGU_CHUNK)
        parts = []
        for g in range(SGU_GROUPS):
            wg = jnp.where(col <= row, w_s[g], 0.0)
            bias = jnp.sum(jnp.where(lane == g, b_st, 0.0), axis=1, keepdims=True)
            parts.append(_bdot(wg, v[rs, g * gd:(g + 1) * gd]) + bias)
        rows.append(jnp.concatenate(parts, axis=1))
    s = rows[0] if len(rows) == 1 else jnp.concatenate(rows, axis=0)
    return u * s


def _hgrn_pre(q, f, table, layer):
    t = table - jnp.max(table, axis=0, keepdims=True)
    e = jnp.exp(t)
    sm = e / jnp.sum(e, axis=0, keepdims=True)
    rows = lax.broadcasted_iota(jnp.int32, table.shape, 0)
    lb = jnp.sum(jnp.where((rows >= 1) & (rows <= layer), sm, 0.0), axis=0, keepdims=True)
    sf = _sigmoid(f)
    return _silu(q), (1.0 - lb) * (1.0 - sf), jnp.log(lb + (1.0 - lb) * sf)


def _ffn_fwd(x, w, seq, sv):
    sv["h2"] = _rms_fwd(x, w["norm"], "ffn_norm")
    sv["u"] = _mm(sv["h2"], w["w_up"], out_dtype=BF16, name="ffn_up")
    sv["a"] = _ffn_mid_fwd(sv["u"], w["w_dw"], seq, "ffn_mid")
    return _mm(sv["a"], w["w_down"], add=x, name="ffn_down")


def _ffn_bwd(x, dy, w, seq, sv):
    g = {}
    da = _mm(dy, w["w_down_t"], out_dtype=BF16, name="ffn_down_dx")
    g["w_down"] = _mm_tn(sv["a"], dy, name="ffn_down_dw")
    dug, duv, dwg, dwv = _ffn_mid_bwd(sv["u"], da, w["w_dw"], seq, "ffn_mid_bwd")
    g["w_dw"] = jnp.concatenate([dwg, dwv], axis=1)
    g["w_up"] = jnp.concatenate([_mm_tn(sv["h2"], dug, name="ffn_up_dw"), _mm_tn(sv["h2"], duv, name="ffn_up_dw")], axis=1)
    dh = _mm(dug, w["w_up_t_gate"], out_dtype=F32, name="ffn_up_dx")
    dh = _mm(duv, w["w_up_t_val"], add=dh, out_dtype=BF16, name="ffn_up_dx2")
    dx, g["norm"] = _rms_bwd(x, w["norm"], dh, dy, "ffn_norm_bwd")
    return dx, g


def _gla_layer_fwd(x, h, w, seq, sv):
    d = x.shape[1]
    dkt = d // 2
    dk, dv = dkt // GLA_HEADS, d // GLA_HEADS
    proj = _mm(h, w["w_main"], out_dtype=F32, name="gla_in")
    glr = _mm(h, w["w_glr"], out_dtype=BF16, name="gla_in_g")
    (lg,), _ = _tile_call("gla_gate", lambda a, b, c: ([_gla_gate(a.astype(F32), b, c)], []), [glr],
                          [w["w_g2p"], w["b_g2"]], [(dkt, F32)], [], 512)
    q, k, v, r = (proj, dkt, 0), (proj, dkt, 1), (proj, d, 1), (proj, d, 2)
    o, states = _gla_fwd(q, k, v, (lg, dkt, 0), heads=GLA_HEADS, dk=dk, dv=dv, scale=dk ** -0.5, chunk=GLA_CHUNK,
                         seq=seq, name="gla_core")
    (o2,), _ = _tile_call("gla_post", lambda ov, rv, gv: ([_head_rms_gate(ov, rv.astype(F32), gv, GLA_HEADS)], []),
                          [o, r], [w["norm"]], [(d, BF16)], [], 256)
    sv.update(proj=proj, glr=glr, lg=lg, o=o, states=states, o2=o2)
    return _mm(o2, w["w_out"], add=x, name="mix_out")


def _gla_layer_bwd(h, dy, w, seq, sv):
    d = dy.shape[1]
    dkt = d // 2
    dk, dv = dkt // GLA_HEADS, d // GLA_HEADS
    proj, glr, lg, o = sv["proj"], sv["glr"], sv["lg"], sv["o"]
    g = {}
    do2 = _mm(dy, w["w_out_t"], out_dtype=F32, name="gla_out_dx")
    g["w_out"] = _mm_tn(sv["o2"], dy, name="mix_out_dw")

    def post_bwd(ov, rv, ctv, gv):
        _, vjp = jax.vjp(functools.partial(_head_rms_gate, heads=GLA_HEADS), ov, rv.astype(F32), gv)
        d_o, d_r, d_g = vjp(ctv.astype(F32))
        return [d_o, d_r], [d_g]

    (d_o, d_r), (g["norm"],) = _tile_call("gla_post_bwd", post_bwd, [o, (proj, d, 2), do2], [w["norm"]],
                                          [(d, F32), (d, BF16)], [w["norm"].shape], 256)
    q, k, v = (proj, dkt, 0), (proj, dkt, 1), (proj, d, 1)
    dq, dkk, dvv, dlg = _gla_bwd(q, k, v, (lg, dkt, 0), sv["states"], d_o, heads=GLA_HEADS, dk=dk, dv=dv,
                                 scale=dk ** -0.5, chunk=GLA_CHUNK, seq=seq, out_dtypes=(BF16, BF16, BF16),
                                 name="gla_core_bwd")

    def gate_bwd(glrv, ctv, wv, bv):
        _, vjp = jax.vjp(_gla_gate, glrv.astype(F32), wv, bv)
        d_glr, d_w, d_b = vjp(ctv)
        return [d_glr], [d_w, d_b]

    (dglr,), (g["w_g2p"], g["b_g2"]) = _tile_call("gla_gate_bwd", gate_bwd, [glr, dlg], [w["w_g2p"], w["b_g2"]],
                                                  [(LANES, BF16)], [w["w_g2p"].shape, w["b_g2"].shape], 512)
    dproj = jnp.concatenate([dq, dkk, dvv, d_r], axis=1)
    g["w_main"] = _mm_tn(h, dproj, name="gla_in_dw")
    g["w_glr"] = _mm_tn(h, dglr, name="gla_in_g_dw")
    dh = _mm(dproj, w["w_main_t"], out_dtype=F32, name="gla_in_dx")
    dh = _mm(dglr, w["w_glr_t"], add=dh, out_dtype=BF16, name="gla_in_g_dx")
    return dh, g


def _cv_layer_fwd(x, h, w, seq, sv):
    d = x.shape[1]
    pre = _mm(h, w["w_in"], out_dtype=BF16, name="cv_in")
    (y1,), _ = _tile_call("cv_glu", lambda a, gt, b: ([_glu(a.astype(F32), gt.astype(F32), b)], []),
                          [(pre, d, 0), (pre, d, 1)], [w["b_in"]], [(d, F32)], [], 512)
    y2 = _dwconv_fwd(y1, w["w_dw"], w["b_dw"], seq, "cv_conv")
    (y3,), _ = _tile_call("cv_ln", lambda y, a, b: ([_ln_silu(y, a, b)], []), [y2], [w["ln_g"], w["ln_b"]],
                          [(d, BF16)], [], 512)
    sv.update(pre=pre, y1=y1, y2=y2, y3=y3)
    return _mm(y3, w["w_out"], bias=w["b_out"], add=x, name="mix_out_b")


def _cv_layer_bwd(h, dy, w, seq, sv):
    d = dy.shape[1]
    pre = sv["pre"]
    g = {}
    dy3 = _mm(dy, w["w_out_t"], out_dtype=BF16, name="mix_out_dx")
    g["w_out"] = _mm_tn(sv["y3"], dy, name="mix_out_dw")
    g["b_out"] = _colsum(dy, "bias_out_dw")

    def ln_bwd(yv, ctv, av, bv):
        _, vjp = jax.vjp(_ln_silu, yv, av, bv)
        d_y, d_a, d_b = vjp(ctv.astype(F32))
        return [d_y], [d_a, d_b]

    (dy2,), (g["ln_g"], g["ln_b"]) = _tile_call("cv_ln_bwd", ln_bwd, [sv["y2"], dy3], [w["ln_g"], w["ln_b"]],
                                                [(d, F32)], [w["ln_g"].shape, w["ln_b"].shape], 512)
    dy1, g["w_dw"], g["b_dw"] = _dwconv_bwd(sv["y1"], dy2, w["w_dw"], seq, "cv_conv_bwd")

    def glu_bwd(av, gv, ctv, bv):
        _, vjp = jax.vjp(_glu, av.astype(F32), gv.astype(F32), bv)
        d_a, d_g, d_b = vjp(ctv)
        return [jnp.concatenate([d_a, d_g], axis=1)], [d_b]

    (dpre,), (g["b_in"],) = _tile_call("cv_glu_bwd", glu_bwd, [(pre, d, 0), (pre, d, 1), dy1], [w["b_in"]],
                                       [(2 * d, BF16)], [w["b_in"].shape], 512)
    g["w_in"] = _mm_tn(h, dpre, name="in2_dw")
    dh = _mm(dpre, w["w_in_t"], out_dtype=BF16, name="in2_dx")
    return dh, g


def _sg_layer_fwd(x, h, w, seq, sv):
    d = x.shape[1]
    pre = _mm(h, w["w_in"], out_dtype=BF16, name="sg_in")
    pars = [w["b_in"], w["ln_g"], w["ln_b"], w["w_s"], w["b_st"]]
    (p,), _ = _tile_call("sg_gate", lambda pv, *ps: ([_sgu(pv.astype(F32), *ps)], []), [pre], pars, [(d, BF16)], [],
                         SGU_CHUNK)
    sv.update(pre=pre, p=p)
    return _mm(p, w["w_out"], bias=w["b_out"], add=x, name="mix_out_b")


def _sg_layer_bwd(h, dy, w, seq, sv):
    d = dy.shape[1]
    g = {}
    dp = _mm(dy, w["w_out_t"], out_dtype=BF16, name="mix_out_dx")
    g["w_out"] = _mm_tn(sv["p"], dy, name="mix_out_dw")
    g["b_out"] = _colsum(dy, "bias_out_dw")
    pars = [w["b_in"], w["ln_g"], w["ln_b"], w["w_s"], w["b_st"]]

    def sgu_bwd(pv, ctv, *ps):
        _, vjp = jax.vjp(_sgu, pv.astype(F32), *ps)
        grads = vjp(ctv.astype(F32))
        return [grads[0]], list(grads[1:])

    (dpre,), (g["b_in"], g["ln_g"], g["ln_b"], g["w_s"], g["b_st"]) = _tile_call(
        "sg_gate_bwd", sgu_bwd, [sv["pre"], dp], pars, [(2 * d, BF16)], [p.shape for p in pars], SGU_CHUNK)
    g["w_in"] = _mm_tn(h, dpre, name="in2_dw")
    dh = _mm(dpre, w["w_in_t"], out_dtype=BF16, name="in2_dx")
    return dh, g


def _hg_layer_fwd(x, h, w, seq, sv, layer):
    d = x.shape[1]
    heads = d // HGRN_EXPAND
    proj = _mm(h, w["w_in"], out_dtype=BF16, name="hg_in")
    pre = functools.partial(_hgrn_pre, layer=layer)
    (qs, kk, lg), _ = _tile_call("hg_pre", lambda qv, fv, tb: (list(pre(qv.astype(F32), fv.astype(F32), tb)), []),
                                 [(proj, d, 0), (proj, d, 1)], [w["lb_table"]], [(d, BF16), (d, F32), (d, F32)], [], 256)
    o, states = _gla_fwd((qs, d, 0), (kk, d, 0), (proj, d, 2), (lg, d, 0), heads=heads, dk=HGRN_EXPAND,
                         dv=HGRN_EXPAND, scale=1.0, chunk=HGRN_CHUNK, seq=seq, name="hg_core")
    (o2,), _ = _tile_call("hg_post", lambda ov, gv, nv: ([_head_rms_gate(ov, gv.astype(F32), nv, heads)], []),
                          [o, (proj, d, 3)], [w["norm"]], [(d, BF16)], [], 256)
    sv.update(proj=proj, qs=qs, kk=kk, lg=lg, o=o, states=states, o2=o2)
    return _mm(o2, w["w_out"], add=x, name="mix_out")


def _hg_layer_bwd(h, dy, w, seq, sv, layer):
    d = dy.shape[1]
    heads = d // HGRN_EXPAND
    proj = sv["proj"]
    g = {}
    do2 = _mm(dy, w["w_out_t"], out_dtype=BF16, name="mix_out_dx")
    g["w_out"] = _mm_tn(sv["o2"], dy, name="mix_out_dw")

    def post_bwd(ov, gv, ctv, nv):
        _, vjp = jax.vjp(functools.partial(_head_rms_gate, heads=heads), ov, gv.astype(F32), nv)
        d_o, d_g, d_n = vjp(ctv.astype(F32))
        return [d_o, d_g], [d_n]

    (d_o, d_gate), (g["norm"],) = _tile_call("hg_post_bwd", post_bwd, [sv["o"], (proj, d, 3), do2], [w["norm"]],
                                             [(d, F32), (d, BF16)], [w["norm"].shape], 256)
    dqs, dkk, di, dlg = _gla_bwd((sv["qs"], d, 0), (sv["kk"], d, 0), (proj, d, 2), (sv["lg"], d, 0), sv["states"], d_o,
                                 heads=heads, dk=HGRN_EXPAND, dv=HGRN_EXPAND, scale=1.0, chunk=HGRN_CHUNK, seq=seq,
                                 out_dtypes=(F32, F32, BF16), name="hg_core_bwd")

    def pre_bwd(qv, fv, c1, c2, c3, tb):
        _, vjp = jax.vjp(functools.partial(_hgrn_pre, layer=layer), qv.astype(F32), fv.astype(F32), tb)
        d_q, d_f, d_t = vjp((c1, c2, c3))
        return [jnp.concatenate([d_q, d_f], axis=1)], [d_t]

    (dqf,), (g["lb_table"],) = _tile_call("hg_pre_bwd", pre_bwd, [(proj, d, 0), (proj, d, 1), dqs, dkk, dlg],
                                          [w["lb_table"]], [(2 * d, BF16)], [w["lb_table"].shape], 256)
    dproj = jnp.concatenate([dqf, di, d_gate], axis=1)
    g["w_in"] = _mm_tn(h, dproj, name="hg_in_dw")
    dh = _mm(dproj, w["w_in_t"], out_dtype=BF16, name="hg_in_dx")
    return dh, g


_MIXERS = ("gla", "cv", "sg", "hg")


def _local_step(x, target, w, seq):
    depth = w["norm_mix"].shape[0]
    d = x.shape[1]
    saved = []
    for layer in range(depth):
        mixer = _MIXERS[layer % 4]
        sv = {"x_in": x}
        wm = dict(w[mixer])
        sv["h"] = _rms_fwd(x, w["norm_mix"][layer:layer + 1], "mix_norm")
        if mixer == "gla":
            x = _gla_layer_fwd(x, sv["h"], wm, seq, sv)
        elif mixer == "cv":
            x = _cv_layer_fwd(x, sv["h"], wm, seq, sv)
        elif mixer == "sg":
            x = _sg_layer_fwd(x, sv["h"], wm, seq, sv)
        else:
            x = _hg_layer_fwd(x, sv["h"], wm, seq, sv, layer)
        sv["x_mid"] = x
        sv["ffn"] = {}
        x = _ffn_fwd(x, dict(w["ffn"][layer], norm=w["norm_ffn"][layer:layer + 1]), seq, sv["ffn"])
        saved.append(sv)

    def head(xv, tv, gv):
        y, vjp = jax.vjp(_rms, xv, gv)
        err = y - tv
        dx, dg = vjp(err * (1.0 / d))
        part = 0.5 * jnp.sum(jnp.mean(err * err, axis=-1, keepdims=True), axis=0, keepdims=True)
        return [dx], [jnp.broadcast_to(part, (1, LANES)), dg]

    (dx,), (loss, g_final) = _tile_call("loss_head", head, [x, target], [w["norm_final"]], [(d, F32)],
                                        [(1, LANES), (1, d)], 512)
    grads = {"norm_final": g_final, "norm_mix": [None] * depth, "norm_ffn": [None] * depth, "ffn": [None] * depth}
    for layer in reversed(range(depth)):
        mixer = _MIXERS[layer % 4]
        sv = saved[layer]
        dx, gf = _ffn_bwd(sv["x_mid"], dx, dict(w["ffn"][layer], norm=w["norm_ffn"][layer:layer + 1]), seq, sv["ffn"])
        grads["norm_ffn"][layer] = gf.pop("norm")
        grads["ffn"][layer] = gf
        wm = dict(w[mixer])
        if mixer == "gla":
            dh, gm = _gla_layer_bwd(sv["h"], dx, wm, seq, sv)
        elif mixer == "cv":
            dh, gm = _cv_layer_bwd(sv["h"], dx, wm, seq, sv)
        elif mixer == "sg":
            dh, gm = _sg_layer_bwd(sv["h"], dx, wm, seq, sv)
        else:
            dh, gm = _hg_layer_bwd(sv["h"], dx, wm, seq, sv, layer)
        grads[mixer] = gm
        dx, grads["norm_mix"][layer] = _rms_bwd(sv["x_in"], w["norm_mix"][layer:layer + 1], dh, dx, "mix_norm_bwd")
    return loss, dx, grads


def _prep_weights(p):
    bf = lambda a: a.astype(BF16)
    row = lambda a: a.reshape(1, -1).astype(F32)
    gla_in = bf(p["gla_w_in"][0])
    n_main = gla_in.shape[1] - GLA_RANK
    w_main = gla_in[:, :n_main]
    w_glr = jnp.pad(gla_in[:, n_main:], ((0, 0), (0, LANES - GLA_RANK)))
    w = {"norm_mix": p["norm_mix"].astype(F32), "norm_ffn": p["norm_ffn"].astype(F32), "norm_final": row(p["norm_final"])}
    w["gla"] = dict(
        w_main=w_main, w_glr=w_glr, w_main_t=w_main.T, w_glr_t=w_glr.T,
        w_g2p=jnp.pad(p["gla_w_g2"][0].astype(F32), ((0, LANES - GLA_RANK), (0, 0))), b_g2=row(p["gla_b_g2"]),
        norm=row(p["gla_norm"]), w_out=bf(p["gla_w_out"][0]), w_out_t=bf(p["gla_w_out"][0]).T)
    w["cv"] = dict(
        w_in=bf(p["cv_w_in"][0]), w_in_t=bf(p["cv_w_in"][0]).T, b_in=row(p["cv_b_in"]), w_dw=p["cv_w_dw"][0].astype(F32),
        b_dw=row(p["cv_b_dw"]), ln_g=row(p["cv_ln_g"]), ln_b=row(p["cv_ln_b"]), w_out=bf(p["cv_w_out"][0]),
        w_out_t=bf(p["cv_w_out"][0]).T, b_out=row(p["cv_b_out"]))
    b_st = jnp.pad(p["sg_b_s"][0].astype(F32).T, ((0, 0), (0, LANES - SGU_GROUPS)))
    w["sg"] = dict(
        w_in=bf(p["sg_w_in"][0]), w_in_t=bf(p["sg_w_in"][0]).T, b_in=row(p["sg_b_in"]), ln_g=row(p["sg_ln_g"]),
        ln_b=row(p["sg_ln_b"]), w_s=p["sg_w_s"][0].astype(F32), b_st=b_st, w_out=bf(p["sg_w_out"][0]),
        w_out_t=bf(p["sg_w_out"][0]).T, b_out=row(p["sg_b_out"]))
    w["hg"] = dict(
        w_in=bf(p["hg_w_in"][0]), w_in_t=bf(p["hg_w_in"][0]).T, lb_table=p["hg_lb_table"].astype(F32),
        norm=row(p["hg_norm"]), w_out=bf(p["hg_w_out"][0]), w_out_t=bf(p["hg_w_out"][0]).T)
    w["ffn"] = []
    for layer in range(p["ffn_w_up"].shape[0]):
        up, down = bf(p["ffn_w_up"][layer]), bf(p["ffn_w_down"][layer])
        f = down.shape[0]
        up_t = up.T
        w["ffn"].append(dict(w_up=up, w_up_t_gate=up_t[:f], w_up_t_val=up_t[f:], w_dw=p["ffn_w_dw"][layer].astype(F32),
                             w_down=down, w_down_t=down.T))
    return w


def _natural_grads(g):
    gla, cv, sg, hg = g["gla"], g["cv"], g["sg"], g["hg"]
    out = {
        "norm_mix": jnp.concatenate(g["norm_mix"], axis=0), "norm_ffn": jnp.concatenate(g["norm_ffn"], axis=0),
        "norm_final": g["norm_final"][0],
        "gla_w_in": jnp.concatenate([gla["w_main"], gla["w_glr"][:, :GLA_RANK]], axis=1)[None],
        "gla_w_g2": gla["w_g2p"][:GLA_RANK][None], "gla_b_g2": gla["b_g2"], "gla_norm": gla["norm"],
        "gla_w_out": gla["w_out"][None],
        "cv_w_in": cv["w_in"][None], "cv_b_in": cv["b_in"], "cv_w_dw": cv["w_dw"][None], "cv_b_dw": cv["b_dw"],
        "cv_ln_g": cv["ln_g"], "cv_ln_b": cv["ln_b"], "cv_w_out": cv["w_out"][None], "cv_b_out": cv["b_out"],
        "sg_w_in": sg["w_in"][None], "sg_b_in": sg["b_in"], "sg_ln_g": sg["ln_g"], "sg_ln_b": sg["ln_b"],
        "sg_w_s": sg["w_s"][None], "sg_b_s": sg["b_st"][:, :SGU_GROUPS].T[None], "sg_w_out": sg["w_out"][None],
        "sg_b_out": sg["b_out"],
        "hg_w_in": hg["w_in"][None], "hg_lb_table": hg["lb_table"], "hg_norm": hg["norm"], "hg_w_out": hg["w_out"][None],
        "ffn_w_up": jnp.stack([f["w_up"] for f in g["ffn"]]), "ffn_w_dw": jnp.stack([f["w_dw"] for f in g["ffn"]]),
        "ffn_w_down": jnp.stack([f["w_down"] for f in g["ffn"]]),
    }
    return out


def _all_gather(x, *, name, in_vmem):
    m_per, n = x.shape

    def body(x_ref, out_ref, send_sems, recv_sems, local_sem):
        mx, my, mc = lax.axis_index("x"), lax.axis_index("y"), lax.axis_index("c")
        me, sibling = (mx, my, mc), (mx, my, 1 - mc)
        chips = [(1 - mx, my), (mx, 1 - my), (1 - mx, 1 - my)]

        def rows(px, py, pc):
            return out_ref.at[pl.ds((4 * px + 2 * py + pc) * m_per, m_per), :]

        def copy(k, block, to, src=None):
            return pltpu.make_async_remote_copy(
                src_ref=rows(*block) if src is None else src, dst_ref=rows(*block), send_sem=send_sems.at[k],
                recv_sem=recv_sems.at[k], device_id=to, device_id_type=MESH)

        mine = pltpu.make_async_copy(x_ref, rows(*me), local_sem)
        mine.start()
        first = [copy(0, me, sibling, src=x_ref)]
        first += [copy(1 + j, me, (*chip, mc), src=x_ref) for j, chip in enumerate(chips)]
        for cp in first:
            cp.start()
        passed = [copy(4 + j, (*chip, mc), sibling) for j, chip in enumerate(chips)]
        for j, chip in enumerate(chips):
            copy(1 + j, (*chip, mc), me).wait_recv()
            passed[j].start()
        copy(0, sibling, me).wait_recv()
        for j, chip in enumerate(chips):
            copy(4 + j, (*chip, 1 - mc), me).wait_recv()
        for cp in first + passed:
            cp.wait_send()
        mine.wait()

    space = pltpu.VMEM if in_vmem else pl.ANY
    return pl.pallas_call(
        body, name=name, out_shape=jax.ShapeDtypeStruct((N_DEV * m_per, n), x.dtype),
        in_specs=[pl.BlockSpec(memory_space=space)], out_specs=pl.BlockSpec(memory_space=space),
        scratch_shapes=[pltpu.SemaphoreType.DMA((7,)), pltpu.SemaphoreType.DMA((7,)), pltpu.SemaphoreType.DMA],
    )(x)


def _all_to_all(x, *, name):
    r8, n = x.shape
    m_per = r8 // N_DEV

    def body(x_ref, out_ref, send_sems, recv_sems, local_sem):
        mx, my, mc = lax.axis_index("x"), lax.axis_index("y"), lax.axis_index("c")
        me = 4 * mx + 2 * my + mc

        def slot(ref, idx):
            return ref.at[pl.ds(idx * m_per, m_per), :]

        mine = pltpu.make_async_copy(slot(x_ref, me), slot(out_ref, me), local_sem)
        mine.start()
        copies = []
        for k in range(1, N_DEV):
            px = 1 - mx if k & 4 else mx
            py = 1 - my if k & 2 else my
            pc = 1 - mc if k & 1 else mc
            peer = 4 * px + 2 * py + pc
            send = pltpu.make_async_remote_copy(
                src_ref=slot(x_ref, peer), dst_ref=slot(out_ref, me), send_sem=send_sems.at[k - 1],
                recv_sem=recv_sems.at[k - 1], device_id=(px, py, pc), device_id_type=MESH)
            recv = pltpu.make_async_remote_copy(
                src_ref=slot(x_ref, me), dst_ref=slot(out_ref, peer), send_sem=send_sems.at[k - 1],
                recv_sem=recv_sems.at[k - 1], device_id=(px, py, pc), device_id_type=MESH)
            send.start()
            copies.append((send, recv))
        for send, recv in copies:
            recv.wait_recv()
        for send, recv in copies:
            send.wait_send()
        mine.wait()

    return pl.pallas_call(
        body, name=name, out_shape=jax.ShapeDtypeStruct((r8, n), x.dtype),
        in_specs=[pl.BlockSpec(memory_space=pl.ANY)], out_specs=pl.BlockSpec(memory_space=pl.ANY),
        scratch_shapes=[pltpu.SemaphoreType.DMA((7,)), pltpu.SemaphoreType.DMA((7,)), pltpu.SemaphoreType.DMA],
    )(x)


def _adamw(slots, w, m, v, *, name):
    r, n = w.shape
    tr = _divisor_tile(r, 1024, 8)
    nb = r // tr
    c1, c2 = 1.0 - ADAM_B1 ** ADAM_STEP, 1.0 - ADAM_B2 ** ADAM_STEP

    def body(*refs):
        w_ref, m_ref, v_ref = refs[N_DEV:N_DEV + 3]
        g_out, d_out, m_out, v_out = refs[N_DEV + 3:]
        g = refs[0][...]
        for p in range(1, N_DEV):
            g = g + refs[p][...]
        m_new = ADAM_B1 * m_ref[...] + (1.0 - ADAM_B1) * g
        v_new = ADAM_B2 * v_ref[...] + (1.0 - ADAM_B2) * (g * g)
        g_out[...] = g
        m_out[...] = m_new
        v_out[...] = v_new
        d_out[...] = -ADAM_LR * ((m_new / c1) / (jnp.sqrt(v_new / c2) + ADAM_EPS) + ADAM_WD * w_ref[...])

    blk = pl.BlockSpec((tr, n), lambda i: (i, 0))
    in_specs = [pl.BlockSpec((tr, n), lambda i, p=p: (p * nb + i, 0)) for p in range(N_DEV)] + [blk] * 3
    return pl.pallas_call(
        body, name=name, grid=(nb,), in_specs=in_specs, out_specs=[blk] * 4,
        out_shape=[jax.ShapeDtypeStruct((r, n), F32)] * 4,
        compiler_params=pltpu.CompilerParams(dimension_semantics=("parallel",)),
    )(*([slots] * N_DEV), w, m, v)


def _layout(shapes, row_align, total_align):
    lay, off = {}, 0
    for name, shape in shapes.items():
        size = int(np.prod(shape))
        rows = -(-size // LANES)
        rows = -(-rows // row_align) * row_align
        lay[name] = (off, rows, size, tuple(shape))
        off += rows
    return lay, -(-off // total_align) * total_align


def _pack(arrs, lay, total, dtype):
    parts = []
    for name, (off, rows, size, shape) in lay.items():
        flat = arrs[name].astype(dtype).reshape(-1)
        parts.append(jnp.pad(flat, (0, rows * LANES - size)).reshape(rows, LANES))
    used = sum(v[1] for v in lay.values())
    if total > used:
        parts.append(jnp.zeros((total - used, LANES), dtype))
    return jnp.concatenate(parts, axis=0)


def _unpack(buf, lay, lead=()):
    out = {}
    nl = len(lead)
    for name, (off, rows, size, shape) in lay.items():
        part = lax.slice_in_dim(buf, off, off + rows, axis=nl).reshape(*lead, rows * LANES)
        out[name] = lax.slice_in_dim(part, 0, size, axis=nl).reshape(*lead, *shape)
    return out


_SHARD_AXIS = {
    "norm_mix": None, "norm_ffn": None, "norm_final": None, "gla_w_in": 2, "gla_w_g2": 2, "gla_b_g2": None,
    "gla_norm": None, "gla_w_out": 1, "cv_w_in": 2, "cv_b_in": 1, "cv_w_dw": 2, "cv_b_dw": 1, "cv_ln_g": 1,
    "cv_ln_b": 1, "cv_w_out": 1, "cv_b_out": 1, "sg_w_in": 2, "sg_b_in": 1, "sg_ln_g": 1, "sg_ln_b": 1, "sg_w_s": None,
    "sg_b_s": None, "sg_w_out": 1, "sg_b_out": 1, "hg_w_in": 2, "hg_lb_table": None, "hg_norm": None, "hg_w_out": 1,
    "ffn_w_up": 2, "ffn_w_dw": 2, "ffn_w_down": 1,
}
_MATMUL_WEIGHTS = ("gla_w_in", "gla_w_out", "cv_w_in", "cv_w_out", "sg_w_in", "sg_w_out", "hg_w_in", "hg_w_out",
                   "ffn_w_up", "ffn_w_down")
_NAMES = tuple(_SHARD_AXIS)


def kernel(x, norm_mix, norm_ffn, norm_final, gla_w_in, gla_w_g2, gla_b_g2, gla_norm, gla_w_out, cv_w_in, cv_b_in, cv_w_dw, cv_b_dw, cv_ln_g, cv_ln_b, cv_w_out, cv_b_out, sg_w_in, sg_b_in, sg_ln_g, sg_ln_b, sg_w_s, sg_b_s, sg_w_out, sg_b_out, hg_w_in, hg_lb_table, hg_norm, hg_w_out, ffn_w_up, ffn_w_dw, ffn_w_down, loss_target, m_norm_mix, m_norm_ffn, m_norm_final, m_gla_w_in, m_gla_w_g2, m_gla_b_g2, m_gla_norm, m_gla_w_out, m_cv_w_in, m_cv_b_in, m_cv_w_dw, m_cv_b_dw, m_cv_ln_g, m_cv_ln_b, m_cv_w_out, m_cv_b_out, m_sg_w_in, m_sg_b_in, m_sg_ln_g, m_sg_ln_b, m_sg_w_s, m_sg_b_s, m_sg_w_out, m_sg_b_out, m_hg_w_in, m_hg_lb_table, m_hg_norm, m_hg_w_out, m_ffn_w_up, m_ffn_w_dw, m_ffn_w_down, v_norm_mix, v_norm_ffn, v_norm_final, v_gla_w_in, v_gla_w_g2, v_gla_b_g2, v_gla_norm, v_gla_w_out, v_cv_w_in, v_cv_b_in, v_cv_w_dw, v_cv_b_dw, v_cv_ln_g, v_cv_ln_b, v_cv_w_out, v_cv_b_out, v_sg_w_in, v_sg_b_in, v_sg_ln_g, v_sg_ln_b, v_sg_w_s, v_sg_b_s, v_sg_w_out, v_sg_b_out, v_hg_w_in, v_hg_lb_table, v_hg_norm, v_hg_w_out, v_ffn_w_up, v_ffn_w_dw, v_ffn_w_down):
    local = dict(locals())
    wts = {n: local[n] for n in _NAMES}
    mom = {n: local["m_" + n] for n in _NAMES}
    var = {n: local["v_" + n] for n in _NAMES}
    sharded = [n for n in _NAMES if _SHARD_AXIS[n] is not None]
    repl = [n for n in _NAMES if _SHARD_AXIS[n] is None]
    small = [n for n in sharded if n not in _MATMUL_WEIGHTS]

    lay_big, r_big = _layout({n: wts[n].shape for n in _MATMUL_WEIGHTS}, SUBLANES_BF16, SUBLANES_BF16)
    lay_small, r_small = _layout({n: wts[n].shape for n in small}, 8, 8)
    got_big = _all_gather(_pack(wts, lay_big, r_big, BF16), name="gather_matmul_weights", in_vmem=False)
    got_small = _all_gather(_pack(wts, lay_small, r_small, F32), name="gather_small_weights", in_vmem=True)
    parts = _unpack(got_big.reshape(N_DEV, r_big, LANES), lay_big, (N_DEV,))
    parts.update(_unpack(got_small.reshape(N_DEV, r_small, LANES), lay_small, (N_DEV,)))
    full = {n: wts[n] for n in repl}
    for n in sharded:
        full[n] = jnp.concatenate([parts[n][p] for p in range(N_DEV)], axis=_SHARD_AXIS[n])

    bsz, seq, d = x.shape
    loss, dx, grads = _local_step(x.reshape(bsz * seq, d), loss_target.reshape(bsz * seq, d), _prep_weights(full), seq)
    grads = _natural_grads(grads)
    loss = lax.psum(loss[0, 0], ("x", "y", "c"))

    lay_sh, r_sh = _layout({n: wts[n].shape for n in sharded}, 8, 1024)
    lay_re, r_re = _layout({n: wts[n].shape for n in repl}, 8, 8)
    slots = []
    for p in range(N_DEV):
        piece = {n: jnp.split(grads[n].reshape(full[n].shape), N_DEV, axis=_SHARD_AXIS[n])[p] for n in sharded}
        slots.append(_pack(piece, lay_sh, r_sh, F32))
    got_sh = _all_to_all(jnp.concatenate(slots, axis=0), name="scatter_weight_grads")
    g_re = {n: grads[n].reshape(wts[n].shape) for n in repl}
    got_re = _all_gather(_pack(g_re, lay_re, r_re, F32), name="gather_replicated_grads", in_vmem=True)

    outs_sh = _adamw(got_sh, _pack(wts, lay_sh, r_sh, F32), _pack(mom, lay_sh, r_sh, F32), _pack(var, lay_sh, r_sh, F32),
                     name="adamw_shards")
    outs_re = _adamw(got_re, _pack(wts, lay_re, r_re, F32), _pack(mom, lay_re, r_re, F32), _pack(var, lay_re, r_re, F32),
                     name="adamw_replicated")
    res = []
    for o_sh, o_re in zip(outs_sh, outs_re):
        un = _unpack(o_sh, lay_sh)
        un.update(_unpack(o_re, lay_re))
        res.append(un)
    out = [loss, dx.reshape(bsz, seq, d)]
    for kind in res:
        out += [kind[n] for n in _NAMES]
    return tuple(out)
```

```python
import functools
import math

import jax
import jax.numpy as jnp
import numpy as np
from jax import lax
from jax.experimental import pallas as pl
from jax.experimental.pallas import tpu as pltpu

F32 = jnp.float32
BF16 = jnp.bfloat16
EPS = 1e-6
N_DEV = 8
LANES = 128
SUBLANES_BF16 = 16
HALO = 32
GLA_HEADS, GLA_RANK, GLA_GATE_NORM, GLA_CHUNK = 4, 16, 16.0, 64
SGU_CHUNK, SGU_GROUPS = 128, 8
HGRN_EXPAND, HGRN_CHUNK = 128, 64
CONV_WIDTH, FFN_CONV_WIDTH = 31, 3
ADAM_LR, ADAM_B1, ADAM_B2, ADAM_EPS, ADAM_WD, ADAM_STEP = 0.001, 0.9, 0.999, 1e-08, 0.01, 10
MESH = pl.DeviceIdType.MESH


def _sigmoid(x):
    return 0.5 * (jnp.tanh(0.5 * x) + 1.0)


def _silu(x):
    return x * _sigmoid(x)


def _log_sigmoid(x):
    return jnp.minimum(x, 0.0) - jnp.log(1.0 + jnp.exp(-jnp.abs(x)))


def _gelu(x):
    return 0.5 * x * (1.0 + jnp.tanh(math.sqrt(2.0 / math.pi) * (x + 0.044715 * (x * x * x))))


def _rms(x, g):
    return x * lax.rsqrt(jnp.mean(x * x, axis=-1, keepdims=True) + EPS) * g


def _layer_norm(x, g, b):
    xc = x - jnp.mean(x, axis=-1, keepdims=True)
    return xc * lax.rsqrt(jnp.mean(xc * xc, axis=-1, keepdims=True) + EPS) * g + b


def _dot_raw(a, b, dims):
    return lax.dot_general(a.astype(BF16), b.astype(BF16), (dims, ((), ())), preferred_element_type=F32)


@jax.custom_vjp
def _bdot(a, b):
    return _dot_raw(a, b, ((1,), (0,)))


@jax.custom_vjp
def _bdot_nt(a, b):
    return _dot_raw(a, b, ((1,), (1,)))


@jax.custom_vjp
def _bdot_tn(a, b):
    return _dot_raw(a, b, ((0,), (0,)))


_bdot.defvjp(lambda a, b: (_bdot(a, b), (a, b)), lambda r, g: (_bdot_nt(g, r[1]), _bdot_tn(r[0], g)))
_bdot_nt.defvjp(lambda a, b: (_bdot_nt(a, b), (a, b)), lambda r, g: (_bdot(g, r[1]), _bdot_tn(g, r[0])))
_bdot_tn.defvjp(lambda a, b: (_bdot_tn(a, b), (a, b)), lambda r, g: (_bdot_nt(r[1], g), _bdot(r[0], g)))


def _hdot(a, b):
    return jnp.dot(a, b, precision=lax.Precision.HIGHEST, preferred_element_type=F32)


def _divisor_tile(n, cap, unit):
    if n <= cap:
        return n
    best = None
    for t in range(unit, cap + 1, unit):
        if n % t == 0:
            best = t
    assert best is not None, (n, cap, unit)
    return best


def _const_map(nd):
    return lambda *_: (0,) * nd


def _dep_specs(deps, grid_rank):
    return [pl.BlockSpec(d.shape, (lambda *_, nd=d.ndim: (0,) * nd)) for d in deps]


def _mm(a, b, *, add=None, bias=None, out_dtype=F32, name, deps=()):
    m, k = a.shape
    k2, n = b.shape
    assert k == k2
    tn = _divisor_tile(n, max(LANES, min(1408, (6 << 20) // (2 * k) // LANES * LANES)), LANES)
    tm = _divisor_tile(m, max(256, min(1024, (4 << 20) // (a.dtype.itemsize * k) // 256 * 256)), 8)
    has_bias, has_add = bias is not None, add is not None

    def body(*refs):
        a_ref, b_ref = refs[0], refs[1]
        o_ref = refs[-1]
        acc = jnp.dot(a_ref[...].astype(BF16), b_ref[...], preferred_element_type=F32)
        pos = 2
        if has_bias:
            acc = acc + refs[pos][...]
            pos += 1
        if has_add:
            acc = acc + refs[pos][...].astype(F32)
        o_ref[...] = acc.astype(o_ref.dtype)

    in_specs = [pl.BlockSpec((tm, k), lambda i, j: (i, 0)), pl.BlockSpec((k, tn), lambda i, j: (0, j))]
    args = [a, b]
    if has_bias:
        in_specs.append(pl.BlockSpec((1, tn), lambda i, j: (0, j)))
        args.append(bias)
    if has_add:
        in_specs.append(pl.BlockSpec((tm, tn), lambda i, j: (i, j)))
        args.append(add)
    in_specs += _dep_specs(deps, 2)
    args += list(deps)
    return pl.pallas_call(
        body, name=name, grid=(m // tm, n // tn), in_specs=in_specs,
        out_specs=pl.BlockSpec((tm, tn), lambda i, j: (i, j)),
        out_shape=jax.ShapeDtypeStruct((m, n), out_dtype),
        compiler_params=pltpu.CompilerParams(dimension_semantics=("parallel", "parallel")),
    )(*args)


def _mm_tn(a, g, *, name):
    m, k = a.shape
    m2, n = g.shape
    assert m == m2
    tk = _divisor_tile(k, 1408, LANES)
    tn = _divisor_tile(n, 1408, LANES)
    tm = _divisor_tile(m, 1024, 8)

    def body(a_ref, g_ref, o_ref):
        @pl.when(pl.program_id(2) == 0)
        def _():
            o_ref[...] = jnp.zeros_like(o_ref)

        o_ref[...] += _dot_raw(a_ref[...], g_ref[...], ((0,), (0,)))

    return pl.pallas_call(
        body, name=name, grid=(k // tk, n // tn, m // tm),
        in_specs=[pl.BlockSpec((tm, tk), lambda i, j, t: (t, i)), pl.BlockSpec((tm, tn), lambda i, j, t: (t, j))],
        out_specs=pl.BlockSpec((tk, tn), lambda i, j, t: (i, j)),
        out_shape=jax.ShapeDtypeStruct((k, n), F32),
        compiler_params=pltpu.CompilerParams(dimension_semantics=("parallel", "parallel", "arbitrary")),
    )(a, g)


def _tile_call(name, fn, tiled, params, out_tiled, out_acc, tile, deps=()):
    tiled = [t if isinstance(t, tuple) else (t, t.shape[1], 0) for t in tiled]
    t_rows = tiled[0][0].shape[0]
    tile = min(tile, t_rows)
    assert t_rows % tile == 0
    n_t, n_p, n_o, n_d = len(tiled), len(params), len(out_tiled), len(deps)

    def body(*refs):
        vals = [r[...] for r in refs[: n_t + n_p]]
        refs = refs[: n_t + n_p] + refs[n_t + n_p + n_d:]
        touts, aouts = fn(*vals)
        for r, v in zip(refs[n_t + n_p: n_t + n_p + n_o], touts):
            r[...] = v.astype(r.dtype)
        acc_refs = refs[n_t + n_p + n_o:]
        if acc_refs:
            @pl.when(pl.program_id(0) == 0)
            def _():
                for r in acc_refs:
                    r[...] = jnp.zeros_like(r)

            for r, v in zip(acc_refs, aouts):
                r[...] += v

    in_specs = [pl.BlockSpec((tile, w), lambda i, cb=cb: (i, cb)) for _, w, cb in tiled]
    in_specs += [pl.BlockSpec(p.shape, _const_map(p.ndim)) for p in params]
    in_specs += _dep_specs(deps, 1)
    out_specs = [pl.BlockSpec((tile, w), lambda i: (i, 0)) for w, _ in out_tiled]
    out_specs += [pl.BlockSpec(s, _const_map(len(s))) for s in out_acc]
    out_shape = [jax.ShapeDtypeStruct((t_rows, w), dt) for w, dt in out_tiled]
    out_shape += [jax.ShapeDtypeStruct(s, F32) for s in out_acc]
    res = pl.pallas_call(
        body, name=name, grid=(t_rows // tile,), in_specs=in_specs, out_specs=out_specs, out_shape=out_shape,
        compiler_params=pltpu.CompilerParams(dimension_semantics=("arbitrary" if out_acc else "parallel",)),
    )(*[t[0] for t in tiled], *params, *deps)
    return res[:n_o], res[n_o:]


def _rms_fwd(x, g, name, deps=()):
    (h,), _ = _tile_call(name, lambda xv, gv: ([_rms(xv, gv)], []), [x], [g], [(x.shape[1], BF16)], [], 512, deps)
    return h


def _rms_bwd(x, g, dh, dres, name):
    def fn(xv, dhv, drv, gv):
        _, vjp = jax.vjp(_rms, xv, gv)
        dx, dg = vjp(dhv.astype(F32))
        return [drv + dx], [dg]

    (dx,), (dg,) = _tile_call(name, fn, [x, dh, dres], [g], [(x.shape[1], F32)], [g.shape], 512)
    return dx, dg


def _colsum(x, name):
    _, (s,) = _tile_call(name, lambda xv: ([], [jnp.sum(xv.astype(F32), axis=0, keepdims=True)]), [x], [], [],
                         [(1, x.shape[1])], 512)
    return s


def _seq_flags(i, tiles_per_seq):
    pos = i % tiles_per_seq
    return pos == 0, pos == tiles_per_seq - 1


def _dwconv_fwd(x, w, b, seq, name):
    t_rows, ch = x.shape
    kw = w.shape[0]
    tile = min(512, seq)
    cb = _divisor_tile(ch, 256, LANES)
    tps, hb = seq // tile, tile // HALO

    def body(x_ref, halo_ref, w_ref, b_ref, y_ref, pad_ref):
        first, _ = _seq_flags(pl.program_id(0), tps)
        pad_ref[0:HALO, :] = jnp.where(first, 0.0, halo_ref[...])
        pad_ref[HALO:HALO + tile, :] = x_ref[...]
        acc = jnp.broadcast_to(b_ref[...], (tile, cb))
        for k in range(kw):
            acc = acc + pad_ref[pl.ds(HALO - (kw - 1) + k, tile), :] * w_ref[k:k + 1, :]
        y_ref[...] = acc

    return pl.pallas_call(
        body, name=name, grid=(t_rows // tile, ch // cb),
        in_specs=[pl.BlockSpec((tile, cb), lambda i, j: (i, j)),
                  pl.BlockSpec((HALO, cb), lambda i, j: (jnp.maximum(i * hb - 1, 0), j)),
                  pl.BlockSpec((kw, cb), lambda i, j: (0, j)), pl.BlockSpec((1, cb), lambda i, j: (0, j))],
        out_specs=pl.BlockSpec((tile, cb), lambda i, j: (i, j)),
        out_shape=jax.ShapeDtypeStruct((t_rows, ch), F32),
        scratch_shapes=[pltpu.VMEM((HALO + tile, cb), F32)],
        compiler_params=pltpu.CompilerParams(dimension_semantics=("parallel", "parallel")),
    )(x, x, w, b)


def _dwconv_bwd(x, dy, w, seq, name):
    t_rows, ch = x.shape
    kw = w.shape[0]
    tile = min(512, seq)
    cb = _divisor_tile(ch, 256, LANES)
    tps, hb, n_hb = seq // tile, tile // HALO, t_rows // HALO

    def body(x_ref, xh_ref, dy_ref, dyh_ref, w_ref, dx_ref, dw_ref, db_ref, xpad, dypad):
        i = pl.program_id(1)
        first, last = _seq_flags(i, tps)

        @pl.when(i == 0)
        def _():
            dw_ref[...] = jnp.zeros_like(dw_ref)
            db_ref[...] = jnp.zeros_like(db_ref)

        xpad[0:HALO, :] = jnp.where(first, 0.0, xh_ref[...])
        xpad[HALO:HALO + tile, :] = x_ref[...]
        dyv = dy_ref[...]
        dypad[0:tile, :] = dyv
        dypad[tile:tile + HALO, :] = jnp.where(last, 0.0, dyh_ref[...])
        acc = jnp.zeros((tile, cb), F32)
        for k in range(kw):
            acc = acc + dypad[pl.ds(kw - 1 - k, tile), :] * w_ref[k:k + 1, :]
            dw_ref[k:k + 1, :] += jnp.sum(dyv * xpad[pl.ds(HALO - (kw - 1) + k, tile), :], axis=0, keepdims=True)
        dx_ref[...] = acc
        db_ref[...] += jnp.sum(dyv, axis=0, keepdims=True)

    return pl.pallas_call(
        body, name=name, grid=(ch // cb, t_rows // tile),
        in_specs=[pl.BlockSpec((tile, cb), lambda j, i: (i, j)),
                  pl.BlockSpec((HALO, cb), lambda j, i: (jnp.maximum(i * hb - 1, 0), j)),
                  pl.BlockSpec((tile, cb), lambda j, i: (i, j)),
                  pl.BlockSpec((HALO, cb), lambda j, i: (jnp.minimum((i + 1) * hb, n_hb - 1), j)),
                  pl.BlockSpec((kw, cb), lambda j, i: (0, j))],
        out_specs=[pl.BlockSpec((tile, cb), lambda j, i: (i, j)), pl.BlockSpec((kw, cb), lambda j, i: (0, j)),
                   pl.BlockSpec((1, cb), lambda j, i: (0, j))],
        out_shape=[jax.ShapeDtypeStruct((t_rows, ch), F32), jax.ShapeDtypeStruct((kw, ch), F32),
                   jax.ShapeDtypeStruct((1, ch), F32)],
        scratch_shapes=[pltpu.VMEM((HALO + tile, cb), F32), pltpu.VMEM((tile + HALO, cb), F32)],
        compiler_params=pltpu.CompilerParams(dimension_semantics=("parallel", "arbitrary")),
    )(x, x, dy, dy, w)


def _ffn_mid_fwd(u, w, seq, name):
    t_rows, f2 = u.shape
    f = f2 // 2
    tile = min(256, seq)
    cb = _divisor_tile(f, 1408, LANES)
    nj, tps, hb, hl = f // cb, seq // tile, tile // SUBLANES_BF16, SUBLANES_BF16

    def body(ug_ref, uv_ref, hg_ref, hv_ref, wg_ref, wv_ref, a_ref, gpad, vpad):
        first, _ = _seq_flags(pl.program_id(0), tps)

        def conv(t_ref, h_ref, w_ref, pad):
            pad[0:hl, :] = jnp.where(first, 0.0, h_ref[...].astype(F32))
            pad[hl:hl + tile, :] = t_ref[...].astype(F32)
            z = pad[pl.ds(hl, tile), :] * w_ref[2:3, :]
            z = z + pad[pl.ds(hl - 1, tile), :] * w_ref[1:2, :]
            return z + pad[pl.ds(hl - 2, tile), :] * w_ref[0:1, :]

        zg = conv(ug_ref, hg_ref, wg_ref, gpad)
        zv = conv(uv_ref, hv_ref, wv_ref, vpad)
        a_ref[...] = (_silu(zg) * zv).astype(a_ref.dtype)

    halo_map = lambda off: (lambda i, j: (jnp.maximum(i * hb - 1, 0), j + off))
    return pl.pallas_call(
        body, name=name, grid=(t_rows // tile, nj),
        in_specs=[pl.BlockSpec((tile, cb), lambda i, j: (i, j)), pl.BlockSpec((tile, cb), lambda i, j: (i, j + nj)),
                  pl.BlockSpec((hl, cb), halo_map(0)), pl.BlockSpec((hl, cb), halo_map(nj)),
                  pl.BlockSpec((3, cb), lambda i, j: (0, j)), pl.BlockSpec((3, cb), lambda i, j: (0, j + nj))],
        out_specs=pl.BlockSpec((tile, cb), lambda i, j: (i, j)),
        out_shape=jax.ShapeDtypeStruct((t_rows, f), BF16),
        scratch_shapes=[pltpu.VMEM((hl + tile, cb), F32), pltpu.VMEM((hl + tile, cb), F32)],
        compiler_params=pltpu.CompilerParams(dimension_semantics=("parallel", "parallel")),
    )(u, u, u, u, w, w)


def _ffn_mid_bwd(u, da, w, seq, name):
    t_rows, f2 = u.shape
    f = f2 // 2
    tile = min(256, seq)
    cb = _divisor_tile(f, 1408, LANES)
    hl = SUBLANES_BF16
    nj, tps, hb, n_hb, ext = f // cb, seq // tile, tile // hl, t_rows // hl, tile + hl

    def body(ug_ref, uv_ref, pg_ref, pv_ref, ng_ref, nv_ref, da_ref, dan_ref, wg_ref, wv_ref,
             dug_ref, duv_ref, dwg_ref, dwv_ref, gpad, vpad, dzg, dzv):
        i = pl.program_id(1)
        first, last = _seq_flags(i, tps)

        @pl.when(i == 0)
        def _():
            dwg_ref[...] = jnp.zeros_like(dwg_ref)
            dwv_ref[...] = jnp.zeros_like(dwv_ref)

        def conv(t_ref, p_ref, n_ref, w_ref, pad):
            pad[0:hl, :] = jnp.where(first, 0.0, p_ref[...].astype(F32))
            pad[hl:hl + tile, :] = t_ref[...].astype(F32)
            pad[hl + tile:hl + ext, :] = jnp.where(last, 0.0, n_ref[...].astype(F32))
            z = pad[pl.ds(hl, ext), :] * w_ref[2:3, :]
            z = z + pad[pl.ds(hl - 1, ext), :] * w_ref[1:2, :]
            return z + pad[pl.ds(hl - 2, ext), :] * w_ref[0:1, :]

        zg = conv(ug_ref, pg_ref, ng_ref, wg_ref, gpad)
        zv = conv(uv_ref, pv_ref, nv_ref, wv_ref, vpad)
        da_ext = jnp.concatenate(
            [da_ref[...].astype(F32), jnp.where(last, 0.0, dan_ref[...].astype(F32))], axis=0)
        sg = _sigmoid(zg)
        dzg[...] = da_ext * zv * (sg * (1.0 + zg * (1.0 - sg)))
        dzv[...] = da_ext * (zg * sg)

        def back(dz, w_ref, pad, du_ref, dw_ref):
            du = dz[pl.ds(2, tile), :] * w_ref[0:1, :]
            du = du + dz[pl.ds(1, tile), :] * w_ref[1:2, :]
            du_ref[...] = (du + dz[pl.ds(0, tile), :] * w_ref[2:3, :]).astype(du_ref.dtype)
            dzt = dz[pl.ds(0, tile), :]
            for k in range(3):
                dw_ref[k:k + 1, :] += jnp.sum(dzt * pad[pl.ds(hl - 2 + k, tile), :], axis=0, keepdims=True)

        back(dzg, wg_ref, gpad, dug_ref, dwg_ref)
        back(dzv, wv_ref, vpad, duv_ref, dwv_ref)

    prev_map = lambda off: (lambda j, i: (jnp.maximum(i * hb - 1, 0), j + off))
    next_map = lambda off: (lambda j, i: (jnp.minimum((i + 1) * hb, n_hb - 1), j + off))
    tile_spec = lambda off: pl.BlockSpec((tile, cb), lambda j, i: (i, j + off))
    w_spec = lambda off: pl.BlockSpec((3, cb), lambda j, i: (0, j + off))
    return pl.pallas_call(
        body, name=name, grid=(nj, t_rows // tile),
        in_specs=[tile_spec(0), tile_spec(nj), pl.BlockSpec((hl, cb), prev_map(0)), pl.BlockSpec((hl, cb), prev_map(nj)),
                  pl.BlockSpec((hl, cb), next_map(0)), pl.BlockSpec((hl, cb), next_map(nj)),
                  tile_spec(0), pl.BlockSpec((hl, cb), next_map(0)), w_spec(0), w_spec(nj)],
        out_specs=[tile_spec(0), tile_spec(0), w_spec(0), w_spec(0)],
        out_shape=[jax.ShapeDtypeStruct((t_rows, f), BF16), jax.ShapeDtypeStruct((t_rows, f), BF16),
                   jax.ShapeDtypeStruct((3, f), F32), jax.ShapeDtypeStruct((3, f), F32)],
        scratch_shapes=[pltpu.VMEM((hl + ext, cb), F32), pltpu.VMEM((hl + ext, cb), F32),
                        pltpu.VMEM((ext, cb), F32), pltpu.VMEM((ext, cb), F32)],
        compiler_params=pltpu.CompilerParams(dimension_semantics=("parallel", "arbitrary")),
    )(u, u, u, u, u, u, da, da, w, w)


def _gla_chunk(q, k, v, lg, st, *, scale, chunk):
    row = lax.broadcasted_iota(jnp.int32, (chunk, chunk), 0)
    col = lax.broadcasted_iota(jnp.int32, (chunk, chunk), 1)
    causal = col <= row
    b = _hdot(causal.astype(F32), lg)
    upto_mid = lax.broadcasted_iota(jnp.int32, lg.shape, 0) <= chunk // 2
    b_mid = jnp.sum(jnp.where(upto_mid, lg, 0.0), axis=0, keepdims=True)
    b_last = jnp.sum(lg, axis=0, keepdims=True)
    qs = q * scale
    scores = _bdot_nt(qs * jnp.exp(b - b_mid), k * jnp.exp(b_mid - b))
    o = _bdot(jnp.where(causal, scores, 0.0), v)
    o = o + _bdot_nt(qs * jnp.exp(b), st)
    st_new = st * jnp.exp(b_last) + _bdot_tn(v, k * jnp.exp(b_last - b))
    return o, st_new


def _gla_specs(specs, chunk, n_chunks, reverse):
    if reverse:
        row = lambda bi, ci: bi * n_chunks + (n_chunks - 1 - ci)
    else:
        row = lambda bi, ci: bi * n_chunks + ci
    return [pl.BlockSpec((chunk, w), lambda bi, ci, cb=cb: (row(bi, ci), cb)) for _, w, cb in specs], row


def _gla_fwd(q, k, v, lg, *, heads, dk, dv, scale, chunk, seq, name):
    t_rows = q[0].shape[0]
    n_chunks = seq // chunk
    fn = functools.partial(_gla_chunk, scale=scale, chunk=chunk)

    def body(q_ref, k_ref, v_ref, lg_ref, o_ref, sts_ref, st_ref):
        @pl.when(pl.program_id(1) == 0)
        def _():
            st_ref[...] = jnp.zeros_like(st_ref)

        sts_ref[0] = st_ref[...]
        for h in range(heads):
            ks, vs = slice(h * dk, (h + 1) * dk), slice(h * dv, (h + 1) * dv)
            o, st = fn(q_ref[:, ks].astype(F32), k_ref[:, ks].astype(F32), v_ref[:, vs].astype(F32),
                       lg_ref[:, ks], st_ref[vs, :])
            o_ref[:, vs] = o
            st_ref[vs, :] = st

    in_specs, row = _gla_specs([q, k, v, lg], chunk, n_chunks, False)
    return pl.pallas_call(
        body, name=name, grid=(t_rows // seq, n_chunks), in_specs=in_specs,
        out_specs=[pl.BlockSpec((chunk, heads * dv), lambda bi, ci: (row(bi, ci), 0)),
                   pl.BlockSpec((1, heads * dv, dk), lambda bi, ci: (row(bi, ci), 0, 0))],
        out_shape=[jax.ShapeDtypeStruct((t_rows, heads * dv), F32),
                   jax.ShapeDtypeStruct((t_rows // chunk, heads * dv, dk), F32)],
        scratch_shapes=[pltpu.VMEM((heads * dv, dk), F32)],
        compiler_params=pltpu.CompilerParams(dimension_semantics=("arbitrary", "arbitrary")),
    )(q[0], k[0], v[0], lg[0])


def _gla_bwd(q, k, v, lg, states, do, *, heads, dk, dv, scale, chunk, seq, out_dtypes, name):
    t_rows = q[0].shape[0]
    n_chunks = seq // chunk
    fn = functools.partial(_gla_chunk, scale=scale, chunk=chunk)

    def body(q_ref, k_ref, v_ref, lg_ref, do_ref, sts_ref, dq_ref, dk_ref, dv_ref, dlg_ref, dst_ref):
        @pl.when(pl.program_id(1) == 0)
        def _():
            dst_ref[...] = jnp.zeros_like(dst_ref)

        for h in range(heads):
            ks, vs = slice(h * dk, (h + 1) * dk), slice(h * dv, (h + 1) * dv)
            _, vjp = jax.vjp(fn, q_ref[:, ks].astype(F32), k_ref[:, ks].astype(F32), v_ref[:, vs].astype(F32),
                             lg_ref[:, ks], sts_ref[0, vs, :])
            dq, dkk, dvv, dlg, dst = vjp((do_ref[:, vs].astype(F32), dst_ref[vs, :]))
            dq_ref[:, ks] = dq.astype(dq_ref.dtype)
            dk_ref[:, ks] = dkk.astype(dk_ref.dtype)
            dv_ref[:, vs] = dvv.astype(dv_ref.dtype)
            dlg_ref[:, ks] = dlg
            dst_ref[vs, :] = dst

    do_view = (do, heads * dv, 0)
    in_specs, row = _gla_specs([q, k, v, lg, do_view], chunk, n_chunks, True)
    in_specs.append(pl.BlockSpec((1, heads * dv, dk), lambda bi, ci: (row(bi, ci), 0, 0)))
    wide = lambda w: pl.BlockSpec((chunk, w), lambda bi, ci: (row(bi, ci), 0))
    return pl.pallas_call(
        body, name=name, grid=(t_rows // seq, n_chunks), in_specs=in_specs,
        out_specs=[wide(heads * dk), wide(heads * dk), wide(heads * dv), wide(heads * dk)],
        out_shape=[jax.ShapeDtypeStruct((t_rows, heads * dk), out_dtypes[0]),
                   jax.ShapeDtypeStruct((t_rows, heads * dk), out_dtypes[1]),
                   jax.ShapeDtypeStruct((t_rows, heads * dv), out_dtypes[2]),
                   jax.ShapeDtypeStruct((t_rows, heads * dk), F32)],
        scratch_shapes=[pltpu.VMEM((heads * dv, dk), F32)],
        compiler_params=pltpu.CompilerParams(dimension_semantics=("arbitrary", "arbitrary")),
    )(q[0], k[0], v[0], lg[0], do, states)


def _head_rms_gate(o, r, g, heads):
    d = o.shape[1] // heads
    parts = [_rms(o[:, h * d:(h + 1) * d], g) for h in range(heads)]
    return jnp.concatenate(parts, axis=1) * _silu(r)


def _gla_gate(glr, w_g2p, b_g2):
    return _log_sigmoid(_bdot(glr, w_g2p) + b_g2) * (1.0 / GLA_GATE_NORM)


def _glu(a, gate, b_in):
    d = a.shape[1]
    return (a + b_in[:, :d]) * _sigmoid(gate + b_in[:, d:])


def _ln_silu(y, g, b):
    return _silu(_layer_norm(y, g, b))


def _sgu(pre, b_in, ln_g, ln_b, w_s, b_st):
    d = pre.shape[1] // 2
    gd = d // SGU_GROUPS
    uv = _gelu(pre + b_in)
    u, v = uv[:, :d], _layer_norm(uv[:, d:], ln_g, ln_b)
    row = lax.broadcasted_iota(jnp.int32, (SGU_CHUNK, SGU_CHUNK), 0)
    col = lax.broadcasted_iota(jnp.int32, (SGU_CHUNK, SGU_CHUNK), 1)
    lane = lax.broadcasted_iota(jnp.int32, b_st.shape, 1)
    rows = []
    for c in range(pre.shape[0] // SGU_CHUNK):
        rs = slice(c * SGU_CHUNK, (c + 1) * SGU_CHUNK)
        parts = []
        for g in range(SGU_GROUPS):
            wg = jnp.where(col <= row, w_s[g], 0.0)
            bias = jnp.sum(jnp.where(lane == g, b_st, 0.0), axis=1, keepdims=True)
            parts.append(_bdot(wg, v[rs, g * gd:(g + 1) * gd]) + bias)
        rows.append(jnp.concatenate(parts, axis=1))
    s = rows[0] if len(rows) == 1 else jnp.concatenate(rows, axis=0)
    return u * s


def _hgrn_pre(q, f, table, layer):
    t = table - jnp.max(table, axis=0, keepdims=True)
    e = jnp.exp(t)
    sm = e / jnp.sum(e, axis=0, keepdims=True)
    rows = lax.broadcasted_iota(jnp.int32, table.shape, 0)
    lb = jnp.sum(jnp.where((rows >= 1) & (rows <= layer), sm, 0.0), axis=0, keepdims=True)
    sf = _sigmoid(f)
    return _silu(q), (1.0 - lb) * (1.0 - sf), jnp.log(lb + (1.0 - lb) * sf)


def _ffn_fwd(x, w, seq, sv):
    sv["h2"] = _rms_fwd(x, w["norm"], "ffn_norm")
    sv["u"] = _mm(sv["h2"], w["w_up"], out_dtype=BF16, name="ffn_up")
    sv["a"] = _ffn_mid_fwd(sv["u"], w["w_dw"], seq, "ffn_mid")
    return _mm(sv["a"], w["w_down"], add=x, name="ffn_down")


def _ffn_bwd(x, dy, w, seq, sv, deps=()):
    g = {}
    da = _mm(dy, w["w_down_t"], out_dtype=BF16, name="ffn_down_dx", deps=deps)
    g["w_down"] = _mm_tn(sv["a"], dy, name="ffn_down_dw")
    dug, duv, dwg, dwv = _ffn_mid_bwd(sv["u"], da, w["w_dw"], seq, "ffn_mid_bwd")
    g["w_dw"] = jnp.concatenate([dwg, dwv], axis=1)
    g["w_up_gate"] = _mm_tn(sv["h2"], dug, name="ffn_up_dw")
    g["w_up_val"] = _mm_tn(sv["h2"], duv, name="ffn_up_dw")
    dh = _mm(dug, w["w_up_t_gate"], out_dtype=F32, name="ffn_up_dx")
    dh = _mm(duv, w["w_up_t_val"], add=dh, out_dtype=BF16, name="ffn_up_dx2")
    dx, g["norm"] = _rms_bwd(x, w["norm"], dh, dy, "ffn_norm_bwd")
    return dx, g


def _gla_layer_fwd(x, h, w, seq, sv):
    d = x.shape[1]
    dkt = d // 2
    dk, dv = dkt // GLA_HEADS, d // GLA_HEADS
    proj = _mm(h, w["w_main"], out_dtype=F32, name="gla_in")
    glr = _mm(h, w["w_glr"], out_dtype=BF16, name="gla_in_g")
    (lg,), _ = _tile_call("gla_gate", lambda a, b, c: ([_gla_gate(a.astype(F32), b, c)], []), [glr],
                          [w["w_g2p"], w["b_g2"]], [(dkt, F32)], [], 512)
    q, k, v, r = (proj, dkt, 0), (proj, dkt, 1), (proj, d, 1), (proj, d, 2)
    o, states = _gla_fwd(q, k, v, (lg, dkt, 0), heads=GLA_HEADS, dk=dk, dv=dv, scale=dk ** -0.5, chunk=GLA_CHUNK,
                         seq=seq, name="gla_core")
    (o2,), _ = _tile_call("gla_post", lambda ov, rv, gv: ([_head_rms_gate(ov, rv.astype(F32), gv, GLA_HEADS)], []),
                          [o, r], [w["norm"]], [(d, BF16)], [], 256)
    sv.update(proj=proj, glr=glr, lg=lg, o=o, states=states, o2=o2)
    return _mm(o2, w["w_out"], add=x, name="mix_out")


def _gla_layer_bwd(h, dy, w, seq, sv, deps=()):
    d = dy.shape[1]
    dkt = d // 2
    dk, dv = dkt // GLA_HEADS, d // GLA_HEADS
    proj, glr, lg, o = sv["proj"], sv["glr"], sv["lg"], sv["o"]
    g = {}
    do2 = _mm(dy, w["w_out_t"], out_dtype=F32, name="gla_out_dx", deps=deps)
    g["w_out"] = _mm_tn(sv["o2"], dy, name="mix_out_dw")

    def post_bwd(ov, rv, ctv, gv):
        _, vjp = jax.vjp(functools.partial(_head_rms_gate, heads=GLA_HEADS), ov, rv.astype(F32), gv)
        d_o, d_r, d_g = vjp(ctv.astype(F32))
        return [d_o, d_r], [d_g]

    (d_o, d_r), (g["norm"],) = _tile_call("gla_post_bwd", post_bwd, [o, (proj, d, 2), do2], [w["norm"]],
                                          [(d, F32), (d, BF16)], [w["norm"].shape], 256)
    q, k, v = (proj, dkt, 0), (proj, dkt, 1), (proj, d, 1)
    dq, dkk, dvv, dlg = _gla_bwd(q, k, v, (lg, dkt, 0), sv["states"], d_o, heads=GLA_HEADS, dk=dk, dv=dv,
                                 scale=dk ** -0.5, chunk=GLA_CHUNK, seq=seq, out_dtypes=(BF16, BF16, BF16),
                                 name="gla_core_bwd")

    def gate_bwd(glrv, ctv, wv, bv):
        _, vjp = jax.vjp(_gla_gate, glrv.astype(F32), wv, bv)
        d_glr, d_w, d_b = vjp(ctv)
        return [d_glr], [d_w, d_b]

    (dglr,), (g["w_g2p"], g["b_g2"]) = _tile_call("gla_gate_bwd", gate_bwd, [glr, dlg], [w["w_g2p"], w["b_g2"]],
                                                  [(LANES, BF16)], [w["w_g2p"].shape, w["b_g2"].shape], 512)
    dproj = jnp.concatenate([dq, dkk, dvv, d_r], axis=1)
    g["w_main"] = _mm_tn(h, dproj, name="gla_in_dw")
    g["w_glr"] = _mm_tn(h, dglr, name="gla_in_g_dw")
    dh = _mm(dproj, w["w_main_t"], out_dtype=F32, name="gla_in_dx")
    dh = _mm(dglr, w["w_glr_t"], add=dh, out_dtype=BF16, name="gla_in_g_dx")
    return dh, g


def _cv_layer_fwd(x, h, w, seq, sv):
    d = x.shape[1]
    pre = _mm(h, w["w_in"], out_dtype=BF16, name="cv_in")
    (y1,), _ = _tile_call("cv_glu", lambda a, gt, b: ([_glu(a.astype(F32), gt.astype(F32), b)], []),
                          [(pre, d, 0), (pre, d, 1)], [w["b_in"]], [(d, F32)], [], 512)
    y2 = _dwconv_fwd(y1, w["w_dw"], w["b_dw"], seq, "cv_conv")
    (y3,), _ = _tile_call("cv_ln", lambda y, a, b: ([_ln_silu(y, a, b)], []), [y2], [w["ln_g"], w["ln_b"]],
                          [(d, BF16)], [], 512)
    sv.update(pre=pre, y1=y1, y2=y2, y3=y3)
    return _mm(y3, w["w_out"], bias=w["b_out"], add=x, name="mix_out_b")


def _cv_layer_bwd(h, dy, w, seq, sv, deps=()):
    d = dy.shape[1]
    pre = sv["pre"]
    g = {}
    dy3 = _mm(dy, w["w_out_t"], out_dtype=BF16, name="mix_out_dx", deps=deps)
    g["w_out"] = _mm_tn(sv["y3"], dy, name="mix_out_dw")
    g["b_out"] = _colsum(dy, "bias_out_dw")

    def ln_bwd(yv, ctv, av, bv):
        _, vjp = jax.vjp(_ln_silu, yv, av, bv)
        d_y, d_a, d_b = vjp(ctv.astype(F32))
        return [d_y], [d_a, d_b]

    (dy2,), (g["ln_g"], g["ln_b"]) = _tile_call("cv_ln_bwd", ln_bwd, [sv["y2"], dy3], [w["ln_g"], w["ln_b"]],
                                                [(d, F32)], [w["ln_g"].shape, w["ln_b"].shape], 512)
    dy1, g["w_dw"], g["b_dw"] = _dwconv_bwd(sv["y1"], dy2, w["w_dw"], seq, "cv_conv_bwd")

    def glu_bwd(av, gv, ctv, bv):
        _, vjp = jax.vjp(_glu, av.astype(F32), gv.astype(F32), bv)
        d_a, d_g, d_b = vjp(ctv)
        return [jnp.concatenate([d_a, d_g], axis=1)], [d_b]

    (dpre,), (g["b_in"],) = _tile_call("cv_glu_bwd", glu_bwd, [(pre, d, 0), (pre, d, 1), dy1], [w["b_in"]],
                                       [(2 * d, BF16)], [w["b_in"].shape], 512)
    g["w_in"] = _mm_tn(h, dpre, name="in2_dw")
    dh = _mm(dpre, w["w_in_t"], out_dtype=BF16, name="in2_dx")
    return dh, g


def _sg_layer_fwd(x, h, w, seq, sv):
    d = x.shape[1]
    pre = _mm(h, w["w_in"], out_dtype=BF16, name="sg_in")
    pars = [w["b_in"], w["ln_g"], w["ln_b"], w["w_s"], w["b_st"]]
    (p,), _ = _tile_call("sg_gate", lambda pv, *ps: ([_sgu(pv.astype(F32), *ps)], []), [pre], pars, [(d, BF16)], [],
                         SGU_CHUNK)
    sv.update(pre=pre, p=p)
    return _mm(p, w["w_out"], bias=w["b_out"], add=x, name="mix_out_b")


def _sg_layer_bwd(h, dy, w, seq, sv, deps=()):
    d = dy.shape[1]
    g = {}
    dp = _mm(dy, w["w_out_t"], out_dtype=BF16, name="mix_out_dx", deps=deps)
    g["w_out"] = _mm_tn(sv["p"], dy, name="mix_out_dw")
    g["b_out"] = _colsum(dy, "bias_out_dw")
    pars = [w["b_in"], w["ln_g"], w["ln_b"], w["w_s"], w["b_st"]]

    def sgu_bwd(pv, ctv, *ps):
        _, vjp = jax.vjp(_sgu, pv.astype(F32), *ps)
        grads = vjp(ctv.astype(F32))
        return [grads[0]], list(grads[1:])

    (dpre,), (g["b_in"], g["ln_g"], g["ln_b"], g["w_s"], g["b_st"]) = _tile_call(
        "sg_gate_bwd", sgu_bwd, [sv["pre"], dp], pars, [(2 * d, BF16)], [p.shape for p in pars], SGU_CHUNK)
    g["w_in"] = _mm_tn(h, dpre, name="in2_dw")
    dh = _mm(dpre, w["w_in_t"], out_dtype=BF16, name="in2_dx")
    return dh, g


def _hg_layer_fwd(x, h, w, seq, sv, layer):
    d = x.shape[1]
    heads = d // HGRN_EXPAND
    proj = _mm(h, w["w_in"], out_dtype=BF16, name="hg_in")
    pre = functools.partial(_hgrn_pre, layer=layer)
    (qs, kk, lg), _ = _tile_call("hg_pre", lambda qv, fv, tb: (list(pre(qv.astype(F32), fv.astype(F32), tb)), []),
                                 [(proj, d, 0), (proj, d, 1)], [w["lb_table"]], [(d, BF16), (d, F32), (d, F32)], [], 256)
    o, states = _gla_fwd((qs, d, 0), (kk, d, 0), (proj, d, 2), (lg, d, 0), heads=heads, dk=HGRN_EXPAND,
                         dv=HGRN_EXPAND, scale=1.0, chunk=HGRN_CHUNK, seq=seq, name="hg_core")
    (o2,), _ = _tile_call("hg_post", lambda ov, gv, nv: ([_head_rms_gate(ov, gv.astype(F32), nv, heads)], []),
                          [o, (proj, d, 3)], [w["norm"]], [(d, BF16)], [], 256)
    sv.update(proj=proj, qs=qs, kk=kk, lg=lg, o=o, states=states, o2=o2)
    return _mm(o2, w["w_out"], add=x, name="mix_out")


def _hg_layer_bwd(h, dy, w, seq, sv, layer, deps=()):
    d = dy.shape[1]
    heads = d // HGRN_EXPAND
    proj = sv["proj"]
    g = {}
    do2 = _mm(dy, w["w_out_t"], out_dtype=BF16, name="mix_out_dx", deps=deps)
    g["w_out"] = _mm_tn(sv["o2"], dy, name="mix_out_dw")

    def post_bwd(ov, gv, ctv, nv):
        _, vjp = jax.vjp(functools.partial(_head_rms_gate, heads=heads), ov, gv.astype(F32), nv)
        d_o, d_g, d_n = vjp(ctv.astype(F32))
        return [d_o, d_g], [d_n]

    (d_o, d_gate), (g["norm"],) = _tile_call("hg_post_bwd", post_bwd, [sv["o"], (proj, d, 3), do2], [w["norm"]],
                                             [(d, F32), (d, BF16)], [w["norm"].shape], 256)
    dqs, dkk, di, dlg = _gla_bwd((sv["qs"], d, 0), (sv["kk"], d, 0), (proj, d, 2), (sv["lg"], d, 0), sv["states"], d_o,
                                 heads=heads, dk=HGRN_EXPAND, dv=HGRN_EXPAND, scale=1.0, chunk=HGRN_CHUNK, seq=seq,
                                 out_dtypes=(F32, F32, BF16), name="hg_core_bwd")

    def pre_bwd(qv, fv, c1, c2, c3, tb):
        _, vjp = jax.vjp(functools.partial(_hgrn_pre, layer=layer), qv.astype(F32), fv.astype(F32), tb)
        d_q, d_f, d_t = vjp((c1, c2, c3))
        return [jnp.concatenate([d_q, d_f], axis=1)], [d_t]

    (dqf,), (g["lb_table"],) = _tile_call("hg_pre_bwd", pre_bwd, [(proj, d, 0), (proj, d, 1), dqs, dkk, dlg],
                                          [w["lb_table"]], [(2 * d, BF16)], [w["lb_table"].shape], 256)
    dproj = jnp.concatenate([dqf, di, d_gate], axis=1)
    g["w_in"] = _mm_tn(h, dproj, name="hg_in_dw")
    dh = _mm(dproj, w["w_in_t"], out_dtype=BF16, name="hg_in_dx")
    return dh, g


_MIXERS = ("gla", "cv", "sg", "hg")


_BIG_KEYS = {"gla": ("w_main", "w_glr", "w_out"), "cv": ("w_in", "w_out"), "sg": ("w_in", "w_out"), "hg": ("w_in", "w_out"),
             "ffn": ("w_up_gate", "w_up_val", "w_down")}


def _local_step(x, target, w, seq, get_big, put_big, deps=()):
    depth = w["norm_mix"].shape[0]
    d = x.shape[1]
    saved, big = [], {}
    for layer in range(depth):
        mixer = _MIXERS[layer % 4]
        sv = {"x_in": x}
        sv["h"] = _rms_fwd(x, w["norm_mix"][layer:layer + 1], "mix_norm", deps if layer == 0 else ())
        big[mixer, layer] = get_big(mixer, layer, sv["h"])
        wm = dict(w[mixer], **big[mixer, layer])
        if mixer == "gla":
            x = _gla_layer_fwd(x, sv["h"], wm, seq, sv)
        elif mixer == "cv":
            x = _cv_layer_fwd(x, sv["h"], wm, seq, sv)
        elif mixer == "sg":
            x = _sg_layer_fwd(x, sv["h"], wm, seq, sv)
        else:
            x = _hg_layer_fwd(x, sv["h"], wm, seq, sv, layer)
        sv["x_mid"] = x
        sv["ffn"] = {}
        big["ffn", layer] = get_big("ffn", layer, x)
        wf = dict(w["ffn"][layer], norm=w["norm_ffn"][layer:layer + 1], **big["ffn", layer])
        x = _ffn_fwd(x, wf, seq, sv["ffn"])
        saved.append(sv)

    def head(xv, tv, gv):
        y, vjp = jax.vjp(_rms, xv, gv)
        err = y - tv
        dx, dg = vjp(err * (1.0 / d))
        part = 0.5 * jnp.sum(jnp.mean(err * err, axis=-1, keepdims=True), axis=0, keepdims=True)
        return [dx], [jnp.broadcast_to(part, (1, LANES)), dg]

    (dx,), (loss, g_final) = _tile_call("loss_head", head, [x, target], [w["norm_final"]], [(d, F32)],
                                        [(1, LANES), (1, d)], 512)
    grads = {"norm_final": g_final, "norm_mix": [None] * depth, "norm_ffn": [None] * depth, "ffn": [None] * depth}
    order = ()
    for layer in reversed(range(depth)):
        mixer = _MIXERS[layer % 4]
        sv = saved[layer]
        wf = dict(w["ffn"][layer], norm=w["norm_ffn"][layer:layer + 1], **big["ffn", layer])
        dx, gf = _ffn_bwd(sv["x_mid"], dx, wf, seq, sv["ffn"], order)
        order = put_big("ffn", layer, {k: gf.pop(k) for k in _BIG_KEYS["ffn"]})
        grads["norm_ffn"][layer] = gf.pop("norm")
        grads["ffn"][layer] = gf
        wm = dict(w[mixer], **big[mixer, layer])
        if mixer == "gla":
            dh, gm = _gla_layer_bwd(sv["h"], dx, wm, seq, sv, order)
        elif mixer == "cv":
            dh, gm = _cv_layer_bwd(sv["h"], dx, wm, seq, sv, order)
        elif mixer == "sg":
            dh, gm = _sg_layer_bwd(sv["h"], dx, wm, seq, sv, order)
        else:
            dh, gm = _hg_layer_bwd(sv["h"], dx, wm, seq, sv, layer, order)
        order = put_big(mixer, layer, {k: gm.pop(k) for k in _BIG_KEYS[mixer]})
        grads[mixer] = gm
        dx, grads["norm_mix"][layer] = _rms_bwd(sv["x_in"], w["norm_mix"][layer:layer + 1], dh, dx, "mix_norm_bwd")
    return loss, dx, grads


def _prep_small(p):
    row = lambda a: a.reshape(1, -1).astype(F32)
    w = {"norm_mix": p["norm_mix"].astype(F32), "norm_ffn": p["norm_ffn"].astype(F32), "norm_final": row(p["norm_final"])}
    w["gla"] = dict(w_g2p=jnp.pad(p["gla_w_g2"][0].astype(F32), ((0, LANES - GLA_RANK), (0, 0))), b_g2=row(p["gla_b_g2"]),
                    norm=row(p["gla_norm"]))
    w["cv"] = dict(b_in=row(p["cv_b_in"]), w_dw=p["cv_w_dw"][0].astype(F32), b_dw=row(p["cv_b_dw"]), ln_g=row(p["cv_ln_g"]),
                   ln_b=row(p["cv_ln_b"]), b_out=row(p["cv_b_out"]))
    b_st = jnp.pad(p["sg_b_s"][0].astype(F32).T, ((0, 0), (0, LANES - SGU_GROUPS)))
    w["sg"] = dict(b_in=row(p["sg_b_in"]), ln_g=row(p["sg_ln_g"]), ln_b=row(p["sg_ln_b"]), w_s=p["sg_w_s"][0].astype(F32),
                   b_st=b_st, b_out=row(p["sg_b_out"]))
    w["hg"] = dict(lb_table=p["hg_lb_table"].astype(F32), norm=row(p["hg_norm"]))
    w["ffn"] = [dict(w_dw=p["ffn_w_dw"][layer].astype(F32)) for layer in range(p["ffn_w_dw"].shape[0])]
    return w


def _small_grads(g):
    gla, cv, sg, hg = g["gla"], g["cv"], g["sg"], g["hg"]
    return {
        "norm_mix": jnp.concatenate(g["norm_mix"], axis=0), "norm_ffn": jnp.concatenate(g["norm_ffn"], axis=0),
        "norm_final": g["norm_final"][0],
        "gla_w_g2": gla["w_g2p"][:GLA_RANK][None], "gla_b_g2": gla["b_g2"], "gla_norm": gla["norm"],
        "cv_b_in": cv["b_in"], "cv_w_dw": cv["w_dw"][None], "cv_b_dw": cv["b_dw"], "cv_ln_g": cv["ln_g"],
        "cv_ln_b": cv["ln_b"], "cv_b_out": cv["b_out"],
        "sg_b_in": sg["b_in"], "sg_ln_g": sg["ln_g"], "sg_ln_b": sg["ln_b"], "sg_w_s": sg["w_s"][None],
        "sg_b_s": sg["b_st"][:, :SGU_GROUPS].T[None], "sg_b_out": sg["b_out"],
        "hg_lb_table": hg["lb_table"], "hg_norm": hg["norm"],
        "ffn_w_dw": jnp.stack([f["w_dw"] for f in g["ffn"]]),
    }


def _oriented(kind, mats):
    if kind == "ffn":
        (up, up_t), (down, down_t) = mats["w_up"], mats["w_down"]
        f = down.shape[0]
        return dict(w_up=up, w_up_t_gate=up_t[:f], w_up_t_val=up_t[f:], w_down=down, w_down_t=down_t)
    (w_in, w_in_t), (w_out, w_out_t) = mats["w_in"], mats["w_out"]
    if kind != "gla":
        return dict(w_in=w_in, w_in_t=w_in_t, w_out=w_out, w_out_t=w_out_t)
    n_main = w_in.shape[1] - GLA_RANK
    return dict(w_main=w_in[:, :n_main], w_glr=jnp.pad(w_in[:, n_main:], ((0, 0), (0, LANES - GLA_RANK))),
                w_main_t=w_in_t[:n_main], w_glr_t=jnp.pad(w_in_t[n_main:], ((0, LANES - GLA_RANK), (0, 0))),
                w_out=w_out, w_out_t=w_out_t)


def _all_gather(x, *, name):
    m_per, n = x.shape

    def body(x_ref, out_ref, send_sems, recv_sems, local_sem):
        mx, my, mc = lax.axis_index("x"), lax.axis_index("y"), lax.axis_index("c")
        me, sibling = (mx, my, mc), (mx, my, 1 - mc)
        chips = [(1 - mx, my), (mx, 1 - my), (1 - mx, 1 - my)]

        def rows(px, py, pc):
            return out_ref.at[pl.ds((4 * px + 2 * py + pc) * m_per, m_per), :]

        def copy(k, block, to, src=None):
            return pltpu.make_async_remote_copy(
                src_ref=rows(*block) if src is None else src, dst_ref=rows(*block), send_sem=send_sems.at[k],
                recv_sem=recv_sems.at[k], device_id=to, device_id_type=MESH)

        mine = pltpu.make_async_copy(x_ref, rows(*me), local_sem)
        mine.start()
        first = [copy(0, me, sibling, src=x_ref)]
        first += [copy(1 + j, me, (*chip, mc), src=x_ref) for j, chip in enumerate(chips)]
        for cp in first:
            cp.start()
        passed = [copy(4 + j, (*chip, mc), sibling) for j, chip in enumerate(chips)]
        for j, chip in enumerate(chips):
            copy(1 + j, (*chip, mc), me).wait_recv()
            passed[j].start()
        copy(0, sibling, me).wait_recv()
        for j, chip in enumerate(chips):
            copy(4 + j, (*chip, 1 - mc), me).wait_recv()
        for cp in first + passed:
            cp.wait_send()
        mine.wait()

    return pl.pallas_call(
        body, name=name, out_shape=jax.ShapeDtypeStruct((N_DEV * m_per, n), x.dtype),
        in_specs=[pl.BlockSpec(memory_space=pltpu.VMEM)], out_specs=pl.BlockSpec(memory_space=pltpu.VMEM),
        scratch_shapes=[pltpu.SemaphoreType.DMA((7,)), pltpu.SemaphoreType.DMA((7,)), pltpu.SemaphoreType.DMA],
    )(x)


def _my_index():
    return 4 * lax.axis_index("x") + 2 * lax.axis_index("y") + lax.axis_index("c")


def _gather_stage(srcs, *, name):
    n = len(srcs)

    def body(*refs):
        x_refs, out_refs = refs[:n], refs[n:2 * n]
        send_sems, recv_sems, local_sems = refs[2 * n:]
        mx, my, mc = lax.axis_index("x"), lax.axis_index("y"), lax.axis_index("c")
        me, sibling = (mx, my, mc), (mx, my, 1 - mc)
        chips = [(1 - mx, my), (mx, 1 - my), (1 - mx, 1 - my)]

        def slot(i, px, py, pc):
            return out_refs[i].at[4 * px + 2 * py + pc]

        def copy(i, k, block, to, src=None):
            return pltpu.make_async_remote_copy(
                src_ref=slot(i, *block) if src is None else src, dst_ref=slot(i, *block), send_sem=send_sems.at[7 * i + k],
                recv_sem=recv_sems.at[7 * i + k], device_id=to, device_id_type=MESH)

        mine = [pltpu.make_async_copy(x_refs[i], slot(i, *me), local_sems.at[i]) for i in range(n)]
        first = [copy(i, 0, me, sibling, src=x_refs[i]) for i in range(n)]
        first += [copy(i, 1 + j, me, (*chip, mc), src=x_refs[i]) for j, chip in enumerate(chips) for i in range(n)]
        for cp in mine + first:
            cp.start()
        passed = []
        for j, chip in enumerate(chips):
            for i in range(n):
                copy(i, 1 + j, (*chip, mc), me).wait_recv()
                passed.append(copy(i, 4 + j, (*chip, mc), sibling))
                passed[-1].start()
        for i in range(n):
            copy(i, 0, sibling, me).wait_recv()
            for j, chip in enumerate(chips):
                copy(i, 4 + j, (*chip, 1 - mc), me).wait_recv()
        for cp in first + passed:
            cp.wait_send()
        for cp in mine:
            cp.wait()

    any_space = pl.BlockSpec(memory_space=pl.ANY)
    return pl.pallas_call(
        body, name=name, out_shape=[jax.ShapeDtypeStruct((N_DEV,) + s.shape, s.dtype) for s in srcs],
        in_specs=[any_space] * n, out_specs=[any_space] * n,
        scratch_shapes=[pltpu.SemaphoreType.DMA((7 * n,)), pltpu.SemaphoreType.DMA((7 * n,)), pltpu.SemaphoreType.DMA((n,))],
    )(*srcs)


def _scatter_stage(srcs, *, name):
    n = len(srcs)

    def body(*refs):
        x_refs, out_refs = refs[:n], refs[n:2 * n]
        send_sems, recv_sems, local_sems = refs[2 * n:]
        mx, my, mc = lax.axis_index("x"), lax.axis_index("y"), lax.axis_index("c")
        me = 4 * mx + 2 * my + mc
        mine = [pltpu.make_async_copy(x_refs[i].at[me], out_refs[i].at[me], local_sems.at[i]) for i in range(n)]
        for cp in mine:
            cp.start()
        sends, recvs = [], []
        for k in range(1, N_DEV):
            px = 1 - mx if k & 4 else mx
            py = 1 - my if k & 2 else my
            pc = 1 - mc if k & 1 else mc
            peer = 4 * px + 2 * py + pc
            for i in range(n):
                sems = dict(send_sem=send_sems.at[7 * i + k - 1], recv_sem=recv_sems.at[7 * i + k - 1],
                            device_id=(px, py, pc), device_id_type=MESH)
                sends.append(pltpu.make_async_remote_copy(src_ref=x_refs[i].at[peer], dst_ref=out_refs[i].at[me], **sems))
                recvs.append(pltpu.make_async_remote_copy(src_ref=x_refs[i].at[me], dst_ref=out_refs[i].at[peer], **sems))
                sends[-1].start()
        for cp in recvs:
            cp.wait_recv()
        for cp in sends:
            cp.wait_send()
        for cp in mine:
            cp.wait()

    any_space = pl.BlockSpec(memory_space=pl.ANY)
    return pl.pallas_call(
        body, name=name, out_shape=[jax.ShapeDtypeStruct(s.shape, s.dtype) for s in srcs],
        in_specs=[any_space] * n, out_specs=[any_space] * n,
        scratch_shapes=[pltpu.SemaphoreType.DMA((7 * n,)), pltpu.SemaphoreType.DMA((7 * n,)), pltpu.SemaphoreType.DMA((n,))],
    )(*srcs)


def _adamw_math(g, w, m, v):
    c1, c2 = 1.0 - ADAM_B1 ** ADAM_STEP, 1.0 - ADAM_B2 ** ADAM_STEP
    m_new = ADAM_B1 * m + (1.0 - ADAM_B1) * g
    v_new = ADAM_B2 * v + (1.0 - ADAM_B2) * (g * g)
    delta = -ADAM_LR * ((m_new / c1) / (jnp.sqrt(v_new / c2) + ADAM_EPS) + ADAM_WD * w)
    return delta, m_new, v_new


def _adamw_big(slots, w, m, v, layer, *, name):
    _, r, c = slots.shape
    tr = _divisor_tile(r, max(8, (200 * 1024) // c // 8 * 8), 8)

    def body(s_ref, w_ref, m_ref, v_ref, g_out, d_out, m_out, v_out):
        g = s_ref[0]
        for p in range(1, N_DEV):
            g = g + s_ref[p]
        g_out[...] = g
        d_out[...], m_out[...], v_out[...] = _adamw_math(g, w_ref[...], m_ref[...], v_ref[...])

    blk = pl.BlockSpec((tr, c), lambda i: (i, 0))
    lay = pl.BlockSpec((None, tr, c), lambda i: (layer, i, 0))
    return pl.pallas_call(
        body, name=name, grid=(r // tr,), in_specs=[pl.BlockSpec((N_DEV, tr, c), lambda i: (0, i, 0)), lay, lay, lay],
        out_specs=[blk] * 4, out_shape=[jax.ShapeDtypeStruct((r, c), F32)] * 4,
        compiler_params=pltpu.CompilerParams(dimension_semantics=("parallel",)),
    )(slots, w, m, v)


def _sum_small(got, r_re, r_sh, *, name):
    per_dev = r_re + N_DEV * r_sh

    def body(got_ref, re_ref, sh_ref):
        mine = r_re + _my_index() * r_sh
        acc_re = got_ref[0:r_re, :]
        acc_sh = got_ref[pl.ds(pl.multiple_of(mine, 8), r_sh), :]
        for p in range(1, N_DEV):
            acc_re = acc_re + got_ref[p * per_dev:p * per_dev + r_re, :]
            acc_sh = acc_sh + got_ref[pl.ds(pl.multiple_of(p * per_dev + mine, 8), r_sh), :]
        re_ref[...] = acc_re
        sh_ref[...] = acc_sh

    return pl.pallas_call(body, name=name, out_shape=[jax.ShapeDtypeStruct((r_re, LANES), F32),
                                                       jax.ShapeDtypeStruct((r_sh, LANES), F32)])(got)


def _adamw_small(gs, ws, ms, vs, *, name):
    n = len(gs)

    def body(*refs):
        ins, outs = refs[:4 * n], refs[4 * n:]
        for i in range(n):
            res = _adamw_math(ins[i][...], ins[n + i][...], ins[2 * n + i][...], ins[3 * n + i][...])
            for j in range(3):
                outs[j * n + i][...] = res[j]

    out = pl.pallas_call(body, name=name, out_shape=[jax.ShapeDtypeStruct(a.shape, F32) for a in ws] * 3)(*gs, *ws, *ms, *vs)
    return out[:n], out[n:2 * n], out[2 * n:]


def _layout(shapes, row_align, total_align):
    lay, off = {}, 0
    for name, shape in shapes.items():
        size = int(np.prod(shape))
        rows = -(-size // LANES)
        rows = -(-rows // row_align) * row_align
        lay[name] = (off, rows, size, tuple(shape))
        off += rows
    return lay, -(-off // total_align) * total_align


def _pack(arrs, lay, total, dtype, lead=()):
    parts = []
    nl = len(lead)
    for name, (off, rows, size, shape) in lay.items():
        flat = arrs[name].astype(dtype).reshape(*lead, size)
        parts.append(jnp.pad(flat, [(0, 0)] * nl + [(0, rows * LANES - size)]).reshape(*lead, rows, LANES))
    used = sum(v[1] for v in lay.values())
    if total > used:
        parts.append(jnp.zeros((*lead, total - used, LANES), dtype))
    return jnp.concatenate(parts, axis=nl)


def _unpack(buf, lay, lead=()):
    out = {}
    nl = len(lead)
    for name, (off, rows, size, shape) in lay.items():
        part = lax.slice_in_dim(buf, off, off + rows, axis=nl).reshape(*lead, rows * LANES)
        out[name] = lax.slice_in_dim(part, 0, size, axis=nl).reshape(*lead, *shape)
    return out


_SHARD_AXIS = {
    "norm_mix": None, "norm_ffn": None, "norm_final": None, "gla_w_in": 2, "gla_w_g2": 2, "gla_b_g2": None,
    "gla_norm": None, "gla_w_out": 1, "cv_w_in": 2, "cv_b_in": 1, "cv_w_dw": 2, "cv_b_dw": 1, "cv_ln_g": 1,
    "cv_ln_b": 1, "cv_w_out": 1, "cv_b_out": 1, "sg_w_in": 2, "sg_b_in": 1, "sg_ln_g": 1, "sg_ln_b": 1, "sg_w_s": None,
    "sg_b_s": None, "sg_w_out": 1, "sg_b_out": 1, "hg_w_in": 2, "hg_lb_table": None, "hg_norm": None, "hg_w_out": 1,
    "ffn_w_up": 2, "ffn_w_dw": 2, "ffn_w_down": 1,
}
_MATMUL_WEIGHTS = ("gla_w_in", "gla_w_out", "cv_w_in", "cv_w_out", "sg_w_in", "sg_w_out", "hg_w_in", "hg_w_out",
                   "ffn_w_up", "ffn_w_down")
_NAMES = tuple(_SHARD_AXIS)


def kernel(x, norm_mix, norm_ffn, norm_final, gla_w_in, gla_w_g2, gla_b_g2, gla_norm, gla_w_out, cv_w_in, cv_b_in, cv_w_dw, cv_b_dw, cv_ln_g, cv_ln_b, cv_w_out, cv_b_out, sg_w_in, sg_b_in, sg_ln_g, sg_ln_b, sg_w_s, sg_b_s, sg_w_out, sg_b_out, hg_w_in, hg_lb_table, hg_norm, hg_w_out, ffn_w_up, ffn_w_dw, ffn_w_down, loss_target, m_norm_mix, m_norm_ffn, m_norm_final, m_gla_w_in, m_gla_w_g2, m_gla_b_g2, m_gla_norm, m_gla_w_out, m_cv_w_in, m_cv_b_in, m_cv_w_dw, m_cv_b_dw, m_cv_ln_g, m_cv_ln_b, m_cv_w_out, m_cv_b_out, m_sg_w_in, m_sg_b_in, m_sg_ln_g, m_sg_ln_b, m_sg_w_s, m_sg_b_s, m_sg_w_out, m_sg_b_out, m_hg_w_in, m_hg_lb_table, m_hg_norm, m_hg_w_out, m_ffn_w_up, m_ffn_w_dw, m_ffn_w_down, v_norm_mix, v_norm_ffn, v_norm_final, v_gla_w_in, v_gla_w_g2, v_gla_b_g2, v_gla_norm, v_gla_w_out, v_cv_w_in, v_cv_b_in, v_cv_w_dw, v_cv_b_dw, v_cv_ln_g, v_cv_ln_b, v_cv_w_out, v_cv_b_out, v_sg_w_in, v_sg_b_in, v_sg_ln_g, v_sg_ln_b, v_sg_w_s, v_sg_b_s, v_sg_w_out, v_sg_b_out, v_hg_w_in, v_hg_lb_table, v_hg_norm, v_hg_w_out, v_ffn_w_up, v_ffn_w_dw, v_ffn_w_down):
    local = dict(locals())
    wts = {n: local[n] for n in _NAMES}
    mom = {n: local["m_" + n] for n in _NAMES}
    var = {n: local["v_" + n] for n in _NAMES}
    small_all = [n for n in _NAMES if n not in _MATMUL_WEIGHTS]
    small_sharded = [n for n in small_all if _SHARD_AXIS[n] is not None]
    bsz, seq, d = x.shape
    depth = norm_mix.shape[0]

    stages = {}
    for layer in range(depth):
        kind = _MIXERS[layer % 4]
        stages[kind, layer] = {"w_in": (kind + "_w_in", layer // 4), "w_out": (kind + "_w_out", layer // 4)}
        stages["ffn", layer] = {"w_up": ("ffn_w_up", layer), "w_down": ("ffn_w_down", layer)}

    gathered = {}
    for (kind, layer), keys in stages.items():
        srcs = [wts[nm][idx].astype(BF16) for nm, idx in keys.values()]
        gathered[kind, layer] = _gather_stage(srcs, name="gather_" + kind)
    lay_sw, r_sw = _layout({n: wts[n].shape for n in small_sharded}, 8, 8)
    got_sw = _all_gather(_pack(wts, lay_sw, r_sw, F32), name="gather_small_weights")
    parts = _unpack(got_sw.reshape(N_DEV, r_sw, LANES), lay_sw, (N_DEV,))
    full_small = {n: wts[n] for n in small_all if _SHARD_AXIS[n] is None}
    for n in small_sharded:
        ax, shape = _SHARD_AXIS[n], wts[n].shape
        full_small[n] = jnp.moveaxis(parts[n], 0, ax).reshape(shape[:ax] + (N_DEV * shape[ax],) + shape[ax + 1:])

    def get_big(kind, layer, after):
        mats = {}
        for (key, (nm, _)), land in zip(stages[kind, layer].items(), gathered[kind, layer]):
            _, r, c = land.shape
            if _SHARD_AXIS[nm] == 2:
                mats[key] = (land.transpose(1, 0, 2).reshape(r, N_DEV * c), land.transpose(0, 2, 1).reshape(N_DEV * c, r))
            else:
                mats[key] = (land.reshape(N_DEV * r, c), land.reshape(N_DEV * r, c).T)
        return _oriented(kind, mats)

    sends = {}

    def put_big(kind, layer, g):
        if kind == "ffn":
            k, f = g["w_up_gate"].shape
            halves = [g[key].reshape(k, N_DEV // 2, 2 * f // N_DEV) for key in ("w_up_gate", "w_up_val")]
            w_in = jnp.concatenate(halves, axis=1)
        else:
            w_in = jnp.concatenate([g["w_main"], g["w_glr"][:, :GLA_RANK]], axis=1) if kind == "gla" else g["w_in"]
            w_in = w_in.reshape(w_in.shape[0], N_DEV, w_in.shape[1] // N_DEV)
        w_out = g["w_down"] if kind == "ffn" else g["w_out"]
        sends[kind, layer] = [w_in.transpose(1, 0, 2), w_out.reshape(N_DEV, w_out.shape[0] // N_DEV, w_out.shape[1])]
        return ()

    loss, dx, grads = _local_step(x.reshape(bsz * seq, d), loss_target.reshape(bsz * seq, d), _prep_small(full_small), seq,
                                  get_big, put_big)
    loss = lax.psum(loss[0, 0], ("x", "y", "c"))

    gs = _small_grads(grads)
    small_repl = [n for n in small_all if _SHARD_AXIS[n] is None]
    lay_re, r_re = _layout({n: wts[n].shape for n in small_repl}, 8, 8)
    slots = {}
    for n in small_sharded:
        ax, shape = _SHARD_AXIS[n], wts[n].shape
        slots[n] = jnp.moveaxis(gs[n].reshape(shape[:ax] + (N_DEV, shape[ax]) + shape[ax + 1:]), ax, 0)
    sent = jnp.concatenate([_pack(gs, lay_re, r_re, F32), _pack(slots, lay_sw, r_sw, F32, (N_DEV,)).reshape(-1, LANES)])
    sum_re, sum_sh = _sum_small(_all_gather(sent, name="gather_small_grads"), r_re, r_sw, name="sum_small_grads")
    g_own = _unpack(sum_re, lay_re)
    g_own.update(_unpack(sum_sh, lay_sw))
    two_d = lambda a: a.reshape(-1, a.shape[-1])
    upd = _adamw_small(*[[two_d(src[n]) for n in small_all] for src in (g_own, wts, mom, var)], name="adamw_small")
    results = {n: [g_own[n]] + [part[i].reshape(wts[n].shape) for part in upd] for i, n in enumerate(small_all)}

    per_layer = {}
    for (kind, layer), arrays in sends.items():
        lands = _scatter_stage(arrays, name="scatter_" + kind)
        for (nm, idx), land in zip(stages[kind, layer].values(), lands):
            three_d = lambda a: a.reshape((a.shape[0],) + land.shape[1:])
            per_layer.setdefault(nm, {})[idx] = _adamw_big(land, three_d(wts[nm]), three_d(mom[nm]), three_d(var[nm]), idx,
                                                           name="adamw_" + nm)
    for nm, by_idx in per_layer.items():
        outs = [by_idx[i] for i in range(len(by_idx))]
        results[nm] = [(outs[0][j] if len(outs) == 1 else jnp.stack([o[j] for o in outs])).reshape(wts[nm].shape)
                       for j in range(4)]
    out = [loss, dx.reshape(bsz, seq, d)]
    for j in range(4):
        out += [results[n][j] for n in _NAMES]
    return tuple(out)
```

```python
import functools
import math

import jax
import jax.numpy as jnp
import numpy as np
from jax import lax
from jax.experimental import pallas as pl
from jax.experimental.pallas import tpu as pltpu

F32 = jnp.float32
BF16 = jnp.bfloat16
EPS = 1e-6
N_DEV = 8
LANES = 128
SUBLANES_BF16 = 16
HALO = 32
GLA_HEADS, GLA_RANK, GLA_GATE_NORM, GLA_CHUNK = 4, 16, 16.0, 64
SGU_CHUNK, SGU_GROUPS = 128, 8
HGRN_EXPAND, HGRN_CHUNK = 128, 64
CONV_WIDTH, FFN_CONV_WIDTH = 31, 3
ADAM_LR, ADAM_B1, ADAM_B2, ADAM_EPS, ADAM_WD, ADAM_STEP = 0.001, 0.9, 0.999, 1e-08, 0.01, 10
MESH = pl.DeviceIdType.MESH


def _sigmoid(x):
    return 0.5 * (jnp.tanh(0.5 * x) + 1.0)


def _silu(x):
    return x * _sigmoid(x)


def _log_sigmoid(x):
    return jnp.minimum(x, 0.0) - jnp.log(1.0 + jnp.exp(-jnp.abs(x)))


def _gelu(x):
    return 0.5 * x * (1.0 + jnp.tanh(math.sqrt(2.0 / math.pi) * (x + 0.044715 * (x * x * x))))


def _rms(x, g):
    return x * lax.rsqrt(jnp.mean(x * x, axis=-1, keepdims=True) + EPS) * g


def _layer_norm(x, g, b):
    xc = x - jnp.mean(x, axis=-1, keepdims=True)
    return xc * lax.rsqrt(jnp.mean(xc * xc, axis=-1, keepdims=True) + EPS) * g + b


def _dot_raw(a, b, dims):
    return lax.dot_general(a.astype(BF16), b.astype(BF16), (dims, ((), ())), preferred_element_type=F32)


@jax.custom_vjp
def _bdot(a, b):
    return _dot_raw(a, b, ((1,), (0,)))


@jax.custom_vjp
def _bdot_nt(a, b):
    return _dot_raw(a, b, ((1,), (1,)))


@jax.custom_vjp
def _bdot_tn(a, b):
    return _dot_raw(a, b, ((0,), (0,)))


_bdot.defvjp(lambda a, b: (_bdot(a, b), (a, b)), lambda r, g: (_bdot_nt(g, r[1]), _bdot_tn(r[0], g)))
_bdot_nt.defvjp(lambda a, b: (_bdot_nt(a, b), (a, b)), lambda r, g: (_bdot(g, r[1]), _bdot_tn(g, r[0])))
_bdot_tn.defvjp(lambda a, b: (_bdot_tn(a, b), (a, b)), lambda r, g: (_bdot_nt(r[1], g), _bdot(r[0], g)))


def _hdot(a, b):
    return jnp.dot(a, b, precision=lax.Precision.HIGHEST, preferred_element_type=F32)


def _divisor_tile(n, cap, unit):
    if n <= cap:
        return n
    best = None
    for t in range(unit, cap + 1, unit):
        if n % t == 0:
            best = t
    assert best is not None, (n, cap, unit)
    return best


def _const_map(nd):
    return lambda *_: (0,) * nd


def _dep_specs(deps, grid_rank):
    return [pl.BlockSpec(d.shape, (lambda *_, nd=d.ndim: (0,) * nd)) for d in deps]


def _mm(a, b, *, add=None, bias=None, out_dtype=F32, name, deps=()):
    m, k = a.shape
    k2, n = b.shape
    assert k == k2
    tn = _divisor_tile(n, max(LANES, min(1408, (6 << 20) // (2 * k) // LANES * LANES)), LANES)
    tm = _divisor_tile(m, max(256, min(1024, (4 << 20) // (a.dtype.itemsize * k) // 256 * 256)), 8)
    has_bias, has_add = bias is not None, add is not None

    def body(*refs):
        a_ref, b_ref = refs[0], refs[1]
        o_ref = refs[-1]
        acc = jnp.dot(a_ref[...].astype(BF16), b_ref[...], preferred_element_type=F32)
        pos = 2
        if has_bias:
            acc = acc + refs[pos][...]
            pos += 1
        if has_add:
            acc = acc + refs[pos][...].astype(F32)
        o_ref[...] = acc.astype(o_ref.dtype)

    in_specs = [pl.BlockSpec((tm, k), lambda i, j: (i, 0)), pl.BlockSpec((k, tn), lambda i, j: (0, j))]
    args = [a, b]
    if has_bias:
        in_specs.append(pl.BlockSpec((1, tn), lambda i, j: (0, j)))
        args.append(bias)
    if has_add:
        in_specs.append(pl.BlockSpec((tm, tn), lambda i, j: (i, j)))
        args.append(add)
    in_specs += _dep_specs(deps, 2)
    args += list(deps)
    return pl.pallas_call(
        body, name=name, grid=(m // tm, n // tn), in_specs=in_specs,
        out_specs=pl.BlockSpec((tm, tn), lambda i, j: (i, j)),
        out_shape=jax.ShapeDtypeStruct((m, n), out_dtype),
        compiler_params=pltpu.CompilerParams(dimension_semantics=("parallel", "parallel")),
    )(*args)


def _mm_tn(a, g, *, name):
    m, k = a.shape
    m2, n = g.shape
    assert m == m2
    tk = _divisor_tile(k, 1408, LANES)
    tn = _divisor_tile(n, 1408, LANES)
    tm = _divisor_tile(m, 1024, 8)

    def body(a_ref, g_ref, o_ref):
        @pl.when(pl.program_id(2) == 0)
        def _():
            o_ref[...] = jnp.zeros_like(o_ref)

        o_ref[...] += _dot_raw(a_ref[...], g_ref[...], ((0,), (0,)))

    return pl.pallas_call(
        body, name=name, grid=(k // tk, n // tn, m // tm),
        in_specs=[pl.BlockSpec((tm, tk), lambda i, j, t: (t, i)), pl.BlockSpec((tm, tn), lambda i, j, t: (t, j))],
        out_specs=pl.BlockSpec((tk, tn), lambda i, j, t: (i, j)),
        out_shape=jax.ShapeDtypeStruct((k, n), F32),
        compiler_params=pltpu.CompilerParams(dimension_semantics=("parallel", "parallel", "arbitrary")),
    )(a, g)


def _tile_call(name, fn, tiled, params, out_tiled, out_acc, tile, deps=()):
    tiled = [t if isinstance(t, tuple) else (t, t.shape[1], 0) for t in tiled]
    t_rows = tiled[0][0].shape[0]
    tile = min(tile, t_rows)
    assert t_rows % tile == 0
    n_t, n_p, n_o, n_d = len(tiled), len(params), len(out_tiled), len(deps)

    def body(*refs):
        vals = [r[...] for r in refs[: n_t + n_p]]
        refs = refs[: n_t + n_p] + refs[n_t + n_p + n_d:]
        touts, aouts = fn(*vals)
        for r, v in zip(refs[n_t + n_p: n_t + n_p + n_o], touts):
            r[...] = v.astype(r.dtype)
        acc_refs = refs[n_t + n_p + n_o:]
        if acc_refs:
            @pl.when(pl.program_id(0) == 0)
            def _():
                for r in acc_refs:
                    r[...] = jnp.zeros_like(r)

            for r, v in zip(acc_refs, aouts):
                r[...] += v

    in_specs = [pl.BlockSpec((tile, w), lambda i, cb=cb: (i, cb)) for _, w, cb in tiled]
    in_specs += [pl.BlockSpec(p.shape, _const_map(p.ndim)) for p in params]
    in_specs += _dep_specs(deps, 1)
    out_specs = [pl.BlockSpec((tile, w), lambda i: (i, 0)) for w, _ in out_tiled]
    out_specs += [pl.BlockSpec(s, _const_map(len(s))) for s in out_acc]
    out_shape = [jax.ShapeDtypeStruct((t_rows, w), dt) for w, dt in out_tiled]
    out_shape += [jax.ShapeDtypeStruct(s, F32) for s in out_acc]
    res = pl.pallas_call(
        body, name=name, grid=(t_rows // tile,), in_specs=in_specs, out_specs=out_specs, out_shape=out_shape,
        compiler_params=pltpu.CompilerParams(dimension_semantics=("arbitrary" if out_acc else "parallel",)),
    )(*[t[0] for t in tiled], *params, *deps)
    return res[:n_o], res[n_o:]


def _rms_fwd(x, g, name, deps=()):
    (h,), _ = _tile_call(name, lambda xv, gv: ([_rms(xv, gv)], []), [x], [g], [(x.shape[1], BF16)], [], 512, deps)
    return h


def _rms_bwd(x, g, dh, dres, name):
    def fn(xv, dhv, drv, gv):
        _, vjp = jax.vjp(_rms, xv, gv)
        dx, dg = vjp(dhv.astype(F32))
        return [drv + dx], [dg]

    (dx,), (dg,) = _tile_call(name, fn, [x, dh, dres], [g], [(x.shape[1], F32)], [g.shape], 512)
    return dx, dg


def _colsum(x, name):
    _, (s,) = _tile_call(name, lambda xv: ([], [jnp.sum(xv.astype(F32), axis=0, keepdims=True)]), [x], [], [],
                         [(1, x.shape[1])], 512)
    return s


def _seq_flags(i, tiles_per_seq):
    pos = i % tiles_per_seq
    return pos == 0, pos == tiles_per_seq - 1


def _dwconv_fwd(x, w, b, seq, name):
    t_rows, ch = x.shape
    kw = w.shape[0]
    tile = min(512, seq)
    cb = _divisor_tile(ch, 256, LANES)
    tps, hb = seq // tile, tile // HALO

    def body(x_ref, halo_ref, w_ref, b_ref, y_ref, pad_ref):
        first, _ = _seq_flags(pl.program_id(0), tps)
        pad_ref[0:HALO, :] = jnp.where(first, 0.0, halo_ref[...])
        pad_ref[HALO:HALO + tile, :] = x_ref[...]
        acc = jnp.broadcast_to(b_ref[...], (tile, cb))
        for k in range(kw):
            acc = acc + pad_ref[pl.ds(HALO - (kw - 1) + k, tile), :] * w_ref[k:k + 1, :]
        y_ref[...] = acc

    return pl.pallas_call(
        body, name=name, grid=(t_rows // tile, ch // cb),
        in_specs=[pl.BlockSpec((tile, cb), lambda i, j: (i, j)),
                  pl.BlockSpec((HALO, cb), lambda i, j: (jnp.maximum(i * hb - 1, 0), j)),
                  pl.BlockSpec((kw, cb), lambda i, j: (0, j)), pl.BlockSpec((1, cb), lambda i, j: (0, j))],
        out_specs=pl.BlockSpec((tile, cb), lambda i, j: (i, j)),
        out_shape=jax.ShapeDtypeStruct((t_rows, ch), F32),
        scratch_shapes=[pltpu.VMEM((HALO + tile, cb), F32)],
        compiler_params=pltpu.CompilerParams(dimension_semantics=("parallel", "parallel")),
    )(x, x, w, b)


def _dwconv_bwd(x, dy, w, seq, name):
    t_rows, ch = x.shape
    kw = w.shape[0]
    tile = min(512, seq)
    cb = _divisor_tile(ch, 256, LANES)
    tps, hb, n_hb = seq // tile, tile // HALO, t_rows // HALO

    def body(x_ref, xh_ref, dy_ref, dyh_ref, w_ref, dx_ref, dw_ref, db_ref, xpad, dypad):
        i = pl.program_id(1)
        first, last = _seq_flags(i, tps)

        @pl.when(i == 0)
        def _():
            dw_ref[...] = jnp.zeros_like(dw_ref)
            db_ref[...] = jnp.zeros_like(db_ref)

        xpad[0:HALO, :] = jnp.where(first, 0.0, xh_ref[...])
        xpad[HALO:HALO + tile, :] = x_ref[...]
        dyv = dy_ref[...]
        dypad[0:tile, :] = dyv
        dypad[tile:tile + HALO, :] = jnp.where(last, 0.0, dyh_ref[...])
        acc = jnp.zeros((tile, cb), F32)
        for k in range(kw):
            acc = acc + dypad[pl.ds(kw - 1 - k, tile), :] * w_ref[k:k + 1, :]
            dw_ref[k:k + 1, :] += jnp.sum(dyv * xpad[pl.ds(HALO - (kw - 1) + k, tile), :], axis=0, keepdims=True)
        dx_ref[...] = acc
        db_ref[...] += jnp.sum(dyv, axis=0, keepdims=True)

    return pl.pallas_call(
        body, name=name, grid=(ch // cb, t_rows // tile),
        in_specs=[pl.BlockSpec((tile, cb), lambda j, i: (i, j)),
                  pl.BlockSpec((HALO, cb), lambda j, i: (jnp.maximum(i * hb - 1, 0), j)),
                  pl.BlockSpec((tile, cb), lambda j, i: (i, j)),
                  pl.BlockSpec((HALO, cb), lambda j, i: (jnp.minimum((i + 1) * hb, n_hb - 1), j)),
                  pl.BlockSpec((kw, cb), lambda j, i: (0, j))],
        out_specs=[pl.BlockSpec((tile, cb), lambda j, i: (i, j)), pl.BlockSpec((kw, cb), lambda j, i: (0, j)),
                   pl.BlockSpec((1, cb), lambda j, i: (0, j))],
        out_shape=[jax.ShapeDtypeStruct((t_rows, ch), F32), jax.ShapeDtypeStruct((kw, ch), F32),
                   jax.ShapeDtypeStruct((1, ch), F32)],
        scratch_shapes=[pltpu.VMEM((HALO + tile, cb), F32), pltpu.VMEM((tile + HALO, cb), F32)],
        compiler_params=pltpu.CompilerParams(dimension_semantics=("parallel", "arbitrary")),
    )(x, x, dy, dy, w)


def _ffn_mid_fwd(u, w, seq, name):
    t_rows, f2 = u.shape
    f = f2 // 2
    tile = min(256, seq)
    cb = _divisor_tile(f, 1408, LANES)
    nj, tps, hb, hl = f // cb, seq // tile, tile // SUBLANES_BF16, SUBLANES_BF16

    def body(ug_ref, uv_ref, hg_ref, hv_ref, wg_ref, wv_ref, a_ref, gpad, vpad):
        first, _ = _seq_flags(pl.program_id(0), tps)

        def conv(t_ref, h_ref, w_ref, pad):
            pad[0:hl, :] = jnp.where(first, 0.0, h_ref[...].astype(F32))
            pad[hl:hl + tile, :] = t_ref[...].astype(F32)
            z = pad[pl.ds(hl, tile), :] * w_ref[2:3, :]
            z = z + pad[pl.ds(hl - 1, tile), :] * w_ref[1:2, :]
            return z + pad[pl.ds(hl - 2, tile), :] * w_ref[0:1, :]

        zg = conv(ug_ref, hg_ref, wg_ref, gpad)
        zv = conv(uv_ref, hv_ref, wv_ref, vpad)
        a_ref[...] = (_silu(zg) * zv).astype(a_ref.dtype)

    halo_map = lambda off: (lambda i, j: (jnp.maximum(i * hb - 1, 0), j + off))
    return pl.pallas_call(
        body, name=name, grid=(t_rows // tile, nj),
        in_specs=[pl.BlockSpec((tile, cb), lambda i, j: (i, j)), pl.BlockSpec((tile, cb), lambda i, j: (i, j + nj)),
                  pl.BlockSpec((hl, cb), halo_map(0)), pl.BlockSpec((hl, cb), halo_map(nj)),
                  pl.BlockSpec((3, cb), lambda i, j: (0, j)), pl.BlockSpec((3, cb), lambda i, j: (0, j + nj))],
        out_specs=pl.BlockSpec((tile, cb), lambda i, j: (i, j)),
        out_shape=jax.ShapeDtypeStruct((t_rows, f), BF16),
        scratch_shapes=[pltpu.VMEM((hl + tile, cb), F32), pltpu.VMEM((hl + tile, cb), F32)],
        compiler_params=pltpu.CompilerParams(dimension_semantics=("parallel", "parallel")),
    )(u, u, u, u, w, w)


def _ffn_mid_bwd(u, da, w, seq, name):
    t_rows, f2 = u.shape
    f = f2 // 2
    tile = min(256, seq)
    cb = _divisor_tile(f, 1408, LANES)
    hl = SUBLANES_BF16
    nj, tps, hb, n_hb, ext = f // cb, seq // tile, tile // hl, t_rows // hl, tile + hl

    def body(ug_ref, uv_ref, pg_ref, pv_ref, ng_ref, nv_ref, da_ref, dan_ref, wg_ref, wv_ref,
             dug_ref, duv_ref, dwg_ref, dwv_ref, gpad, vpad, dzg, dzv):
        i = pl.program_id(1)
        first, last = _seq_flags(i, tps)

        @pl.when(i == 0)
        def _():
            dwg_ref[...] = jnp.zeros_like(dwg_ref)
            dwv_ref[...] = jnp.zeros_like(dwv_ref)

        def conv(t_ref, p_ref, n_ref, w_ref, pad):
            pad[0:hl, :] = jnp.where(first, 0.0, p_ref[...].astype(F32))
            pad[hl:hl + tile, :] = t_ref[...].astype(F32)
            pad[hl + tile:hl + ext, :] = jnp.where(last, 0.0, n_ref[...].astype(F32))
            z = pad[pl.ds(hl, ext), :] * w_ref[2:3, :]
            z = z + pad[pl.ds(hl - 1, ext), :] * w_ref[1:2, :]
            return z + pad[pl.ds(hl - 2, ext), :] * w_ref[0:1, :]

        zg = conv(ug_ref, pg_ref, ng_ref, wg_ref, gpad)
        zv = conv(uv_ref, pv_ref, nv_ref, wv_ref, vpad)
        da_ext = jnp.concatenate(
            [da_ref[...].astype(F32), jnp.where(last, 0.0, dan_ref[...].astype(F32))], axis=0)
        sg = _sigmoid(zg)
        dzg[...] = da_ext * zv * (sg * (1.0 + zg * (1.0 - sg)))
        dzv[...] = da_ext * (zg * sg)

        def back(dz, w_ref, pad, du_ref, dw_ref):
            du = dz[pl.ds(2, tile), :] * w_ref[0:1, :]
            du = du + dz[pl.ds(1, tile), :] * w_ref[1:2, :]
            du_ref[...] = (du + dz[pl.ds(0, tile), :] * w_ref[2:3, :]).astype(du_ref.dtype)
            dzt = dz[pl.ds(0, tile), :]
            for k in range(3):
                dw_ref[k:k + 1, :] += jnp.sum(dzt * pad[pl.ds(hl - 2 + k, tile), :], axis=0, keepdims=True)

        back(dzg, wg_ref, gpad, dug_ref, dwg_ref)
        back(dzv, wv_ref, vpad, duv_ref, dwv_ref)

    prev_map = lambda off: (lambda j, i: (jnp.maximum(i * hb - 1, 0), j + off))
    next_map = lambda off: (lambda j, i: (jnp.minimum((i + 1) * hb, n_hb - 1), j + off))
    tile_spec = lambda off: pl.BlockSpec((tile, cb), lambda j, i: (i, j + off))
    w_spec = lambda off: pl.BlockSpec((3, cb), lambda j, i: (0, j + off))
    return pl.pallas_call(
        body, name=name, grid=(nj, t_rows // tile),
        in_specs=[tile_spec(0), tile_spec(nj), pl.BlockSpec((hl, cb), prev_map(0)), pl.BlockSpec((hl, cb), prev_map(nj)),
                  pl.BlockSpec((hl, cb), next_map(0)), pl.BlockSpec((hl, cb), next_map(nj)),
                  tile_spec(0), pl.BlockSpec((hl, cb), next_map(0)), w_spec(0), w_spec(nj)],
        out_specs=[tile_spec(0), tile_spec(0), w_spec(0), w_spec(0)],
        out_shape=[jax.ShapeDtypeStruct((t_rows, f), BF16), jax.ShapeDtypeStruct((t_rows, f), BF16),
                   jax.ShapeDtypeStruct((3, f), F32), jax.ShapeDtypeStruct((3, f), F32)],
        scratch_shapes=[pltpu.VMEM((hl + ext, cb), F32), pltpu.VMEM((hl + ext, cb), F32),
                        pltpu.VMEM((ext, cb), F32), pltpu.VMEM((ext, cb), F32)],
        compiler_params=pltpu.CompilerParams(dimension_semantics=("parallel", "arbitrary")),
    )(u, u, u, u, u, u, da, da, w, w)


def _gla_chunk(q, k, v, lg, st, *, scale, chunk):
    row = lax.broadcasted_iota(jnp.int32, (chunk, chunk), 0)
    col = lax.broadcasted_iota(jnp.int32, (chunk, chunk), 1)
    causal = col <= row
    b = _hdot(causal.astype(F32), lg)
    upto_mid = lax.broadcasted_iota(jnp.int32, lg.shape, 0) <= chunk // 2
    b_mid = jnp.sum(jnp.where(upto_mid, lg, 0.0), axis=0, keepdims=True)
    b_last = jnp.sum(lg, axis=0, keepdims=True)
    qs = q * scale
    scores = _bdot_nt(qs * jnp.exp(b - b_mid), k * jnp.exp(b_mid - b))
    o = _bdot(jnp.where(causal, scores, 0.0), v)
    o = o + _bdot_nt(qs * jnp.exp(b), st)
    st_new = st * jnp.exp(b_last) + _bdot_tn(v, k * jnp.exp(b_last - b))
    return o, st_new


def _gla_specs(specs, chunk, n_chunks, reverse):
    if reverse:
        row = lambda bi, ci: bi * n_chunks + (n_chunks - 1 - ci)
    else:
        row = lambda bi, ci: bi * n_chunks + ci
    return [pl.BlockSpec((chunk, w), lambda bi, ci, cb=cb: (row(bi, ci), cb)) for _, w, cb in specs], row


def _gla_fwd(q, k, v, lg, *, heads, dk, dv, scale, chunk, seq, name):
    t_rows = q[0].shape[0]
    n_chunks = seq // chunk
    fn = functools.partial(_gla_chunk, scale=scale, chunk=chunk)

    def body(q_ref, k_ref, v_ref, lg_ref, o_ref, sts_ref, st_ref):
        @pl.when(pl.program_id(1) == 0)
        def _():
            st_ref[...] = jnp.zeros_like(st_ref)

        sts_ref[0] = st_ref[...]
        for h in range(heads):
            ks, vs = slice(h * dk, (h + 1) * dk), slice(h * dv, (h + 1) * dv)
            o, st = fn(q_ref[:, ks].astype(F32), k_ref[:, ks].astype(F32), v_ref[:, vs].astype(F32),
                       lg_ref[:, ks], st_ref[vs, :])
            o_ref[:, vs] = o
            st_ref[vs, :] = st

    in_specs, row = _gla_specs([q, k, v, lg], chunk, n_chunks, False)
    return pl.pallas_call(
        body, name=name, grid=(t_rows // seq, n_chunks), in_specs=in_specs,
        out_specs=[pl.BlockSpec((chunk, heads * dv), lambda bi, ci: (row(bi, ci), 0)),
                   pl.BlockSpec((1, heads * dv, dk), lambda bi, ci: (row(bi, ci), 0, 0))],
        out_shape=[jax.ShapeDtypeStruct((t_rows, heads * dv), F32),
                   jax.ShapeDtypeStruct((t_rows // chunk, heads * dv, dk), F32)],
        scratch_shapes=[pltpu.VMEM((heads * dv, dk), F32)],
        compiler_params=pltpu.CompilerParams(dimension_semantics=("arbitrary", "arbitrary")),
    )(q[0], k[0], v[0], lg[0])


def _gla_bwd(q, k, v, lg, states, do, *, heads, dk, dv, scale, chunk, seq, out_dtypes, name):
    t_rows = q[0].shape[0]
    n_chunks = seq // chunk
    fn = functools.partial(_gla_chunk, scale=scale, chunk=chunk)

    def body(q_ref, k_ref, v_ref, lg_ref, do_ref, sts_ref, dq_ref, dk_ref, dv_ref, dlg_ref, dst_ref):
        @pl.when(pl.program_id(1) == 0)
        def _():
            dst_ref[...] = jnp.zeros_like(dst_ref)

        for h in range(heads):
            ks, vs = slice(h * dk, (h + 1) * dk), slice(h * dv, (h + 1) * dv)
            _, vjp = jax.vjp(fn, q_ref[:, ks].astype(F32), k_ref[:, ks].astype(F32), v_ref[:, vs].astype(F32),
                             lg_ref[:, ks], sts_ref[0, vs, :])
            dq, dkk, dvv, dlg, dst = vjp((do_ref[:, vs].astype(F32), dst_ref[vs, :]))
            dq_ref[:, ks] = dq.astype(dq_ref.dtype)
            dk_ref[:, ks] = dkk.astype(dk_ref.dtype)
            dv_ref[:, vs] = dvv.astype(dv_ref.dtype)
            dlg_ref[:, ks] = dlg
            dst_ref[vs, :] = dst

    do_view = (do, heads * dv, 0)
    in_specs, row = _gla_specs([q, k, v, lg, do_view], chunk, n_chunks, True)
    in_specs.append(pl.BlockSpec((1, heads * dv, dk), lambda bi, ci: (row(bi, ci), 0, 0)))
    wide = lambda w: pl.BlockSpec((chunk, w), lambda bi, ci: (row(bi, ci), 0))
    return pl.pallas_call(
        body, name=name, grid=(t_rows // seq, n_chunks), in_specs=in_specs,
        out_specs=[wide(heads * dk), wide(heads * dk), wide(heads * dv), wide(heads * dk)],
        out_shape=[jax.ShapeDtypeStruct((t_rows, heads * dk), out_dtypes[0]),
                   jax.ShapeDtypeStruct((t_rows, heads * dk), out_dtypes[1]),
                   jax.ShapeDtypeStruct((t_rows, heads * dv), out_dtypes[2]),
                   jax.ShapeDtypeStruct((t_rows, heads * dk), F32)],
        scratch_shapes=[pltpu.VMEM((heads * dv, dk), F32)],
        compiler_params=pltpu.CompilerParams(dimension_semantics=("arbitrary", "arbitrary")),
    )(q[0], k[0], v[0], lg[0], do, states)


def _head_rms_gate(o, r, g, heads):
    d = o.shape[1] // heads
    parts = [_rms(o[:, h * d:(h + 1) * d], g) for h in range(heads)]
    return jnp.concatenate(parts, axis=1) * _silu(r)


def _gla_gate(glr, w_g2p, b_g2):
    return _log_sigmoid(_bdot(glr, w_g2p) + b_g2) * (1.0 / GLA_GATE_NORM)


def _glu(a, gate, b_in):
    d = a.shape[1]
    return (a + b_in[:, :d]) * _sigmoid(gate + b_in[:, d:])


def _ln_silu(y, g, b):
    return _silu(_layer_norm(y, g, b))


def _sgu(pre, b_in, ln_g, ln_b, w_s, b_st):
    d = pre.shape[1] // 2
    gd = d // SGU_GROUPS
    uv = _gelu(pre + b_in)
    u, v = uv[:, :d], _layer_norm(uv[:, d:], ln_g, ln_b)
    row = lax.broadcasted_iota(jnp.int32, (SGU_CHUNK, SGU_CHUNK), 0)
    col = lax.broadcasted_iota(jnp.int32, (SGU_CHUNK, SGU_CHUNK), 1)
    lane = lax.broadcasted_iota(jnp.int32, b_st.shape, 1)
    rows = []
    for c in range(pre.shape[0] // SGU_CHUNK):
        rs = slice(c * SGU_CHUNK, (c + 1) * SGU_CHUNK)
        parts = []
        for g in range(SGU_GROUPS):
            wg = jnp.where(col <= row, w_s[g], 0.0)
            bias = jnp.sum(jnp.where(lane == g, b_st, 0.0), axis=1, keepdims=True)
            parts.append(_bdot(wg, v[rs, g * gd:(g + 1) * gd]) + bias)
        rows.append(jnp.concatenate(parts, axis=1))
    s = rows[0] if len(rows) == 1 else jnp.concatenate(rows, axis=0)
    return u * s


def _hgrn_pre(q, f, table, layer):
    t = table - jnp.max(table, axis=0, keepdims=True)
    e = jnp.exp(t)
    sm = e / jnp.sum(e, axis=0, keepdims=True)
    rows = lax.broadcasted_iota(jnp.int32, table.shape, 0)
    lb = jnp.sum(jnp.where((rows >= 1) & (rows <= layer), sm, 0.0), axis=0, keepdims=True)
    sf = _sigmoid(f)
    return _silu(q), (1.0 - lb) * (1.0 - sf), jnp.log(lb + (1.0 - lb) * sf)


def _ffn_fwd(x, w, seq, sv):
    sv["h2"] = _rms_fwd(x, w["norm"], "ffn_norm")
    sv["u"] = _mm(sv["h2"], w["w_up"], out_dtype=BF16, name="ffn_up")
    sv["a"] = _ffn_mid_fwd(sv["u"], w["w_dw"], seq, "ffn_mid")
    return _mm(sv["a"], w["w_down"], add=x, name="ffn_down")


def _ffn_bwd(x, dy, w, seq, sv, deps=()):
    g = {}
    da = _mm(dy, w["w_down_t"], out_dtype=BF16, name="ffn_down_dx", deps=deps)
    g["w_down"] = _mm_tn(sv["a"], dy, name="ffn_down_dw")
    dug, duv, dwg, dwv = _ffn_mid_bwd(sv["u"], da, w["w_dw"], seq, "ffn_mid_bwd")
    g["w_dw"] = jnp.concatenate([dwg, dwv], axis=1)
    g["w_up_gate"] = _mm_tn(sv["h2"], dug, name="ffn_up_dw")
    g["w_up_val"] = _mm_tn(sv["h2"], duv, name="ffn_up_dw")
    dh = _mm(dug, w["w_up_t_gate"], out_dtype=F32, name="ffn_up_dx")
    dh = _mm(duv, w["w_up_t_val"], add=dh, out_dtype=BF16, name="ffn_up_dx2")
    dx, g["norm"] = _rms_bwd(x, w["norm"], dh, dy, "ffn_norm_bwd")
    return dx, g


def _gla_layer_fwd(x, h, w, seq, sv):
    d = x.shape[1]
    dkt = d // 2
    dk, dv = dkt // GLA_HEADS, d // GLA_HEADS
    proj = _mm(h, w["w_main"], out_dtype=F32, name="gla_in")
    glr = _mm(h, w["w_glr"], out_dtype=BF16, name="gla_in_g")
    (lg,), _ = _tile_call("gla_gate", lambda a, b, c: ([_gla_gate(a.astype(F32), b, c)], []), [glr],
                          [w["w_g2p"], w["b_g2"]], [(dkt, F32)], [], 512)
    q, k, v, r = (proj, dkt, 0), (proj, dkt, 1), (proj, d, 1), (proj, d, 2)
    o, states = _gla_fwd(q, k, v, (lg, dkt, 0), heads=GLA_HEADS, dk=dk, dv=dv, scale=dk ** -0.5, chunk=GLA_CHUNK,
                         seq=seq, name="gla_core")
    (o2,), _ = _tile_call("gla_post", lambda ov, rv, gv: ([_head_rms_gate(ov, rv.astype(F32), gv, GLA_HEADS)], []),
                          [o, r], [w["norm"]], [(d, BF16)], [], 256)
    sv.update(proj=proj, glr=glr, lg=lg, o=o, states=states, o2=o2)
    return _mm(o2, w["w_out"], add=x, name="mix_out")


def _gla_layer_bwd(h, dy, w, seq, sv, deps=()):
    d = dy.shape[1]
    dkt = d // 2
    dk, dv = dkt // GLA_HEADS, d // GLA_HEADS
    proj, glr, lg, o = sv["proj"], sv["glr"], sv["lg"], sv["o"]
    g = {}
    do2 = _mm(dy, w["w_out_t"], out_dtype=F32, name="gla_out_dx", deps=deps)
    g["w_out"] = _mm_tn(sv["o2"], dy, name="mix_out_dw")

    def post_bwd(ov, rv, ctv, gv):
        _, vjp = jax.vjp(functools.partial(_head_rms_gate, heads=GLA_HEADS), ov, rv.astype(F32), gv)
        d_o, d_r, d_g = vjp(ctv.astype(F32))
        return [d_o, d_r], [d_g]

    (d_o, d_r), (g["norm"],) = _tile_call("gla_post_bwd", post_bwd, [o, (proj, d, 2), do2], [w["norm"]],
                                          [(d, F32), (d, BF16)], [w["norm"].shape], 256)
    q, k, v = (proj, dkt, 0), (proj, dkt, 1), (proj, d, 1)
    dq, dkk, dvv, dlg = _gla_bwd(q, k, v, (lg, dkt, 0), sv["states"], d_o, heads=GLA_HEADS, dk=dk, dv=dv,
                                 scale=dk ** -0.5, chunk=GLA_CHUNK, seq=seq, out_dtypes=(BF16, BF16, BF16),
                                 name="gla_core_bwd")

    def gate_bwd(glrv, ctv, wv, bv):
        _, vjp = jax.vjp(_gla_gate, glrv.astype(F32), wv, bv)
        d_glr, d_w, d_b = vjp(ctv)
        return [d_glr], [d_w, d_b]

    (dglr,), (g["w_g2p"], g["b_g2"]) = _tile_call("gla_gate_bwd", gate_bwd, [glr, dlg], [w["w_g2p"], w["b_g2"]],
                                                  [(LANES, BF16)], [w["w_g2p"].shape, w["b_g2"].shape], 512)
    dproj = jnp.concatenate([dq, dkk, dvv, d_r], axis=1)
    g["w_main"] = _mm_tn(h, dproj, name="gla_in_dw")
    g["w_glr"] = _mm_tn(h, dglr, name="gla_in_g_dw")
    dh = _mm(dproj, w["w_main_t"], out_dtype=F32, name="gla_in_dx")
    dh = _mm(dglr, w["w_glr_t"], add=dh, out_dtype=BF16, name="gla_in_g_dx")
    return dh, g


def _cv_layer_fwd(x, h, w, seq, sv):
    d = x.shape[1]
    pre = _mm(h, w["w_in"], out_dtype=BF16, name="cv_in")
    (y1,), _ = _tile_call("cv_glu", lambda a, gt, b: ([_glu(a.astype(F32), gt.astype(F32), b)], []),
                          [(pre, d, 0), (pre, d, 1)], [w["b_in"]], [(d, F32)], [], 512)
    y2 = _dwconv_fwd(y1, w["w_dw"], w["b_dw"], seq, "cv_conv")
    (y3,), _ = _tile_call("cv_ln", lambda y, a, b: ([_ln_silu(y, a, b)], []), [y2], [w["ln_g"], w["ln_b"]],
                          [(d, BF16)], [], 512)
    sv.update(pre=pre, y1=y1, y2=y2, y3=y3)
    return _mm(y3, w["w_out"], bias=w["b_out"], add=x, name="mix_out_b")


def _cv_layer_bwd(h, dy, w, seq, sv, deps=()):
    d = dy.shape[1]
    pre = sv["pre"]
    g = {}
    dy3 = _mm(dy, w["w_out_t"], out_dtype=BF16, name="mix_out_dx", deps=deps)
    g["w_out"] = _mm_tn(sv["y3"], dy, name="mix_out_dw")
    g["b_out"] = _colsum(dy, "bias_out_dw")

    def ln_bwd(yv, ctv, av, bv):
        _, vjp = jax.vjp(_ln_silu, yv, av, bv)
        d_y, d_a, d_b = vjp(ctv.astype(F32))
        return [d_y], [d_a, d_b]

    (dy2,), (g["ln_g"], g["ln_b"]) = _tile_call("cv_ln_bwd", ln_bwd, [sv["y2"], dy3], [w["ln_g"], w["ln_b"]],
                                                [(d, F32)], [w["ln_g"].shape, w["ln_b"].shape], 512)
    dy1, g["w_dw"], g["b_dw"] = _dwconv_bwd(sv["y1"], dy2, w["w_dw"], seq, "cv_conv_bwd")

    def glu_bwd(av, gv, ctv, bv):
        _, vjp = jax.vjp(_glu, av.astype(F32), gv.astype(F32), bv)
        d_a, d_g, d_b = vjp(ctv)
        return [jnp.concatenate([d_a, d_g], axis=1)], [d_b]

    (dpre,), (g["b_in"],) = _tile_call("cv_glu_bwd", glu_bwd, [(pre, d, 0), (pre, d, 1), dy1], [w["b_in"]],
                                       [(2 * d, BF16)], [w["b_in"].shape], 512)
    g["w_in"] = _mm_tn(h, dpre, name="in2_dw")
    dh = _mm(dpre, w["w_in_t"], out_dtype=BF16, name="in2_dx")
    return dh, g


def _sg_layer_fwd(x, h, w, seq, sv):
    d = x.shape[1]
    pre = _mm(h, w["w_in"], out_dtype=BF16, name="sg_in")
    pars = [w["b_in"], w["ln_g"], w["ln_b"], w["w_s"], w["b_st"]]
    (p,), _ = _tile_call("sg_gate", lambda pv, *ps: ([_sgu(pv.astype(F32), *ps)], []), [pre], pars, [(d, BF16)], [],
                         SGU_CHUNK)
    sv.update(pre=pre, p=p)
    return _mm(p, w["w_out"], bias=w["b_out"], add=x, name="mix_out_b")


def _sg_layer_bwd(h, dy, w, seq, sv, deps=()):
    d = dy.shape[1]
    g = {}
    dp = _mm(dy, w["w_out_t"], out_dtype=BF16, name="mix_out_dx", deps=deps)
    g["w_out"] = _mm_tn(sv["p"], dy, name="mix_out_dw")
    g["b_out"] = _colsum(dy, "bias_out_dw")
    pars = [w["b_in"], w["ln_g"], w["ln_b"], w["w_s"], w["b_st"]]

    def sgu_bwd(pv, ctv, *ps):
        _, vjp = jax.vjp(_sgu, pv.astype(F32), *ps)
        grads = vjp(ctv.astype(F32))
        return [grads[0]], list(grads[1:])

    (dpre,), (g["b_in"], g["ln_g"], g["ln_b"], g["w_s"], g["b_st"]) = _tile_call(
        "sg_gate_bwd", sgu_bwd, [sv["pre"], dp], pars, [(2 * d, BF16)], [p.shape for p in pars], SGU_CHUNK)
    g["w_in"] = _mm_tn(h, dpre, name="in2_dw")
    dh = _mm(dpre, w["w_in_t"], out_dtype=BF16, name="in2_dx")
    return dh, g


def _hg_layer_fwd(x, h, w, seq, sv, layer):
    d = x.shape[1]
    heads = d // HGRN_EXPAND
    proj = _mm(h, w["w_in"], out_dtype=BF16, name="hg_in")
    pre = functools.partial(_hgrn_pre, layer=layer)
    (qs, kk, lg), _ = _tile_call("hg_pre", lambda qv, fv, tb: (list(pre(qv.astype(F32), fv.astype(F32), tb)), []),
                                 [(proj, d, 0), (proj, d, 1)], [w["lb_table"]], [(d, BF16), (d, F32), (d, F32)], [], 256)
    o, states = _gla_fwd((qs, d, 0), (kk, d, 0), (proj, d, 2), (lg, d, 0), heads=heads, dk=HGRN_EXPAND,
                         dv=HGRN_EXPAND, scale=1.0, chunk=HGRN_CHUNK, seq=seq, name="hg_core")
    (o2,), _ = _tile_call("hg_post", lambda ov, gv, nv: ([_head_rms_gate(ov, gv.astype(F32), nv, heads)], []),
                          [o, (proj, d, 3)], [w["norm"]], [(d, BF16)], [], 256)
    sv.update(proj=proj, qs=qs, kk=kk, lg=lg, o=o, states=states, o2=o2)
    return _mm(o2, w["w_out"], add=x, name="mix_out")


def _hg_layer_bwd(h, dy, w, seq, sv, layer, deps=()):
    d = dy.shape[1]
    heads = d // HGRN_EXPAND
    proj = sv["proj"]
    g = {}
    do2 = _mm(dy, w["w_out_t"], out_dtype=BF16, name="mix_out_dx", deps=deps)
    g["w_out"] = _mm_tn(sv["o2"], dy, name="mix_out_dw")

    def post_bwd(ov, gv, ctv, nv):
        _, vjp = jax.vjp(functools.partial(_head_rms_gate, heads=heads), ov, gv.astype(F32), nv)
        d_o, d_g, d_n = vjp(ctv.astype(F32))
        return [d_o, d_g], [d_n]

    (d_o, d_gate), (g["norm"],) = _tile_call("hg_post_bwd", post_bwd, [sv["o"], (proj, d, 3), do2], [w["norm"]],
                                             [(d, F32), (d, BF16)], [w["norm"].shape], 256)
    dqs, dkk, di, dlg = _gla_bwd((sv["qs"], d, 0), (sv["kk"], d, 0), (proj, d, 2), (sv["lg"], d, 0), sv["states"], d_o,
                                 heads=heads, dk=HGRN_EXPAND, dv=HGRN_EXPAND, scale=1.0, chunk=HGRN_CHUNK, seq=seq,
                                 out_dtypes=(F32, F32, BF16), name="hg_core_bwd")

    def pre_bwd(qv, fv, c1, c2, c3, tb):
        _, vjp = jax.vjp(functools.partial(_hgrn_pre, layer=layer), qv.astype(F32), fv.astype(F32), tb)
        d_q, d_f, d_t = vjp((c1, c2, c3))
        return [jnp.concatenate([d_q, d_f], axis=1)], [d_t]

    (dqf,), (g["lb_table"],) = _tile_call("hg_pre_bwd", pre_bwd, [(proj, d, 0), (proj, d, 1), dqs, dkk, dlg],
                                          [w["lb_table"]], [(2 * d, BF16)], [w["lb_table"].shape], 256)
    dproj = jnp.concatenate([dqf, di, d_gate], axis=1)
    g["w_in"] = _mm_tn(h, dproj, name="hg_in_dw")
    dh = _mm(dproj, w["w_in_t"], out_dtype=BF16, name="hg_in_dx")
    return dh, g


_MIXERS = ("gla", "cv", "sg", "hg")


_BIG_KEYS = {"gla": ("w_main", "w_glr", "w_out"), "cv": ("w_in", "w_out"), "sg": ("w_in", "w_out"), "hg": ("w_in", "w_out"),
             "ffn": ("w_up_gate", "w_up_val", "w_down")}


def _local_step(x, target, w, seq, get_big, put_big, deps=()):
    depth = w["norm_mix"].shape[0]
    d = x.shape[1]
    saved, big = [], {}
    for layer in range(depth):
        mixer = _MIXERS[layer % 4]
        sv = {"x_in": x}
        sv["h"] = _rms_fwd(x, w["norm_mix"][layer:layer + 1], "mix_norm", deps if layer == 0 else ())
        big[mixer, layer] = get_big(mixer, layer, sv["h"])
        wm = dict(w[mixer], **big[mixer, layer])
        if mixer == "gla":
            x = _gla_layer_fwd(x, sv["h"], wm, seq, sv)
        elif mixer == "cv":
            x = _cv_layer_fwd(x, sv["h"], wm, seq, sv)
        elif mixer == "sg":
            x = _sg_layer_fwd(x, sv["h"], wm, seq, sv)
        else:
            x = _hg_layer_fwd(x, sv["h"], wm, seq, sv, layer)
        sv["x_mid"] = x
        sv["ffn"] = {}
        big["ffn", layer] = get_big("ffn", layer, x)
        wf = dict(w["ffn"][layer], norm=w["norm_ffn"][layer:layer + 1], **big["ffn", layer])
        x = _ffn_fwd(x, wf, seq, sv["ffn"])
        saved.append(sv)

    def head(xv, tv, gv):
        y, vjp = jax.vjp(_rms, xv, gv)
        err = y - tv
        dx, dg = vjp(err * (1.0 / d))
        part = 0.5 * jnp.sum(jnp.mean(err * err, axis=-1, keepdims=True), axis=0, keepdims=True)
        return [dx], [jnp.broadcast_to(part, (1, LANES)), dg]

    (dx,), (loss, g_final) = _tile_call("loss_head", head, [x, target], [w["norm_final"]], [(d, F32)],
                                        [(1, LANES), (1, d)], 512)
    grads = {"norm_final": g_final, "norm_mix": [None] * depth, "norm_ffn": [None] * depth, "ffn": [None] * depth}
    order = ()
    for layer in reversed(range(depth)):
        mixer = _MIXERS[layer % 4]
        sv = saved[layer]
        wf = dict(w["ffn"][layer], norm=w["norm_ffn"][layer:layer + 1], **big["ffn", layer])
        dx, gf = _ffn_bwd(sv["x_mid"], dx, wf, seq, sv["ffn"], order)
        order = put_big("ffn", layer, {k: gf.pop(k) for k in _BIG_KEYS["ffn"]})
        grads["norm_ffn"][layer] = gf.pop("norm")
        grads["ffn"][layer] = gf
        wm = dict(w[mixer], **big[mixer, layer])
        if mixer == "gla":
            dh, gm = _gla_layer_bwd(sv["h"], dx, wm, seq, sv, order)
        elif mixer == "cv":
            dh, gm = _cv_layer_bwd(sv["h"], dx, wm, seq, sv, order)
        elif mixer == "sg":
            dh, gm = _sg_layer_bwd(sv["h"], dx, wm, seq, sv, order)
        else:
            dh, gm = _hg_layer_bwd(sv["h"], dx, wm, seq, sv, layer, order)
        order = put_big(mixer, layer, {k: gm.pop(k) for k in _BIG_KEYS[mixer]})
        grads[mixer] = gm
        dx, grads["norm_mix"][layer] = _rms_bwd(sv["x_in"], w["norm_mix"][layer:layer + 1], dh, dx, "mix_norm_bwd")
    return loss, dx, grads


def _prep_small(p):
    row = lambda a: a.reshape(1, -1).astype(F32)
    w = {"norm_mix": p["norm_mix"].astype(F32), "norm_ffn": p["norm_ffn"].astype(F32), "norm_final": row(p["norm_final"])}
    w["gla"] = dict(w_g2p=jnp.pad(p["gla_w_g2"][0].astype(F32), ((0, LANES - GLA_RANK), (0, 0))), b_g2=row(p["gla_b_g2"]),
                    norm=row(p["gla_norm"]))
    w["cv"] = dict(b_in=row(p["cv_b_in"]), w_dw=p["cv_w_dw"][0].astype(F32), b_dw=row(p["cv_b_dw"]), ln_g=row(p["cv_ln_g"]),
                   ln_b=row(p["cv_ln_b"]), b_out=row(p["cv_b_out"]))
    b_st = jnp.pad(p["sg_b_s"][0].astype(F32).T, ((0, 0), (0, LANES - SGU_GROUPS)))
    w["sg"] = dict(b_in=row(p["sg_b_in"]), ln_g=row(p["sg_ln_g"]), ln_b=row(p["sg_ln_b"]), w_s=p["sg_w_s"][0].astype(F32),
                   b_st=b_st, b_out=row(p["sg_b_out"]))
    w["hg"] = dict(lb_table=p["hg_lb_table"].astype(F32), norm=row(p["hg_norm"]))
    w["ffn"] = [dict(w_dw=p["ffn_w_dw"][layer].astype(F32)) for layer in range(p["ffn_w_dw"].shape[0])]
    return w


def _small_grads(g):
    gla, cv, sg, hg = g["gla"], g["cv"], g["sg"], g["hg"]
    return {
        "norm_mix": jnp.concatenate(g["norm_mix"], axis=0), "norm_ffn": jnp.concatenate(g["norm_ffn"], axis=0),
        "norm_final": g["norm_final"][0],
        "gla_w_g2": gla["w_g2p"][:GLA_RANK][None], "gla_b_g2": gla["b_g2"], "gla_norm": gla["norm"],
        "cv_b_in": cv["b_in"], "cv_w_dw": cv["w_dw"][None], "cv_b_dw": cv["b_dw"], "cv_ln_g": cv["ln_g"],
        "cv_ln_b": cv["ln_b"], "cv_b_out": cv["b_out"],
        "sg_b_in": sg["b_in"], "sg_ln_g": sg["ln_g"], "sg_ln_b": sg["ln_b"], "sg_w_s": sg["w_s"][None],
        "sg_b_s": sg["b_st"][:, :SGU_GROUPS].T[None], "sg_b_out": sg["b_out"],
        "hg_lb_table": hg["lb_table"], "hg_norm": hg["norm"],
        "ffn_w_dw": jnp.stack([f["w_dw"] for f in g["ffn"]]),
    }


def _oriented(kind, mats):
    if kind == "ffn":
        (up, up_t), (down, down_t) = mats["w_up"], mats["w_down"]
        f = down.shape[0]
        return dict(w_up=up, w_up_t_gate=up_t[:f], w_up_t_val=up_t[f:], w_down=down, w_down_t=down_t)
    (w_in, w_in_t), (w_out, w_out_t) = mats["w_in"], mats["w_out"]
    if kind != "gla":
        return dict(w_in=w_in, w_in_t=w_in_t, w_out=w_out, w_out_t=w_out_t)
    n_main = w_in.shape[1] - GLA_RANK
    return dict(w_main=w_in[:, :n_main], w_glr=jnp.pad(w_in[:, n_main:], ((0, 0), (0, LANES - GLA_RANK))),
                w_main_t=w_in_t[:n_main], w_glr_t=jnp.pad(w_in_t[n_main:], ((0, LANES - GLA_RANK), (0, 0))),
                w_out=w_out, w_out_t=w_out_t)


def _all_gather(x, *, name):
    m_per, n = x.shape

    def body(x_ref, out_ref, send_sems, recv_sems, local_sem):
        mx, my, mc = lax.axis_index("x"), lax.axis_index("y"), lax.axis_index("c")
        me, sibling = (mx, my, mc), (mx, my, 1 - mc)
        chips = [(1 - mx, my), (mx, 1 - my), (1 - mx, 1 - my)]

        def rows(px, py, pc):
            return out_ref.at[pl.ds((4 * px + 2 * py + pc) * m_per, m_per), :]

        def copy(k, block, to, src=None):
            return pltpu.make_async_remote_copy(
                src_ref=rows(*block) if src is None else src, dst_ref=rows(*block), send_sem=send_sems.at[k],
                recv_sem=recv_sems.at[k], device_id=to, device_id_type=MESH)

        mine = pltpu.make_async_copy(x_ref, rows(*me), local_sem)
        mine.start()
        first = [copy(0, me, sibling, src=x_ref)]
        first += [copy(1 + j, me, (*chip, mc), src=x_ref) for j, chip in enumerate(chips)]
        for cp in first:
            cp.start()
        passed = [copy(4 + j, (*chip, mc), sibling) for j, chip in enumerate(chips)]
        for j, chip in enumerate(chips):
            copy(1 + j, (*chip, mc), me).wait_recv()
            passed[j].start()
        copy(0, sibling, me).wait_recv()
        for j, chip in enumerate(chips):
            copy(4 + j, (*chip, 1 - mc), me).wait_recv()
        for cp in first + passed:
            cp.wait_send()
        mine.wait()

    return pl.pallas_call(
        body, name=name, out_shape=jax.ShapeDtypeStruct((N_DEV * m_per, n), x.dtype),
        in_specs=[pl.BlockSpec(memory_space=pltpu.VMEM)], out_specs=pl.BlockSpec(memory_space=pltpu.VMEM),
        scratch_shapes=[pltpu.SemaphoreType.DMA((7,)), pltpu.SemaphoreType.DMA((7,)), pltpu.SemaphoreType.DMA],
    )(x)


def _my_index():
    return 4 * lax.axis_index("x") + 2 * lax.axis_index("y") + lax.axis_index("c")


def _gather_stage(srcs, *, name):
    n = len(srcs)

    def body(*refs):
        x_refs, out_refs = refs[:n], refs[n:2 * n]
        send_sems, recv_sems, local_sems = refs[2 * n:]
        mx, my, mc = lax.axis_index("x"), lax.axis_index("y"), lax.axis_index("c")
        me, sibling = (mx, my, mc), (mx, my, 1 - mc)
        chips = [(1 - mx, my), (mx, 1 - my), (1 - mx, 1 - my)]

        def slot(i, px, py, pc):
            return out_refs[i].at[4 * px + 2 * py + pc]

        def copy(i, k, block, to, src=None):
            return pltpu.make_async_remote_copy(
                src_ref=slot(i, *block) if src is None else src, dst_ref=slot(i, *block), send_sem=send_sems.at[7 * i + k],
                recv_sem=recv_sems.at[7 * i + k], device_id=to, device_id_type=MESH)

        mine = [pltpu.make_async_copy(x_refs[i], slot(i, *me), local_sems.at[i]) for i in range(n)]
        first = [copy(i, 0, me, sibling, src=x_refs[i]) for i in range(n)]
        first += [copy(i, 1 + j, me, (*chip, mc), src=x_refs[i]) for j, chip in enumerate(chips) for i in range(n)]
        for cp in mine + first:
            cp.start()
        passed = []
        for j, chip in enumerate(chips):
            for i in range(n):
                copy(i, 1 + j, (*chip, mc), me).wait_recv()
                passed.append(copy(i, 4 + j, (*chip, mc), sibling))
                passed[-1].start()
        for i in range(n):
            copy(i, 0, sibling, me).wait_recv()
            for j, chip in enumerate(chips):
                copy(i, 4 + j, (*chip, 1 - mc), me).wait_recv()
        for cp in first + passed:
            cp.wait_send()
        for cp in mine:
            cp.wait()

    any_space = pl.BlockSpec(memory_space=pl.ANY)
    return pl.pallas_call(
        body, name=name, out_shape=[jax.ShapeDtypeStruct((N_DEV,) + s.shape, s.dtype) for s in srcs],
        in_specs=[any_space] * n, out_specs=[any_space] * n,
        scratch_shapes=[pltpu.SemaphoreType.DMA((7 * n,)), pltpu.SemaphoreType.DMA((7 * n,)), pltpu.SemaphoreType.DMA((n,))],
    )(*srcs)


def _scatter_stage(srcs, *, name):
    n = len(srcs)

    def body(*refs):
        x_refs, out_refs = refs[:n], refs[n:2 * n]
        send_sems, recv_sems, local_sems = refs[2 * n:]
        mx, my, mc = lax.axis_index("x"), lax.axis_index("y"), lax.axis_index("c")
        me = 4 * mx + 2 * my + mc
        mine = [pltpu.make_async_copy(x_refs[i].at[me], out_refs[i].at[me], local_sems.at[i]) for i in range(n)]
        for cp in mine:
            cp.start()
        sends, recvs = [], []
        for k in range(1, N_DEV):
            px = 1 - mx if k & 4 else mx
            py = 1 - my if k & 2 else my
            pc = 1 - mc if k & 1 else mc
            peer = 4 * px + 2 * py + pc
            for i in range(n):
                sems = dict(send_sem=send_sems.at[7 * i + k - 1], recv_sem=recv_sems.at[7 * i + k - 1],
                            device_id=(px, py, pc), device_id_type=MESH)
                sends.append(pltpu.make_async_remote_copy(src_ref=x_refs[i].at[peer], dst_ref=out_refs[i].at[me], **sems))
                recvs.append(pltpu.make_async_remote_copy(src_ref=x_refs[i].at[me], dst_ref=out_refs[i].at[peer], **sems))
                sends[-1].start()
        for cp in recvs:
            cp.wait_recv()
        for cp in sends:
            cp.wait_send()
        for cp in mine:
            cp.wait()

    any_space = pl.BlockSpec(memory_space=pl.ANY)
    return pl.pallas_call(
        body, name=name, out_shape=[jax.ShapeDtypeStruct(s.shape, s.dtype) for s in srcs],
        in_specs=[any_space] * n, out_specs=[any_space] * n,
        scratch_shapes=[pltpu.SemaphoreType.DMA((7 * n,)), pltpu.SemaphoreType.DMA((7 * n,)), pltpu.SemaphoreType.DMA((n,))],
    )(*srcs)


def _adamw_math(g, w, m, v):
    c1, c2 = 1.0 - ADAM_B1 ** ADAM_STEP, 1.0 - ADAM_B2 ** ADAM_STEP
    m_new = ADAM_B1 * m + (1.0 - ADAM_B1) * g
    v_new = ADAM_B2 * v + (1.0 - ADAM_B2) * (g * g)
    delta = -ADAM_LR * ((m_new / c1) / (jnp.sqrt(v_new / c2) + ADAM_EPS) + ADAM_WD * w)
    return delta, m_new, v_new


def _adamw_big(slots, w, m, v, layer, *, name):
    _, r, c = slots.shape
    tr = _divisor_tile(r, max(8, (200 * 1024) // c // 8 * 8), 8)

    def body(s_ref, w_ref, m_ref, v_ref, g_out, d_out, m_out, v_out):
        g = s_ref[0].astype(F32)
        for p in range(1, N_DEV):
            g = g + s_ref[p].astype(F32)
        g_out[...] = g
        d_out[...], m_out[...], v_out[...] = _adamw_math(g, w_ref[...], m_ref[...], v_ref[...])

    blk = pl.BlockSpec((tr, c), lambda i: (i, 0))
    lay = pl.BlockSpec((None, tr, c), lambda i: (layer, i, 0))
    return pl.pallas_call(
        body, name=name, grid=(r // tr,), in_specs=[pl.BlockSpec((N_DEV, tr, c), lambda i: (0, i, 0)), lay, lay, lay],
        out_specs=[blk] * 4, out_shape=[jax.ShapeDtypeStruct((r, c), F32)] * 4,
        compiler_params=pltpu.CompilerParams(dimension_semantics=("parallel",)),
    )(slots, w, m, v)


def _sum_small(got, r_re, r_sh, *, name):
    per_dev = r_re + N_DEV * r_sh

    def body(got_ref, re_ref, sh_ref):
        mine = r_re + _my_index() * r_sh
        acc_re = got_ref[0:r_re, :]
        acc_sh = got_ref[pl.ds(pl.multiple_of(mine, 8), r_sh), :]
        for p in range(1, N_DEV):
            acc_re = acc_re + got_ref[p * per_dev:p * per_dev + r_re, :]
            acc_sh = acc_sh + got_ref[pl.ds(pl.multiple_of(p * per_dev + mine, 8), r_sh), :]
        re_ref[...] = acc_re
        sh_ref[...] = acc_sh

    return pl.pallas_call(body, name=name, out_shape=[jax.ShapeDtypeStruct((r_re, LANES), F32),
                                                       jax.ShapeDtypeStruct((r_sh, LANES), F32)])(got)


def _adamw_small(gs, ws, ms, vs, *, name):
    n = len(gs)

    def body(*refs):
        ins, outs = refs[:4 * n], refs[4 * n:]
        for i in range(n):
            res = _adamw_math(ins[i][...], ins[n + i][...], ins[2 * n + i][...], ins[3 * n + i][...])
            for j in range(3):
                outs[j * n + i][...] = res[j]

    out = pl.pallas_call(body, name=name, out_shape=[jax.ShapeDtypeStruct(a.shape, F32) for a in ws] * 3)(*gs, *ws, *ms, *vs)
    return out[:n], out[n:2 * n], out[2 * n:]


def _layout(shapes, row_align, total_align):
    lay, off = {}, 0
    for name, shape in shapes.items():
        size = int(np.prod(shape))
        rows = -(-size // LANES)
        rows = -(-rows // row_align) * row_align
        lay[name] = (off, rows, size, tuple(shape))
        off += rows
    return lay, -(-off // total_align) * total_align


def _pack(arrs, lay, total, dtype, lead=()):
    parts = []
    nl = len(lead)
    for name, (off, rows, size, shape) in lay.items():
        flat = arrs[name].astype(dtype).reshape(*lead, size)
        parts.append(jnp.pad(flat, [(0, 0)] * nl + [(0, rows * LANES - size)]).reshape(*lead, rows, LANES))
    used = sum(v[1] for v in lay.values())
    if total > used:
        parts.append(jnp.zeros((*lead, total - used, LANES), dtype))
    return jnp.concatenate(parts, axis=nl)


def _unpack(buf, lay, lead=()):
    out = {}
    nl = len(lead)
    for name, (off, rows, size, shape) in lay.items():
        part = lax.slice_in_dim(buf, off, off + rows, axis=nl).reshape(*lead, rows * LANES)
        out[name] = lax.slice_in_dim(part, 0, size, axis=nl).reshape(*lead, *shape)
    return out


_SHARD_AXIS = {
    "norm_mix": None, "norm_ffn": None, "norm_final": None, "gla_w_in": 2, "gla_w_g2": 2, "gla_b_g2": None,
    "gla_norm": None, "gla_w_out": 1, "cv_w_in": 2, "cv_b_in": 1, "cv_w_dw": 2, "cv_b_dw": 1, "cv_ln_g": 1,
    "cv_ln_b": 1, "cv_w_out": 1, "cv_b_out": 1, "sg_w_in": 2, "sg_b_in": 1, "sg_ln_g": 1, "sg_ln_b": 1, "sg_w_s": None,
    "sg_b_s": None, "sg_w_out": 1, "sg_b_out": 1, "hg_w_in": 2, "hg_lb_table": None, "hg_norm": None, "hg_w_out": 1,
    "ffn_w_up": 2, "ffn_w_dw": 2, "ffn_w_down": 1,
}
_MATMUL_WEIGHTS = ("gla_w_in", "gla_w_out", "cv_w_in", "cv_w_out", "sg_w_in", "sg_w_out", "hg_w_in", "hg_w_out",
                   "ffn_w_up", "ffn_w_down")
_NAMES = tuple(_SHARD_AXIS)


def kernel(x, norm_mix, norm_ffn, norm_final, gla_w_in, gla_w_g2, gla_b_g2, gla_norm, gla_w_out, cv_w_in, cv_b_in, cv_w_dw, cv_b_dw, cv_ln_g, cv_ln_b, cv_w_out, cv_b_out, sg_w_in, sg_b_in, sg_ln_g, sg_ln_b, sg_w_s, sg_b_s, sg_w_out, sg_b_out, hg_w_in, hg_lb_table, hg_norm, hg_w_out, ffn_w_up, ffn_w_dw, ffn_w_down, loss_target, m_norm_mix, m_norm_ffn, m_norm_final, m_gla_w_in, m_gla_w_g2, m_gla_b_g2, m_gla_norm, m_gla_w_out, m_cv_w_in, m_cv_b_in, m_cv_w_dw, m_cv_b_dw, m_cv_ln_g, m_cv_ln_b, m_cv_w_out, m_cv_b_out, m_sg_w_in, m_sg_b_in, m_sg_ln_g, m_sg_ln_b, m_sg_w_s, m_sg_b_s, m_sg_w_out, m_sg_b_out, m_hg_w_in, m_hg_lb_table, m_hg_norm, m_hg_w_out, m_ffn_w_up, m_ffn_w_dw, m_ffn_w_down, v_norm_mix, v_norm_ffn, v_norm_final, v_gla_w_in, v_gla_w_g2, v_gla_b_g2, v_gla_norm, v_gla_w_out, v_cv_w_in, v_cv_b_in, v_cv_w_dw, v_cv_b_dw, v_cv_ln_g, v_cv_ln_b, v_cv_w_out, v_cv_b_out, v_sg_w_in, v_sg_b_in, v_sg_ln_g, v_sg_ln_b, v_sg_w_s, v_sg_b_s, v_sg_w_out, v_sg_b_out, v_hg_w_in, v_hg_lb_table, v_hg_norm, v_hg_w_out, v_ffn_w_up, v_ffn_w_dw, v_ffn_w_down):
    local = dict(locals())
    wts = {n: local[n] for n in _NAMES}
    mom = {n: local["m_" + n] for n in _NAMES}
    var = {n: local["v_" + n] for n in _NAMES}
    small_all = [n for n in _NAMES if n not in _MATMUL_WEIGHTS]
    small_sharded = [n for n in small_all if _SHARD_AXIS[n] is not None]
    bsz, seq, d = x.shape
    depth = norm_mix.shape[0]

    stages = {}
    for layer in range(depth):
        kind = _MIXERS[layer % 4]
        stages[kind, layer] = {"w_in": (kind + "_w_in", layer // 4), "w_out": (kind + "_w_out", layer // 4)}
        stages["ffn", layer] = {"w_up": ("ffn_w_up", layer), "w_down": ("ffn_w_down", layer)}

    gathered = {}
    for (kind, layer), keys in stages.items():
        srcs = [wts[nm][idx].astype(BF16) for nm, idx in keys.values()]
        gathered[kind, layer] = _gather_stage(srcs, name="gather_" + kind)
    lay_sw, r_sw = _layout({n: wts[n].shape for n in small_sharded}, 8, 8)
    got_sw = _all_gather(_pack(wts, lay_sw, r_sw, F32), name="gather_small_weights")
    parts = _unpack(got_sw.reshape(N_DEV, r_sw, LANES), lay_sw, (N_DEV,))
    full_small = {n: wts[n] for n in small_all if _SHARD_AXIS[n] is None}
    for n in small_sharded:
        ax, shape = _SHARD_AXIS[n], wts[n].shape
        full_small[n] = jnp.moveaxis(parts[n], 0, ax).reshape(shape[:ax] + (N_DEV * shape[ax],) + shape[ax + 1:])

    def get_big(kind, layer, after):
        mats = {}
        for (key, (nm, _)), land in zip(stages[kind, layer].items(), gathered[kind, layer]):
            _, r, c = land.shape
            if _SHARD_AXIS[nm] == 2:
                mats[key] = (land.transpose(1, 0, 2).reshape(r, N_DEV * c), land.transpose(0, 2, 1).reshape(N_DEV * c, r))
            else:
                mats[key] = (land.reshape(N_DEV * r, c), land.reshape(N_DEV * r, c).T)
        return _oriented(kind, mats)

    sends = {}

    def put_big(kind, layer, g):
        if kind == "ffn":
            k, f = g["w_up_gate"].shape
            halves = [g[key].reshape(k, N_DEV // 2, 2 * f // N_DEV) for key in ("w_up_gate", "w_up_val")]
            w_in = jnp.concatenate(halves, axis=1)
        else:
            w_in = jnp.concatenate([g["w_main"], g["w_glr"][:, :GLA_RANK]], axis=1) if kind == "gla" else g["w_in"]
            w_in = w_in.reshape(w_in.shape[0], N_DEV, w_in.shape[1] // N_DEV)
        w_out = g["w_down"] if kind == "ffn" else g["w_out"]
        w_out = w_out.reshape(N_DEV, w_out.shape[0] // N_DEV, w_out.shape[1])
        sends[kind, layer] = [w_in.transpose(1, 0, 2).astype(BF16), w_out.astype(BF16)]
        return ()

    loss, dx, grads = _local_step(x.reshape(bsz * seq, d), loss_target.reshape(bsz * seq, d), _prep_small(full_small), seq,
                                  get_big, put_big)
    loss = lax.psum(loss[0, 0], ("x", "y", "c"))

    gs = _small_grads(grads)
    small_repl = [n for n in small_all if _SHARD_AXIS[n] is None]
    lay_re, r_re = _layout({n: wts[n].shape for n in small_repl}, 8, 8)
    slots = {}
    for n in small_sharded:
        ax, shape = _SHARD_AXIS[n], wts[n].shape
        slots[n] = jnp.moveaxis(gs[n].reshape(shape[:ax] + (N_DEV, shape[ax]) + shape[ax + 1:]), ax, 0)
    sent = jnp.concatenate([_pack(gs, lay_re, r_re, F32), _pack(slots, lay_sw, r_sw, F32, (N_DEV,)).reshape(-1, LANES)])
    sum_re, sum_sh = _sum_small(_all_gather(sent, name="gather_small_grads"), r_re, r_sw, name="sum_small_grads")
    g_own = _unpack(sum_re, lay_re)
    g_own.update(_unpack(sum_sh, lay_sw))
    two_d = lambda a: a.reshape(-1, a.shape[-1])
    upd = _adamw_small(*[[two_d(src[n]) for n in small_all] for src in (g_own, wts, mom, var)], name="adamw_small")
    results = {n: [g_own[n]] + [part[i].reshape(wts[n].shape) for part in upd] for i, n in enumerate(small_all)}

    per_layer = {}
    for (kind, layer), arrays in sends.items():
        lands = _scatter_stage(arrays, name="scatter_" + kind)
        for (nm, idx), land in zip(stages[kind, layer].values(), lands):
            three_d = lambda a: a.reshape((a.shape[0],) + land.shape[1:])
            per_layer.setdefault(nm, {})[idx] = _adamw_big(land, three_d(wts[nm]), three_d(mom[nm]), three_d(var[nm]), idx,
                                                           name="adamw_" + nm)
    for nm, by_idx in per_layer.items():
        outs = [by_idx[i] for i in range(len(by_idx))]
        results[nm] = [(outs[0][j] if len(outs) == 1 else jnp.stack([o[j] for o in outs])).reshape(wts[nm].shape)
                       for j in range(4)]
    out = [loss, dx.reshape(bsz, seq, d)]
    for j in range(4):
        out += [results[n][j] for n in _NAMES]
    return tuple(out)
```

```python
import functools
import math

import jax
import jax.numpy as jnp
import numpy as np
from jax import lax
from jax.experimental import pallas as pl
from jax.experimental.pallas import tpu as pltpu

F32 = jnp.float32
BF16 = jnp.bfloat16
EPS = 1e-6
N_DEV = 8
LANES = 128
SUBLANES_BF16 = 16
HALO = 32
GLA_HEADS, GLA_RANK, GLA_GATE_NORM, GLA_CHUNK = 4, 16, 16.0, 64
SGU_CHUNK, SGU_GROUPS = 128, 8
HGRN_EXPAND, HGRN_CHUNK = 128, 64
CONV_WIDTH, FFN_CONV_WIDTH = 31, 3
ADAM_LR, ADAM_B1, ADAM_B2, ADAM_EPS, ADAM_WD, ADAM_STEP = 0.001, 0.9, 0.999, 1e-08, 0.01, 10
MESH = pl.DeviceIdType.MESH


def _sigmoid(x):
    return 0.5 * (jnp.tanh(0.5 * x) + 1.0)


def _silu(x):
    return x * _sigmoid(x)


def _log_sigmoid(x):
    return jnp.minimum(x, 0.0) - jnp.log(1.0 + jnp.exp(-jnp.abs(x)))


def _gelu(x):
    return 0.5 * x * (1.0 + jnp.tanh(math.sqrt(2.0 / math.pi) * (x + 0.044715 * (x * x * x))))


def _rms(x, g):
    return x * lax.rsqrt(jnp.mean(x * x, axis=-1, keepdims=True) + EPS) * g


def _layer_norm(x, g, b):
    xc = x - jnp.mean(x, axis=-1, keepdims=True)
    return xc * lax.rsqrt(jnp.mean(xc * xc, axis=-1, keepdims=True) + EPS) * g + b


def _dot_raw(a, b, dims):
    return lax.dot_general(a.astype(BF16), b.astype(BF16), (dims, ((), ())), preferred_element_type=F32)


@jax.custom_vjp
def _bdot(a, b):
    return _dot_raw(a, b, ((1,), (0,)))


@jax.custom_vjp
def _bdot_nt(a, b):
    return _dot_raw(a, b, ((1,), (1,)))


@jax.custom_vjp
def _bdot_tn(a, b):
    return _dot_raw(a, b, ((0,), (0,)))


_bdot.defvjp(lambda a, b: (_bdot(a, b), (a, b)), lambda r, g: (_bdot_nt(g, r[1]), _bdot_tn(r[0], g)))
_bdot_nt.defvjp(lambda a, b: (_bdot_nt(a, b), (a, b)), lambda r, g: (_bdot(g, r[1]), _bdot_tn(g, r[0])))
_bdot_tn.defvjp(lambda a, b: (_bdot_tn(a, b), (a, b)), lambda r, g: (_bdot_nt(r[1], g), _bdot(r[0], g)))


def _hdot(a, b):
    return jnp.dot(a, b, precision=lax.Precision.HIGHEST, preferred_element_type=F32)


def _divisor_tile(n, cap, unit):
    if n <= cap:
        return n
    best = None
    for t in range(unit, cap + 1, unit):
        if n % t == 0:
            best = t
    assert best is not None, (n, cap, unit)
    return best


def _const_map(nd):
    return lambda *_: (0,) * nd


def _dep_specs(deps, grid_rank):
    return [pl.BlockSpec(d.shape, (lambda *_, nd=d.ndim: (0,) * nd)) for d in deps]


def _mm(a, b, *, add=None, bias=None, out_dtype=F32, name, deps=()):
    m, k = a.shape
    k2, n = b.shape
    assert k == k2
    tn = _divisor_tile(n, max(LANES, min(1408, (6 << 20) // (2 * k) // LANES * LANES)), LANES)
    tm = _divisor_tile(m, max(256, min(1024, (4 << 20) // (a.dtype.itemsize * k) // 256 * 256)), 8)
    has_bias, has_add = bias is not None, add is not None

    def body(*refs):
        a_ref, b_ref = refs[0], refs[1]
        o_ref = refs[-1]
        acc = jnp.dot(a_ref[...].astype(BF16), b_ref[...], preferred_element_type=F32)
        pos = 2
        if has_bias:
            acc = acc + refs[pos][...]
            pos += 1
        if has_add:
            acc = acc + refs[pos][...].astype(F32)
        o_ref[...] = acc.astype(o_ref.dtype)

    in_specs = [pl.BlockSpec((tm, k), lambda i, j: (i, 0)), pl.BlockSpec((k, tn), lambda i, j: (0, j))]
    args = [a, b]
    if has_bias:
        in_specs.append(pl.BlockSpec((1, tn), lambda i, j: (0, j)))
        args.append(bias)
    if has_add:
        in_specs.append(pl.BlockSpec((tm, tn), lambda i, j: (i, j)))
        args.append(add)
    in_specs += _dep_specs(deps, 2)
    args += list(deps)
    return pl.pallas_call(
        body, name=name, grid=(m // tm, n // tn), in_specs=in_specs,
        out_specs=pl.BlockSpec((tm, tn), lambda i, j: (i, j)),
        out_shape=jax.ShapeDtypeStruct((m, n), out_dtype),
        compiler_params=pltpu.CompilerParams(dimension_semantics=("parallel", "parallel")),
    )(*args)


def _mm_tn(a, g, *, name):
    m, k = a.shape
    m2, n = g.shape
    assert m == m2
    tk = _divisor_tile(k, 1408, LANES)
    tn = _divisor_tile(n, 1408, LANES)
    tm = _divisor_tile(m, 1024, 8)

    def body(a_ref, g_ref, o_ref):
        @pl.when(pl.program_id(2) == 0)
        def _():
            o_ref[...] = jnp.zeros_like(o_ref)

        o_ref[...] += _dot_raw(a_ref[...], g_ref[...], ((0,), (0,)))

    return pl.pallas_call(
        body, name=name, grid=(k // tk, n // tn, m // tm),
        in_specs=[pl.BlockSpec((tm, tk), lambda i, j, t: (t, i)), pl.BlockSpec((tm, tn), lambda i, j, t: (t, j))],
        out_specs=pl.BlockSpec((tk, tn), lambda i, j, t: (i, j)),
        out_shape=jax.ShapeDtypeStruct((k, n), F32),
        compiler_params=pltpu.CompilerParams(dimension_semantics=("parallel", "parallel", "arbitrary")),
    )(a, g)


def _tile_call(name, fn, tiled, params, out_tiled, out_acc, tile, deps=()):
    tiled = [t if isinstance(t, tuple) else (t, t.shape[1], 0) for t in tiled]
    t_rows = tiled[0][0].shape[0]
    tile = min(tile, t_rows)
    assert t_rows % tile == 0
    n_t, n_p, n_o, n_d = len(tiled), len(params), len(out_tiled), len(deps)

    def body(*refs):
        vals = [r[...] for r in refs[: n_t + n_p]]
        refs = refs[: n_t + n_p] + refs[n_t + n_p + n_d:]
        touts, aouts = fn(*vals)
        for r, v in zip(refs[n_t + n_p: n_t + n_p + n_o], touts):
            r[...] = v.astype(r.dtype)
        acc_refs = refs[n_t + n_p + n_o:]
        if acc_refs:
            @pl.when(pl.program_id(0) == 0)
            def _():
                for r in acc_refs:
                    r[...] = jnp.zeros_like(r)

            for r, v in zip(acc_refs, aouts):
                r[...] += v

    in_specs = [pl.BlockSpec((tile, w), lambda i, cb=cb: (i, cb)) for _, w, cb in tiled]
    in_specs += [pl.BlockSpec(p.shape, _const_map(p.ndim)) for p in params]
    in_specs += _dep_specs(deps, 1)
    out_specs = [pl.BlockSpec((tile, w), lambda i: (i, 0)) for w, _ in out_tiled]
    out_specs += [pl.BlockSpec(s, _const_map(len(s))) for s in out_acc]
    out_shape = [jax.ShapeDtypeStruct((t_rows, w), dt) for w, dt in out_tiled]
    out_shape += [jax.ShapeDtypeStruct(s, F32) for s in out_acc]
    res = pl.pallas_call(
        body, name=name, grid=(t_rows // tile,), in_specs=in_specs, out_specs=out_specs, out_shape=out_shape,
        compiler_params=pltpu.CompilerParams(dimension_semantics=("arbitrary" if out_acc else "parallel",)),
    )(*[t[0] for t in tiled], *params, *deps)
    return res[:n_o], res[n_o:]


def _rms_fwd(x, g, name, deps=()):
    (h,), _ = _tile_call(name, lambda xv, gv: ([_rms(xv, gv)], []), [x], [g], [(x.shape[1], BF16)], [], 512, deps)
    return h


def _rms_bwd(x, g, dh, dres, name):
    def fn(xv, dhv, drv, gv):
        _, vjp = jax.vjp(_rms, xv, gv)
        dx, dg = vjp(dhv.astype(F32))
        return [drv + dx], [dg]

    (dx,), (dg,) = _tile_call(name, fn, [x, dh, dres], [g], [(x.shape[1], F32)], [g.shape], 512)
    return dx, dg


def _colsum(x, name):
    _, (s,) = _tile_call(name, lambda xv: ([], [jnp.sum(xv.astype(F32), axis=0, keepdims=True)]), [x], [], [],
                         [(1, x.shape[1])], 512)
    return s


def _seq_flags(i, tiles_per_seq):
    pos = i % tiles_per_seq
    return pos == 0, pos == tiles_per_seq - 1


def _dwconv_fwd(x, w, b, seq, name):
    t_rows, ch = x.shape
    kw = w.shape[0]
    tile = min(512, seq)
    cb = _divisor_tile(ch, 256, LANES)
    tps, hb = seq // tile, tile // HALO

    def body(x_ref, halo_ref, w_ref, b_ref, y_ref, pad_ref):
        first, _ = _seq_flags(pl.program_id(0), tps)
        pad_ref[0:HALO, :] = jnp.where(first, 0.0, halo_ref[...])
        pad_ref[HALO:HALO + tile, :] = x_ref[...]
        acc = jnp.broadcast_to(b_ref[...], (tile, cb))
        for k in range(kw):
            acc = acc + pad_ref[pl.ds(HALO - (kw - 1) + k, tile), :] * w_ref[k:k + 1, :]
        y_ref[...] = acc

    return pl.pallas_call(
        body, name=name, grid=(t_rows // tile, ch // cb),
        in_specs=[pl.BlockSpec((tile, cb), lambda i, j: (i, j)),
                  pl.BlockSpec((HALO, cb), lambda i, j: (jnp.maximum(i * hb - 1, 0), j)),
                  pl.BlockSpec((kw, cb), lambda i, j: (0, j)), pl.BlockSpec((1, cb), lambda i, j: (0, j))],
        out_specs=pl.BlockSpec((tile, cb), lambda i, j: (i, j)),
        out_shape=jax.ShapeDtypeStruct((t_rows, ch), F32),
        scratch_shapes=[pltpu.VMEM((HALO + tile, cb), F32)],
        compiler_params=pltpu.CompilerParams(dimension_semantics=("parallel", "parallel")),
    )(x, x, w, b)


def _dwconv_bwd(x, dy, w, seq, name):
    t_rows, ch = x.shape
    kw = w.shape[0]
    tile = min(512, seq)
    cb = _divisor_tile(ch, 256, LANES)
    tps, hb, n_hb = seq // tile, tile // HALO, t_rows // HALO

    def body(x_ref, xh_ref, dy_ref, dyh_ref, w_ref, dx_ref, dw_ref, db_ref, xpad, dypad):
        i = pl.program_id(1)
        first, last = _seq_flags(i, tps)

        @pl.when(i == 0)
        def _():
            dw_ref[...] = jnp.zeros_like(dw_ref)
            db_ref[...] = jnp.zeros_like(db_ref)

        xpad[0:HALO, :] = jnp.where(first, 0.0, xh_ref[...])
        xpad[HALO:HALO + tile, :] = x_ref[...]
        dyv = dy_ref[...]
        dypad[0:tile, :] = dyv
        dypad[tile:tile + HALO, :] = jnp.where(last, 0.0, dyh_ref[...])
        acc = jnp.zeros((tile, cb), F32)
        for k in range(kw):
            acc = acc + dypad[pl.ds(kw - 1 - k, tile), :] * w_ref[k:k + 1, :]
            dw_ref[k:k + 1, :] += jnp.sum(dyv * xpad[pl.ds(HALO - (kw - 1) + k, tile), :], axis=0, keepdims=True)
        dx_ref[...] = acc
        db_ref[...] += jnp.sum(dyv, axis=0, keepdims=True)

    return pl.pallas_call(
        body, name=name, grid=(ch // cb, t_rows // tile),
        in_specs=[pl.BlockSpec((tile, cb), lambda j, i: (i, j)),
                  pl.BlockSpec((HALO, cb), lambda j, i: (jnp.maximum(i * hb - 1, 0), j)),
                  pl.BlockSpec((tile, cb), lambda j, i: (i, j)),
                  pl.BlockSpec((HALO, cb), lambda j, i: (jnp.minimum((i + 1) * hb, n_hb - 1), j)),
                  pl.BlockSpec((kw, cb), lambda j, i: (0, j))],
        out_specs=[pl.BlockSpec((tile, cb), lambda j, i: (i, j)), pl.BlockSpec((kw, cb), lambda j, i: (0, j)),
                   pl.BlockSpec((1, cb), lambda j, i: (0, j))],
        out_shape=[jax.ShapeDtypeStruct((t_rows, ch), F32), jax.ShapeDtypeStruct((kw, ch), F32),
                   jax.ShapeDtypeStruct((1, ch), F32)],
        scratch_shapes=[pltpu.VMEM((HALO + tile, cb), F32), pltpu.VMEM((tile + HALO, cb), F32)],
        compiler_params=pltpu.CompilerParams(dimension_semantics=("parallel", "arbitrary")),
    )(x, x, dy, dy, w)


_ROWS = SUBLANES_BF16


def _lane_chunks(width, cap=6 * LANES):
    return [slice(c0, min(c0 + cap, width)) for c0 in range(0, width, cap)]


def _tap_rows(w_ref, cols):
    return [w_ref[k:k + 1, cols] for k in range(FFN_CONV_WIDTH)]


def _conv3_at(pad, taps, row, cols):
    z = pad[pl.ds(row, _ROWS), cols] * taps[2]
    z = z + pad[pl.ds(row - 1, _ROWS), cols] * taps[1]
    return z + pad[pl.ds(row - 2, _ROWS), cols] * taps[0]


def _ffn_mid_fwd(u, w, seq, name):
    t_rows, f2 = u.shape
    f = f2 // 2
    tile = min(256, seq)
    cb = _divisor_tile(f, 1408, LANES)
    nj, tps, hb, hl = f // cb, seq // tile, tile // SUBLANES_BF16, SUBLANES_BF16

    def body(ug_ref, uv_ref, hg_ref, hv_ref, wg_ref, wv_ref, a_ref, gpad, vpad):
        first, _ = _seq_flags(pl.program_id(0), tps)
        for t_ref, h_ref, pad in ((ug_ref, hg_ref, gpad), (uv_ref, hv_ref, vpad)):
            pad[0:hl, :] = jnp.where(first, 0.0, h_ref[...].astype(F32))
            pad[hl:hl + tile, :] = t_ref[...].astype(F32)
        for cols in _lane_chunks(cb):
            wg, wv = _tap_rows(wg_ref, cols), _tap_rows(wv_ref, cols)
            for r0 in range(0, tile, _ROWS):
                zg = _conv3_at(gpad, wg, hl + r0, cols)
                zv = _conv3_at(vpad, wv, hl + r0, cols)
                half = 0.5 * zg
                a_ref[pl.ds(r0, _ROWS), cols] = ((jnp.tanh(half) + 1.0) * half * zv).astype(a_ref.dtype)

    halo_map = lambda off: (lambda i, j: (jnp.maximum(i * hb - 1, 0), j + off))
    return pl.pallas_call(
        body, name=name, grid=(t_rows // tile, nj),
        in_specs=[pl.BlockSpec((tile, cb), lambda i, j: (i, j)), pl.BlockSpec((tile, cb), lambda i, j: (i, j + nj)),
                  pl.BlockSpec((hl, cb), halo_map(0)), pl.BlockSpec((hl, cb), halo_map(nj)),
                  pl.BlockSpec((3, cb), lambda i, j: (0, j)), pl.BlockSpec((3, cb), lambda i, j: (0, j + nj))],
        out_specs=pl.BlockSpec((tile, cb), lambda i, j: (i, j)),
        out_shape=jax.ShapeDtypeStruct((t_rows, f), BF16),
        scratch_shapes=[pltpu.VMEM((hl + tile, cb), F32), pltpu.VMEM((hl + tile, cb), F32)],
        compiler_params=pltpu.CompilerParams(dimension_semantics=("parallel", "parallel")),
    )(u, u, u, u, w, w)


def _ffn_mid_bwd(u, da, w, seq, name):
    t_rows, f2 = u.shape
    f = f2 // 2
    tile = min(256, seq)
    cb = _divisor_tile(f, 1408, LANES)
    hl = SUBLANES_BF16
    nj, tps, hb, n_hb, ext = f // cb, seq // tile, tile // hl, t_rows // hl, tile + hl

    def body(ug_ref, uv_ref, pg_ref, pv_ref, ng_ref, nv_ref, da_ref, dan_ref, wg_ref, wv_ref,
             dug_ref, duv_ref, dwg_ref, dwv_ref, gpad, vpad, dzg, dzv):
        i = pl.program_id(1)
        first, last = _seq_flags(i, tps)

        @pl.when(i == 0)
        def _():
            dwg_ref[...] = jnp.zeros_like(dwg_ref)
            dwv_ref[...] = jnp.zeros_like(dwv_ref)

        for t_ref, p_ref, n_ref, pad in ((ug_ref, pg_ref, ng_ref, gpad), (uv_ref, pv_ref, nv_ref, vpad)):
            pad[0:hl, :] = jnp.where(first, 0.0, p_ref[...].astype(F32))
            pad[hl:hl + tile, :] = t_ref[...].astype(F32)
            pad[hl + tile:hl + ext, :] = jnp.where(last, 0.0, n_ref[...].astype(F32))
        for cols in _lane_chunks(cb):
            wg, wv = _tap_rows(wg_ref, cols), _tap_rows(wv_ref, cols)
            for r0 in range(0, ext, _ROWS):
                zg = _conv3_at(gpad, wg, hl + r0, cols)
                zv = _conv3_at(vpad, wv, hl + r0, cols)
                if r0 < tile:
                    da = da_ref[pl.ds(r0, _ROWS), cols].astype(F32)
                else:
                    da = jnp.where(last, 0.0, dan_ref[:, cols].astype(F32))
                sg = _sigmoid(zg)
                dzg[pl.ds(r0, _ROWS), cols] = da * zv * (sg * (1.0 + zg * (1.0 - sg)))
                dzv[pl.ds(r0, _ROWS), cols] = da * (zg * sg)
        for dz, w_ref, pad, du_ref, dw_ref in ((dzg, wg_ref, gpad, dug_ref, dwg_ref), (dzv, wv_ref, vpad, duv_ref, dwv_ref)):
            for cols in _lane_chunks(cb):
                taps = _tap_rows(w_ref, cols)
                width = cols.stop - cols.start
                acc = [jnp.zeros((8, width), F32) for _ in range(FFN_CONV_WIDTH)]
                for r0 in range(0, tile, _ROWS):
                    d0 = dz[pl.ds(r0, _ROWS), cols]
                    du = dz[pl.ds(r0 + 2, _ROWS), cols] * taps[0] + dz[pl.ds(r0 + 1, _ROWS), cols] * taps[1] + d0 * taps[2]
                    du_ref[pl.ds(r0, _ROWS), cols] = du.astype(du_ref.dtype)
                    for k in range(FFN_CONV_WIDTH):
                        prod = d0 * pad[pl.ds(hl - 2 + k + r0, _ROWS), cols]
                        acc[k] = acc[k] + prod[0:8] + prod[8:16]
                for k in range(FFN_CONV_WIDTH):
                    dw_ref[k:k + 1, cols] += jnp.sum(acc[k], axis=0, keepdims=True)

    prev_map = lambda off: (lambda j, i: (jnp.maximum(i * hb - 1, 0), j + off))
    next_map = lambda off: (lambda j, i: (jnp.minimum((i + 1) * hb, n_hb - 1), j + off))
    tile_spec = lambda off: pl.BlockSpec((tile, cb), lambda j, i: (i, j + off))
    w_spec = lambda off: pl.BlockSpec((3, cb), lambda j, i: (0, j + off))
    return pl.pallas_call(
        body, name=name, grid=(nj, t_rows // tile),
        in_specs=[tile_spec(0), tile_spec(nj), pl.BlockSpec((hl, cb), prev_map(0)), pl.BlockSpec((hl, cb), prev_map(nj)),
                  pl.BlockSpec((hl, cb), next_map(0)), pl.BlockSpec((hl, cb), next_map(nj)),
                  tile_spec(0), pl.BlockSpec((hl, cb), next_map(0)), w_spec(0), w_spec(nj)],
        out_specs=[tile_spec(0), tile_spec(0), w_spec(0), w_spec(0)],
        out_shape=[jax.ShapeDtypeStruct((t_rows, f), BF16), jax.ShapeDtypeStruct((t_rows, f), BF16),
                   jax.ShapeDtypeStruct((3, f), F32), jax.ShapeDtypeStruct((3, f), F32)],
        scratch_shapes=[pltpu.VMEM((hl + ext, cb), F32), pltpu.VMEM((hl + ext, cb), F32),
                        pltpu.VMEM((ext, cb), F32), pltpu.VMEM((ext, cb), F32)],
        compiler_params=pltpu.CompilerParams(dimension_semantics=("parallel", "arbitrary")),
    )(u, u, u, u, u, u, da, da, w, w)


def _gla_chunk(q, k, v, lg, st, *, scale, chunk):
    row = lax.broadcasted_iota(jnp.int32, (chunk, chunk), 0)
    col = lax.broadcasted_iota(jnp.int32, (chunk, chunk), 1)
    causal = col <= row
    b = _hdot(causal.astype(F32), lg)
    upto_mid = lax.broadcasted_iota(jnp.int32, lg.shape, 0) <= chunk // 2
    b_mid = jnp.sum(jnp.where(upto_mid, lg, 0.0), axis=0, keepdims=True)
    b_last = jnp.sum(lg, axis=0, keepdims=True)
    qs = q * scale
    scores = _bdot_nt(qs * jnp.exp(b - b_mid), k * jnp.exp(b_mid - b))
    o = _bdot(jnp.where(causal, scores, 0.0), v)
    o = o + _bdot_nt(qs * jnp.exp(b), st)
    st_new = st * jnp.exp(b_last) + _bdot_tn(v, k * jnp.exp(b_last - b))
    return o, st_new


def _gla_specs(specs, chunk, n_chunks, reverse):
    if reverse:
        row = lambda bi, ci: bi * n_chunks + (n_chunks - 1 - ci)
    else:
        row = lambda bi, ci: bi * n_chunks + ci
    return [pl.BlockSpec((chunk, w), lambda bi, ci, cb=cb: (row(bi, ci), cb)) for _, w, cb in specs], row


def _gla_fwd(q, k, v, lg, *, heads, dk, dv, scale, chunk, seq, name):
    t_rows = q[0].shape[0]
    n_chunks = seq // chunk
    fn = functools.partial(_gla_chunk, scale=scale, chunk=chunk)

    def body(q_ref, k_ref, v_ref, lg_ref, o_ref, sts_ref, st_ref):
        @pl.when(pl.program_id(1) == 0)
        def _():
            st_ref[...] = jnp.zeros_like(st_ref)

        sts_ref[0] = st_ref[...]
        ks = [slice(h * dk, (h + 1) * dk) for h in range(heads)]
        vs = [slice(h * dv, (h + 1) * dv) for h in range(heads)]
        ins = [(q_ref[:, ks[h]].astype(F32), k_ref[:, ks[h]].astype(F32), v_ref[:, vs[h]].astype(F32), lg_ref[:, ks[h]],
                st_ref[vs[h], :]) for h in range(heads)]
        outs = [fn(*args) for args in ins]
        for h, (o, st) in enumerate(outs):
            o_ref[:, vs[h]] = o
            st_ref[vs[h], :] = st

    in_specs, row = _gla_specs([q, k, v, lg], chunk, n_chunks, False)
    return pl.pallas_call(
        body, name=name, grid=(t_rows // seq, n_chunks), in_specs=in_specs,
        out_specs=[pl.BlockSpec((chunk, heads * dv), lambda bi, ci: (row(bi, ci), 0)),
                   pl.BlockSpec((1, heads * dv, dk), lambda bi, ci: (row(bi, ci), 0, 0))],
        out_shape=[jax.ShapeDtypeStruct((t_rows, heads * dv), F32),
                   jax.ShapeDtypeStruct((t_rows // chunk, heads * dv, dk), F32)],
        scratch_shapes=[pltpu.VMEM((heads * dv, dk), F32)],
        compiler_params=pltpu.CompilerParams(dimension_semantics=("arbitrary", "arbitrary")),
    )(q[0], k[0], v[0], lg[0])


def _gla_bwd(q, k, v, lg, states, do, *, heads, dk, dv, scale, chunk, seq, out_dtypes, name):
    t_rows = q[0].shape[0]
    n_chunks = seq // chunk
    fn = functools.partial(_gla_chunk, scale=scale, chunk=chunk)

    def body(q_ref, k_ref, v_ref, lg_ref, do_ref, sts_ref, dq_ref, dk_ref, dv_ref, dlg_ref, dst_ref):
        @pl.when(pl.program_id(1) == 0)
        def _():
            dst_ref[...] = jnp.zeros_like(dst_ref)

        ks = [slice(h * dk, (h + 1) * dk) for h in range(heads)]
        vs = [slice(h * dv, (h + 1) * dv) for h in range(heads)]
        ins = [(q_ref[:, ks[h]].astype(F32), k_ref[:, ks[h]].astype(F32), v_ref[:, vs[h]].astype(F32), lg_ref[:, ks[h]],
                sts_ref[0, vs[h], :]) for h in range(heads)]
        cts = [(do_ref[:, vs[h]].astype(F32), dst_ref[vs[h], :]) for h in range(heads)]
        outs = [jax.vjp(fn, *ins[h])[1](cts[h]) for h in range(heads)]
        for h, (dq, dkk, dvv, dlg, dst) in enumerate(outs):
            dq_ref[:, ks[h]] = dq.astype(dq_ref.dtype)
            dk_ref[:, ks[h]] = dkk.astype(dk_ref.dtype)
            dv_ref[:, vs[h]] = dvv.astype(dv_ref.dtype)
            dlg_ref[:, ks[h]] = dlg
            dst_ref[vs[h], :] = dst

    do_view = (do, heads * dv, 0)
    in_specs, row = _gla_specs([q, k, v, lg, do_view], chunk, n_chunks, True)
    in_specs.append(pl.BlockSpec((1, heads * dv, dk), lambda bi, ci: (row(bi, ci), 0, 0)))
    wide = lambda w: pl.BlockSpec((chunk, w), lambda bi, ci: (row(bi, ci), 0))
    return pl.pallas_call(
        body, name=name, grid=(t_rows // seq, n_chunks), in_specs=in_specs,
        out_specs=[wide(heads * dk), wide(heads * dk), wide(heads * dv), wide(heads * dk)],
        out_shape=[jax.ShapeDtypeStruct((t_rows, heads * dk), out_dtypes[0]),
                   jax.ShapeDtypeStruct((t_rows, heads * dk), out_dtypes[1]),
                   jax.ShapeDtypeStruct((t_rows, heads * dv), out_dtypes[2]),
                   jax.ShapeDtypeStruct((t_rows, heads * dk), F32)],
        scratch_shapes=[pltpu.VMEM((heads * dv, dk), F32)],
        compiler_params=pltpu.CompilerParams(dimension_semantics=("arbitrary", "arbitrary")),
    )(q[0], k[0], v[0], lg[0], do, states)


def _head_rms_gate(o, r, g, heads):
    d = o.shape[1] // heads
    parts = [_rms(o[:, h * d:(h + 1) * d], g) for h in range(heads)]
    return jnp.concatenate(parts, axis=1) * _silu(r)


def _gla_gate(glr, w_g2p, b_g2):
    return _log_sigmoid(_bdot(glr, w_g2p) + b_g2) * (1.0 / GLA_GATE_NORM)


def _glu(a, gate, b_in):
    d = a.shape[1]
    return (a + b_in[:, :d]) * _sigmoid(gate + b_in[:, d:])


def _ln_silu(y, g, b):
    return _silu(_layer_norm(y, g, b))


def _sgu(pre, b_in, ln_g, ln_b, w_s, b_st):
    d = pre.shape[1] // 2
    gd = d // SGU_GROUPS
    uv = _gelu(pre + b_in)
    u, v = uv[:, :d], _layer_norm(uv[:, d:], ln_g, ln_b)
    row = lax.broadcasted_iota(jnp.int32, (SGU_CHUNK, SGU_CHUNK), 0)
    col = lax.broadcasted_iota(jnp.int32, (SGU_CHUNK, SGU_CHUNK), 1)
    lane = lax.broadcasted_iota(jnp.int32, b_st.shape, 1)
    rows = []
    for c in range(pre.shape[0] // SGU_CHUNK):
        rs = slice(c * SGU_CHUNK, (c + 1) * SGU_CHUNK)
        parts = []
        for g in range(SGU_GROUPS):
            wg = jnp.where(col <= row, w_s[g], 0.0)
            bias = jnp.sum(jnp.where(lane == g, b_st, 0.0), axis=1, keepdims=True)
            parts.append(_bdot(wg, v[rs, g * gd:(g + 1) * gd]) + bias)
        rows.append(jnp.concatenate(parts, axis=1))
    s = rows[0] if len(rows) == 1 else jnp.concatenate(rows, axis=0)
    return u * s


def _hgrn_pre(q, f, table, layer):
    t = table - jnp.max(table, axis=0, keepdims=True)
    e = jnp.exp(t)
    sm = e / jnp.sum(e, axis=0, keepdims=True)
    rows = lax.broadcasted_iota(jnp.int32, table.shape, 0)
    lb = jnp.sum(jnp.where((rows >= 1) & (rows <= layer), sm, 0.0), axis=0, keepdims=True)
    sf = _sigmoid(f)
    return _silu(q), (1.0 - lb) * (1.0 - sf), jnp.log(lb + (1.0 - lb) * sf)


def _ffn_fwd(x, w, seq, sv):
    sv["h2"] = _rms_fwd(x, w["norm"], "ffn_norm")
    sv["u"] = _mm(sv["h2"], w["w_up"], out_dtype=BF16, name="ffn_up")
    sv["a"] = _ffn_mid_fwd(sv["u"], w["w_dw"], seq, "ffn_mid")
    return _mm(sv["a"], w["w_down"], add=x, name="ffn_down")


def _ffn_bwd(x, dy, w, seq, sv, deps=()):
    g = {}
    da = _mm(dy, w["w_down_t"], out_dtype=BF16, name="ffn_down_dx", deps=deps)
    g["w_down"] = _mm_tn(sv["a"], dy, name="ffn_down_dw")
    dug, duv, dwg, dwv = _ffn_mid_bwd(sv["u"], da, w["w_dw"], seq, "ffn_mid_bwd")
    g["w_dw"] = jnp.concatenate([dwg, dwv], axis=1)
    g["w_up_gate"] = _mm_tn(sv["h2"], dug, name="ffn_up_dw")
    g["w_up_val"] = _mm_tn(sv["h2"], duv, name="ffn_up_dw")
    dh = _mm(dug, w["w_up_t_gate"], out_dtype=F32, name="ffn_up_dx")
    dh = _mm(duv, w["w_up_t_val"], add=dh, out_dtype=BF16, name="ffn_up_dx2")
    dx, g["norm"] = _rms_bwd(x, w["norm"], dh, dy, "ffn_norm_bwd")
    return dx, g


def _gla_layer_fwd(x, h, w, seq, sv):
    d = x.shape[1]
    dkt = d // 2
    dk, dv = dkt // GLA_HEADS, d // GLA_HEADS
    proj = _mm(h, w["w_main"], out_dtype=F32, name="gla_in")
    glr = _mm(h, w["w_glr"], out_dtype=BF16, name="gla_in_g")
    (lg,), _ = _tile_call("gla_gate", lambda a, b, c: ([_gla_gate(a.astype(F32), b, c)], []), [glr],
                          [w["w_g2p"], w["b_g2"]], [(dkt, F32)], [], 512)
    q, k, v, r = (proj, dkt, 0), (proj, dkt, 1), (proj, d, 1), (proj, d, 2)
    o, states = _gla_fwd(q, k, v, (lg, dkt, 0), heads=GLA_HEADS, dk=dk, dv=dv, scale=dk ** -0.5, chunk=GLA_CHUNK,
                         seq=seq, name="gla_core")
    (o2,), _ = _tile_call("gla_post", lambda ov, rv, gv: ([_head_rms_gate(ov, rv.astype(F32), gv, GLA_HEADS)], []),
                          [o, r], [w["norm"]], [(d, BF16)], [], 256)
    sv.update(proj=proj, glr=glr, lg=lg, o=o, states=states, o2=o2)
    return _mm(o2, w["w_out"], add=x, name="mix_out")


def _gla_layer_bwd(h, dy, w, seq, sv, deps=()):
    d = dy.shape[1]
    dkt = d // 2
    dk, dv = dkt // GLA_HEADS, d // GLA_HEADS
    proj, glr, lg, o = sv["proj"], sv["glr"], sv["lg"], sv["o"]
    g = {}
    do2 = _mm(dy, w["w_out_t"], out_dtype=F32, name="gla_out_dx", deps=deps)
    g["w_out"] = _mm_tn(sv["o2"], dy, name="mix_out_dw")

    def post_bwd(ov, rv, ctv, gv):
        _, vjp = jax.vjp(functools.partial(_head_rms_gate, heads=GLA_HEADS), ov, rv.astype(F32), gv)
        d_o, d_r, d_g = vjp(ctv.astype(F32))
        return [d_o, d_r], [d_g]

    (d_o, d_r), (g["norm"],) = _tile_call("gla_post_bwd", post_bwd, [o, (proj, d, 2), do2], [w["norm"]],
                                          [(d, F32), (d, BF16)], [w["norm"].shape], 256)
    q, k, v = (proj, dkt, 0), (proj, dkt, 1), (proj, d, 1)
    dq, dkk, dvv, dlg = _gla_bwd(q, k, v, (lg, dkt, 0), sv["states"], d_o, heads=GLA_HEADS, dk=dk, dv=dv,
                                 scale=dk ** -0.5, chunk=GLA_CHUNK, seq=seq, out_dtypes=(BF16, BF16, BF16),
                                 name="gla_core_bwd")

    def gate_bwd(glrv, ctv, wv, bv):
        _, vjp = jax.vjp(_gla_gate, glrv.astype(F32), wv, bv)
        d_glr, d_w, d_b = vjp(ctv)
        return [d_glr], [d_w, d_b]

    (dglr,), (g["w_g2p"], g["b_g2"]) = _tile_call("gla_gate_bwd", gate_bwd, [glr, dlg], [w["w_g2p"], w["b_g2"]],
                                                  [(LANES, BF16)], [w["w_g2p"].shape, w["b_g2"].shape], 512)
    dproj = jnp.concatenate([dq, dkk, dvv, d_r], axis=1)
    g["w_main"] = _mm_tn(h, dproj, name="gla_in_dw")
    g["w_glr"] = _mm_tn(h, dglr, name="gla_in_g_dw")
    dh = _mm(dproj, w["w_main_t"], out_dtype=F32, name="gla_in_dx")
    dh = _mm(dglr, w["w_glr_t"], add=dh, out_dtype=BF16, name="gla_in_g_dx")
    return dh, g


def _cv_layer_fwd(x, h, w, seq, sv):
    d = x.shape[1]
    pre = _mm(h, w["w_in"], out_dtype=BF16, name="cv_in")
    (y1,), _ = _tile_call("cv_glu", lambda a, gt, b: ([_glu(a.astype(F32), gt.astype(F32), b)], []),
                          [(pre, d, 0), (pre, d, 1)], [w["b_in"]], [(d, F32)], [], 512)
    y2 = _dwconv_fwd(y1, w["w_dw"], w["b_dw"], seq, "cv_conv")
    (y3,), _ = _tile_call("cv_ln", lambda y, a, b: ([_ln_silu(y, a, b)], []), [y2], [w["ln_g"], w["ln_b"]],
                          [(d, BF16)], [], 512)
    sv.update(pre=pre, y1=y1, y2=y2, y3=y3)
    return _mm(y3, w["w_out"], bias=w["b_out"], add=x, name="mix_out_b")


def _cv_layer_bwd(h, dy, w, seq, sv, deps=()):
    d = dy.shape[1]
    pre = sv["pre"]
    g = {}
    dy3 = _mm(dy, w["w_out_t"], out_dtype=BF16, name="mix_out_dx", deps=deps)
    g["w_out"] = _mm_tn(sv["y3"], dy, name="mix_out_dw")
    g["b_out"] = _colsum(dy, "bias_out_dw")

    def ln_bwd(yv, ctv, av, bv):
        _, vjp = jax.vjp(_ln_silu, yv, av, bv)
        d_y, d_a, d_b = vjp(ctv.astype(F32))
        return [d_y], [d_a, d_b]

    (dy2,), (g["ln_g"], g["ln_b"]) = _tile_call("cv_ln_bwd", ln_bwd, [sv["y2"], dy3], [w["ln_g"], w["ln_b"]],
                                                [(d, F32)], [w["ln_g"].shape, w["ln_b"].shape], 512)
    dy1, g["w_dw"], g["b_dw"] = _dwconv_bwd(sv["y1"], dy2, w["w_dw"], seq, "cv_conv_bwd")

    def glu_bwd(av, gv, ctv, bv):
        _, vjp = jax.vjp(_glu, av.astype(F32), gv.astype(F32), bv)
        d_a, d_g, d_b = vjp(ctv)
        return [jnp.concatenate([d_a, d_g], axis=1)], [d_b]

    (dpre,), (g["b_in"],) = _tile_call("cv_glu_bwd", glu_bwd, [(pre, d, 0), (pre, d, 1), dy1], [w["b_in"]],
                                       [(2 * d, BF16)], [w["b_in"].shape], 512)
    g["w_in"] = _mm_tn(h, dpre, name="in2_dw")
    dh = _mm(dpre, w["w_in_t"], out_dtype=BF16, name="in2_dx")
    return dh, g


def _sg_layer_fwd(x, h, w, seq, sv):
    d = x.shape[1]
    pre = _mm(h, w["w_in"], out_dtype=BF16, name="sg_in")
    pars = [w["b_in"], w["ln_g"], w["ln_b"], w["w_s"], w["b_st"]]
    (p,), _ = _tile_call("sg_gate", lambda pv, *ps: ([_sgu(pv.astype(F32), *ps)], []), [pre], pars, [(d, BF16)], [],
                         SGU_CHUNK)
    sv.update(pre=pre, p=p)
    return _mm(p, w["w_out"], bias=w["b_out"], add=x, name="mix_out_b")


def _sg_layer_bwd(h, dy, w, seq, sv, deps=()):
    d = dy.shape[1]
    g = {}
    dp = _mm(dy, w["w_out_t"], out_dtype=BF16, name="mix_out_dx", deps=deps)
    g["w_out"] = _mm_tn(sv["p"], dy, name="mix_out_dw")
    g["b_out"] = _colsum(dy, "bias_out_dw")
    pars = [w["b_in"], w["ln_g"], w["ln_b"], w["w_s"], w["b_st"]]

    def sgu_bwd(pv, ctv, *ps):
        _, vjp = jax.vjp(_sgu, pv.astype(F32), *ps)
        grads = vjp(ctv.astype(F32))
        return [grads[0]], list(grads[1:])

    (dpre,), (g["b_in"], g["ln_g"], g["ln_b"], g["w_s"], g["b_st"]) = _tile_call(
        "sg_gate_bwd", sgu_bwd, [sv["pre"], dp], pars, [(2 * d, BF16)], [p.shape for p in pars], SGU_CHUNK)
    g["w_in"] = _mm_tn(h, dpre, name="in2_dw")
    dh = _mm(dpre, w["w_in_t"], out_dtype=BF16, name="in2_dx")
    return dh, g


def _hg_layer_fwd(x, h, w, seq, sv, layer):
    d = x.shape[1]
    heads = d // HGRN_EXPAND
    proj = _mm(h, w["w_in"], out_dtype=BF16, name="hg_in")
    pre = functools.partial(_hgrn_pre, layer=layer)
    (qs, kk, lg), _ = _tile_call("hg_pre", lambda qv, fv, tb: (list(pre(qv.astype(F32), fv.astype(F32), tb)), []),
                                 [(proj, d, 0), (proj, d, 1)], [w["lb_table"]], [(d, BF16), (d, F32), (d, F32)], [], 256)
    o, states = _gla_fwd((qs, d, 0), (kk, d, 0), (proj, d, 2), (lg, d, 0), heads=heads, dk=HGRN_EXPAND,
                         dv=HGRN_EXPAND, scale=1.0, chunk=HGRN_CHUNK, seq=seq, name="hg_core")
    (o2,), _ = _tile_call("hg_post", lambda ov, gv, nv: ([_head_rms_gate(ov, gv.astype(F32), nv, heads)], []),
                          [o, (proj, d, 3)], [w["norm"]], [(d, BF16)], [], 256)
    sv.update(proj=proj, qs=qs, kk=kk, lg=lg, o=o, states=states, o2=o2)
    return _mm(o2, w["w_out"], add=x, name="mix_out")


def _hg_layer_bwd(h, dy, w, seq, sv, layer, deps=()):
    d = dy.shape[1]
    heads = d // HGRN_EXPAND
    proj = sv["proj"]
    g = {}
    do2 = _mm(dy, w["w_out_t"], out_dtype=BF16, name="mix_out_dx", deps=deps)
    g["w_out"] = _mm_tn(sv["o2"], dy, name="mix_out_dw")

    def post_bwd(ov, gv, ctv, nv):
        _, vjp = jax.vjp(functools.partial(_head_rms_gate, heads=heads), ov, gv.astype(F32), nv)
        d_o, d_g, d_n = vjp(ctv.astype(F32))
        return [d_o, d_g], [d_n]

    (d_o, d_gate), (g["norm"],) = _tile_call("hg_post_bwd", post_bwd, [sv["o"], (proj, d, 3), do2], [w["norm"]],
                                             [(d, F32), (d, BF16)], [w["norm"].shape], 256)
    dqs, dkk, di, dlg = _gla_bwd((sv["qs"], d, 0), (sv["kk"], d, 0), (proj, d, 2), (sv["lg"], d, 0), sv["states"], d_o,
                                 heads=heads, dk=HGRN_EXPAND, dv=HGRN_EXPAND, scale=1.0, chunk=HGRN_CHUNK, seq=seq,
                                 out_dtypes=(F32, F32, BF16), name="hg_core_bwd")

    def pre_bwd(qv, fv, c1, c2, c3, tb):
        _, vjp = jax.vjp(functools.partial(_hgrn_pre, layer=layer), qv.astype(F32), fv.astype(F32), tb)
        d_q, d_f, d_t = vjp((c1, c2, c3))
        return [jnp.concatenate([d_q, d_f], axis=1)], [d_t]

    (dqf,), (g["lb_table"],) = _tile_call("hg_pre_bwd", pre_bwd, [(proj, d, 0), (proj, d, 1), dqs, dkk, dlg],
                                          [w["lb_table"]], [(2 * d, BF16)], [w["lb_table"].shape], 256)
    dproj = jnp.concatenate([dqf, di, d_gate], axis=1)
    g["w_in"] = _mm_tn(h, dproj, name="hg_in_dw")
    dh = _mm(dproj, w["w_in_t"], out_dtype=BF16, name="hg_in_dx")
    return dh, g


_MIXERS = ("gla", "cv", "sg", "hg")


_BIG_KEYS = {"gla": ("w_main", "w_glr", "w_out"), "cv": ("w_in", "w_out"), "sg": ("w_in", "w_out"), "hg": ("w_in", "w_out"),
             "ffn": ("w_up_gate", "w_up_val", "w_down")}


def _local_step(x, target, w, seq, get_big, put_big, deps=()):
    depth = w["norm_mix"].shape[0]
    d = x.shape[1]
    saved, big = [], {}
    for layer in range(depth):
        mixer = _MIXERS[layer % 4]
        sv = {"x_in": x}
        sv["h"] = _rms_fwd(x, w["norm_mix"][layer:layer + 1], "mix_norm", deps if layer == 0 else ())
        big[mixer, layer] = get_big(mixer, layer, sv["h"])
        wm = dict(w[mixer], **big[mixer, layer])
        if mixer == "gla":
            x = _gla_layer_fwd(x, sv["h"], wm, seq, sv)
        elif mixer == "cv":
            x = _cv_layer_fwd(x, sv["h"], wm, seq, sv)
        elif mixer == "sg":
            x = _sg_layer_fwd(x, sv["h"], wm, seq, sv)
        else:
            x = _hg_layer_fwd(x, sv["h"], wm, seq, sv, layer)
        sv["x_mid"] = x
        sv["ffn"] = {}
        big["ffn", layer] = get_big("ffn", layer, x)
        wf = dict(w["ffn"][layer], norm=w["norm_ffn"][layer:layer + 1], **big["ffn", layer])
        x = _ffn_fwd(x, wf, seq, sv["ffn"])
        saved.append(sv)

    def head(xv, tv, gv):
        y, vjp = jax.vjp(_rms, xv, gv)
        err = y - tv
        dx, dg = vjp(err * (1.0 / d))
        part = 0.5 * jnp.sum(jnp.mean(err * err, axis=-1, keepdims=True), axis=0, keepdims=True)
        return [dx], [jnp.broadcast_to(part, (1, LANES)), dg]

    (dx,), (loss, g_final) = _tile_call("loss_head", head, [x, target], [w["norm_final"]], [(d, F32)],
                                        [(1, LANES), (1, d)], 512)
    grads = {"norm_final": g_final, "norm_mix": [None] * depth, "norm_ffn": [None] * depth, "ffn": [None] * depth}
    order = ()
    for layer in reversed(range(depth)):
        mixer = _MIXERS[layer % 4]
        sv = saved[layer]
        wf = dict(w["ffn"][layer], norm=w["norm_ffn"][layer:layer + 1], **big["ffn", layer])
        dx, gf = _ffn_bwd(sv["x_mid"], dx, wf, seq, sv["ffn"], order)
        order = put_big("ffn", layer, {k: gf.pop(k) for k in _BIG_KEYS["ffn"]})
        grads["norm_ffn"][layer] = gf.pop("norm")
        grads["ffn"][layer] = gf
        wm = dict(w[mixer], **big[mixer, layer])
        if mixer == "gla":
            dh, gm = _gla_layer_bwd(sv["h"], dx, wm, seq, sv, order)
        elif mixer == "cv":
            dh, gm = _cv_layer_bwd(sv["h"], dx, wm, seq, sv, order)
        elif mixer == "sg":
            dh, gm = _sg_layer_bwd(sv["h"], dx, wm, seq, sv, order)
        else:
            dh, gm = _hg_layer_bwd(sv["h"], dx, wm, seq, sv, layer, order)
        order = put_big(mixer, layer, {k: gm.pop(k) for k in _BIG_KEYS[mixer]})
        grads[mixer] = gm
        dx, grads["norm_mix"][layer] = _rms_bwd(sv["x_in"], w["norm_mix"][layer:layer + 1], dh, dx, "mix_norm_bwd")
    return loss, dx, grads


def _prep_small(p):
    row = lambda a: a.reshape(1, -1).astype(F32)
    w = {"norm_mix": p["norm_mix"].astype(F32), "norm_ffn": p["norm_ffn"].astype(F32), "norm_final": row(p["norm_final"])}
    w["gla"] = dict(w_g2p=jnp.pad(p["gla_w_g2"][0].astype(F32), ((0, LANES - GLA_RANK), (0, 0))), b_g2=row(p["gla_b_g2"]),
                    norm=row(p["gla_norm"]))
    w["cv"] = dict(b_in=row(p["cv_b_in"]), w_dw=p["cv_w_dw"][0].astype(F32), b_dw=row(p["cv_b_dw"]), ln_g=row(p["cv_ln_g"]),
                   ln_b=row(p["cv_ln_b"]), b_out=row(p["cv_b_out"]))
    b_st = jnp.pad(p["sg_b_s"][0].astype(F32).T, ((0, 0), (0, LANES - SGU_GROUPS)))
    w["sg"] = dict(b_in=row(p["sg_b_in"]), ln_g=row(p["sg_ln_g"]), ln_b=row(p["sg_ln_b"]), w_s=p["sg_w_s"][0].astype(F32),
                   b_st=b_st, b_out=row(p["sg_b_out"]))
    w["hg"] = dict(lb_table=p["hg_lb_table"].astype(F32), norm=row(p["hg_norm"]))
    w["ffn"] = [dict(w_dw=p["ffn_w_dw"][layer].astype(F32)) for layer in range(p["ffn_w_dw"].shape[0])]
    return w


def _small_grads(g):
    gla, cv, sg, hg = g["gla"], g["cv"], g["sg"], g["hg"]
    return {
        "norm_mix": jnp.concatenate(g["norm_mix"], axis=0), "norm_ffn": jnp.concatenate(g["norm_ffn"], axis=0),
        "norm_final": g["norm_final"][0],
        "gla_w_g2": gla["w_g2p"][:GLA_RANK][None], "gla_b_g2": gla["b_g2"], "gla_norm": gla["norm"],
        "cv_b_in": cv["b_in"], "cv_w_dw": cv["w_dw"][None], "cv_b_dw": cv["b_dw"], "cv_ln_g": cv["ln_g"],
        "cv_ln_b": cv["ln_b"], "cv_b_out": cv["b_out"],
        "sg_b_in": sg["b_in"], "sg_ln_g": sg["ln_g"], "sg_ln_b": sg["ln_b"], "sg_w_s": sg["w_s"][None],
        "sg_b_s": sg["b_st"][:, :SGU_GROUPS].T[None], "sg_b_out": sg["b_out"],
        "hg_lb_table": hg["lb_table"], "hg_norm": hg["norm"],
        "ffn_w_dw": jnp.stack([f["w_dw"] for f in g["ffn"]]),
    }


def _oriented(kind, mats):
    if kind == "ffn":
        (up, up_t), (down, down_t) = mats["w_up"], mats["w_down"]
        f = down.shape[0]
        return dict(w_up=up, w_up_t_gate=up_t[:f], w_up_t_val=up_t[f:], w_down=down, w_down_t=down_t)
    (w_in, w_in_t), (w_out, w_out_t) = mats["w_in"], mats["w_out"]
    if kind != "gla":
        return dict(w_in=w_in, w_in_t=w_in_t, w_out=w_out, w_out_t=w_out_t)
    n_main = w_in.shape[1] - GLA_RANK
    return dict(w_main=w_in[:, :n_main], w_glr=jnp.pad(w_in[:, n_main:], ((0, 0), (0, LANES - GLA_RANK))),
                w_main_t=w_in_t[:n_main], w_glr_t=jnp.pad(w_in_t[n_main:], ((0, LANES - GLA_RANK), (0, 0))),
                w_out=w_out, w_out_t=w_out_t)


def _all_gather(x, *, name):
    m_per, n = x.shape

    def body(x_ref, out_ref, send_sems, recv_sems, local_sem):
        mx, my, mc = lax.axis_index("x"), lax.axis_index("y"), lax.axis_index("c")
        me, sibling = (mx, my, mc), (mx, my, 1 - mc)
        chips = [(1 - mx, my), (mx, 1 - my), (1 - mx, 1 - my)]

        def rows(px, py, pc):
            return out_ref.at[pl.ds((4 * px + 2 * py + pc) * m_per, m_per), :]

        def copy(k, block, to, src=None):
            return pltpu.make_async_remote_copy(
                src_ref=rows(*block) if src is None else src, dst_ref=rows(*block), send_sem=send_sems.at[k],
                recv_sem=recv_sems.at[k], device_id=to, device_id_type=MESH)

        mine = pltpu.make_async_copy(x_ref, rows(*me), local_sem)
        mine.start()
        first = [copy(0, me, sibling, src=x_ref)]
        first += [copy(1 + j, me, (*chip, mc), src=x_ref) for j, chip in enumerate(chips)]
        for cp in first:
            cp.start()
        passed = [copy(4 + j, (*chip, mc), sibling) for j, chip in enumerate(chips)]
        for j, chip in enumerate(chips):
            copy(1 + j, (*chip, mc), me).wait_recv()
            passed[j].start()
        copy(0, sibling, me).wait_recv()
        for j, chip in enumerate(chips):
            copy(4 + j, (*chip, 1 - mc), me).wait_recv()
        for cp in first + passed:
            cp.wait_send()
        mine.wait()

    return pl.pallas_call(
        body, name=name, out_shape=jax.ShapeDtypeStruct((N_DEV * m_per, n), x.dtype),
        in_specs=[pl.BlockSpec(memory_space=pltpu.VMEM)], out_specs=pl.BlockSpec(memory_space=pltpu.VMEM),
        scratch_shapes=[pltpu.SemaphoreType.DMA((7,)), pltpu.SemaphoreType.DMA((7,)), pltpu.SemaphoreType.DMA],
    )(x)


def _my_index():
    return 4 * lax.axis_index("x") + 2 * lax.axis_index("y") + lax.axis_index("c")


def _gather_stage(srcs, *, name):
    n = len(srcs)

    def body(*refs):
        x_refs, out_refs = refs[:n], refs[n:2 * n]
        send_sems, recv_sems, local_sems = refs[2 * n:]
        mx, my, mc = lax.axis_index("x"), lax.axis_index("y"), lax.axis_index("c")
        me, sibling = (mx, my, mc), (mx, my, 1 - mc)
        chips = [(1 - mx, my), (mx, 1 - my), (1 - mx, 1 - my)]

        def slot(i, px, py, pc):
            return out_refs[i].at[4 * px + 2 * py + pc]

        def copy(i, k, block, to, src=None):
            return pltpu.make_async_remote_copy(
                src_ref=slot(i, *block) if src is None else src, dst_ref=slot(i, *block), send_sem=send_sems.at[7 * i + k],
                recv_sem=recv_sems.at[7 * i + k], device_id=to, device_id_type=MESH)

        mine = [pltpu.make_async_copy(x_refs[i], slot(i, *me), local_sems.at[i]) for i in range(n)]
        first = [copy(i, 0, me, sibling, src=x_refs[i]) for i in range(n)]
        first += [copy(i, 1 + j, me, (*chip, mc), src=x_refs[i]) for j, chip in enumerate(chips) for i in range(n)]
        for cp in mine + first:
            cp.start()
        passed = []
        for j, chip in enumerate(chips):
            for i in range(n):
                copy(i, 1 + j, (*chip, mc), me).wait_recv()
                passed.append(copy(i, 4 + j, (*chip, mc), sibling))
                passed[-1].start()
        for i in range(n):
            copy(i, 0, sibling, me).wait_recv()
            for j, chip in enumerate(chips):
                copy(i, 4 + j, (*chip, 1 - mc), me).wait_recv()
        for cp in first + passed:
            cp.wait_send()
        for cp in mine:
            cp.wait()

    any_space = pl.BlockSpec(memory_space=pl.ANY)
    return pl.pallas_call(
        body, name=name, out_shape=[jax.ShapeDtypeStruct((N_DEV,) + s.shape, s.dtype) for s in srcs],
        in_specs=[any_space] * n, out_specs=[any_space] * n,
        scratch_shapes=[pltpu.SemaphoreType.DMA((7 * n,)), pltpu.SemaphoreType.DMA((7 * n,)), pltpu.SemaphoreType.DMA((n,))],
    )(*srcs)


def _scatter_stage(srcs, *, name):
    n = len(srcs)

    def body(*refs):
        x_refs, out_refs = refs[:n], refs[n:2 * n]
        send_sems, recv_sems, local_sems = refs[2 * n:]
        mx, my, mc = lax.axis_index("x"), lax.axis_index("y"), lax.axis_index("c")
        me = 4 * mx + 2 * my + mc
        mine = [pltpu.make_async_copy(x_refs[i].at[me], out_refs[i].at[me], local_sems.at[i]) for i in range(n)]
        for cp in mine:
            cp.start()
        sends, recvs = [], []
        for k in range(1, N_DEV):
            px = 1 - mx if k & 4 else mx
            py = 1 - my if k & 2 else my
            pc = 1 - mc if k & 1 else mc
            peer = 4 * px + 2 * py + pc
            for i in range(n):
                sems = dict(send_sem=send_sems.at[7 * i + k - 1], recv_sem=recv_sems.at[7 * i + k - 1],
                            device_id=(px, py, pc), device_id_type=MESH)
                sends.append(pltpu.make_async_remote_copy(src_ref=x_refs[i].at[peer], dst_ref=out_refs[i].at[me], **sems))
                recvs.append(pltpu.make_async_remote_copy(src_ref=x_refs[i].at[me], dst_ref=out_refs[i].at[peer], **sems))
                sends[-1].start()
        for cp in recvs:
            cp.wait_recv()
        for cp in sends:
            cp.wait_send()
        for cp in mine:
            cp.wait()

    any_space = pl.BlockSpec(memory_space=pl.ANY)
    return pl.pallas_call(
        body, name=name, out_shape=[jax.ShapeDtypeStruct(s.shape, s.dtype) for s in srcs],
        in_specs=[any_space] * n, out_specs=[any_space] * n,
        scratch_shapes=[pltpu.SemaphoreType.DMA((7 * n,)), pltpu.SemaphoreType.DMA((7 * n,)), pltpu.SemaphoreType.DMA((n,))],
    )(*srcs)


def _adamw_math(g, w, m, v):
    c1, c2 = 1.0 - ADAM_B1 ** ADAM_STEP, 1.0 - ADAM_B2 ** ADAM_STEP
    m_new = ADAM_B1 * m + (1.0 - ADAM_B1) * g
    v_new = ADAM_B2 * v + (1.0 - ADAM_B2) * (g * g)
    delta = -ADAM_LR * ((m_new / c1) / (jnp.sqrt(v_new / c2) + ADAM_EPS) + ADAM_WD * w)
    return delta, m_new, v_new


def _adamw_big(slots, w, m, v, layer, *, name):
    _, r, c = slots.shape
    tr = _divisor_tile(r, max(8, (200 * 1024) // c // 8 * 8), 8)

    def body(s_ref, w_ref, m_ref, v_ref, g_out, d_out, m_out, v_out):
        g = s_ref[0].astype(F32)
        for p in range(1, N_DEV):
            g = g + s_ref[p].astype(F32)
        g_out[...] = g
        d_out[...], m_out[...], v_out[...] = _adamw_math(g, w_ref[...], m_ref[...], v_ref[...])

    blk = pl.BlockSpec((tr, c), lambda i: (i, 0))
    lay = pl.BlockSpec((None, tr, c), lambda i: (layer, i, 0))
    return pl.pallas_call(
        body, name=name, grid=(r // tr,), in_specs=[pl.BlockSpec((N_DEV, tr, c), lambda i: (0, i, 0)), lay, lay, lay],
        out_specs=[blk] * 4, out_shape=[jax.ShapeDtypeStruct((r, c), F32)] * 4,
        compiler_params=pltpu.CompilerParams(dimension_semantics=("parallel",)),
    )(slots, w, m, v)


def _sum_small(got, r_re, r_sh, *, name):
    per_dev = r_re + N_DEV * r_sh

    def body(got_ref, re_ref, sh_ref):
        mine = r_re + _my_index() * r_sh
        acc_re = got_ref[0:r_re, :]
        acc_sh = got_ref[pl.ds(pl.multiple_of(mine, 8), r_sh), :]
        for p in range(1, N_DEV):
            acc_re = acc_re + got_ref[p * per_dev:p * per_dev + r_re, :]
            acc_sh = acc_sh + got_ref[pl.ds(pl.multiple_of(p * per_dev + mine, 8), r_sh), :]
        re_ref[...] = acc_re
        sh_ref[...] = acc_sh

    return pl.pallas_call(body, name=name, out_shape=[jax.ShapeDtypeStruct((r_re, LANES), F32),
                                                       jax.ShapeDtypeStruct((r_sh, LANES), F32)])(got)


def _adamw_small(gs, ws, ms, vs, *, name):
    n = len(gs)

    def body(*refs):
        ins, outs = refs[:4 * n], refs[4 * n:]
        for i in range(n):
            res = _adamw_math(ins[i][...], ins[n + i][...], ins[2 * n + i][...], ins[3 * n + i][...])
            for j in range(3):
                outs[j * n + i][...] = res[j]

    out = pl.pallas_call(body, name=name, out_shape=[jax.ShapeDtypeStruct(a.shape, F32) for a in ws] * 3)(*gs, *ws, *ms, *vs)
    return out[:n], out[n:2 * n], out[2 * n:]


def _layout(shapes, row_align, total_align):
    lay, off = {}, 0
    for name, shape in shapes.items():
        size = int(np.prod(shape))
        rows = -(-size // LANES)
        rows = -(-rows // row_align) * row_align
        lay[name] = (off, rows, size, tuple(shape))
        off += rows
    return lay, -(-off // total_align) * total_align


def _pack(arrs, lay, total, dtype, lead=()):
    parts = []
    nl = len(lead)
    for name, (off, rows, size, shape) in lay.items():
        flat = arrs[name].astype(dtype).reshape(*lead, size)
        parts.append(jnp.pad(flat, [(0, 0)] * nl + [(0, rows * LANES - size)]).reshape(*lead, rows, LANES))
    used = sum(v[1] for v in lay.values())
    if total > used:
        parts.append(jnp.zeros((*lead, total - used, LANES), dtype))
    return jnp.concatenate(parts, axis=nl)


def _unpack(buf, lay, lead=()):
    out = {}
    nl = len(lead)
    for name, (off, rows, size, shape) in lay.items():
        part = lax.slice_in_dim(buf, off, off + rows, axis=nl).reshape(*lead, rows * LANES)
        out[name] = lax.slice_in_dim(part, 0, size, axis=nl).reshape(*lead, *shape)
    return out


_SHARD_AXIS = {
    "norm_mix": None, "norm_ffn": None, "norm_final": None, "gla_w_in": 2, "gla_w_g2": 2, "gla_b_g2": None,
    "gla_norm": None, "gla_w_out": 1, "cv_w_in": 2, "cv_b_in": 1, "cv_w_dw": 2, "cv_b_dw": 1, "cv_ln_g": 1,
    "cv_ln_b": 1, "cv_w_out": 1, "cv_b_out": 1, "sg_w_in": 2, "sg_b_in": 1, "sg_ln_g": 1, "sg_ln_b": 1, "sg_w_s": None,
    "sg_b_s": None, "sg_w_out": 1, "sg_b_out": 1, "hg_w_in": 2, "hg_lb_table": None, "hg_norm": None, "hg_w_out": 1,
    "ffn_w_up": 2, "ffn_w_dw": 2, "ffn_w_down": 1,
}
_MATMUL_WEIGHTS = ("gla_w_in", "gla_w_out", "cv_w_in", "cv_w_out", "sg_w_in", "sg_w_out", "hg_w_in", "hg_w_out",
                   "ffn_w_up", "ffn_w_down")
_NAMES = tuple(_SHARD_AXIS)


def kernel(x, norm_mix, norm_ffn, norm_final, gla_w_in, gla_w_g2, gla_b_g2, gla_norm, gla_w_out, cv_w_in, cv_b_in, cv_w_dw, cv_b_dw, cv_ln_g, cv_ln_b, cv_w_out, cv_b_out, sg_w_in, sg_b_in, sg_ln_g, sg_ln_b, sg_w_s, sg_b_s, sg_w_out, sg_b_out, hg_w_in, hg_lb_table, hg_norm, hg_w_out, ffn_w_up, ffn_w_dw, ffn_w_down, loss_target, m_norm_mix, m_norm_ffn, m_norm_final, m_gla_w_in, m_gla_w_g2, m_gla_b_g2, m_gla_norm, m_gla_w_out, m_cv_w_in, m_cv_b_in, m_cv_w_dw, m_cv_b_dw, m_cv_ln_g, m_cv_ln_b, m_cv_w_out, m_cv_b_out, m_sg_w_in, m_sg_b_in, m_sg_ln_g, m_sg_ln_b, m_sg_w_s, m_sg_b_s, m_sg_w_out, m_sg_b_out, m_hg_w_in, m_hg_lb_table, m_hg_norm, m_hg_w_out, m_ffn_w_up, m_ffn_w_dw, m_ffn_w_down, v_norm_mix, v_norm_ffn, v_norm_final, v_gla_w_in, v_gla_w_g2, v_gla_b_g2, v_gla_norm, v_gla_w_out, v_cv_w_in, v_cv_b_in, v_cv_w_dw, v_cv_b_dw, v_cv_ln_g, v_cv_ln_b, v_cv_w_out, v_cv_b_out, v_sg_w_in, v_sg_b_in, v_sg_ln_g, v_sg_ln_b, v_sg_w_s, v_sg_b_s, v_sg_w_out, v_sg_b_out, v_hg_w_in, v_hg_lb_table, v_hg_norm, v_hg_w_out, v_ffn_w_up, v_ffn_w_dw, v_ffn_w_down):
    local = dict(locals())
    wts = {n: local[n] for n in _NAMES}
    mom = {n: local["m_" + n] for n in _NAMES}
    var = {n: local["v_" + n] for n in _NAMES}
    small_all = [n for n in _NAMES if n not in _MATMUL_WEIGHTS]
    small_sharded = [n for n in small_all if _SHARD_AXIS[n] is not None]
    bsz, seq, d = x.shape
    depth = norm_mix.shape[0]

    stages = {}
    for layer in range(depth):
        kind = _MIXERS[layer % 4]
        stages[kind, layer] = {"w_in": (kind + "_w_in", layer // 4), "w_out": (kind + "_w_out", layer // 4)}
        stages["ffn", layer] = {"w_up": ("ffn_w_up", layer), "w_down": ("ffn_w_down", layer)}

    gathered = {}
    for (kind, layer), keys in stages.items():
        srcs = [wts[nm][idx].astype(BF16) for nm, idx in keys.values()]
        gathered[kind, layer] = _gather_stage(srcs, name="gather_" + kind)
    lay_sw, r_sw = _layout({n: wts[n].shape for n in small_sharded}, 8, 8)
    got_sw = _all_gather(_pack(wts, lay_sw, r_sw, F32), name="gather_small_weights")
    parts = _unpack(got_sw.reshape(N_DEV, r_sw, LANES), lay_sw, (N_DEV,))
    full_small = {n: wts[n] for n in small_all if _SHARD_AXIS[n] is None}
    for n in small_sharded:
        ax, shape = _SHARD_AXIS[n], wts[n].shape
        full_small[n] = jnp.moveaxis(parts[n], 0, ax).reshape(shape[:ax] + (N_DEV * shape[ax],) + shape[ax + 1:])

    def get_big(kind, layer, after):
        mats = {}
        for (key, (nm, _)), land in zip(stages[kind, layer].items(), gathered[kind, layer]):
            _, r, c = land.shape
            if _SHARD_AXIS[nm] == 2:
                mats[key] = (land.transpose(1, 0, 2).reshape(r, N_DEV * c), land.transpose(0, 2, 1).reshape(N_DEV * c, r))
            else:
                mats[key] = (land.reshape(N_DEV * r, c), land.reshape(N_DEV * r, c).T)
        return _oriented(kind, mats)

    sends = {}

    def put_big(kind, layer, g):
        if kind == "ffn":
            k, f = g["w_up_gate"].shape
            halves = [g[key].reshape(k, N_DEV // 2, 2 * f // N_DEV) for key in ("w_up_gate", "w_up_val")]
            w_in = jnp.concatenate(halves, axis=1)
        else:
            w_in = jnp.concatenate([g["w_main"], g["w_glr"][:, :GLA_RANK]], axis=1) if kind == "gla" else g["w_in"]
            w_in = w_in.reshape(w_in.shape[0], N_DEV, w_in.shape[1] // N_DEV)
        w_out = g["w_down"] if kind == "ffn" else g["w_out"]
        w_out = w_out.reshape(N_DEV, w_out.shape[0] // N_DEV, w_out.shape[1])
        sends[kind, layer] = [w_in.transpose(1, 0, 2).astype(BF16), w_out.astype(BF16)]
        return ()

    loss, dx, grads = _local_step(x.reshape(bsz * seq, d), loss_target.reshape(bsz * seq, d), _prep_small(full_small), seq,
                                  get_big, put_big)
    loss = lax.psum(loss[0, 0], ("x", "y", "c"))

    gs = _small_grads(grads)
    small_repl = [n for n in small_all if _SHARD_AXIS[n] is None]
    lay_re, r_re = _layout({n: wts[n].shape for n in small_repl}, 8, 8)
    slots = {}
    for n in small_sharded:
        ax, shape = _SHARD_AXIS[n], wts[n].shape
        slots[n] = jnp.moveaxis(gs[n].reshape(shape[:ax] + (N_DEV, shape[ax]) + shape[ax + 1:]), ax, 0)
    sent = jnp.concatenate([_pack(gs, lay_re, r_re, F32), _pack(slots, lay_sw, r_sw, F32, (N_DEV,)).reshape(-1, LANES)])
    sum_re, sum_sh = _sum_small(_all_gather(sent, name="gather_small_grads"), r_re, r_sw, name="sum_small_grads")
    g_own = _unpack(sum_re, lay_re)
    g_own.update(_unpack(sum_sh, lay_sw))
    two_d = lambda a: a.reshape(-1, a.shape[-1])
    upd = _adamw_small(*[[two_d(src[n]) for n in small_all] for src in (g_own, wts, mom, var)], name="adamw_small")
    results = {n: [g_own[n]] + [part[i].reshape(wts[n].shape) for part in upd] for i, n in enumerate(small_all)}

    per_layer = {}
    for (kind, layer), arrays in sends.items():
        lands = _scatter_stage(arrays, name="scatter_" + kind)
        for (nm, idx), land in zip(stages[kind, layer].values(), lands):
            three_d = lambda a: a.reshape((a.shape[0],) + land.shape[1:])
            per_layer.setdefault(nm, {})[idx] = _adamw_big(land, three_d(wts[nm]), three_d(mom[nm]), three_d(var[nm]), idx,
                                                           name="adamw_" + nm)
    for nm, by_idx in per_layer.items():
        outs = [by_idx[i] for i in range(len(by_idx))]
        results[nm] = [(outs[0][j] if len(outs) == 1 else jnp.stack([o[j] for o in outs])).reshape(wts[nm].shape)
                       for j in range(4)]
    out = [loss, dx.reshape(bsz, seq, d)]
    for j in range(4):
        out += [results[n][j] for n in _NAMES]
    return tuple(out)
```

```python
import functools
import math

import jax
import jax.numpy as jnp
import numpy as np
from jax import lax
from jax.experimental import pallas as pl
from jax.experimental.pallas import tpu as pltpu

F32 = jnp.float32
BF16 = jnp.bfloat16
EPS = 1e-6
N_DEV = 8
LANES = 128
SUBLANES_BF16 = 16
HALO = 32
GLA_HEADS, GLA_RANK, GLA_GATE_NORM, GLA_CHUNK = 4, 16, 16.0, 64
SGU_CHUNK, SGU_GROUPS = 128, 8
HGRN_EXPAND, HGRN_CHUNK = 128, 64
CONV_WIDTH, FFN_CONV_WIDTH = 31, 3
ADAM_LR, ADAM_B1, ADAM_B2, ADAM_EPS, ADAM_WD, ADAM_STEP = 0.001, 0.9, 0.999, 1e-08, 0.01, 10
MESH = pl.DeviceIdType.MESH


def _sigmoid(x):
    return 0.5 * (jnp.tanh(0.5 * x) + 1.0)


def _silu(x):
    return x * _sigmoid(x)


def _log_sigmoid(x):
    return jnp.minimum(x, 0.0) - jnp.log(1.0 + jnp.exp(-jnp.abs(x)))


def _gelu(x):
    return 0.5 * x * (1.0 + jnp.tanh(math.sqrt(2.0 / math.pi) * (x + 0.044715 * (x * x * x))))


def _rms(x, g):
    return x * lax.rsqrt(jnp.mean(x * x, axis=-1, keepdims=True) + EPS) * g


def _layer_norm(x, g, b):
    xc = x - jnp.mean(x, axis=-1, keepdims=True)
    return xc * lax.rsqrt(jnp.mean(xc * xc, axis=-1, keepdims=True) + EPS) * g + b


def _dot_raw(a, b, dims):
    return lax.dot_general(a.astype(BF16), b.astype(BF16), (dims, ((), ())), preferred_element_type=F32)


@jax.custom_vjp
def _bdot(a, b):
    return _dot_raw(a, b, ((1,), (0,)))


@jax.custom_vjp
def _bdot_nt(a, b):
    return _dot_raw(a, b, ((1,), (1,)))


@jax.custom_vjp
def _bdot_tn(a, b):
    return _dot_raw(a, b, ((0,), (0,)))


_bdot.defvjp(lambda a, b: (_bdot(a, b), (a, b)), lambda r, g: (_bdot_nt(g, r[1]), _bdot_tn(r[0], g)))
_bdot_nt.defvjp(lambda a, b: (_bdot_nt(a, b), (a, b)), lambda r, g: (_bdot(g, r[1]), _bdot_tn(g, r[0])))
_bdot_tn.defvjp(lambda a, b: (_bdot_tn(a, b), (a, b)), lambda r, g: (_bdot_nt(r[1], g), _bdot(r[0], g)))


def _scan_rows(x, reverse):
    n = x.shape[0]
    row = lax.broadcasted_iota(jnp.int32, x.shape, 0)
    step = 1
    while step < n:
        if reverse:
            x = x + jnp.where(row < n - step, pltpu.roll(x, n - step, 0), 0.0)
        else:
            x = x + jnp.where(row >= step, pltpu.roll(x, step, 0), 0.0)
        step *= 2
    return x


@jax.custom_vjp
def _cumsum_rows(x):
    return _scan_rows(x, False)


_cumsum_rows.defvjp(lambda x: (_scan_rows(x, False), None), lambda _, g: (_scan_rows(g, True),))


def _divisor_tile(n, cap, unit):
    if n <= cap:
        return n
    best = None
    for t in range(unit, cap + 1, unit):
        if n % t == 0:
            best = t
    assert best is not None, (n, cap, unit)
    return best


def _const_map(nd):
    return lambda *_: (0,) * nd


def _dep_specs(deps, grid_rank):
    return [pl.BlockSpec(d.shape, (lambda *_, nd=d.ndim: (0,) * nd)) for d in deps]


def _mm(a, b, *, add=None, bias=None, out_dtype=F32, name, deps=()):
    m, k = a.shape
    k2, n = b.shape
    assert k == k2
    tn = _divisor_tile(n, max(LANES, min(1408, (6 << 20) // (2 * k) // LANES * LANES)), LANES)
    tm = _divisor_tile(m, max(256, min(1024, (4 << 20) // (a.dtype.itemsize * k) // 256 * 256)), 8)
    has_bias, has_add = bias is not None, add is not None

    def body(*refs):
        a_ref, b_ref = refs[0], refs[1]
        o_ref = refs[-1]
        acc = jnp.dot(a_ref[...].astype(BF16), b_ref[...], preferred_element_type=F32)
        pos = 2
        if has_bias:
            acc = acc + refs[pos][...]
            pos += 1
        if has_add:
            acc = acc + refs[pos][...].astype(F32)
        o_ref[...] = acc.astype(o_ref.dtype)

    in_specs = [pl.BlockSpec((tm, k), lambda i, j: (i, 0)), pl.BlockSpec((k, tn), lambda i, j: (0, j))]
    args = [a, b]
    if has_bias:
        in_specs.append(pl.BlockSpec((1, tn), lambda i, j: (0, j)))
        args.append(bias)
    if has_add:
        in_specs.append(pl.BlockSpec((tm, tn), lambda i, j: (i, j)))
        args.append(add)
    in_specs += _dep_specs(deps, 2)
    args += list(deps)
    return pl.pallas_call(
        body, name=name, grid=(m // tm, n // tn), in_specs=in_specs,
        out_specs=pl.BlockSpec((tm, tn), lambda i, j: (i, j)),
        out_shape=jax.ShapeDtypeStruct((m, n), out_dtype),
        compiler_params=pltpu.CompilerParams(dimension_semantics=("parallel", "parallel")),
    )(*args)


def _mm_tn(a, g, *, name):
    m, k = a.shape
    m2, n = g.shape
    assert m == m2
    tk = _divisor_tile(k, 1408, LANES)
    tn = _divisor_tile(n, 1408, LANES)
    tm = _divisor_tile(m, 1024, 8)

    def body(a_ref, g_ref, o_ref):
        @pl.when(pl.program_id(2) == 0)
        def _():
            o_ref[...] = jnp.zeros_like(o_ref)

        o_ref[...] += _dot_raw(a_ref[...], g_ref[...], ((0,), (0,)))

    return pl.pallas_call(
        body, name=name, grid=(k // tk, n // tn, m // tm),
        in_specs=[pl.BlockSpec((tm, tk), lambda i, j, t: (t, i)), pl.BlockSpec((tm, tn), lambda i, j, t: (t, j))],
        out_specs=pl.BlockSpec((tk, tn), lambda i, j, t: (i, j)),
        out_shape=jax.ShapeDtypeStruct((k, n), F32),
        compiler_params=pltpu.CompilerParams(dimension_semantics=("parallel", "parallel", "arbitrary")),
    )(a, g)


def _tile_call(name, fn, tiled, params, out_tiled, out_acc, tile, deps=()):
    tiled = [t if isinstance(t, tuple) else (t, t.shape[1], 0) for t in tiled]
    t_rows = tiled[0][0].shape[0]
    tile = min(tile, t_rows)
    assert t_rows % tile == 0
    n_t, n_p, n_o, n_d = len(tiled), len(params), len(out_tiled), len(deps)

    def body(*refs):
        vals = [r[...] for r in refs[: n_t + n_p]]
        refs = refs[: n_t + n_p] + refs[n_t + n_p + n_d:]
        touts, aouts = fn(*vals)
        for r, v in zip(refs[n_t + n_p: n_t + n_p + n_o], touts):
            r[...] = v.astype(r.dtype)
        acc_refs = refs[n_t + n_p + n_o:]
        if acc_refs:
            @pl.when(pl.program_id(0) == 0)
            def _():
                for r in acc_refs:
                    r[...] = jnp.zeros_like(r)

            for r, v in zip(acc_refs, aouts):
                r[...] += v

    in_specs = [pl.BlockSpec((tile, w), lambda i, cb=cb: (i, cb)) for _, w, cb in tiled]
    in_specs += [pl.BlockSpec(p.shape, _const_map(p.ndim)) for p in params]
    in_specs += _dep_specs(deps, 1)
    out_specs = [pl.BlockSpec((tile, w), lambda i: (i, 0)) for w, _ in out_tiled]
    out_specs += [pl.BlockSpec(s, _const_map(len(s))) for s in out_acc]
    out_shape = [jax.ShapeDtypeStruct((t_rows, w), dt) for w, dt in out_tiled]
    out_shape += [jax.ShapeDtypeStruct(s, F32) for s in out_acc]
    res = pl.pallas_call(
        body, name=name, grid=(t_rows // tile,), in_specs=in_specs, out_specs=out_specs, out_shape=out_shape,
        compiler_params=pltpu.CompilerParams(dimension_semantics=("arbitrary" if out_acc else "parallel",)),
    )(*[t[0] for t in tiled], *params, *deps)
    return res[:n_o], res[n_o:]


def _rms_fwd(x, g, name, deps=()):
    (h,), _ = _tile_call(name, lambda xv, gv: ([_rms(xv, gv)], []), [x], [g], [(x.shape[1], BF16)], [], 512, deps)
    return h


def _rms_bwd(x, g, dh, dres, name):
    def fn(xv, dhv, drv, gv):
        _, vjp = jax.vjp(_rms, xv, gv)
        dx, dg = vjp(dhv.astype(F32))
        return [drv + dx], [dg]

    (dx,), (dg,) = _tile_call(name, fn, [x, dh, dres], [g], [(x.shape[1], F32)], [g.shape], 512)
    return dx, dg


def _colsum(x, name):
    _, (s,) = _tile_call(name, lambda xv: ([], [jnp.sum(xv.astype(F32), axis=0, keepdims=True)]), [x], [], [],
                         [(1, x.shape[1])], 512)
    return s


def _seq_flags(i, tiles_per_seq):
    pos = i % tiles_per_seq
    return pos == 0, pos == tiles_per_seq - 1


def _dwconv_fwd(x, w, b, seq, name):
    t_rows, ch = x.shape
    kw = w.shape[0]
    tile = min(512, seq)
    cb = _divisor_tile(ch, 256, LANES)
    tps, hb = seq // tile, tile // HALO

    def body(x_ref, halo_ref, w_ref, b_ref, y_ref, pad_ref):
        first, _ = _seq_flags(pl.program_id(0), tps)
        pad_ref[0:HALO, :] = jnp.where(first, 0.0, halo_ref[...])
        pad_ref[HALO:HALO + tile, :] = x_ref[...]
        acc = jnp.broadcast_to(b_ref[...], (tile, cb))
        for k in range(kw):
            acc = acc + pad_ref[pl.ds(HALO - (kw - 1) + k, tile), :] * w_ref[k:k + 1, :]
        y_ref[...] = acc

    return pl.pallas_call(
        body, name=name, grid=(t_rows // tile, ch // cb),
        in_specs=[pl.BlockSpec((tile, cb), lambda i, j: (i, j)),
                  pl.BlockSpec((HALO, cb), lambda i, j: (jnp.maximum(i * hb - 1, 0), j)),
                  pl.BlockSpec((kw, cb), lambda i, j: (0, j)), pl.BlockSpec((1, cb), lambda i, j: (0, j))],
        out_specs=pl.BlockSpec((tile, cb), lambda i, j: (i, j)),
        out_shape=jax.ShapeDtypeStruct((t_rows, ch), F32),
        scratch_shapes=[pltpu.VMEM((HALO + tile, cb), F32)],
        compiler_params=pltpu.CompilerParams(dimension_semantics=("parallel", "parallel")),
    )(x, x, w, b)


def _dwconv_bwd(x, dy, w, seq, name):
    t_rows, ch = x.shape
    kw = w.shape[0]
    tile = min(512, seq)
    cb = _divisor_tile(ch, 256, LANES)
    tps, hb, n_hb = seq // tile, tile // HALO, t_rows // HALO

    def body(x_ref, xh_ref, dy_ref, dyh_ref, w_ref, dx_ref, dw_ref, db_ref, xpad, dypad):
        i = pl.program_id(1)
        first, last = _seq_flags(i, tps)

        @pl.when(i == 0)
        def _():
            dw_ref[...] = jnp.zeros_like(dw_ref)
            db_ref[...] = jnp.zeros_like(db_ref)

        xpad[0:HALO, :] = jnp.where(first, 0.0, xh_ref[...])
        xpad[HALO:HALO + tile, :] = x_ref[...]
        dyv = dy_ref[...]
        dypad[0:tile, :] = dyv
        dypad[tile:tile + HALO, :] = jnp.where(last, 0.0, dyh_ref[...])
        acc = jnp.zeros((tile, cb), F32)
        for k in range(kw):
            acc = acc + dypad[pl.ds(kw - 1 - k, tile), :] * w_ref[k:k + 1, :]
            dw_ref[k:k + 1, :] += jnp.sum(dyv * xpad[pl.ds(HALO - (kw - 1) + k, tile), :], axis=0, keepdims=True)
        dx_ref[...] = acc
        db_ref[...] += jnp.sum(dyv, axis=0, keepdims=True)

    return pl.pallas_call(
        body, name=name, grid=(ch // cb, t_rows // tile),
        in_specs=[pl.BlockSpec((tile, cb), lambda j, i: (i, j)),
                  pl.BlockSpec((HALO, cb), lambda j, i: (jnp.maximum(i * hb - 1, 0), j)),
                  pl.BlockSpec((tile, cb), lambda j, i: (i, j)),
                  pl.BlockSpec((HALO, cb), lambda j, i: (jnp.minimum((i + 1) * hb, n_hb - 1), j)),
                  pl.BlockSpec((kw, cb), lambda j, i: (0, j))],
        out_specs=[pl.BlockSpec((tile, cb), lambda j, i: (i, j)), pl.BlockSpec((kw, cb), lambda j, i: (0, j)),
                   pl.BlockSpec((1, cb), lambda j, i: (0, j))],
        out_shape=[jax.ShapeDtypeStruct((t_rows, ch), F32), jax.ShapeDtypeStruct((kw, ch), F32),
                   jax.ShapeDtypeStruct((1, ch), F32)],
        scratch_shapes=[pltpu.VMEM((HALO + tile, cb), F32), pltpu.VMEM((tile + HALO, cb), F32)],
        compiler_params=pltpu.CompilerParams(dimension_semantics=("parallel", "arbitrary")),
    )(x, x, dy, dy, w)


_ROWS = SUBLANES_BF16


def _lane_chunks(width, cap=6 * LANES):
    return [slice(c0, min(c0 + cap, width)) for c0 in range(0, width, cap)]


def _tap_rows(w_ref, cols):
    return [w_ref[k:k + 1, cols] for k in range(FFN_CONV_WIDTH)]


def _conv3_at(pad, taps, row, cols):
    z = pad[pl.ds(row, _ROWS), cols] * taps[2]
    z = z + pad[pl.ds(row - 1, _ROWS), cols] * taps[1]
    return z + pad[pl.ds(row - 2, _ROWS), cols] * taps[0]


def _ffn_mid_fwd(u, w, seq, name):
    t_rows, f2 = u.shape
    f = f2 // 2
    tile = min(256, seq)
    cb = _divisor_tile(f, 1408, LANES)
    nj, tps, hb, hl = f // cb, seq // tile, tile // SUBLANES_BF16, SUBLANES_BF16

    def body(ug_ref, uv_ref, hg_ref, hv_ref, wg_ref, wv_ref, a_ref, gpad, vpad):
        first, _ = _seq_flags(pl.program_id(0), tps)
        for t_ref, h_ref, pad in ((ug_ref, hg_ref, gpad), (uv_ref, hv_ref, vpad)):
            pad[0:hl, :] = jnp.where(first, 0.0, h_ref[...].astype(F32))
            pad[hl:hl + tile, :] = t_ref[...].astype(F32)
        for cols in _lane_chunks(cb):
            wg, wv = _tap_rows(wg_ref, cols), _tap_rows(wv_ref, cols)
            for r0 in range(0, tile, _ROWS):
                zg = _conv3_at(gpad, wg, hl + r0, cols)
                zv = _conv3_at(vpad, wv, hl + r0, cols)
                half = 0.5 * zg
                a_ref[pl.ds(r0, _ROWS), cols] = ((jnp.tanh(half) + 1.0) * half * zv).astype(a_ref.dtype)

    halo_map = lambda off: (lambda i, j: (jnp.maximum(i * hb - 1, 0), j + off))
    return pl.pallas_call(
        body, name=name, grid=(t_rows // tile, nj),
        in_specs=[pl.BlockSpec((tile, cb), lambda i, j: (i, j)), pl.BlockSpec((tile, cb), lambda i, j: (i, j + nj)),
                  pl.BlockSpec((hl, cb), halo_map(0)), pl.BlockSpec((hl, cb), halo_map(nj)),
                  pl.BlockSpec((3, cb), lambda i, j: (0, j)), pl.BlockSpec((3, cb), lambda i, j: (0, j + nj))],
        out_specs=pl.BlockSpec((tile, cb), lambda i, j: (i, j)),
        out_shape=jax.ShapeDtypeStruct((t_rows, f), BF16),
        scratch_shapes=[pltpu.VMEM((hl + tile, cb), F32), pltpu.VMEM((hl + tile, cb), F32)],
        compiler_params=pltpu.CompilerParams(dimension_semantics=("parallel", "parallel")),
    )(u, u, u, u, w, w)


def _ffn_mid_bwd(u, da, w, seq, name):
    t_rows, f2 = u.shape
    f = f2 // 2
    tile = min(256, seq)
    cb = _divisor_tile(f, 1408, LANES)
    hl = SUBLANES_BF16
    nj, tps, hb, n_hb, ext = f // cb, seq // tile, tile // hl, t_rows // hl, tile + hl

    def body(ug_ref, uv_ref, pg_ref, pv_ref, ng_ref, nv_ref, da_ref, dan_ref, wg_ref, wv_ref,
             dug_ref, duv_ref, dwg_ref, dwv_ref, gpad, vpad, dzg, dzv):
        i = pl.program_id(1)
        first, last = _seq_flags(i, tps)

        @pl.when(i == 0)
        def _():
            dwg_ref[...] = jnp.zeros_like(dwg_ref)
            dwv_ref[...] = jnp.zeros_like(dwv_ref)

        for t_ref, p_ref, n_ref, pad in ((ug_ref, pg_ref, ng_ref, gpad), (uv_ref, pv_ref, nv_ref, vpad)):
            pad[0:hl, :] = jnp.where(first, 0.0, p_ref[...].astype(F32))
            pad[hl:hl + tile, :] = t_ref[...].astype(F32)
            pad[hl + tile:hl + ext, :] = jnp.where(last, 0.0, n_ref[...].astype(F32))
        for cols in _lane_chunks(cb):
            wg, wv = _tap_rows(wg_ref, cols), _tap_rows(wv_ref, cols)
            for r0 in range(0, ext, _ROWS):
                zg = _conv3_at(gpad, wg, hl + r0, cols)
                zv = _conv3_at(vpad, wv, hl + r0, cols)
                if r0 < tile:
                    da = da_ref[pl.ds(r0, _ROWS), cols].astype(F32)
                else:
                    da = jnp.where(last, 0.0, dan_ref[:, cols].astype(F32))
                sg = _sigmoid(zg)
                dzg[pl.ds(r0, _ROWS), cols] = da * zv * (sg * (1.0 + zg * (1.0 - sg)))
                dzv[pl.ds(r0, _ROWS), cols] = da * (zg * sg)
        for dz, w_ref, pad, du_ref, dw_ref in ((dzg, wg_ref, gpad, dug_ref, dwg_ref), (dzv, wv_ref, vpad, duv_ref, dwv_ref)):
            for cols in _lane_chunks(cb):
                taps = _tap_rows(w_ref, cols)
                width = cols.stop - cols.start
                acc = [jnp.zeros((8, width), F32) for _ in range(FFN_CONV_WIDTH)]
                for r0 in range(0, tile, _ROWS):
                    d0 = dz[pl.ds(r0, _ROWS), cols]
                    du = dz[pl.ds(r0 + 2, _ROWS), cols] * taps[0] + dz[pl.ds(r0 + 1, _ROWS), cols] * taps[1] + d0 * taps[2]
                    du_ref[pl.ds(r0, _ROWS), cols] = du.astype(du_ref.dtype)
                    for k in range(FFN_CONV_WIDTH):
                        prod = d0 * pad[pl.ds(hl - 2 + k + r0, _ROWS), cols]
                        acc[k] = acc[k] + prod[0:8] + prod[8:16]
                for k in range(FFN_CONV_WIDTH):
                    dw_ref[k:k + 1, cols] += jnp.sum(acc[k], axis=0, keepdims=True)

    prev_map = lambda off: (lambda j, i: (jnp.maximum(i * hb - 1, 0), j + off))
    next_map = lambda off: (lambda j, i: (jnp.minimum((i + 1) * hb, n_hb - 1), j + off))
    tile_spec = lambda off: pl.BlockSpec((tile, cb), lambda j, i: (i, j + off))
    w_spec = lambda off: pl.BlockSpec((3, cb), lambda j, i: (0, j + off))
    return pl.pallas_call(
        body, name=name, grid=(nj, t_rows // tile),
        in_specs=[tile_spec(0), tile_spec(nj), pl.BlockSpec((hl, cb), prev_map(0)), pl.BlockSpec((hl, cb), prev_map(nj)),
                  pl.BlockSpec((hl, cb), next_map(0)), pl.BlockSpec((hl, cb), next_map(nj)),
                  tile_spec(0), pl.BlockSpec((hl, cb), next_map(0)), w_spec(0), w_spec(nj)],
        out_specs=[tile_spec(0), tile_spec(0), w_spec(0), w_spec(0)],
        out_shape=[jax.ShapeDtypeStruct((t_rows, f), BF16), jax.ShapeDtypeStruct((t_rows, f), BF16),
                   jax.ShapeDtypeStruct((3, f), F32), jax.ShapeDtypeStruct((3, f), F32)],
        scratch_shapes=[pltpu.VMEM((hl + ext, cb), F32), pltpu.VMEM((hl + ext, cb), F32),
                        pltpu.VMEM((ext, cb), F32), pltpu.VMEM((ext, cb), F32)],
        compiler_params=pltpu.CompilerParams(dimension_semantics=("parallel", "arbitrary")),
    )(u, u, u, u, u, u, da, da, w, w)


def _gla_chunk(q, k, v, lg, st, *, scale, chunk):
    row = lax.broadcasted_iota(jnp.int32, (chunk, chunk), 0)
    col = lax.broadcasted_iota(jnp.int32, (chunk, chunk), 1)
    causal = col <= row
    b = _cumsum_rows(lg)
    upto_mid = lax.broadcasted_iota(jnp.int32, lg.shape, 0) <= chunk // 2
    b_mid = jnp.sum(jnp.where(upto_mid, lg, 0.0), axis=0, keepdims=True)
    b_last = jnp.sum(lg, axis=0, keepdims=True)
    qs = q * scale
    scores = _bdot_nt(qs * jnp.exp(b - b_mid), k * jnp.exp(b_mid - b))
    o = _bdot(jnp.where(causal, scores, 0.0), v)
    o = o + _bdot_nt(qs * jnp.exp(b), st)
    st_new = st * jnp.exp(b_last) + _bdot_tn(v, k * jnp.exp(b_last - b))
    return o, st_new


def _gla_specs(specs, chunk, n_chunks, reverse):
    if reverse:
        row = lambda bi, ci: bi * n_chunks + (n_chunks - 1 - ci)
    else:
        row = lambda bi, ci: bi * n_chunks + ci
    return [pl.BlockSpec((chunk, w), lambda bi, ci, cb=cb: (row(bi, ci), cb)) for _, w, cb in specs], row


def _gla_fwd(q, k, v, lg, *, heads, dk, dv, scale, chunk, seq, name):
    t_rows = q[0].shape[0]
    n_chunks = seq // chunk
    fn = functools.partial(_gla_chunk, scale=scale, chunk=chunk)

    def body(q_ref, k_ref, v_ref, lg_ref, o_ref, sts_ref, st_ref):
        @pl.when(pl.program_id(1) == 0)
        def _():
            st_ref[...] = jnp.zeros_like(st_ref)

        sts_ref[0] = st_ref[...]
        ks = [slice(h * dk, (h + 1) * dk) for h in range(heads)]
        vs = [slice(h * dv, (h + 1) * dv) for h in range(heads)]
        ins = [(q_ref[:, ks[h]].astype(F32), k_ref[:, ks[h]].astype(F32), v_ref[:, vs[h]].astype(F32), lg_ref[:, ks[h]],
                st_ref[vs[h], :]) for h in range(heads)]
        outs = [fn(*args) for args in ins]
        for h, (o, st) in enumerate(outs):
            o_ref[:, vs[h]] = o
            st_ref[vs[h], :] = st

    in_specs, row = _gla_specs([q, k, v, lg], chunk, n_chunks, False)
    return pl.pallas_call(
        body, name=name, grid=(t_rows // seq, n_chunks), in_specs=in_specs,
        out_specs=[pl.BlockSpec((chunk, heads * dv), lambda bi, ci: (row(bi, ci), 0)),
                   pl.BlockSpec((1, heads * dv, dk), lambda bi, ci: (row(bi, ci), 0, 0))],
        out_shape=[jax.ShapeDtypeStruct((t_rows, heads * dv), F32),
                   jax.ShapeDtypeStruct((t_rows // chunk, heads * dv, dk), F32)],
        scratch_shapes=[pltpu.VMEM((heads * dv, dk), F32)],
        compiler_params=pltpu.CompilerParams(dimension_semantics=("arbitrary", "arbitrary")),
    )(q[0], k[0], v[0], lg[0])


def _gla_bwd(q, k, v, lg, states, do, *, heads, dk, dv, scale, chunk, seq, out_dtypes, name):
    t_rows = q[0].shape[0]
    n_chunks = seq // chunk
    fn = functools.partial(_gla_chunk, scale=scale, chunk=chunk)

    def body(q_ref, k_ref, v_ref, lg_ref, do_ref, sts_ref, dq_ref, dk_ref, dv_ref, dlg_ref, dst_ref):
        @pl.when(pl.program_id(1) == 0)
        def _():
            dst_ref[...] = jnp.zeros_like(dst_ref)

        ks = [slice(h * dk, (h + 1) * dk) for h in range(heads)]
        vs = [slice(h * dv, (h + 1) * dv) for h in range(heads)]
        ins = [(q_ref[:, ks[h]].astype(F32), k_ref[:, ks[h]].astype(F32), v_ref[:, vs[h]].astype(F32), lg_ref[:, ks[h]],
                sts_ref[0, vs[h], :]) for h in range(heads)]
        cts = [(do_ref[:, vs[h]].astype(F32), dst_ref[vs[h], :]) for h in range(heads)]
        outs = [jax.vjp(fn, *ins[h])[1](cts[h]) for h in range(heads)]
        for h, (dq, dkk, dvv, dlg, dst) in enumerate(outs):
            dq_ref[:, ks[h]] = dq.astype(dq_ref.dtype)
            dk_ref[:, ks[h]] = dkk.astype(dk_ref.dtype)
            dv_ref[:, vs[h]] = dvv.astype(dv_ref.dtype)
            dlg_ref[:, ks[h]] = dlg
            dst_ref[vs[h], :] = dst

    do_view = (do, heads * dv, 0)
    in_specs, row = _gla_specs([q, k, v, lg, do_view], chunk, n_chunks, True)
    in_specs.append(pl.BlockSpec((1, heads * dv, dk), lambda bi, ci: (row(bi, ci), 0, 0)))
    wide = lambda w: pl.BlockSpec((chunk, w), lambda bi, ci: (row(bi, ci), 0))
    return pl.pallas_call(
        body, name=name, grid=(t_rows // seq, n_chunks), in_specs=in_specs,
        out_specs=[wide(heads * dk), wide(heads * dk), wide(heads * dv), wide(heads * dk)],
        out_shape=[jax.ShapeDtypeStruct((t_rows, heads * dk), out_dtypes[0]),
                   jax.ShapeDtypeStruct((t_rows, heads * dk), out_dtypes[1]),
                   jax.ShapeDtypeStruct((t_rows, heads * dv), out_dtypes[2]),
                   jax.ShapeDtypeStruct((t_rows, heads * dk), F32)],
        scratch_shapes=[pltpu.VMEM((heads * dv, dk), F32)],
        compiler_params=pltpu.CompilerParams(dimension_semantics=("arbitrary", "arbitrary")),
    )(q[0], k[0], v[0], lg[0], do, states)


def _head_rms_gate(o, r, g, heads):
    d = o.shape[1] // heads
    parts = [_rms(o[:, h * d:(h + 1) * d], g) for h in range(heads)]
    return jnp.concatenate(parts, axis=1) * _silu(r)


def _gla_gate(glr, w_g2p, b_g2):
    return _log_sigmoid(_bdot(glr, w_g2p) + b_g2) * (1.0 / GLA_GATE_NORM)


def _glu(a, gate, b_in):
    d = a.shape[1]
    return (a + b_in[:, :d]) * _sigmoid(gate + b_in[:, d:])


def _ln_silu(y, g, b):
    return _silu(_layer_norm(y, g, b))


def _sgu(pre, b_in, ln_g, ln_b, w_s, b_st):
    d = pre.shape[1] // 2
    gd = d // SGU_GROUPS
    uv = _gelu(pre + b_in)
    u, v = uv[:, :d], _layer_norm(uv[:, d:], ln_g, ln_b)
    row = lax.broadcasted_iota(jnp.int32, (SGU_CHUNK, SGU_CHUNK), 0)
    col = lax.broadcasted_iota(jnp.int32, (SGU_CHUNK, SGU_CHUNK), 1)
    lane = lax.broadcasted_iota(jnp.int32, b_st.shape, 1)
    rows = []
    for c in range(pre.shape[0] // SGU_CHUNK):
        rs = slice(c * SGU_CHUNK, (c + 1) * SGU_CHUNK)
        parts = []
        for g in range(SGU_GROUPS):
            wg = jnp.where(col <= row, w_s[g], 0.0)
            bias = jnp.sum(jnp.where(lane == g, b_st, 0.0), axis=1, keepdims=True)
            parts.append(_bdot(wg, v[rs, g * gd:(g + 1) * gd]) + bias)
        rows.append(jnp.concatenate(parts, axis=1))
    s = rows[0] if len(rows) == 1 else jnp.concatenate(rows, axis=0)
    return u * s


def _hgrn_pre(q, f, table, layer):
    t = table - jnp.max(table, axis=0, keepdims=True)
    e = jnp.exp(t)
    sm = e / jnp.sum(e, axis=0, keepdims=True)
    rows = lax.broadcasted_iota(jnp.int32, table.shape, 0)
    lb = jnp.sum(jnp.where((rows >= 1) & (rows <= layer), sm, 0.0), axis=0, keepdims=True)
    sf = _sigmoid(f)
    return _silu(q), (1.0 - lb) * (1.0 - sf), jnp.log(lb + (1.0 - lb) * sf)


def _ffn_fwd(x, w, seq, sv):
    sv["h2"] = _rms_fwd(x, w["norm"], "ffn_norm")
    sv["u"] = _mm(sv["h2"], w["w_up"], out_dtype=BF16, name="ffn_up")
    sv["a"] = _ffn_mid_fwd(sv["u"], w["w_dw"], seq, "ffn_mid")
    return _mm(sv["a"], w["w_down"], add=x, name="ffn_down")


def _ffn_bwd(x, dy, w, seq, sv, deps=()):
    g = {}
    da = _mm(dy, w["w_down_t"], out_dtype=BF16, name="ffn_down_dx", deps=deps)
    g["w_down"] = _mm_tn(sv["a"], dy, name="ffn_down_dw")
    dug, duv, dwg, dwv = _ffn_mid_bwd(sv["u"], da, w["w_dw"], seq, "ffn_mid_bwd")
    g["w_dw"] = jnp.concatenate([dwg, dwv], axis=1)
    g["w_up_gate"] = _mm_tn(sv["h2"], dug, name="ffn_up_dw")
    g["w_up_val"] = _mm_tn(sv["h2"], duv, name="ffn_up_dw")
    dh = _mm(dug, w["w_up_t_gate"], out_dtype=F32, name="ffn_up_dx")
    dh = _mm(duv, w["w_up_t_val"], add=dh, out_dtype=BF16, name="ffn_up_dx2")
    dx, g["norm"] = _rms_bwd(x, w["norm"], dh, dy, "ffn_norm_bwd")
    return dx, g


def _gla_layer_fwd(x, h, w, seq, sv):
    d = x.shape[1]
    dkt = d // 2
    dk, dv = dkt // GLA_HEADS, d // GLA_HEADS
    proj = _mm(h, w["w_main"], out_dtype=F32, name="gla_in")
    glr = _mm(h, w["w_glr"], out_dtype=BF16, name="gla_in_g")
    (lg,), _ = _tile_call("gla_gate", lambda a, b, c: ([_gla_gate(a.astype(F32), b, c)], []), [glr],
                          [w["w_g2p"], w["b_g2"]], [(dkt, F32)], [], 512)
    q, k, v, r = (proj, dkt, 0), (proj, dkt, 1), (proj, d, 1), (proj, d, 2)
    o, states = _gla_fwd(q, k, v, (lg, dkt, 0), heads=GLA_HEADS, dk=dk, dv=dv, scale=dk ** -0.5, chunk=GLA_CHUNK,
                         seq=seq, name="gla_core")
    (o2,), _ = _tile_call("gla_post", lambda ov, rv, gv: ([_head_rms_gate(ov, rv.astype(F32), gv, GLA_HEADS)], []),
                          [o, r], [w["norm"]], [(d, BF16)], [], 256)
    sv.update(proj=proj, glr=glr, lg=lg, o=o, states=states, o2=o2)
    return _mm(o2, w["w_out"], add=x, name="mix_out")


def _gla_layer_bwd(h, dy, w, seq, sv, deps=()):
    d = dy.shape[1]
    dkt = d // 2
    dk, dv = dkt // GLA_HEADS, d // GLA_HEADS
    proj, glr, lg, o = sv["proj"], sv["glr"], sv["lg"], sv["o"]
    g = {}
    do2 = _mm(dy, w["w_out_t"], out_dtype=F32, name="gla_out_dx", deps=deps)
    g["w_out"] = _mm_tn(sv["o2"], dy, name="mix_out_dw")

    def post_bwd(ov, rv, ctv, gv):
        _, vjp = jax.vjp(functools.partial(_head_rms_gate, heads=GLA_HEADS), ov, rv.astype(F32), gv)
        d_o, d_r, d_g = vjp(ctv.astype(F32))
        return [d_o, d_r], [d_g]

    (d_o, d_r), (g["norm"],) = _tile_call("gla_post_bwd", post_bwd, [o, (proj, d, 2), do2], [w["norm"]],
                                          [(d, F32), (d, BF16)], [w["norm"].shape], 256)
    q, k, v = (proj, dkt, 0), (proj, dkt, 1), (proj, d, 1)
    dq, dkk, dvv, dlg = _gla_bwd(q, k, v, (lg, dkt, 0), sv["states"], d_o, heads=GLA_HEADS, dk=dk, dv=dv,
                                 scale=dk ** -0.5, chunk=GLA_CHUNK, seq=seq, out_dtypes=(BF16, BF16, BF16),
                                 name="gla_core_bwd")

    def gate_bwd(glrv, ctv, wv, bv):
        _, vjp = jax.vjp(_gla_gate, glrv.astype(F32), wv, bv)
        d_glr, d_w, d_b = vjp(ctv)
        return [d_glr], [d_w, d_b]

    (dglr,), (g["w_g2p"], g["b_g2"]) = _tile_call("gla_gate_bwd", gate_bwd, [glr, dlg], [w["w_g2p"], w["b_g2"]],
                                                  [(LANES, BF16)], [w["w_g2p"].shape, w["b_g2"].shape], 512)
    dproj = jnp.concatenate([dq, dkk, dvv, d_r], axis=1)
    g["w_main"] = _mm_tn(h, dproj, name="gla_in_dw")
    g["w_glr"] = _mm_tn(h, dglr, name="gla_in_g_dw")
    dh = _mm(dproj, w["w_main_t"], out_dtype=F32, name="gla_in_dx")
    dh = _mm(dglr, w["w_glr_t"], add=dh, out_dtype=BF16, name="gla_in_g_dx")
    return dh, g


def _cv_layer_fwd(x, h, w, seq, sv):
    d = x.shape[1]
    pre = _mm(h, w["w_in"], out_dtype=BF16, name="cv_in")
    (y1,), _ = _tile_call("cv_glu", lambda a, gt, b: ([_glu(a.astype(F32), gt.astype(F32), b)], []),
                          [(pre, d, 0), (pre, d, 1)], [w["b_in"]], [(d, F32)], [], 512)
    y2 = _dwconv_fwd(y1, w["w_dw"], w["b_dw"], seq, "cv_conv")
    (y3,), _ = _tile_call("cv_ln", lambda y, a, b: ([_ln_silu(y, a, b)], []), [y2], [w["ln_g"], w["ln_b"]],
                          [(d, BF16)], [], 512)
    sv.update(pre=pre, y1=y1, y2=y2, y3=y3)
    return _mm(y3, w["w_out"], bias=w["b_out"], add=x, name="mix_out_b")


def _cv_layer_bwd(h, dy, w, seq, sv, deps=()):
    d = dy.shape[1]
    pre = sv["pre"]
    g = {}
    dy3 = _mm(dy, w["w_out_t"], out_dtype=BF16, name="mix_out_dx", deps=deps)
    g["w_out"] = _mm_tn(sv["y3"], dy, name="mix_out_dw")
    g["b_out"] = _colsum(dy, "bias_out_dw")

    def ln_bwd(yv, ctv, av, bv):
        _, vjp = jax.vjp(_ln_silu, yv, av, bv)
        d_y, d_a, d_b = vjp(ctv.astype(F32))
        return [d_y], [d_a, d_b]

    (dy2,), (g["ln_g"], g["ln_b"]) = _tile_call("cv_ln_bwd", ln_bwd, [sv["y2"], dy3], [w["ln_g"], w["ln_b"]],
                                                [(d, F32)], [w["ln_g"].shape, w["ln_b"].shape], 512)
    dy1, g["w_dw"], g["b_dw"] = _dwconv_bwd(sv["y1"], dy2, w["w_dw"], seq, "cv_conv_bwd")

    def glu_bwd(av, gv, ctv, bv):
        _, vjp = jax.vjp(_glu, av.astype(F32), gv.astype(F32), bv)
        d_a, d_g, d_b = vjp(ctv)
        return [jnp.concatenate([d_a, d_g], axis=1)], [d_b]

    (dpre,), (g["b_in"],) = _tile_call("cv_glu_bwd", glu_bwd, [(pre, d, 0), (pre, d, 1), dy1], [w["b_in"]],
                                       [(2 * d, BF16)], [w["b_in"].shape], 512)
    g["w_in"] = _mm_tn(h, dpre, name="in2_dw")
    dh = _mm(dpre, w["w_in_t"], out_dtype=BF16, name="in2_dx")
    return dh, g


def _sg_layer_fwd(x, h, w, seq, sv):
    d = x.shape[1]
    pre = _mm(h, w["w_in"], out_dtype=BF16, name="sg_in")
    pars = [w["b_in"], w["ln_g"], w["ln_b"], w["w_s"], w["b_st"]]
    (p,), _ = _tile_call("sg_gate", lambda pv, *ps: ([_sgu(pv.astype(F32), *ps)], []), [pre], pars, [(d, BF16)], [],
                         SGU_CHUNK)
    sv.update(pre=pre, p=p)
    return _mm(p, w["w_out"], bias=w["b_out"], add=x, name="mix_out_b")


def _sg_layer_bwd(h, dy, w, seq, sv, deps=()):
    d = dy.shape[1]
    g = {}
    dp = _mm(dy, w["w_out_t"], out_dtype=BF16, name="mix_out_dx", deps=deps)
    g["w_out"] = _mm_tn(sv["p"], dy, name="mix_out_dw")
    g["b_out"] = _colsum(dy, "bias_out_dw")
    pars = [w["b_in"], w["ln_g"], w["ln_b"], w["w_s"], w["b_st"]]

    def sgu_bwd(pv, ctv, *ps):
        _, vjp = jax.vjp(_sgu, pv.astype(F32), *ps)
        grads = vjp(ctv.astype(F32))
        return [grads[0]], list(grads[1:])

    (dpre,), (g["b_in"], g["ln_g"], g["ln_b"], g["w_s"], g["b_st"]) = _tile_call(
        "sg_gate_bwd", sgu_bwd, [sv["pre"], dp], pars, [(2 * d, BF16)], [p.shape for p in pars], SGU_CHUNK)
    g["w_in"] = _mm_tn(h, dpre, name="in2_dw")
    dh = _mm(dpre, w["w_in_t"], out_dtype=BF16, name="in2_dx")
    return dh, g


def _hg_layer_fwd(x, h, w, seq, sv, layer):
    d = x.shape[1]
    heads = d // HGRN_EXPAND
    proj = _mm(h, w["w_in"], out_dtype=BF16, name="hg_in")
    pre = functools.partial(_hgrn_pre, layer=layer)
    (qs, kk, lg), _ = _tile_call("hg_pre", lambda qv, fv, tb: (list(pre(qv.astype(F32), fv.astype(F32), tb)), []),
                                 [(proj, d, 0), (proj, d, 1)], [w["lb_table"]], [(d, BF16), (d, F32), (d, F32)], [], 256)
    o, states = _gla_fwd((qs, d, 0), (kk, d, 0), (proj, d, 2), (lg, d, 0), heads=heads, dk=HGRN_EXPAND,
                         dv=HGRN_EXPAND, scale=1.0, chunk=HGRN_CHUNK, seq=seq, name="hg_core")
    (o2,), _ = _tile_call("hg_post", lambda ov, gv, nv: ([_head_rms_gate(ov, gv.astype(F32), nv, heads)], []),
                          [o, (proj, d, 3)], [w["norm"]], [(d, BF16)], [], 256)
    sv.update(proj=proj, qs=qs, kk=kk, lg=lg, o=o, states=states, o2=o2)
    return _mm(o2, w["w_out"], add=x, name="mix_out")


def _hg_layer_bwd(h, dy, w, seq, sv, layer, deps=()):
    d = dy.shape[1]
    heads = d // HGRN_EXPAND
    proj = sv["proj"]
    g = {}
    do2 = _mm(dy, w["w_out_t"], out_dtype=BF16, name="mix_out_dx", deps=deps)
    g["w_out"] = _mm_tn(sv["o2"], dy, name="mix_out_dw")

    def post_bwd(ov, gv, ctv, nv):
        _, vjp = jax.vjp(functools.partial(_head_rms_gate, heads=heads), ov, gv.astype(F32), nv)
        d_o, d_g, d_n = vjp(ctv.astype(F32))
        return [d_o, d_g], [d_n]

    (d_o, d_gate), (g["norm"],) = _tile_call("hg_post_bwd", post_bwd, [sv["o"], (proj, d, 3), do2], [w["norm"]],
                                             [(d, F32), (d, BF16)], [w["norm"].shape], 256)
    dqs, dkk, di, dlg = _gla_bwd((sv["qs"], d, 0), (sv["kk"], d, 0), (proj, d, 2), (sv["lg"], d, 0), sv["states"], d_o,
                                 heads=heads, dk=HGRN_EXPAND, dv=HGRN_EXPAND, scale=1.0, chunk=HGRN_CHUNK, seq=seq,
                                 out_dtypes=(F32, F32, BF16), name="hg_core_bwd")

    def pre_bwd(qv, fv, c1, c2, c3, tb):
        _, vjp = jax.vjp(functools.partial(_hgrn_pre, layer=layer), qv.astype(F32), fv.astype(F32), tb)
        d_q, d_f, d_t = vjp((c1, c2, c3))
        return [jnp.concatenate([d_q, d_f], axis=1)], [d_t]

    (dqf,), (g["lb_table"],) = _tile_call("hg_pre_bwd", pre_bwd, [(proj, d, 0), (proj, d, 1), dqs, dkk, dlg],
                                          [w["lb_table"]], [(2 * d, BF16)], [w["lb_table"].shape], 256)
    dproj = jnp.concatenate([dqf, di, d_gate], axis=1)
    g["w_in"] = _mm_tn(h, dproj, name="hg_in_dw")
    dh = _mm(dproj, w["w_in_t"], out_dtype=BF16, name="hg_in_dx")
    return dh, g


_MIXERS = ("gla", "cv", "sg", "hg")


_BIG_KEYS = {"gla": ("w_main", "w_glr", "w_out"), "cv": ("w_in", "w_out"), "sg": ("w_in", "w_out"), "hg": ("w_in", "w_out"),
             "ffn": ("w_up_gate", "w_up_val", "w_down")}


def _local_step(x, target, w, seq, get_big, put_big, deps=()):
    depth = w["norm_mix"].shape[0]
    d = x.shape[1]
    saved, big = [], {}
    for layer in range(depth):
        mixer = _MIXERS[layer % 4]
        sv = {"x_in": x}
        sv["h"] = _rms_fwd(x, w["norm_mix"][layer:layer + 1], "mix_norm", deps if layer == 0 else ())
        big[mixer, layer] = get_big(mixer, layer, sv["h"])
        wm = dict(w[mixer], **big[mixer, layer])
        if mixer == "gla":
            x = _gla_layer_fwd(x, sv["h"], wm, seq, sv)
        elif mixer == "cv":
            x = _cv_layer_fwd(x, sv["h"], wm, seq, sv)
        elif mixer == "sg":
            x = _sg_layer_fwd(x, sv["h"], wm, seq, sv)
        else:
            x = _hg_layer_fwd(x, sv["h"], wm, seq, sv, layer)
        sv["x_mid"] = x
        sv["ffn"] = {}
        big["ffn", layer] = get_big("ffn", layer, x)
        wf = dict(w["ffn"][layer], norm=w["norm_ffn"][layer:layer + 1], **big["ffn", layer])
        x = _ffn_fwd(x, wf, seq, sv["ffn"])
        saved.append(sv)

    def head(xv, tv, gv):
        y, vjp = jax.vjp(_rms, xv, gv)
        err = y - tv
        dx, dg = vjp(err * (1.0 / d))
        part = 0.5 * jnp.sum(jnp.mean(err * err, axis=-1, keepdims=True), axis=0, keepdims=True)
        return [dx], [jnp.broadcast_to(part, (1, LANES)), dg]

    (dx,), (loss, g_final) = _tile_call("loss_head", head, [x, target], [w["norm_final"]], [(d, F32)],
                                        [(1, LANES), (1, d)], 512)
    grads = {"norm_final": g_final, "norm_mix": [None] * depth, "norm_ffn": [None] * depth, "ffn": [None] * depth}
    order = ()
    for layer in reversed(range(depth)):
        mixer = _MIXERS[layer % 4]
        sv = saved[layer]
        wf = dict(w["ffn"][layer], norm=w["norm_ffn"][layer:layer + 1], **big["ffn", layer])
        dx, gf = _ffn_bwd(sv["x_mid"], dx, wf, seq, sv["ffn"], order)
        order = put_big("ffn", layer, {k: gf.pop(k) for k in _BIG_KEYS["ffn"]})
        grads["norm_ffn"][layer] = gf.pop("norm")
        grads["ffn"][layer] = gf
        wm = dict(w[mixer], **big[mixer, layer])
        if mixer == "gla":
            dh, gm = _gla_layer_bwd(sv["h"], dx, wm, seq, sv, order)
        elif mixer == "cv":
            dh, gm = _cv_layer_bwd(sv["h"], dx, wm, seq, sv, order)
        elif mixer == "sg":
            dh, gm = _sg_layer_bwd(sv["h"], dx, wm, seq, sv, order)
        else:
            dh, gm = _hg_layer_bwd(sv["h"], dx, wm, seq, sv, layer, order)
        order = put_big(mixer, layer, {k: gm.pop(k) for k in _BIG_KEYS[mixer]})
        grads[mixer] = gm
        dx, grads["norm_mix"][layer] = _rms_bwd(sv["x_in"], w["norm_mix"][layer:layer + 1], dh, dx, "mix_norm_bwd")
    return loss, dx, grads


def _prep_small(p):
    row = lambda a: a.reshape(1, -1).astype(F32)
    w = {"norm_mix": p["norm_mix"].astype(F32), "norm_ffn": p["norm_ffn"].astype(F32), "norm_final": row(p["norm_final"])}
    w["gla"] = dict(w_g2p=jnp.pad(p["gla_w_g2"][0].astype(F32), ((0, LANES - GLA_RANK), (0, 0))), b_g2=row(p["gla_b_g2"]),
                    norm=row(p["gla_norm"]))
    w["cv"] = dict(b_in=row(p["cv_b_in"]), w_dw=p["cv_w_dw"][0].astype(F32), b_dw=row(p["cv_b_dw"]), ln_g=row(p["cv_ln_g"]),
                   ln_b=row(p["cv_ln_b"]), b_out=row(p["cv_b_out"]))
    b_st = jnp.pad(p["sg_b_s"][0].astype(F32).T, ((0, 0), (0, LANES - SGU_GROUPS)))
    w["sg"] = dict(b_in=row(p["sg_b_in"]), ln_g=row(p["sg_ln_g"]), ln_b=row(p["sg_ln_b"]), w_s=p["sg_w_s"][0].astype(F32),
                   b_st=b_st, b_out=row(p["sg_b_out"]))
    w["hg"] = dict(lb_table=p["hg_lb_table"].astype(F32), norm=row(p["hg_norm"]))
    w["ffn"] = [dict(w_dw=p["ffn_w_dw"][layer].astype(F32)) for layer in range(p["ffn_w_dw"].shape[0])]
    return w


def _small_grads(g):
    gla, cv, sg, hg = g["gla"], g["cv"], g["sg"], g["hg"]
    return {
        "norm_mix": jnp.concatenate(g["norm_mix"], axis=0), "norm_ffn": jnp.concatenate(g["norm_ffn"], axis=0),
        "norm_final": g["norm_final"][0],
        "gla_w_g2": gla["w_g2p"][:GLA_RANK][None], "gla_b_g2": gla["b_g2"], "gla_norm": gla["norm"],
        "cv_b_in": cv["b_in"], "cv_w_dw": cv["w_dw"][None], "cv_b_dw": cv["b_dw"], "cv_ln_g": cv["ln_g"],
        "cv_ln_b": cv["ln_b"], "cv_b_out": cv["b_out"],
        "sg_b_in": sg["b_in"], "sg_ln_g": sg["ln_g"], "sg_ln_b": sg["ln_b"], "sg_w_s": sg["w_s"][None],
        "sg_b_s": sg["b_st"][:, :SGU_GROUPS].T[None], "sg_b_out": sg["b_out"],
        "hg_lb_table": hg["lb_table"], "hg_norm": hg["norm"],
        "ffn_w_dw": jnp.stack([f["w_dw"] for f in g["ffn"]]),
    }


def _oriented(kind, mats):
    if kind == "ffn":
        (up, up_t), (down, down_t) = mats["w_up"], mats["w_down"]
        f = down.shape[0]
        return dict(w_up=up, w_up_t_gate=up_t[:f], w_up_t_val=up_t[f:], w_down=down, w_down_t=down_t)
    (w_in, w_in_t), (w_out, w_out_t) = mats["w_in"], mats["w_out"]
    if kind != "gla":
        return dict(w_in=w_in, w_in_t=w_in_t, w_out=w_out, w_out_t=w_out_t)
    n_main = w_in.shape[1] - GLA_RANK
    return dict(w_main=w_in[:, :n_main], w_glr=jnp.pad(w_in[:, n_main:], ((0, 0), (0, LANES - GLA_RANK))),
                w_main_t=w_in_t[:n_main], w_glr_t=jnp.pad(w_in_t[n_main:], ((0, LANES - GLA_RANK), (0, 0))),
                w_out=w_out, w_out_t=w_out_t)


def _all_gather(x, *, name):
    m_per, n = x.shape

    def body(x_ref, out_ref, send_sems, recv_sems, local_sem):
        mx, my, mc = lax.axis_index("x"), lax.axis_index("y"), lax.axis_index("c")
        me, sibling = (mx, my, mc), (mx, my, 1 - mc)
        chips = [(1 - mx, my), (mx, 1 - my), (1 - mx, 1 - my)]

        def rows(px, py, pc):
            return out_ref.at[pl.ds((4 * px + 2 * py + pc) * m_per, m_per), :]

        def copy(k, block, to, src=None):
            return pltpu.make_async_remote_copy(
                src_ref=rows(*block) if src is None else src, dst_ref=rows(*block), send_sem=send_sems.at[k],
                recv_sem=recv_sems.at[k], device_id=to, device_id_type=MESH)

        mine = pltpu.make_async_copy(x_ref, rows(*me), local_sem)
        mine.start()
        first = [copy(0, me, sibling, src=x_ref)]
        first += [copy(1 + j, me, (*chip, mc), src=x_ref) for j, chip in enumerate(chips)]
        for cp in first:
            cp.start()
        passed = [copy(4 + j, (*chip, mc), sibling) for j, chip in enumerate(chips)]
        for j, chip in enumerate(chips):
            copy(1 + j, (*chip, mc), me).wait_recv()
            passed[j].start()
        copy(0, sibling, me).wait_recv()
        for j, chip in enumerate(chips):
            copy(4 + j, (*chip, 1 - mc), me).wait_recv()
        for cp in first + passed:
            cp.wait_send()
        mine.wait()

    return pl.pallas_call(
        body, name=name, out_shape=jax.ShapeDtypeStruct((N_DEV * m_per, n), x.dtype),
        in_specs=[pl.BlockSpec(memory_space=pltpu.VMEM)], out_specs=pl.BlockSpec(memory_space=pltpu.VMEM),
        scratch_shapes=[pltpu.SemaphoreType.DMA((7,)), pltpu.SemaphoreType.DMA((7,)), pltpu.SemaphoreType.DMA],
    )(x)


def _my_index():
    return 4 * lax.axis_index("x") + 2 * lax.axis_index("y") + lax.axis_index("c")


def _gather_stage(srcs, *, name):
    n = len(srcs)

    def body(*refs):
        x_refs, out_refs = refs[:n], refs[n:2 * n]
        send_sems, recv_sems, local_sems = refs[2 * n:]
        mx, my, mc = lax.axis_index("x"), lax.axis_index("y"), lax.axis_index("c")
        me, sibling = (mx, my, mc), (mx, my, 1 - mc)
        chips = [(1 - mx, my), (mx, 1 - my), (1 - mx, 1 - my)]

        def slot(i, px, py, pc):
            return out_refs[i].at[4 * px + 2 * py + pc]

        def copy(i, k, block, to, src=None):
            return pltpu.make_async_remote_copy(
                src_ref=slot(i, *block) if src is None else src, dst_ref=slot(i, *block), send_sem=send_sems.at[7 * i + k],
                recv_sem=recv_sems.at[7 * i + k], device_id=to, device_id_type=MESH)

        mine = [pltpu.make_async_copy(x_refs[i], slot(i, *me), local_sems.at[i]) for i in range(n)]
        first = [copy(i, 0, me, sibling, src=x_refs[i]) for i in range(n)]
        first += [copy(i, 1 + j, me, (*chip, mc), src=x_refs[i]) for j, chip in enumerate(chips) for i in range(n)]
        for cp in mine + first:
            cp.start()
        passed = []
        for j, chip in enumerate(chips):
            for i in range(n):
                copy(i, 1 + j, (*chip, mc), me).wait_recv()
                passed.append(copy(i, 4 + j, (*chip, mc), sibling))
                passed[-1].start()
        for i in range(n):
            copy(i, 0, sibling, me).wait_recv()
            for j, chip in enumerate(chips):
                copy(i, 4 + j, (*chip, 1 - mc), me).wait_recv()
        for cp in first + passed:
            cp.wait_send()
        for cp in mine:
            cp.wait()

    any_space = pl.BlockSpec(memory_space=pl.ANY)
    return pl.pallas_call(
        body, name=name, out_shape=[jax.ShapeDtypeStruct((N_DEV,) + s.shape, s.dtype) for s in srcs],
        in_specs=[any_space] * n, out_specs=[any_space] * n,
        scratch_shapes=[pltpu.SemaphoreType.DMA((7 * n,)), pltpu.SemaphoreType.DMA((7 * n,)), pltpu.SemaphoreType.DMA((n,))],
    )(*srcs)


def _scatter_stage(srcs, *, name):
    n = len(srcs)

    def body(*refs):
        x_refs, out_refs = refs[:n], refs[n:2 * n]
        send_sems, recv_sems, local_sems = refs[2 * n:]
        mx, my, mc = lax.axis_index("x"), lax.axis_index("y"), lax.axis_index("c")
        me = 4 * mx + 2 * my + mc
        mine = [pltpu.make_async_copy(x_refs[i].at[me], out_refs[i].at[me], local_sems.at[i]) for i in range(n)]
        for cp in mine:
            cp.start()
        sends, recvs = [], []
        for k in range(1, N_DEV):
            px = 1 - mx if k & 4 else mx
            py = 1 - my if k & 2 else my
            pc = 1 - mc if k & 1 else mc
            peer = 4 * px + 2 * py + pc
            for i in range(n):
                sems = dict(send_sem=send_sems.at[7 * i + k - 1], recv_sem=recv_sems.at[7 * i + k - 1],
                            device_id=(px, py, pc), device_id_type=MESH)
                sends.append(pltpu.make_async_remote_copy(src_ref=x_refs[i].at[peer], dst_ref=out_refs[i].at[me], **sems))
                recvs.append(pltpu.make_async_remote_copy(src_ref=x_refs[i].at[me], dst_ref=out_refs[i].at[peer], **sems))
                sends[-1].start()
        for cp in recvs:
            cp.wait_recv()
        for cp in sends:
            cp.wait_send()
        for cp in mine:
            cp.wait()

    any_space = pl.BlockSpec(memory_space=pl.ANY)
    return pl.pallas_call(
        body, name=name, out_shape=[jax.ShapeDtypeStruct(s.shape, s.dtype) for s in srcs],
        in_specs=[any_space] * n, out_specs=[any_space] * n,
        scratch_shapes=[pltpu.SemaphoreType.DMA((7 * n,)), pltpu.SemaphoreType.DMA((7 * n,)), pltpu.SemaphoreType.DMA((n,))],
    )(*srcs)


def _adamw_math(g, w, m, v):
    c1, c2 = 1.0 - ADAM_B1 ** ADAM_STEP, 1.0 - ADAM_B2 ** ADAM_STEP
    m_new = ADAM_B1 * m + (1.0 - ADAM_B1) * g
    v_new = ADAM_B2 * v + (1.0 - ADAM_B2) * (g * g)
    delta = -ADAM_LR * ((m_new / c1) / (jnp.sqrt(v_new / c2) + ADAM_EPS) + ADAM_WD * w)
    return delta, m_new, v_new


def _adamw_big(slots, w, m, v, layer, *, name):
    _, r, c = slots.shape
    tr = _divisor_tile(r, max(8, (200 * 1024) // c // 8 * 8), 8)

    def body(s_ref, w_ref, m_ref, v_ref, g_out, d_out, m_out, v_out):
        g = s_ref[0].astype(F32)
        for p in range(1, N_DEV):
            g = g + s_ref[p].astype(F32)
        g_out[...] = g
        d_out[...], m_out[...], v_out[...] = _adamw_math(g, w_ref[...], m_ref[...], v_ref[...])

    blk = pl.BlockSpec((tr, c), lambda i: (i, 0))
    lay = pl.BlockSpec((None, tr, c), lambda i: (layer, i, 0))
    return pl.pallas_call(
        body, name=name, grid=(r // tr,), in_specs=[pl.BlockSpec((N_DEV, tr, c), lambda i: (0, i, 0)), lay, lay, lay],
        out_specs=[blk] * 4, out_shape=[jax.ShapeDtypeStruct((r, c), F32)] * 4,
        compiler_params=pltpu.CompilerParams(dimension_semantics=("parallel",)),
    )(slots, w, m, v)


def _sum_small(got, r_re, r_sh, *, name):
    per_dev = r_re + N_DEV * r_sh

    def body(got_ref, re_ref, sh_ref):
        mine = r_re + _my_index() * r_sh
        acc_re = got_ref[0:r_re, :]
        acc_sh = got_ref[pl.ds(pl.multiple_of(mine, 8), r_sh), :]
        for p in range(1, N_DEV):
            acc_re = acc_re + got_ref[p * per_dev:p * per_dev + r_re, :]
            acc_sh = acc_sh + got_ref[pl.ds(pl.multiple_of(p * per_dev + mine, 8), r_sh), :]
        re_ref[...] = acc_re
        sh_ref[...] = acc_sh

    return pl.pallas_call(body, name=name, out_shape=[jax.ShapeDtypeStruct((r_re, LANES), F32),
                                                       jax.ShapeDtypeStruct((r_sh, LANES), F32)])(got)


def _adamw_small(gs, ws, ms, vs, *, name):
    n = len(gs)

    def body(*refs):
        ins, outs = refs[:4 * n], refs[4 * n:]
        for i in range(n):
            res = _adamw_math(ins[i][...], ins[n + i][...], ins[2 * n + i][...], ins[3 * n + i][...])
            for j in range(3):
                outs[j * n + i][...] = res[j]

    out = pl.pallas_call(body, name=name, out_shape=[jax.ShapeDtypeStruct(a.shape, F32) for a in ws] * 3)(*gs, *ws, *ms, *vs)
    return out[:n], out[n:2 * n], out[2 * n:]


def _layout(shapes, row_align, total_align):
    lay, off = {}, 0
    for name, shape in shapes.items():
        size = int(np.prod(shape))
        rows = -(-size // LANES)
        rows = -(-rows // row_align) * row_align
        lay[name] = (off, rows, size, tuple(shape))
        off += rows
    return lay, -(-off // total_align) * total_align


def _pack(arrs, lay, total, dtype, lead=()):
    parts = []
    nl = len(lead)
    for name, (off, rows, size, shape) in lay.items():
        flat = arrs[name].astype(dtype).reshape(*lead, size)
        parts.append(jnp.pad(flat, [(0, 0)] * nl + [(0, rows * LANES - size)]).reshape(*lead, rows, LANES))
    used = sum(v[1] for v in lay.values())
    if total > used:
        parts.append(jnp.zeros((*lead, total - used, LANES), dtype))
    return jnp.concatenate(parts, axis=nl)


def _unpack(buf, lay, lead=()):
    out = {}
    nl = len(lead)
    for name, (off, rows, size, shape) in lay.items():
        part = lax.slice_in_dim(buf, off, off + rows, axis=nl).reshape(*lead, rows * LANES)
        out[name] = lax.slice_in_dim(part, 0, size, axis=nl).reshape(*lead, *shape)
    return out


_SHARD_AXIS = {
    "norm_mix": None, "norm_ffn": None, "norm_final": None, "gla_w_in": 2, "gla_w_g2": 2, "gla_b_g2": None,
    "gla_norm": None, "gla_w_out": 1, "cv_w_in": 2, "cv_b_in": 1, "cv_w_dw": 2, "cv_b_dw": 1, "cv_ln_g": 1,
    "cv_ln_b": 1, "cv_w_out": 1, "cv_b_out": 1, "sg_w_in": 2, "sg_b_in": 1, "sg_ln_g": 1, "sg_ln_b": 1, "sg_w_s": None,
    "sg_b_s": None, "sg_w_out": 1, "sg_b_out": 1, "hg_w_in": 2, "hg_lb_table": None, "hg_norm": None, "hg_w_out": 1,
    "ffn_w_up": 2, "ffn_w_dw": 2, "ffn_w_down": 1,
}
_MATMUL_WEIGHTS = ("gla_w_in", "gla_w_out", "cv_w_in", "cv_w_out", "sg_w_in", "sg_w_out", "hg_w_in", "hg_w_out",
                   "ffn_w_up", "ffn_w_down")
_NAMES = tuple(_SHARD_AXIS)


def kernel(x, norm_mix, norm_ffn, norm_final, gla_w_in, gla_w_g2, gla_b_g2, gla_norm, gla_w_out, cv_w_in, cv_b_in, cv_w_dw, cv_b_dw, cv_ln_g, cv_ln_b, cv_w_out, cv_b_out, sg_w_in, sg_b_in, sg_ln_g, sg_ln_b, sg_w_s, sg_b_s, sg_w_out, sg_b_out, hg_w_in, hg_lb_table, hg_norm, hg_w_out, ffn_w_up, ffn_w_dw, ffn_w_down, loss_target, m_norm_mix, m_norm_ffn, m_norm_final, m_gla_w_in, m_gla_w_g2, m_gla_b_g2, m_gla_norm, m_gla_w_out, m_cv_w_in, m_cv_b_in, m_cv_w_dw, m_cv_b_dw, m_cv_ln_g, m_cv_ln_b, m_cv_w_out, m_cv_b_out, m_sg_w_in, m_sg_b_in, m_sg_ln_g, m_sg_ln_b, m_sg_w_s, m_sg_b_s, m_sg_w_out, m_sg_b_out, m_hg_w_in, m_hg_lb_table, m_hg_norm, m_hg_w_out, m_ffn_w_up, m_ffn_w_dw, m_ffn_w_down, v_norm_mix, v_norm_ffn, v_norm_final, v_gla_w_in, v_gla_w_g2, v_gla_b_g2, v_gla_norm, v_gla_w_out, v_cv_w_in, v_cv_b_in, v_cv_w_dw, v_cv_b_dw, v_cv_ln_g, v_cv_ln_b, v_cv_w_out, v_cv_b_out, v_sg_w_in, v_sg_b_in, v_sg_ln_g, v_sg_ln_b, v_sg_w_s, v_sg_b_s, v_sg_w_out, v_sg_b_out, v_hg_w_in, v_hg_lb_table, v_hg_norm, v_hg_w_out, v_ffn_w_up, v_ffn_w_dw, v_ffn_w_down):
    local = dict(locals())
    wts = {n: local[n] for n in _NAMES}
    mom = {n: local["m_" + n] for n in _NAMES}
    var = {n: local["v_" + n] for n in _NAMES}
    small_all = [n for n in _NAMES if n not in _MATMUL_WEIGHTS]
    small_sharded = [n for n in small_all if _SHARD_AXIS[n] is not None]
    bsz, seq, d = x.shape
    depth = norm_mix.shape[0]

    stages = {}
    for layer in range(depth):
        kind = _MIXERS[layer % 4]
        stages[kind, layer] = {"w_in": (kind + "_w_in", layer // 4), "w_out": (kind + "_w_out", layer // 4)}
        stages["ffn", layer] = {"w_up": ("ffn_w_up", layer), "w_down": ("ffn_w_down", layer)}

    gathered = {}
    for (kind, layer), keys in stages.items():
        srcs = [wts[nm][idx].astype(BF16) for nm, idx in keys.values()]
        gathered[kind, layer] = _gather_stage(srcs, name="gather_" + kind)
    lay_sw, r_sw = _layout({n: wts[n].shape for n in small_sharded}, 8, 8)
    got_sw = _all_gather(_pack(wts, lay_sw, r_sw, F32), name="gather_small_weights")
    parts = _unpack(got_sw.reshape(N_DEV, r_sw, LANES), lay_sw, (N_DEV,))
    full_small = {n: wts[n] for n in small_all if _SHARD_AXIS[n] is None}
    for n in small_sharded:
        ax, shape = _SHARD_AXIS[n], wts[n].shape
        full_small[n] = jnp.moveaxis(parts[n], 0, ax).reshape(shape[:ax] + (N_DEV * shape[ax],) + shape[ax + 1:])

    def get_big(kind, layer, after):
        mats = {}
        for (key, (nm, _)), land in zip(stages[kind, layer].items(), gathered[kind, layer]):
            _, r, c = land.shape
            if _SHARD_AXIS[nm] == 2:
                mats[key] = (land.transpose(1, 0, 2).reshape(r, N_DEV * c), land.transpose(0, 2, 1).reshape(N_DEV * c, r))
            else:
                mats[key] = (land.reshape(N_DEV * r, c), land.reshape(N_DEV * r, c).T)
        return _oriented(kind, mats)

    sends = {}

    def put_big(kind, layer, g):
        if kind == "ffn":
            k, f = g["w_up_gate"].shape
            halves = [g[key].reshape(k, N_DEV // 2, 2 * f // N_DEV) for key in ("w_up_gate", "w_up_val")]
            w_in = jnp.concatenate(halves, axis=1)
        else:
            w_in = jnp.concatenate([g["w_main"], g["w_glr"][:, :GLA_RANK]], axis=1) if kind == "gla" else g["w_in"]
            w_in = w_in.reshape(w_in.shape[0], N_DEV, w_in.shape[1] // N_DEV)
        w_out = g["w_down"] if kind == "ffn" else g["w_out"]
        w_out = w_out.reshape(N_DEV, w_out.shape[0] // N_DEV, w_out.shape[1])
        sends[kind, layer] = [w_in.transpose(1, 0, 2).astype(BF16), w_out.astype(BF16)]
        return ()

    loss, dx, grads = _local_step(x.reshape(bsz * seq, d), loss_target.reshape(bsz * seq, d), _prep_small(full_small), seq,
                                  get_big, put_big)
    loss = lax.psum(loss[0, 0], ("x", "y", "c"))

    gs = _small_grads(grads)
    small_repl = [n for n in small_all if _SHARD_AXIS[n] is None]
    lay_re, r_re = _layout({n: wts[n].shape for n in small_repl}, 8, 8)
    slots = {}
    for n in small_sharded:
        ax, shape = _SHARD_AXIS[n], wts[n].shape
        slots[n] = jnp.moveaxis(gs[n].reshape(shape[:ax] + (N_DEV, shape[ax]) + shape[ax + 1:]), ax, 0)
    sent = jnp.concatenate([_pack(gs, lay_re, r_re, F32), _pack(slots, lay_sw, r_sw, F32, (N_DEV,)).reshape(-1, LANES)])
    sum_re, sum_sh = _sum_small(_all_gather(sent, name="gather_small_grads"), r_re, r_sw, name="sum_small_grads")
    g_own = _unpack(sum_re, lay_re)
    g_own.update(_unpack(sum_sh, lay_sw))
    two_d = lambda a: a.reshape(-1, a.shape[-1])
    upd = _adamw_small(*[[two_d(src[n]) for n in small_all] for src in (g_own, wts, mom, var)], name="adamw_small")
    results = {n: [g_own[n]] + [part[i].reshape(wts[n].shape) for part in upd] for i, n in enumerate(small_all)}

    per_layer = {}
    for (kind, layer), arrays in sends.items():
        lands = _scatter_stage(arrays, name="scatter_" + kind)
        for (nm, idx), land in zip(stages[kind, layer].values(), lands):
            three_d = lambda a: a.reshape((a.shape[0],) + land.shape[1:])
            per_layer.setdefault(nm, {})[idx] = _adamw_big(land, three_d(wts[nm]), three_d(mom[nm]), three_d(var[nm]), idx,
                                                           name="adamw_" + nm)
    for nm, by_idx in per_layer.items():
        outs = [by_idx[i] for i in range(len(by_idx))]
        results[nm] = [(outs[0][j] if len(outs) == 1 else jnp.stack([o[j] for o in outs])).reshape(wts[nm].shape)
                       for j in range(4)]
    out = [loss, dx.reshape(bsz, seq, d)]
    for j in range(4):
        out += [results[n][j] for n in _NAMES]
    return tuple(out)
```

```python
import functools
import math

import jax
import jax.numpy as jnp
import numpy as np
from jax import lax
from jax.experimental import pallas as pl
from jax.experimental.pallas import tpu as pltpu

F32 = jnp.float32
BF16 = jnp.bfloat16
EPS = 1e-6
N_DEV = 8
LANES = 128
SUBLANES_BF16 = 16
HALO = 32
GLA_HEADS, GLA_RANK, GLA_GATE_NORM, GLA_CHUNK = 4, 16, 16.0, 64
SGU_CHUNK, SGU_GROUPS = 128, 8
HGRN_EXPAND, HGRN_CHUNK = 128, 64
CONV_WIDTH, FFN_CONV_WIDTH = 31, 3
ADAM_LR, ADAM_B1, ADAM_B2, ADAM_EPS, ADAM_WD, ADAM_STEP = 0.001, 0.9, 0.999, 1e-08, 0.01, 10
MESH = pl.DeviceIdType.MESH


def _sigmoid(x):
    return 0.5 * (jnp.tanh(0.5 * x) + 1.0)


def _silu(x):
    return x * _sigmoid(x)


def _log_sigmoid(x):
    return jnp.minimum(x, 0.0) - jnp.log(1.0 + jnp.exp(-jnp.abs(x)))


def _gelu(x):
    return 0.5 * x * (1.0 + jnp.tanh(math.sqrt(2.0 / math.pi) * (x + 0.044715 * (x * x * x))))


def _rms(x, g):
    return x * lax.rsqrt(jnp.mean(x * x, axis=-1, keepdims=True) + EPS) * g


def _layer_norm(x, g, b):
    xc = x - jnp.mean(x, axis=-1, keepdims=True)
    return xc * lax.rsqrt(jnp.mean(xc * xc, axis=-1, keepdims=True) + EPS) * g + b


def _dot_raw(a, b, dims):
    return lax.dot_general(a.astype(BF16), b.astype(BF16), (dims, ((), ())), preferred_element_type=F32)


@jax.custom_vjp
def _bdot(a, b):
    return _dot_raw(a, b, ((1,), (0,)))


@jax.custom_vjp
def _bdot_nt(a, b):
    return _dot_raw(a, b, ((1,), (1,)))


@jax.custom_vjp
def _bdot_tn(a, b):
    return _dot_raw(a, b, ((0,), (0,)))


_bdot.defvjp(lambda a, b: (_bdot(a, b), (a, b)), lambda r, g: (_bdot_nt(g, r[1]), _bdot_tn(r[0], g)))
_bdot_nt.defvjp(lambda a, b: (_bdot_nt(a, b), (a, b)), lambda r, g: (_bdot(g, r[1]), _bdot_tn(g, r[0])))
_bdot_tn.defvjp(lambda a, b: (_bdot_tn(a, b), (a, b)), lambda r, g: (_bdot_nt(r[1], g), _bdot(r[0], g)))


def _scan_rows(x, reverse):
    n = x.shape[0]
    row = lax.broadcasted_iota(jnp.int32, x.shape, 0)
    step = 1
    while step < n:
        if reverse:
            x = x + jnp.where(row < n - step, pltpu.roll(x, n - step, 0), 0.0)
        else:
            x = x + jnp.where(row >= step, pltpu.roll(x, step, 0), 0.0)
        step *= 2
    return x


@jax.custom_vjp
def _cumsum_rows(x):
    return _scan_rows(x, False)


_cumsum_rows.defvjp(lambda x: (_scan_rows(x, False), None), lambda _, g: (_scan_rows(g, True),))


def _divisor_tile(n, cap, unit):
    if n <= cap:
        return n
    best = None
    for t in range(unit, cap + 1, unit):
        if n % t == 0:
            best = t
    assert best is not None, (n, cap, unit)
    return best


def _const_map(nd):
    return lambda *_: (0,) * nd


class _Side:
    def __init__(self, srcs, scatter):
        self.srcs, self.scatter, self.lands = list(srcs), scatter, None


def _pallas(body, side, *, name, grid, in_specs, out_specs, out_shape, scratch_shapes=(), semantics):
    if side is None:
        return pl.pallas_call(body, name=name, grid=grid, in_specs=in_specs, out_specs=out_specs, out_shape=out_shape,
                              scratch_shapes=list(scratch_shapes),
                              compiler_params=pltpu.CompilerParams(dimension_semantics=semantics))
    single = not isinstance(out_shape, (list, tuple))
    out_specs, out_shape = ([out_specs], [out_shape]) if single else (list(out_specs), list(out_shape))
    n, n_in, n_out, n_scr = len(side.srcs), len(in_specs), len(out_shape), len(scratch_shapes)
    lands = [jax.ShapeDtypeStruct((N_DEV,) + (s.shape[1:] if side.scatter else s.shape), s.dtype) for s in side.srcs]

    def body2(*refs):
        x_refs, land_refs = refs[n_in:n_in + n], refs[n_in + n + n_out:n_in + 2 * n + n_out]
        send_sems, recv_sems, local_sems = refs[-3:]
        steps = [pl.program_id(a) for a in range(len(grid))]
        first = functools.reduce(jnp.logical_and, [s == 0 for s in steps])
        last = functools.reduce(jnp.logical_and, [s == g - 1 for s, g in zip(steps, grid)])

        def copies():
            mx, my, mc = lax.axis_index("x"), lax.axis_index("y"), lax.axis_index("c")
            me = 4 * mx + 2 * my + mc
            mine = [pltpu.make_async_copy(x_refs[i].at[me] if side.scatter else x_refs[i], land_refs[i].at[me],
                                          local_sems.at[i]) for i in range(n)]
            sends, recvs = [], []
            for k in range(1, N_DEV):
                px = 1 - mx if k & 4 else mx
                py = 1 - my if k & 2 else my
                pc = 1 - mc if k & 1 else mc
                peer = 4 * px + 2 * py + pc
                for i in range(n):
                    sems = dict(send_sem=send_sems.at[7 * i + k - 1], recv_sem=recv_sems.at[7 * i + k - 1],
                                device_id=(px, py, pc), device_id_type=MESH)
                    src = x_refs[i].at[peer] if side.scatter else x_refs[i]
                    sends.append(pltpu.make_async_remote_copy(src_ref=src, dst_ref=land_refs[i].at[me], **sems))
                    recvs.append(pltpu.make_async_remote_copy(src_ref=src, dst_ref=land_refs[i].at[peer], **sems))
            return mine, sends, recvs

        @pl.when(first)
        def _():
            mine, sends, _ = copies()
            for cp in mine + sends:
                cp.start()

        body(*refs[:n_in], *refs[n_in + n:n_in + n + n_out], *refs[n_in + 2 * n + n_out:n_in + 2 * n + n_out + n_scr])

        @pl.when(last)
        def _():
            mine, sends, recvs = copies()
            for cp in recvs:
                cp.wait_recv()
            for cp in sends:
                cp.wait_send()
            for cp in mine:
                cp.wait()

    any_space = pl.BlockSpec(memory_space=pl.ANY)
    call = pl.pallas_call(
        body2, name=name, grid=grid, in_specs=list(in_specs) + [any_space] * n, out_specs=out_specs + [any_space] * n,
        out_shape=out_shape + lands,
        scratch_shapes=list(scratch_shapes) + [pltpu.SemaphoreType.DMA((7 * n,)), pltpu.SemaphoreType.DMA((7 * n,)),
                                               pltpu.SemaphoreType.DMA((n,))],
        compiler_params=pltpu.CompilerParams(dimension_semantics=("arbitrary",) * len(grid)))

    def run(*args):
        res = call(*args, *side.srcs)
        side.lands = list(res[n_out:])
        return res[0] if single else res[:n_out]

    return run


def _dep_specs(deps, grid_rank):
    return [pl.BlockSpec(d.shape, (lambda *_, nd=d.ndim: (0,) * nd)) for d in deps]


def _mm(a, b, *, add=None, bias=None, out_dtype=F32, name, deps=()):
    m, k = a.shape
    k2, n = b.shape
    assert k == k2
    tn = _divisor_tile(n, max(LANES, min(1408, (6 << 20) // (2 * k) // LANES * LANES)), LANES)
    tm = _divisor_tile(m, max(256, min(1024, (4 << 20) // (a.dtype.itemsize * k) // 256 * 256)), 8)
    has_bias, has_add = bias is not None, add is not None

    def body(*refs):
        a_ref, b_ref = refs[0], refs[1]
        o_ref = refs[-1]
        acc = jnp.dot(a_ref[...].astype(BF16), b_ref[...], preferred_element_type=F32)
        pos = 2
        if has_bias:
            acc = acc + refs[pos][...]
            pos += 1
        if has_add:
            acc = acc + refs[pos][...].astype(F32)
        o_ref[...] = acc.astype(o_ref.dtype)

    in_specs = [pl.BlockSpec((tm, k), lambda i, j: (i, 0)), pl.BlockSpec((k, tn), lambda i, j: (0, j))]
    args = [a, b]
    if has_bias:
        in_specs.append(pl.BlockSpec((1, tn), lambda i, j: (0, j)))
        args.append(bias)
    if has_add:
        in_specs.append(pl.BlockSpec((tm, tn), lambda i, j: (i, j)))
        args.append(add)
    in_specs += _dep_specs(deps, 2)
    args += list(deps)
    return pl.pallas_call(
        body, name=name, grid=(m // tm, n // tn), in_specs=in_specs,
        out_specs=pl.BlockSpec((tm, tn), lambda i, j: (i, j)),
        out_shape=jax.ShapeDtypeStruct((m, n), out_dtype),
        compiler_params=pltpu.CompilerParams(dimension_semantics=("parallel", "parallel")),
    )(*args)


def _mm_tn(a, g, *, name):
    m, k = a.shape
    m2, n = g.shape
    assert m == m2
    tk = _divisor_tile(k, 1408, LANES)
    tn = _divisor_tile(n, 1408, LANES)
    tm = _divisor_tile(m, 1024, 8)

    def body(a_ref, g_ref, o_ref):
        @pl.when(pl.program_id(2) == 0)
        def _():
            o_ref[...] = jnp.zeros_like(o_ref)

        o_ref[...] += _dot_raw(a_ref[...], g_ref[...], ((0,), (0,)))

    return pl.pallas_call(
        body, name=name, grid=(k // tk, n // tn, m // tm),
        in_specs=[pl.BlockSpec((tm, tk), lambda i, j, t: (t, i)), pl.BlockSpec((tm, tn), lambda i, j, t: (t, j))],
        out_specs=pl.BlockSpec((tk, tn), lambda i, j, t: (i, j)),
        out_shape=jax.ShapeDtypeStruct((k, n), F32),
        compiler_params=pltpu.CompilerParams(dimension_semantics=("parallel", "parallel", "arbitrary")),
    )(a, g)


def _tile_call(name, fn, tiled, params, out_tiled, out_acc, tile, deps=(), side=None):
    tiled = [t if isinstance(t, tuple) else (t, t.shape[1], 0) for t in tiled]
    t_rows = tiled[0][0].shape[0]
    tile = min(tile, t_rows)
    assert t_rows % tile == 0
    n_t, n_p, n_o, n_d = len(tiled), len(params), len(out_tiled), len(deps)

    def body(*refs):
        vals = [r[...] for r in refs[: n_t + n_p]]
        refs = refs[: n_t + n_p] + refs[n_t + n_p + n_d:]
        touts, aouts = fn(*vals)
        for r, v in zip(refs[n_t + n_p: n_t + n_p + n_o], touts):
            r[...] = v.astype(r.dtype)
        acc_refs = refs[n_t + n_p + n_o:]
        if acc_refs:
            @pl.when(pl.program_id(0) == 0)
            def _():
                for r in acc_refs:
                    r[...] = jnp.zeros_like(r)

            for r, v in zip(acc_refs, aouts):
                r[...] += v

    in_specs = [pl.BlockSpec((tile, w), lambda i, cb=cb: (i, cb)) for _, w, cb in tiled]
    in_specs += [pl.BlockSpec(p.shape, _const_map(p.ndim)) for p in params]
    in_specs += _dep_specs(deps, 1)
    out_specs = [pl.BlockSpec((tile, w), lambda i: (i, 0)) for w, _ in out_tiled]
    out_specs += [pl.BlockSpec(s, _const_map(len(s))) for s in out_acc]
    out_shape = [jax.ShapeDtypeStruct((t_rows, w), dt) for w, dt in out_tiled]
    out_shape += [jax.ShapeDtypeStruct(s, F32) for s in out_acc]
    res = _pallas(
        body, side, name=name, grid=(t_rows // tile,), in_specs=in_specs, out_specs=out_specs, out_shape=out_shape,
        semantics=("arbitrary" if out_acc else "parallel",),
    )(*[t[0] for t in tiled], *params, *deps)
    return res[:n_o], res[n_o:]


def _rms_fwd(x, g, name, deps=()):
    (h,), _ = _tile_call(name, lambda xv, gv: ([_rms(xv, gv)], []), [x], [g], [(x.shape[1], BF16)], [], 512, deps)
    return h


def _rms_bwd(x, g, dh, dres, name):
    def fn(xv, dhv, drv, gv):
        _, vjp = jax.vjp(_rms, xv, gv)
        dx, dg = vjp(dhv.astype(F32))
        return [drv + dx], [dg]

    (dx,), (dg,) = _tile_call(name, fn, [x, dh, dres], [g], [(x.shape[1], F32)], [g.shape], 512)
    return dx, dg


def _colsum(x, name):
    _, (s,) = _tile_call(name, lambda xv: ([], [jnp.sum(xv.astype(F32), axis=0, keepdims=True)]), [x], [], [],
                         [(1, x.shape[1])], 512)
    return s


def _seq_flags(i, tiles_per_seq):
    pos = i % tiles_per_seq
    return pos == 0, pos == tiles_per_seq - 1


def _dwconv_fwd(x, w, b, seq, name, side=None):
    t_rows, ch = x.shape
    kw = w.shape[0]
    tile = min(512, seq)
    cb = _divisor_tile(ch, 256, LANES)
    tps, hb = seq // tile, tile // HALO

    def body(x_ref, halo_ref, w_ref, b_ref, y_ref, pad_ref):
        first, _ = _seq_flags(pl.program_id(0), tps)
        pad_ref[0:HALO, :] = jnp.where(first, 0.0, halo_ref[...])
        pad_ref[HALO:HALO + tile, :] = x_ref[...]
        acc = jnp.broadcast_to(b_ref[...], (tile, cb))
        for k in range(kw):
            acc = acc + pad_ref[pl.ds(HALO - (kw - 1) + k, tile), :] * w_ref[k:k + 1, :]
        y_ref[...] = acc

    return _pallas(
        body, side, name=name, grid=(t_rows // tile, ch // cb),
        in_specs=[pl.BlockSpec((tile, cb), lambda i, j: (i, j)),
                  pl.BlockSpec((HALO, cb), lambda i, j: (jnp.maximum(i * hb - 1, 0), j)),
                  pl.BlockSpec((kw, cb), lambda i, j: (0, j)), pl.BlockSpec((1, cb), lambda i, j: (0, j))],
        out_specs=pl.BlockSpec((tile, cb), lambda i, j: (i, j)),
        out_shape=jax.ShapeDtypeStruct((t_rows, ch), F32),
        scratch_shapes=[pltpu.VMEM((HALO + tile, cb), F32)],
        semantics=("parallel", "parallel"),
    )(x, x, w, b)


def _dwconv_bwd(x, dy, w, seq, name, side=None):
    t_rows, ch = x.shape
    kw = w.shape[0]
    tile = min(512, seq)
    cb = _divisor_tile(ch, 256, LANES)
    tps, hb, n_hb = seq // tile, tile // HALO, t_rows // HALO

    def body(x_ref, xh_ref, dy_ref, dyh_ref, w_ref, dx_ref, dw_ref, db_ref, xpad, dypad):
        i = pl.program_id(1)
        first, last = _seq_flags(i, tps)

        @pl.when(i == 0)
        def _():
            dw_ref[...] = jnp.zeros_like(dw_ref)
            db_ref[...] = jnp.zeros_like(db_ref)

        xpad[0:HALO, :] = jnp.where(first, 0.0, xh_ref[...])
        xpad[HALO:HALO + tile, :] = x_ref[...]
        dyv = dy_ref[...]
        dypad[0:tile, :] = dyv
        dypad[tile:tile + HALO, :] = jnp.where(last, 0.0, dyh_ref[...])
        acc = jnp.zeros((tile, cb), F32)
        for k in range(kw):
            acc = acc + dypad[pl.ds(kw - 1 - k, tile), :] * w_ref[k:k + 1, :]
            dw_ref[k:k + 1, :] += jnp.sum(dyv * xpad[pl.ds(HALO - (kw - 1) + k, tile), :], axis=0, keepdims=True)
        dx_ref[...] = acc
        db_ref[...] += jnp.sum(dyv, axis=0, keepdims=True)

    return _pallas(
        body, side, name=name, grid=(ch // cb, t_rows // tile),
        in_specs=[pl.BlockSpec((tile, cb), lambda j, i: (i, j)),
                  pl.BlockSpec((HALO, cb), lambda j, i: (jnp.maximum(i * hb - 1, 0), j)),
                  pl.BlockSpec((tile, cb), lambda j, i: (i, j)),
                  pl.BlockSpec((HALO, cb), lambda j, i: (jnp.minimum((i + 1) * hb, n_hb - 1), j)),
                  pl.BlockSpec((kw, cb), lambda j, i: (0, j))],
        out_specs=[pl.BlockSpec((tile, cb), lambda j, i: (i, j)), pl.BlockSpec((kw, cb), lambda j, i: (0, j)),
                   pl.BlockSpec((1, cb), lambda j, i: (0, j))],
        out_shape=[jax.ShapeDtypeStruct((t_rows, ch), F32), jax.ShapeDtypeStruct((kw, ch), F32),
                   jax.ShapeDtypeStruct((1, ch), F32)],
        scratch_shapes=[pltpu.VMEM((HALO + tile, cb), F32), pltpu.VMEM((tile + HALO, cb), F32)],
        semantics=("parallel", "arbitrary"),
    )(x, x, dy, dy, w)


_ROWS = SUBLANES_BF16


def _lane_chunks(width, cap=6 * LANES):
    return [slice(c0, min(c0 + cap, width)) for c0 in range(0, width, cap)]


def _tap_rows(w_ref, cols):
    return [w_ref[k:k + 1, cols] for k in range(FFN_CONV_WIDTH)]


def _conv3_at(pad, taps, row, cols):
    z = pad[pl.ds(row, _ROWS), cols] * taps[2]
    z = z + pad[pl.ds(row - 1, _ROWS), cols] * taps[1]
    return z + pad[pl.ds(row - 2, _ROWS), cols] * taps[0]


def _ffn_mid_fwd(u, w, seq, name, side=None):
    t_rows, f2 = u.shape
    f = f2 // 2
    tile = min(256, seq)
    cb = _divisor_tile(f, 1408, LANES)
    nj, tps, hb, hl = f // cb, seq // tile, tile // SUBLANES_BF16, SUBLANES_BF16

    def body(ug_ref, uv_ref, hg_ref, hv_ref, wg_ref, wv_ref, a_ref, gpad, vpad):
        first, _ = _seq_flags(pl.program_id(0), tps)
        for t_ref, h_ref, pad in ((ug_ref, hg_ref, gpad), (uv_ref, hv_ref, vpad)):
            pad[0:hl, :] = jnp.where(first, 0.0, h_ref[...].astype(F32))
            pad[hl:hl + tile, :] = t_ref[...].astype(F32)
        for cols in _lane_chunks(cb):
            wg, wv = _tap_rows(wg_ref, cols), _tap_rows(wv_ref, cols)
            for r0 in range(0, tile, _ROWS):
                zg = _conv3_at(gpad, wg, hl + r0, cols)
                zv = _conv3_at(vpad, wv, hl + r0, cols)
                half = 0.5 * zg
                a_ref[pl.ds(r0, _ROWS), cols] = ((jnp.tanh(half) + 1.0) * half * zv).astype(a_ref.dtype)

    halo_map = lambda off: (lambda i, j: (jnp.maximum(i * hb - 1, 0), j + off))
    return _pallas(
        body, side, name=name, grid=(t_rows // tile, nj),
        in_specs=[pl.BlockSpec((tile, cb), lambda i, j: (i, j)), pl.BlockSpec((tile, cb), lambda i, j: (i, j + nj)),
                  pl.BlockSpec((hl, cb), halo_map(0)), pl.BlockSpec((hl, cb), halo_map(nj)),
                  pl.BlockSpec((3, cb), lambda i, j: (0, j)), pl.BlockSpec((3, cb), lambda i, j: (0, j + nj))],
        out_specs=pl.BlockSpec((tile, cb), lambda i, j: (i, j)),
        out_shape=jax.ShapeDtypeStruct((t_rows, f), BF16),
        scratch_shapes=[pltpu.VMEM((hl + tile, cb), F32), pltpu.VMEM((hl + tile, cb), F32)],
        semantics=("parallel", "parallel"),
    )(u, u, u, u, w, w)


def _ffn_mid_bwd(u, da, w, seq, name, side=None):
    t_rows, f2 = u.shape
    f = f2 // 2
    tile = min(256, seq)
    cb = _divisor_tile(f, 1408, LANES)
    hl = SUBLANES_BF16
    nj, tps, hb, n_hb, ext = f // cb, seq // tile, tile // hl, t_rows // hl, tile + hl

    def body(ug_ref, uv_ref, pg_ref, pv_ref, ng_ref, nv_ref, da_ref, dan_ref, wg_ref, wv_ref,
             dug_ref, duv_ref, dwg_ref, dwv_ref, gpad, vpad, dzg, dzv):
        i = pl.program_id(1)
        first, last = _seq_flags(i, tps)

        @pl.when(i == 0)
        def _():
            dwg_ref[...] = jnp.zeros_like(dwg_ref)
            dwv_ref[...] = jnp.zeros_like(dwv_ref)

        for t_ref, p_ref, n_ref, pad in ((ug_ref, pg_ref, ng_ref, gpad), (uv_ref, pv_ref, nv_ref, vpad)):
            pad[0:hl, :] = jnp.where(first, 0.0, p_ref[...].astype(F32))
            pad[hl:hl + tile, :] = t_ref[...].astype(F32)
            pad[hl + tile:hl + ext, :] = jnp.where(last, 0.0, n_ref[...].astype(F32))
        for cols in _lane_chunks(cb):
            wg, wv = _tap_rows(wg_ref, cols), _tap_rows(wv_ref, cols)
            for r0 in range(0, ext, _ROWS):
                zg = _conv3_at(gpad, wg, hl + r0, cols)
                zv = _conv3_at(vpad, wv, hl + r0, cols)
                if r0 < tile:
                    da = da_ref[pl.ds(r0, _ROWS), cols].astype(F32)
                else:
                    da = jnp.where(last, 0.0, dan_ref[:, cols].astype(F32))
                sg = _sigmoid(zg)
                dzg[pl.ds(r0, _ROWS), cols] = da * zv * (sg * (1.0 + zg * (1.0 - sg)))
                dzv[pl.ds(r0, _ROWS), cols] = da * (zg * sg)
        for dz, w_ref, pad, du_ref, dw_ref in ((dzg, wg_ref, gpad, dug_ref, dwg_ref), (dzv, wv_ref, vpad, duv_ref, dwv_ref)):
            for cols in _lane_chunks(cb):
                taps = _tap_rows(w_ref, cols)
                width = cols.stop - cols.start
                acc = [jnp.zeros((8, width), F32) for _ in range(FFN_CONV_WIDTH)]
                for r0 in range(0, tile, _ROWS):
                    d0 = dz[pl.ds(r0, _ROWS), cols]
                    du = dz[pl.ds(r0 + 2, _ROWS), cols] * taps[0] + dz[pl.ds(r0 + 1, _ROWS), cols] * taps[1] + d0 * taps[2]
                    du_ref[pl.ds(r0, _ROWS), cols] = du.astype(du_ref.dtype)
                    for k in range(FFN_CONV_WIDTH):
                        prod = d0 * pad[pl.ds(hl - 2 + k + r0, _ROWS), cols]
                        acc[k] = acc[k] + prod[0:8] + prod[8:16]
                for k in range(FFN_CONV_WIDTH):
                    dw_ref[k:k + 1, cols] += jnp.sum(acc[k], axis=0, keepdims=True)

    prev_map = lambda off: (lambda j, i: (jnp.maximum(i * hb - 1, 0), j + off))
    next_map = lambda off: (lambda j, i: (jnp.minimum((i + 1) * hb, n_hb - 1), j + off))
    tile_spec = lambda off: pl.BlockSpec((tile, cb), lambda j, i: (i, j + off))
    w_spec = lambda off: pl.BlockSpec((3, cb), lambda j, i: (0, j + off))
    return _pallas(
        body, side, name=name, grid=(nj, t_rows // tile),
        in_specs=[tile_spec(0), tile_spec(nj), pl.BlockSpec((hl, cb), prev_map(0)), pl.BlockSpec((hl, cb), prev_map(nj)),
                  pl.BlockSpec((hl, cb), next_map(0)), pl.BlockSpec((hl, cb), next_map(nj)),
                  tile_spec(0), pl.BlockSpec((hl, cb), next_map(0)), w_spec(0), w_spec(nj)],
        out_specs=[tile_spec(0), tile_spec(0), w_spec(0), w_spec(0)],
        out_shape=[jax.ShapeDtypeStruct((t_rows, f), BF16), jax.ShapeDtypeStruct((t_rows, f), BF16),
                   jax.ShapeDtypeStruct((3, f), F32), jax.ShapeDtypeStruct((3, f), F32)],
        scratch_shapes=[pltpu.VMEM((hl + ext, cb), F32), pltpu.VMEM((hl + ext, cb), F32),
                        pltpu.VMEM((ext, cb), F32), pltpu.VMEM((ext, cb), F32)],
        semantics=("parallel", "arbitrary"),
    )(u, u, u, u, u, u, da, da, w, w)


def _gla_chunk(q, k, v, lg, st, *, scale, chunk):
    row = lax.broadcasted_iota(jnp.int32, (chunk, chunk), 0)
    col = lax.broadcasted_iota(jnp.int32, (chunk, chunk), 1)
    causal = col <= row
    b = _cumsum_rows(lg)
    upto_mid = lax.broadcasted_iota(jnp.int32, lg.shape, 0) <= chunk // 2
    b_mid = jnp.sum(jnp.where(upto_mid, lg, 0.0), axis=0, keepdims=True)
    b_last = jnp.sum(lg, axis=0, keepdims=True)
    qs = q * scale
    scores = _bdot_nt(qs * jnp.exp(b - b_mid), k * jnp.exp(b_mid - b))
    o = _bdot(jnp.where(causal, scores, 0.0), v)
    o = o + _bdot_nt(qs * jnp.exp(b), st)
    st_new = st * jnp.exp(b_last) + _bdot_tn(v, k * jnp.exp(b_last - b))
    return o, st_new


def _gla_specs(specs, chunk, n_chunks, reverse):
    if reverse:
        row = lambda bi, ci: bi * n_chunks + (n_chunks - 1 - ci)
    else:
        row = lambda bi, ci: bi * n_chunks + ci
    return [pl.BlockSpec((chunk, w), lambda bi, ci, cb=cb: (row(bi, ci), cb)) for _, w, cb in specs], row


def _gla_fwd(q, k, v, lg, *, heads, dk, dv, scale, chunk, seq, name, side=None):
    t_rows = q[0].shape[0]
    n_chunks = seq // chunk
    fn = functools.partial(_gla_chunk, scale=scale, chunk=chunk)

    def body(q_ref, k_ref, v_ref, lg_ref, o_ref, sts_ref, st_ref):
        @pl.when(pl.program_id(1) == 0)
        def _():
            st_ref[...] = jnp.zeros_like(st_ref)

        sts_ref[0] = st_ref[...]
        ks = [slice(h * dk, (h + 1) * dk) for h in range(heads)]
        vs = [slice(h * dv, (h + 1) * dv) for h in range(heads)]
        ins = [(q_ref[:, ks[h]].astype(F32), k_ref[:, ks[h]].astype(F32), v_ref[:, vs[h]].astype(F32), lg_ref[:, ks[h]],
                st_ref[vs[h], :]) for h in range(heads)]
        outs = [fn(*args) for args in ins]
        for h, (o, st) in enumerate(outs):
            o_ref[:, vs[h]] = o
            st_ref[vs[h], :] = st

    in_specs, row = _gla_specs([q, k, v, lg], chunk, n_chunks, False)
    return _pallas(
        body, side, name=name, grid=(t_rows // seq, n_chunks), in_specs=in_specs,
        out_specs=[pl.BlockSpec((chunk, heads * dv), lambda bi, ci: (row(bi, ci), 0)),
                   pl.BlockSpec((1, heads * dv, dk), lambda bi, ci: (row(bi, ci), 0, 0))],
        out_shape=[jax.ShapeDtypeStruct((t_rows, heads * dv), F32),
                   jax.ShapeDtypeStruct((t_rows // chunk, heads * dv, dk), F32)],
        scratch_shapes=[pltpu.VMEM((heads * dv, dk), F32)],
        semantics=("arbitrary", "arbitrary"),
    )(q[0], k[0], v[0], lg[0])


def _gla_bwd(q, k, v, lg, states, do, *, heads, dk, dv, scale, chunk, seq, out_dtypes, name, side=None):
    t_rows = q[0].shape[0]
    n_chunks = seq // chunk
    fn = functools.partial(_gla_chunk, scale=scale, chunk=chunk)

    def body(q_ref, k_ref, v_ref, lg_ref, do_ref, sts_ref, dq_ref, dk_ref, dv_ref, dlg_ref, dst_ref):
        @pl.when(pl.program_id(1) == 0)
        def _():
            dst_ref[...] = jnp.zeros_like(dst_ref)

        ks = [slice(h * dk, (h + 1) * dk) for h in range(heads)]
        vs = [slice(h * dv, (h + 1) * dv) for h in range(heads)]
        ins = [(q_ref[:, ks[h]].astype(F32), k_ref[:, ks[h]].astype(F32), v_ref[:, vs[h]].astype(F32), lg_ref[:, ks[h]],
                sts_ref[0, vs[h], :]) for h in range(heads)]
        cts = [(do_ref[:, vs[h]].astype(F32), dst_ref[vs[h], :]) for h in range(heads)]
        outs = [jax.vjp(fn, *ins[h])[1](cts[h]) for h in range(heads)]
        for h, (dq, dkk, dvv, dlg, dst) in enumerate(outs):
            dq_ref[:, ks[h]] = dq.astype(dq_ref.dtype)
            dk_ref[:, ks[h]] = dkk.astype(dk_ref.dtype)
            dv_ref[:, vs[h]] = dvv.astype(dv_ref.dtype)
            dlg_ref[:, ks[h]] = dlg
            dst_ref[vs[h], :] = dst

    do_view = (do, heads * dv, 0)
    in_specs, row = _gla_specs([q, k, v, lg, do_view], chunk, n_chunks, True)
    in_specs.append(pl.BlockSpec((1, heads * dv, dk), lambda bi, ci: (row(bi, ci), 0, 0)))
    wide = lambda w: pl.BlockSpec((chunk, w), lambda bi, ci: (row(bi, ci), 0))
    return _pallas(
        body, side, name=name, grid=(t_rows // seq, n_chunks), in_specs=in_specs,
        out_specs=[wide(heads * dk), wide(heads * dk), wide(heads * dv), wide(heads * dk)],
        out_shape=[jax.ShapeDtypeStruct((t_rows, heads * dk), out_dtypes[0]),
                   jax.ShapeDtypeStruct((t_rows, heads * dk), out_dtypes[1]),
                   jax.ShapeDtypeStruct((t_rows, heads * dv), out_dtypes[2]),
                   jax.ShapeDtypeStruct((t_rows, heads * dk), F32)],
        scratch_shapes=[pltpu.VMEM((heads * dv, dk), F32)],
        semantics=("arbitrary", "arbitrary"),
    )(q[0], k[0], v[0], lg[0], do, states)


def _head_rms_gate(o, r, g, heads):
    d = o.shape[1] // heads
    parts = [_rms(o[:, h * d:(h + 1) * d], g) for h in range(heads)]
    return jnp.concatenate(parts, axis=1) * _silu(r)


def _gla_gate(glr, w_g2p, b_g2):
    return _log_sigmoid(_bdot(glr, w_g2p) + b_g2) * (1.0 / GLA_GATE_NORM)


def _glu(a, gate, b_in):
    d = a.shape[1]
    return (a + b_in[:, :d]) * _sigmoid(gate + b_in[:, d:])


def _ln_silu(y, g, b):
    return _silu(_layer_norm(y, g, b))


def _sgu(pre, b_in, ln_g, ln_b, w_s, b_st):
    d = pre.shape[1] // 2
    gd = d // SGU_GROUPS
    uv = _gelu(pre + b_in)
    u, v = uv[:, :d], _layer_norm(uv[:, d:], ln_g, ln_b)
    row = lax.broadcasted_iota(jnp.int32, (SGU_CHUNK, SGU_CHUNK), 0)
    col = lax.broadcasted_iota(jnp.int32, (SGU_CHUNK, SGU_CHUNK), 1)
    lane = lax.broadcasted_iota(jnp.int32, b_st.shape, 1)
    rows = []
    for c in range(pre.shape[0] // SGU_CHUNK):
        rs = slice(c * SGU_CHUNK, (c + 1) * SGU_CHUNK)
        parts = []
        for g in range(SGU_GROUPS):
            wg = jnp.where(col <= row, w_s[g], 0.0)
            bias = jnp.sum(jnp.where(lane == g, b_st, 0.0), axis=1, keepdims=True)
            parts.append(_bdot(wg, v[rs, g * gd:(g + 1) * gd]) + bias)
        rows.append(jnp.concatenate(parts, axis=1))
    s = rows[0] if len(rows) == 1 else jnp.concatenate(rows, axis=0)
    return u * s


def _hgrn_pre(q, f, table, layer):
    t = table - jnp.max(table, axis=0, keepdims=True)
    e = jnp.exp(t)
    sm = e / jnp.sum(e, axis=0, keepdims=True)
    rows = lax.broadcasted_iota(jnp.int32, table.shape, 0)
    lb = jnp.sum(jnp.where((rows >= 1) & (rows <= layer), sm, 0.0), axis=0, keepdims=True)
    sf = _sigmoid(f)
    return _silu(q), (1.0 - lb) * (1.0 - sf), jnp.log(lb + (1.0 - lb) * sf)


def _ffn_fwd(x, w, seq, sv):
    sv["h2"] = _rms_fwd(x, w["norm"], "ffn_norm")
    sv["u"] = _mm(sv["h2"], w["w_up"], out_dtype=BF16, name="ffn_up")
    sv["a"] = _ffn_mid_fwd(sv["u"], w["w_dw"], seq, "ffn_mid", sv.pop("side", None))
    return _mm(sv["a"], w["w_down"], add=x, name="ffn_down")


def _ffn_bwd(x, dy, w, seq, sv, deps=()):
    g = {}
    da = _mm(dy, w["w_down_t"], out_dtype=BF16, name="ffn_down_dx", deps=deps)
    g["w_down"] = _mm_tn(sv["a"], dy, name="ffn_down_dw")
    dug, duv, dwg, dwv = _ffn_mid_bwd(sv["u"], da, w["w_dw"], seq, "ffn_mid_bwd", sv.pop("side", None))
    g["w_dw"] = jnp.concatenate([dwg, dwv], axis=1)
    g["w_up_gate"] = _mm_tn(sv["h2"], dug, name="ffn_up_dw")
    g["w_up_val"] = _mm_tn(sv["h2"], duv, name="ffn_up_dw")
    dh = _mm(dug, w["w_up_t_gate"], out_dtype=F32, name="ffn_up_dx")
    dh = _mm(duv, w["w_up_t_val"], add=dh, out_dtype=BF16, name="ffn_up_dx2")
    dx, g["norm"] = _rms_bwd(x, w["norm"], dh, dy, "ffn_norm_bwd")
    return dx, g


def _gla_layer_fwd(x, h, w, seq, sv):
    d = x.shape[1]
    dkt = d // 2
    dk, dv = dkt // GLA_HEADS, d // GLA_HEADS
    proj = _mm(h, w["w_main"], out_dtype=F32, name="gla_in")
    glr = _mm(h, w["w_glr"], out_dtype=BF16, name="gla_in_g")
    (lg,), _ = _tile_call("gla_gate", lambda a, b, c: ([_gla_gate(a.astype(F32), b, c)], []), [glr],
                          [w["w_g2p"], w["b_g2"]], [(dkt, F32)], [], 512)
    q, k, v, r = (proj, dkt, 0), (proj, dkt, 1), (proj, d, 1), (proj, d, 2)
    o, states = _gla_fwd(q, k, v, (lg, dkt, 0), heads=GLA_HEADS, dk=dk, dv=dv, scale=dk ** -0.5, chunk=GLA_CHUNK,
                         seq=seq, name="gla_core", side=sv.pop("side", None))
    (o2,), _ = _tile_call("gla_post", lambda ov, rv, gv: ([_head_rms_gate(ov, rv.astype(F32), gv, GLA_HEADS)], []),
                          [o, r], [w["norm"]], [(d, BF16)], [], 256)
    sv.update(proj=proj, glr=glr, lg=lg, o=o, states=states, o2=o2)
    return _mm(o2, w["w_out"], add=x, name="mix_out")


def _gla_layer_bwd(h, dy, w, seq, sv, deps=()):
    d = dy.shape[1]
    dkt = d // 2
    dk, dv = dkt // GLA_HEADS, d // GLA_HEADS
    proj, glr, lg, o = sv["proj"], sv["glr"], sv["lg"], sv["o"]
    g = {}
    do2 = _mm(dy, w["w_out_t"], out_dtype=F32, name="gla_out_dx", deps=deps)
    g["w_out"] = _mm_tn(sv["o2"], dy, name="mix_out_dw")

    def post_bwd(ov, rv, ctv, gv):
        _, vjp = jax.vjp(functools.partial(_head_rms_gate, heads=GLA_HEADS), ov, rv.astype(F32), gv)
        d_o, d_r, d_g = vjp(ctv.astype(F32))
        return [d_o, d_r], [d_g]

    (d_o, d_r), (g["norm"],) = _tile_call("gla_post_bwd", post_bwd, [o, (proj, d, 2), do2], [w["norm"]],
                                          [(d, F32), (d, BF16)], [w["norm"].shape], 256)
    q, k, v = (proj, dkt, 0), (proj, dkt, 1), (proj, d, 1)
    dq, dkk, dvv, dlg = _gla_bwd(q, k, v, (lg, dkt, 0), sv["states"], d_o, heads=GLA_HEADS, dk=dk, dv=dv,
                                 scale=dk ** -0.5, chunk=GLA_CHUNK, seq=seq, out_dtypes=(BF16, BF16, BF16),
                                 name="gla_core_bwd", side=sv.pop("side", None))

    def gate_bwd(glrv, ctv, wv, bv):
        _, vjp = jax.vjp(_gla_gate, glrv.astype(F32), wv, bv)
        d_glr, d_w, d_b = vjp(ctv)
        return [d_glr], [d_w, d_b]

    (dglr,), (g["w_g2p"], g["b_g2"]) = _tile_call("gla_gate_bwd", gate_bwd, [glr, dlg], [w["w_g2p"], w["b_g2"]],
                                                  [(LANES, BF16)], [w["w_g2p"].shape, w["b_g2"].shape], 512)
    dproj = jnp.concatenate([dq, dkk, dvv, d_r], axis=1)
    g["w_main"] = _mm_tn(h, dproj, name="gla_in_dw")
    g["w_glr"] = _mm_tn(h, dglr, name="gla_in_g_dw")
    dh = _mm(dproj, w["w_main_t"], out_dtype=F32, name="gla_in_dx")
    dh = _mm(dglr, w["w_glr_t"], add=dh, out_dtype=BF16, name="gla_in_g_dx")
    return dh, g


def _cv_layer_fwd(x, h, w, seq, sv):
    d = x.shape[1]
    pre = _mm(h, w["w_in"], out_dtype=BF16, name="cv_in")
    (y1,), _ = _tile_call("cv_glu", lambda a, gt, b: ([_glu(a.astype(F32), gt.astype(F32), b)], []),
                          [(pre, d, 0), (pre, d, 1)], [w["b_in"]], [(d, F32)], [], 512)
    y2 = _dwconv_fwd(y1, w["w_dw"], w["b_dw"], seq, "cv_conv", sv.pop("side", None))
    (y3,), _ = _tile_call("cv_ln", lambda y, a, b: ([_ln_silu(y, a, b)], []), [y2], [w["ln_g"], w["ln_b"]],
                          [(d, BF16)], [], 512)
    sv.update(pre=pre, y1=y1, y2=y2, y3=y3)
    return _mm(y3, w["w_out"], bias=w["b_out"], add=x, name="mix_out_b")


def _cv_layer_bwd(h, dy, w, seq, sv, deps=()):
    d = dy.shape[1]
    pre = sv["pre"]
    g = {}
    dy3 = _mm(dy, w["w_out_t"], out_dtype=BF16, name="mix_out_dx", deps=deps)
    g["w_out"] = _mm_tn(sv["y3"], dy, name="mix_out_dw")
    g["b_out"] = _colsum(dy, "bias_out_dw")

    def ln_bwd(yv, ctv, av, bv):
        _, vjp = jax.vjp(_ln_silu, yv, av, bv)
        d_y, d_a, d_b = vjp(ctv.astype(F32))
        return [d_y], [d_a, d_b]

    (dy2,), (g["ln_g"], g["ln_b"]) = _tile_call("cv_ln_bwd", ln_bwd, [sv["y2"], dy3], [w["ln_g"], w["ln_b"]],
                                                [(d, F32)], [w["ln_g"].shape, w["ln_b"].shape], 512)
    dy1, g["w_dw"], g["b_dw"] = _dwconv_bwd(sv["y1"], dy2, w["w_dw"], seq, "cv_conv_bwd", sv.pop("side", None))

    def glu_bwd(av, gv, ctv, bv):
        _, vjp = jax.vjp(_glu, av.astype(F32), gv.astype(F32), bv)
        d_a, d_g, d_b = vjp(ctv)
        return [jnp.concatenate([d_a, d_g], axis=1)], [d_b]

    (dpre,), (g["b_in"],) = _tile_call("cv_glu_bwd", glu_bwd, [(pre, d, 0), (pre, d, 1), dy1], [w["b_in"]],
                                       [(2 * d, BF16)], [w["b_in"].shape], 512)
    g["w_in"] = _mm_tn(h, dpre, name="in2_dw")
    dh = _mm(dpre, w["w_in_t"], out_dtype=BF16, name="in2_dx")
    return dh, g


def _sg_layer_fwd(x, h, w, seq, sv):
    d = x.shape[1]
    pre = _mm(h, w["w_in"], out_dtype=BF16, name="sg_in")
    pars = [w["b_in"], w["ln_g"], w["ln_b"], w["w_s"], w["b_st"]]
    (p,), _ = _tile_call("sg_gate", lambda pv, *ps: ([_sgu(pv.astype(F32), *ps)], []), [pre], pars, [(d, BF16)], [],
                         SGU_CHUNK, side=sv.pop("side", None))
    sv.update(pre=pre, p=p)
    return _mm(p, w["w_out"], bias=w["b_out"], add=x, name="mix_out_b")


def _sg_layer_bwd(h, dy, w, seq, sv, deps=()):
    d = dy.shape[1]
    g = {}
    dp = _mm(dy, w["w_out_t"], out_dtype=BF16, name="mix_out_dx", deps=deps)
    g["w_out"] = _mm_tn(sv["p"], dy, name="mix_out_dw")
    g["b_out"] = _colsum(dy, "bias_out_dw")
    pars = [w["b_in"], w["ln_g"], w["ln_b"], w["w_s"], w["b_st"]]

    def sgu_bwd(pv, ctv, *ps):
        _, vjp = jax.vjp(_sgu, pv.astype(F32), *ps)
        grads = vjp(ctv.astype(F32))
        return [grads[0]], list(grads[1:])

    (dpre,), (g["b_in"], g["ln_g"], g["ln_b"], g["w_s"], g["b_st"]) = _tile_call(
        "sg_gate_bwd", sgu_bwd, [sv["pre"], dp], pars, [(2 * d, BF16)], [p.shape for p in pars], SGU_CHUNK,
        side=sv.pop("side", None))
    g["w_in"] = _mm_tn(h, dpre, name="in2_dw")
    dh = _mm(dpre, w["w_in_t"], out_dtype=BF16, name="in2_dx")
    return dh, g


def _hg_layer_fwd(x, h, w, seq, sv, layer):
    d = x.shape[1]
    heads = d // HGRN_EXPAND
    proj = _mm(h, w["w_in"], out_dtype=BF16, name="hg_in")
    pre = functools.partial(_hgrn_pre, layer=layer)
    (qs, kk, lg), _ = _tile_call("hg_pre", lambda qv, fv, tb: (list(pre(qv.astype(F32), fv.astype(F32), tb)), []),
                                 [(proj, d, 0), (proj, d, 1)], [w["lb_table"]], [(d, BF16), (d, F32), (d, F32)], [], 256)
    o, states = _gla_fwd((qs, d, 0), (kk, d, 0), (proj, d, 2), (lg, d, 0), heads=heads, dk=HGRN_EXPAND,
                         dv=HGRN_EXPAND, scale=1.0, chunk=HGRN_CHUNK, seq=seq, name="hg_core",
                         side=sv.pop("side", None))
    (o2,), _ = _tile_call("hg_post", lambda ov, gv, nv: ([_head_rms_gate(ov, gv.astype(F32), nv, heads)], []),
                          [o, (proj, d, 3)], [w["norm"]], [(d, BF16)], [], 256)
    sv.update(proj=proj, qs=qs, kk=kk, lg=lg, o=o, states=states, o2=o2)
    return _mm(o2, w["w_out"], add=x, name="mix_out")


def _hg_layer_bwd(h, dy, w, seq, sv, layer, deps=()):
    d = dy.shape[1]
    heads = d // HGRN_EXPAND
    proj = sv["proj"]
    g = {}
    do2 = _mm(dy, w["w_out_t"], out_dtype=BF16, name="mix_out_dx", deps=deps)
    g["w_out"] = _mm_tn(sv["o2"], dy, name="mix_out_dw")

    def post_bwd(ov, gv, ctv, nv):
        _, vjp = jax.vjp(functools.partial(_head_rms_gate, heads=heads), ov, gv.astype(F32), nv)
        d_o, d_g, d_n = vjp(ctv.astype(F32))
        return [d_o, d_g], [d_n]

    (d_o, d_gate), (g["norm"],) = _tile_call("hg_post_bwd", post_bwd, [sv["o"], (proj, d, 3), do2], [w["norm"]],
                                             [(d, F32), (d, BF16)], [w["norm"].shape], 256)
    dqs, dkk, di, dlg = _gla_bwd((sv["qs"], d, 0), (sv["kk"], d, 0), (proj, d, 2), (sv["lg"], d, 0), sv["states"], d_o,
                                 heads=heads, dk=HGRN_EXPAND, dv=HGRN_EXPAND, scale=1.0, chunk=HGRN_CHUNK, seq=seq,
                                 out_dtypes=(F32, F32, BF16), name="hg_core_bwd", side=sv.pop("side", None))

    def pre_bwd(qv, fv, c1, c2, c3, tb):
        _, vjp = jax.vjp(functools.partial(_hgrn_pre, layer=layer), qv.astype(F32), fv.astype(F32), tb)
        d_q, d_f, d_t = vjp((c1, c2, c3))
        return [jnp.concatenate([d_q, d_f], axis=1)], [d_t]

    (dqf,), (g["lb_table"],) = _tile_call("hg_pre_bwd", pre_bwd, [(proj, d, 0), (proj, d, 1), dqs, dkk, dlg],
                                          [w["lb_table"]], [(2 * d, BF16)], [w["lb_table"].shape], 256)
    dproj = jnp.concatenate([dqf, di, d_gate], axis=1)
    g["w_in"] = _mm_tn(h, dproj, name="hg_in_dw")
    dh = _mm(dproj, w["w_in_t"], out_dtype=BF16, name="hg_in_dx")
    return dh, g


_MIXERS = ("gla", "cv", "sg", "hg")


_BIG_KEYS = {"gla": ("w_main", "w_glr", "w_out"), "cv": ("w_in", "w_out"), "sg": ("w_in", "w_out"), "hg": ("w_in", "w_out"),
             "ffn": ("w_up_gate", "w_up_val", "w_down")}


def _local_step(x, target, w, seq, get_big, put_big, ride=lambda kind, layer, forward: None, deps=()):
    depth = w["norm_mix"].shape[0]
    d = x.shape[1]
    saved, big = [], {}
    for layer in range(depth):
        mixer = _MIXERS[layer % 4]
        sv = {"x_in": x, "side": ride(mixer, layer, True)}
        sv["h"] = _rms_fwd(x, w["norm_mix"][layer:layer + 1], "mix_norm", deps if layer == 0 else ())
        big[mixer, layer] = get_big(mixer, layer, sv["h"])
        wm = dict(w[mixer], **big[mixer, layer])
        if mixer == "gla":
            x = _gla_layer_fwd(x, sv["h"], wm, seq, sv)
        elif mixer == "cv":
            x = _cv_layer_fwd(x, sv["h"], wm, seq, sv)
        elif mixer == "sg":
            x = _sg_layer_fwd(x, sv["h"], wm, seq, sv)
        else:
            x = _hg_layer_fwd(x, sv["h"], wm, seq, sv, layer)
        sv["x_mid"] = x
        big["ffn", layer] = get_big("ffn", layer, x)
        sv["ffn"] = {"side": ride("ffn", layer, True)}
        wf = dict(w["ffn"][layer], norm=w["norm_ffn"][layer:layer + 1], **big["ffn", layer])
        x = _ffn_fwd(x, wf, seq, sv["ffn"])
        saved.append(sv)

    def head(xv, tv, gv):
        y, vjp = jax.vjp(_rms, xv, gv)
        err = y - tv
        dx, dg = vjp(err * (1.0 / d))
        part = 0.5 * jnp.sum(jnp.mean(err * err, axis=-1, keepdims=True), axis=0, keepdims=True)
        return [dx], [jnp.broadcast_to(part, (1, LANES)), dg]

    (dx,), (loss, g_final) = _tile_call("loss_head", head, [x, target], [w["norm_final"]], [(d, F32)],
                                        [(1, LANES), (1, d)], 512)
    grads = {"norm_final": g_final, "norm_mix": [None] * depth, "norm_ffn": [None] * depth, "ffn": [None] * depth}
    order = ()
    for layer in reversed(range(depth)):
        mixer = _MIXERS[layer % 4]
        sv = saved[layer]
        wf = dict(w["ffn"][layer], norm=w["norm_ffn"][layer:layer + 1], **big["ffn", layer])
        sv["ffn"]["side"] = ride("ffn", layer, False)
        dx, gf = _ffn_bwd(sv["x_mid"], dx, wf, seq, sv["ffn"], order)
        order = put_big("ffn", layer, {k: gf.pop(k) for k in _BIG_KEYS["ffn"]})
        sv["side"] = ride(mixer, layer, False)
        grads["norm_ffn"][layer] = gf.pop("norm")
        grads["ffn"][layer] = gf
        wm = dict(w[mixer], **big[mixer, layer])
        if mixer == "gla":
            dh, gm = _gla_layer_bwd(sv["h"], dx, wm, seq, sv, order)
        elif mixer == "cv":
            dh, gm = _cv_layer_bwd(sv["h"], dx, wm, seq, sv, order)
        elif mixer == "sg":
            dh, gm = _sg_layer_bwd(sv["h"], dx, wm, seq, sv, order)
        else:
            dh, gm = _hg_layer_bwd(sv["h"], dx, wm, seq, sv, layer, order)
        order = put_big(mixer, layer, {k: gm.pop(k) for k in _BIG_KEYS[mixer]})
        grads[mixer] = gm
        dx, grads["norm_mix"][layer] = _rms_bwd(sv["x_in"], w["norm_mix"][layer:layer + 1], dh, dx, "mix_norm_bwd")
    return loss, dx, grads


def _prep_small(p):
    row = lambda a: a.reshape(1, -1).astype(F32)
    w = {"norm_mix": p["norm_mix"].astype(F32), "norm_ffn": p["norm_ffn"].astype(F32), "norm_final": row(p["norm_final"])}
    w["gla"] = dict(w_g2p=jnp.pad(p["gla_w_g2"][0].astype(F32), ((0, LANES - GLA_RANK), (0, 0))), b_g2=row(p["gla_b_g2"]),
                    norm=row(p["gla_norm"]))
    w["cv"] = dict(b_in=row(p["cv_b_in"]), w_dw=p["cv_w_dw"][0].astype(F32), b_dw=row(p["cv_b_dw"]), ln_g=row(p["cv_ln_g"]),
                   ln_b=row(p["cv_ln_b"]), b_out=row(p["cv_b_out"]))
    b_st = jnp.pad(p["sg_b_s"][0].astype(F32).T, ((0, 0), (0, LANES - SGU_GROUPS)))
    w["sg"] = dict(b_in=row(p["sg_b_in"]), ln_g=row(p["sg_ln_g"]), ln_b=row(p["sg_ln_b"]), w_s=p["sg_w_s"][0].astype(F32),
                   b_st=b_st, b_out=row(p["sg_b_out"]))
    w["hg"] = dict(lb_table=p["hg_lb_table"].astype(F32), norm=row(p["hg_norm"]))
    w["ffn"] = [dict(w_dw=p["ffn_w_dw"][layer].astype(F32)) for layer in range(p["ffn_w_dw"].shape[0])]
    return w


def _small_grads(g):
    gla, cv, sg, hg = g["gla"], g["cv"], g["sg"], g["hg"]
    return {
        "norm_mix": jnp.concatenate(g["norm_mix"], axis=0), "norm_ffn": jnp.concatenate(g["norm_ffn"], axis=0),
        "norm_final": g["norm_final"][0],
        "gla_w_g2": gla["w_g2p"][:GLA_RANK][None], "gla_b_g2": gla["b_g2"], "gla_norm": gla["norm"],
        "cv_b_in": cv["b_in"], "cv_w_dw": cv["w_dw"][None], "cv_b_dw": cv["b_dw"], "cv_ln_g": cv["ln_g"],
        "cv_ln_b": cv["ln_b"], "cv_b_out": cv["b_out"],
        "sg_b_in": sg["b_in"], "sg_ln_g": sg["ln_g"], "sg_ln_b": sg["ln_b"], "sg_w_s": sg["w_s"][None],
        "sg_b_s": sg["b_st"][:, :SGU_GROUPS].T[None], "sg_b_out": sg["b_out"],
        "hg_lb_table": hg["lb_table"], "hg_norm": hg["norm"],
        "ffn_w_dw": jnp.stack([f["w_dw"] for f in g["ffn"]]),
    }


def _oriented(kind, mats):
    if kind == "ffn":
        (up, up_t), (down, down_t) = mats["w_up"], mats["w_down"]
        f = down.shape[0]
        return dict(w_up=up, w_up_t_gate=up_t[:f], w_up_t_val=up_t[f:], w_down=down, w_down_t=down_t)
    (w_in, w_in_t), (w_out, w_out_t) = mats["w_in"], mats["w_out"]
    if kind != "gla":
        return dict(w_in=w_in, w_in_t=w_in_t, w_out=w_out, w_out_t=w_out_t)
    n_main = w_in.shape[1] - GLA_RANK
    return dict(w_main=w_in[:, :n_main], w_glr=jnp.pad(w_in[:, n_main:], ((0, 0), (0, LANES - GLA_RANK))),
                w_main_t=w_in_t[:n_main], w_glr_t=jnp.pad(w_in_t[n_main:], ((0, LANES - GLA_RANK), (0, 0))),
                w_out=w_out, w_out_t=w_out_t)


def _all_gather(x, *, name):
    m_per, n = x.shape

    def body(x_ref, out_ref, send_sems, recv_sems, local_sem):
        mx, my, mc = lax.axis_index("x"), lax.axis_index("y"), lax.axis_index("c")
        me, sibling = (mx, my, mc), (mx, my, 1 - mc)
        chips = [(1 - mx, my), (mx, 1 - my), (1 - mx, 1 - my)]

        def rows(px, py, pc):
            return out_ref.at[pl.ds((4 * px + 2 * py + pc) * m_per, m_per), :]

        def copy(k, block, to, src=None):
            return pltpu.make_async_remote_copy(
                src_ref=rows(*block) if src is None else src, dst_ref=rows(*block), send_sem=send_sems.at[k],
                recv_sem=recv_sems.at[k], device_id=to, device_id_type=MESH)

        mine = pltpu.make_async_copy(x_ref, rows(*me), local_sem)
        mine.start()
        first = [copy(0, me, sibling, src=x_ref)]
        first += [copy(1 + j, me, (*chip, mc), src=x_ref) for j, chip in enumerate(chips)]
        for cp in first:
            cp.start()
        passed = [copy(4 + j, (*chip, mc), sibling) for j, chip in enumerate(chips)]
        for j, chip in enumerate(chips):
            copy(1 + j, (*chip, mc), me).wait_recv()
            passed[j].start()
        copy(0, sibling, me).wait_recv()
        for j, chip in enumerate(chips):
            copy(4 + j, (*chip, 1 - mc), me).wait_recv()
        for cp in first + passed:
            cp.wait_send()
        mine.wait()

    return pl.pallas_call(
        body, name=name, out_shape=jax.ShapeDtypeStruct((N_DEV * m_per, n), x.dtype),
        in_specs=[pl.BlockSpec(memory_space=pltpu.VMEM)], out_specs=pl.BlockSpec(memory_space=pltpu.VMEM),
        scratch_shapes=[pltpu.SemaphoreType.DMA((7,)), pltpu.SemaphoreType.DMA((7,)), pltpu.SemaphoreType.DMA],
    )(x)


def _my_index():
    return 4 * lax.axis_index("x") + 2 * lax.axis_index("y") + lax.axis_index("c")


def _gather_stage(srcs, *, name):
    n = len(srcs)

    def body(*refs):
        x_refs, out_refs = refs[:n], refs[n:2 * n]
        send_sems, recv_sems, local_sems = refs[2 * n:]
        mx, my, mc = lax.axis_index("x"), lax.axis_index("y"), lax.axis_index("c")
        me, sibling = (mx, my, mc), (mx, my, 1 - mc)
        chips = [(1 - mx, my), (mx, 1 - my), (1 - mx, 1 - my)]

        def slot(i, px, py, pc):
            return out_refs[i].at[4 * px + 2 * py + pc]

        def copy(i, k, block, to, src=None):
            return pltpu.make_async_remote_copy(
                src_ref=slot(i, *block) if src is None else src, dst_ref=slot(i, *block), send_sem=send_sems.at[7 * i + k],
                recv_sem=recv_sems.at[7 * i + k], device_id=to, device_id_type=MESH)

        mine = [pltpu.make_async_copy(x_refs[i], slot(i, *me), local_sems.at[i]) for i in range(n)]
        first = [copy(i, 0, me, sibling, src=x_refs[i]) for i in range(n)]
        first += [copy(i, 1 + j, me, (*chip, mc), src=x_refs[i]) for j, chip in enumerate(chips) for i in range(n)]
        for cp in mine + first:
            cp.start()
        passed = []
        for j, chip in enumerate(chips):
            for i in range(n):
                copy(i, 1 + j, (*chip, mc), me).wait_recv()
                passed.append(copy(i, 4 + j, (*chip, mc), sibling))
                passed[-1].start()
        for i in range(n):
            copy(i, 0, sibling, me).wait_recv()
            for j, chip in enumerate(chips):
                copy(i, 4 + j, (*chip, 1 - mc), me).wait_recv()
        for cp in first + passed:
            cp.wait_send()
        for cp in mine:
            cp.wait()

    any_space = pl.BlockSpec(memory_space=pl.ANY)
    return pl.pallas_call(
        body, name=name, out_shape=[jax.ShapeDtypeStruct((N_DEV,) + s.shape, s.dtype) for s in srcs],
        in_specs=[any_space] * n, out_specs=[any_space] * n,
        scratch_shapes=[pltpu.SemaphoreType.DMA((7 * n,)), pltpu.SemaphoreType.DMA((7 * n,)), pltpu.SemaphoreType.DMA((n,))],
    )(*srcs)


def _scatter_stage(srcs, *, name):
    n = len(srcs)

    def body(*refs):
        x_refs, out_refs = refs[:n], refs[n:2 * n]
        send_sems, recv_sems, local_sems = refs[2 * n:]
        mx, my, mc = lax.axis_index("x"), lax.axis_index("y"), lax.axis_index("c")
        me = 4 * mx + 2 * my + mc
        mine = [pltpu.make_async_copy(x_refs[i].at[me], out_refs[i].at[me], local_sems.at[i]) for i in range(n)]
        for cp in mine:
            cp.start()
        sends, recvs = [], []
        for k in range(1, N_DEV):
            px = 1 - mx if k & 4 else mx
            py = 1 - my if k & 2 else my
            pc = 1 - mc if k & 1 else mc
            peer = 4 * px + 2 * py + pc
            for i in range(n):
                sems = dict(send_sem=send_sems.at[7 * i + k - 1], recv_sem=recv_sems.at[7 * i + k - 1],
                            device_id=(px, py, pc), device_id_type=MESH)
                sends.append(pltpu.make_async_remote_copy(src_ref=x_refs[i].at[peer], dst_ref=out_refs[i].at[me], **sems))
                recvs.append(pltpu.make_async_remote_copy(src_ref=x_refs[i].at[me], dst_ref=out_refs[i].at[peer], **sems))
                sends[-1].start()
        for cp in recvs:
            cp.wait_recv()
        for cp in sends:
            cp.wait_send()
        for cp in mine:
            cp.wait()

    any_space = pl.BlockSpec(memory_space=pl.ANY)
    return pl.pallas_call(
        body, name=name, out_shape=[jax.ShapeDtypeStruct(s.shape, s.dtype) for s in srcs],
        in_specs=[any_space] * n, out_specs=[any_space] * n,
        scratch_shapes=[pltpu.SemaphoreType.DMA((7 * n,)), pltpu.SemaphoreType.DMA((7 * n,)), pltpu.SemaphoreType.DMA((n,))],
    )(*srcs)


def _adamw_math(g, w, m, v):
    c1, c2 = 1.0 - ADAM_B1 ** ADAM_STEP, 1.0 - ADAM_B2 ** ADAM_STEP
    m_new = ADAM_B1 * m + (1.0 - ADAM_B1) * g
    v_new = ADAM_B2 * v + (1.0 - ADAM_B2) * (g * g)
    delta = -ADAM_LR * ((m_new / c1) / (jnp.sqrt(v_new / c2) + ADAM_EPS) + ADAM_WD * w)
    return delta, m_new, v_new


def _adamw_big(slots, w, m, v, layer, *, name):
    _, r, c = slots.shape
    tr = _divisor_tile(r, max(8, (200 * 1024) // c // 8 * 8), 8)

    def body(s_ref, w_ref, m_ref, v_ref, g_out, d_out, m_out, v_out):
        g = s_ref[0].astype(F32)
        for p in range(1, N_DEV):
            g = g + s_ref[p].astype(F32)
        g_out[...] = g
        d_out[...], m_out[...], v_out[...] = _adamw_math(g, w_ref[...], m_ref[...], v_ref[...])

    blk = pl.BlockSpec((tr, c), lambda i: (i, 0))
    lay = pl.BlockSpec((None, tr, c), lambda i: (layer, i, 0))
    return pl.pallas_call(
        body, name=name, grid=(r // tr,), in_specs=[pl.BlockSpec((N_DEV, tr, c), lambda i: (0, i, 0)), lay, lay, lay],
        out_specs=[blk] * 4, out_shape=[jax.ShapeDtypeStruct((r, c), F32)] * 4,
        compiler_params=pltpu.CompilerParams(dimension_semantics=("parallel",)),
    )(slots, w, m, v)


def _sum_small(got, r_re, r_sh, *, name):
    per_dev = r_re + N_DEV * r_sh

    def body(got_ref, re_ref, sh_ref):
        mine = r_re + _my_index() * r_sh
        acc_re = got_ref[0:r_re, :]
        acc_sh = got_ref[pl.ds(pl.multiple_of(mine, 8), r_sh), :]
        for p in range(1, N_DEV):
            acc_re = acc_re + got_ref[p * per_dev:p * per_dev + r_re, :]
            acc_sh = acc_sh + got_ref[pl.ds(pl.multiple_of(p * per_dev + mine, 8), r_sh), :]
        re_ref[...] = acc_re
        sh_ref[...] = acc_sh

    return pl.pallas_call(body, name=name, out_shape=[jax.ShapeDtypeStruct((r_re, LANES), F32),
                                                       jax.ShapeDtypeStruct((r_sh, LANES), F32)])(got)


def _adamw_small(gs, ws, ms, vs, *, name):
    n = len(gs)

    def body(*refs):
        ins, outs = refs[:4 * n], refs[4 * n:]
        for i in range(n):
            res = _adamw_math(ins[i][...], ins[n + i][...], ins[2 * n + i][...], ins[3 * n + i][...])
            for j in range(3):
                outs[j * n + i][...] = res[j]

    out = pl.pallas_call(body, name=name, out_shape=[jax.ShapeDtypeStruct(a.shape, F32) for a in ws] * 3)(*gs, *ws, *ms, *vs)
    return out[:n], out[n:2 * n], out[2 * n:]


def _layout(shapes, row_align, total_align):
    lay, off = {}, 0
    for name, shape in shapes.items():
        size = int(np.prod(shape))
        rows = -(-size // LANES)
        rows = -(-rows // row_align) * row_align
        lay[name] = (off, rows, size, tuple(shape))
        off += rows
    return lay, -(-off // total_align) * total_align


def _pack(arrs, lay, total, dtype, lead=()):
    parts = []
    nl = len(lead)
    for name, (off, rows, size, shape) in lay.items():
        flat = arrs[name].astype(dtype).reshape(*lead, size)
        parts.append(jnp.pad(flat, [(0, 0)] * nl + [(0, rows * LANES - size)]).reshape(*lead, rows, LANES))
    used = sum(v[1] for v in lay.values())
    if total > used:
        parts.append(jnp.zeros((*lead, total - used, LANES), dtype))
    return jnp.concatenate(parts, axis=nl)


def _unpack(buf, lay, lead=()):
    out = {}
    nl = len(lead)
    for name, (off, rows, size, shape) in lay.items():
        part = lax.slice_in_dim(buf, off, off + rows, axis=nl).reshape(*lead, rows * LANES)
        out[name] = lax.slice_in_dim(part, 0, size, axis=nl).reshape(*lead, *shape)
    return out


_SHARD_AXIS = {
    "norm_mix": None, "norm_ffn": None, "norm_final": None, "gla_w_in": 2, "gla_w_g2": 2, "gla_b_g2": None,
    "gla_norm": None, "gla_w_out": 1, "cv_w_in": 2, "cv_b_in": 1, "cv_w_dw": 2, "cv_b_dw": 1, "cv_ln_g": 1,
    "cv_ln_b": 1, "cv_w_out": 1, "cv_b_out": 1, "sg_w_in": 2, "sg_b_in": 1, "sg_ln_g": 1, "sg_ln_b": 1, "sg_w_s": None,
    "sg_b_s": None, "sg_w_out": 1, "sg_b_out": 1, "hg_w_in": 2, "hg_lb_table": None, "hg_norm": None, "hg_w_out": 1,
    "ffn_w_up": 2, "ffn_w_dw": 2, "ffn_w_down": 1,
}
_MATMUL_WEIGHTS = ("gla_w_in", "gla_w_out", "cv_w_in", "cv_w_out", "sg_w_in", "sg_w_out", "hg_w_in", "hg_w_out",
                   "ffn_w_up", "ffn_w_down")
_NAMES = tuple(_SHARD_AXIS)


def kernel(x, norm_mix, norm_ffn, norm_final, gla_w_in, gla_w_g2, gla_b_g2, gla_norm, gla_w_out, cv_w_in, cv_b_in, cv_w_dw, cv_b_dw, cv_ln_g, cv_ln_b, cv_w_out, cv_b_out, sg_w_in, sg_b_in, sg_ln_g, sg_ln_b, sg_w_s, sg_b_s, sg_w_out, sg_b_out, hg_w_in, hg_lb_table, hg_norm, hg_w_out, ffn_w_up, ffn_w_dw, ffn_w_down, loss_target, m_norm_mix, m_norm_ffn, m_norm_final, m_gla_w_in, m_gla_w_g2, m_gla_b_g2, m_gla_norm, m_gla_w_out, m_cv_w_in, m_cv_b_in, m_cv_w_dw, m_cv_b_dw, m_cv_ln_g, m_cv_ln_b, m_cv_w_out, m_cv_b_out, m_sg_w_in, m_sg_b_in, m_sg_ln_g, m_sg_ln_b, m_sg_w_s, m_sg_b_s, m_sg_w_out, m_sg_b_out, m_hg_w_in, m_hg_lb_table, m_hg_norm, m_hg_w_out, m_ffn_w_up, m_ffn_w_dw, m_ffn_w_down, v_norm_mix, v_norm_ffn, v_norm_final, v_gla_w_in, v_gla_w_g2, v_gla_b_g2, v_gla_norm, v_gla_w_out, v_cv_w_in, v_cv_b_in, v_cv_w_dw, v_cv_b_dw, v_cv_ln_g, v_cv_ln_b, v_cv_w_out, v_cv_b_out, v_sg_w_in, v_sg_b_in, v_sg_ln_g, v_sg_ln_b, v_sg_w_s, v_sg_b_s, v_sg_w_out, v_sg_b_out, v_hg_w_in, v_hg_lb_table, v_hg_norm, v_hg_w_out, v_ffn_w_up, v_ffn_w_dw, v_ffn_w_down):
    local = dict(locals())
    wts = {n: local[n] for n in _NAMES}
    mom = {n: local["m_" + n] for n in _NAMES}
    var = {n: local["v_" + n] for n in _NAMES}
    small_all = [n for n in _NAMES if n not in _MATMUL_WEIGHTS]
    small_sharded = [n for n in small_all if _SHARD_AXIS[n] is not None]
    bsz, seq, d = x.shape
    depth = norm_mix.shape[0]

    stages = {}
    for layer in range(depth):
        kind = _MIXERS[layer % 4]
        stages[kind, layer] = {"w_in": (kind + "_w_in", layer // 4), "w_out": (kind + "_w_out", layer // 4)}
        stages["ffn", layer] = {"w_up": ("ffn_w_up", layer), "w_down": ("ffn_w_down", layer)}

    order = list(stages)
    shards = lambda stage: [wts[nm][idx].astype(BF16) for nm, idx in stages[stage].values()]
    gathers = {order[0]: _Side(shards(order[0]), False)}
    gathers[order[0]].lands = _gather_stage(gathers[order[0]].srcs, name="gather_first")
    scatters, waiting = {}, []

    def ride(kind, layer, forward):
        if not forward:
            return waiting.pop() if waiting else None
        at = order.index((kind, layer)) + 1
        if at == len(order):
            return None
        gathers[order[at]] = _Side(shards(order[at]), False)
        return gathers[order[at]]

    lay_sw, r_sw = _layout({n: wts[n].shape for n in small_sharded}, 8, 8)
    got_sw = _all_gather(_pack(wts, lay_sw, r_sw, F32), name="gather_small_weights")
    parts = _unpack(got_sw.reshape(N_DEV, r_sw, LANES), lay_sw, (N_DEV,))
    full_small = {n: wts[n] for n in small_all if _SHARD_AXIS[n] is None}
    for n in small_sharded:
        ax, shape = _SHARD_AXIS[n], wts[n].shape
        full_small[n] = jnp.moveaxis(parts[n], 0, ax).reshape(shape[:ax] + (N_DEV * shape[ax],) + shape[ax + 1:])

    def get_big(kind, layer, after):
        mats = {}
        for (key, (nm, _)), land in zip(stages[kind, layer].items(), gathers[kind, layer].lands):
            _, r, c = land.shape
            if _SHARD_AXIS[nm] == 2:
                mats[key] = (land.transpose(1, 0, 2).reshape(r, N_DEV * c), land.transpose(0, 2, 1).reshape(N_DEV * c, r))
            else:
                mats[key] = (land.reshape(N_DEV * r, c), land.reshape(N_DEV * r, c).T)
        return _oriented(kind, mats)

    def put_big(kind, layer, g):
        if kind == "ffn":
            k, f = g["w_up_gate"].shape
            halves = [g[key].reshape(k, N_DEV // 2, 2 * f // N_DEV) for key in ("w_up_gate", "w_up_val")]
            w_in = jnp.concatenate(halves, axis=1)
        else:
            w_in = jnp.concatenate([g["w_main"], g["w_glr"][:, :GLA_RANK]], axis=1) if kind == "gla" else g["w_in"]
            w_in = w_in.reshape(w_in.shape[0], N_DEV, w_in.shape[1] // N_DEV)
        w_out = g["w_down"] if kind == "ffn" else g["w_out"]
        w_out = w_out.reshape(N_DEV, w_out.shape[0] // N_DEV, w_out.shape[1])
        scatters[kind, layer] = _Side([w_in.transpose(1, 0, 2).astype(BF16), w_out.astype(BF16)], True)
        waiting.append(scatters[kind, layer])
        return ()

    loss, dx, grads = _local_step(x.reshape(bsz * seq, d), loss_target.reshape(bsz * seq, d), _prep_small(full_small), seq,
                                  get_big, put_big, ride)
    loss = lax.psum(loss[0, 0], ("x", "y", "c"))

    gs = _small_grads(grads)
    small_repl = [n for n in small_all if _SHARD_AXIS[n] is None]
    lay_re, r_re = _layout({n: wts[n].shape for n in small_repl}, 8, 8)
    slots = {}
    for n in small_sharded:
        ax, shape = _SHARD_AXIS[n], wts[n].shape
        slots[n] = jnp.moveaxis(gs[n].reshape(shape[:ax] + (N_DEV, shape[ax]) + shape[ax + 1:]), ax, 0)
    sent = jnp.concatenate([_pack(gs, lay_re, r_re, F32), _pack(slots, lay_sw, r_sw, F32, (N_DEV,)).reshape(-1, LANES)])
    sum_re, sum_sh = _sum_small(_all_gather(sent, name="gather_small_grads"), r_re, r_sw, name="sum_small_grads")
    g_own = _unpack(sum_re, lay_re)
    g_own.update(_unpack(sum_sh, lay_sw))
    two_d = lambda a: a.reshape(-1, a.shape[-1])
    upd = _adamw_small(*[[two_d(src[n]) for n in small_all] for src in (g_own, wts, mom, var)], name="adamw_small")
    results = {n: [g_own[n]] + [part[i].reshape(wts[n].shape) for part in upd] for i, n in enumerate(small_all)}

    per_layer = {}
    for (kind, layer), side in scatters.items():
        lands = side.lands if side.lands is not None else _scatter_stage(side.srcs, name="scatter_last")
        for (nm, idx), land in zip(stages[kind, layer].values(), lands):
            three_d = lambda a: a.reshape((a.shape[0],) + land.shape[1:])
            per_layer.setdefault(nm, {})[idx] = _adamw_big(land, three_d(wts[nm]), three_d(mom[nm]), three_d(var[nm]), idx,
                                                           name="adamw_" + nm)
    for nm, by_idx in per_layer.items():
        outs = [by_idx[i] for i in range(len(by_idx))]
        results[nm] = [(outs[0][j] if len(outs) == 1 else jnp.stack([o[j] for o in outs])).reshape(wts[nm].shape)
                       for j in range(4)]
    out = [loss, dx.reshape(bsz, seq, d)]
    for j in range(4):
        out += [results[n][j] for n in _NAMES]
    return tuple(out)
```

```python
import functools
import math

import jax
import jax.numpy as jnp
import numpy as np
from jax import lax
from jax.experimental import pallas as pl
from jax.experimental.pallas import tpu as pltpu

F32 = jnp.float32
BF16 = jnp.bfloat16
EPS = 1e-6
N_DEV = 8
LANES = 128
SUBLANES_BF16 = 16
HALO = 32
GLA_HEADS, GLA_RANK, GLA_GATE_NORM, GLA_CHUNK = 4, 16, 16.0, 64
SGU_CHUNK, SGU_GROUPS = 128, 8
HGRN_EXPAND, HGRN_CHUNK = 128, 64
CONV_WIDTH, FFN_CONV_WIDTH = 31, 3
ADAM_LR, ADAM_B1, ADAM_B2, ADAM_EPS, ADAM_WD, ADAM_STEP = 0.001, 0.9, 0.999, 1e-08, 0.01, 10
MESH = pl.DeviceIdType.MESH


def _sigmoid(x):
    return 0.5 * (jnp.tanh(0.5 * x) + 1.0)


def _silu(x):
    return x * _sigmoid(x)


def _log_sigmoid(x):
    return jnp.minimum(x, 0.0) - jnp.log(1.0 + jnp.exp(-jnp.abs(x)))


def _gelu(x):
    return 0.5 * x * (1.0 + jnp.tanh(math.sqrt(2.0 / math.pi) * (x + 0.044715 * (x * x * x))))


def _rms(x, g):
    return x * lax.rsqrt(jnp.mean(x * x, axis=-1, keepdims=True) + EPS) * g


def _layer_norm(x, g, b):
    xc = x - jnp.mean(x, axis=-1, keepdims=True)
    return xc * lax.rsqrt(jnp.mean(xc * xc, axis=-1, keepdims=True) + EPS) * g + b


def _dot_raw(a, b, dims):
    return lax.dot_general(a.astype(BF16), b.astype(BF16), (dims, ((), ())), preferred_element_type=F32)


@jax.custom_vjp
def _bdot(a, b):
    return _dot_raw(a, b, ((1,), (0,)))


@jax.custom_vjp
def _bdot_nt(a, b):
    return _dot_raw(a, b, ((1,), (1,)))


@jax.custom_vjp
def _bdot_tn(a, b):
    return _dot_raw(a, b, ((0,), (0,)))


_bdot.defvjp(lambda a, b: (_bdot(a, b), (a, b)), lambda r, g: (_bdot_nt(g, r[1]), _bdot_tn(r[0], g)))
_bdot_nt.defvjp(lambda a, b: (_bdot_nt(a, b), (a, b)), lambda r, g: (_bdot(g, r[1]), _bdot_tn(g, r[0])))
_bdot_tn.defvjp(lambda a, b: (_bdot_tn(a, b), (a, b)), lambda r, g: (_bdot_nt(r[1], g), _bdot(r[0], g)))


def _scan_rows(x, reverse):
    n = x.shape[0]
    row = lax.broadcasted_iota(jnp.int32, x.shape, 0)
    step = 1
    while step < n:
        if reverse:
            x = x + jnp.where(row < n - step, pltpu.roll(x, n - step, 0), 0.0)
        else:
            x = x + jnp.where(row >= step, pltpu.roll(x, step, 0), 0.0)
        step *= 2
    return x


@jax.custom_vjp
def _cumsum_rows(x):
    return _scan_rows(x, False)


_cumsum_rows.defvjp(lambda x: (_scan_rows(x, False), None), lambda _, g: (_scan_rows(g, True),))


def _divisor_tile(n, cap, unit):
    if n <= cap:
        return n
    best = None
    for t in range(unit, cap + 1, unit):
        if n % t == 0:
            best = t
    assert best is not None, (n, cap, unit)
    return best


def _const_map(nd):
    return lambda *_: (0,) * nd


class _Side:
    def __init__(self, srcs, scatter, parts=()):
        self.srcs, self.scatter, self.lands, self.parts = list(srcs), scatter, None, list(parts)

    @staticmethod
    def join(sides):
        sides = [s for s in sides if s is not None]
        if len(sides) < 2:
            return sides[0] if sides else None
        assert len({s.scatter for s in sides}) == 1
        return _Side([a for s in sides for a in s.srcs], sides[0].scatter, sides)

    def landed(self, lands):
        self.lands = list(lands)
        at = 0
        for part in self.parts:
            part.landed(self.lands[at:at + len(part.srcs)])
            at += len(part.srcs)


def _pallas(body, side, *, name, grid, in_specs, out_specs, out_shape, scratch_shapes=(), semantics):
    if side is None:
        return pl.pallas_call(body, name=name, grid=grid, in_specs=in_specs, out_specs=out_specs, out_shape=out_shape,
                              scratch_shapes=list(scratch_shapes),
                              compiler_params=pltpu.CompilerParams(dimension_semantics=semantics))
    single = not isinstance(out_shape, (list, tuple))
    out_specs, out_shape = ([out_specs], [out_shape]) if single else (list(out_specs), list(out_shape))
    n, n_in, n_out, n_scr = len(side.srcs), len(in_specs), len(out_shape), len(scratch_shapes)
    lands = [jax.ShapeDtypeStruct((N_DEV,) + (s.shape[1:] if side.scatter else s.shape), s.dtype) for s in side.srcs]

    def body2(*refs):
        x_refs, land_refs = refs[n_in:n_in + n], refs[n_in + n + n_out:n_in + 2 * n + n_out]
        send_sems, recv_sems, local_sems = refs[-3:]
        steps = [pl.program_id(a) for a in range(len(grid))]
        first = functools.reduce(jnp.logical_and, [s == 0 for s in steps])
        last = functools.reduce(jnp.logical_and, [s == g - 1 for s, g in zip(steps, grid)])

        def copies():
            mx, my, mc = lax.axis_index("x"), lax.axis_index("y"), lax.axis_index("c")
            me = 4 * mx + 2 * my + mc
            mine = [pltpu.make_async_copy(x_refs[i].at[me] if side.scatter else x_refs[i], land_refs[i].at[me],
                                          local_sems.at[i]) for i in range(n)]
            sends, recvs = [], []
            for k in range(1, N_DEV):
                px = 1 - mx if k & 4 else mx
                py = 1 - my if k & 2 else my
                pc = 1 - mc if k & 1 else mc
                peer = 4 * px + 2 * py + pc
                for i in range(n):
                    sems = dict(send_sem=send_sems.at[7 * i + k - 1], recv_sem=recv_sems.at[7 * i + k - 1],
                                device_id=(px, py, pc), device_id_type=MESH)
                    src = x_refs[i].at[peer] if side.scatter else x_refs[i]
                    sends.append(pltpu.make_async_remote_copy(src_ref=src, dst_ref=land_refs[i].at[me], **sems))
                    recvs.append(pltpu.make_async_remote_copy(src_ref=src, dst_ref=land_refs[i].at[peer], **sems))
            return mine, sends, recvs

        @pl.when(first)
        def _():
            mine, sends, _ = copies()
            for cp in mine + sends:
                cp.start()

        body(*refs[:n_in], *refs[n_in + n:n_in + n + n_out], *refs[n_in + 2 * n + n_out:n_in + 2 * n + n_out + n_scr])

        @pl.when(last)
        def _():
            mine, sends, recvs = copies()
            for cp in recvs:
                cp.wait_recv()
            for cp in sends:
                cp.wait_send()
            for cp in mine:
                cp.wait()

    any_space = pl.BlockSpec(memory_space=pl.ANY)
    call = pl.pallas_call(
        body2, name=name, grid=grid, in_specs=list(in_specs) + [any_space] * n, out_specs=out_specs + [any_space] * n,
        out_shape=out_shape + lands,
        scratch_shapes=list(scratch_shapes) + [pltpu.SemaphoreType.DMA((7 * n,)), pltpu.SemaphoreType.DMA((7 * n,)),
                                               pltpu.SemaphoreType.DMA((n,))],
        compiler_params=pltpu.CompilerParams(dimension_semantics=("arbitrary",) * len(grid)))

    def run(*args):
        res = call(*args, *side.srcs)
        side.landed(res[n_out:])
        return res[0] if single else res[:n_out]

    return run


def _dep_specs(deps, grid_rank):
    return [pl.BlockSpec(d.shape, (lambda *_, nd=d.ndim: (0,) * nd)) for d in deps]


def _mm(a, b, *, add=None, bias=None, out_dtype=F32, name, deps=()):
    m, k = a.shape
    k2, n = b.shape
    assert k == k2
    tn = _divisor_tile(n, max(LANES, min(1408, (6 << 20) // (2 * k) // LANES * LANES)), LANES)
    tm = _divisor_tile(m, max(256, min(1024, (4 << 20) // (a.dtype.itemsize * k) // 256 * 256)), 8)
    has_bias, has_add = bias is not None, add is not None

    def body(*refs):
        a_ref, b_ref = refs[0], refs[1]
        o_ref = refs[-1]
        acc = jnp.dot(a_ref[...].astype(BF16), b_ref[...], preferred_element_type=F32)
        pos = 2
        if has_bias:
            acc = acc + refs[pos][...]
            pos += 1
        if has_add:
            acc = acc + refs[pos][...].astype(F32)
        o_ref[...] = acc.astype(o_ref.dtype)

    in_specs = [pl.BlockSpec((tm, k), lambda i, j: (i, 0)), pl.BlockSpec((k, tn), lambda i, j: (0, j))]
    args = [a, b]
    if has_bias:
        in_specs.append(pl.BlockSpec((1, tn), lambda i, j: (0, j)))
        args.append(bias)
    if has_add:
        in_specs.append(pl.BlockSpec((tm, tn), lambda i, j: (i, j)))
        args.append(add)
    in_specs += _dep_specs(deps, 2)
    args += list(deps)
    return pl.pallas_call(
        body, name=name, grid=(m // tm, n // tn), in_specs=in_specs,
        out_specs=pl.BlockSpec((tm, tn), lambda i, j: (i, j)),
        out_shape=jax.ShapeDtypeStruct((m, n), out_dtype),
        compiler_params=pltpu.CompilerParams(dimension_semantics=("parallel", "parallel")),
    )(*args)


def _mm_tn(a, g, *, name):
    m, k = a.shape
    m2, n = g.shape
    assert m == m2
    tk = _divisor_tile(k, 1408, LANES)
    tn = _divisor_tile(n, 1408, LANES)
    tm = _divisor_tile(m, 1024, 8)

    def body(a_ref, g_ref, o_ref):
        @pl.when(pl.program_id(2) == 0)
        def _():
            o_ref[...] = jnp.zeros_like(o_ref)

        o_ref[...] += _dot_raw(a_ref[...], g_ref[...], ((0,), (0,)))

    return pl.pallas_call(
        body, name=name, grid=(k // tk, n // tn, m // tm),
        in_specs=[pl.BlockSpec((tm, tk), lambda i, j, t: (t, i)), pl.BlockSpec((tm, tn), lambda i, j, t: (t, j))],
        out_specs=pl.BlockSpec((tk, tn), lambda i, j, t: (i, j)),
        out_shape=jax.ShapeDtypeStruct((k, n), F32),
        compiler_params=pltpu.CompilerParams(dimension_semantics=("parallel", "parallel", "arbitrary")),
    )(a, g)


def _tile_call(name, fn, tiled, params, out_tiled, out_acc, tile, deps=(), side=None):
    tiled = [t if isinstance(t, tuple) else (t, t.shape[1], 0) for t in tiled]
    t_rows = tiled[0][0].shape[0]
    tile = min(tile, t_rows)
    assert t_rows % tile == 0
    n_t, n_p, n_o, n_d = len(tiled), len(params), len(out_tiled), len(deps)

    def body(*refs):
        vals = [r[...] for r in refs[: n_t + n_p]]
        refs = refs[: n_t + n_p] + refs[n_t + n_p + n_d:]
        touts, aouts = fn(*vals)
        for r, v in zip(refs[n_t + n_p: n_t + n_p + n_o], touts):
            r[...] = v.astype(r.dtype)
        acc_refs = refs[n_t + n_p + n_o:]
        if acc_refs:
            @pl.when(pl.program_id(0) == 0)
            def _():
                for r in acc_refs:
                    r[...] = jnp.zeros_like(r)

            for r, v in zip(acc_refs, aouts):
                r[...] += v

    in_specs = [pl.BlockSpec((tile, w), lambda i, cb=cb: (i, cb)) for _, w, cb in tiled]
    in_specs += [pl.BlockSpec(p.shape, _const_map(p.ndim)) for p in params]
    in_specs += _dep_specs(deps, 1)
    out_specs = [pl.BlockSpec((tile, w), lambda i: (i, 0)) for w, _ in out_tiled]
    out_specs += [pl.BlockSpec(s, _const_map(len(s))) for s in out_acc]
    out_shape = [jax.ShapeDtypeStruct((t_rows, w), dt) for w, dt in out_tiled]
    out_shape += [jax.ShapeDtypeStruct(s, F32) for s in out_acc]
    res = _pallas(
        body, side, name=name, grid=(t_rows // tile,), in_specs=in_specs, out_specs=out_specs, out_shape=out_shape,
        semantics=("arbitrary" if out_acc else "parallel",),
    )(*[t[0] for t in tiled], *params, *deps)
    return res[:n_o], res[n_o:]


def _rms_fwd(x, g, name, deps=()):
    (h,), _ = _tile_call(name, lambda xv, gv: ([_rms(xv, gv)], []), [x], [g], [(x.shape[1], BF16)], [], 512, deps)
    return h


def _rms_bwd(x, g, dh, dres, name):
    def fn(xv, dhv, drv, gv):
        _, vjp = jax.vjp(_rms, xv, gv)
        dx, dg = vjp(dhv.astype(F32))
        return [drv + dx], [dg]

    (dx,), (dg,) = _tile_call(name, fn, [x, dh, dres], [g], [(x.shape[1], F32)], [g.shape], 512)
    return dx, dg


def _colsum(x, name):
    _, (s,) = _tile_call(name, lambda xv: ([], [jnp.sum(xv.astype(F32), axis=0, keepdims=True)]), [x], [], [],
                         [(1, x.shape[1])], 512)
    return s


def _seq_flags(i, tiles_per_seq):
    pos = i % tiles_per_seq
    return pos == 0, pos == tiles_per_seq - 1


def _dwconv_fwd(x, w, b, seq, name, side=None):
    t_rows, ch = x.shape
    kw = w.shape[0]
    tile = min(512, seq)
    cb = _divisor_tile(ch, 256, LANES)
    tps, hb = seq // tile, tile // HALO

    def body(x_ref, halo_ref, w_ref, b_ref, y_ref, pad_ref):
        first, _ = _seq_flags(pl.program_id(0), tps)
        pad_ref[0:HALO, :] = jnp.where(first, 0.0, halo_ref[...])
        pad_ref[HALO:HALO + tile, :] = x_ref[...]
        for r0 in range(0, tile, HALO):
            acc = jnp.broadcast_to(b_ref[...], (HALO, cb))
            for k in range(kw):
                acc = acc + pad_ref[pl.ds(HALO - (kw - 1) + k + r0, HALO), :] * w_ref[k:k + 1, :]
            y_ref[pl.ds(r0, HALO), :] = acc

    return _pallas(
        body, side, name=name, grid=(t_rows // tile, ch // cb),
        in_specs=[pl.BlockSpec((tile, cb), lambda i, j: (i, j)),
                  pl.BlockSpec((HALO, cb), lambda i, j: (jnp.maximum(i * hb - 1, 0), j)),
                  pl.BlockSpec((kw, cb), lambda i, j: (0, j)), pl.BlockSpec((1, cb), lambda i, j: (0, j))],
        out_specs=pl.BlockSpec((tile, cb), lambda i, j: (i, j)),
        out_shape=jax.ShapeDtypeStruct((t_rows, ch), F32),
        scratch_shapes=[pltpu.VMEM((HALO + tile, cb), F32)],
        semantics=("parallel", "parallel"),
    )(x, x, w, b)


def _dwconv_bwd(x, dy, w, seq, name, side=None):
    t_rows, ch = x.shape
    kw = w.shape[0]
    tile = min(512, seq)
    cb = _divisor_tile(ch, 256, LANES)
    tps, hb, n_hb = seq // tile, tile // HALO, t_rows // HALO

    def body(x_ref, xh_ref, dy_ref, dyh_ref, w_ref, dx_ref, dw_ref, db_ref, xpad, dypad, sums):
        i = pl.program_id(1)
        first, last = _seq_flags(i, tps)

        @pl.when(i == 0)
        def _():
            sums[...] = jnp.zeros_like(sums)

        xpad[0:HALO, :] = jnp.where(first, 0.0, xh_ref[...])
        xpad[HALO:HALO + tile, :] = x_ref[...]
        dypad[0:tile, :] = dy_ref[...]
        dypad[tile:tile + HALO, :] = jnp.where(last, 0.0, dyh_ref[...])
        fold = lambda v: functools.reduce(jnp.add, [v[r:r + 8] for r in range(0, HALO, 8)])
        for r0 in range(0, tile, HALO):
            dyc = dy_ref[pl.ds(r0, HALO), :]
            acc = jnp.zeros((HALO, cb), F32)
            for k in range(kw):
                acc = acc + dypad[pl.ds(kw - 1 - k + r0, HALO), :] * w_ref[k:k + 1, :]
                sums[8 * k:8 * k + 8, :] += fold(dyc * xpad[pl.ds(HALO - (kw - 1) + k + r0, HALO), :])
            dx_ref[pl.ds(r0, HALO), :] = acc
            sums[8 * kw:8 * kw + 8, :] += fold(dyc)

        @pl.when(i == t_rows // tile - 1)
        def _():
            for k in range(kw):
                dw_ref[k:k + 1, :] = jnp.sum(sums[8 * k:8 * k + 8, :], axis=0, keepdims=True)
            db_ref[...] = jnp.sum(sums[8 * kw:8 * kw + 8, :], axis=0, keepdims=True)

    return _pallas(
        body, side, name=name, grid=(ch // cb, t_rows // tile),
        in_specs=[pl.BlockSpec((tile, cb), lambda j, i: (i, j)),
                  pl.BlockSpec((HALO, cb), lambda j, i: (jnp.maximum(i * hb - 1, 0), j)),
                  pl.BlockSpec((tile, cb), lambda j, i: (i, j)),
                  pl.BlockSpec((HALO, cb), lambda j, i: (jnp.minimum((i + 1) * hb, n_hb - 1), j)),
                  pl.BlockSpec((kw, cb), lambda j, i: (0, j))],
        out_specs=[pl.BlockSpec((tile, cb), lambda j, i: (i, j)), pl.BlockSpec((kw, cb), lambda j, i: (0, j)),
                   pl.BlockSpec((1, cb), lambda j, i: (0, j))],
        out_shape=[jax.ShapeDtypeStruct((t_rows, ch), F32), jax.ShapeDtypeStruct((kw, ch), F32),
                   jax.ShapeDtypeStruct((1, ch), F32)],
        scratch_shapes=[pltpu.VMEM((HALO + tile, cb), F32), pltpu.VMEM((tile + HALO, cb), F32),
                        pltpu.VMEM((8 * (kw + 1), cb), F32)],
        semantics=("parallel", "arbitrary"),
    )(x, x, dy, dy, w)


_ROWS = SUBLANES_BF16


def _lane_chunks(width, cap=6 * LANES):
    return [slice(c0, min(c0 + cap, width)) for c0 in range(0, width, cap)]


def _tap_rows(w_ref, cols):
    return [w_ref[k:k + 1, cols] for k in range(FFN_CONV_WIDTH)]


def _conv3_at(pad, taps, row, cols):
    z = pad[pl.ds(row, _ROWS), cols] * taps[2]
    z = z + pad[pl.ds(row - 1, _ROWS), cols] * taps[1]
    return z + pad[pl.ds(row - 2, _ROWS), cols] * taps[0]


def _ffn_mid_fwd(u, w, seq, name, side=None):
    t_rows, f2 = u.shape
    f = f2 // 2
    tile = min(256, seq)
    cb = _divisor_tile(f, 1408, LANES)
    nj, tps, hb, hl = f // cb, seq // tile, tile // SUBLANES_BF16, SUBLANES_BF16

    def body(ug_ref, uv_ref, hg_ref, hv_ref, wg_ref, wv_ref, a_ref, gpad, vpad):
        first, _ = _seq_flags(pl.program_id(0), tps)
        for t_ref, h_ref, pad in ((ug_ref, hg_ref, gpad), (uv_ref, hv_ref, vpad)):
            pad[0:hl, :] = jnp.where(first, 0.0, h_ref[...].astype(F32))
            pad[hl:hl + tile, :] = t_ref[...].astype(F32)
        for cols in _lane_chunks(cb):
            wg, wv = _tap_rows(wg_ref, cols), _tap_rows(wv_ref, cols)
            for r0 in range(0, tile, _ROWS):
                zg = _conv3_at(gpad, wg, hl + r0, cols)
                zv = _conv3_at(vpad, wv, hl + r0, cols)
                half = 0.5 * zg
                a_ref[pl.ds(r0, _ROWS), cols] = ((jnp.tanh(half) + 1.0) * half * zv).astype(a_ref.dtype)

    halo_map = lambda off: (lambda i, j: (jnp.maximum(i * hb - 1, 0), j + off))
    return _pallas(
        body, side, name=name, grid=(t_rows // tile, nj),
        in_specs=[pl.BlockSpec((tile, cb), lambda i, j: (i, j)), pl.BlockSpec((tile, cb), lambda i, j: (i, j + nj)),
                  pl.BlockSpec((hl, cb), halo_map(0)), pl.BlockSpec((hl, cb), halo_map(nj)),
                  pl.BlockSpec((3, cb), lambda i, j: (0, j)), pl.BlockSpec((3, cb), lambda i, j: (0, j + nj))],
        out_specs=pl.BlockSpec((tile, cb), lambda i, j: (i, j)),
        out_shape=jax.ShapeDtypeStruct((t_rows, f), BF16),
        scratch_shapes=[pltpu.VMEM((hl + tile, cb), F32), pltpu.VMEM((hl + tile, cb), F32)],
        semantics=("parallel", "parallel"),
    )(u, u, u, u, w, w)


def _ffn_mid_bwd(u, da, w, seq, name, side=None):
    t_rows, f2 = u.shape
    f = f2 // 2
    tile = min(256, seq)
    cb = _divisor_tile(f, 1408, LANES)
    hl = SUBLANES_BF16
    nj, tps, hb, n_hb, ext = f // cb, seq // tile, tile // hl, t_rows // hl, tile + hl

    def body(ug_ref, uv_ref, pg_ref, pv_ref, ng_ref, nv_ref, da_ref, dan_ref, wg_ref, wv_ref,
             dug_ref, duv_ref, dwg_ref, dwv_ref, gpad, vpad, dzg, dzv):
        i = pl.program_id(1)
        first, last = _seq_flags(i, tps)

        @pl.when(i == 0)
        def _():
            dwg_ref[...] = jnp.zeros_like(dwg_ref)
            dwv_ref[...] = jnp.zeros_like(dwv_ref)

        for t_ref, p_ref, n_ref, pad in ((ug_ref, pg_ref, ng_ref, gpad), (uv_ref, pv_ref, nv_ref, vpad)):
            pad[0:hl, :] = jnp.where(first, 0.0, p_ref[...].astype(F32))
            pad[hl:hl + tile, :] = t_ref[...].astype(F32)
            pad[hl + tile:hl + ext, :] = jnp.where(last, 0.0, n_ref[...].astype(F32))
        for cols in _lane_chunks(cb):
            wg, wv = _tap_rows(wg_ref, cols), _tap_rows(wv_ref, cols)
            for r0 in range(0, ext, _ROWS):
                zg = _conv3_at(gpad, wg, hl + r0, cols)
                zv = _conv3_at(vpad, wv, hl + r0, cols)
                if r0 < tile:
                    da = da_ref[pl.ds(r0, _ROWS), cols].astype(F32)
                else:
                    da = jnp.where(last, 0.0, dan_ref[:, cols].astype(F32))
                sg = _sigmoid(zg)
                dzg[pl.ds(r0, _ROWS), cols] = da * zv * (sg * (1.0 + zg * (1.0 - sg)))
                dzv[pl.ds(r0, _ROWS), cols] = da * (zg * sg)
        for dz, w_ref, pad, du_ref, dw_ref in ((dzg, wg_ref, gpad, dug_ref, dwg_ref), (dzv, wv_ref, vpad, duv_ref, dwv_ref)):
            for cols in _lane_chunks(cb):
                taps = _tap_rows(w_ref, cols)
                width = cols.stop - cols.start
                acc = [jnp.zeros((8, width), F32) for _ in range(FFN_CONV_WIDTH)]
                for r0 in range(0, tile, _ROWS):
                    d0 = dz[pl.ds(r0, _ROWS), cols]
                    du = dz[pl.ds(r0 + 2, _ROWS), cols] * taps[0] + dz[pl.ds(r0 + 1, _ROWS), cols] * taps[1] + d0 * taps[2]
                    du_ref[pl.ds(r0, _ROWS), cols] = du.astype(du_ref.dtype)
                    for k in range(FFN_CONV_WIDTH):
                        prod = d0 * pad[pl.ds(hl - 2 + k + r0, _ROWS), cols]
                        acc[k] = acc[k] + prod[0:8] + prod[8:16]
                for k in range(FFN_CONV_WIDTH):
                    dw_ref[k:k + 1, cols] += jnp.sum(acc[k], axis=0, keepdims=True)

    prev_map = lambda off: (lambda j, i: (jnp.maximum(i * hb - 1, 0), j + off))
    next_map = lambda off: (lambda j, i: (jnp.minimum((i + 1) * hb, n_hb - 1), j + off))
    tile_spec = lambda off: pl.BlockSpec((tile, cb), lambda j, i: (i, j + off))
    w_spec = lambda off: pl.BlockSpec((3, cb), lambda j, i: (0, j + off))
    return _pallas(
        body, side, name=name, grid=(nj, t_rows // tile),
        in_specs=[tile_spec(0), tile_spec(nj), pl.BlockSpec((hl, cb), prev_map(0)), pl.BlockSpec((hl, cb), prev_map(nj)),
                  pl.BlockSpec((hl, cb), next_map(0)), pl.BlockSpec((hl, cb), next_map(nj)),
                  tile_spec(0), pl.BlockSpec((hl, cb), next_map(0)), w_spec(0), w_spec(nj)],
        out_specs=[tile_spec(0), tile_spec(0), w_spec(0), w_spec(0)],
        out_shape=[jax.ShapeDtypeStruct((t_rows, f), BF16), jax.ShapeDtypeStruct((t_rows, f), BF16),
                   jax.ShapeDtypeStruct((3, f), F32), jax.ShapeDtypeStruct((3, f), F32)],
        scratch_shapes=[pltpu.VMEM((hl + ext, cb), F32), pltpu.VMEM((hl + ext, cb), F32),
                        pltpu.VMEM((ext, cb), F32), pltpu.VMEM((ext, cb), F32)],
        semantics=("parallel", "arbitrary"),
    )(u, u, u, u, u, u, da, da, w, w)


def _gla_chunk(q, k, v, lg, st, *, scale, chunk):
    row = lax.broadcasted_iota(jnp.int32, (chunk, chunk), 0)
    col = lax.broadcasted_iota(jnp.int32, (chunk, chunk), 1)
    causal = col <= row
    b = _cumsum_rows(lg)
    upto_mid = lax.broadcasted_iota(jnp.int32, lg.shape, 0) <= chunk // 2
    b_mid = jnp.sum(jnp.where(upto_mid, lg, 0.0), axis=0, keepdims=True)
    b_last = jnp.sum(lg, axis=0, keepdims=True)
    qs = q * scale
    scores = _bdot_nt(qs * jnp.exp(b - b_mid), k * jnp.exp(b_mid - b))
    o = _bdot(jnp.where(causal, scores, 0.0), v)
    o = o + _bdot_nt(qs * jnp.exp(b), st)
    st_new = st * jnp.exp(b_last) + _bdot_tn(v, k * jnp.exp(b_last - b))
    return o, st_new


def _gla_specs(specs, chunk, n_chunks, reverse):
    if reverse:
        row = lambda bi, ci: bi * n_chunks + (n_chunks - 1 - ci)
    else:
        row = lambda bi, ci: bi * n_chunks + ci
    return [pl.BlockSpec((chunk, w), lambda bi, ci, cb=cb: (row(bi, ci), cb)) for _, w, cb in specs], row


def _gla_fwd(q, k, v, lg, *, heads, dk, dv, scale, chunk, seq, name, side=None):
    t_rows = q[0].shape[0]
    n_chunks = seq // chunk
    fn = functools.partial(_gla_chunk, scale=scale, chunk=chunk)

    def body(q_ref, k_ref, v_ref, lg_ref, o_ref, sts_ref, st_ref):
        @pl.when(pl.program_id(1) == 0)
        def _():
            st_ref[...] = jnp.zeros_like(st_ref)

        sts_ref[0] = st_ref[...]
        ks = [slice(h * dk, (h + 1) * dk) for h in range(heads)]
        vs = [slice(h * dv, (h + 1) * dv) for h in range(heads)]
        ins = [(q_ref[:, ks[h]].astype(F32), k_ref[:, ks[h]].astype(F32), v_ref[:, vs[h]].astype(F32), lg_ref[:, ks[h]],
                st_ref[vs[h], :]) for h in range(heads)]
        outs = [fn(*args) for args in ins]
        for h, (o, st) in enumerate(outs):
            o_ref[:, vs[h]] = o
            st_ref[vs[h], :] = st

    in_specs, row = _gla_specs([q, k, v, lg], chunk, n_chunks, False)
    return _pallas(
        body, side, name=name, grid=(t_rows // seq, n_chunks), in_specs=in_specs,
        out_specs=[pl.BlockSpec((chunk, heads * dv), lambda bi, ci: (row(bi, ci), 0)),
                   pl.BlockSpec((1, heads * dv, dk), lambda bi, ci: (row(bi, ci), 0, 0))],
        out_shape=[jax.ShapeDtypeStruct((t_rows, heads * dv), F32),
                   jax.ShapeDtypeStruct((t_rows // chunk, heads * dv, dk), F32)],
        scratch_shapes=[pltpu.VMEM((heads * dv, dk), F32)],
        semantics=("arbitrary", "arbitrary"),
    )(q[0], k[0], v[0], lg[0])


def _gla_bwd(q, k, v, lg, states, do, *, heads, dk, dv, scale, chunk, seq, out_dtypes, name, side=None):
    t_rows = q[0].shape[0]
    n_chunks = seq // chunk
    fn = functools.partial(_gla_chunk, scale=scale, chunk=chunk)

    def body(q_ref, k_ref, v_ref, lg_ref, do_ref, sts_ref, dq_ref, dk_ref, dv_ref, dlg_ref, dst_ref):
        @pl.when(pl.program_id(1) == 0)
        def _():
            dst_ref[...] = jnp.zeros_like(dst_ref)

        ks = [slice(h * dk, (h + 1) * dk) for h in range(heads)]
        vs = [slice(h * dv, (h + 1) * dv) for h in range(heads)]
        ins = [(q_ref[:, ks[h]].astype(F32), k_ref[:, ks[h]].astype(F32), v_ref[:, vs[h]].astype(F32), lg_ref[:, ks[h]],
                sts_ref[0, vs[h], :]) for h in range(heads)]
        cts = [(do_ref[:, vs[h]].astype(F32), dst_ref[vs[h], :]) for h in range(heads)]
        outs = [jax.vjp(fn, *ins[h])[1](cts[h]) for h in range(heads)]
        for h, (dq, dkk, dvv, dlg, dst) in enumerate(outs):
            dq_ref[:, ks[h]] = dq.astype(dq_ref.dtype)
            dk_ref[:, ks[h]] = dkk.astype(dk_ref.dtype)
            dv_ref[:, vs[h]] = dvv.astype(dv_ref.dtype)
            dlg_ref[:, ks[h]] = dlg
            dst_ref[vs[h], :] = dst

    do_view = (do, heads * dv, 0)
    in_specs, row = _gla_specs([q, k, v, lg, do_view], chunk, n_chunks, True)
    in_specs.append(pl.BlockSpec((1, heads * dv, dk), lambda bi, ci: (row(bi, ci), 0, 0)))
    wide = lambda w: pl.BlockSpec((chunk, w), lambda bi, ci: (row(bi, ci), 0))
    return _pallas(
        body, side, name=name, grid=(t_rows // seq, n_chunks), in_specs=in_specs,
        out_specs=[wide(heads * dk), wide(heads * dk), wide(heads * dv), wide(heads * dk)],
        out_shape=[jax.ShapeDtypeStruct((t_rows, heads * dk), out_dtypes[0]),
                   jax.ShapeDtypeStruct((t_rows, heads * dk), out_dtypes[1]),
                   jax.ShapeDtypeStruct((t_rows, heads * dv), out_dtypes[2]),
                   jax.ShapeDtypeStruct((t_rows, heads * dk), F32)],
        scratch_shapes=[pltpu.VMEM((heads * dv, dk), F32)],
        semantics=("arbitrary", "arbitrary"),
    )(q[0], k[0], v[0], lg[0], do, states)


def _head_rms_gate(o, r, g, heads):
    d = o.shape[1] // heads
    parts = [_rms(o[:, h * d:(h + 1) * d], g) for h in range(heads)]
    return jnp.concatenate(parts, axis=1) * _silu(r)


def _gla_gate(glr, w_g2p, b_g2):
    return _log_sigmoid(_bdot(glr, w_g2p) + b_g2) * (1.0 / GLA_GATE_NORM)


def _glu(a, gate, b_in):
    d = a.shape[1]
    return (a + b_in[:, :d]) * _sigmoid(gate + b_in[:, d:])


def _ln_silu(y, g, b):
    return _silu(_layer_norm(y, g, b))


def _sgu(pre, b_in, ln_g, ln_b, w_s, b_st):
    d = pre.shape[1] // 2
    gd = d // SGU_GROUPS
    uv = _gelu(pre + b_in)
    u, v = uv[:, :d], _layer_norm(uv[:, d:], ln_g, ln_b)
    row = lax.broadcasted_iota(jnp.int32, (SGU_CHUNK, SGU_CHUNK), 0)
    col = lax.broadcasted_iota(jnp.int32, (SGU_CHUNK, SGU_CHUNK), 1)
    lane = lax.broadcasted_iota(jnp.int32, b_st.shape, 1)
    rows = []
    for c in range(pre.shape[0] // SGU_CHUNK):
        rs = slice(c * SGU_CHUNK, (c + 1) * SGU_CHUNK)
        parts = []
        for g in range(SGU_GROUPS):
            wg = jnp.where(col <= row, w_s[g], 0.0)
            bias = jnp.sum(jnp.where(lane == g, b_st, 0.0), axis=1, keepdims=True)
            parts.append(_bdot(wg, v[rs, g * gd:(g + 1) * gd]) + bias)
        rows.append(jnp.concatenate(parts, axis=1))
    s = rows[0] if len(rows) == 1 else jnp.concatenate(rows, axis=0)
    return u * s


def _hgrn_pre(q, f, table, layer):
    t = table - jnp.max(table, axis=0, keepdims=True)
    e = jnp.exp(t)
    sm = e / jnp.sum(e, axis=0, keepdims=True)
    rows = lax.broadcasted_iota(jnp.int32, table.shape, 0)
    lb = jnp.sum(jnp.where((rows >= 1) & (rows <= layer), sm, 0.0), axis=0, keepdims=True)
    sf = _sigmoid(f)
    return _silu(q), (1.0 - lb) * (1.0 - sf), jnp.log(lb + (1.0 - lb) * sf)


def _ffn_fwd(x, w, seq, sv):
    sv["h2"] = _rms_fwd(x, w["norm"], "ffn_norm")
    sv["u"] = _mm(sv["h2"], w["w_up"], out_dtype=BF16, name="ffn_up")
    late = w.get("late")
    side = _Side.join([sv.pop("side", None), late[0] if late else None])
    sv["a"] = _ffn_mid_fwd(sv["u"], w["w_dw"], seq, "ffn_mid", side)
    if late:
        sv["late_w"] = late[1]()
        w = dict(w, **sv["late_w"])
    return _mm(sv["a"], w["w_down"], add=x, name="ffn_down")


def _ffn_bwd(x, dy, w, seq, sv, deps=()):
    g = {}
    da = _mm(dy, w["w_down_t"], out_dtype=BF16, name="ffn_down_dx", deps=deps)
    g["w_down"] = _mm_tn(sv["a"], dy, name="ffn_down_dw")
    early = sv.pop("put_early", None)
    side = _Side.join([sv.pop("side", None), early(g["w_down"]) if early else None])
    dug, duv, dwg, dwv = _ffn_mid_bwd(sv["u"], da, w["w_dw"], seq, "ffn_mid_bwd", side)
    g["w_dw"] = jnp.concatenate([dwg, dwv], axis=1)
    g["w_up_gate"] = _mm_tn(sv["h2"], dug, name="ffn_up_dw")
    g["w_up_val"] = _mm_tn(sv["h2"], duv, name="ffn_up_dw")
    dh = _mm(dug, w["w_up_t_gate"], out_dtype=F32, name="ffn_up_dx")
    dh = _mm(duv, w["w_up_t_val"], add=dh, out_dtype=BF16, name="ffn_up_dx2")
    dx, g["norm"] = _rms_bwd(x, w["norm"], dh, dy, "ffn_norm_bwd")
    return dx, g


def _gla_layer_fwd(x, h, w, seq, sv):
    d = x.shape[1]
    dkt = d // 2
    dk, dv = dkt // GLA_HEADS, d // GLA_HEADS
    proj = _mm(h, w["w_main"], out_dtype=F32, name="gla_in")
    glr = _mm(h, w["w_glr"], out_dtype=BF16, name="gla_in_g")
    (lg,), _ = _tile_call("gla_gate", lambda a, b, c: ([_gla_gate(a.astype(F32), b, c)], []), [glr],
                          [w["w_g2p"], w["b_g2"]], [(dkt, F32)], [], 512)
    q, k, v, r = (proj, dkt, 0), (proj, dkt, 1), (proj, d, 1), (proj, d, 2)
    o, states = _gla_fwd(q, k, v, (lg, dkt, 0), heads=GLA_HEADS, dk=dk, dv=dv, scale=dk ** -0.5, chunk=GLA_CHUNK,
                         seq=seq, name="gla_core", side=sv.pop("side", None))
    (o2,), _ = _tile_call("gla_post", lambda ov, rv, gv: ([_head_rms_gate(ov, rv.astype(F32), gv, GLA_HEADS)], []),
                          [o, r], [w["norm"]], [(d, BF16)], [], 256)
    sv.update(proj=proj, glr=glr, lg=lg, o=o, states=states, o2=o2)
    return _mm(o2, w["w_out"], add=x, name="mix_out")


def _gla_layer_bwd(h, dy, w, seq, sv, deps=()):
    d = dy.shape[1]
    dkt = d // 2
    dk, dv = dkt // GLA_HEADS, d // GLA_HEADS
    proj, glr, lg, o = sv["proj"], sv["glr"], sv["lg"], sv["o"]
    g = {}
    do2 = _mm(dy, w["w_out_t"], out_dtype=F32, name="gla_out_dx", deps=deps)
    g["w_out"] = _mm_tn(sv["o2"], dy, name="mix_out_dw")

    def post_bwd(ov, rv, ctv, gv):
        _, vjp = jax.vjp(functools.partial(_head_rms_gate, heads=GLA_HEADS), ov, rv.astype(F32), gv)
        d_o, d_r, d_g = vjp(ctv.astype(F32))
        return [d_o, d_r], [d_g]

    (d_o, d_r), (g["norm"],) = _tile_call("gla_post_bwd", post_bwd, [o, (proj, d, 2), do2], [w["norm"]],
                                          [(d, F32), (d, BF16)], [w["norm"].shape], 256)
    q, k, v = (proj, dkt, 0), (proj, dkt, 1), (proj, d, 1)
    dq, dkk, dvv, dlg = _gla_bwd(q, k, v, (lg, dkt, 0), sv["states"], d_o, heads=GLA_HEADS, dk=dk, dv=dv,
                                 scale=dk ** -0.5, chunk=GLA_CHUNK, seq=seq, out_dtypes=(BF16, BF16, BF16),
                                 name="gla_core_bwd", side=sv.pop("side", None))

    def gate_bwd(glrv, ctv, wv, bv):
        _, vjp = jax.vjp(_gla_gate, glrv.astype(F32), wv, bv)
        d_glr, d_w, d_b = vjp(ctv)
        return [d_glr], [d_w, d_b]

    (dglr,), (g["w_g2p"], g["b_g2"]) = _tile_call("gla_gate_bwd", gate_bwd, [glr, dlg], [w["w_g2p"], w["b_g2"]],
                                                  [(LANES, BF16)], [w["w_g2p"].shape, w["b_g2"].shape], 512)
    dproj = jnp.concatenate([dq, dkk, dvv, d_r], axis=1)
    g["w_main"] = _mm_tn(h, dproj, name="gla_in_dw")
    g["w_glr"] = _mm_tn(h, dglr, name="gla_in_g_dw")
    dh = _mm(dproj, w["w_main_t"], out_dtype=F32, name="gla_in_dx")
    dh = _mm(dglr, w["w_glr_t"], add=dh, out_dtype=BF16, name="gla_in_g_dx")
    return dh, g


def _cv_layer_fwd(x, h, w, seq, sv):
    d = x.shape[1]
    pre = _mm(h, w["w_in"], out_dtype=BF16, name="cv_in")
    (y1,), _ = _tile_call("cv_glu", lambda a, gt, b: ([_glu(a.astype(F32), gt.astype(F32), b)], []),
                          [(pre, d, 0), (pre, d, 1)], [w["b_in"]], [(d, F32)], [], 512)
    y2 = _dwconv_fwd(y1, w["w_dw"], w["b_dw"], seq, "cv_conv", sv.pop("side", None))
    (y3,), _ = _tile_call("cv_ln", lambda y, a, b: ([_ln_silu(y, a, b)], []), [y2], [w["ln_g"], w["ln_b"]],
                          [(d, BF16)], [], 512)
    sv.update(pre=pre, y1=y1, y2=y2, y3=y3)
    return _mm(y3, w["w_out"], bias=w["b_out"], add=x, name="mix_out_b")


def _cv_layer_bwd(h, dy, w, seq, sv, deps=()):
    d = dy.shape[1]
    pre = sv["pre"]
    g = {}
    dy3 = _mm(dy, w["w_out_t"], out_dtype=BF16, name="mix_out_dx", deps=deps)
    g["w_out"] = _mm_tn(sv["y3"], dy, name="mix_out_dw")
    g["b_out"] = _colsum(dy, "bias_out_dw")

    def ln_bwd(yv, ctv, av, bv):
        _, vjp = jax.vjp(_ln_silu, yv, av, bv)
        d_y, d_a, d_b = vjp(ctv.astype(F32))
        return [d_y], [d_a, d_b]

    (dy2,), (g["ln_g"], g["ln_b"]) = _tile_call("cv_ln_bwd", ln_bwd, [sv["y2"], dy3], [w["ln_g"], w["ln_b"]],
                                                [(d, F32)], [w["ln_g"].shape, w["ln_b"].shape], 512)
    dy1, g["w_dw"], g["b_dw"] = _dwconv_bwd(sv["y1"], dy2, w["w_dw"], seq, "cv_conv_bwd", sv.pop("side", None))

    def glu_bwd(av, gv, ctv, bv):
        _, vjp = jax.vjp(_glu, av.astype(F32), gv.astype(F32), bv)
        d_a, d_g, d_b = vjp(ctv)
        return [jnp.concatenate([d_a, d_g], axis=1)], [d_b]

    (dpre,), (g["b_in"],) = _tile_call("cv_glu_bwd", glu_bwd, [(pre, d, 0), (pre, d, 1), dy1], [w["b_in"]],
                                       [(2 * d, BF16)], [w["b_in"].shape], 512)
    g["w_in"] = _mm_tn(h, dpre, name="in2_dw")
    dh = _mm(dpre, w["w_in_t"], out_dtype=BF16, name="in2_dx")
    return dh, g


def _sg_layer_fwd(x, h, w, seq, sv):
    d = x.shape[1]
    pre = _mm(h, w["w_in"], out_dtype=BF16, name="sg_in")
    pars = [w["b_in"], w["ln_g"], w["ln_b"], w["w_s"], w["b_st"]]
    (p,), _ = _tile_call("sg_gate", lambda pv, *ps: ([_sgu(pv.astype(F32), *ps)], []), [pre], pars, [(d, BF16)], [],
                         SGU_CHUNK, side=sv.pop("side", None))
    sv.update(pre=pre, p=p)
    return _mm(p, w["w_out"], bias=w["b_out"], add=x, name="mix_out_b")


def _sg_layer_bwd(h, dy, w, seq, sv, deps=()):
    d = dy.shape[1]
    g = {}
    dp = _mm(dy, w["w_out_t"], out_dtype=BF16, name="mix_out_dx", deps=deps)
    g["w_out"] = _mm_tn(sv["p"], dy, name="mix_out_dw")
    g["b_out"] = _colsum(dy, "bias_out_dw")
    pars = [w["b_in"], w["ln_g"], w["ln_b"], w["w_s"], w["b_st"]]

    def sgu_bwd(pv, ctv, *ps):
        _, vjp = jax.vjp(_sgu, pv.astype(F32), *ps)
        grads = vjp(ctv.astype(F32))
        return [grads[0]], list(grads[1:])

    (dpre,), (g["b_in"], g["ln_g"], g["ln_b"], g["w_s"], g["b_st"]) = _tile_call(
        "sg_gate_bwd", sgu_bwd, [sv["pre"], dp], pars, [(2 * d, BF16)], [p.shape for p in pars], SGU_CHUNK,
        side=sv.pop("side", None))
    g["w_in"] = _mm_tn(h, dpre, name="in2_dw")
    dh = _mm(dpre, w["w_in_t"], out_dtype=BF16, name="in2_dx")
    return dh, g


def _hg_layer_fwd(x, h, w, seq, sv, layer):
    d = x.shape[1]
    heads = d // HGRN_EXPAND
    proj = _mm(h, w["w_in"], out_dtype=BF16, name="hg_in")
    pre = functools.partial(_hgrn_pre, layer=layer)
    (qs, kk, lg), _ = _tile_call("hg_pre", lambda qv, fv, tb: (list(pre(qv.astype(F32), fv.astype(F32), tb)), []),
                                 [(proj, d, 0), (proj, d, 1)], [w["lb_table"]], [(d, BF16), (d, F32), (d, F32)], [], 256)
    o, states = _gla_fwd((qs, d, 0), (kk, d, 0), (proj, d, 2), (lg, d, 0), heads=heads, dk=HGRN_EXPAND,
                         dv=HGRN_EXPAND, scale=1.0, chunk=HGRN_CHUNK, seq=seq, name="hg_core",
                         side=sv.pop("side", None))
    (o2,), _ = _tile_call("hg_post", lambda ov, gv, nv: ([_head_rms_gate(ov, gv.astype(F32), nv, heads)], []),
                          [o, (proj, d, 3)], [w["norm"]], [(d, BF16)], [], 256)
    sv.update(proj=proj, qs=qs, kk=kk, lg=lg, o=o, states=states, o2=o2)
    return _mm(o2, w["w_out"], add=x, name="mix_out")


def _hg_layer_bwd(h, dy, w, seq, sv, layer, deps=()):
    d = dy.shape[1]
    heads = d // HGRN_EXPAND
    proj = sv["proj"]
    g = {}
    do2 = _mm(dy, w["w_out_t"], out_dtype=BF16, name="mix_out_dx", deps=deps)
    g["w_out"] = _mm_tn(sv["o2"], dy, name="mix_out_dw")

    def post_bwd(ov, gv, ctv, nv):
        _, vjp = jax.vjp(functools.partial(_head_rms_gate, heads=heads), ov, gv.astype(F32), nv)
        d_o, d_g, d_n = vjp(ctv.astype(F32))
        return [d_o, d_g], [d_n]

    (d_o, d_gate), (g["norm"],) = _tile_call("hg_post_bwd", post_bwd, [sv["o"], (proj, d, 3), do2], [w["norm"]],
                                             [(d, F32), (d, BF16)], [w["norm"].shape], 256)
    dqs, dkk, di, dlg = _gla_bwd((sv["qs"], d, 0), (sv["kk"], d, 0), (proj, d, 2), (sv["lg"], d, 0), sv["states"], d_o,
                                 heads=heads, dk=HGRN_EXPAND, dv=HGRN_EXPAND, scale=1.0, chunk=HGRN_CHUNK, seq=seq,
                                 out_dtypes=(F32, F32, BF16), name="hg_core_bwd", side=sv.pop("side", None))

    def pre_bwd(qv, fv, c1, c2, c3, tb):
        _, vjp = jax.vjp(functools.partial(_hgrn_pre, layer=layer), qv.astype(F32), fv.astype(F32), tb)
        d_q, d_f, d_t = vjp((c1, c2, c3))
        return [jnp.concatenate([d_q, d_f], axis=1)], [d_t]

    (dqf,), (g["lb_table"],) = _tile_call("hg_pre_bwd", pre_bwd, [(proj, d, 0), (proj, d, 1), dqs, dkk, dlg],
                                          [w["lb_table"]], [(2 * d, BF16)], [w["lb_table"].shape], 256)
    dproj = jnp.concatenate([dqf, di, d_gate], axis=1)
    g["w_in"] = _mm_tn(h, dproj, name="hg_in_dw")
    dh = _mm(dproj, w["w_in_t"], out_dtype=BF16, name="hg_in_dx")
    return dh, g


_MIXERS = ("gla", "cv", "sg", "hg")


_BIG_KEYS = {"gla": ("w_main", "w_glr", "w_out"), "cv": ("w_in", "w_out"), "sg": ("w_in", "w_out"), "hg": ("w_in", "w_out"),
             "ffn": ("w_up_gate", "w_up_val", "w_down")}


def _local_step(x, target, w, seq, get_big, put_big, ride=lambda kind, layer, forward: None, put_early=None, deps=()):
    depth = w["norm_mix"].shape[0]
    d = x.shape[1]
    saved, big = [], {}
    for layer in range(depth):
        mixer = _MIXERS[layer % 4]
        sv = {"x_in": x, "side": ride(mixer, layer, True)}
        sv["h"] = _rms_fwd(x, w["norm_mix"][layer:layer + 1], "mix_norm", deps if layer == 0 else ())
        big[mixer, layer] = get_big(mixer, layer, sv["h"])
        wm = dict(w[mixer], **big[mixer, layer])
        if mixer == "gla":
            x = _gla_layer_fwd(x, sv["h"], wm, seq, sv)
        elif mixer == "cv":
            x = _cv_layer_fwd(x, sv["h"], wm, seq, sv)
        elif mixer == "sg":
            x = _sg_layer_fwd(x, sv["h"], wm, seq, sv)
        else:
            x = _hg_layer_fwd(x, sv["h"], wm, seq, sv, layer)
        sv["x_mid"] = x
        big["ffn", layer] = get_big("ffn", layer, x)
        sv["ffn"] = {"side": ride("ffn", layer, True)}
        wf = dict(w["ffn"][layer], norm=w["norm_ffn"][layer:layer + 1], **big["ffn", layer])
        x = _ffn_fwd(x, wf, seq, sv["ffn"])
        big["ffn", layer].pop("late", None)
        big["ffn", layer].update(sv["ffn"].pop("late_w", {}))
        saved.append(sv)

    def head(xv, tv, gv):
        y, vjp = jax.vjp(_rms, xv, gv)
        err = y - tv
        dx, dg = vjp(err * (1.0 / d))
        part = 0.5 * jnp.sum(jnp.mean(err * err, axis=-1, keepdims=True), axis=0, keepdims=True)
        return [dx], [jnp.broadcast_to(part, (1, LANES)), dg]

    (dx,), (loss, g_final) = _tile_call("loss_head", head, [x, target], [w["norm_final"]], [(d, F32)],
                                        [(1, LANES), (1, d)], 512)
    grads = {"norm_final": g_final, "norm_mix": [None] * depth, "norm_ffn": [None] * depth, "ffn": [None] * depth}
    order = ()
    for layer in reversed(range(depth)):
        mixer = _MIXERS[layer % 4]
        sv = saved[layer]
        wf = dict(w["ffn"][layer], norm=w["norm_ffn"][layer:layer + 1], **big["ffn", layer])
        sv["ffn"]["side"] = ride("ffn", layer, False)
        if put_early is not None:
            sv["ffn"]["put_early"] = functools.partial(put_early, layer)
        dx, gf = _ffn_bwd(sv["x_mid"], dx, wf, seq, sv["ffn"], order)
        order = put_big("ffn", layer, {k: gf.pop(k) for k in _BIG_KEYS["ffn"]})
        sv["side"] = ride(mixer, layer, False)
        grads["norm_ffn"][layer] = gf.pop("norm")
        grads["ffn"][layer] = gf
        wm = dict(w[mixer], **big[mixer, layer])
        if mixer == "gla":
            dh, gm = _gla_layer_bwd(sv["h"], dx, wm, seq, sv, order)
        elif mixer == "cv":
            dh, gm = _cv_layer_bwd(sv["h"], dx, wm, seq, sv, order)
        elif mixer == "sg":
            dh, gm = _sg_layer_bwd(sv["h"], dx, wm, seq, sv, order)
        else:
            dh, gm = _hg_layer_bwd(sv["h"], dx, wm, seq, sv, layer, order)
        order = put_big(mixer, layer, {k: gm.pop(k) for k in _BIG_KEYS[mixer]})
        grads[mixer] = gm
        dx, grads["norm_mix"][layer] = _rms_bwd(sv["x_in"], w["norm_mix"][layer:layer + 1], dh, dx, "mix_norm_bwd")
    return loss, dx, grads


def _prep_small(p):
    row = lambda a: a.reshape(1, -1).astype(F32)
    w = {"norm_mix": p["norm_mix"].astype(F32), "norm_ffn": p["norm_ffn"].astype(F32), "norm_final": row(p["norm_final"])}
    w["gla"] = dict(w_g2p=jnp.pad(p["gla_w_g2"][0].astype(F32), ((0, LANES - GLA_RANK), (0, 0))), b_g2=row(p["gla_b_g2"]),
                    norm=row(p["gla_norm"]))
    w["cv"] = dict(b_in=row(p["cv_b_in"]), w_dw=p["cv_w_dw"][0].astype(F32), b_dw=row(p["cv_b_dw"]), ln_g=row(p["cv_ln_g"]),
                   ln_b=row(p["cv_ln_b"]), b_out=row(p["cv_b_out"]))
    b_st = jnp.pad(p["sg_b_s"][0].astype(F32).T, ((0, 0), (0, LANES - SGU_GROUPS)))
    w["sg"] = dict(b_in=row(p["sg_b_in"]), ln_g=row(p["sg_ln_g"]), ln_b=row(p["sg_ln_b"]), w_s=p["sg_w_s"][0].astype(F32),
                   b_st=b_st, b_out=row(p["sg_b_out"]))
    w["hg"] = dict(lb_table=p["hg_lb_table"].astype(F32), norm=row(p["hg_norm"]))
    w["ffn"] = [dict(w_dw=p["ffn_w_dw"][layer].astype(F32)) for layer in range(p["ffn_w_dw"].shape[0])]
    return w


def _small_grads(g):
    gla, cv, sg, hg = g["gla"], g["cv"], g["sg"], g["hg"]
    return {
        "norm_mix": jnp.concatenate(g["norm_mix"], axis=0), "norm_ffn": jnp.concatenate(g["norm_ffn"], axis=0),
        "norm_final": g["norm_final"][0],
        "gla_w_g2": gla["w_g2p"][:GLA_RANK][None], "gla_b_g2": gla["b_g2"], "gla_norm": gla["norm"],
        "cv_b_in": cv["b_in"], "cv_w_dw": cv["w_dw"][None], "cv_b_dw": cv["b_dw"], "cv_ln_g": cv["ln_g"],
        "cv_ln_b": cv["ln_b"], "cv_b_out": cv["b_out"],
        "sg_b_in": sg["b_in"], "sg_ln_g": sg["ln_g"], "sg_ln_b": sg["ln_b"], "sg_w_s": sg["w_s"][None],
        "sg_b_s": sg["b_st"][:, :SGU_GROUPS].T[None], "sg_b_out": sg["b_out"],
        "hg_lb_table": hg["lb_table"], "hg_norm": hg["norm"],
        "ffn_w_dw": jnp.stack([f["w_dw"] for f in g["ffn"]]),
    }


def _oriented(kind, mats):
    if kind == "ffn":
        (up, up_t), (down, down_t) = mats["w_up"], mats["w_down"]
        f = down.shape[0]
        return dict(w_up=up, w_up_t_gate=up_t[:f], w_up_t_val=up_t[f:], w_down=down, w_down_t=down_t)
    (w_in, w_in_t), (w_out, w_out_t) = mats["w_in"], mats["w_out"]
    if kind != "gla":
        return dict(w_in=w_in, w_in_t=w_in_t, w_out=w_out, w_out_t=w_out_t)
    n_main = w_in.shape[1] - GLA_RANK
    return dict(w_main=w_in[:, :n_main], w_glr=jnp.pad(w_in[:, n_main:], ((0, 0), (0, LANES - GLA_RANK))),
                w_main_t=w_in_t[:n_main], w_glr_t=jnp.pad(w_in_t[n_main:], ((0, LANES - GLA_RANK), (0, 0))),
                w_out=w_out, w_out_t=w_out_t)


def _all_gather(x, *, name):
    m_per, n = x.shape

    def body(x_ref, out_ref, send_sems, recv_sems, local_sem):
        mx, my, mc = lax.axis_index("x"), lax.axis_index("y"), lax.axis_index("c")
        me, sibling = (mx, my, mc), (mx, my, 1 - mc)
        chips = [(1 - mx, my), (mx, 1 - my), (1 - mx, 1 - my)]

        def rows(px, py, pc):
            return out_ref.at[pl.ds((4 * px + 2 * py + pc) * m_per, m_per), :]

        def copy(k, block, to, src=None):
            return pltpu.make_async_remote_copy(
                src_ref=rows(*block) if src is None else src, dst_ref=rows(*block), send_sem=send_sems.at[k],
                recv_sem=recv_sems.at[k], device_id=to, device_id_type=MESH)

        mine = pltpu.make_async_copy(x_ref, rows(*me), local_sem)
        mine.start()
        first = [copy(0, me, sibling, src=x_ref)]
        first += [copy(1 + j, me, (*chip, mc), src=x_ref) for j, chip in enumerate(chips)]
        for cp in first:
            cp.start()
        passed = [copy(4 + j, (*chip, mc), sibling) for j, chip in enumerate(chips)]
        for j, chip in enumerate(chips):
            copy(1 + j, (*chip, mc), me).wait_recv()
            passed[j].start()
        copy(0, sibling, me).wait_recv()
        for j, chip in enumerate(chips):
            copy(4 + j, (*chip, 1 - mc), me).wait_recv()
        for cp in first + passed:
            cp.wait_send()
        mine.wait()

    return pl.pallas_call(
        body, name=name, out_shape=jax.ShapeDtypeStruct((N_DEV * m_per, n), x.dtype),
        in_specs=[pl.BlockSpec(memory_space=pltpu.VMEM)], out_specs=pl.BlockSpec(memory_space=pltpu.VMEM),
        scratch_shapes=[pltpu.SemaphoreType.DMA((7,)), pltpu.SemaphoreType.DMA((7,)), pltpu.SemaphoreType.DMA],
    )(x)


def _my_index():
    return 4 * lax.axis_index("x") + 2 * lax.axis_index("y") + lax.axis_index("c")


def _gather_stage(srcs, *, name):
    n = len(srcs)

    def body(*refs):
        x_refs, out_refs = refs[:n], refs[n:2 * n]
        send_sems, recv_sems, local_sems = refs[2 * n:]
        mx, my, mc = lax.axis_index("x"), lax.axis_index("y"), lax.axis_index("c")
        me, sibling = (mx, my, mc), (mx, my, 1 - mc)
        chips = [(1 - mx, my), (mx, 1 - my), (1 - mx, 1 - my)]

        def slot(i, px, py, pc):
            return out_refs[i].at[4 * px + 2 * py + pc]

        def copy(i, k, block, to, src=None):
            return pltpu.make_async_remote_copy(
                src_ref=slot(i, *block) if src is None else src, dst_ref=slot(i, *block), send_sem=send_sems.at[7 * i + k],
                recv_sem=recv_sems.at[7 * i + k], device_id=to, device_id_type=MESH)

        mine = [pltpu.make_async_copy(x_refs[i], slot(i, *me), local_sems.at[i]) for i in range(n)]
        first = [copy(i, 0, me, sibling, src=x_refs[i]) for i in range(n)]
        first += [copy(i, 1 + j, me, (*chip, mc), src=x_refs[i]) for j, chip in enumerate(chips) for i in range(n)]
        for cp in mine + first:
            cp.start()
        passed = []
        for j, chip in enumerate(chips):
            for i in range(n):
                copy(i, 1 + j, (*chip, mc), me).wait_recv()
                passed.append(copy(i, 4 + j, (*chip, mc), sibling))
                passed[-1].start()
        for i in range(n):
            copy(i, 0, sibling, me).wait_recv()
            for j, chip in enumerate(chips):
                copy(i, 4 + j, (*chip, 1 - mc), me).wait_recv()
        for cp in first + passed:
            cp.wait_send()
        for cp in mine:
            cp.wait()

    any_space = pl.BlockSpec(memory_space=pl.ANY)
    return pl.pallas_call(
        body, name=name, out_shape=[jax.ShapeDtypeStruct((N_DEV,) + s.shape, s.dtype) for s in srcs],
        in_specs=[any_space] * n, out_specs=[any_space] * n,
        scratch_shapes=[pltpu.SemaphoreType.DMA((7 * n,)), pltpu.SemaphoreType.DMA((7 * n,)), pltpu.SemaphoreType.DMA((n,))],
    )(*srcs)


def _scatter_stage(srcs, *, name):
    n = len(srcs)

    def body(*refs):
        x_refs, out_refs = refs[:n], refs[n:2 * n]
        send_sems, recv_sems, local_sems = refs[2 * n:]
        mx, my, mc = lax.axis_index("x"), lax.axis_index("y"), lax.axis_index("c")
        me = 4 * mx + 2 * my + mc
        mine = [pltpu.make_async_copy(x_refs[i].at[me], out_refs[i].at[me], local_sems.at[i]) for i in range(n)]
        for cp in mine:
            cp.start()
        sends, recvs = [], []
        for k in range(1, N_DEV):
            px = 1 - mx if k & 4 else mx
            py = 1 - my if k & 2 else my
            pc = 1 - mc if k & 1 else mc
            peer = 4 * px + 2 * py + pc
            for i in range(n):
                sems = dict(send_sem=send_sems.at[7 * i + k - 1], recv_sem=recv_sems.at[7 * i + k - 1],
                            device_id=(px, py, pc), device_id_type=MESH)
                sends.append(pltpu.make_async_remote_copy(src_ref=x_refs[i].at[peer], dst_ref=out_refs[i].at[me], **sems))
                recvs.append(pltpu.make_async_remote_copy(src_ref=x_refs[i].at[me], dst_ref=out_refs[i].at[peer], **sems))
                sends[-1].start()
        for cp in recvs:
            cp.wait_recv()
        for cp in sends:
            cp.wait_send()
        for cp in mine:
            cp.wait()

    any_space = pl.BlockSpec(memory_space=pl.ANY)
    return pl.pallas_call(
        body, name=name, out_shape=[jax.ShapeDtypeStruct(s.shape, s.dtype) for s in srcs],
        in_specs=[any_space] * n, out_specs=[any_space] * n,
        scratch_shapes=[pltpu.SemaphoreType.DMA((7 * n,)), pltpu.SemaphoreType.DMA((7 * n,)), pltpu.SemaphoreType.DMA((n,))],
    )(*srcs)


def _adamw_math(g, w, m, v):
    c1, c2 = 1.0 - ADAM_B1 ** ADAM_STEP, 1.0 - ADAM_B2 ** ADAM_STEP
    m_new = ADAM_B1 * m + (1.0 - ADAM_B1) * g
    v_new = ADAM_B2 * v + (1.0 - ADAM_B2) * (g * g)
    delta = -ADAM_LR * ((m_new / c1) / (jnp.sqrt(v_new / c2) + ADAM_EPS) + ADAM_WD * w)
    return delta, m_new, v_new


def _adamw_big(slots, w, m, v, layer, *, name):
    _, r, c = slots.shape
    tr = _divisor_tile(r, max(8, (200 * 1024) // c // 8 * 8), 8)

    def body(s_ref, w_ref, m_ref, v_ref, g_out, d_out, m_out, v_out):
        g = s_ref[0].astype(F32)
        for p in range(1, N_DEV):
            g = g + s_ref[p].astype(F32)
        g_out[...] = g
        d_out[...], m_out[...], v_out[...] = _adamw_math(g, w_ref[...], m_ref[...], v_ref[...])

    blk = pl.BlockSpec((tr, c), lambda i: (i, 0))
    lay = pl.BlockSpec((None, tr, c), lambda i: (layer, i, 0))
    return pl.pallas_call(
        body, name=name, grid=(r // tr,), in_specs=[pl.BlockSpec((N_DEV, tr, c), lambda i: (0, i, 0)), lay, lay, lay],
        out_specs=[blk] * 4, out_shape=[jax.ShapeDtypeStruct((r, c), F32)] * 4,
        compiler_params=pltpu.CompilerParams(dimension_semantics=("parallel",)),
    )(slots, w, m, v)


def _sum_small(got, r_re, r_sh, *, name):
    per_dev = r_re + N_DEV * r_sh

    def body(got_ref, re_ref, sh_ref):
        mine = r_re + _my_index() * r_sh
        acc_re = got_ref[0:r_re, :]
        acc_sh = got_ref[pl.ds(pl.multiple_of(mine, 8), r_sh), :]
        for p in range(1, N_DEV):
            acc_re = acc_re + got_ref[p * per_dev:p * per_dev + r_re, :]
            acc_sh = acc_sh + got_ref[pl.ds(pl.multiple_of(p * per_dev + mine, 8), r_sh), :]
        re_ref[...] = acc_re
        sh_ref[...] = acc_sh

    return pl.pallas_call(body, name=name, out_shape=[jax.ShapeDtypeStruct((r_re, LANES), F32),
                                                       jax.ShapeDtypeStruct((r_sh, LANES), F32)])(got)


def _adamw_small(gs, ws, ms, vs, *, name):
    n = len(gs)

    def body(*refs):
        ins, outs = refs[:4 * n], refs[4 * n:]
        for i in range(n):
            res = _adamw_math(ins[i][...], ins[n + i][...], ins[2 * n + i][...], ins[3 * n + i][...])
            for j in range(3):
                outs[j * n + i][...] = res[j]

    out = pl.pallas_call(body, name=name, out_shape=[jax.ShapeDtypeStruct(a.shape, F32) for a in ws] * 3)(*gs, *ws, *ms, *vs)
    return out[:n], out[n:2 * n], out[2 * n:]


def _layout(shapes, row_align, total_align):
    lay, off = {}, 0
    for name, shape in shapes.items():
        size = int(np.prod(shape))
        rows = -(-size // LANES)
        rows = -(-rows // row_align) * row_align
        lay[name] = (off, rows, size, tuple(shape))
        off += rows
    return lay, -(-off // total_align) * total_align


def _pack(arrs, lay, total, dtype, lead=()):
    parts = []
    nl = len(lead)
    for name, (off, rows, size, shape) in lay.items():
        flat = arrs[name].astype(dtype).reshape(*lead, size)
        parts.append(jnp.pad(flat, [(0, 0)] * nl + [(0, rows * LANES - size)]).reshape(*lead, rows, LANES))
    used = sum(v[1] for v in lay.values())
    if total > used:
        parts.append(jnp.zeros((*lead, total - used, LANES), dtype))
    return jnp.concatenate(parts, axis=nl)


def _unpack(buf, lay, lead=()):
    out = {}
    nl = len(lead)
    for name, (off, rows, size, shape) in lay.items():
        part = lax.slice_in_dim(buf, off, off + rows, axis=nl).reshape(*lead, rows * LANES)
        out[name] = lax.slice_in_dim(part, 0, size, axis=nl).reshape(*lead, *shape)
    return out


_SHARD_AXIS = {
    "norm_mix": None, "norm_ffn": None, "norm_final": None, "gla_w_in": 2, "gla_w_g2": 2, "gla_b_g2": None,
    "gla_norm": None, "gla_w_out": 1, "cv_w_in": 2, "cv_b_in": 1, "cv_w_dw": 2, "cv_b_dw": 1, "cv_ln_g": 1,
    "cv_ln_b": 1, "cv_w_out": 1, "cv_b_out": 1, "sg_w_in": 2, "sg_b_in": 1, "sg_ln_g": 1, "sg_ln_b": 1, "sg_w_s": None,
    "sg_b_s": None, "sg_w_out": 1, "sg_b_out": 1, "hg_w_in": 2, "hg_lb_table": None, "hg_norm": None, "hg_w_out": 1,
    "ffn_w_up": 2, "ffn_w_dw": 2, "ffn_w_down": 1,
}
_MATMUL_WEIGHTS = ("gla_w_in", "gla_w_out", "cv_w_in", "cv_w_out", "sg_w_in", "sg_w_out", "hg_w_in", "hg_w_out",
                   "ffn_w_up", "ffn_w_down")
_NAMES = tuple(_SHARD_AXIS)


def kernel(x, norm_mix, norm_ffn, norm_final, gla_w_in, gla_w_g2, gla_b_g2, gla_norm, gla_w_out, cv_w_in, cv_b_in, cv_w_dw, cv_b_dw, cv_ln_g, cv_ln_b, cv_w_out, cv_b_out, sg_w_in, sg_b_in, sg_ln_g, sg_ln_b, sg_w_s, sg_b_s, sg_w_out, sg_b_out, hg_w_in, hg_lb_table, hg_norm, hg_w_out, ffn_w_up, ffn_w_dw, ffn_w_down, loss_target, m_norm_mix, m_norm_ffn, m_norm_final, m_gla_w_in, m_gla_w_g2, m_gla_b_g2, m_gla_norm, m_gla_w_out, m_cv_w_in, m_cv_b_in, m_cv_w_dw, m_cv_b_dw, m_cv_ln_g, m_cv_ln_b, m_cv_w_out, m_cv_b_out, m_sg_w_in, m_sg_b_in, m_sg_ln_g, m_sg_ln_b, m_sg_w_s, m_sg_b_s, m_sg_w_out, m_sg_b_out, m_hg_w_in, m_hg_lb_table, m_hg_norm, m_hg_w_out, m_ffn_w_up, m_ffn_w_dw, m_ffn_w_down, v_norm_mix, v_norm_ffn, v_norm_final, v_gla_w_in, v_gla_w_g2, v_gla_b_g2, v_gla_norm, v_gla_w_out, v_cv_w_in, v_cv_b_in, v_cv_w_dw, v_cv_b_dw, v_cv_ln_g, v_cv_ln_b, v_cv_w_out, v_cv_b_out, v_sg_w_in, v_sg_b_in, v_sg_ln_g, v_sg_ln_b, v_sg_w_s, v_sg_b_s, v_sg_w_out, v_sg_b_out, v_hg_w_in, v_hg_lb_table, v_hg_norm, v_hg_w_out, v_ffn_w_up, v_ffn_w_dw, v_ffn_w_down):
    local = dict(locals())
    wts = {n: local[n] for n in _NAMES}
    mom = {n: local["m_" + n] for n in _NAMES}
    var = {n: local["v_" + n] for n in _NAMES}
    small_all = [n for n in _NAMES if n not in _MATMUL_WEIGHTS]
    small_sharded = [n for n in small_all if _SHARD_AXIS[n] is not None]
    bsz, seq, d = x.shape
    depth = norm_mix.shape[0]

    stages = {}
    for layer in range(depth):
        kind = _MIXERS[layer % 4]
        stages[kind, layer] = {"w_in": (kind + "_w_in", layer // 4), "w_out": (kind + "_w_out", layer // 4)}
        stages["ffn", layer] = {"w_up": ("ffn_w_up", layer), "w_down": ("ffn_w_down", layer)}

    order = list(stages)
    shards = lambda stage: [wts[nm][idx].astype(BF16) for nm, idx in stages[stage].values()]
    gathers = {order[0]: _Side(shards(order[0]), False)}
    gathers[order[0]].lands = _gather_stage(gathers[order[0]].srcs, name="gather_first")
    scatters, waiting, down_gathers, down_scatters = {}, [], {}, {}

    def ride(kind, layer, forward):
        if not forward:
            return waiting.pop() if waiting else None
        at = order.index((kind, layer)) + 1
        if at == len(order):
            return None
        srcs = shards(order[at])
        if order[at][0] == "ffn":
            down_gathers[order[at][1]] = _Side(srcs[1:], False)
            srcs = srcs[:1]
        gathers[order[at]] = _Side(srcs, False)
        return gathers[order[at]]

    lay_sw, r_sw = _layout({n: wts[n].shape for n in small_sharded}, 8, 8)
    got_sw = _all_gather(_pack(wts, lay_sw, r_sw, F32), name="gather_small_weights")
    parts = _unpack(got_sw.reshape(N_DEV, r_sw, LANES), lay_sw, (N_DEV,))
    full_small = {n: wts[n] for n in small_all if _SHARD_AXIS[n] is None}
    for n in small_sharded:
        ax, shape = _SHARD_AXIS[n], wts[n].shape
        full_small[n] = jnp.moveaxis(parts[n], 0, ax).reshape(shape[:ax] + (N_DEV * shape[ax],) + shape[ax + 1:])

    def full_size(nm, land):
        _, r, c = land.shape
        if _SHARD_AXIS[nm] == 2:
            return land.transpose(1, 0, 2).reshape(r, N_DEV * c), land.transpose(0, 2, 1).reshape(N_DEV * c, r)
        return land.reshape(N_DEV * r, c), land.reshape(N_DEV * r, c).T

    def get_big(kind, layer, after):
        names = [nm for nm, _ in stages[kind, layer].values()]
        lands = gathers[kind, layer].lands
        if kind != "ffn" or layer not in down_gathers:
            return _oriented(kind, {key: full_size(nm, land) for key, nm, land in zip(stages[kind, layer], names, lands)})
        up, up_t = full_size(names[0], lands[0])
        f = up.shape[1] // 2

        def down_landed():
            down, down_t = full_size(names[1], down_gathers[layer].lands[0])
            return dict(w_down=down, w_down_t=down_t)

        return dict(w_up=up, w_up_t_gate=up_t[:f], w_up_t_val=up_t[f:], late=(down_gathers[layer], down_landed))

    def put_big(kind, layer, g):
        if kind == "ffn":
            k, f = g["w_up_gate"].shape
            halves = [g[key].reshape(k, N_DEV // 2, 2 * f // N_DEV) for key in ("w_up_gate", "w_up_val")]
            w_in = jnp.concatenate(halves, axis=1)
        else:
            w_in = jnp.concatenate([g["w_main"], g["w_glr"][:, :GLA_RANK]], axis=1) if kind == "gla" else g["w_in"]
            w_in = w_in.reshape(w_in.shape[0], N_DEV, w_in.shape[1] // N_DEV)
        sends = [w_in.transpose(1, 0, 2).astype(BF16)]
        if kind != "ffn":
            sends.append(row_slots(g["w_out"]))
        scatters[kind, layer] = _Side(sends, True)
        waiting.append(scatters[kind, layer])
        return ()

    def row_slots(grad):
        return grad.reshape(N_DEV, grad.shape[0] // N_DEV, grad.shape[1]).astype(BF16)

    def put_early(layer, grad_w_down):
        down_scatters[layer] = _Side([row_slots(grad_w_down)], True)
        return down_scatters[layer]

    loss, dx, grads = _local_step(x.reshape(bsz * seq, d), loss_target.reshape(bsz * seq, d), _prep_small(full_small), seq,
                                  get_big, put_big, ride, put_early)
    loss = lax.psum(loss[0, 0], ("x", "y", "c"))

    gs = _small_grads(grads)
    small_repl = [n for n in small_all if _SHARD_AXIS[n] is None]
    lay_re, r_re = _layout({n: wts[n].shape for n in small_repl}, 8, 8)
    slots = {}
    for n in small_sharded:
        ax, shape = _SHARD_AXIS[n], wts[n].shape
        slots[n] = jnp.moveaxis(gs[n].reshape(shape[:ax] + (N_DEV, shape[ax]) + shape[ax + 1:]), ax, 0)
    sent = jnp.concatenate([_pack(gs, lay_re, r_re, F32), _pack(slots, lay_sw, r_sw, F32, (N_DEV,)).reshape(-1, LANES)])
    sum_re, sum_sh = _sum_small(_all_gather(sent, name="gather_small_grads"), r_re, r_sw, name="sum_small_grads")
    g_own = _unpack(sum_re, lay_re)
    g_own.update(_unpack(sum_sh, lay_sw))
    two_d = lambda a: a.reshape(-1, a.shape[-1])
    upd = _adamw_small(*[[two_d(src[n]) for n in small_all] for src in (g_own, wts, mom, var)], name="adamw_small")
    results = {n: [g_own[n]] + [part[i].reshape(wts[n].shape) for part in upd] for i, n in enumerate(small_all)}

    per_layer = {}
    for (kind, layer), side in scatters.items():
        lands = side.lands if side.lands is not None else _scatter_stage(side.srcs, name="scatter_last")
        if kind == "ffn":
            lands = list(lands) + down_scatters[layer].lands
        for (nm, idx), land in zip(stages[kind, layer].values(), lands):
            three_d = lambda a: a.reshape((a.shape[0],) + land.shape[1:])
            per_layer.setdefault(nm, {})[idx] = _adamw_big(land, three_d(wts[nm]), three_d(mom[nm]), three_d(var[nm]), idx,
                                                           name="adamw_" + nm)
    for nm, by_idx in per_layer.items():
        outs = [by_idx[i] for i in range(len(by_idx))]
        results[nm] = [(outs[0][j] if len(outs) == 1 else jnp.stack([o[j] for o in outs])).reshape(wts[nm].shape)
                       for j in range(4)]
    out = [loss, dx.reshape(bsz, seq, d)]
    for j in range(4):
        out += [results[n][j] for n in _NAMES]
    return tuple(out)
```

```python
import functools
import math

import jax
import jax.numpy as jnp
import numpy as np
from jax import lax
from jax.experimental import pallas as pl
from jax.experimental.pallas import tpu as pltpu

F32 = jnp.float32
BF16 = jnp.bfloat16
EPS = 1e-6
N_DEV = 8
LANES = 128
SUBLANES_BF16 = 16
HALO = 32
GLA_HEADS, GLA_RANK, GLA_GATE_NORM, GLA_CHUNK = 4, 16, 16.0, 64
SGU_CHUNK, SGU_GROUPS = 128, 8
HGRN_EXPAND, HGRN_CHUNK = 128, 64
CONV_WIDTH, FFN_CONV_WIDTH = 31, 3
ADAM_LR, ADAM_B1, ADAM_B2, ADAM_EPS, ADAM_WD, ADAM_STEP = 0.001, 0.9, 0.999, 1e-08, 0.01, 10
MESH = pl.DeviceIdType.MESH


def _sigmoid(x):
    return 0.5 * (jnp.tanh(0.5 * x) + 1.0)


def _silu(x):
    return x * _sigmoid(x)


def _log_sigmoid(x):
    return jnp.minimum(x, 0.0) - jnp.log(1.0 + jnp.exp(-jnp.abs(x)))


def _gelu(x):
    return 0.5 * x * (1.0 + jnp.tanh(math.sqrt(2.0 / math.pi) * (x + 0.044715 * (x * x * x))))


def _rms(x, g):
    return x * lax.rsqrt(jnp.mean(x * x, axis=-1, keepdims=True) + EPS) * g


def _layer_norm(x, g, b):
    xc = x - jnp.mean(x, axis=-1, keepdims=True)
    return xc * lax.rsqrt(jnp.mean(xc * xc, axis=-1, keepdims=True) + EPS) * g + b


def _dot_raw(a, b, dims):
    return lax.dot_general(a.astype(BF16), b.astype(BF16), (dims, ((), ())), preferred_element_type=F32)


@jax.custom_vjp
def _bdot(a, b):
    return _dot_raw(a, b, ((1,), (0,)))


@jax.custom_vjp
def _bdot_nt(a, b):
    return _dot_raw(a, b, ((1,), (1,)))


@jax.custom_vjp
def _bdot_tn(a, b):
    return _dot_raw(a, b, ((0,), (0,)))


_bdot.defvjp(lambda a, b: (_bdot(a, b), (a, b)), lambda r, g: (_bdot_nt(g, r[1]), _bdot_tn(r[0], g)))
_bdot_nt.defvjp(lambda a, b: (_bdot_nt(a, b), (a, b)), lambda r, g: (_bdot(g, r[1]), _bdot_tn(g, r[0])))
_bdot_tn.defvjp(lambda a, b: (_bdot_tn(a, b), (a, b)), lambda r, g: (_bdot_nt(r[1], g), _bdot(r[0], g)))


def _scan_rows(x, reverse):
    n = x.shape[0]
    row = lax.broadcasted_iota(jnp.int32, x.shape, 0)
    step = 1
    while step < n:
        if reverse:
            x = x + jnp.where(row < n - step, pltpu.roll(x, n - step, 0), 0.0)
        else:
            x = x + jnp.where(row >= step, pltpu.roll(x, step, 0), 0.0)
        step *= 2
    return x


@jax.custom_vjp
def _cumsum_rows(x):
    return _scan_rows(x, False)


_cumsum_rows.defvjp(lambda x: (_scan_rows(x, False), None), lambda _, g: (_scan_rows(g, True),))


def _divisor_tile(n, cap, unit):
    if n <= cap:
        return n
    best = None
    for t in range(unit, cap + 1, unit):
        if n % t == 0:
            best = t
    assert best is not None, (n, cap, unit)
    return best


def _const_map(nd):
    return lambda *_: (0,) * nd


class _Side:
    def __init__(self, srcs, scatter, parts=()):
        self.srcs, self.scatter, self.lands, self.parts = list(srcs), scatter, None, list(parts)

    @staticmethod
    def join(sides):
        sides = [s for s in sides if s is not None]
        if len(sides) < 2:
            return sides[0] if sides else None
        assert len({s.scatter for s in sides}) == 1
        return _Side([a for s in sides for a in s.srcs], sides[0].scatter, sides)

    def landed(self, lands):
        self.lands = list(lands)
        at = 0
        for part in self.parts:
            part.landed(self.lands[at:at + len(part.srcs)])
            at += len(part.srcs)


def _pallas(body, side, *, name, grid, in_specs, out_specs, out_shape, scratch_shapes=(), semantics):
    if side is None:
        return pl.pallas_call(body, name=name, grid=grid, in_specs=in_specs, out_specs=out_specs, out_shape=out_shape,
                              scratch_shapes=list(scratch_shapes),
                              compiler_params=pltpu.CompilerParams(dimension_semantics=semantics))
    single = not isinstance(out_shape, (list, tuple))
    out_specs, out_shape = ([out_specs], [out_shape]) if single else (list(out_specs), list(out_shape))
    n, n_in, n_out, n_scr = len(side.srcs), len(in_specs), len(out_shape), len(scratch_shapes)
    lands = [jax.ShapeDtypeStruct((N_DEV,) + (s.shape[1:] if side.scatter else s.shape), s.dtype) for s in side.srcs]

    def body2(*refs):
        x_refs, land_refs = refs[n_in:n_in + n], refs[n_in + n + n_out:n_in + 2 * n + n_out]
        send_sems, recv_sems, local_sems = refs[-3:]
        steps = [pl.program_id(a) for a in range(len(grid))]
        first = functools.reduce(jnp.logical_and, [s == 0 for s in steps])
        last = functools.reduce(jnp.logical_and, [s == g - 1 for s, g in zip(steps, grid)])

        def copies():
            mx, my, mc = lax.axis_index("x"), lax.axis_index("y"), lax.axis_index("c")
            me = 4 * mx + 2 * my + mc
            mine = [pltpu.make_async_copy(x_refs[i].at[me] if side.scatter else x_refs[i], land_refs[i].at[me],
                                          local_sems.at[i]) for i in range(n)]
            sends, recvs = [], []
            for k in range(1, N_DEV):
                px = 1 - mx if k & 4 else mx
                py = 1 - my if k & 2 else my
                pc = 1 - mc if k & 1 else mc
                peer = 4 * px + 2 * py + pc
                for i in range(n):
                    sems = dict(send_sem=send_sems.at[7 * i + k - 1], recv_sem=recv_sems.at[7 * i + k - 1],
                                device_id=(px, py, pc), device_id_type=MESH)
                    src = x_refs[i].at[peer] if side.scatter else x_refs[i]
                    sends.append(pltpu.make_async_remote_copy(src_ref=src, dst_ref=land_refs[i].at[me], **sems))
                    recvs.append(pltpu.make_async_remote_copy(src_ref=src, dst_ref=land_refs[i].at[peer], **sems))
            return mine, sends, recvs

        @pl.when(first)
        def _():
            mine, sends, _ = copies()
            for cp in mine + sends:
                cp.start()

        body(*refs[:n_in], *refs[n_in + n:n_in + n + n_out], *refs[n_in + 2 * n + n_out:n_in + 2 * n + n_out + n_scr])

        @pl.when(last)
        def _():
            mine, sends, recvs = copies()
            for cp in recvs:
                cp.wait_recv()
            for cp in sends:
                cp.wait_send()
            for cp in mine:
                cp.wait()

    any_space = pl.BlockSpec(memory_space=pl.ANY)
    call = pl.pallas_call(
        body2, name=name, grid=grid, in_specs=list(in_specs) + [any_space] * n, out_specs=out_specs + [any_space] * n,
        out_shape=out_shape + lands,
        scratch_shapes=list(scratch_shapes) + [pltpu.SemaphoreType.DMA((7 * n,)), pltpu.SemaphoreType.DMA((7 * n,)),
                                               pltpu.SemaphoreType.DMA((n,))],
        compiler_params=pltpu.CompilerParams(dimension_semantics=("arbitrary",) * len(grid)))

    def run(*args):
        res = call(*args, *side.srcs)
        side.landed(res[n_out:])
        return res[0] if single else res[:n_out]

    return run


def _mm(a, b, *, add=None, bias=None, out_dtype=F32, name, side=None):
    m, k = a.shape
    k2, n = b.shape
    assert k == k2
    tn = _divisor_tile(n, max(LANES, min(1408, (6 << 20) // (2 * k) // LANES * LANES)), LANES)
    tm = _divisor_tile(m, max(256, min(1024, (4 << 20) // (a.dtype.itemsize * k) // 256 * 256)), 8)
    has_bias, has_add = bias is not None, add is not None

    def body(*refs):
        a_ref, b_ref = refs[0], refs[1]
        o_ref = refs[-1]
        acc = jnp.dot(a_ref[...].astype(BF16), b_ref[...], preferred_element_type=F32)
        pos = 2
        if has_bias:
            acc = acc + refs[pos][...]
            pos += 1
        if has_add:
            acc = acc + refs[pos][...].astype(F32)
        o_ref[...] = acc.astype(o_ref.dtype)

    in_specs = [pl.BlockSpec((tm, k), lambda i, j: (i, 0)), pl.BlockSpec((k, tn), lambda i, j: (0, j))]
    args = [a, b]
    if has_bias:
        in_specs.append(pl.BlockSpec((1, tn), lambda i, j: (0, j)))
        args.append(bias)
    if has_add:
        in_specs.append(pl.BlockSpec((tm, tn), lambda i, j: (i, j)))
        args.append(add)
    return _pallas(
        body, side, name=name, grid=(m // tm, n // tn), in_specs=in_specs,
        out_specs=pl.BlockSpec((tm, tn), lambda i, j: (i, j)),
        out_shape=jax.ShapeDtypeStruct((m, n), out_dtype),
        semantics=("parallel", "parallel"),
    )(*args)


def _mm_tn(a, g, *, name):
    m, k = a.shape
    m2, n = g.shape
    assert m == m2
    tk = _divisor_tile(k, 1408, LANES)
    tn = _divisor_tile(n, 1408, LANES)
    tm = _divisor_tile(m, 1024, 8)

    def body(a_ref, g_ref, o_ref):
        @pl.when(pl.program_id(2) == 0)
        def _():
            o_ref[...] = jnp.zeros_like(o_ref)

        o_ref[...] += _dot_raw(a_ref[...], g_ref[...], ((0,), (0,)))

    return pl.pallas_call(
        body, name=name, grid=(k // tk, n // tn, m // tm),
        in_specs=[pl.BlockSpec((tm, tk), lambda i, j, t: (t, i)), pl.BlockSpec((tm, tn), lambda i, j, t: (t, j))],
        out_specs=pl.BlockSpec((tk, tn), lambda i, j, t: (i, j)),
        out_shape=jax.ShapeDtypeStruct((k, n), F32),
        compiler_params=pltpu.CompilerParams(dimension_semantics=("parallel", "parallel", "arbitrary")),
    )(a, g)


def _tile_call(name, fn, tiled, params, out_tiled, out_acc, tile, side=None):
    tiled = [t if isinstance(t, tuple) else (t, t.shape[1], 0) for t in tiled]
    t_rows = tiled[0][0].shape[0]
    tile = min(tile, t_rows)
    assert t_rows % tile == 0
    n_t, n_p, n_o = len(tiled), len(params), len(out_tiled)

    def body(*refs):
        vals = [r[...] for r in refs[: n_t + n_p]]
        touts, aouts = fn(*vals)
        for r, v in zip(refs[n_t + n_p: n_t + n_p + n_o], touts):
            r[...] = v.astype(r.dtype)
        acc_refs = refs[n_t + n_p + n_o:]
        if acc_refs:
            @pl.when(pl.program_id(0) == 0)
            def _():
                for r in acc_refs:
                    r[...] = jnp.zeros_like(r)

            for r, v in zip(acc_refs, aouts):
                r[...] += v

    in_specs = [pl.BlockSpec((tile, w), lambda i, cb=cb: (i, cb)) for _, w, cb in tiled]
    in_specs += [pl.BlockSpec(p.shape, _const_map(p.ndim)) for p in params]
    out_specs = [pl.BlockSpec((tile, w), lambda i: (i, 0)) for w, _ in out_tiled]
    out_specs += [pl.BlockSpec(s, _const_map(len(s))) for s in out_acc]
    out_shape = [jax.ShapeDtypeStruct((t_rows, w), dt) for w, dt in out_tiled]
    out_shape += [jax.ShapeDtypeStruct(s, F32) for s in out_acc]
    res = _pallas(
        body, side, name=name, grid=(t_rows // tile,), in_specs=in_specs, out_specs=out_specs, out_shape=out_shape,
        semantics=("arbitrary" if out_acc else "parallel",),
    )(*[t[0] for t in tiled], *params)
    return res[:n_o], res[n_o:]


def _rms_fwd(x, g, name):
    (h,), _ = _tile_call(name, lambda xv, gv: ([_rms(xv, gv)], []), [x], [g], [(x.shape[1], BF16)], [], 512)
    return h


def _rms_bwd(x, g, dh, dres, name):
    def fn(xv, dhv, drv, gv):
        _, vjp = jax.vjp(_rms, xv, gv)
        dx, dg = vjp(dhv.astype(F32))
        return [drv + dx], [dg]

    (dx,), (dg,) = _tile_call(name, fn, [x, dh, dres], [g], [(x.shape[1], F32)], [g.shape], 512)
    return dx, dg


def _colsum(x, name):
    _, (s,) = _tile_call(name, lambda xv: ([], [jnp.sum(xv.astype(F32), axis=0, keepdims=True)]), [x], [], [],
                         [(1, x.shape[1])], 512)
    return s


def _seq_flags(i, tiles_per_seq):
    pos = i % tiles_per_seq
    return pos == 0, pos == tiles_per_seq - 1


def _dwconv_fwd(x, w, b, seq, name, side=None):
    t_rows, ch = x.shape
    kw = w.shape[0]
    tile = min(512, seq)
    cb = _divisor_tile(ch, 256, LANES)
    tps, hb = seq // tile, tile // HALO

    def body(x_ref, halo_ref, w_ref, b_ref, y_ref, pad_ref):
        first, _ = _seq_flags(pl.program_id(0), tps)
        pad_ref[0:HALO, :] = jnp.where(first, 0.0, halo_ref[...])
        pad_ref[HALO:HALO + tile, :] = x_ref[...]
        for r0 in range(0, tile, HALO):
            acc = jnp.broadcast_to(b_ref[...], (HALO, cb))
            for k in range(kw):
                acc = acc + pad_ref[pl.ds(HALO - (kw - 1) + k + r0, HALO), :] * w_ref[k:k + 1, :]
            y_ref[pl.ds(r0, HALO), :] = acc

    return _pallas(
        body, side, name=name, grid=(t_rows // tile, ch // cb),
        in_specs=[pl.BlockSpec((tile, cb), lambda i, j: (i, j)),
                  pl.BlockSpec((HALO, cb), lambda i, j: (jnp.maximum(i * hb - 1, 0), j)),
                  pl.BlockSpec((kw, cb), lambda i, j: (0, j)), pl.BlockSpec((1, cb), lambda i, j: (0, j))],
        out_specs=pl.BlockSpec((tile, cb), lambda i, j: (i, j)),
        out_shape=jax.ShapeDtypeStruct((t_rows, ch), F32),
        scratch_shapes=[pltpu.VMEM((HALO + tile, cb), F32)],
        semantics=("parallel", "parallel"),
    )(x, x, w, b)


def _dwconv_bwd(x, dy, w, seq, name, side=None):
    t_rows, ch = x.shape
    kw = w.shape[0]
    tile = min(512, seq)
    cb = _divisor_tile(ch, 256, LANES)
    tps, hb, n_hb = seq // tile, tile // HALO, t_rows // HALO

    def body(x_ref, xh_ref, dy_ref, dyh_ref, w_ref, dx_ref, dw_ref, db_ref, xpad, dypad, sums):
        i = pl.program_id(1)
        first, last = _seq_flags(i, tps)

        @pl.when(i == 0)
        def _():
            sums[...] = jnp.zeros_like(sums)

        xpad[0:HALO, :] = jnp.where(first, 0.0, xh_ref[...])
        xpad[HALO:HALO + tile, :] = x_ref[...]
        dypad[0:tile, :] = dy_ref[...]
        dypad[tile:tile + HALO, :] = jnp.where(last, 0.0, dyh_ref[...])
        fold = lambda v: functools.reduce(jnp.add, [v[r:r + 8] for r in range(0, HALO, 8)])
        for r0 in range(0, tile, HALO):
            dyc = dy_ref[pl.ds(r0, HALO), :]
            acc = jnp.zeros((HALO, cb), F32)
            for k in range(kw):
                acc = acc + dypad[pl.ds(kw - 1 - k + r0, HALO), :] * w_ref[k:k + 1, :]
                sums[8 * k:8 * k + 8, :] += fold(dyc * xpad[pl.ds(HALO - (kw - 1) + k + r0, HALO), :])
            dx_ref[pl.ds(r0, HALO), :] = acc
            sums[8 * kw:8 * kw + 8, :] += fold(dyc)

        @pl.when(i == t_rows // tile - 1)
        def _():
            for k in range(kw):
                dw_ref[k:k + 1, :] = jnp.sum(sums[8 * k:8 * k + 8, :], axis=0, keepdims=True)
            db_ref[...] = jnp.sum(sums[8 * kw:8 * kw + 8, :], axis=0, keepdims=True)

    return _pallas(
        body, side, name=name, grid=(ch // cb, t_rows // tile),
        in_specs=[pl.BlockSpec((tile, cb), lambda j, i: (i, j)),
                  pl.BlockSpec((HALO, cb), lambda j, i: (jnp.maximum(i * hb - 1, 0), j)),
                  pl.BlockSpec((tile, cb), lambda j, i: (i, j)),
                  pl.BlockSpec((HALO, cb), lambda j, i: (jnp.minimum((i + 1) * hb, n_hb - 1), j)),
                  pl.BlockSpec((kw, cb), lambda j, i: (0, j))],
        out_specs=[pl.BlockSpec((tile, cb), lambda j, i: (i, j)), pl.BlockSpec((kw, cb), lambda j, i: (0, j)),
                   pl.BlockSpec((1, cb), lambda j, i: (0, j))],
        out_shape=[jax.ShapeDtypeStruct((t_rows, ch), F32), jax.ShapeDtypeStruct((kw, ch), F32),
                   jax.ShapeDtypeStruct((1, ch), F32)],
        scratch_shapes=[pltpu.VMEM((HALO + tile, cb), F32), pltpu.VMEM((tile + HALO, cb), F32),
                        pltpu.VMEM((8 * (kw + 1), cb), F32)],
        semantics=("parallel", "arbitrary"),
    )(x, x, dy, dy, w)


_ROWS = SUBLANES_BF16


def _lane_chunks(width, cap=6 * LANES):
    return [slice(c0, min(c0 + cap, width)) for c0 in range(0, width, cap)]


def _tap_rows(w_ref, cols):
    return [w_ref[k:k + 1, cols] for k in range(FFN_CONV_WIDTH)]


def _conv3_at(pad, taps, row, cols):
    z = pad[pl.ds(row, _ROWS), cols] * taps[2]
    z = z + pad[pl.ds(row - 1, _ROWS), cols] * taps[1]
    return z + pad[pl.ds(row - 2, _ROWS), cols] * taps[0]


def _ffn_mid_fwd(u, w, seq, name, side=None):
    t_rows, f2 = u.shape
    f = f2 // 2
    tile = min(256, seq)
    cb = _divisor_tile(f, 1408, LANES)
    nj, tps, hb, hl = f // cb, seq // tile, tile // SUBLANES_BF16, SUBLANES_BF16

    def body(ug_ref, uv_ref, hg_ref, hv_ref, wg_ref, wv_ref, a_ref, gpad, vpad):
        first, _ = _seq_flags(pl.program_id(0), tps)
        for t_ref, h_ref, pad in ((ug_ref, hg_ref, gpad), (uv_ref, hv_ref, vpad)):
            pad[0:hl, :] = jnp.where(first, 0.0, h_ref[...].astype(F32))
            pad[hl:hl + tile, :] = t_ref[...].astype(F32)
        for cols in _lane_chunks(cb):
            wg, wv = _tap_rows(wg_ref, cols), _tap_rows(wv_ref, cols)
            for r0 in range(0, tile, _ROWS):
                zg = _conv3_at(gpad, wg, hl + r0, cols)
                zv = _conv3_at(vpad, wv, hl + r0, cols)
                half = 0.5 * zg
                a_ref[pl.ds(r0, _ROWS), cols] = ((jnp.tanh(half) + 1.0) * half * zv).astype(a_ref.dtype)

    halo_map = lambda off: (lambda i, j: (jnp.maximum(i * hb - 1, 0), j + off))
    return _pallas(
        body, side, name=name, grid=(t_rows // tile, nj),
        in_specs=[pl.BlockSpec((tile, cb), lambda i, j: (i, j)), pl.BlockSpec((tile, cb), lambda i, j: (i, j + nj)),
                  pl.BlockSpec((hl, cb), halo_map(0)), pl.BlockSpec((hl, cb), halo_map(nj)),
                  pl.BlockSpec((3, cb), lambda i, j: (0, j)), pl.BlockSpec((3, cb), lambda i, j: (0, j + nj))],
        out_specs=pl.BlockSpec((tile, cb), lambda i, j: (i, j)),
        out_shape=jax.ShapeDtypeStruct((t_rows, f), BF16),
        scratch_shapes=[pltpu.VMEM((hl + tile, cb), F32), pltpu.VMEM((hl + tile, cb), F32)],
        semantics=("parallel", "parallel"),
    )(u, u, u, u, w, w)


def _ffn_mid_bwd(u, da, w, seq, name, side=None):
    t_rows, f2 = u.shape
    f = f2 // 2
    tile = min(256, seq)
    cb = _divisor_tile(f, 1408, LANES)
    hl = SUBLANES_BF16
    nj, tps, hb, n_hb, ext = f // cb, seq // tile, tile // hl, t_rows // hl, tile + hl

    def body(ug_ref, uv_ref, pg_ref, pv_ref, ng_ref, nv_ref, da_ref, dan_ref, wg_ref, wv_ref,
             dug_ref, duv_ref, dwg_ref, dwv_ref, gpad, vpad, dzg, dzv):
        i = pl.program_id(1)
        first, last = _seq_flags(i, tps)

        @pl.when(i == 0)
        def _():
            dwg_ref[...] = jnp.zeros_like(dwg_ref)
            dwv_ref[...] = jnp.zeros_like(dwv_ref)

        for t_ref, p_ref, n_ref, pad in ((ug_ref, pg_ref, ng_ref, gpad), (uv_ref, pv_ref, nv_ref, vpad)):
            pad[0:hl, :] = jnp.where(first, 0.0, p_ref[...].astype(F32))
            pad[hl:hl + tile, :] = t_ref[...].astype(F32)
            pad[hl + tile:hl + ext, :] = jnp.where(last, 0.0, n_ref[...].astype(F32))
        for cols in _lane_chunks(cb):
            wg, wv = _tap_rows(wg_ref, cols), _tap_rows(wv_ref, cols)
            for r0 in range(0, ext, _ROWS):
                zg = _conv3_at(gpad, wg, hl + r0, cols)
                zv = _conv3_at(vpad, wv, hl + r0, cols)
                if r0 < tile:
                    da = da_ref[pl.ds(r0, _ROWS), cols].astype(F32)
                else:
                    da = jnp.where(last, 0.0, dan_ref[:, cols].astype(F32))
                sg = _sigmoid(zg)
                dzg[pl.ds(r0, _ROWS), cols] = da * zv * (sg * (1.0 + zg * (1.0 - sg)))
                dzv[pl.ds(r0, _ROWS), cols] = da * (zg * sg)
        for dz, w_ref, pad, du_ref, dw_ref in ((dzg, wg_ref, gpad, dug_ref, dwg_ref), (dzv, wv_ref, vpad, duv_ref, dwv_ref)):
            for cols in _lane_chunks(cb):
                taps = _tap_rows(w_ref, cols)
                width = cols.stop - cols.start
                acc = [jnp.zeros((8, width), F32) for _ in range(FFN_CONV_WIDTH)]
                for r0 in range(0, tile, _ROWS):
                    d0 = dz[pl.ds(r0, _ROWS), cols]
                    du = dz[pl.ds(r0 + 2, _ROWS), cols] * taps[0] + dz[pl.ds(r0 + 1, _ROWS), cols] * taps[1] + d0 * taps[2]
                    du_ref[pl.ds(r0, _ROWS), cols] = du.astype(du_ref.dtype)
                    for k in range(FFN_CONV_WIDTH):
                        prod = d0 * pad[pl.ds(hl - 2 + k + r0, _ROWS), cols]
                        acc[k] = acc[k] + prod[0:8] + prod[8:16]
                for k in range(FFN_CONV_WIDTH):
                    dw_ref[k:k + 1, cols] += jnp.sum(acc[k], axis=0, keepdims=True)

    prev_map = lambda off: (lambda j, i: (jnp.maximum(i * hb - 1, 0), j + off))
    next_map = lambda off: (lambda j, i: (jnp.minimum((i + 1) * hb, n_hb - 1), j + off))
    tile_spec = lambda off: pl.BlockSpec((tile, cb), lambda j, i: (i, j + off))
    w_spec = lambda off: pl.BlockSpec((3, cb), lambda j, i: (0, j + off))
    return _pallas(
        body, side, name=name, grid=(nj, t_rows // tile),
        in_specs=[tile_spec(0), tile_spec(nj), pl.BlockSpec((hl, cb), prev_map(0)), pl.BlockSpec((hl, cb), prev_map(nj)),
                  pl.BlockSpec((hl, cb), next_map(0)), pl.BlockSpec((hl, cb), next_map(nj)),
                  tile_spec(0), pl.BlockSpec((hl, cb), next_map(0)), w_spec(0), w_spec(nj)],
        out_specs=[tile_spec(0), tile_spec(0), w_spec(0), w_spec(0)],
        out_shape=[jax.ShapeDtypeStruct((t_rows, f), BF16), jax.ShapeDtypeStruct((t_rows, f), BF16),
                   jax.ShapeDtypeStruct((3, f), F32), jax.ShapeDtypeStruct((3, f), F32)],
        scratch_shapes=[pltpu.VMEM((hl + ext, cb), F32), pltpu.VMEM((hl + ext, cb), F32),
                        pltpu.VMEM((ext, cb), F32), pltpu.VMEM((ext, cb), F32)],
        semantics=("parallel", "arbitrary"),
    )(u, u, u, u, u, u, da, da, w, w)


def _gla_chunk(q, k, v, lg, st, *, scale, chunk):
    row = lax.broadcasted_iota(jnp.int32, (chunk, chunk), 0)
    col = lax.broadcasted_iota(jnp.int32, (chunk, chunk), 1)
    causal = col <= row
    b = _cumsum_rows(lg)
    upto_mid = lax.broadcasted_iota(jnp.int32, lg.shape, 0) <= chunk // 2
    b_mid = jnp.sum(jnp.where(upto_mid, lg, 0.0), axis=0, keepdims=True)
    b_last = jnp.sum(lg, axis=0, keepdims=True)
    qs = q * scale
    scores = _bdot_nt(qs * jnp.exp(b - b_mid), k * jnp.exp(b_mid - b))
    o = _bdot(jnp.where(causal, scores, 0.0), v)
    o = o + _bdot_nt(qs * jnp.exp(b), st)
    st_new = st * jnp.exp(b_last) + _bdot_tn(v, k * jnp.exp(b_last - b))
    return o, st_new


def _gla_specs(specs, chunk, n_chunks, reverse):
    if reverse:
        row = lambda bi, ci: bi * n_chunks + (n_chunks - 1 - ci)
    else:
        row = lambda bi, ci: bi * n_chunks + ci
    return [pl.BlockSpec((chunk, w), lambda bi, ci, cb=cb: (row(bi, ci), cb)) for _, w, cb in specs], row


def _gla_fwd(q, k, v, lg, *, heads, dk, dv, scale, chunk, seq, name, side=None):
    t_rows = q[0].shape[0]
    n_chunks = seq // chunk
    fn = functools.partial(_gla_chunk, scale=scale, chunk=chunk)

    def body(q_ref, k_ref, v_ref, lg_ref, o_ref, sts_ref, st_ref):
        @pl.when(pl.program_id(1) == 0)
        def _():
            st_ref[...] = jnp.zeros_like(st_ref)

        sts_ref[0] = st_ref[...]
        ks = [slice(h * dk, (h + 1) * dk) for h in range(heads)]
        vs = [slice(h * dv, (h + 1) * dv) for h in range(heads)]
        ins = [(q_ref[:, ks[h]].astype(F32), k_ref[:, ks[h]].astype(F32), v_ref[:, vs[h]].astype(F32), lg_ref[:, ks[h]],
                st_ref[vs[h], :]) for h in range(heads)]
        outs = [fn(*args) for args in ins]
        for h, (o, st) in enumerate(outs):
            o_ref[:, vs[h]] = o
            st_ref[vs[h], :] = st

    in_specs, row = _gla_specs([q, k, v, lg], chunk, n_chunks, False)
    return _pallas(
        body, side, name=name, grid=(t_rows // seq, n_chunks), in_specs=in_specs,
        out_specs=[pl.BlockSpec((chunk, heads * dv), lambda bi, ci: (row(bi, ci), 0)),
                   pl.BlockSpec((1, heads * dv, dk), lambda bi, ci: (row(bi, ci), 0, 0))],
        out_shape=[jax.ShapeDtypeStruct((t_rows, heads * dv), F32),
                   jax.ShapeDtypeStruct((t_rows // chunk, heads * dv, dk), F32)],
        scratch_shapes=[pltpu.VMEM((heads * dv, dk), F32)],
        semantics=("arbitrary", "arbitrary"),
    )(q[0], k[0], v[0], lg[0])


def _gla_bwd(q, k, v, lg, states, do, *, heads, dk, dv, scale, chunk, seq, out_dtypes, name, side=None):
    t_rows = q[0].shape[0]
    n_chunks = seq // chunk
    fn = functools.partial(_gla_chunk, scale=scale, chunk=chunk)

    def body(q_ref, k_ref, v_ref, lg_ref, do_ref, sts_ref, dq_ref, dk_ref, dv_ref, dlg_ref, dst_ref):
        @pl.when(pl.program_id(1) == 0)
        def _():
            dst_ref[...] = jnp.zeros_like(dst_ref)

        ks = [slice(h * dk, (h + 1) * dk) for h in range(heads)]
        vs = [slice(h * dv, (h + 1) * dv) for h in range(heads)]
        ins = [(q_ref[:, ks[h]].astype(F32), k_ref[:, ks[h]].astype(F32), v_ref[:, vs[h]].astype(F32), lg_ref[:, ks[h]],
                sts_ref[0, vs[h], :]) for h in range(heads)]
        cts = [(do_ref[:, vs[h]].astype(F32), dst_ref[vs[h], :]) for h in range(heads)]
        outs = [jax.vjp(fn, *ins[h])[1](cts[h]) for h in range(heads)]
        for h, (dq, dkk, dvv, dlg, dst) in enumerate(outs):
            dq_ref[:, ks[h]] = dq.astype(dq_ref.dtype)
            dk_ref[:, ks[h]] = dkk.astype(dk_ref.dtype)
            dv_ref[:, vs[h]] = dvv.astype(dv_ref.dtype)
            dlg_ref[:, ks[h]] = dlg
            dst_ref[vs[h], :] = dst

    do_view = (do, heads * dv, 0)
    in_specs, row = _gla_specs([q, k, v, lg, do_view], chunk, n_chunks, True)
    in_specs.append(pl.BlockSpec((1, heads * dv, dk), lambda bi, ci: (row(bi, ci), 0, 0)))
    wide = lambda w: pl.BlockSpec((chunk, w), lambda bi, ci: (row(bi, ci), 0))
    return _pallas(
        body, side, name=name, grid=(t_rows // seq, n_chunks), in_specs=in_specs,
        out_specs=[wide(heads * dk), wide(heads * dk), wide(heads * dv), wide(heads * dk)],
        out_shape=[jax.ShapeDtypeStruct((t_rows, heads * dk), out_dtypes[0]),
                   jax.ShapeDtypeStruct((t_rows, heads * dk), out_dtypes[1]),
                   jax.ShapeDtypeStruct((t_rows, heads * dv), out_dtypes[2]),
                   jax.ShapeDtypeStruct((t_rows, heads * dk), F32)],
        scratch_shapes=[pltpu.VMEM((heads * dv, dk), F32)],
        semantics=("arbitrary", "arbitrary"),
    )(q[0], k[0], v[0], lg[0], do, states)


def _head_rms_gate(o, r, g, heads):
    d = o.shape[1] // heads
    parts = [_rms(o[:, h * d:(h + 1) * d], g) for h in range(heads)]
    return jnp.concatenate(parts, axis=1) * _silu(r)


def _gla_gate(glr, w_g2p, b_g2):
    return _log_sigmoid(_bdot(glr, w_g2p) + b_g2) * (1.0 / GLA_GATE_NORM)


def _glu(a, gate, b_in):
    d = a.shape[1]
    return (a + b_in[:, :d]) * _sigmoid(gate + b_in[:, d:])


def _ln_silu(y, g, b):
    return _silu(_layer_norm(y, g, b))


def _sgu(pre, b_in, ln_g, ln_b, w_s, b_st):
    d = pre.shape[1] // 2
    gd = d // SGU_GROUPS
    uv = _gelu(pre + b_in)
    u, v = uv[:, :d], _layer_norm(uv[:, d:], ln_g, ln_b)
    row = lax.broadcasted_iota(jnp.int32, (SGU_CHUNK, SGU_CHUNK), 0)
    col = lax.broadcasted_iota(jnp.int32, (SGU_CHUNK, SGU_CHUNK), 1)
    lane = lax.broadcasted_iota(jnp.int32, b_st.shape, 1)
    rows = []
    for c in range(pre.shape[0] // SGU_CHUNK):
        rs = slice(c * SGU_CHUNK, (c + 1) * SGU_CHUNK)
        parts = []
        for g in range(SGU_GROUPS):
            wg = jnp.where(col <= row, w_s[g], 0.0)
            bias = jnp.sum(jnp.where(lane == g, b_st, 0.0), axis=1, keepdims=True)
            parts.append(_bdot(wg, v[rs, g * gd:(g + 1) * gd]) + bias)
        rows.append(jnp.concatenate(parts, axis=1))
    s = rows[0] if len(rows) == 1 else jnp.concatenate(rows, axis=0)
    return u * s


def _hgrn_pre(q, f, table, layer):
    t = table - jnp.max(table, axis=0, keepdims=True)
    e = jnp.exp(t)
    sm = e / jnp.sum(e, axis=0, keepdims=True)
    rows = lax.broadcasted_iota(jnp.int32, table.shape, 0)
    lb = jnp.sum(jnp.where((rows >= 1) & (rows <= layer), sm, 0.0), axis=0, keepdims=True)
    sf = _sigmoid(f)
    return _silu(q), (1.0 - lb) * (1.0 - sf), jnp.log(lb + (1.0 - lb) * sf)


def _ffn_fwd(x, w, seq, sv):
    sv["h2"] = _rms_fwd(x, w["norm"], "ffn_norm")
    late = w.get("late")
    sv["u"] = _mm(sv["h2"], w["w_up"], out_dtype=BF16, name="ffn_up", side=late[0] if late else None)
    sv["a"] = _ffn_mid_fwd(sv["u"], w["w_dw"], seq, "ffn_mid", sv.pop("side", None))
    if late:
        sv["late_w"] = late[1]()
        w = dict(w, **sv["late_w"])
    return _mm(sv["a"], w["w_down"], add=x, name="ffn_down")


def _ffn_bwd(x, dy, w, seq, sv):
    g = {}
    da = _mm(dy, w["w_down_t"], out_dtype=BF16, name="ffn_down_dx")
    g["w_down"] = _mm_tn(sv["a"], dy, name="ffn_down_dw")
    early = sv.pop("put_early", None)
    side = _Side.join([sv.pop("side", None), early(g["w_down"]) if early else None])
    dug, duv, dwg, dwv = _ffn_mid_bwd(sv["u"], da, w["w_dw"], seq, "ffn_mid_bwd", side)
    g["w_dw"] = jnp.concatenate([dwg, dwv], axis=1)
    g["w_up_gate"] = _mm_tn(sv["h2"], dug, name="ffn_up_dw")
    g["w_up_val"] = _mm_tn(sv["h2"], duv, name="ffn_up_dw")
    dh = _mm(dug, w["w_up_t_gate"], out_dtype=F32, name="ffn_up_dx")
    dh = _mm(duv, w["w_up_t_val"], add=dh, out_dtype=BF16, name="ffn_up_dx2")
    dx, g["norm"] = _rms_bwd(x, w["norm"], dh, dy, "ffn_norm_bwd")
    return dx, g


def _gla_layer_fwd(x, h, w, seq, sv):
    d = x.shape[1]
    dkt = d // 2
    dk, dv = dkt // GLA_HEADS, d // GLA_HEADS
    proj = _mm(h, w["w_main"], out_dtype=F32, name="gla_in")
    glr = _mm(h, w["w_glr"], out_dtype=BF16, name="gla_in_g")
    (lg,), _ = _tile_call("gla_gate", lambda a, b, c: ([_gla_gate(a.astype(F32), b, c)], []), [glr],
                          [w["w_g2p"], w["b_g2"]], [(dkt, F32)], [], 512)
    q, k, v, r = (proj, dkt, 0), (proj, dkt, 1), (proj, d, 1), (proj, d, 2)
    o, states = _gla_fwd(q, k, v, (lg, dkt, 0), heads=GLA_HEADS, dk=dk, dv=dv, scale=dk ** -0.5, chunk=GLA_CHUNK,
                         seq=seq, name="gla_core", side=sv.pop("side", None))
    (o2,), _ = _tile_call("gla_post", lambda ov, rv, gv: ([_head_rms_gate(ov, rv.astype(F32), gv, GLA_HEADS)], []),
                          [o, r], [w["norm"]], [(d, BF16)], [], 256)
    sv.update(proj=proj, glr=glr, lg=lg, o=o, states=states, o2=o2)
    return _mm(o2, w["w_out"], add=x, name="mix_out")


def _gla_layer_bwd(h, dy, w, seq, sv):
    d = dy.shape[1]
    dkt = d // 2
    dk, dv = dkt // GLA_HEADS, d // GLA_HEADS
    proj, glr, lg, o = sv["proj"], sv["glr"], sv["lg"], sv["o"]
    g = {}
    do2 = _mm(dy, w["w_out_t"], out_dtype=F32, name="gla_out_dx")
    g["w_out"] = _mm_tn(sv["o2"], dy, name="mix_out_dw")

    def post_bwd(ov, rv, ctv, gv):
        _, vjp = jax.vjp(functools.partial(_head_rms_gate, heads=GLA_HEADS), ov, rv.astype(F32), gv)
        d_o, d_r, d_g = vjp(ctv.astype(F32))
        return [d_o, d_r], [d_g]

    (d_o, d_r), (g["norm"],) = _tile_call("gla_post_bwd", post_bwd, [o, (proj, d, 2), do2], [w["norm"]],
                                          [(d, F32), (d, BF16)], [w["norm"].shape], 256)
    q, k, v = (proj, dkt, 0), (proj, dkt, 1), (proj, d, 1)
    dq, dkk, dvv, dlg = _gla_bwd(q, k, v, (lg, dkt, 0), sv["states"], d_o, heads=GLA_HEADS, dk=dk, dv=dv,
                                 scale=dk ** -0.5, chunk=GLA_CHUNK, seq=seq, out_dtypes=(BF16, BF16, BF16),
                                 name="gla_core_bwd", side=sv.pop("side", None))

    def gate_bwd(glrv, ctv, wv, bv):
        _, vjp = jax.vjp(_gla_gate, glrv.astype(F32), wv, bv)
        d_glr, d_w, d_b = vjp(ctv)
        return [d_glr], [d_w, d_b]

    (dglr,), (g["w_g2p"], g["b_g2"]) = _tile_call("gla_gate_bwd", gate_bwd, [glr, dlg], [w["w_g2p"], w["b_g2"]],
                                                  [(LANES, BF16)], [w["w_g2p"].shape, w["b_g2"].shape], 512)
    dproj = jnp.concatenate([dq, dkk, dvv, d_r], axis=1)
    g["w_main"] = _mm_tn(h, dproj, name="gla_in_dw")
    g["w_glr"] = _mm_tn(h, dglr, name="gla_in_g_dw")
    dh = _mm(dproj, w["w_main_t"], out_dtype=F32, name="gla_in_dx")
    dh = _mm(dglr, w["w_glr_t"], add=dh, out_dtype=BF16, name="gla_in_g_dx")
    return dh, g


def _cv_layer_fwd(x, h, w, seq, sv):
    d = x.shape[1]
    pre = _mm(h, w["w_in"], out_dtype=BF16, name="cv_in")
    (y1,), _ = _tile_call("cv_glu", lambda a, gt, b: ([_glu(a.astype(F32), gt.astype(F32), b)], []),
                          [(pre, d, 0), (pre, d, 1)], [w["b_in"]], [(d, F32)], [], 512)
    y2 = _dwconv_fwd(y1, w["w_dw"], w["b_dw"], seq, "cv_conv", sv.pop("side", None))
    (y3,), _ = _tile_call("cv_ln", lambda y, a, b: ([_ln_silu(y, a, b)], []), [y2], [w["ln_g"], w["ln_b"]],
                          [(d, BF16)], [], 512)
    sv.update(pre=pre, y1=y1, y2=y2, y3=y3)
    return _mm(y3, w["w_out"], bias=w["b_out"], add=x, name="mix_out_b")


def _cv_layer_bwd(h, dy, w, seq, sv):
    d = dy.shape[1]
    pre = sv["pre"]
    g = {}
    dy3 = _mm(dy, w["w_out_t"], out_dtype=BF16, name="mix_out_dx")
    g["w_out"] = _mm_tn(sv["y3"], dy, name="mix_out_dw")
    g["b_out"] = _colsum(dy, "bias_out_dw")

    def ln_bwd(yv, ctv, av, bv):
        _, vjp = jax.vjp(_ln_silu, yv, av, bv)
        d_y, d_a, d_b = vjp(ctv.astype(F32))
        return [d_y], [d_a, d_b]

    (dy2,), (g["ln_g"], g["ln_b"]) = _tile_call("cv_ln_bwd", ln_bwd, [sv["y2"], dy3], [w["ln_g"], w["ln_b"]],
                                                [(d, F32)], [w["ln_g"].shape, w["ln_b"].shape], 512)
    dy1, g["w_dw"], g["b_dw"] = _dwconv_bwd(sv["y1"], dy2, w["w_dw"], seq, "cv_conv_bwd", sv.pop("side", None))

    def glu_bwd(av, gv, ctv, bv):
        _, vjp = jax.vjp(_glu, av.astype(F32), gv.astype(F32), bv)
        d_a, d_g, d_b = vjp(ctv)
        return [jnp.concatenate([d_a, d_g], axis=1)], [d_b]

    (dpre,), (g["b_in"],) = _tile_call("cv_glu_bwd", glu_bwd, [(pre, d, 0), (pre, d, 1), dy1], [w["b_in"]],
                                       [(2 * d, BF16)], [w["b_in"].shape], 512)
    g["w_in"] = _mm_tn(h, dpre, name="in2_dw")
    dh = _mm(dpre, w["w_in_t"], out_dtype=BF16, name="in2_dx")
    return dh, g


def _sg_layer_fwd(x, h, w, seq, sv):
    d = x.shape[1]
    pre = _mm(h, w["w_in"], out_dtype=BF16, name="sg_in")
    pars = [w["b_in"], w["ln_g"], w["ln_b"], w["w_s"], w["b_st"]]
    (p,), _ = _tile_call("sg_gate", lambda pv, *ps: ([_sgu(pv.astype(F32), *ps)], []), [pre], pars, [(d, BF16)], [],
                         SGU_CHUNK, side=sv.pop("side", None))
    sv.update(pre=pre, p=p)
    return _mm(p, w["w_out"], bias=w["b_out"], add=x, name="mix_out_b")


def _sg_layer_bwd(h, dy, w, seq, sv):
    d = dy.shape[1]
    g = {}
    dp = _mm(dy, w["w_out_t"], out_dtype=BF16, name="mix_out_dx")
    g["w_out"] = _mm_tn(sv["p"], dy, name="mix_out_dw")
    g["b_out"] = _colsum(dy, "bias_out_dw")
    pars = [w["b_in"], w["ln_g"], w["ln_b"], w["w_s"], w["b_st"]]

    def sgu_bwd(pv, ctv, *ps):
        _, vjp = jax.vjp(_sgu, pv.astype(F32), *ps)
        grads = vjp(ctv.astype(F32))
        return [grads[0]], list(grads[1:])

    (dpre,), (g["b_in"], g["ln_g"], g["ln_b"], g["w_s"], g["b_st"]) = _tile_call(
        "sg_gate_bwd", sgu_bwd, [sv["pre"], dp], pars, [(2 * d, BF16)], [p.shape for p in pars], SGU_CHUNK,
        side=sv.pop("side", None))
    g["w_in"] = _mm_tn(h, dpre, name="in2_dw")
    dh = _mm(dpre, w["w_in_t"], out_dtype=BF16, name="in2_dx")
    return dh, g


def _hg_layer_fwd(x, h, w, seq, sv, layer):
    d = x.shape[1]
    heads = d // HGRN_EXPAND
    proj = _mm(h, w["w_in"], out_dtype=BF16, name="hg_in")
    pre = functools.partial(_hgrn_pre, layer=layer)
    (qs, kk, lg), _ = _tile_call("hg_pre", lambda qv, fv, tb: (list(pre(qv.astype(F32), fv.astype(F32), tb)), []),
                                 [(proj, d, 0), (proj, d, 1)], [w["lb_table"]], [(d, BF16), (d, F32), (d, F32)], [], 256)
    o, states = _gla_fwd((qs, d, 0), (kk, d, 0), (proj, d, 2), (lg, d, 0), heads=heads, dk=HGRN_EXPAND,
                         dv=HGRN_EXPAND, scale=1.0, chunk=HGRN_CHUNK, seq=seq, name="hg_core",
                         side=sv.pop("side", None))
    (o2,), _ = _tile_call("hg_post", lambda ov, gv, nv: ([_head_rms_gate(ov, gv.astype(F32), nv, heads)], []),
                          [o, (proj, d, 3)], [w["norm"]], [(d, BF16)], [], 256)
    sv.update(proj=proj, qs=qs, kk=kk, lg=lg, o=o, states=states, o2=o2)
    return _mm(o2, w["w_out"], add=x, name="mix_out")


def _hg_layer_bwd(h, dy, w, seq, sv, layer):
    d = dy.shape[1]
    heads = d // HGRN_EXPAND
    proj = sv["proj"]
    g = {}
    do2 = _mm(dy, w["w_out_t"], out_dtype=BF16, name="mix_out_dx")
    g["w_out"] = _mm_tn(sv["o2"], dy, name="mix_out_dw")

    def post_bwd(ov, gv, ctv, nv):
        _, vjp = jax.vjp(functools.partial(_head_rms_gate, heads=heads), ov, gv.astype(F32), nv)
        d_o, d_g, d_n = vjp(ctv.astype(F32))
        return [d_o, d_g], [d_n]

    (d_o, d_gate), (g["norm"],) = _tile_call("hg_post_bwd", post_bwd, [sv["o"], (proj, d, 3), do2], [w["norm"]],
                                             [(d, F32), (d, BF16)], [w["norm"].shape], 256)
    dqs, dkk, di, dlg = _gla_bwd((sv["qs"], d, 0), (sv["kk"], d, 0), (proj, d, 2), (sv["lg"], d, 0), sv["states"], d_o,
                                 heads=heads, dk=HGRN_EXPAND, dv=HGRN_EXPAND, scale=1.0, chunk=HGRN_CHUNK, seq=seq,
                                 out_dtypes=(F32, F32, BF16), name="hg_core_bwd", side=sv.pop("side", None))

    def pre_bwd(qv, fv, c1, c2, c3, tb):
        _, vjp = jax.vjp(functools.partial(_hgrn_pre, layer=layer), qv.astype(F32), fv.astype(F32), tb)
        d_q, d_f, d_t = vjp((c1, c2, c3))
        return [jnp.concatenate([d_q, d_f], axis=1)], [d_t]

    (dqf,), (g["lb_table"],) = _tile_call("hg_pre_bwd", pre_bwd, [(proj, d, 0), (proj, d, 1), dqs, dkk, dlg],
                                          [w["lb_table"]], [(2 * d, BF16)], [w["lb_table"].shape], 256)
    dproj = jnp.concatenate([dqf, di, d_gate], axis=1)
    g["w_in"] = _mm_tn(h, dproj, name="hg_in_dw")
    dh = _mm(dproj, w["w_in_t"], out_dtype=BF16, name="hg_in_dx")
    return dh, g


_MIXERS = ("gla", "cv", "sg", "hg")


_BIG_KEYS = {"gla": ("w_main", "w_glr", "w_out"), "cv": ("w_in", "w_out"), "sg": ("w_in", "w_out"), "hg": ("w_in", "w_out"),
             "ffn": ("w_up_gate", "w_up_val", "w_down")}


def _local_step(x, target, w, seq, get_big, put_big, ride=lambda kind, layer, forward: None, put_early=None):
    depth = w["norm_mix"].shape[0]
    d = x.shape[1]
    saved, big = [], {}
    for layer in range(depth):
        mixer = _MIXERS[layer % 4]
        sv = {"x_in": x, "side": ride(mixer, layer, True)}
        sv["h"] = _rms_fwd(x, w["norm_mix"][layer:layer + 1], "mix_norm")
        big[mixer, layer] = get_big(mixer, layer)
        wm = dict(w[mixer], **big[mixer, layer])
        if mixer == "gla":
            x = _gla_layer_fwd(x, sv["h"], wm, seq, sv)
        elif mixer == "cv":
            x = _cv_layer_fwd(x, sv["h"], wm, seq, sv)
        elif mixer == "sg":
            x = _sg_layer_fwd(x, sv["h"], wm, seq, sv)
        else:
            x = _hg_layer_fwd(x, sv["h"], wm, seq, sv, layer)
        sv["x_mid"] = x
        big["ffn", layer] = get_big("ffn", layer)
        sv["ffn"] = {"side": ride("ffn", layer, True)}
        wf = dict(w["ffn"][layer], norm=w["norm_ffn"][layer:layer + 1], **big["ffn", layer])
        x = _ffn_fwd(x, wf, seq, sv["ffn"])
        big["ffn", layer].pop("late", None)
        big["ffn", layer].update(sv["ffn"].pop("late_w", {}))
        saved.append(sv)

    def head(xv, tv, gv):
        y, vjp = jax.vjp(_rms, xv, gv)
        err = y - tv
        dx, dg = vjp(err * (1.0 / d))
        part = 0.5 * jnp.sum(jnp.mean(err * err, axis=-1, keepdims=True), axis=0, keepdims=True)
        return [dx], [jnp.broadcast_to(part, (1, LANES)), dg]

    (dx,), (loss, g_final) = _tile_call("loss_head", head, [x, target], [w["norm_final"]], [(d, F32)],
                                        [(1, LANES), (1, d)], 512)
    grads = {"norm_final": g_final, "norm_mix": [None] * depth, "norm_ffn": [None] * depth, "ffn": [None] * depth}
    for layer in reversed(range(depth)):
        mixer = _MIXERS[layer % 4]
        sv = saved[layer]
        wf = dict(w["ffn"][layer], norm=w["norm_ffn"][layer:layer + 1], **big["ffn", layer])
        sv["ffn"]["side"] = ride("ffn", layer, False)
        if put_early is not None:
            sv["ffn"]["put_early"] = functools.partial(put_early, layer)
        dx, gf = _ffn_bwd(sv["x_mid"], dx, wf, seq, sv["ffn"])
        put_big("ffn", layer, {k: gf.pop(k) for k in _BIG_KEYS["ffn"]})
        sv["side"] = ride(mixer, layer, False)
        grads["norm_ffn"][layer] = gf.pop("norm")
        grads["ffn"][layer] = gf
        wm = dict(w[mixer], **big[mixer, layer])
        if mixer == "gla":
            dh, gm = _gla_layer_bwd(sv["h"], dx, wm, seq, sv)
        elif mixer == "cv":
            dh, gm = _cv_layer_bwd(sv["h"], dx, wm, seq, sv)
        elif mixer == "sg":
            dh, gm = _sg_layer_bwd(sv["h"], dx, wm, seq, sv)
        else:
            dh, gm = _hg_layer_bwd(sv["h"], dx, wm, seq, sv, layer)
        put_big(mixer, layer, {k: gm.pop(k) for k in _BIG_KEYS[mixer]})
        grads[mixer] = gm
        dx, grads["norm_mix"][layer] = _rms_bwd(sv["x_in"], w["norm_mix"][layer:layer + 1], dh, dx, "mix_norm_bwd")
    return loss, dx, grads


def _prep_small(p):
    row = lambda a: a.reshape(1, -1).astype(F32)
    w = {"norm_mix": p["norm_mix"].astype(F32), "norm_ffn": p["norm_ffn"].astype(F32), "norm_final": row(p["norm_final"])}
    w["gla"] = dict(w_g2p=jnp.pad(p["gla_w_g2"][0].astype(F32), ((0, LANES - GLA_RANK), (0, 0))), b_g2=row(p["gla_b_g2"]),
                    norm=row(p["gla_norm"]))
    w["cv"] = dict(b_in=row(p["cv_b_in"]), w_dw=p["cv_w_dw"][0].astype(F32), b_dw=row(p["cv_b_dw"]), ln_g=row(p["cv_ln_g"]),
                   ln_b=row(p["cv_ln_b"]), b_out=row(p["cv_b_out"]))
    b_st = jnp.pad(p["sg_b_s"][0].astype(F32).T, ((0, 0), (0, LANES - SGU_GROUPS)))
    w["sg"] = dict(b_in=row(p["sg_b_in"]), ln_g=row(p["sg_ln_g"]), ln_b=row(p["sg_ln_b"]), w_s=p["sg_w_s"][0].astype(F32),
                   b_st=b_st, b_out=row(p["sg_b_out"]))
    w["hg"] = dict(lb_table=p["hg_lb_table"].astype(F32), norm=row(p["hg_norm"]))
    w["ffn"] = [dict(w_dw=p["ffn_w_dw"][layer].astype(F32)) for layer in range(p["ffn_w_dw"].shape[0])]
    return w


def _small_grads(g):
    gla, cv, sg, hg = g["gla"], g["cv"], g["sg"], g["hg"]
    return {
        "norm_mix": jnp.concatenate(g["norm_mix"], axis=0), "norm_ffn": jnp.concatenate(g["norm_ffn"], axis=0),
        "norm_final": g["norm_final"][0],
        "gla_w_g2": gla["w_g2p"][:GLA_RANK][None], "gla_b_g2": gla["b_g2"], "gla_norm": gla["norm"],
        "cv_b_in": cv["b_in"], "cv_w_dw": cv["w_dw"][None], "cv_b_dw": cv["b_dw"], "cv_ln_g": cv["ln_g"],
        "cv_ln_b": cv["ln_b"], "cv_b_out": cv["b_out"],
        "sg_b_in": sg["b_in"], "sg_ln_g": sg["ln_g"], "sg_ln_b": sg["ln_b"], "sg_w_s": sg["w_s"][None],
        "sg_b_s": sg["b_st"][:, :SGU_GROUPS].T[None], "sg_b_out": sg["b_out"],
        "hg_lb_table": hg["lb_table"], "hg_norm": hg["norm"],
        "ffn_w_dw": jnp.stack([f["w_dw"] for f in g["ffn"]]),
    }


def _oriented(kind, mats):
    if kind == "ffn":
        (up, up_t), (down, down_t) = mats["w_up"], mats["w_down"]
        f = down.shape[0]
        return dict(w_up=up, w_up_t_gate=up_t[:f], w_up_t_val=up_t[f:], w_down=down, w_down_t=down_t)
    (w_in, w_in_t), (w_out, w_out_t) = mats["w_in"], mats["w_out"]
    if kind != "gla":
        return dict(w_in=w_in, w_in_t=w_in_t, w_out=w_out, w_out_t=w_out_t)
    n_main = w_in.shape[1] - GLA_RANK
    return dict(w_main=w_in[:, :n_main], w_glr=jnp.pad(w_in[:, n_main:], ((0, 0), (0, LANES - GLA_RANK))),
                w_main_t=w_in_t[:n_main], w_glr_t=jnp.pad(w_in_t[n_main:], ((0, LANES - GLA_RANK), (0, 0))),
                w_out=w_out, w_out_t=w_out_t)


def _all_gather(x, *, name):
    m_per, n = x.shape

    def body(x_ref, out_ref, send_sems, recv_sems, local_sem):
        mx, my, mc = lax.axis_index("x"), lax.axis_index("y"), lax.axis_index("c")
        me, sibling = (mx, my, mc), (mx, my, 1 - mc)
        chips = [(1 - mx, my), (mx, 1 - my), (1 - mx, 1 - my)]

        def rows(px, py, pc):
            return out_ref.at[pl.ds((4 * px + 2 * py + pc) * m_per, m_per), :]

        def copy(k, block, to, src=None):
            return pltpu.make_async_remote_copy(
                src_ref=rows(*block) if src is None else src, dst_ref=rows(*block), send_sem=send_sems.at[k],
                recv_sem=recv_sems.at[k], device_id=to, device_id_type=MESH)

        mine = pltpu.make_async_copy(x_ref, rows(*me), local_sem)
        mine.start()
        first = [copy(0, me, sibling, src=x_ref)]
        first += [copy(1 + j, me, (*chip, mc), src=x_ref) for j, chip in enumerate(chips)]
        for cp in first:
            cp.start()
        passed = [copy(4 + j, (*chip, mc), sibling) for j, chip in enumerate(chips)]
        for j, chip in enumerate(chips):
            copy(1 + j, (*chip, mc), me).wait_recv()
            passed[j].start()
        copy(0, sibling, me).wait_recv()
        for j, chip in enumerate(chips):
            copy(4 + j, (*chip, 1 - mc), me).wait_recv()
        for cp in first + passed:
            cp.wait_send()
        mine.wait()

    return pl.pallas_call(
        body, name=name, out_shape=jax.ShapeDtypeStruct((N_DEV * m_per, n), x.dtype),
        in_specs=[pl.BlockSpec(memory_space=pltpu.VMEM)], out_specs=pl.BlockSpec(memory_space=pltpu.VMEM),
        scratch_shapes=[pltpu.SemaphoreType.DMA((7,)), pltpu.SemaphoreType.DMA((7,)), pltpu.SemaphoreType.DMA],
    )(x)


def _my_index():
    return 4 * lax.axis_index("x") + 2 * lax.axis_index("y") + lax.axis_index("c")


def _gather_stage(srcs, *, name):
    n = len(srcs)

    def body(*refs):
        x_refs, out_refs = refs[:n], refs[n:2 * n]
        send_sems, recv_sems, local_sems = refs[2 * n:]
        mx, my, mc = lax.axis_index("x"), lax.axis_index("y"), lax.axis_index("c")
        me, sibling = (mx, my, mc), (mx, my, 1 - mc)
        chips = [(1 - mx, my), (mx, 1 - my), (1 - mx, 1 - my)]

        def slot(i, px, py, pc):
            return out_refs[i].at[4 * px + 2 * py + pc]

        def copy(i, k, block, to, src=None):
            return pltpu.make_async_remote_copy(
                src_ref=slot(i, *block) if src is None else src, dst_ref=slot(i, *block), send_sem=send_sems.at[7 * i + k],
                recv_sem=recv_sems.at[7 * i + k], device_id=to, device_id_type=MESH)

        mine = [pltpu.make_async_copy(x_refs[i], slot(i, *me), local_sems.at[i]) for i in range(n)]
        first = [copy(i, 0, me, sibling, src=x_refs[i]) for i in range(n)]
        first += [copy(i, 1 + j, me, (*chip, mc), src=x_refs[i]) for j, chip in enumerate(chips) for i in range(n)]
        for cp in mine + first:
            cp.start()
        passed = []
        for j, chip in enumerate(chips):
            for i in range(n):
                copy(i, 1 + j, (*chip, mc), me).wait_recv()
                passed.append(copy(i, 4 + j, (*chip, mc), sibling))
                passed[-1].start()
        for i in range(n):
            copy(i, 0, sibling, me).wait_recv()
            for j, chip in enumerate(chips):
                copy(i, 4 + j, (*chip, 1 - mc), me).wait_recv()
        for cp in first + passed:
            cp.wait_send()
        for cp in mine:
            cp.wait()

    any_space = pl.BlockSpec(memory_space=pl.ANY)
    return pl.pallas_call(
        body, name=name, out_shape=[jax.ShapeDtypeStruct((N_DEV,) + s.shape, s.dtype) for s in srcs],
        in_specs=[any_space] * n, out_specs=[any_space] * n,
        scratch_shapes=[pltpu.SemaphoreType.DMA((7 * n,)), pltpu.SemaphoreType.DMA((7 * n,)), pltpu.SemaphoreType.DMA((n,))],
    )(*srcs)


def _scatter_stage(srcs, *, name):
    n = len(srcs)

    def body(*refs):
        x_refs, out_refs = refs[:n], refs[n:2 * n]
        send_sems, recv_sems, local_sems = refs[2 * n:]
        mx, my, mc = lax.axis_index("x"), lax.axis_index("y"), lax.axis_index("c")
        me = 4 * mx + 2 * my + mc
        mine = [pltpu.make_async_copy(x_refs[i].at[me], out_refs[i].at[me], local_sems.at[i]) for i in range(n)]
        for cp in mine:
            cp.start()
        sends, recvs = [], []
        for k in range(1, N_DEV):
            px = 1 - mx if k & 4 else mx
            py = 1 - my if k & 2 else my
            pc = 1 - mc if k & 1 else mc
            peer = 4 * px + 2 * py + pc
            for i in range(n):
                sems = dict(send_sem=send_sems.at[7 * i + k - 1], recv_sem=recv_sems.at[7 * i + k - 1],
                            device_id=(px, py, pc), device_id_type=MESH)
                sends.append(pltpu.make_async_remote_copy(src_ref=x_refs[i].at[peer], dst_ref=out_refs[i].at[me], **sems))
                recvs.append(pltpu.make_async_remote_copy(src_ref=x_refs[i].at[me], dst_ref=out_refs[i].at[peer], **sems))
                sends[-1].start()
        for cp in recvs:
            cp.wait_recv()
        for cp in sends:
            cp.wait_send()
        for cp in mine:
            cp.wait()

    any_space = pl.BlockSpec(memory_space=pl.ANY)
    return pl.pallas_call(
        body, name=name, out_shape=[jax.ShapeDtypeStruct(s.shape, s.dtype) for s in srcs],
        in_specs=[any_space] * n, out_specs=[any_space] * n,
        scratch_shapes=[pltpu.SemaphoreType.DMA((7 * n,)), pltpu.SemaphoreType.DMA((7 * n,)), pltpu.SemaphoreType.DMA((n,))],
    )(*srcs)


def _adamw_math(g, w, m, v):
    c1, c2 = 1.0 - ADAM_B1 ** ADAM_STEP, 1.0 - ADAM_B2 ** ADAM_STEP
    m_new = ADAM_B1 * m + (1.0 - ADAM_B1) * g
    v_new = ADAM_B2 * v + (1.0 - ADAM_B2) * (g * g)
    delta = -ADAM_LR * ((m_new / c1) / (jnp.sqrt(v_new / c2) + ADAM_EPS) + ADAM_WD * w)
    return delta, m_new, v_new


def _adamw_big(slots, w, m, v, layer, *, name):
    _, r, c = slots.shape
    tr = _divisor_tile(r, max(8, (200 * 1024) // c // 8 * 8), 8)

    def body(s_ref, w_ref, m_ref, v_ref, g_out, d_out, m_out, v_out):
        g = s_ref[0].astype(F32)
        for p in range(1, N_DEV):
            g = g + s_ref[p].astype(F32)
        g_out[...] = g
        d_out[...], m_out[...], v_out[...] = _adamw_math(g, w_ref[...], m_ref[...], v_ref[...])

    blk = pl.BlockSpec((tr, c), lambda i: (i, 0))
    lay = pl.BlockSpec((None, tr, c), lambda i: (layer, i, 0))
    return pl.pallas_call(
        body, name=name, grid=(r // tr,), in_specs=[pl.BlockSpec((N_DEV, tr, c), lambda i: (0, i, 0)), lay, lay, lay],
        out_specs=[blk] * 4, out_shape=[jax.ShapeDtypeStruct((r, c), F32)] * 4,
        compiler_params=pltpu.CompilerParams(dimension_semantics=("parallel",)),
    )(slots, w, m, v)


def _sum_small(got, r_re, r_sh, *, name):
    per_dev = r_re + N_DEV * r_sh

    def body(got_ref, re_ref, sh_ref):
        mine = r_re + _my_index() * r_sh
        acc_re = got_ref[0:r_re, :]
        acc_sh = got_ref[pl.ds(pl.multiple_of(mine, 8), r_sh), :]
        for p in range(1, N_DEV):
            acc_re = acc_re + got_ref[p * per_dev:p * per_dev + r_re, :]
            acc_sh = acc_sh + got_ref[pl.ds(pl.multiple_of(p * per_dev + mine, 8), r_sh), :]
        re_ref[...] = acc_re
        sh_ref[...] = acc_sh

    return pl.pallas_call(body, name=name, out_shape=[jax.ShapeDtypeStruct((r_re, LANES), F32),
                                                       jax.ShapeDtypeStruct((r_sh, LANES), F32)])(got)


def _adamw_small(gs, ws, ms, vs, *, name):
    n = len(gs)

    def body(*refs):
        ins, outs = refs[:4 * n], refs[4 * n:]
        for i in range(n):
            res = _adamw_math(ins[i][...], ins[n + i][...], ins[2 * n + i][...], ins[3 * n + i][...])
            for j in range(3):
                outs[j * n + i][...] = res[j]

    out = pl.pallas_call(body, name=name, out_shape=[jax.ShapeDtypeStruct(a.shape, F32) for a in ws] * 3)(*gs, *ws, *ms, *vs)
    return out[:n], out[n:2 * n], out[2 * n:]


def _layout(shapes, row_align, total_align):
    lay, off = {}, 0
    for name, shape in shapes.items():
        size = int(np.prod(shape))
        rows = -(-size // LANES)
        rows = -(-rows // row_align) * row_align
        lay[name] = (off, rows, size, tuple(shape))
        off += rows
    return lay, -(-off // total_align) * total_align


def _pack(arrs, lay, total, dtype, lead=()):
    parts = []
    nl = len(lead)
    for name, (off, rows, size, shape) in lay.items():
        flat = arrs[name].astype(dtype).reshape(*lead, size)
        parts.append(jnp.pad(flat, [(0, 0)] * nl + [(0, rows * LANES - size)]).reshape(*lead, rows, LANES))
    used = sum(v[1] for v in lay.values())
    if total > used:
        parts.append(jnp.zeros((*lead, total - used, LANES), dtype))
    return jnp.concatenate(parts, axis=nl)


def _unpack(buf, lay, lead=()):
    out = {}
    nl = len(lead)
    for name, (off, rows, size, shape) in lay.items():
        part = lax.slice_in_dim(buf, off, off + rows, axis=nl).reshape(*lead, rows * LANES)
        out[name] = lax.slice_in_dim(part, 0, size, axis=nl).reshape(*lead, *shape)
    return out


_SHARD_AXIS = {
    "norm_mix": None, "norm_ffn": None, "norm_final": None, "gla_w_in": 2, "gla_w_g2": 2, "gla_b_g2": None,
    "gla_norm": None, "gla_w_out": 1, "cv_w_in": 2, "cv_b_in": 1, "cv_w_dw": 2, "cv_b_dw": 1, "cv_ln_g": 1,
    "cv_ln_b": 1, "cv_w_out": 1, "cv_b_out": 1, "sg_w_in": 2, "sg_b_in": 1, "sg_ln_g": 1, "sg_ln_b": 1, "sg_w_s": None,
    "sg_b_s": None, "sg_w_out": 1, "sg_b_out": 1, "hg_w_in": 2, "hg_lb_table": None, "hg_norm": None, "hg_w_out": 1,
    "ffn_w_up": 2, "ffn_w_dw": 2, "ffn_w_down": 1,
}
_MATMUL_WEIGHTS = ("gla_w_in", "gla_w_out", "cv_w_in", "cv_w_out", "sg_w_in", "sg_w_out", "hg_w_in", "hg_w_out",
                   "ffn_w_up", "ffn_w_down")
_NAMES = tuple(_SHARD_AXIS)


def kernel(x, norm_mix, norm_ffn, norm_final, gla_w_in, gla_w_g2, gla_b_g2, gla_norm, gla_w_out, cv_w_in, cv_b_in, cv_w_dw, cv_b_dw, cv_ln_g, cv_ln_b, cv_w_out, cv_b_out, sg_w_in, sg_b_in, sg_ln_g, sg_ln_b, sg_w_s, sg_b_s, sg_w_out, sg_b_out, hg_w_in, hg_lb_table, hg_norm, hg_w_out, ffn_w_up, ffn_w_dw, ffn_w_down, loss_target, m_norm_mix, m_norm_ffn, m_norm_final, m_gla_w_in, m_gla_w_g2, m_gla_b_g2, m_gla_norm, m_gla_w_out, m_cv_w_in, m_cv_b_in, m_cv_w_dw, m_cv_b_dw, m_cv_ln_g, m_cv_ln_b, m_cv_w_out, m_cv_b_out, m_sg_w_in, m_sg_b_in, m_sg_ln_g, m_sg_ln_b, m_sg_w_s, m_sg_b_s, m_sg_w_out, m_sg_b_out, m_hg_w_in, m_hg_lb_table, m_hg_norm, m_hg_w_out, m_ffn_w_up, m_ffn_w_dw, m_ffn_w_down, v_norm_mix, v_norm_ffn, v_norm_final, v_gla_w_in, v_gla_w_g2, v_gla_b_g2, v_gla_norm, v_gla_w_out, v_cv_w_in, v_cv_b_in, v_cv_w_dw, v_cv_b_dw, v_cv_ln_g, v_cv_ln_b, v_cv_w_out, v_cv_b_out, v_sg_w_in, v_sg_b_in, v_sg_ln_g, v_sg_ln_b, v_sg_w_s, v_sg_b_s, v_sg_w_out, v_sg_b_out, v_hg_w_in, v_hg_lb_table, v_hg_norm, v_hg_w_out, v_ffn_w_up, v_ffn_w_dw, v_ffn_w_down):
    local = dict(locals())
    wts = {n: local[n] for n in _NAMES}
    mom = {n: local["m_" + n] for n in _NAMES}
    var = {n: local["v_" + n] for n in _NAMES}
    small_all = [n for n in _NAMES if n not in _MATMUL_WEIGHTS]
    small_sharded = [n for n in small_all if _SHARD_AXIS[n] is not None]
    bsz, seq, d = x.shape
    depth = norm_mix.shape[0]

    stages = {}
    for layer in range(depth):
        kind = _MIXERS[layer % 4]
        stages[kind, layer] = {"w_in": (kind + "_w_in", layer // 4), "w_out": (kind + "_w_out", layer // 4)}
        stages["ffn", layer] = {"w_up": ("ffn_w_up", layer), "w_down": ("ffn_w_down", layer)}

    order = list(stages)
    shards = lambda stage: [wts[nm][idx].astype(BF16) for nm, idx in stages[stage].values()]
    gathers = {order[0]: _Side(shards(order[0]), False)}
    gathers[order[0]].lands = _gather_stage(gathers[order[0]].srcs, name="gather_first")
    scatters, waiting, down_gathers, down_scatters = {}, [], {}, {}

    def ride(kind, layer, forward):
        if not forward:
            return waiting.pop() if waiting else None
        at = order.index((kind, layer)) + 1
        if at == len(order):
            return None
        srcs = shards(order[at])
        if order[at][0] == "ffn":
            down_gathers[order[at][1]] = _Side(srcs[1:], False)
            srcs = srcs[:1]
        gathers[order[at]] = _Side(srcs, False)
        return gathers[order[at]]

    lay_sw, r_sw = _layout({n: wts[n].shape for n in small_sharded}, 8, 8)
    got_sw = _all_gather(_pack(wts, lay_sw, r_sw, F32), name="gather_small_weights")
    parts = _unpack(got_sw.reshape(N_DEV, r_sw, LANES), lay_sw, (N_DEV,))
    full_small = {n: wts[n] for n in small_all if _SHARD_AXIS[n] is None}
    for n in small_sharded:
        ax, shape = _SHARD_AXIS[n], wts[n].shape
        full_small[n] = jnp.moveaxis(parts[n], 0, ax).reshape(shape[:ax] + (N_DEV * shape[ax],) + shape[ax + 1:])

    def full_size(nm, land):
        _, r, c = land.shape
        if _SHARD_AXIS[nm] == 2:
            return land.transpose(1, 0, 2).reshape(r, N_DEV * c), land.transpose(0, 2, 1).reshape(N_DEV * c, r)
        return land.reshape(N_DEV * r, c), land.reshape(N_DEV * r, c).T

    def get_big(kind, layer):
        names = [nm for nm, _ in stages[kind, layer].values()]
        lands = gathers[kind, layer].lands
        if kind != "ffn" or layer not in down_gathers:
            return _oriented(kind, {key: full_size(nm, land) for key, nm, land in zip(stages[kind, layer], names, lands)})
        up, up_t = full_size(names[0], lands[0])
        f = up.shape[1] // 2

        def down_landed():
            down, down_t = full_size(names[1], down_gathers[layer].lands[0])
            return dict(w_down=down, w_down_t=down_t)

        return dict(w_up=up, w_up_t_gate=up_t[:f], w_up_t_val=up_t[f:], late=(down_gathers[layer], down_landed))

    def put_big(kind, layer, g):
        if kind == "ffn":
            k, f = g["w_up_gate"].shape
            halves = [g[key].reshape(k, N_DEV // 2, 2 * f // N_DEV) for key in ("w_up_gate", "w_up_val")]
            w_in = jnp.concatenate(halves, axis=1)
        else:
            w_in = jnp.concatenate([g["w_main"], g["w_glr"][:, :GLA_RANK]], axis=1) if kind == "gla" else g["w_in"]
            w_in = w_in.reshape(w_in.shape[0], N_DEV, w_in.shape[1] // N_DEV)
        sends = [w_in.transpose(1, 0, 2).astype(BF16)]
        if kind != "ffn":
            sends.append(row_slots(g["w_out"]))
        scatters[kind, layer] = _Side(sends, True)
        waiting.append(scatters[kind, layer])

    def row_slots(grad):
        return grad.reshape(N_DEV, grad.shape[0] // N_DEV, grad.shape[1]).astype(BF16)

    def put_early(layer, grad_w_down):
        down_scatters[layer] = _Side([row_slots(grad_w_down)], True)
        return down_scatters[layer]

    loss, dx, grads = _local_step(x.reshape(bsz * seq, d), loss_target.reshape(bsz * seq, d), _prep_small(full_small), seq,
                                  get_big, put_big, ride, put_early)
    loss = lax.psum(loss[0, 0], ("x", "y", "c"))

    gs = _small_grads(grads)
    small_repl = [n for n in small_all if _SHARD_AXIS[n] is None]
    lay_re, r_re = _layout({n: wts[n].shape for n in small_repl}, 8, 8)
    slots = {}
    for n in small_sharded:
        ax, shape = _SHARD_AXIS[n], wts[n].shape
        slots[n] = jnp.moveaxis(gs[n].reshape(shape[:ax] + (N_DEV, shape[ax]) + shape[ax + 1:]), ax, 0)
    sent = jnp.concatenate([_pack(gs, lay_re, r_re, F32), _pack(slots, lay_sw, r_sw, F32, (N_DEV,)).reshape(-1, LANES)])
    sum_re, sum_sh = _sum_small(_all_gather(sent, name="gather_small_grads"), r_re, r_sw, name="sum_small_grads")
    g_own = _unpack(sum_re, lay_re)
    g_own.update(_unpack(sum_sh, lay_sw))
    two_d = lambda a: a.reshape(-1, a.shape[-1])
    upd = _adamw_small(*[[two_d(src[n]) for n in small_all] for src in (g_own, wts, mom, var)], name="adamw_small")
    results = {n: [g_own[n]] + [part[i].reshape(wts[n].shape) for part in upd] for i, n in enumerate(small_all)}

    per_layer = {}
    for (kind, layer), side in scatters.items():
        lands = side.lands if side.lands is not None else _scatter_stage(side.srcs, name="scatter_last")
        if kind == "ffn":
            lands = list(lands) + down_scatters[layer].lands
        for (nm, idx), land in zip(stages[kind, layer].values(), lands):
            three_d = lambda a: a.reshape((a.shape[0],) + land.shape[1:])
            per_layer.setdefault(nm, {})[idx] = _adamw_big(land, three_d(wts[nm]), three_d(mom[nm]), three_d(var[nm]), idx,
                                                           name="adamw_" + nm)
    for nm, by_idx in per_layer.items():
        outs = [by_idx[i] for i in range(len(by_idx))]
        results[nm] = [(outs[0][j] if len(outs) == 1 else jnp.stack([o[j] for o in outs])).reshape(wts[nm].shape)
                       for j in range(4)]
    out = [loss, dx.reshape(bsz, seq, d)]
    for j in range(4):
        out += [results[n][j] for n in _NAMES]
    return tuple(out)
```

```python
import functools
import math

import jax
import jax.numpy as jnp
import numpy as np
from jax import lax
from jax.experimental import pallas as pl
from jax.experimental.pallas import tpu as pltpu

F32 = jnp.float32
BF16 = jnp.bfloat16
EPS = 1e-6
N_DEV = 8
LANES = 128
SUBLANES_BF16 = 16
HALO = 32
GLA_HEADS, GLA_RANK, GLA_GATE_NORM, GLA_CHUNK = 4, 16, 16.0, 64
SGU_CHUNK, SGU_GROUPS = 128, 8
HGRN_EXPAND, HGRN_CHUNK = 128, 64
CONV_WIDTH, FFN_CONV_WIDTH = 31, 3
ADAM_LR, ADAM_B1, ADAM_B2, ADAM_EPS, ADAM_WD, ADAM_STEP = 0.001, 0.9, 0.999, 1e-08, 0.01, 10
MESH = pl.DeviceIdType.MESH


def _sigmoid(x):
    return 0.5 * (jnp.tanh(0.5 * x) + 1.0)


def _silu(x):
    return x * _sigmoid(x)


def _log_sigmoid(x):
    return jnp.minimum(x, 0.0) - jnp.log(1.0 + jnp.exp(-jnp.abs(x)))


def _gelu(x):
    return 0.5 * x * (1.0 + jnp.tanh(math.sqrt(2.0 / math.pi) * (x + 0.044715 * (x * x * x))))


def _rms(x, g):
    return x * lax.rsqrt(jnp.mean(x * x, axis=-1, keepdims=True) + EPS) * g


def _layer_norm(x, g, b):
    xc = x - jnp.mean(x, axis=-1, keepdims=True)
    return xc * lax.rsqrt(jnp.mean(xc * xc, axis=-1, keepdims=True) + EPS) * g + b


def _dot_raw(a, b, dims):
    return lax.dot_general(a.astype(BF16), b.astype(BF16), (dims, ((), ())), preferred_element_type=F32)


@jax.custom_vjp
def _bdot(a, b):
    return _dot_raw(a, b, ((1,), (0,)))


@jax.custom_vjp
def _bdot_nt(a, b):
    return _dot_raw(a, b, ((1,), (1,)))


@jax.custom_vjp
def _bdot_tn(a, b):
    return _dot_raw(a, b, ((0,), (0,)))


_bdot.defvjp(lambda a, b: (_bdot(a, b), (a, b)), lambda r, g: (_bdot_nt(g, r[1]), _bdot_tn(r[0], g)))
_bdot_nt.defvjp(lambda a, b: (_bdot_nt(a, b), (a, b)), lambda r, g: (_bdot(g, r[1]), _bdot_tn(g, r[0])))
_bdot_tn.defvjp(lambda a, b: (_bdot_tn(a, b), (a, b)), lambda r, g: (_bdot_nt(r[1], g), _bdot(r[0], g)))


def _scan_rows(x, reverse):
    n = x.shape[0]
    row = lax.broadcasted_iota(jnp.int32, x.shape, 0)
    step = 1
    while step < n:
        if reverse:
            x = x + jnp.where(row < n - step, pltpu.roll(x, n - step, 0), 0.0)
        else:
            x = x + jnp.where(row >= step, pltpu.roll(x, step, 0), 0.0)
        step *= 2
    return x


@jax.custom_vjp
def _cumsum_rows(x):
    return _scan_rows(x, False)


_cumsum_rows.defvjp(lambda x: (_scan_rows(x, False), None), lambda _, g: (_scan_rows(g, True),))


def _divisor_tile(n, cap, unit):
    if n <= cap:
        return n
    best = None
    for t in range(unit, cap + 1, unit):
        if n % t == 0:
            best = t
    assert best is not None, (n, cap, unit)
    return best


def _const_map(nd):
    return lambda *_: (0,) * nd


class _Side:
    def __init__(self, srcs, scatter, parts=()):
        self.srcs, self.scatter, self.lands, self.parts = list(srcs), scatter, None, list(parts)

    @staticmethod
    def join(sides):
        sides = [s for s in sides if s is not None]
        if len(sides) < 2:
            return sides[0] if sides else None
        assert len({s.scatter for s in sides}) == 1
        return _Side([a for s in sides for a in s.srcs], sides[0].scatter, sides)

    def landed(self, lands):
        self.lands = list(lands)
        at = 0
        for part in self.parts:
            part.landed(self.lands[at:at + len(part.srcs)])
            at += len(part.srcs)


def _pallas(body, side, *, name, grid, in_specs, out_specs, out_shape, scratch_shapes=(), semantics):
    if side is None:
        return pl.pallas_call(body, name=name, grid=grid, in_specs=in_specs, out_specs=out_specs, out_shape=out_shape,
                              scratch_shapes=list(scratch_shapes),
                              compiler_params=pltpu.CompilerParams(dimension_semantics=semantics))
    single = not isinstance(out_shape, (list, tuple))
    out_specs, out_shape = ([out_specs], [out_shape]) if single else (list(out_specs), list(out_shape))
    n, n_in, n_out, n_scr = len(side.srcs), len(in_specs), len(out_shape), len(scratch_shapes)
    lands = [jax.ShapeDtypeStruct((N_DEV,) + (s.shape[1:] if side.scatter else s.shape), s.dtype) for s in side.srcs]

    def body2(*refs):
        x_refs, land_refs = refs[n_in:n_in + n], refs[n_in + n + n_out:n_in + 2 * n + n_out]
        send_sems, recv_sems, local_sems = refs[-3:]
        steps = [pl.program_id(a) for a in range(len(grid))]
        first = functools.reduce(jnp.logical_and, [s == 0 for s in steps])
        last = functools.reduce(jnp.logical_and, [s == g - 1 for s, g in zip(steps, grid)])

        def copies():
            mx, my, mc = lax.axis_index("x"), lax.axis_index("y"), lax.axis_index("c")
            me = 4 * mx + 2 * my + mc
            mine = [pltpu.make_async_copy(x_refs[i].at[me] if side.scatter else x_refs[i], land_refs[i].at[me],
                                          local_sems.at[i]) for i in range(n)]
            sends, recvs = [], []
            for k in range(1, N_DEV):
                px = 1 - mx if k & 4 else mx
                py = 1 - my if k & 2 else my
                pc = 1 - mc if k & 1 else mc
                peer = 4 * px + 2 * py + pc
                for i in range(n):
                    sems = dict(send_sem=send_sems.at[7 * i + k - 1], recv_sem=recv_sems.at[7 * i + k - 1],
                                device_id=(px, py, pc), device_id_type=MESH)
                    src = x_refs[i].at[peer] if side.scatter else x_refs[i]
                    sends.append(pltpu.make_async_remote_copy(src_ref=src, dst_ref=land_refs[i].at[me], **sems))
                    recvs.append(pltpu.make_async_remote_copy(src_ref=src, dst_ref=land_refs[i].at[peer], **sems))
            return mine, sends, recvs

        @pl.when(first)
        def _():
            mine, sends, _ = copies()
            for cp in mine + sends:
                cp.start()

        body(*refs[:n_in], *refs[n_in + n:n_in + n + n_out], *refs[n_in + 2 * n + n_out:n_in + 2 * n + n_out + n_scr])

        @pl.when(last)
        def _():
            mine, sends, recvs = copies()
            for cp in recvs:
                cp.wait_recv()
            for cp in sends:
                cp.wait_send()
            for cp in mine:
                cp.wait()

    any_space = pl.BlockSpec(memory_space=pl.ANY)
    call = pl.pallas_call(
        body2, name=name, grid=grid, in_specs=list(in_specs) + [any_space] * n, out_specs=out_specs + [any_space] * n,
        out_shape=out_shape + lands,
        scratch_shapes=list(scratch_shapes) + [pltpu.SemaphoreType.DMA((7 * n,)), pltpu.SemaphoreType.DMA((7 * n,)),
                                               pltpu.SemaphoreType.DMA((n,))],
        compiler_params=pltpu.CompilerParams(dimension_semantics=("arbitrary",) * len(grid)))

    def run(*args):
        res = call(*args, *side.srcs)
        side.landed(res[n_out:])
        return res[0] if single else res[:n_out]

    return run


def _mm(a, b, *, second=None, add=None, bias=None, out_dtype=F32, name, side=None):
    m, k = a.shape
    n = b.shape[1]
    assert b.shape[0] == k
    pairs = [(a, b)] + ([second] if second is not None else [])
    k_all = sum(p[0].shape[1] for p in pairs)
    tn = _divisor_tile(n, max(LANES, min(1408, (6 << 20) // (2 * k_all) // LANES * LANES)), LANES)
    tm = _divisor_tile(m, max(256, min(1024, (4 << 20) // (a.dtype.itemsize * k_all) // 256 * 256)), 8)
    has_bias, has_add = bias is not None, add is not None

    def body(*refs):
        o_ref = refs[-1]
        acc = jnp.dot(refs[0][...].astype(BF16), refs[1][...], preferred_element_type=F32)
        pos = 2
        if second is not None:
            acc = acc + jnp.dot(refs[2][...].astype(BF16), refs[3][...], preferred_element_type=F32)
            pos = 4
        if has_bias:
            acc = acc + refs[pos][...]
            pos += 1
        if has_add:
            acc = acc + refs[pos][...].astype(F32)
        o_ref[...] = acc.astype(o_ref.dtype)

    in_specs, args = [], []
    for a_p, b_p in pairs:
        in_specs += [pl.BlockSpec((tm, a_p.shape[1]), lambda i, j: (i, 0)), pl.BlockSpec((a_p.shape[1], tn), lambda i, j: (0, j))]
        args += [a_p, b_p]
    if has_bias:
        in_specs.append(pl.BlockSpec((1, tn), lambda i, j: (0, j)))
        args.append(bias)
    if has_add:
        in_specs.append(pl.BlockSpec((tm, tn), lambda i, j: (i, j)))
        args.append(add)
    return _pallas(
        body, side, name=name, grid=(m // tm, n // tn), in_specs=in_specs,
        out_specs=pl.BlockSpec((tm, tn), lambda i, j: (i, j)),
        out_shape=jax.ShapeDtypeStruct((m, n), out_dtype),
        semantics=("parallel", "parallel"),
    )(*args)


def _mm_tn(a, g, *, name):
    m, k = a.shape
    m2, n = g.shape
    assert m == m2
    tk = _divisor_tile(k, 1408, LANES)
    tn = _divisor_tile(n, 1408, LANES)
    tm = _divisor_tile(m, 1024, 8)

    def body(a_ref, g_ref, o_ref):
        @pl.when(pl.program_id(2) == 0)
        def _():
            o_ref[...] = jnp.zeros_like(o_ref)

        o_ref[...] += _dot_raw(a_ref[...], g_ref[...], ((0,), (0,)))

    return pl.pallas_call(
        body, name=name, grid=(k // tk, n // tn, m // tm),
        in_specs=[pl.BlockSpec((tm, tk), lambda i, j, t: (t, i)), pl.BlockSpec((tm, tn), lambda i, j, t: (t, j))],
        out_specs=pl.BlockSpec((tk, tn), lambda i, j, t: (i, j)),
        out_shape=jax.ShapeDtypeStruct((k, n), F32),
        compiler_params=pltpu.CompilerParams(dimension_semantics=("parallel", "parallel", "arbitrary")),
    )(a, g)


def _tile_call(name, fn, tiled, params, out_tiled, out_acc, tile, side=None):
    tiled = [t if isinstance(t, tuple) else (t, t.shape[1], 0) for t in tiled]
    t_rows = tiled[0][0].shape[0]
    tile = min(tile, t_rows)
    assert t_rows % tile == 0
    n_t, n_p, n_o = len(tiled), len(params), len(out_tiled)

    def body(*refs):
        vals = [r[...] for r in refs[: n_t + n_p]]
        touts, aouts = fn(*vals)
        for r, v in zip(refs[n_t + n_p: n_t + n_p + n_o], touts):
            r[...] = v.astype(r.dtype)
        acc_refs = refs[n_t + n_p + n_o:]
        if acc_refs:
            @pl.when(pl.program_id(0) == 0)
            def _():
                for r in acc_refs:
                    r[...] = jnp.zeros_like(r)

            for r, v in zip(acc_refs, aouts):
                r[...] += v

    in_specs = [pl.BlockSpec((tile, w), lambda i, cb=cb: (i, cb)) for _, w, cb in tiled]
    in_specs += [pl.BlockSpec(p.shape, _const_map(p.ndim)) for p in params]
    out_specs = [pl.BlockSpec((tile, w), lambda i: (i, 0)) for w, _ in out_tiled]
    out_specs += [pl.BlockSpec(s, _const_map(len(s))) for s in out_acc]
    out_shape = [jax.ShapeDtypeStruct((t_rows, w), dt) for w, dt in out_tiled]
    out_shape += [jax.ShapeDtypeStruct(s, F32) for s in out_acc]
    res = _pallas(
        body, side, name=name, grid=(t_rows // tile,), in_specs=in_specs, out_specs=out_specs, out_shape=out_shape,
        semantics=("arbitrary" if out_acc else "parallel",),
    )(*[t[0] for t in tiled], *params)
    return res[:n_o], res[n_o:]


def _rms_fwd(x, g, name):
    (h,), _ = _tile_call(name, lambda xv, gv: ([_rms(xv, gv)], []), [x], [g], [(x.shape[1], BF16)], [], 512)
    return h


def _rms_bwd(x, g, dh, dres, name):
    def fn(xv, dhv, drv, gv):
        _, vjp = jax.vjp(_rms, xv, gv)
        dx, dg = vjp(dhv.astype(F32))
        return [drv + dx], [dg]

    (dx,), (dg,) = _tile_call(name, fn, [x, dh, dres], [g], [(x.shape[1], F32)], [g.shape], 512)
    return dx, dg


def _colsum(x, name):
    _, (s,) = _tile_call(name, lambda xv: ([], [jnp.sum(xv.astype(F32), axis=0, keepdims=True)]), [x], [], [],
                         [(1, x.shape[1])], 512)
    return s


def _seq_flags(i, tiles_per_seq):
    pos = i % tiles_per_seq
    return pos == 0, pos == tiles_per_seq - 1


def _dwconv_fwd(x, w, b, seq, name, side=None):
    t_rows, ch = x.shape
    kw = w.shape[0]
    tile = min(512, seq)
    cb = _divisor_tile(ch, 256, LANES)
    tps, hb = seq // tile, tile // HALO

    def body(x_ref, halo_ref, w_ref, b_ref, y_ref, pad_ref):
        first, _ = _seq_flags(pl.program_id(0), tps)
        pad_ref[0:HALO, :] = jnp.where(first, 0.0, halo_ref[...])
        pad_ref[HALO:HALO + tile, :] = x_ref[...]
        for r0 in range(0, tile, HALO):
            acc = jnp.broadcast_to(b_ref[...], (HALO, cb))
            for k in range(kw):
                acc = acc + pad_ref[pl.ds(HALO - (kw - 1) + k + r0, HALO), :] * w_ref[k:k + 1, :]
            y_ref[pl.ds(r0, HALO), :] = acc

    return _pallas(
        body, side, name=name, grid=(t_rows // tile, ch // cb),
        in_specs=[pl.BlockSpec((tile, cb), lambda i, j: (i, j)),
                  pl.BlockSpec((HALO, cb), lambda i, j: (jnp.maximum(i * hb - 1, 0), j)),
                  pl.BlockSpec((kw, cb), lambda i, j: (0, j)), pl.BlockSpec((1, cb), lambda i, j: (0, j))],
        out_specs=pl.BlockSpec((tile, cb), lambda i, j: (i, j)),
        out_shape=jax.ShapeDtypeStruct((t_rows, ch), F32),
        scratch_shapes=[pltpu.VMEM((HALO + tile, cb), F32)],
        semantics=("parallel", "parallel"),
    )(x, x, w, b)


def _dwconv_bwd(x, dy, w, seq, name, side=None):
    t_rows, ch = x.shape
    kw = w.shape[0]
    tile = min(512, seq)
    cb = _divisor_tile(ch, 256, LANES)
    tps, hb, n_hb = seq // tile, tile // HALO, t_rows // HALO

    def body(x_ref, xh_ref, dy_ref, dyh_ref, w_ref, dx_ref, dw_ref, db_ref, xpad, dypad, sums):
        i = pl.program_id(1)
        first, last = _seq_flags(i, tps)

        @pl.when(i == 0)
        def _():
            sums[...] = jnp.zeros_like(sums)

        xpad[0:HALO, :] = jnp.where(first, 0.0, xh_ref[...])
        xpad[HALO:HALO + tile, :] = x_ref[...]
        dypad[0:tile, :] = dy_ref[...]
        dypad[tile:tile + HALO, :] = jnp.where(last, 0.0, dyh_ref[...])
        fold = lambda v: functools.reduce(jnp.add, [v[r:r + 8] for r in range(0, HALO, 8)])
        for r0 in range(0, tile, HALO):
            dyc = dy_ref[pl.ds(r0, HALO), :]
            acc = jnp.zeros((HALO, cb), F32)
            for k in range(kw):
                acc = acc + dypad[pl.ds(kw - 1 - k + r0, HALO), :] * w_ref[k:k + 1, :]
                sums[8 * k:8 * k + 8, :] += fold(dyc * xpad[pl.ds(HALO - (kw - 1) + k + r0, HALO), :])
            dx_ref[pl.ds(r0, HALO), :] = acc
            sums[8 * kw:8 * kw + 8, :] += fold(dyc)

        @pl.when(i == t_rows // tile - 1)
        def _():
            for k in range(kw):
                dw_ref[k:k + 1, :] = jnp.sum(sums[8 * k:8 * k + 8, :], axis=0, keepdims=True)
            db_ref[...] = jnp.sum(sums[8 * kw:8 * kw + 8, :], axis=0, keepdims=True)

    return _pallas(
        body, side, name=name, grid=(ch // cb, t_rows // tile),
        in_specs=[pl.BlockSpec((tile, cb), lambda j, i: (i, j)),
                  pl.BlockSpec((HALO, cb), lambda j, i: (jnp.maximum(i * hb - 1, 0), j)),
                  pl.BlockSpec((tile, cb), lambda j, i: (i, j)),
                  pl.BlockSpec((HALO, cb), lambda j, i: (jnp.minimum((i + 1) * hb, n_hb - 1), j)),
                  pl.BlockSpec((kw, cb), lambda j, i: (0, j))],
        out_specs=[pl.BlockSpec((tile, cb), lambda j, i: (i, j)), pl.BlockSpec((kw, cb), lambda j, i: (0, j)),
                   pl.BlockSpec((1, cb), lambda j, i: (0, j))],
        out_shape=[jax.ShapeDtypeStruct((t_rows, ch), F32), jax.ShapeDtypeStruct((kw, ch), F32),
                   jax.ShapeDtypeStruct((1, ch), F32)],
        scratch_shapes=[pltpu.VMEM((HALO + tile, cb), F32), pltpu.VMEM((tile + HALO, cb), F32),
                        pltpu.VMEM((8 * (kw + 1), cb), F32)],
        semantics=("parallel", "arbitrary"),
    )(x, x, dy, dy, w)


_ROWS = SUBLANES_BF16


def _lane_chunks(width, cap=6 * LANES):
    return [slice(c0, min(c0 + cap, width)) for c0 in range(0, width, cap)]


def _tap_rows(w_ref, cols):
    return [w_ref[k:k + 1, cols] for k in range(FFN_CONV_WIDTH)]


def _conv3_at(pad, taps, row, cols):
    z = pad[pl.ds(row, _ROWS), cols] * taps[2]
    z = z + pad[pl.ds(row - 1, _ROWS), cols] * taps[1]
    return z + pad[pl.ds(row - 2, _ROWS), cols] * taps[0]


def _ffn_mid_fwd(u, w, seq, name, side=None):
    t_rows, f2 = u.shape
    f = f2 // 2
    tile = min(256, seq)
    cb = _divisor_tile(f, 1408, LANES)
    nj, tps, hb, hl = f // cb, seq // tile, tile // SUBLANES_BF16, SUBLANES_BF16

    def body(ug_ref, uv_ref, hg_ref, hv_ref, wg_ref, wv_ref, a_ref, gpad, vpad):
        first, _ = _seq_flags(pl.program_id(0), tps)
        for t_ref, h_ref, pad in ((ug_ref, hg_ref, gpad), (uv_ref, hv_ref, vpad)):
            pad[0:hl, :] = jnp.where(first, 0.0, h_ref[...].astype(F32))
            pad[hl:hl + tile, :] = t_ref[...].astype(F32)
        for cols in _lane_chunks(cb):
            wg, wv = _tap_rows(wg_ref, cols), _tap_rows(wv_ref, cols)
            for r0 in range(0, tile, _ROWS):
                zg = _conv3_at(gpad, wg, hl + r0, cols)
                zv = _conv3_at(vpad, wv, hl + r0, cols)
                half = 0.5 * zg
                a_ref[pl.ds(r0, _ROWS), cols] = ((jnp.tanh(half) + 1.0) * half * zv).astype(a_ref.dtype)

    halo_map = lambda off: (lambda i, j: (jnp.maximum(i * hb - 1, 0), j + off))
    return _pallas(
        body, side, name=name, grid=(t_rows // tile, nj),
        in_specs=[pl.BlockSpec((tile, cb), lambda i, j: (i, j)), pl.BlockSpec((tile, cb), lambda i, j: (i, j + nj)),
                  pl.BlockSpec((hl, cb), halo_map(0)), pl.BlockSpec((hl, cb), halo_map(nj)),
                  pl.BlockSpec((3, cb), lambda i, j: (0, j)), pl.BlockSpec((3, cb), lambda i, j: (0, j + nj))],
        out_specs=pl.BlockSpec((tile, cb), lambda i, j: (i, j)),
        out_shape=jax.ShapeDtypeStruct((t_rows, f), BF16),
        scratch_shapes=[pltpu.VMEM((hl + tile, cb), F32), pltpu.VMEM((hl + tile, cb), F32)],
        semantics=("parallel", "parallel"),
    )(u, u, u, u, w, w)


def _ffn_mid_bwd(u, da, w, seq, name, side=None):
    t_rows, f2 = u.shape
    f = f2 // 2
    tile = min(256, seq)
    cb = _divisor_tile(f, 1408, LANES)
    hl = SUBLANES_BF16
    nj, tps, hb, n_hb, ext = f // cb, seq // tile, tile // hl, t_rows // hl, tile + hl

    def body(ug_ref, uv_ref, pg_ref, pv_ref, ng_ref, nv_ref, da_ref, dan_ref, wg_ref, wv_ref,
             dug_ref, duv_ref, dwg_ref, dwv_ref, gpad, vpad, dzg, dzv):
        i = pl.program_id(1)
        first, last = _seq_flags(i, tps)

        @pl.when(i == 0)
        def _():
            dwg_ref[...] = jnp.zeros_like(dwg_ref)
            dwv_ref[...] = jnp.zeros_like(dwv_ref)

        for t_ref, p_ref, n_ref, pad in ((ug_ref, pg_ref, ng_ref, gpad), (uv_ref, pv_ref, nv_ref, vpad)):
            pad[0:hl, :] = jnp.where(first, 0.0, p_ref[...].astype(F32))
            pad[hl:hl + tile, :] = t_ref[...].astype(F32)
            pad[hl + tile:hl + ext, :] = jnp.where(last, 0.0, n_ref[...].astype(F32))
        for cols in _lane_chunks(cb):
            wg, wv = _tap_rows(wg_ref, cols), _tap_rows(wv_ref, cols)
            for r0 in range(0, ext, _ROWS):
                zg = _conv3_at(gpad, wg, hl + r0, cols)
                zv = _conv3_at(vpad, wv, hl + r0, cols)
                if r0 < tile:
                    da = da_ref[pl.ds(r0, _ROWS), cols].astype(F32)
                else:
                    da = jnp.where(last, 0.0, dan_ref[:, cols].astype(F32))
                sg = _sigmoid(zg)
                dzg[pl.ds(r0, _ROWS), cols] = da * zv * (sg * (1.0 + zg * (1.0 - sg)))
                dzv[pl.ds(r0, _ROWS), cols] = da * (zg * sg)
        for dz, w_ref, pad, du_ref, dw_ref in ((dzg, wg_ref, gpad, dug_ref, dwg_ref), (dzv, wv_ref, vpad, duv_ref, dwv_ref)):
            for cols in _lane_chunks(cb):
                taps = _tap_rows(w_ref, cols)
                width = cols.stop - cols.start
                acc = [jnp.zeros((8, width), F32) for _ in range(FFN_CONV_WIDTH)]
                for r0 in range(0, tile, _ROWS):
                    d0 = dz[pl.ds(r0, _ROWS), cols]
                    du = dz[pl.ds(r0 + 2, _ROWS), cols] * taps[0] + dz[pl.ds(r0 + 1, _ROWS), cols] * taps[1] + d0 * taps[2]
                    du_ref[pl.ds(r0, _ROWS), cols] = du.astype(du_ref.dtype)
                    for k in range(FFN_CONV_WIDTH):
                        prod = d0 * pad[pl.ds(hl - 2 + k + r0, _ROWS), cols]
                        acc[k] = acc[k] + prod[0:8] + prod[8:16]
                for k in range(FFN_CONV_WIDTH):
                    dw_ref[k:k + 1, cols] += jnp.sum(acc[k], axis=0, keepdims=True)

    prev_map = lambda off: (lambda j, i: (jnp.maximum(i * hb - 1, 0), j + off))
    next_map = lambda off: (lambda j, i: (jnp.minimum((i + 1) * hb, n_hb - 1), j + off))
    tile_spec = lambda off: pl.BlockSpec((tile, cb), lambda j, i: (i, j + off))
    w_spec = lambda off: pl.BlockSpec((3, cb), lambda j, i: (0, j + off))
    return _pallas(
        body, side, name=name, grid=(nj, t_rows // tile),
        in_specs=[tile_spec(0), tile_spec(nj), pl.BlockSpec((hl, cb), prev_map(0)), pl.BlockSpec((hl, cb), prev_map(nj)),
                  pl.BlockSpec((hl, cb), next_map(0)), pl.BlockSpec((hl, cb), next_map(nj)),
                  tile_spec(0), pl.BlockSpec((hl, cb), next_map(0)), w_spec(0), w_spec(nj)],
        out_specs=[tile_spec(0), tile_spec(0), w_spec(0), w_spec(0)],
        out_shape=[jax.ShapeDtypeStruct((t_rows, f), BF16), jax.ShapeDtypeStruct((t_rows, f), BF16),
                   jax.ShapeDtypeStruct((3, f), F32), jax.ShapeDtypeStruct((3, f), F32)],
        scratch_shapes=[pltpu.VMEM((hl + ext, cb), F32), pltpu.VMEM((hl + ext, cb), F32),
                        pltpu.VMEM((ext, cb), F32), pltpu.VMEM((ext, cb), F32)],
        semantics=("parallel", "arbitrary"),
    )(u, u, u, u, u, u, da, da, w, w)


def _gla_chunk(q, k, v, lg, st, *, scale, chunk):
    row = lax.broadcasted_iota(jnp.int32, (chunk, chunk), 0)
    col = lax.broadcasted_iota(jnp.int32, (chunk, chunk), 1)
    causal = col <= row
    b = _cumsum_rows(lg)
    upto_mid = lax.broadcasted_iota(jnp.int32, lg.shape, 0) <= chunk // 2
    b_mid = jnp.sum(jnp.where(upto_mid, lg, 0.0), axis=0, keepdims=True)
    b_last = jnp.sum(lg, axis=0, keepdims=True)
    qs = q * scale
    scores = _bdot_nt(qs * jnp.exp(b - b_mid), k * jnp.exp(b_mid - b))
    o = _bdot(jnp.where(causal, scores, 0.0), v)
    o = o + _bdot_nt(qs * jnp.exp(b), st)
    st_new = st * jnp.exp(b_last) + _bdot_tn(v, k * jnp.exp(b_last - b))
    return o, st_new


_CHUNKS_PER_STEP = 2


def _gla_specs(specs, rows, n_blocks, reverse):
    if reverse:
        row = lambda bi, ci: bi * n_blocks + (n_blocks - 1 - ci)
    else:
        row = lambda bi, ci: bi * n_blocks + ci
    return [pl.BlockSpec((rows, w), lambda bi, ci, cb=cb: (row(bi, ci), cb)) for _, w, cb in specs], row


def _gla_fwd(q, k, v, lg, *, heads, dk, dv, scale, chunk, seq, name, side=None):
    t_rows = q[0].shape[0]
    per = _CHUNKS_PER_STEP if (seq // chunk) % _CHUNKS_PER_STEP == 0 else 1
    n_blocks = seq // (per * chunk)
    fn = functools.partial(_gla_chunk, scale=scale, chunk=chunk)

    def body(q_ref, k_ref, v_ref, lg_ref, o_ref, sts_ref, st_ref):
        @pl.when(pl.program_id(1) == 0)
        def _():
            st_ref[...] = jnp.zeros_like(st_ref)

        ks = [slice(h * dk, (h + 1) * dk) for h in range(heads)]
        vs = [slice(h * dv, (h + 1) * dv) for h in range(heads)]
        st = [st_ref[vs[h], :] for h in range(heads)]
        for s in range(per):
            rows = pl.ds(s * chunk, chunk)
            for h in range(heads):
                sts_ref[s, vs[h], :] = st[h]
                o, st[h] = fn(q_ref[rows, ks[h]].astype(F32), k_ref[rows, ks[h]].astype(F32),
                              v_ref[rows, vs[h]].astype(F32), lg_ref[rows, ks[h]], st[h])
                o_ref[rows, vs[h]] = o
        for h in range(heads):
            st_ref[vs[h], :] = st[h]

    in_specs, row = _gla_specs([q, k, v, lg], per * chunk, n_blocks, False)
    return _pallas(
        body, side, name=name, grid=(t_rows // seq, n_blocks), in_specs=in_specs,
        out_specs=[pl.BlockSpec((per * chunk, heads * dv), lambda bi, ci: (row(bi, ci), 0)),
                   pl.BlockSpec((per, heads * dv, dk), lambda bi, ci: (row(bi, ci), 0, 0))],
        out_shape=[jax.ShapeDtypeStruct((t_rows, heads * dv), F32),
                   jax.ShapeDtypeStruct((t_rows // chunk, heads * dv, dk), F32)],
        scratch_shapes=[pltpu.VMEM((heads * dv, dk), F32)],
        semantics=("arbitrary", "arbitrary"),
    )(q[0], k[0], v[0], lg[0])


def _gla_bwd(q, k, v, lg, states, do, *, heads, dk, dv, scale, chunk, seq, out_dtypes, name, side=None):
    t_rows = q[0].shape[0]
    per = _CHUNKS_PER_STEP if (seq // chunk) % _CHUNKS_PER_STEP == 0 else 1
    n_blocks = seq // (per * chunk)
    fn = functools.partial(_gla_chunk, scale=scale, chunk=chunk)

    def body(q_ref, k_ref, v_ref, lg_ref, do_ref, sts_ref, dq_ref, dk_ref, dv_ref, dlg_ref, dst_ref):
        @pl.when(pl.program_id(1) == 0)
        def _():
            dst_ref[...] = jnp.zeros_like(dst_ref)

        ks = [slice(h * dk, (h + 1) * dk) for h in range(heads)]
        vs = [slice(h * dv, (h + 1) * dv) for h in range(heads)]
        dst = [dst_ref[vs[h], :] for h in range(heads)]
        for s in reversed(range(per)):
            rows = pl.ds(s * chunk, chunk)
            for h in range(heads):
                _, vjp = jax.vjp(fn, q_ref[rows, ks[h]].astype(F32), k_ref[rows, ks[h]].astype(F32),
                                 v_ref[rows, vs[h]].astype(F32), lg_ref[rows, ks[h]], sts_ref[s, vs[h], :])
                dq, dkk, dvv, dlg, dst[h] = vjp((do_ref[rows, vs[h]].astype(F32), dst[h]))
                dq_ref[rows, ks[h]] = dq.astype(dq_ref.dtype)
                dk_ref[rows, ks[h]] = dkk.astype(dk_ref.dtype)
                dv_ref[rows, vs[h]] = dvv.astype(dv_ref.dtype)
                dlg_ref[rows, ks[h]] = dlg
        for h in range(heads):
            dst_ref[vs[h], :] = dst[h]

    do_view = (do, heads * dv, 0)
    in_specs, row = _gla_specs([q, k, v, lg, do_view], per * chunk, n_blocks, True)
    in_specs.append(pl.BlockSpec((per, heads * dv, dk), lambda bi, ci: (row(bi, ci), 0, 0)))
    wide = lambda w: pl.BlockSpec((per * chunk, w), lambda bi, ci: (row(bi, ci), 0))
    return _pallas(
        body, side, name=name, grid=(t_rows // seq, n_blocks), in_specs=in_specs,
        out_specs=[wide(heads * dk), wide(heads * dk), wide(heads * dv), wide(heads * dk)],
        out_shape=[jax.ShapeDtypeStruct((t_rows, heads * dk), out_dtypes[0]),
                   jax.ShapeDtypeStruct((t_rows, heads * dk), out_dtypes[1]),
                   jax.ShapeDtypeStruct((t_rows, heads * dv), out_dtypes[2]),
                   jax.ShapeDtypeStruct((t_rows, heads * dk), F32)],
        scratch_shapes=[pltpu.VMEM((heads * dv, dk), F32)],
        semantics=("arbitrary", "arbitrary"),
    )(q[0], k[0], v[0], lg[0], do, states)


def _head_rms_gate(o, r, g, heads):
    d = o.shape[1] // heads
    parts = [_rms(o[:, h * d:(h + 1) * d], g) for h in range(heads)]
    return jnp.concatenate(parts, axis=1) * _silu(r)


def _gla_gate(glr, w_g2p, b_g2):
    return _log_sigmoid(_bdot(glr, w_g2p) + b_g2) * (1.0 / GLA_GATE_NORM)


def _glu(a, gate, b_in):
    d = a.shape[1]
    return (a + b_in[:, :d]) * _sigmoid(gate + b_in[:, d:])


def _ln_silu(y, g, b):
    return _silu(_layer_norm(y, g, b))


def _sgu(pre, b_in, ln_g, ln_b, w_s, b_st):
    d = pre.shape[1] // 2
    gd = d // SGU_GROUPS
    uv = _gelu(pre + b_in)
    u, v = uv[:, :d], _layer_norm(uv[:, d:], ln_g, ln_b)
    row = lax.broadcasted_iota(jnp.int32, (SGU_CHUNK, SGU_CHUNK), 0)
    col = lax.broadcasted_iota(jnp.int32, (SGU_CHUNK, SGU_CHUNK), 1)
    lane = lax.broadcasted_iota(jnp.int32, b_st.shape, 1)
    rows = []
    for c in range(pre.shape[0] // SGU_CHUNK):
        rs = slice(c * SGU_CHUNK, (c + 1) * SGU_CHUNK)
        parts = []
        for g in range(SGU_GROUPS):
            wg = jnp.where(col <= row, w_s[g], 0.0)
            bias = jnp.sum(jnp.where(lane == g, b_st, 0.0), axis=1, keepdims=True)
            parts.append(_bdot(wg, v[rs, g * gd:(g + 1) * gd]) + bias)
        rows.append(jnp.concatenate(parts, axis=1))
    s = rows[0] if len(rows) == 1 else jnp.concatenate(rows, axis=0)
    return u * s


def _hgrn_pre(q, f, table, layer):
    t = table - jnp.max(table, axis=0, keepdims=True)
    e = jnp.exp(t)
    sm = e / jnp.sum(e, axis=0, keepdims=True)
    rows = lax.broadcasted_iota(jnp.int32, table.shape, 0)
    lb = jnp.sum(jnp.where((rows >= 1) & (rows <= layer), sm, 0.0), axis=0, keepdims=True)
    sf = _sigmoid(f)
    return _silu(q), (1.0 - lb) * (1.0 - sf), jnp.log(lb + (1.0 - lb) * sf)


def _ffn_fwd(x, w, seq, sv):
    sv["h2"] = _rms_fwd(x, w["norm"], "ffn_norm")
    late = w.get("late")
    sv["u"] = _mm(sv["h2"], w["w_up"], out_dtype=BF16, name="ffn_up", side=late[0] if late else None)
    sv["a"] = _ffn_mid_fwd(sv["u"], w["w_dw"], seq, "ffn_mid", sv.pop("side", None))
    if late:
        sv["late_w"] = late[1]()
        w = dict(w, **sv["late_w"])
    return _mm(sv["a"], w["w_down"], add=x, name="ffn_down")


def _ffn_bwd(x, dy, w, seq, sv):
    g = {}
    da = _mm(dy, w["w_down_t"], out_dtype=BF16, name="ffn_down_dx")
    g["w_down"] = _mm_tn(sv["a"], dy, name="ffn_down_dw")
    early = sv.pop("put_early", None)
    side = _Side.join([sv.pop("side", None), early(g["w_down"]) if early else None])
    dug, duv, dwg, dwv = _ffn_mid_bwd(sv["u"], da, w["w_dw"], seq, "ffn_mid_bwd", side)
    g["w_dw"] = jnp.concatenate([dwg, dwv], axis=1)
    g["w_up_gate"] = _mm_tn(sv["h2"], dug, name="ffn_up_dw")
    g["w_up_val"] = _mm_tn(sv["h2"], duv, name="ffn_up_dw")
    dh = _mm(dug, w["w_up_t_gate"], second=(duv, w["w_up_t_val"]), out_dtype=BF16, name="ffn_up_dx")
    dx, g["norm"] = _rms_bwd(x, w["norm"], dh, dy, "ffn_norm_bwd")
    return dx, g


def _gla_layer_fwd(x, h, w, seq, sv):
    d = x.shape[1]
    dkt = d // 2
    dk, dv = dkt // GLA_HEADS, d // GLA_HEADS
    proj = _mm(h, w["w_main"], out_dtype=F32, name="gla_in")
    glr = _mm(h, w["w_glr"], out_dtype=BF16, name="gla_in_g")
    (lg,), _ = _tile_call("gla_gate", lambda a, b, c: ([_gla_gate(a.astype(F32), b, c)], []), [glr],
                          [w["w_g2p"], w["b_g2"]], [(dkt, F32)], [], 512)
    q, k, v, r = (proj, dkt, 0), (proj, dkt, 1), (proj, d, 1), (proj, d, 2)
    o, states = _gla_fwd(q, k, v, (lg, dkt, 0), heads=GLA_HEADS, dk=dk, dv=dv, scale=dk ** -0.5, chunk=GLA_CHUNK,
                         seq=seq, name="gla_core", side=sv.pop("side", None))
    (o2,), _ = _tile_call("gla_post", lambda ov, rv, gv: ([_head_rms_gate(ov, rv.astype(F32), gv, GLA_HEADS)], []),
                          [o, r], [w["norm"]], [(d, BF16)], [], 256)
    sv.update(proj=proj, glr=glr, lg=lg, o=o, states=states, o2=o2)
    return _mm(o2, w["w_out"], add=x, name="mix_out")


def _gla_layer_bwd(h, dy, w, seq, sv):
    d = dy.shape[1]
    dkt = d // 2
    dk, dv = dkt // GLA_HEADS, d // GLA_HEADS
    proj, glr, lg, o = sv["proj"], sv["glr"], sv["lg"], sv["o"]
    g = {}
    do2 = _mm(dy, w["w_out_t"], out_dtype=F32, name="gla_out_dx")
    g["w_out"] = _mm_tn(sv["o2"], dy, name="mix_out_dw")

    def post_bwd(ov, rv, ctv, gv):
        _, vjp = jax.vjp(functools.partial(_head_rms_gate, heads=GLA_HEADS), ov, rv.astype(F32), gv)
        d_o, d_r, d_g = vjp(ctv.astype(F32))
        return [d_o, d_r], [d_g]

    (d_o, d_r), (g["norm"],) = _tile_call("gla_post_bwd", post_bwd, [o, (proj, d, 2), do2], [w["norm"]],
                                          [(d, F32), (d, BF16)], [w["norm"].shape], 256)
    q, k, v = (proj, dkt, 0), (proj, dkt, 1), (proj, d, 1)
    dq, dkk, dvv, dlg = _gla_bwd(q, k, v, (lg, dkt, 0), sv["states"], d_o, heads=GLA_HEADS, dk=dk, dv=dv,
                                 scale=dk ** -0.5, chunk=GLA_CHUNK, seq=seq, out_dtypes=(BF16, BF16, BF16),
                                 name="gla_core_bwd", side=sv.pop("side", None))

    def gate_bwd(glrv, ctv, wv, bv):
        _, vjp = jax.vjp(_gla_gate, glrv.astype(F32), wv, bv)
        d_glr, d_w, d_b = vjp(ctv)
        return [d_glr], [d_w, d_b]

    (dglr,), (g["w_g2p"], g["b_g2"]) = _tile_call("gla_gate_bwd", gate_bwd, [glr, dlg], [w["w_g2p"], w["b_g2"]],
                                                  [(LANES, BF16)], [w["w_g2p"].shape, w["b_g2"].shape], 512)
    dproj = jnp.concatenate([dq, dkk, dvv, d_r], axis=1)
    g["w_main"] = _mm_tn(h, dproj, name="gla_in_dw")
    g["w_glr"] = _mm_tn(h, dglr, name="gla_in_g_dw")
    dh = _mm(dproj, w["w_main_t"], second=(dglr, w["w_glr_t"]), out_dtype=BF16, name="gla_in_dx")
    return dh, g


def _cv_layer_fwd(x, h, w, seq, sv):
    d = x.shape[1]
    pre = _mm(h, w["w_in"], out_dtype=BF16, name="cv_in")
    (y1,), _ = _tile_call("cv_glu", lambda a, gt, b: ([_glu(a.astype(F32), gt.astype(F32), b)], []),
                          [(pre, d, 0), (pre, d, 1)], [w["b_in"]], [(d, F32)], [], 512)
    y2 = _dwconv_fwd(y1, w["w_dw"], w["b_dw"], seq, "cv_conv", sv.pop("side", None))
    (y3,), _ = _tile_call("cv_ln", lambda y, a, b: ([_ln_silu(y, a, b)], []), [y2], [w["ln_g"], w["ln_b"]],
                          [(d, BF16)], [], 512)
    sv.update(pre=pre, y1=y1, y2=y2, y3=y3)
    return _mm(y3, w["w_out"], bias=w["b_out"], add=x, name="mix_out_b")


def _cv_layer_bwd(h, dy, w, seq, sv):
    d = dy.shape[1]
    pre = sv["pre"]
    g = {}
    dy3 = _mm(dy, w["w_out_t"], out_dtype=BF16, name="mix_out_dx")
    g["w_out"] = _mm_tn(sv["y3"], dy, name="mix_out_dw")
    g["b_out"] = _colsum(dy, "bias_out_dw")

    def ln_bwd(yv, ctv, av, bv):
        _, vjp = jax.vjp(_ln_silu, yv, av, bv)
        d_y, d_a, d_b = vjp(ctv.astype(F32))
        return [d_y], [d_a, d_b]

    (dy2,), (g["ln_g"], g["ln_b"]) = _tile_call("cv_ln_bwd", ln_bwd, [sv["y2"], dy3], [w["ln_g"], w["ln_b"]],
                                                [(d, F32)], [w["ln_g"].shape, w["ln_b"].shape], 512)
    dy1, g["w_dw"], g["b_dw"] = _dwconv_bwd(sv["y1"], dy2, w["w_dw"], seq, "cv_conv_bwd", sv.pop("side", None))

    def glu_bwd(av, gv, ctv, bv):
        _, vjp = jax.vjp(_glu, av.astype(F32), gv.astype(F32), bv)
        d_a, d_g, d_b = vjp(ctv)
        return [jnp.concatenate([d_a, d_g], axis=1)], [d_b]

    (dpre,), (g["b_in"],) = _tile_call("cv_glu_bwd", glu_bwd, [(pre, d, 0), (pre, d, 1), dy1], [w["b_in"]],
                                       [(2 * d, BF16)], [w["b_in"].shape], 512)
    g["w_in"] = _mm_tn(h, dpre, name="in2_dw")
    dh = _mm(dpre, w["w_in_t"], out_dtype=BF16, name="in2_dx")
    return dh, g


def _sg_layer_fwd(x, h, w, seq, sv):
    d = x.shape[1]
    pre = _mm(h, w["w_in"], out_dtype=BF16, name="sg_in")
    pars = [w["b_in"], w["ln_g"], w["ln_b"], w["w_s"], w["b_st"]]
    (p,), _ = _tile_call("sg_gate", lambda pv, *ps: ([_sgu(pv.astype(F32), *ps)], []), [pre], pars, [(d, BF16)], [],
                         SGU_CHUNK, side=sv.pop("side", None))
    sv.update(pre=pre, p=p)
    return _mm(p, w["w_out"], bias=w["b_out"], add=x, name="mix_out_b")


def _sg_layer_bwd(h, dy, w, seq, sv):
    d = dy.shape[1]
    g = {}
    dp = _mm(dy, w["w_out_t"], out_dtype=BF16, name="mix_out_dx")
    g["w_out"] = _mm_tn(sv["p"], dy, name="mix_out_dw")
    g["b_out"] = _colsum(dy, "bias_out_dw")
    pars = [w["b_in"], w["ln_g"], w["ln_b"], w["w_s"], w["b_st"]]

    def sgu_bwd(pv, ctv, *ps):
        _, vjp = jax.vjp(_sgu, pv.astype(F32), *ps)
        grads = vjp(ctv.astype(F32))
        return [grads[0]], list(grads[1:])

    (dpre,), (g["b_in"], g["ln_g"], g["ln_b"], g["w_s"], g["b_st"]) = _tile_call(
        "sg_gate_bwd", sgu_bwd, [sv["pre"], dp], pars, [(2 * d, BF16)], [p.shape for p in pars], SGU_CHUNK,
        side=sv.pop("side", None))
    g["w_in"] = _mm_tn(h, dpre, name="in2_dw")
    dh = _mm(dpre, w["w_in_t"], out_dtype=BF16, name="in2_dx")
    return dh, g


def _hg_layer_fwd(x, h, w, seq, sv, layer):
    d = x.shape[1]
    heads = d // HGRN_EXPAND
    proj = _mm(h, w["w_in"], out_dtype=BF16, name="hg_in")
    pre = functools.partial(_hgrn_pre, layer=layer)
    (qs, kk, lg), _ = _tile_call("hg_pre", lambda qv, fv, tb: (list(pre(qv.astype(F32), fv.astype(F32), tb)), []),
                                 [(proj, d, 0), (proj, d, 1)], [w["lb_table"]], [(d, BF16), (d, F32), (d, F32)], [], 256)
    o, states = _gla_fwd((qs, d, 0), (kk, d, 0), (proj, d, 2), (lg, d, 0), heads=heads, dk=HGRN_EXPAND,
                         dv=HGRN_EXPAND, scale=1.0, chunk=HGRN_CHUNK, seq=seq, name="hg_core",
                         side=sv.pop("side", None))
    (o2,), _ = _tile_call("hg_post", lambda ov, gv, nv: ([_head_rms_gate(ov, gv.astype(F32), nv, heads)], []),
                          [o, (proj, d, 3)], [w["norm"]], [(d, BF16)], [], 256)
    sv.update(proj=proj, qs=qs, kk=kk, lg=lg, o=o, states=states, o2=o2)
    return _mm(o2, w["w_out"], add=x, name="mix_out")


def _hg_layer_bwd(h, dy, w, seq, sv, layer):
    d = dy.shape[1]
    heads = d // HGRN_EXPAND
    proj = sv["proj"]
    g = {}
    do2 = _mm(dy, w["w_out_t"], out_dtype=BF16, name="mix_out_dx")
    g["w_out"] = _mm_tn(sv["o2"], dy, name="mix_out_dw")

    def post_bwd(ov, gv, ctv, nv):
        _, vjp = jax.vjp(functools.partial(_head_rms_gate, heads=heads), ov, gv.astype(F32), nv)
        d_o, d_g, d_n = vjp(ctv.astype(F32))
        return [d_o, d_g], [d_n]

    (d_o, d_gate), (g["norm"],) = _tile_call("hg_post_bwd", post_bwd, [sv["o"], (proj, d, 3), do2], [w["norm"]],
                                             [(d, F32), (d, BF16)], [w["norm"].shape], 256)
    dqs, dkk, di, dlg = _gla_bwd((sv["qs"], d, 0), (sv["kk"], d, 0), (proj, d, 2), (sv["lg"], d, 0), sv["states"], d_o,
                                 heads=heads, dk=HGRN_EXPAND, dv=HGRN_EXPAND, scale=1.0, chunk=HGRN_CHUNK, seq=seq,
                                 out_dtypes=(F32, F32, BF16), name="hg_core_bwd", side=sv.pop("side", None))

    def pre_bwd(qv, fv, c1, c2, c3, tb):
        _, vjp = jax.vjp(functools.partial(_hgrn_pre, layer=layer), qv.astype(F32), fv.astype(F32), tb)
        d_q, d_f, d_t = vjp((c1, c2, c3))
        return [jnp.concatenate([d_q, d_f], axis=1)], [d_t]

    (dqf,), (g["lb_table"],) = _tile_call("hg_pre_bwd", pre_bwd, [(proj, d, 0), (proj, d, 1), dqs, dkk, dlg],
                                          [w["lb_table"]], [(2 * d, BF16)], [w["lb_table"].shape], 256)
    dproj = jnp.concatenate([dqf, di, d_gate], axis=1)
    g["w_in"] = _mm_tn(h, dproj, name="hg_in_dw")
    dh = _mm(dproj, w["w_in_t"], out_dtype=BF16, name="hg_in_dx")
    return dh, g


_MIXERS = ("gla", "cv", "sg", "hg")


_BIG_KEYS = {"gla": ("w_main", "w_glr", "w_out"), "cv": ("w_in", "w_out"), "sg": ("w_in", "w_out"), "hg": ("w_in", "w_out"),
             "ffn": ("w_up_gate", "w_up_val", "w_down")}


def _local_step(x, target, w, seq, get_big, put_big, ride=lambda kind, layer, forward: None, put_early=None):
    depth = w["norm_mix"].shape[0]
    d = x.shape[1]
    saved, big = [], {}
    for layer in range(depth):
        mixer = _MIXERS[layer % 4]
        sv = {"x_in": x, "side": ride(mixer, layer, True)}
        sv["h"] = _rms_fwd(x, w["norm_mix"][layer:layer + 1], "mix_norm")
        big[mixer, layer] = get_big(mixer, layer)
        wm = dict(w[mixer], **big[mixer, layer])
        if mixer == "gla":
            x = _gla_layer_fwd(x, sv["h"], wm, seq, sv)
        elif mixer == "cv":
            x = _cv_layer_fwd(x, sv["h"], wm, seq, sv)
        elif mixer == "sg":
            x = _sg_layer_fwd(x, sv["h"], wm, seq, sv)
        else:
            x = _hg_layer_fwd(x, sv["h"], wm, seq, sv, layer)
        sv["x_mid"] = x
        big["ffn", layer] = get_big("ffn", layer)
        sv["ffn"] = {"side": ride("ffn", layer, True)}
        wf = dict(w["ffn"][layer], norm=w["norm_ffn"][layer:layer + 1], **big["ffn", layer])
        x = _ffn_fwd(x, wf, seq, sv["ffn"])
        big["ffn", layer].pop("late", None)
        big["ffn", layer].update(sv["ffn"].pop("late_w", {}))
        saved.append(sv)

    def head(xv, tv, gv):
        y, vjp = jax.vjp(_rms, xv, gv)
        err = y - tv
        dx, dg = vjp(err * (1.0 / d))
        part = 0.5 * jnp.sum(jnp.mean(err * err, axis=-1, keepdims=True), axis=0, keepdims=True)
        return [dx], [jnp.broadcast_to(part, (1, LANES)), dg]

    (dx,), (loss, g_final) = _tile_call("loss_head", head, [x, target], [w["norm_final"]], [(d, F32)],
                                        [(1, LANES), (1, d)], 512)
    grads = {"norm_final": g_final, "norm_mix": [None] * depth, "norm_ffn": [None] * depth, "ffn": [None] * depth}
    for layer in reversed(range(depth)):
        mixer = _MIXERS[layer % 4]
        sv = saved[layer]
        wf = dict(w["ffn"][layer], norm=w["norm_ffn"][layer:layer + 1], **big["ffn", layer])
        sv["ffn"]["side"] = ride("ffn", layer, False)
        if put_early is not None:
            sv["ffn"]["put_early"] = functools.partial(put_early, layer)
        dx, gf = _ffn_bwd(sv["x_mid"], dx, wf, seq, sv["ffn"])
        put_big("ffn", layer, {k: gf.pop(k) for k in _BIG_KEYS["ffn"]})
        sv["side"] = ride(mixer, layer, False)
        grads["norm_ffn"][layer] = gf.pop("norm")
        grads["ffn"][layer] = gf
        wm = dict(w[mixer], **big[mixer, layer])
        if mixer == "gla":
            dh, gm = _gla_layer_bwd(sv["h"], dx, wm, seq, sv)
        elif mixer == "cv":
            dh, gm = _cv_layer_bwd(sv["h"], dx, wm, seq, sv)
        elif mixer == "sg":
            dh, gm = _sg_layer_bwd(sv["h"], dx, wm, seq, sv)
        else:
            dh, gm = _hg_layer_bwd(sv["h"], dx, wm, seq, sv, layer)
        put_big(mixer, layer, {k: gm.pop(k) for k in _BIG_KEYS[mixer]})
        grads[mixer] = gm
        dx, grads["norm_mix"][layer] = _rms_bwd(sv["x_in"], w["norm_mix"][layer:layer + 1], dh, dx, "mix_norm_bwd")
    return loss, dx, grads


def _prep_small(p):
    row = lambda a: a.reshape(1, -1).astype(F32)
    w = {"norm_mix": p["norm_mix"].astype(F32), "norm_ffn": p["norm_ffn"].astype(F32), "norm_final": row(p["norm_final"])}
    w["gla"] = dict(w_g2p=jnp.pad(p["gla_w_g2"][0].astype(F32), ((0, LANES - GLA_RANK), (0, 0))), b_g2=row(p["gla_b_g2"]),
                    norm=row(p["gla_norm"]))
    w["cv"] = dict(b_in=row(p["cv_b_in"]), w_dw=p["cv_w_dw"][0].astype(F32), b_dw=row(p["cv_b_dw"]), ln_g=row(p["cv_ln_g"]),
                   ln_b=row(p["cv_ln_b"]), b_out=row(p["cv_b_out"]))
    b_st = jnp.pad(p["sg_b_s"][0].astype(F32).T, ((0, 0), (0, LANES - SGU_GROUPS)))
    w["sg"] = dict(b_in=row(p["sg_b_in"]), ln_g=row(p["sg_ln_g"]), ln_b=row(p["sg_ln_b"]), w_s=p["sg_w_s"][0].astype(F32),
                   b_st=b_st, b_out=row(p["sg_b_out"]))
    w["hg"] = dict(lb_table=p["hg_lb_table"].astype(F32), norm=row(p["hg_norm"]))
    w["ffn"] = [dict(w_dw=p["ffn_w_dw"][layer].astype(F32)) for layer in range(p["ffn_w_dw"].shape[0])]
    return w


def _small_grads(g):
    gla, cv, sg, hg = g["gla"], g["cv"], g["sg"], g["hg"]
    return {
        "norm_mix": jnp.concatenate(g["norm_mix"], axis=0), "norm_ffn": jnp.concatenate(g["norm_ffn"], axis=0),
        "norm_final": g["norm_final"][0],
        "gla_w_g2": gla["w_g2p"][:GLA_RANK][None], "gla_b_g2": gla["b_g2"], "gla_norm": gla["norm"],
        "cv_b_in": cv["b_in"], "cv_w_dw": cv["w_dw"][None], "cv_b_dw": cv["b_dw"], "cv_ln_g": cv["ln_g"],
        "cv_ln_b": cv["ln_b"], "cv_b_out": cv["b_out"],
        "sg_b_in": sg["b_in"], "sg_ln_g": sg["ln_g"], "sg_ln_b": sg["ln_b"], "sg_w_s": sg["w_s"][None],
        "sg_b_s": sg["b_st"][:, :SGU_GROUPS].T[None], "sg_b_out": sg["b_out"],
        "hg_lb_table": hg["lb_table"], "hg_norm": hg["norm"],
        "ffn_w_dw": jnp.stack([f["w_dw"] for f in g["ffn"]]),
    }


def _oriented(kind, mats):
    if kind == "ffn":
        (up, up_t), (down, down_t) = mats["w_up"], mats["w_down"]
        f = down.shape[0]
        return dict(w_up=up, w_up_t_gate=up_t[:f], w_up_t_val=up_t[f:], w_down=down, w_down_t=down_t)
    (w_in, w_in_t), (w_out, w_out_t) = mats["w_in"], mats["w_out"]
    if kind != "gla":
        return dict(w_in=w_in, w_in_t=w_in_t, w_out=w_out, w_out_t=w_out_t)
    n_main = w_in.shape[1] - GLA_RANK
    return dict(w_main=w_in[:, :n_main], w_glr=jnp.pad(w_in[:, n_main:], ((0, 0), (0, LANES - GLA_RANK))),
                w_main_t=w_in_t[:n_main], w_glr_t=jnp.pad(w_in_t[n_main:], ((0, LANES - GLA_RANK), (0, 0))),
                w_out=w_out, w_out_t=w_out_t)


def _all_gather(x, *, name):
    m_per, n = x.shape

    def body(x_ref, out_ref, send_sems, recv_sems, local_sem):
        mx, my, mc = lax.axis_index("x"), lax.axis_index("y"), lax.axis_index("c")
        me, sibling = (mx, my, mc), (mx, my, 1 - mc)
        chips = [(1 - mx, my), (mx, 1 - my), (1 - mx, 1 - my)]

        def rows(px, py, pc):
            return out_ref.at[pl.ds((4 * px + 2 * py + pc) * m_per, m_per), :]

        def copy(k, block, to, src=None):
            return pltpu.make_async_remote_copy(
                src_ref=rows(*block) if src is None else src, dst_ref=rows(*block), send_sem=send_sems.at[k],
                recv_sem=recv_sems.at[k], device_id=to, device_id_type=MESH)

        mine = pltpu.make_async_copy(x_ref, rows(*me), local_sem)
        mine.start()
        first = [copy(0, me, sibling, src=x_ref)]
        first += [copy(1 + j, me, (*chip, mc), src=x_ref) for j, chip in enumerate(chips)]
        for cp in first:
            cp.start()
        passed = [copy(4 + j, (*chip, mc), sibling) for j, chip in enumerate(chips)]
        for j, chip in enumerate(chips):
            copy(1 + j, (*chip, mc), me).wait_recv()
            passed[j].start()
        copy(0, sibling, me).wait_recv()
        for j, chip in enumerate(chips):
            copy(4 + j, (*chip, 1 - mc), me).wait_recv()
        for cp in first + passed:
            cp.wait_send()
        mine.wait()

    return pl.pallas_call(
        body, name=name, out_shape=jax.ShapeDtypeStruct((N_DEV * m_per, n), x.dtype),
        in_specs=[pl.BlockSpec(memory_space=pltpu.VMEM)], out_specs=pl.BlockSpec(memory_space=pltpu.VMEM),
        scratch_shapes=[pltpu.SemaphoreType.DMA((7,)), pltpu.SemaphoreType.DMA((7,)), pltpu.SemaphoreType.DMA],
    )(x)


def _my_index():
    return 4 * lax.axis_index("x") + 2 * lax.axis_index("y") + lax.axis_index("c")


def _gather_stage(srcs, *, name):
    n = len(srcs)

    def body(*refs):
        x_refs, out_refs = refs[:n], refs[n:2 * n]
        send_sems, recv_sems, local_sems = refs[2 * n:]
        mx, my, mc = lax.axis_index("x"), lax.axis_index("y"), lax.axis_index("c")
        me, sibling = (mx, my, mc), (mx, my, 1 - mc)
        chips = [(1 - mx, my), (mx, 1 - my), (1 - mx, 1 - my)]

        def slot(i, px, py, pc):
            return out_refs[i].at[4 * px + 2 * py + pc]

        def copy(i, k, block, to, src=None):
            return pltpu.make_async_remote_copy(
                src_ref=slot(i, *block) if src is None else src, dst_ref=slot(i, *block), send_sem=send_sems.at[7 * i + k],
                recv_sem=recv_sems.at[7 * i + k], device_id=to, device_id_type=MESH)

        mine = [pltpu.make_async_copy(x_refs[i], slot(i, *me), local_sems.at[i]) for i in range(n)]
        first = [copy(i, 0, me, sibling, src=x_refs[i]) for i in range(n)]
        first += [copy(i, 1 + j, me, (*chip, mc), src=x_refs[i]) for j, chip in enumerate(chips) for i in range(n)]
        for cp in mine + first:
            cp.start()
        passed = []
        for j, chip in enumerate(chips):
            for i in range(n):
                copy(i, 1 + j, (*chip, mc), me).wait_recv()
                passed.append(copy(i, 4 + j, (*chip, mc), sibling))
                passed[-1].start()
        for i in range(n):
            copy(i, 0, sibling, me).wait_recv()
            for j, chip in enumerate(chips):
                copy(i, 4 + j, (*chip, 1 - mc), me).wait_recv()
        for cp in first + passed:
            cp.wait_send()
        for cp in mine:
            cp.wait()

    any_space = pl.BlockSpec(memory_space=pl.ANY)
    return pl.pallas_call(
        body, name=name, out_shape=[jax.ShapeDtypeStruct((N_DEV,) + s.shape, s.dtype) for s in srcs],
        in_specs=[any_space] * n, out_specs=[any_space] * n,
        scratch_shapes=[pltpu.SemaphoreType.DMA((7 * n,)), pltpu.SemaphoreType.DMA((7 * n,)), pltpu.SemaphoreType.DMA((n,))],
    )(*srcs)


def _scatter_stage(srcs, *, name):
    n = len(srcs)

    def body(*refs):
        x_refs, out_refs = refs[:n], refs[n:2 * n]
        send_sems, recv_sems, local_sems = refs[2 * n:]
        mx, my, mc = lax.axis_index("x"), lax.axis_index("y"), lax.axis_index("c")
        me = 4 * mx + 2 * my + mc
        mine = [pltpu.make_async_copy(x_refs[i].at[me], out_refs[i].at[me], local_sems.at[i]) for i in range(n)]
        for cp in mine:
            cp.start()
        sends, recvs = [], []
        for k in range(1, N_DEV):
            px = 1 - mx if k & 4 else mx
            py = 1 - my if k & 2 else my
            pc = 1 - mc if k & 1 else mc
            peer = 4 * px + 2 * py + pc
            for i in range(n):
                sems = dict(send_sem=send_sems.at[7 * i + k - 1], recv_sem=recv_sems.at[7 * i + k - 1],
                            device_id=(px, py, pc), device_id_type=MESH)
                sends.append(pltpu.make_async_remote_copy(src_ref=x_refs[i].at[peer], dst_ref=out_refs[i].at[me], **sems))
                recvs.append(pltpu.make_async_remote_copy(src_ref=x_refs[i].at[me], dst_ref=out_refs[i].at[peer], **sems))
                sends[-1].start()
        for cp in recvs:
            cp.wait_recv()
        for cp in sends:
            cp.wait_send()
        for cp in mine:
            cp.wait()

    any_space = pl.BlockSpec(memory_space=pl.ANY)
    return pl.pallas_call(
        body, name=name, out_shape=[jax.ShapeDtypeStruct(s.shape, s.dtype) for s in srcs],
        in_specs=[any_space] * n, out_specs=[any_space] * n,
        scratch_shapes=[pltpu.SemaphoreType.DMA((7 * n,)), pltpu.SemaphoreType.DMA((7 * n,)), pltpu.SemaphoreType.DMA((n,))],
    )(*srcs)


def _adamw_math(g, w, m, v):
    c1, c2 = 1.0 - ADAM_B1 ** ADAM_STEP, 1.0 - ADAM_B2 ** ADAM_STEP
    m_new = ADAM_B1 * m + (1.0 - ADAM_B1) * g
    v_new = ADAM_B2 * v + (1.0 - ADAM_B2) * (g * g)
    delta = -ADAM_LR * ((m_new / c1) / (jnp.sqrt(v_new / c2) + ADAM_EPS) + ADAM_WD * w)
    return delta, m_new, v_new


def _adamw_big(slots, w, m, v, layer, *, name):
    _, r, c = slots.shape
    tr = _divisor_tile(r, max(8, (200 * 1024) // c // 8 * 8), 8)

    def body(s_ref, w_ref, m_ref, v_ref, g_out, d_out, m_out, v_out):
        g = s_ref[0].astype(F32)
        for p in range(1, N_DEV):
            g = g + s_ref[p].astype(F32)
        g_out[...] = g
        d_out[...], m_out[...], v_out[...] = _adamw_math(g, w_ref[...], m_ref[...], v_ref[...])

    blk = pl.BlockSpec((tr, c), lambda i: (i, 0))
    lay = pl.BlockSpec((None, tr, c), lambda i: (layer, i, 0))
    return pl.pallas_call(
        body, name=name, grid=(r // tr,), in_specs=[pl.BlockSpec((N_DEV, tr, c), lambda i: (0, i, 0)), lay, lay, lay],
        out_specs=[blk] * 4, out_shape=[jax.ShapeDtypeStruct((r, c), F32)] * 4,
        compiler_params=pltpu.CompilerParams(dimension_semantics=("parallel",)),
    )(slots, w, m, v)


def _sum_small(got, r_re, r_sh, *, name):
    per_dev = r_re + N_DEV * r_sh

    def body(got_ref, re_ref, sh_ref):
        mine = r_re + _my_index() * r_sh
        acc_re = got_ref[0:r_re, :]
        acc_sh = got_ref[pl.ds(pl.multiple_of(mine, 8), r_sh), :]
        for p in range(1, N_DEV):
            acc_re = acc_re + got_ref[p * per_dev:p * per_dev + r_re, :]
            acc_sh = acc_sh + got_ref[pl.ds(pl.multiple_of(p * per_dev + mine, 8), r_sh), :]
        re_ref[...] = acc_re
        sh_ref[...] = acc_sh

    return pl.pallas_call(body, name=name, out_shape=[jax.ShapeDtypeStruct((r_re, LANES), F32),
                                                       jax.ShapeDtypeStruct((r_sh, LANES), F32)])(got)


def _adamw_small(gs, ws, ms, vs, *, name):
    n = len(gs)

    def body(*refs):
        ins, outs = refs[:4 * n], refs[4 * n:]
        for i in range(n):
            res = _adamw_math(ins[i][...], ins[n + i][...], ins[2 * n + i][...], ins[3 * n + i][...])
            for j in range(3):
                outs[j * n + i][...] = res[j]

    out = pl.pallas_call(body, name=name, out_shape=[jax.ShapeDtypeStruct(a.shape, F32) for a in ws] * 3)(*gs, *ws, *ms, *vs)
    return out[:n], out[n:2 * n], out[2 * n:]


def _layout(shapes, row_align, total_align):
    lay, off = {}, 0
    for name, shape in shapes.items():
        size = int(np.prod(shape))
        rows = -(-size // LANES)
        rows = -(-rows // row_align) * row_align
        lay[name] = (off, rows, size, tuple(shape))
        off += rows
    return lay, -(-off // total_align) * total_align


def _pack(arrs, lay, total, dtype, lead=()):
    parts = []
    nl = len(lead)
    for name, (off, rows, size, shape) in lay.items():
        flat = arrs[name].astype(dtype).reshape(*lead, size)
        parts.append(jnp.pad(flat, [(0, 0)] * nl + [(0, rows * LANES - size)]).reshape(*lead, rows, LANES))
    used = sum(v[1] for v in lay.values())
    if total > used:
        parts.append(jnp.zeros((*lead, total - used, LANES), dtype))
    return jnp.concatenate(parts, axis=nl)


def _unpack(buf, lay, lead=()):
    out = {}
    nl = len(lead)
    for name, (off, rows, size, shape) in lay.items():
        part = lax.slice_in_dim(buf, off, off + rows, axis=nl).reshape(*lead, rows * LANES)
        out[name] = lax.slice_in_dim(part, 0, size, axis=nl).reshape(*lead, *shape)
    return out


_SHARD_AXIS = {
    "norm_mix": None, "norm_ffn": None, "norm_final": None, "gla_w_in": 2, "gla_w_g2": 2, "gla_b_g2": None,
    "gla_norm": None, "gla_w_out": 1, "cv_w_in": 2, "cv_b_in": 1, "cv_w_dw": 2, "cv_b_dw": 1, "cv_ln_g": 1,
    "cv_ln_b": 1, "cv_w_out": 1, "cv_b_out": 1, "sg_w_in": 2, "sg_b_in": 1, "sg_ln_g": 1, "sg_ln_b": 1, "sg_w_s": None,
    "sg_b_s": None, "sg_w_out": 1, "sg_b_out": 1, "hg_w_in": 2, "hg_lb_table": None, "hg_norm": None, "hg_w_out": 1,
    "ffn_w_up": 2, "ffn_w_dw": 2, "ffn_w_down": 1,
}
_MATMUL_WEIGHTS = ("gla_w_in", "gla_w_out", "cv_w_in", "cv_w_out", "sg_w_in", "sg_w_out", "hg_w_in", "hg_w_out",
                   "ffn_w_up", "ffn_w_down")
_NAMES = tuple(_SHARD_AXIS)


def kernel(x, norm_mix, norm_ffn, norm_final, gla_w_in, gla_w_g2, gla_b_g2, gla_norm, gla_w_out, cv_w_in, cv_b_in, cv_w_dw, cv_b_dw, cv_ln_g, cv_ln_b, cv_w_out, cv_b_out, sg_w_in, sg_b_in, sg_ln_g, sg_ln_b, sg_w_s, sg_b_s, sg_w_out, sg_b_out, hg_w_in, hg_lb_table, hg_norm, hg_w_out, ffn_w_up, ffn_w_dw, ffn_w_down, loss_target, m_norm_mix, m_norm_ffn, m_norm_final, m_gla_w_in, m_gla_w_g2, m_gla_b_g2, m_gla_norm, m_gla_w_out, m_cv_w_in, m_cv_b_in, m_cv_w_dw, m_cv_b_dw, m_cv_ln_g, m_cv_ln_b, m_cv_w_out, m_cv_b_out, m_sg_w_in, m_sg_b_in, m_sg_ln_g, m_sg_ln_b, m_sg_w_s, m_sg_b_s, m_sg_w_out, m_sg_b_out, m_hg_w_in, m_hg_lb_table, m_hg_norm, m_hg_w_out, m_ffn_w_up, m_ffn_w_dw, m_ffn_w_down, v_norm_mix, v_norm_ffn, v_norm_final, v_gla_w_in, v_gla_w_g2, v_gla_b_g2, v_gla_norm, v_gla_w_out, v_cv_w_in, v_cv_b_in, v_cv_w_dw, v_cv_b_dw, v_cv_ln_g, v_cv_ln_b, v_cv_w_out, v_cv_b_out, v_sg_w_in, v_sg_b_in, v_sg_ln_g, v_sg_ln_b, v_sg_w_s, v_sg_b_s, v_sg_w_out, v_sg_b_out, v_hg_w_in, v_hg_lb_table, v_hg_norm, v_hg_w_out, v_ffn_w_up, v_ffn_w_dw, v_ffn_w_down):
    local = dict(locals())
    wts = {n: local[n] for n in _NAMES}
    mom = {n: local["m_" + n] for n in _NAMES}
    var = {n: local["v_" + n] for n in _NAMES}
    small_all = [n for n in _NAMES if n not in _MATMUL_WEIGHTS]
    small_sharded = [n for n in small_all if _SHARD_AXIS[n] is not None]
    bsz, seq, d = x.shape
    depth = norm_mix.shape[0]

    stages = {}
    for layer in range(depth):
        kind = _MIXERS[layer % 4]
        stages[kind, layer] = {"w_in": (kind + "_w_in", layer // 4), "w_out": (kind + "_w_out", layer // 4)}
        stages["ffn", layer] = {"w_up": ("ffn_w_up", layer), "w_down": ("ffn_w_down", layer)}

    order = list(stages)
    shards = lambda stage: [wts[nm][idx].astype(BF16) for nm, idx in stages[stage].values()]
    gathers = {order[0]: _Side(shards(order[0]), False)}
    gathers[order[0]].lands = _gather_stage(gathers[order[0]].srcs, name="gather_first")
    scatters, waiting, down_gathers, down_scatters = {}, [], {}, {}

    def ride(kind, layer, forward):
        if not forward:
            return waiting.pop() if waiting else None
        at = order.index((kind, layer)) + 1
        if at == len(order):
            return None
        srcs = shards(order[at])
        if order[at][0] == "ffn":
            down_gathers[order[at][1]] = _Side(srcs[1:], False)
            srcs = srcs[:1]
        gathers[order[at]] = _Side(srcs, False)
        return gathers[order[at]]

    lay_sw, r_sw = _layout({n: wts[n].shape for n in small_sharded}, 8, 8)
    got_sw = _all_gather(_pack(wts, lay_sw, r_sw, F32), name="gather_small_weights")
    parts = _unpack(got_sw.reshape(N_DEV, r_sw, LANES), lay_sw, (N_DEV,))
    full_small = {n: wts[n] for n in small_all if _SHARD_AXIS[n] is None}
    for n in small_sharded:
        ax, shape = _SHARD_AXIS[n], wts[n].shape
        full_small[n] = jnp.moveaxis(parts[n], 0, ax).reshape(shape[:ax] + (N_DEV * shape[ax],) + shape[ax + 1:])

    def full_size(nm, land):
        _, r, c = land.shape
        if _SHARD_AXIS[nm] == 2:
            return land.transpose(1, 0, 2).reshape(r, N_DEV * c), land.transpose(0, 2, 1).reshape(N_DEV * c, r)
        return land.reshape(N_DEV * r, c), land.reshape(N_DEV * r, c).T

    def get_big(kind, layer):
        names = [nm for nm, _ in stages[kind, layer].values()]
        lands = gathers[kind, layer].lands
        if kind != "ffn" or layer not in down_gathers:
            return _oriented(kind, {key: full_size(nm, land) for key, nm, land in zip(stages[kind, layer], names, lands)})
        up, up_t = full_size(names[0], lands[0])
        f = up.shape[1] // 2

        def down_landed():
            down, down_t = full_size(names[1], down_gathers[layer].lands[0])
            return dict(w_down=down, w_down_t=down_t)

        return dict(w_up=up, w_up_t_gate=up_t[:f], w_up_t_val=up_t[f:], late=(down_gathers[layer], down_landed))

    def put_big(kind, layer, g):
        if kind == "ffn":
            k, f = g["w_up_gate"].shape
            halves = [g[key].reshape(k, N_DEV // 2, 2 * f // N_DEV) for key in ("w_up_gate", "w_up_val")]
            w_in = jnp.concatenate(halves, axis=1)
        else:
            w_in = jnp.concatenate([g["w_main"], g["w_glr"][:, :GLA_RANK]], axis=1) if kind == "gla" else g["w_in"]
            w_in = w_in.reshape(w_in.shape[0], N_DEV, w_in.shape[1] // N_DEV)
        sends = [w_in.transpose(1, 0, 2).astype(BF16)]
        if kind != "ffn":
            sends.append(row_slots(g["w_out"]))
        scatters[kind, layer] = _Side(sends, True)
        waiting.append(scatters[kind, layer])

    def row_slots(grad):
        return grad.reshape(N_DEV, grad.shape[0] // N_DEV, grad.shape[1]).astype(BF16)

    def put_early(layer, grad_w_down):
        down_scatters[layer] = _Side([row_slots(grad_w_down)], True)
        return down_scatters[layer]

    loss, dx, grads = _local_step(x.reshape(bsz * seq, d), loss_target.reshape(bsz * seq, d), _prep_small(full_small), seq,
                                  get_big, put_big, ride, put_early)
    loss = lax.psum(loss[0, 0], ("x", "y", "c"))

    gs = _small_grads(grads)
    small_repl = [n for n in small_all if _SHARD_AXIS[n] is None]
    lay_re, r_re = _layout({n: wts[n].shape for n in small_repl}, 8, 8)
    slots = {}
    for n in small_sharded:
        ax, shape = _SHARD_AXIS[n], wts[n].shape
        slots[n] = jnp.moveaxis(gs[n].reshape(shape[:ax] + (N_DEV, shape[ax]) + shape[ax + 1:]), ax, 0)
    sent = jnp.concatenate([_pack(gs, lay_re, r_re, F32), _pack(slots, lay_sw, r_sw, F32, (N_DEV,)).reshape(-1, LANES)])
    sum_re, sum_sh = _sum_small(_all_gather(sent, name="gather_small_grads"), r_re, r_sw, name="sum_small_grads")
    g_own = _unpack(sum_re, lay_re)
    g_own.update(_unpack(sum_sh, lay_sw))
    two_d = lambda a: a.reshape(-1, a.shape[-1])
    upd = _adamw_small(*[[two_d(src[n]) for n in small_all] for src in (g_own, wts, mom, var)], name="adamw_small")
    results = {n: [g_own[n]] + [part[i].reshape(wts[n].shape) for part in upd] for i, n in enumerate(small_all)}

    per_layer = {}
    for (kind, layer), side in scatters.items():
        lands = side.lands if side.lands is not None else _scatter_stage(side.srcs, name="scatter_last")
        if kind == "ffn":
            lands = list(lands) + down_scatters[layer].lands
        for (nm, idx), land in zip(stages[kind, layer].values(), lands):
            three_d = lambda a: a.reshape((a.shape[0],) + land.shape[1:])
            per_layer.setdefault(nm, {})[idx] = _adamw_big(land, three_d(wts[nm]), three_d(mom[nm]), three_d(var[nm]), idx,
                                                           name="adamw_" + nm)
    for nm, by_idx in per_layer.items():
        outs = [by_idx[i] for i in range(len(by_idx))]
        results[nm] = [(outs[0][j] if len(outs) == 1 else jnp.stack([o[j] for o in outs])).reshape(wts[nm].shape)
                       for j in range(4)]
    out = [loss, dx.reshape(bsz, seq, d)]
    for j in range(4):
        out += [results[n][j] for n in _NAMES]
    return tuple(out)
```

```python
import functools
import math

import jax
import jax.numpy as jnp
import numpy as np
from jax import lax
from jax.experimental import pallas as pl
from jax.experimental.pallas import tpu as pltpu

F32 = jnp.float32
BF16 = jnp.bfloat16
EPS = 1e-6
N_DEV = 8
LANES = 128
SUBLANES_BF16 = 16
HALO = 32
GLA_HEADS, GLA_RANK, GLA_GATE_NORM, GLA_CHUNK = 4, 16, 16.0, 64
SGU_CHUNK, SGU_GROUPS = 128, 8
HGRN_EXPAND, HGRN_CHUNK = 128, 64
CONV_WIDTH, FFN_CONV_WIDTH = 31, 3
ADAM_LR, ADAM_B1, ADAM_B2, ADAM_EPS, ADAM_WD, ADAM_STEP = 0.001, 0.9, 0.999, 1e-08, 0.01, 10
MESH = pl.DeviceIdType.MESH


def _sigmoid(x):
    return 0.5 * (jnp.tanh(0.5 * x) + 1.0)


def _silu(x):
    return x * _sigmoid(x)


def _log_sigmoid(x):
    return jnp.minimum(x, 0.0) - jnp.log(1.0 + jnp.exp(-jnp.abs(x)))


def _gelu(x):
    return 0.5 * x * (1.0 + jnp.tanh(math.sqrt(2.0 / math.pi) * (x + 0.044715 * (x * x * x))))


def _rms(x, g):
    return x * lax.rsqrt(jnp.mean(x * x, axis=-1, keepdims=True) + EPS) * g


def _layer_norm(x, g, b):
    xc = x - jnp.mean(x, axis=-1, keepdims=True)
    return xc * lax.rsqrt(jnp.mean(xc * xc, axis=-1, keepdims=True) + EPS) * g + b


def _dot_raw(a, b, dims):
    return lax.dot_general(a.astype(BF16), b.astype(BF16), (dims, ((), ())), preferred_element_type=F32)


@jax.custom_vjp
def _bdot(a, b):
    return _dot_raw(a, b, ((1,), (0,)))


@jax.custom_vjp
def _bdot_nt(a, b):
    return _dot_raw(a, b, ((1,), (1,)))


@jax.custom_vjp
def _bdot_tn(a, b):
    return _dot_raw(a, b, ((0,), (0,)))


_bdot.defvjp(lambda a, b: (_bdot(a, b), (a, b)), lambda r, g: (_bdot_nt(g, r[1]), _bdot_tn(r[0], g)))
_bdot_nt.defvjp(lambda a, b: (_bdot_nt(a, b), (a, b)), lambda r, g: (_bdot(g, r[1]), _bdot_tn(g, r[0])))
_bdot_tn.defvjp(lambda a, b: (_bdot_tn(a, b), (a, b)), lambda r, g: (_bdot_nt(r[1], g), _bdot(r[0], g)))


def _scan_rows(x, reverse):
    n = x.shape[0]
    row = lax.broadcasted_iota(jnp.int32, x.shape, 0)
    step = 1
    while step < n:
        if reverse:
            x = x + jnp.where(row < n - step, pltpu.roll(x, n - step, 0), 0.0)
        else:
            x = x + jnp.where(row >= step, pltpu.roll(x, step, 0), 0.0)
        step *= 2
    return x


@jax.custom_vjp
def _cumsum_rows(x):
    return _scan_rows(x, False)


_cumsum_rows.defvjp(lambda x: (_scan_rows(x, False), None), lambda _, g: (_scan_rows(g, True),))


def _divisor_tile(n, cap, unit):
    if n <= cap:
        return n
    best = None
    for t in range(unit, cap + 1, unit):
        if n % t == 0:
            best = t
    assert best is not None, (n, cap, unit)
    return best


def _const_map(nd):
    return lambda *_: (0,) * nd


class _Side:
    def __init__(self, srcs, scatter, parts=()):
        self.srcs, self.scatter, self.lands, self.parts = list(srcs), scatter, None, list(parts)

    @staticmethod
    def join(sides):
        sides = [s for s in sides if s is not None]
        if len(sides) < 2:
            return sides[0] if sides else None
        assert len({s.scatter for s in sides}) == 1
        return _Side([a for s in sides for a in s.srcs], sides[0].scatter, sides)

    def landed(self, lands):
        self.lands = list(lands)
        at = 0
        for part in self.parts:
            part.landed(self.lands[at:at + len(part.srcs)])
            at += len(part.srcs)


def _pallas(body, side, *, name, grid, in_specs, out_specs, out_shape, scratch_shapes=(), semantics):
    if side is None:
        return pl.pallas_call(body, name=name, grid=grid, in_specs=in_specs, out_specs=out_specs, out_shape=out_shape,
                              scratch_shapes=list(scratch_shapes),
                              compiler_params=pltpu.CompilerParams(dimension_semantics=semantics))
    single = not isinstance(out_shape, (list, tuple))
    out_specs, out_shape = ([out_specs], [out_shape]) if single else (list(out_specs), list(out_shape))
    n, n_in, n_out, n_scr = len(side.srcs), len(in_specs), len(out_shape), len(scratch_shapes)
    lands = [jax.ShapeDtypeStruct((N_DEV,) + (s.shape[1:] if side.scatter else s.shape), s.dtype) for s in side.srcs]

    def body2(*refs):
        x_refs, land_refs = refs[n_in:n_in + n], refs[n_in + n + n_out:n_in + 2 * n + n_out]
        send_sems, recv_sems, local_sems = refs[-3:]
        steps = [pl.program_id(a) for a in range(len(grid))]
        first = functools.reduce(jnp.logical_and, [s == 0 for s in steps])
        last = functools.reduce(jnp.logical_and, [s == g - 1 for s, g in zip(steps, grid)])

        def copies():
            mx, my, mc = lax.axis_index("x"), lax.axis_index("y"), lax.axis_index("c")
            me = 4 * mx + 2 * my + mc
            mine = [pltpu.make_async_copy(x_refs[i].at[me] if side.scatter else x_refs[i], land_refs[i].at[me],
                                          local_sems.at[i]) for i in range(n)]
            sends, recvs = [], []
            for k in range(1, N_DEV):
                px = 1 - mx if k & 4 else mx
                py = 1 - my if k & 2 else my
                pc = 1 - mc if k & 1 else mc
                peer = 4 * px + 2 * py + pc
                for i in range(n):
                    sems = dict(send_sem=send_sems.at[7 * i + k - 1], recv_sem=recv_sems.at[7 * i + k - 1],
                                device_id=(px, py, pc), device_id_type=MESH)
                    src = x_refs[i].at[peer] if side.scatter else x_refs[i]
                    sends.append(pltpu.make_async_remote_copy(src_ref=src, dst_ref=land_refs[i].at[me], **sems))
                    recvs.append(pltpu.make_async_remote_copy(src_ref=src, dst_ref=land_refs[i].at[peer], **sems))
            return mine, sends, recvs

        @pl.when(first)
        def _():
            mine, sends, _ = copies()
            for cp in mine + sends:
                cp.start()

        body(*refs[:n_in], *refs[n_in + n:n_in + n + n_out], *refs[n_in + 2 * n + n_out:n_in + 2 * n + n_out + n_scr])

        @pl.when(last)
        def _():
            mine, sends, recvs = copies()
            for cp in recvs:
                cp.wait_recv()
            for cp in sends:
                cp.wait_send()
            for cp in mine:
                cp.wait()

    any_space = pl.BlockSpec(memory_space=pl.ANY)
    call = pl.pallas_call(
        body2, name=name, grid=grid, in_specs=list(in_specs) + [any_space] * n, out_specs=out_specs + [any_space] * n,
        out_shape=out_shape + lands,
        scratch_shapes=list(scratch_shapes) + [pltpu.SemaphoreType.DMA((7 * n,)), pltpu.SemaphoreType.DMA((7 * n,)),
                                               pltpu.SemaphoreType.DMA((n,))],
        compiler_params=pltpu.CompilerParams(dimension_semantics=("arbitrary",) * len(grid)))

    def run(*args):
        res = call(*args, *side.srcs)
        side.landed(res[n_out:])
        return res[0] if single else res[:n_out]

    return run


def _mm(a, b, *, add=None, bias=None, out_dtype=F32, name, side=None):
    m, k = a.shape
    k2, n = b.shape
    assert k == k2
    tn = _divisor_tile(n, max(LANES, min(1408, (6 << 20) // (2 * k) // LANES * LANES)), LANES)
    tm = _divisor_tile(m, max(256, min(1024, (4 << 20) // (a.dtype.itemsize * k) // 256 * 256)), 8)
    has_bias, has_add = bias is not None, add is not None

    def body(*refs):
        a_ref, b_ref = refs[0], refs[1]
        o_ref = refs[-1]
        acc = jnp.dot(a_ref[...].astype(BF16), b_ref[...], preferred_element_type=F32)
        pos = 2
        if has_bias:
            acc = acc + refs[pos][...]
            pos += 1
        if has_add:
            acc = acc + refs[pos][...].astype(F32)
        o_ref[...] = acc.astype(o_ref.dtype)

    in_specs = [pl.BlockSpec((tm, k), lambda i, j: (i, 0)), pl.BlockSpec((k, tn), lambda i, j: (0, j))]
    args = [a, b]
    if has_bias:
        in_specs.append(pl.BlockSpec((1, tn), lambda i, j: (0, j)))
        args.append(bias)
    if has_add:
        in_specs.append(pl.BlockSpec((tm, tn), lambda i, j: (i, j)))
        args.append(add)
    return _pallas(
        body, side, name=name, grid=(m // tm, n // tn), in_specs=in_specs,
        out_specs=pl.BlockSpec((tm, tn), lambda i, j: (i, j)),
        out_shape=jax.ShapeDtypeStruct((m, n), out_dtype),
        semantics=("parallel", "parallel"),
    )(*args)


def _mm_tn(a, g, *, name):
    m, k = a.shape
    m2, n = g.shape
    assert m == m2
    tk = _divisor_tile(k, 1408, LANES)
    tn = _divisor_tile(n, 1408, LANES)
    tm = _divisor_tile(m, 1024, 8)

    def body(a_ref, g_ref, o_ref):
        @pl.when(pl.program_id(2) == 0)
        def _():
            o_ref[...] = jnp.zeros_like(o_ref)

        o_ref[...] += _dot_raw(a_ref[...], g_ref[...], ((0,), (0,)))

    return pl.pallas_call(
        body, name=name, grid=(k // tk, n // tn, m // tm),
        in_specs=[pl.BlockSpec((tm, tk), lambda i, j, t: (t, i)), pl.BlockSpec((tm, tn), lambda i, j, t: (t, j))],
        out_specs=pl.BlockSpec((tk, tn), lambda i, j, t: (i, j)),
        out_shape=jax.ShapeDtypeStruct((k, n), F32),
        compiler_params=pltpu.CompilerParams(dimension_semantics=("parallel", "parallel", "arbitrary")),
    )(a, g)


def _tile_call(name, fn, tiled, params, out_tiled, out_acc, tile, side=None):
    tiled = [t if isinstance(t, tuple) else (t, t.shape[1], 0) for t in tiled]
    t_rows = tiled[0][0].shape[0]
    tile = min(tile, t_rows)
    assert t_rows % tile == 0
    n_t, n_p, n_o = len(tiled), len(params), len(out_tiled)

    def body(*refs):
        vals = [r[...] for r in refs[: n_t + n_p]]
        touts, aouts = fn(*vals)
        for r, v in zip(refs[n_t + n_p: n_t + n_p + n_o], touts):
            r[...] = v.astype(r.dtype)
        acc_refs = refs[n_t + n_p + n_o:]
        if acc_refs:
            @pl.when(pl.program_id(0) == 0)
            def _():
                for r in acc_refs:
                    r[...] = jnp.zeros_like(r)

            for r, v in zip(acc_refs, aouts):
                r[...] += v

    in_specs = [pl.BlockSpec((tile, w), lambda i, cb=cb: (i, cb)) for _, w, cb in tiled]
    in_specs += [pl.BlockSpec(p.shape, _const_map(p.ndim)) for p in params]
    out_specs = [pl.BlockSpec((tile, w), lambda i: (i, 0)) for w, _ in out_tiled]
    out_specs += [pl.BlockSpec(s, _const_map(len(s))) for s in out_acc]
    out_shape = [jax.ShapeDtypeStruct((t_rows, w), dt) for w, dt in out_tiled]
    out_shape += [jax.ShapeDtypeStruct(s, F32) for s in out_acc]
    res = _pallas(
        body, side, name=name, grid=(t_rows // tile,), in_specs=in_specs, out_specs=out_specs, out_shape=out_shape,
        semantics=("arbitrary" if out_acc else "parallel",),
    )(*[t[0] for t in tiled], *params)
    return res[:n_o], res[n_o:]


def _rms_fwd(x, g, name):
    (h,), _ = _tile_call(name, lambda xv, gv: ([_rms(xv, gv)], []), [x], [g], [(x.shape[1], BF16)], [], 512)
    return h


def _rms_bwd(x, g, dh, dres, name):
    def fn(xv, dhv, drv, gv):
        _, vjp = jax.vjp(_rms, xv, gv)
        dx, dg = vjp(dhv.astype(F32))
        return [drv + dx], [dg]

    (dx,), (dg,) = _tile_call(name, fn, [x, dh, dres], [g], [(x.shape[1], F32)], [g.shape], 512)
    return dx, dg


def _colsum(x, name):
    _, (s,) = _tile_call(name, lambda xv: ([], [jnp.sum(xv.astype(F32), axis=0, keepdims=True)]), [x], [], [],
                         [(1, x.shape[1])], 512)
    return s


def _seq_flags(i, tiles_per_seq):
    pos = i % tiles_per_seq
    return pos == 0, pos == tiles_per_seq - 1


def _dwconv_fwd(x, w, b, seq, name, side=None):
    t_rows, ch = x.shape
    kw = w.shape[0]
    tile = min(512, seq)
    cb = _divisor_tile(ch, 256, LANES)
    tps, hb = seq // tile, tile // HALO

    def body(x_ref, halo_ref, w_ref, b_ref, y_ref, pad_ref):
        first, _ = _seq_flags(pl.program_id(0), tps)
        pad_ref[0:HALO, :] = jnp.where(first, 0.0, halo_ref[...])
        pad_ref[HALO:HALO + tile, :] = x_ref[...]
        for r0 in range(0, tile, HALO):
            acc = jnp.broadcast_to(b_ref[...], (HALO, cb))
            for k in range(kw):
                acc = acc + pad_ref[pl.ds(HALO - (kw - 1) + k + r0, HALO), :] * w_ref[k:k + 1, :]
            y_ref[pl.ds(r0, HALO), :] = acc

    return _pallas(
        body, side, name=name, grid=(t_rows // tile, ch // cb),
        in_specs=[pl.BlockSpec((tile, cb), lambda i, j: (i, j)),
                  pl.BlockSpec((HALO, cb), lambda i, j: (jnp.maximum(i * hb - 1, 0), j)),
                  pl.BlockSpec((kw, cb), lambda i, j: (0, j)), pl.BlockSpec((1, cb), lambda i, j: (0, j))],
        out_specs=pl.BlockSpec((tile, cb), lambda i, j: (i, j)),
        out_shape=jax.ShapeDtypeStruct((t_rows, ch), F32),
        scratch_shapes=[pltpu.VMEM((HALO + tile, cb), F32)],
        semantics=("parallel", "parallel"),
    )(x, x, w, b)


def _dwconv_bwd(x, dy, w, seq, name, side=None):
    t_rows, ch = x.shape
    kw = w.shape[0]
    tile = min(512, seq)
    cb = _divisor_tile(ch, 256, LANES)
    tps, hb, n_hb = seq // tile, tile // HALO, t_rows // HALO

    def body(x_ref, xh_ref, dy_ref, dyh_ref, w_ref, dx_ref, dw_ref, db_ref, xpad, dypad, sums):
        i = pl.program_id(1)
        first, last = _seq_flags(i, tps)

        @pl.when(i == 0)
        def _():
            sums[...] = jnp.zeros_like(sums)

        xpad[0:HALO, :] = jnp.where(first, 0.0, xh_ref[...])
        xpad[HALO:HALO + tile, :] = x_ref[...]
        dypad[0:tile, :] = dy_ref[...]
        dypad[tile:tile + HALO, :] = jnp.where(last, 0.0, dyh_ref[...])
        fold = lambda v: functools.reduce(jnp.add, [v[r:r + 8] for r in range(0, HALO, 8)])
        for r0 in range(0, tile, HALO):
            dyc = dy_ref[pl.ds(r0, HALO), :]
            acc = jnp.zeros((HALO, cb), F32)
            for k in range(kw):
                acc = acc + dypad[pl.ds(kw - 1 - k + r0, HALO), :] * w_ref[k:k + 1, :]
                sums[8 * k:8 * k + 8, :] += fold(dyc * xpad[pl.ds(HALO - (kw - 1) + k + r0, HALO), :])
            dx_ref[pl.ds(r0, HALO), :] = acc
            sums[8 * kw:8 * kw + 8, :] += fold(dyc)

        @pl.when(i == t_rows // tile - 1)
        def _():
            for k in range(kw):
                dw_ref[k:k + 1, :] = jnp.sum(sums[8 * k:8 * k + 8, :], axis=0, keepdims=True)
            db_ref[...] = jnp.sum(sums[8 * kw:8 * kw + 8, :], axis=0, keepdims=True)

    return _pallas(
        body, side, name=name, grid=(ch // cb, t_rows // tile),
        in_specs=[pl.BlockSpec((tile, cb), lambda j, i: (i, j)),
                  pl.BlockSpec((HALO, cb), lambda j, i: (jnp.maximum(i * hb - 1, 0), j)),
                  pl.BlockSpec((tile, cb), lambda j, i: (i, j)),
                  pl.BlockSpec((HALO, cb), lambda j, i: (jnp.minimum((i + 1) * hb, n_hb - 1), j)),
                  pl.BlockSpec((kw, cb), lambda j, i: (0, j))],
        out_specs=[pl.BlockSpec((tile, cb), lambda j, i: (i, j)), pl.BlockSpec((kw, cb), lambda j, i: (0, j)),
                   pl.BlockSpec((1, cb), lambda j, i: (0, j))],
        out_shape=[jax.ShapeDtypeStruct((t_rows, ch), F32), jax.ShapeDtypeStruct((kw, ch), F32),
                   jax.ShapeDtypeStruct((1, ch), F32)],
        scratch_shapes=[pltpu.VMEM((HALO + tile, cb), F32), pltpu.VMEM((tile + HALO, cb), F32),
                        pltpu.VMEM((8 * (kw + 1), cb), F32)],
        semantics=("parallel", "arbitrary"),
    )(x, x, dy, dy, w)


_ROWS = SUBLANES_BF16


def _lane_chunks(width, cap=6 * LANES):
    return [slice(c0, min(c0 + cap, width)) for c0 in range(0, width, cap)]


def _tap_rows(w_ref, cols):
    return [w_ref[k:k + 1, cols] for k in range(FFN_CONV_WIDTH)]


def _conv3_at(pad, taps, row, cols):
    z = pad[pl.ds(row, _ROWS), cols] * taps[2]
    z = z + pad[pl.ds(row - 1, _ROWS), cols] * taps[1]
    return z + pad[pl.ds(row - 2, _ROWS), cols] * taps[0]


def _ffn_mid_fwd(u, w, seq, name, side=None):
    t_rows, f2 = u.shape
    f = f2 // 2
    tile = min(256, seq)
    cb = _divisor_tile(f, 1408, LANES)
    nj, tps, hb, hl = f // cb, seq // tile, tile // SUBLANES_BF16, SUBLANES_BF16

    def body(ug_ref, uv_ref, hg_ref, hv_ref, wg_ref, wv_ref, a_ref, gpad, vpad):
        first, _ = _seq_flags(pl.program_id(0), tps)
        for t_ref, h_ref, pad in ((ug_ref, hg_ref, gpad), (uv_ref, hv_ref, vpad)):
            pad[0:hl, :] = jnp.where(first, 0.0, h_ref[...].astype(F32))
            pad[hl:hl + tile, :] = t_ref[...].astype(F32)
        for cols in _lane_chunks(cb):
            wg, wv = _tap_rows(wg_ref, cols), _tap_rows(wv_ref, cols)
            for r0 in range(0, tile, _ROWS):
                zg = _conv3_at(gpad, wg, hl + r0, cols)
                zv = _conv3_at(vpad, wv, hl + r0, cols)
                half = 0.5 * zg
                a_ref[pl.ds(r0, _ROWS), cols] = ((jnp.tanh(half) + 1.0) * half * zv).astype(a_ref.dtype)

    halo_map = lambda off: (lambda i, j: (jnp.maximum(i * hb - 1, 0), j + off))
    return _pallas(
        body, side, name=name, grid=(t_rows // tile, nj),
        in_specs=[pl.BlockSpec((tile, cb), lambda i, j: (i, j)), pl.BlockSpec((tile, cb), lambda i, j: (i, j + nj)),
                  pl.BlockSpec((hl, cb), halo_map(0)), pl.BlockSpec((hl, cb), halo_map(nj)),
                  pl.BlockSpec((3, cb), lambda i, j: (0, j)), pl.BlockSpec((3, cb), lambda i, j: (0, j + nj))],
        out_specs=pl.BlockSpec((tile, cb), lambda i, j: (i, j)),
        out_shape=jax.ShapeDtypeStruct((t_rows, f), BF16),
        scratch_shapes=[pltpu.VMEM((hl + tile, cb), F32), pltpu.VMEM((hl + tile, cb), F32)],
        semantics=("parallel", "parallel"),
    )(u, u, u, u, w, w)


def _ffn_mid_bwd(u, da, w, seq, name, side=None):
    t_rows, f2 = u.shape
    f = f2 // 2
    tile = min(256, seq)
    cb = _divisor_tile(f, 1408, LANES)
    hl = SUBLANES_BF16
    nj, tps, hb, n_hb, ext = f // cb, seq // tile, tile // hl, t_rows // hl, tile + hl

    def body(ug_ref, uv_ref, pg_ref, pv_ref, ng_ref, nv_ref, da_ref, dan_ref, wg_ref, wv_ref,
             dug_ref, duv_ref, dwg_ref, dwv_ref, gpad, vpad, dzg, dzv):
        i = pl.program_id(1)
        first, last = _seq_flags(i, tps)

        @pl.when(i == 0)
        def _():
            dwg_ref[...] = jnp.zeros_like(dwg_ref)
            dwv_ref[...] = jnp.zeros_like(dwv_ref)

        for t_ref, p_ref, n_ref, pad in ((ug_ref, pg_ref, ng_ref, gpad), (uv_ref, pv_ref, nv_ref, vpad)):
            pad[0:hl, :] = jnp.where(first, 0.0, p_ref[...].astype(F32))
            pad[hl:hl + tile, :] = t_ref[...].astype(F32)
            pad[hl + tile:hl + ext, :] = jnp.where(last, 0.0, n_ref[...].astype(F32))
        for cols in _lane_chunks(cb):
            wg, wv = _tap_rows(wg_ref, cols), _tap_rows(wv_ref, cols)
            for r0 in range(0, ext, _ROWS):
                zg = _conv3_at(gpad, wg, hl + r0, cols)
                zv = _conv3_at(vpad, wv, hl + r0, cols)
                if r0 < tile:
                    da = da_ref[pl.ds(r0, _ROWS), cols].astype(F32)
                else:
                    da = jnp.where(last, 0.0, dan_ref[:, cols].astype(F32))
                sg = _sigmoid(zg)
                dzg[pl.ds(r0, _ROWS), cols] = da * zv * (sg * (1.0 + zg * (1.0 - sg)))
                dzv[pl.ds(r0, _ROWS), cols] = da * (zg * sg)
        for dz, w_ref, pad, du_ref, dw_ref in ((dzg, wg_ref, gpad, dug_ref, dwg_ref), (dzv, wv_ref, vpad, duv_ref, dwv_ref)):
            for cols in _lane_chunks(cb):
                taps = _tap_rows(w_ref, cols)
                width = cols.stop - cols.start
                acc = [jnp.zeros((8, width), F32) for _ in range(FFN_CONV_WIDTH)]
                for r0 in range(0, tile, _ROWS):
                    d0 = dz[pl.ds(r0, _ROWS), cols]
                    du = dz[pl.ds(r0 + 2, _ROWS), cols] * taps[0] + dz[pl.ds(r0 + 1, _ROWS), cols] * taps[1] + d0 * taps[2]
                    du_ref[pl.ds(r0, _ROWS), cols] = du.astype(du_ref.dtype)
                    for k in range(FFN_CONV_WIDTH):
                        prod = d0 * pad[pl.ds(hl - 2 + k + r0, _ROWS), cols]
                        acc[k] = acc[k] + prod[0:8] + prod[8:16]
                for k in range(FFN_CONV_WIDTH):
                    dw_ref[k:k + 1, cols] += jnp.sum(acc[k], axis=0, keepdims=True)

    prev_map = lambda off: (lambda j, i: (jnp.maximum(i * hb - 1, 0), j + off))
    next_map = lambda off: (lambda j, i: (jnp.minimum((i + 1) * hb, n_hb - 1), j + off))
    tile_spec = lambda off: pl.BlockSpec((tile, cb), lambda j, i: (i, j + off))
    w_spec = lambda off: pl.BlockSpec((3, cb), lambda j, i: (0, j + off))
    return _pallas(
        body, side, name=name, grid=(nj, t_rows // tile),
        in_specs=[tile_spec(0), tile_spec(nj), pl.BlockSpec((hl, cb), prev_map(0)), pl.BlockSpec((hl, cb), prev_map(nj)),
                  pl.BlockSpec((hl, cb), next_map(0)), pl.BlockSpec((hl, cb), next_map(nj)),
                  tile_spec(0), pl.BlockSpec((hl, cb), next_map(0)), w_spec(0), w_spec(nj)],
        out_specs=[tile_spec(0), tile_spec(0), w_spec(0), w_spec(0)],
        out_shape=[jax.ShapeDtypeStruct((t_rows, f), BF16), jax.ShapeDtypeStruct((t_rows, f), BF16),
                   jax.ShapeDtypeStruct((3, f), F32), jax.ShapeDtypeStruct((3, f), F32)],
        scratch_shapes=[pltpu.VMEM((hl + ext, cb), F32), pltpu.VMEM((hl + ext, cb), F32),
                        pltpu.VMEM((ext, cb), F32), pltpu.VMEM((ext, cb), F32)],
        semantics=("parallel", "arbitrary"),
    )(u, u, u, u, u, u, da, da, w, w)


def _gla_chunk(q, k, v, lg, st, *, scale, chunk):
    row = lax.broadcasted_iota(jnp.int32, (chunk, chunk), 0)
    col = lax.broadcasted_iota(jnp.int32, (chunk, chunk), 1)
    causal = col <= row
    b = _cumsum_rows(lg)
    upto_mid = lax.broadcasted_iota(jnp.int32, lg.shape, 0) <= chunk // 2
    b_mid = jnp.sum(jnp.where(upto_mid, lg, 0.0), axis=0, keepdims=True)
    b_last = jnp.sum(lg, axis=0, keepdims=True)
    qs = q * scale
    scores = _bdot_nt(qs * jnp.exp(b - b_mid), k * jnp.exp(b_mid - b))
    o = _bdot(jnp.where(causal, scores, 0.0), v)
    o = o + _bdot_nt(qs * jnp.exp(b), st)
    st_new = st * jnp.exp(b_last) + _bdot_tn(v, k * jnp.exp(b_last - b))
    return o, st_new


_CHUNKS_PER_STEP = 2


def _gla_specs(specs, rows, n_blocks, reverse):
    if reverse:
        row = lambda bi, ci: bi * n_blocks + (n_blocks - 1 - ci)
    else:
        row = lambda bi, ci: bi * n_blocks + ci
    return [pl.BlockSpec((rows, w), lambda bi, ci, cb=cb: (row(bi, ci), cb)) for _, w, cb in specs], row


def _gla_fwd(q, k, v, lg, *, heads, dk, dv, scale, chunk, seq, name, side=None):
    t_rows = q[0].shape[0]
    per = _CHUNKS_PER_STEP if (seq // chunk) % _CHUNKS_PER_STEP == 0 else 1
    n_blocks = seq // (per * chunk)
    fn = functools.partial(_gla_chunk, scale=scale, chunk=chunk)

    def body(q_ref, k_ref, v_ref, lg_ref, o_ref, sts_ref, st_ref):
        @pl.when(pl.program_id(1) == 0)
        def _():
            st_ref[...] = jnp.zeros_like(st_ref)

        ks = [slice(h * dk, (h + 1) * dk) for h in range(heads)]
        vs = [slice(h * dv, (h + 1) * dv) for h in range(heads)]
        st = [st_ref[vs[h], :] for h in range(heads)]
        for s in range(per):
            rows = pl.ds(s * chunk, chunk)
            for h in range(heads):
                sts_ref[s, vs[h], :] = st[h]
                o, st[h] = fn(q_ref[rows, ks[h]].astype(F32), k_ref[rows, ks[h]].astype(F32),
                              v_ref[rows, vs[h]].astype(F32), lg_ref[rows, ks[h]], st[h])
                o_ref[rows, vs[h]] = o
        for h in range(heads):
            st_ref[vs[h], :] = st[h]

    in_specs, row = _gla_specs([q, k, v, lg], per * chunk, n_blocks, False)
    return _pallas(
        body, side, name=name, grid=(t_rows // seq, n_blocks), in_specs=in_specs,
        out_specs=[pl.BlockSpec((per * chunk, heads * dv), lambda bi, ci: (row(bi, ci), 0)),
                   pl.BlockSpec((per, heads * dv, dk), lambda bi, ci: (row(bi, ci), 0, 0))],
        out_shape=[jax.ShapeDtypeStruct((t_rows, heads * dv), F32),
                   jax.ShapeDtypeStruct((t_rows // chunk, heads * dv, dk), F32)],
        scratch_shapes=[pltpu.VMEM((heads * dv, dk), F32)],
        semantics=("arbitrary", "arbitrary"),
    )(q[0], k[0], v[0], lg[0])


def _gla_bwd(q, k, v, lg, states, do, *, heads, dk, dv, scale, chunk, seq, out_dtypes, name, side=None):
    t_rows = q[0].shape[0]
    per = _CHUNKS_PER_STEP if (seq // chunk) % _CHUNKS_PER_STEP == 0 else 1
    n_blocks = seq // (per * chunk)
    fn = functools.partial(_gla_chunk, scale=scale, chunk=chunk)

    def body(q_ref, k_ref, v_ref, lg_ref, do_ref, sts_ref, dq_ref, dk_ref, dv_ref, dlg_ref, dst_ref):
        @pl.when(pl.program_id(1) == 0)
        def _():
            dst_ref[...] = jnp.zeros_like(dst_ref)

        ks = [slice(h * dk, (h + 1) * dk) for h in range(heads)]
        vs = [slice(h * dv, (h + 1) * dv) for h in range(heads)]
        dst = [dst_ref[vs[h], :] for h in range(heads)]
        for s in reversed(range(per)):
            rows = pl.ds(s * chunk, chunk)
            for h in range(heads):
                _, vjp = jax.vjp(fn, q_ref[rows, ks[h]].astype(F32), k_ref[rows, ks[h]].astype(F32),
                                 v_ref[rows, vs[h]].astype(F32), lg_ref[rows, ks[h]], sts_ref[s, vs[h], :])
                dq, dkk, dvv, dlg, dst[h] = vjp((do_ref[rows, vs[h]].astype(F32), dst[h]))
                dq_ref[rows, ks[h]] = dq.astype(dq_ref.dtype)
                dk_ref[rows, ks[h]] = dkk.astype(dk_ref.dtype)
                dv_ref[rows, vs[h]] = dvv.astype(dv_ref.dtype)
                dlg_ref[rows, ks[h]] = dlg
        for h in range(heads):
            dst_ref[vs[h], :] = dst[h]

    do_view = (do, heads * dv, 0)
    in_specs, row = _gla_specs([q, k, v, lg, do_view], per * chunk, n_blocks, True)
    in_specs.append(pl.BlockSpec((per, heads * dv, dk), lambda bi, ci: (row(bi, ci), 0, 0)))
    wide = lambda w: pl.BlockSpec((per * chunk, w), lambda bi, ci: (row(bi, ci), 0))
    return _pallas(
        body, side, name=name, grid=(t_rows // seq, n_blocks), in_specs=in_specs,
        out_specs=[wide(heads * dk), wide(heads * dk), wide(heads * dv), wide(heads * dk)],
        out_shape=[jax.ShapeDtypeStruct((t_rows, heads * dk), out_dtypes[0]),
                   jax.ShapeDtypeStruct((t_rows, heads * dk), out_dtypes[1]),
                   jax.ShapeDtypeStruct((t_rows, heads * dv), out_dtypes[2]),
                   jax.ShapeDtypeStruct((t_rows, heads * dk), F32)],
        scratch_shapes=[pltpu.VMEM((heads * dv, dk), F32)],
        semantics=("arbitrary", "arbitrary"),
    )(q[0], k[0], v[0], lg[0], do, states)


def _head_rms_gate(o, r, g, heads):
    d = o.shape[1] // heads
    parts = [_rms(o[:, h * d:(h + 1) * d], g) for h in range(heads)]
    return jnp.concatenate(parts, axis=1) * _silu(r)


def _gla_gate(glr, w_g2p, b_g2):
    return _log_sigmoid(_bdot(glr, w_g2p) + b_g2) * (1.0 / GLA_GATE_NORM)


def _glu(a, gate, b_in):
    d = a.shape[1]
    return (a + b_in[:, :d]) * _sigmoid(gate + b_in[:, d:])


def _ln_silu(y, g, b):
    return _silu(_layer_norm(y, g, b))


def _sgu(pre, b_in, ln_g, ln_b, w_s, b_st):
    d = pre.shape[1] // 2
    gd = d // SGU_GROUPS
    uv = _gelu(pre + b_in)
    u, v = uv[:, :d], _layer_norm(uv[:, d:], ln_g, ln_b)
    row = lax.broadcasted_iota(jnp.int32, (SGU_CHUNK, SGU_CHUNK), 0)
    col = lax.broadcasted_iota(jnp.int32, (SGU_CHUNK, SGU_CHUNK), 1)
    lane = lax.broadcasted_iota(jnp.int32, b_st.shape, 1)
    rows = []
    for c in range(pre.shape[0] // SGU_CHUNK):
        rs = slice(c * SGU_CHUNK, (c + 1) * SGU_CHUNK)
        parts = []
        for g in range(SGU_GROUPS):
            wg = jnp.where(col <= row, w_s[g], 0.0)
            bias = jnp.sum(jnp.where(lane == g, b_st, 0.0), axis=1, keepdims=True)
            parts.append(_bdot(wg, v[rs, g * gd:(g + 1) * gd]) + bias)
        rows.append(jnp.concatenate(parts, axis=1))
    s = rows[0] if len(rows) == 1 else jnp.concatenate(rows, axis=0)
    return u * s


def _hgrn_pre(q, f, table, layer):
    t = table - jnp.max(table, axis=0, keepdims=True)
    e = jnp.exp(t)
    sm = e / jnp.sum(e, axis=0, keepdims=True)
    rows = lax.broadcasted_iota(jnp.int32, table.shape, 0)
    lb = jnp.sum(jnp.where((rows >= 1) & (rows <= layer), sm, 0.0), axis=0, keepdims=True)
    sf = _sigmoid(f)
    return _silu(q), (1.0 - lb) * (1.0 - sf), jnp.log(lb + (1.0 - lb) * sf)


def _ffn_fwd(x, w, seq, sv):
    sv["h2"] = _rms_fwd(x, w["norm"], "ffn_norm")
    late = w.get("late")
    sv["u"] = _mm(sv["h2"], w["w_up"], out_dtype=BF16, name="ffn_up", side=late[0] if late else None)
    sv["a"] = _ffn_mid_fwd(sv["u"], w["w_dw"], seq, "ffn_mid", sv.pop("side", None))
    if late:
        sv["late_w"] = late[1]()
        w = dict(w, **sv["late_w"])
    return _mm(sv["a"], w["w_down"], add=x, name="ffn_down")


def _ffn_bwd(x, dy, w, seq, sv):
    g = {}
    da = _mm(dy, w["w_down_t"], out_dtype=BF16, name="ffn_down_dx")
    g["w_down"] = _mm_tn(sv["a"], dy, name="ffn_down_dw")
    early = sv.pop("put_early", None)
    side = _Side.join([sv.pop("side", None), early(g["w_down"]) if early else None])
    dug, duv, dwg, dwv = _ffn_mid_bwd(sv["u"], da, w["w_dw"], seq, "ffn_mid_bwd", side)
    g["w_dw"] = jnp.concatenate([dwg, dwv], axis=1)
    g["w_up_gate"] = _mm_tn(sv["h2"], dug, name="ffn_up_dw")
    g["w_up_val"] = _mm_tn(sv["h2"], duv, name="ffn_up_dw")
    dh = _mm(dug, w["w_up_t_gate"], out_dtype=F32, name="ffn_up_dx")
    dh = _mm(duv, w["w_up_t_val"], add=dh, out_dtype=BF16, name="ffn_up_dx2")
    dx, g["norm"] = _rms_bwd(x, w["norm"], dh, dy, "ffn_norm_bwd")
    return dx, g


def _gla_layer_fwd(x, h, w, seq, sv):
    d = x.shape[1]
    dkt = d // 2
    dk, dv = dkt // GLA_HEADS, d // GLA_HEADS
    proj = _mm(h, w["w_main"], out_dtype=F32, name="gla_in")
    glr = _mm(h, w["w_glr"], out_dtype=BF16, name="gla_in_g")
    (lg,), _ = _tile_call("gla_gate", lambda a, b, c: ([_gla_gate(a.astype(F32), b, c)], []), [glr],
                          [w["w_g2p"], w["b_g2"]], [(dkt, F32)], [], 512)
    q, k, v, r = (proj, dkt, 0), (proj, dkt, 1), (proj, d, 1), (proj, d, 2)
    o, states = _gla_fwd(q, k, v, (lg, dkt, 0), heads=GLA_HEADS, dk=dk, dv=dv, scale=dk ** -0.5, chunk=GLA_CHUNK,
                         seq=seq, name="gla_core", side=sv.pop("side", None))
    (o2,), _ = _tile_call("gla_post", lambda ov, rv, gv: ([_head_rms_gate(ov, rv.astype(F32), gv, GLA_HEADS)], []),
                          [o, r], [w["norm"]], [(d, BF16)], [], 256)
    sv.update(proj=proj, glr=glr, lg=lg, o=o, states=states, o2=o2)
    return _mm(o2, w["w_out"], add=x, name="mix_out")


def _gla_layer_bwd(h, dy, w, seq, sv):
    d = dy.shape[1]
    dkt = d // 2
    dk, dv = dkt // GLA_HEADS, d // GLA_HEADS
    proj, glr, lg, o = sv["proj"], sv["glr"], sv["lg"], sv["o"]
    g = {}
    do2 = _mm(dy, w["w_out_t"], out_dtype=F32, name="gla_out_dx")
    g["w_out"] = _mm_tn(sv["o2"], dy, name="mix_out_dw")

    def post_bwd(ov, rv, ctv, gv):
        _, vjp = jax.vjp(functools.partial(_head_rms_gate, heads=GLA_HEADS), ov, rv.astype(F32), gv)
        d_o, d_r, d_g = vjp(ctv.astype(F32))
        return [d_o, d_r], [d_g]

    (d_o, d_r), (g["norm"],) = _tile_call("gla_post_bwd", post_bwd, [o, (proj, d, 2), do2], [w["norm"]],
                                          [(d, F32), (d, BF16)], [w["norm"].shape], 256)
    q, k, v = (proj, dkt, 0), (proj, dkt, 1), (proj, d, 1)
    dq, dkk, dvv, dlg = _gla_bwd(q, k, v, (lg, dkt, 0), sv["states"], d_o, heads=GLA_HEADS, dk=dk, dv=dv,
                                 scale=dk ** -0.5, chunk=GLA_CHUNK, seq=seq, out_dtypes=(BF16, BF16, BF16),
                                 name="gla_core_bwd", side=sv.pop("side", None))

    def gate_bwd(glrv, ctv, wv, bv):
        _, vjp = jax.vjp(_gla_gate, glrv.astype(F32), wv, bv)
        d_glr, d_w, d_b = vjp(ctv)
        return [d_glr], [d_w, d_b]

    (dglr,), (g["w_g2p"], g["b_g2"]) = _tile_call("gla_gate_bwd", gate_bwd, [glr, dlg], [w["w_g2p"], w["b_g2"]],
                                                  [(LANES, BF16)], [w["w_g2p"].shape, w["b_g2"].shape], 512)
    dproj = jnp.concatenate([dq, dkk, dvv, d_r], axis=1)
    g["w_main"] = _mm_tn(h, dproj, name="gla_in_dw")
    g["w_glr"] = _mm_tn(h, dglr, name="gla_in_g_dw")
    dh = _mm(dproj, w["w_main_t"], out_dtype=F32, name="gla_in_dx")
    dh = _mm(dglr, w["w_glr_t"], add=dh, out_dtype=BF16, name="gla_in_g_dx")
    return dh, g


def _cv_layer_fwd(x, h, w, seq, sv):
    d = x.shape[1]
    pre = _mm(h, w["w_in"], out_dtype=BF16, name="cv_in")
    (y1,), _ = _tile_call("cv_glu", lambda a, gt, b: ([_glu(a.astype(F32), gt.astype(F32), b)], []),
                          [(pre, d, 0), (pre, d, 1)], [w["b_in"]], [(d, F32)], [], 512)
    y2 = _dwconv_fwd(y1, w["w_dw"], w["b_dw"], seq, "cv_conv", sv.pop("side", None))
    (y3,), _ = _tile_call("cv_ln", lambda y, a, b: ([_ln_silu(y, a, b)], []), [y2], [w["ln_g"], w["ln_b"]],
                          [(d, BF16)], [], 512)
    sv.update(pre=pre, y1=y1, y2=y2, y3=y3)
    return _mm(y3, w["w_out"], bias=w["b_out"], add=x, name="mix_out_b")


def _cv_layer_bwd(h, dy, w, seq, sv):
    d = dy.shape[1]
    pre = sv["pre"]
    g = {}
    dy3 = _mm(dy, w["w_out_t"], out_dtype=BF16, name="mix_out_dx")
    g["w_out"] = _mm_tn(sv["y3"], dy, name="mix_out_dw")
    g["b_out"] = _colsum(dy, "bias_out_dw")

    def ln_bwd(yv, ctv, av, bv):
        _, vjp = jax.vjp(_ln_silu, yv, av, bv)
        d_y, d_a, d_b = vjp(ctv.astype(F32))
        return [d_y], [d_a, d_b]

    (dy2,), (g["ln_g"], g["ln_b"]) = _tile_call("cv_ln_bwd", ln_bwd, [sv["y2"], dy3], [w["ln_g"], w["ln_b"]],
                                                [(d, F32)], [w["ln_g"].shape, w["ln_b"].shape], 512)
    dy1, g["w_dw"], g["b_dw"] = _dwconv_bwd(sv["y1"], dy2, w["w_dw"], seq, "cv_conv_bwd", sv.pop("side", None))

    def glu_bwd(av, gv, ctv, bv):
        _, vjp = jax.vjp(_glu, av.astype(F32), gv.astype(F32), bv)
        d_a, d_g, d_b = vjp(ctv)
        return [jnp.concatenate([d_a, d_g], axis=1)], [d_b]

    (dpre,), (g["b_in"],) = _tile_call("cv_glu_bwd", glu_bwd, [(pre, d, 0), (pre, d, 1), dy1], [w["b_in"]],
                                       [(2 * d, BF16)], [w["b_in"].shape], 512)
    g["w_in"] = _mm_tn(h, dpre, name="in2_dw")
    dh = _mm(dpre, w["w_in_t"], out_dtype=BF16, name="in2_dx")
    return dh, g


def _sg_layer_fwd(x, h, w, seq, sv):
    d = x.shape[1]
    pre = _mm(h, w["w_in"], out_dtype=BF16, name="sg_in")
    pars = [w["b_in"], w["ln_g"], w["ln_b"], w["w_s"], w["b_st"]]
    (p,), _ = _tile_call("sg_gate", lambda pv, *ps: ([_sgu(pv.astype(F32), *ps)], []), [pre], pars, [(d, BF16)], [],
                         SGU_CHUNK, side=sv.pop("side", None))
    sv.update(pre=pre, p=p)
    return _mm(p, w["w_out"], bias=w["b_out"], add=x, name="mix_out_b")


def _sg_layer_bwd(h, dy, w, seq, sv):
    d = dy.shape[1]
    g = {}
    dp = _mm(dy, w["w_out_t"], out_dtype=BF16, name="mix_out_dx")
    g["w_out"] = _mm_tn(sv["p"], dy, name="mix_out_dw")
    g["b_out"] = _colsum(dy, "bias_out_dw")
    pars = [w["b_in"], w["ln_g"], w["ln_b"], w["w_s"], w["b_st"]]

    def sgu_bwd(pv, ctv, *ps):
        _, vjp = jax.vjp(_sgu, pv.astype(F32), *ps)
        grads = vjp(ctv.astype(F32))
        return [grads[0]], list(grads[1:])

    (dpre,), (g["b_in"], g["ln_g"], g["ln_b"], g["w_s"], g["b_st"]) = _tile_call(
        "sg_gate_bwd", sgu_bwd, [sv["pre"], dp], pars, [(2 * d, BF16)], [p.shape for p in pars], SGU_CHUNK,
        side=sv.pop("side", None))
    g["w_in"] = _mm_tn(h, dpre, name="in2_dw")
    dh = _mm(dpre, w["w_in_t"], out_dtype=BF16, name="in2_dx")
    return dh, g


def _hg_layer_fwd(x, h, w, seq, sv, layer):
    d = x.shape[1]
    heads = d // HGRN_EXPAND
    proj = _mm(h, w["w_in"], out_dtype=BF16, name="hg_in")
    pre = functools.partial(_hgrn_pre, layer=layer)
    (qs, kk, lg), _ = _tile_call("hg_pre", lambda qv, fv, tb: (list(pre(qv.astype(F32), fv.astype(F32), tb)), []),
                                 [(proj, d, 0), (proj, d, 1)], [w["lb_table"]], [(d, BF16), (d, F32), (d, F32)], [], 256)
    o, states = _gla_fwd((qs, d, 0), (kk, d, 0), (proj, d, 2), (lg, d, 0), heads=heads, dk=HGRN_EXPAND,
                         dv=HGRN_EXPAND, scale=1.0, chunk=HGRN_CHUNK, seq=seq, name="hg_core",
                         side=sv.pop("side", None))
    (o2,), _ = _tile_call("hg_post", lambda ov, gv, nv: ([_head_rms_gate(ov, gv.astype(F32), nv, heads)], []),
                          [o, (proj, d, 3)], [w["norm"]], [(d, BF16)], [], 256)
    sv.update(proj=proj, qs=qs, kk=kk, lg=lg, o=o, states=states, o2=o2)
    return _mm(o2, w["w_out"], add=x, name="mix_out")


def _hg_layer_bwd(h, dy, w, seq, sv, layer):
    d = dy.shape[1]
    heads = d // HGRN_EXPAND
    proj = sv["proj"]
    g = {}
    do2 = _mm(dy, w["w_out_t"], out_dtype=BF16, name="mix_out_dx")
    g["w_out"] = _mm_tn(sv["o2"], dy, name="mix_out_dw")

    def post_bwd(ov, gv, ctv, nv):
        _, vjp = jax.vjp(functools.partial(_head_rms_gate, heads=heads), ov, gv.astype(F32), nv)
        d_o, d_g, d_n = vjp(ctv.astype(F32))
        return [d_o, d_g], [d_n]

    (d_o, d_gate), (g["norm"],) = _tile_call("hg_post_bwd", post_bwd, [sv["o"], (proj, d, 3), do2], [w["norm"]],
                                             [(d, F32), (d, BF16)], [w["norm"].shape], 256)
    dqs, dkk, di, dlg = _gla_bwd((sv["qs"], d, 0), (sv["kk"], d, 0), (proj, d, 2), (sv["lg"], d, 0), sv["states"], d_o,
                                 heads=heads, dk=HGRN_EXPAND, dv=HGRN_EXPAND, scale=1.0, chunk=HGRN_CHUNK, seq=seq,
                                 out_dtypes=(F32, F32, BF16), name="hg_core_bwd", side=sv.pop("side", None))

    def pre_bwd(qv, fv, c1, c2, c3, tb):
        _, vjp = jax.vjp(functools.partial(_hgrn_pre, layer=layer), qv.astype(F32), fv.astype(F32), tb)
        d_q, d_f, d_t = vjp((c1, c2, c3))
        return [jnp.concatenate([d_q, d_f], axis=1)], [d_t]

    (dqf,), (g["lb_table"],) = _tile_call("hg_pre_bwd", pre_bwd, [(proj, d, 0), (proj, d, 1), dqs, dkk, dlg],
                                          [w["lb_table"]], [(2 * d, BF16)], [w["lb_table"].shape], 256)
    dproj = jnp.concatenate([dqf, di, d_gate], axis=1)
    g["w_in"] = _mm_tn(h, dproj, name="hg_in_dw")
    dh = _mm(dproj, w["w_in_t"], out_dtype=BF16, name="hg_in_dx")
    return dh, g


_MIXERS = ("gla", "cv", "sg", "hg")


_BIG_KEYS = {"gla": ("w_main", "w_glr", "w_out"), "cv": ("w_in", "w_out"), "sg": ("w_in", "w_out"), "hg": ("w_in", "w_out"),
             "ffn": ("w_up_gate", "w_up_val", "w_down")}


def _local_step(x, target, w, seq, get_big, put_big, ride=lambda kind, layer, forward: None, put_early=None):
    depth = w["norm_mix"].shape[0]
    d = x.shape[1]
    saved, big = [], {}
    for layer in range(depth):
        mixer = _MIXERS[layer % 4]
        sv = {"x_in": x, "side": ride(mixer, layer, True)}
        sv["h"] = _rms_fwd(x, w["norm_mix"][layer:layer + 1], "mix_norm")
        big[mixer, layer] = get_big(mixer, layer)
        wm = dict(w[mixer], **big[mixer, layer])
        if mixer == "gla":
            x = _gla_layer_fwd(x, sv["h"], wm, seq, sv)
        elif mixer == "cv":
            x = _cv_layer_fwd(x, sv["h"], wm, seq, sv)
        elif mixer == "sg":
            x = _sg_layer_fwd(x, sv["h"], wm, seq, sv)
        else:
            x = _hg_layer_fwd(x, sv["h"], wm, seq, sv, layer)
        sv["x_mid"] = x
        big["ffn", layer] = get_big("ffn", layer)
        sv["ffn"] = {"side": ride("ffn", layer, True)}
        wf = dict(w["ffn"][layer], norm=w["norm_ffn"][layer:layer + 1], **big["ffn", layer])
        x = _ffn_fwd(x, wf, seq, sv["ffn"])
        big["ffn", layer].pop("late", None)
        big["ffn", layer].update(sv["ffn"].pop("late_w", {}))
        saved.append(sv)

    def head(xv, tv, gv):
        y, vjp = jax.vjp(_rms, xv, gv)
        err = y - tv
        dx, dg = vjp(err * (1.0 / d))
        part = 0.5 * jnp.sum(jnp.mean(err * err, axis=-1, keepdims=True), axis=0, keepdims=True)
        return [dx], [jnp.broadcast_to(part, (1, LANES)), dg]

    (dx,), (loss, g_final) = _tile_call("loss_head", head, [x, target], [w["norm_final"]], [(d, F32)],
                                        [(1, LANES), (1, d)], 512)
    grads = {"norm_final": g_final, "norm_mix": [None] * depth, "norm_ffn": [None] * depth, "ffn": [None] * depth}
    for layer in reversed(range(depth)):
        mixer = _MIXERS[layer % 4]
        sv = saved[layer]
        wf = dict(w["ffn"][layer], norm=w["norm_ffn"][layer:layer + 1], **big["ffn", layer])
        sv["ffn"]["side"] = ride("ffn", layer, False)
        if put_early is not None:
            sv["ffn"]["put_early"] = functools.partial(put_early, layer)
        dx, gf = _ffn_bwd(sv["x_mid"], dx, wf, seq, sv["ffn"])
        put_big("ffn", layer, {k: gf.pop(k) for k in _BIG_KEYS["ffn"]})
        sv["side"] = ride(mixer, layer, False)
        grads["norm_ffn"][layer] = gf.pop("norm")
        grads["ffn"][layer] = gf
        wm = dict(w[mixer], **big[mixer, layer])
        if mixer == "gla":
            dh, gm = _gla_layer_bwd(sv["h"], dx, wm, seq, sv)
        elif mixer == "cv":
            dh, gm = _cv_layer_bwd(sv["h"], dx, wm, seq, sv)
        elif mixer == "sg":
            dh, gm = _sg_layer_bwd(sv["h"], dx, wm, seq, sv)
        else:
            dh, gm = _hg_layer_bwd(sv["h"], dx, wm, seq, sv, layer)
        put_big(mixer, layer, {k: gm.pop(k) for k in _BIG_KEYS[mixer]})
        grads[mixer] = gm
        dx, grads["norm_mix"][layer] = _rms_bwd(sv["x_in"], w["norm_mix"][layer:layer + 1], dh, dx, "mix_norm_bwd")
    return loss, dx, grads


def _prep_small(p):
    row = lambda a: a.reshape(1, -1).astype(F32)
    w = {"norm_mix": p["norm_mix"].astype(F32), "norm_ffn": p["norm_ffn"].astype(F32), "norm_final": row(p["norm_final"])}
    w["gla"] = dict(w_g2p=jnp.pad(p["gla_w_g2"][0].astype(F32), ((0, LANES - GLA_RANK), (0, 0))), b_g2=row(p["gla_b_g2"]),
                    norm=row(p["gla_norm"]))
    w["cv"] = dict(b_in=row(p["cv_b_in"]), w_dw=p["cv_w_dw"][0].astype(F32), b_dw=row(p["cv_b_dw"]), ln_g=row(p["cv_ln_g"]),
                   ln_b=row(p["cv_ln_b"]), b_out=row(p["cv_b_out"]))
    b_st = jnp.pad(p["sg_b_s"][0].astype(F32).T, ((0, 0), (0, LANES - SGU_GROUPS)))
    w["sg"] = dict(b_in=row(p["sg_b_in"]), ln_g=row(p["sg_ln_g"]), ln_b=row(p["sg_ln_b"]), w_s=p["sg_w_s"][0].astype(F32),
                   b_st=b_st, b_out=row(p["sg_b_out"]))
    w["hg"] = dict(lb_table=p["hg_lb_table"].astype(F32), norm=row(p["hg_norm"]))
    w["ffn"] = [dict(w_dw=p["ffn_w_dw"][layer].astype(F32)) for layer in range(p["ffn_w_dw"].shape[0])]
    return w


def _small_grads(g):
    gla, cv, sg, hg = g["gla"], g["cv"], g["sg"], g["hg"]
    return {
        "norm_mix": jnp.concatenate(g["norm_mix"], axis=0), "norm_ffn": jnp.concatenate(g["norm_ffn"], axis=0),
        "norm_final": g["norm_final"][0],
        "gla_w_g2": gla["w_g2p"][:GLA_RANK][None], "gla_b_g2": gla["b_g2"], "gla_norm": gla["norm"],
        "cv_b_in": cv["b_in"], "cv_w_dw": cv["w_dw"][None], "cv_b_dw": cv["b_dw"], "cv_ln_g": cv["ln_g"],
        "cv_ln_b": cv["ln_b"], "cv_b_out": cv["b_out"],
        "sg_b_in": sg["b_in"], "sg_ln_g": sg["ln_g"], "sg_ln_b": sg["ln_b"], "sg_w_s": sg["w_s"][None],
        "sg_b_s": sg["b_st"][:, :SGU_GROUPS].T[None], "sg_b_out": sg["b_out"],
        "hg_lb_table": hg["lb_table"], "hg_norm": hg["norm"],
        "ffn_w_dw": jnp.stack([f["w_dw"] for f in g["ffn"]]),
    }


def _oriented(kind, mats):
    if kind == "ffn":
        (up, up_t), (down, down_t) = mats["w_up"], mats["w_down"]
        f = down.shape[0]
        return dict(w_up=up, w_up_t_gate=up_t[:f], w_up_t_val=up_t[f:], w_down=down, w_down_t=down_t)
    (w_in, w_in_t), (w_out, w_out_t) = mats["w_in"], mats["w_out"]
    if kind != "gla":
        return dict(w_in=w_in, w_in_t=w_in_t, w_out=w_out, w_out_t=w_out_t)
    n_main = w_in.shape[1] - GLA_RANK
    return dict(w_main=w_in[:, :n_main], w_glr=jnp.pad(w_in[:, n_main:], ((0, 0), (0, LANES - GLA_RANK))),
                w_main_t=w_in_t[:n_main], w_glr_t=jnp.pad(w_in_t[n_main:], ((0, LANES - GLA_RANK), (0, 0))),
                w_out=w_out, w_out_t=w_out_t)


def _all_gather(x, *, name):
    m_per, n = x.shape

    def body(x_ref, out_ref, send_sems, recv_sems, local_sem):
        mx, my, mc = lax.axis_index("x"), lax.axis_index("y"), lax.axis_index("c")
        me, sibling = (mx, my, mc), (mx, my, 1 - mc)
        chips = [(1 - mx, my), (mx, 1 - my), (1 - mx, 1 - my)]

        def rows(px, py, pc):
            return out_ref.at[pl.ds((4 * px + 2 * py + pc) * m_per, m_per), :]

        def copy(k, block, to, src=None):
            return pltpu.make_async_remote_copy(
                src_ref=rows(*block) if src is None else src, dst_ref=rows(*block), send_sem=send_sems.at[k],
                recv_sem=recv_sems.at[k], device_id=to, device_id_type=MESH)

        mine = pltpu.make_async_copy(x_ref, rows(*me), local_sem)
        mine.start()
        first = [copy(0, me, sibling, src=x_ref)]
        first += [copy(1 + j, me, (*chip, mc), src=x_ref) for j, chip in enumerate(chips)]
        for cp in first:
            cp.start()
        passed = [copy(4 + j, (*chip, mc), sibling) for j, chip in enumerate(chips)]
        for j, chip in enumerate(chips):
            copy(1 + j, (*chip, mc), me).wait_recv()
            passed[j].start()
        copy(0, sibling, me).wait_recv()
        for j, chip in enumerate(chips):
            copy(4 + j, (*chip, 1 - mc), me).wait_recv()
        for cp in first + passed:
            cp.wait_send()
        mine.wait()

    return pl.pallas_call(
        body, name=name, out_shape=jax.ShapeDtypeStruct((N_DEV * m_per, n), x.dtype),
        in_specs=[pl.BlockSpec(memory_space=pltpu.VMEM)], out_specs=pl.BlockSpec(memory_space=pltpu.VMEM),
        scratch_shapes=[pltpu.SemaphoreType.DMA((7,)), pltpu.SemaphoreType.DMA((7,)), pltpu.SemaphoreType.DMA],
    )(x)


def _my_index():
    return 4 * lax.axis_index("x") + 2 * lax.axis_index("y") + lax.axis_index("c")


def _gather_stage(srcs, *, name):
    n = len(srcs)

    def body(*refs):
        x_refs, out_refs = refs[:n], refs[n:2 * n]
        send_sems, recv_sems, local_sems = refs[2 * n:]
        mx, my, mc = lax.axis_index("x"), lax.axis_index("y"), lax.axis_index("c")
        me, sibling = (mx, my, mc), (mx, my, 1 - mc)
        chips = [(1 - mx, my), (mx, 1 - my), (1 - mx, 1 - my)]

        def slot(i, px, py, pc):
            return out_refs[i].at[4 * px + 2 * py + pc]

        def copy(i, k, block, to, src=None):
            return pltpu.make_async_remote_copy(
                src_ref=slot(i, *block) if src is None else src, dst_ref=slot(i, *block), send_sem=send_sems.at[7 * i + k],
                recv_sem=recv_sems.at[7 * i + k], device_id=to, device_id_type=MESH)

        mine = [pltpu.make_async_copy(x_refs[i], slot(i, *me), local_sems.at[i]) for i in range(n)]
        first = [copy(i, 0, me, sibling, src=x_refs[i]) for i in range(n)]
        first += [copy(i, 1 + j, me, (*chip, mc), src=x_refs[i]) for j, chip in enumerate(chips) for i in range(n)]
        for cp in mine + first:
            cp.start()
        passed = []
        for j, chip in enumerate(chips):
            for i in range(n):
                copy(i, 1 + j, (*chip, mc), me).wait_recv()
                passed.append(copy(i, 4 + j, (*chip, mc), sibling))
                passed[-1].start()
        for i in range(n):
            copy(i, 0, sibling, me).wait_recv()
            for j, chip in enumerate(chips):
                copy(i, 4 + j, (*chip, 1 - mc), me).wait_recv()
        for cp in first + passed:
            cp.wait_send()
        for cp in mine:
            cp.wait()

    any_space = pl.BlockSpec(memory_space=pl.ANY)
    return pl.pallas_call(
        body, name=name, out_shape=[jax.ShapeDtypeStruct((N_DEV,) + s.shape, s.dtype) for s in srcs],
        in_specs=[any_space] * n, out_specs=[any_space] * n,
        scratch_shapes=[pltpu.SemaphoreType.DMA((7 * n,)), pltpu.SemaphoreType.DMA((7 * n,)), pltpu.SemaphoreType.DMA((n,))],
    )(*srcs)


def _scatter_stage(srcs, *, name):
    n = len(srcs)

    def body(*refs):
        x_refs, out_refs = refs[:n], refs[n:2 * n]
        send_sems, recv_sems, local_sems = refs[2 * n:]
        mx, my, mc = lax.axis_index("x"), lax.axis_index("y"), lax.axis_index("c")
        me = 4 * mx + 2 * my + mc
        mine = [pltpu.make_async_copy(x_refs[i].at[me], out_refs[i].at[me], local_sems.at[i]) for i in range(n)]
        for cp in mine:
            cp.start()
        sends, recvs = [], []
        for k in range(1, N_DEV):
            px = 1 - mx if k & 4 else mx
            py = 1 - my if k & 2 else my
            pc = 1 - mc if k & 1 else mc
            peer = 4 * px + 2 * py + pc
            for i in range(n):
                sems = dict(send_sem=send_sems.at[7 * i + k - 1], recv_sem=recv_sems.at[7 * i + k - 1],
                            device_id=(px, py, pc), device_id_type=MESH)
                sends.append(pltpu.make_async_remote_copy(src_ref=x_refs[i].at[peer], dst_ref=out_refs[i].at[me], **sems))
                recvs.append(pltpu.make_async_remote_copy(src_ref=x_refs[i].at[me], dst_ref=out_refs[i].at[peer], **sems))
                sends[-1].start()
        for cp in recvs:
            cp.wait_recv()
        for cp in sends:
            cp.wait_send()
        for cp in mine:
            cp.wait()

    any_space = pl.BlockSpec(memory_space=pl.ANY)
    return pl.pallas_call(
        body, name=name, out_shape=[jax.ShapeDtypeStruct(s.shape, s.dtype) for s in srcs],
        in_specs=[any_space] * n, out_specs=[any_space] * n,
        scratch_shapes=[pltpu.SemaphoreType.DMA((7 * n,)), pltpu.SemaphoreType.DMA((7 * n,)), pltpu.SemaphoreType.DMA((n,))],
    )(*srcs)


def _adamw_math(g, w, m, v):
    c1, c2 = 1.0 - ADAM_B1 ** ADAM_STEP, 1.0 - ADAM_B2 ** ADAM_STEP
    m_new = ADAM_B1 * m + (1.0 - ADAM_B1) * g
    v_new = ADAM_B2 * v + (1.0 - ADAM_B2) * (g * g)
    delta = -ADAM_LR * ((m_new / c1) / (jnp.sqrt(v_new / c2) + ADAM_EPS) + ADAM_WD * w)
    return delta, m_new, v_new


def _adamw_big(slots, w, m, v, layer, *, name):
    _, r, c = slots.shape
    tr = _divisor_tile(r, max(8, (200 * 1024) // c // 8 * 8), 8)

    def body(s_ref, w_ref, m_ref, v_ref, g_out, d_out, m_out, v_out):
        g = s_ref[0].astype(F32)
        for p in range(1, N_DEV):
            g = g + s_ref[p].astype(F32)
        g_out[...] = g
        d_out[...], m_out[...], v_out[...] = _adamw_math(g, w_ref[...], m_ref[...], v_ref[...])

    blk = pl.BlockSpec((tr, c), lambda i: (i, 0))
    lay = pl.BlockSpec((None, tr, c), lambda i: (layer, i, 0))
    return pl.pallas_call(
        body, name=name, grid=(r // tr,), in_specs=[pl.BlockSpec((N_DEV, tr, c), lambda i: (0, i, 0)), lay, lay, lay],
        out_specs=[blk] * 4, out_shape=[jax.ShapeDtypeStruct((r, c), F32)] * 4,
        compiler_params=pltpu.CompilerParams(dimension_semantics=("parallel",)),
    )(slots, w, m, v)


def _sum_small(got, r_re, r_sh, *, name):
    per_dev = r_re + N_DEV * r_sh

    def body(got_ref, re_ref, sh_ref):
        mine = r_re + _my_index() * r_sh
        acc_re = got_ref[0:r_re, :]
        acc_sh = got_ref[pl.ds(pl.multiple_of(mine, 8), r_sh), :]
        for p in range(1, N_DEV):
            acc_re = acc_re + got_ref[p * per_dev:p * per_dev + r_re, :]
            acc_sh = acc_sh + got_ref[pl.ds(pl.multiple_of(p * per_dev + mine, 8), r_sh), :]
        re_ref[...] = acc_re
        sh_ref[...] = acc_sh

    return pl.pallas_call(body, name=name, out_shape=[jax.ShapeDtypeStruct((r_re, LANES), F32),
                                                       jax.ShapeDtypeStruct((r_sh, LANES), F32)])(got)


def _adamw_small(gs, ws, ms, vs, *, name):
    n = len(gs)

    def body(*refs):
        ins, outs = refs[:4 * n], refs[4 * n:]
        for i in range(n):
            res = _adamw_math(ins[i][...], ins[n + i][...], ins[2 * n + i][...], ins[3 * n + i][...])
            for j in range(3):
                outs[j * n + i][...] = res[j]

    out = pl.pallas_call(body, name=name, out_shape=[jax.ShapeDtypeStruct(a.shape, F32) for a in ws] * 3)(*gs, *ws, *ms, *vs)
    return out[:n], out[n:2 * n], out[2 * n:]


def _layout(shapes, row_align, total_align):
    lay, off = {}, 0
    for name, shape in shapes.items():
        size = int(np.prod(shape))
        rows = -(-size // LANES)
        rows = -(-rows // row_align) * row_align
        lay[name] = (off, rows, size, tuple(shape))
        off += rows
    return lay, -(-off // total_align) * total_align


def _pack(arrs, lay, total, dtype, lead=()):
    parts = []
    nl = len(lead)
    for name, (off, rows, size, shape) in lay.items():
        flat = arrs[name].astype(dtype).reshape(*lead, size)
        parts.append(jnp.pad(flat, [(0, 0)] * nl + [(0, rows * LANES - size)]).reshape(*lead, rows, LANES))
    used = sum(v[1] for v in lay.values())
    if total > used:
        parts.append(jnp.zeros((*lead, total - used, LANES), dtype))
    return jnp.concatenate(parts, axis=nl)


def _unpack(buf, lay, lead=()):
    out = {}
    nl = len(lead)
    for name, (off, rows, size, shape) in lay.items():
        part = lax.slice_in_dim(buf, off, off + rows, axis=nl).reshape(*lead, rows * LANES)
        out[name] = lax.slice_in_dim(part, 0, size, axis=nl).reshape(*lead, *shape)
    return out


_SHARD_AXIS = {
    "norm_mix": None, "norm_ffn": None, "norm_final": None, "gla_w_in": 2, "gla_w_g2": 2, "gla_b_g2": None,
    "gla_norm": None, "gla_w_out": 1, "cv_w_in": 2, "cv_b_in": 1, "cv_w_dw": 2, "cv_b_dw": 1, "cv_ln_g": 1,
    "cv_ln_b": 1, "cv_w_out": 1, "cv_b_out": 1, "sg_w_in": 2, "sg_b_in": 1, "sg_ln_g": 1, "sg_ln_b": 1, "sg_w_s": None,
    "sg_b_s": None, "sg_w_out": 1, "sg_b_out": 1, "hg_w_in": 2, "hg_lb_table": None, "hg_norm": None, "hg_w_out": 1,
    "ffn_w_up": 2, "ffn_w_dw": 2, "ffn_w_down": 1,
}
_MATMUL_WEIGHTS = ("gla_w_in", "gla_w_out", "cv_w_in", "cv_w_out", "sg_w_in", "sg_w_out", "hg_w_in", "hg_w_out",
                   "ffn_w_up", "ffn_w_down")
_NAMES = tuple(_SHARD_AXIS)


def kernel(x, norm_mix, norm_ffn, norm_final, gla_w_in, gla_w_g2, gla_b_g2, gla_norm, gla_w_out, cv_w_in, cv_b_in, cv_w_dw, cv_b_dw, cv_ln_g, cv_ln_b, cv_w_out, cv_b_out, sg_w_in, sg_b_in, sg_ln_g, sg_ln_b, sg_w_s, sg_b_s, sg_w_out, sg_b_out, hg_w_in, hg_lb_table, hg_norm, hg_w_out, ffn_w_up, ffn_w_dw, ffn_w_down, loss_target, m_norm_mix, m_norm_ffn, m_norm_final, m_gla_w_in, m_gla_w_g2, m_gla_b_g2, m_gla_norm, m_gla_w_out, m_cv_w_in, m_cv_b_in, m_cv_w_dw, m_cv_b_dw, m_cv_ln_g, m_cv_ln_b, m_cv_w_out, m_cv_b_out, m_sg_w_in, m_sg_b_in, m_sg_ln_g, m_sg_ln_b, m_sg_w_s, m_sg_b_s, m_sg_w_out, m_sg_b_out, m_hg_w_in, m_hg_lb_table, m_hg_norm, m_hg_w_out, m_ffn_w_up, m_ffn_w_dw, m_ffn_w_down, v_norm_mix, v_norm_ffn, v_norm_final, v_gla_w_in, v_gla_w_g2, v_gla_b_g2, v_gla_norm, v_gla_w_out, v_cv_w_in, v_cv_b_in, v_cv_w_dw, v_cv_b_dw, v_cv_ln_g, v_cv_ln_b, v_cv_w_out, v_cv_b_out, v_sg_w_in, v_sg_b_in, v_sg_ln_g, v_sg_ln_b, v_sg_w_s, v_sg_b_s, v_sg_w_out, v_sg_b_out, v_hg_w_in, v_hg_lb_table, v_hg_norm, v_hg_w_out, v_ffn_w_up, v_ffn_w_dw, v_ffn_w_down):
    local = dict(locals())
    wts = {n: local[n] for n in _NAMES}
    mom = {n: local["m_" + n] for n in _NAMES}
    var = {n: local["v_" + n] for n in _NAMES}
    small_all = [n for n in _NAMES if n not in _MATMUL_WEIGHTS]
    small_sharded = [n for n in small_all if _SHARD_AXIS[n] is not None]
    bsz, seq, d = x.shape
    depth = norm_mix.shape[0]

    stages = {}
    for layer in range(depth):
        kind = _MIXERS[layer % 4]
        stages[kind, layer] = {"w_in": (kind + "_w_in", layer // 4), "w_out": (kind + "_w_out", layer // 4)}
        stages["ffn", layer] = {"w_up": ("ffn_w_up", layer), "w_down": ("ffn_w_down", layer)}

    order = list(stages)
    shards = lambda stage: [wts[nm][idx].astype(BF16) for nm, idx in stages[stage].values()]
    gathers = {order[0]: _Side(shards(order[0]), False)}
    gathers[order[0]].lands = _gather_stage(gathers[order[0]].srcs, name="gather_first")
    scatters, waiting, down_gathers, down_scatters = {}, [], {}, {}

    def ride(kind, layer, forward):
        if not forward:
            return waiting.pop() if waiting else None
        at = order.index((kind, layer)) + 1
        if at == len(order):
            return None
        srcs = shards(order[at])
        if order[at][0] == "ffn":
            down_gathers[order[at][1]] = _Side(srcs[1:], False)
            srcs = srcs[:1]
        gathers[order[at]] = _Side(srcs, False)
        return gathers[order[at]]

    lay_sw, r_sw = _layout({n: wts[n].shape for n in small_sharded}, 8, 8)
    got_sw = _all_gather(_pack(wts, lay_sw, r_sw, F32), name="gather_small_weights")
    parts = _unpack(got_sw.reshape(N_DEV, r_sw, LANES), lay_sw, (N_DEV,))
    full_small = {n: wts[n] for n in small_all if _SHARD_AXIS[n] is None}
    for n in small_sharded:
        ax, shape = _SHARD_AXIS[n], wts[n].shape
        full_small[n] = jnp.moveaxis(parts[n], 0, ax).reshape(shape[:ax] + (N_DEV * shape[ax],) + shape[ax + 1:])

    def full_size(nm, land):
        _, r, c = land.shape
        if _SHARD_AXIS[nm] == 2:
            return land.transpose(1, 0, 2).reshape(r, N_DEV * c), land.transpose(0, 2, 1).reshape(N_DEV * c, r)
        return land.reshape(N_DEV * r, c), land.reshape(N_DEV * r, c).T

    def get_big(kind, layer):
        names = [nm for nm, _ in stages[kind, layer].values()]
        lands = gathers[kind, layer].lands
        if kind != "ffn" or layer not in down_gathers:
            return _oriented(kind, {key: full_size(nm, land) for key, nm, land in zip(stages[kind, layer], names, lands)})
        up, up_t = full_size(names[0], lands[0])
        f = up.shape[1] // 2

        def down_landed():
            down, down_t = full_size(names[1], down_gathers[layer].lands[0])
            return dict(w_down=down, w_down_t=down_t)

        return dict(w_up=up, w_up_t_gate=up_t[:f], w_up_t_val=up_t[f:], late=(down_gathers[layer], down_landed))

    def put_big(kind, layer, g):
        if kind == "ffn":
            k, f = g["w_up_gate"].shape
            halves = [g[key].reshape(k, N_DEV // 2, 2 * f // N_DEV) for key in ("w_up_gate", "w_up_val")]
            w_in = jnp.concatenate(halves, axis=1)
        else:
            w_in = jnp.concatenate([g["w_main"], g["w_glr"][:, :GLA_RANK]], axis=1) if kind == "gla" else g["w_in"]
            w_in = w_in.reshape(w_in.shape[0], N_DEV, w_in.shape[1] // N_DEV)
        sends = [w_in.transpose(1, 0, 2).astype(BF16)]
        if kind != "ffn":
            sends.append(row_slots(g["w_out"]))
        scatters[kind, layer] = _Side(sends, True)
        waiting.append(scatters[kind, layer])

    def row_slots(grad):
        return grad.reshape(N_DEV, grad.shape[0] // N_DEV, grad.shape[1]).astype(BF16)

    def put_early(layer, grad_w_down):
        down_scatters[layer] = _Side([row_slots(grad_w_down)], True)
        return down_scatters[layer]

    loss, dx, grads = _local_step(x.reshape(bsz * seq, d), loss_target.reshape(bsz * seq, d), _prep_small(full_small), seq,
                                  get_big, put_big, ride, put_early)
    loss = lax.psum(loss[0, 0], ("x", "y", "c"))

    gs = _small_grads(grads)
    small_repl = [n for n in small_all if _SHARD_AXIS[n] is None]
    lay_re, r_re = _layout({n: wts[n].shape for n in small_repl}, 8, 8)
    slots = {}
    for n in small_sharded:
        ax, shape = _SHARD_AXIS[n], wts[n].shape
        slots[n] = jnp.moveaxis(gs[n].reshape(shape[:ax] + (N_DEV, shape[ax]) + shape[ax + 1:]), ax, 0)
    sent = jnp.concatenate([_pack(gs, lay_re, r_re, F32), _pack(slots, lay_sw, r_sw, F32, (N_DEV,)).reshape(-1, LANES)])
    sum_re, sum_sh = _sum_small(_all_gather(sent, name="gather_small_grads"), r_re, r_sw, name="sum_small_grads")
    g_own = _unpack(sum_re, lay_re)
    g_own.update(_unpack(sum_sh, lay_sw))
    two_d = lambda a: a.reshape(-1, a.shape[-1])
    upd = _adamw_small(*[[two_d(src[n]) for n in small_all] for src in (g_own, wts, mom, var)], name="adamw_small")
    results = {n: [g_own[n]] + [part[i].reshape(wts[n].shape) for part in upd] for i, n in enumerate(small_all)}

    per_layer = {}
    for (kind, layer), side in scatters.items():
        lands = side.lands if side.lands is not None else _scatter_stage(side.srcs, name="scatter_last")
        if kind == "ffn":
            lands = list(lands) + down_scatters[layer].lands
        for (nm, idx), land in zip(stages[kind, layer].values(), lands):
            three_d = lambda a: a.reshape((a.shape[0],) + land.shape[1:])
            per_layer.setdefault(nm, {})[idx] = _adamw_big(land, three_d(wts[nm]), three_d(mom[nm]), three_d(var[nm]), idx,
                                                           name="adamw_" + nm)
    for nm, by_idx in per_layer.items():
        outs = [by_idx[i] for i in range(len(by_idx))]
        results[nm] = [(outs[0][j] if len(outs) == 1 else jnp.stack([o[j] for o in outs])).reshape(wts[nm].shape)
                       for j in range(4)]
    out = [loss, dx.reshape(bsz, seq, d)]
    for j in range(4):
        out += [results[n][j] for n in _NAMES]
    return tuple(out)
```

```python
import functools
import math

import jax
import jax.numpy as jnp
import numpy as np
from jax import lax
from jax.experimental import pallas as pl
from jax.experimental.pallas import tpu as pltpu

F32 = jnp.float32
BF16 = jnp.bfloat16
EPS = 1e-6
N_DEV = 8
LANES = 128
SUBLANES_BF16 = 16
HALO = 32
GLA_HEADS, GLA_RANK, GLA_GATE_NORM, GLA_CHUNK = 4, 16, 16.0, 64
SGU_CHUNK, SGU_GROUPS = 128, 8
HGRN_EXPAND, HGRN_CHUNK = 128, 64
FFN_CONV_WIDTH = 3
ADAM_LR, ADAM_B1, ADAM_B2, ADAM_EPS, ADAM_WD, ADAM_STEP = 0.001, 0.9, 0.999, 1e-08, 0.01, 10
MESH = pl.DeviceIdType.MESH


def _sigmoid(x):
    return 0.5 * (jnp.tanh(0.5 * x) + 1.0)


def _silu(x):
    return x * _sigmoid(x)


def _log_sigmoid(x):
    return jnp.minimum(x, 0.0) - jnp.log(1.0 + jnp.exp(-jnp.abs(x)))


def _gelu(x):
    return 0.5 * x * (1.0 + jnp.tanh(math.sqrt(2.0 / math.pi) * (x + 0.044715 * (x * x * x))))


def _rms(x, g):
    return x * lax.rsqrt(jnp.mean(x * x, axis=-1, keepdims=True) + EPS) * g


def _layer_norm(x, g, b):
    xc = x - jnp.mean(x, axis=-1, keepdims=True)
    return xc * lax.rsqrt(jnp.mean(xc * xc, axis=-1, keepdims=True) + EPS) * g + b


def _dot_raw(a, b, dims):
    return lax.dot_general(a.astype(BF16), b.astype(BF16), (dims, ((), ())), preferred_element_type=F32)


@jax.custom_vjp
def _bdot(a, b):
    return _dot_raw(a, b, ((1,), (0,)))


@jax.custom_vjp
def _bdot_nt(a, b):
    return _dot_raw(a, b, ((1,), (1,)))


@jax.custom_vjp
def _bdot_tn(a, b):
    return _dot_raw(a, b, ((0,), (0,)))


_bdot.defvjp(lambda a, b: (_bdot(a, b), (a, b)), lambda r, g: (_bdot_nt(g, r[1]), _bdot_tn(r[0], g)))
_bdot_nt.defvjp(lambda a, b: (_bdot_nt(a, b), (a, b)), lambda r, g: (_bdot(g, r[1]), _bdot_tn(g, r[0])))
_bdot_tn.defvjp(lambda a, b: (_bdot_tn(a, b), (a, b)), lambda r, g: (_bdot_nt(r[1], g), _bdot(r[0], g)))


def _scan_rows(x, reverse):
    n = x.shape[0]
    row = lax.broadcasted_iota(jnp.int32, x.shape, 0)
    step = 1
    while step < n:
        if reverse:
            x = x + jnp.where(row < n - step, pltpu.roll(x, n - step, 0), 0.0)
        else:
            x = x + jnp.where(row >= step, pltpu.roll(x, step, 0), 0.0)
        step *= 2
    return x


@jax.custom_vjp
def _cumsum_rows(x):
    return _scan_rows(x, False)


_cumsum_rows.defvjp(lambda x: (_scan_rows(x, False), None), lambda _, g: (_scan_rows(g, True),))


def _divisor_tile(n, cap, unit):
    if n <= cap:
        return n
    best = None
    for t in range(unit, cap + 1, unit):
        if n % t == 0:
            best = t
    assert best is not None, (n, cap, unit)
    return best


def _const_map(nd):
    return lambda *_: (0,) * nd


class _Side:
    def __init__(self, srcs, scatter, parts=()):
        self.srcs, self.scatter, self.lands, self.parts = list(srcs), scatter, None, list(parts)

    @staticmethod
    def join(sides):
        sides = [s for s in sides if s is not None]
        if len(sides) < 2:
            return sides[0] if sides else None
        assert len({s.scatter for s in sides}) == 1
        return _Side([a for s in sides for a in s.srcs], sides[0].scatter, sides)

    def landed(self, lands):
        self.lands = list(lands)
        at = 0
        for part in self.parts:
            part.landed(self.lands[at:at + len(part.srcs)])
            at += len(part.srcs)


def _pallas(body, side, *, name, grid, in_specs, out_specs, out_shape, scratch_shapes=(), semantics):
    if side is None:
        return pl.pallas_call(body, name=name, grid=grid, in_specs=in_specs, out_specs=out_specs, out_shape=out_shape,
                              scratch_shapes=list(scratch_shapes),
                              compiler_params=pltpu.CompilerParams(dimension_semantics=semantics))
    single = not isinstance(out_shape, (list, tuple))
    out_specs, out_shape = ([out_specs], [out_shape]) if single else (list(out_specs), list(out_shape))
    n, n_in, n_out, n_scr = len(side.srcs), len(in_specs), len(out_shape), len(scratch_shapes)
    lands = [jax.ShapeDtypeStruct((N_DEV,) + (s.shape[1:] if side.scatter else s.shape), s.dtype) for s in side.srcs]

    def body2(*refs):
        x_refs, land_refs = refs[n_in:n_in + n], refs[n_in + n + n_out:n_in + 2 * n + n_out]
        send_sems, recv_sems, local_sems = refs[-3:]
        steps = [pl.program_id(a) for a in range(len(grid))]
        first = functools.reduce(jnp.logical_and, [s == 0 for s in steps])
        last = functools.reduce(jnp.logical_and, [s == g - 1 for s, g in zip(steps, grid)])

        def copies():
            mx, my, mc = lax.axis_index("x"), lax.axis_index("y"), lax.axis_index("c")
            me = 4 * mx + 2 * my + mc
            mine = [pltpu.make_async_copy(x_refs[i].at[me] if side.scatter else x_refs[i], land_refs[i].at[me],
                                          local_sems.at[i]) for i in range(n)]
            sends, recvs = [], []
            for k in range(1, N_DEV):
                px = 1 - mx if k & 4 else mx
                py = 1 - my if k & 2 else my
                pc = 1 - mc if k & 1 else mc
                peer = 4 * px + 2 * py + pc
                for i in range(n):
                    sems = dict(send_sem=send_sems.at[7 * i + k - 1], recv_sem=recv_sems.at[7 * i + k - 1],
                                device_id=(px, py, pc), device_id_type=MESH)
                    src = x_refs[i].at[peer] if side.scatter else x_refs[i]
                    sends.append(pltpu.make_async_remote_copy(src_ref=src, dst_ref=land_refs[i].at[me], **sems))
                    recvs.append(pltpu.make_async_remote_copy(src_ref=src, dst_ref=land_refs[i].at[peer], **sems))
            return mine, sends, recvs

        @pl.when(first)
        def _():
            mine, sends, _ = copies()
            for cp in mine + sends:
                cp.start()

        body(*refs[:n_in], *refs[n_in + n:n_in + n + n_out], *refs[n_in + 2 * n + n_out:n_in + 2 * n + n_out + n_scr])

        @pl.when(last)
        def _():
            mine, sends, recvs = copies()
            for cp in recvs:
                cp.wait_recv()
            for cp in sends:
                cp.wait_send()
            for cp in mine:
                cp.wait()

    any_space = pl.BlockSpec(memory_space=pl.ANY)
    call = pl.pallas_call(
        body2, name=name, grid=grid, in_specs=list(in_specs) + [any_space] * n, out_specs=out_specs + [any_space] * n,
        out_shape=out_shape + lands,
        scratch_shapes=list(scratch_shapes) + [pltpu.SemaphoreType.DMA((7 * n,)), pltpu.SemaphoreType.DMA((7 * n,)),
                                               pltpu.SemaphoreType.DMA((n,))],
        compiler_params=pltpu.CompilerParams(dimension_semantics=("arbitrary",) * len(grid)))

    def run(*args):
        res = call(*args, *side.srcs)
        side.landed(res[n_out:])
        return res[0] if single else res[:n_out]

    return run


def _mm(a, b, *, add=None, bias=None, out_dtype=F32, name, side=None):
    m, k = a.shape
    k2, n = b.shape
    assert k == k2
    tn = _divisor_tile(n, max(LANES, min(1408, (6 << 20) // (2 * k) // LANES * LANES)), LANES)
    tm = _divisor_tile(m, max(256, min(1024, (4 << 20) // (a.dtype.itemsize * k) // 256 * 256)), 8)
    has_bias, has_add = bias is not None, add is not None

    def body(*refs):
        a_ref, b_ref = refs[0], refs[1]
        o_ref = refs[-1]
        acc = jnp.dot(a_ref[...].astype(BF16), b_ref[...], preferred_element_type=F32)
        pos = 2
        if has_bias:
            acc = acc + refs[pos][...]
            pos += 1
        if has_add:
            acc = acc + refs[pos][...].astype(F32)
        o_ref[...] = acc.astype(o_ref.dtype)

    in_specs = [pl.BlockSpec((tm, k), lambda i, j: (i, 0)), pl.BlockSpec((k, tn), lambda i, j: (0, j))]
    args = [a, b]
    if has_bias:
        in_specs.append(pl.BlockSpec((1, tn), lambda i, j: (0, j)))
        args.append(bias)
    if has_add:
        in_specs.append(pl.BlockSpec((tm, tn), lambda i, j: (i, j)))
        args.append(add)
    return _pallas(
        body, side, name=name, grid=(m // tm, n // tn), in_specs=in_specs,
        out_specs=pl.BlockSpec((tm, tn), lambda i, j: (i, j)),
        out_shape=jax.ShapeDtypeStruct((m, n), out_dtype),
        semantics=("parallel", "parallel"),
    )(*args)


def _mm_tn(a, g, *, name):
    m, k = a.shape
    m2, n = g.shape
    assert m == m2
    tk = _divisor_tile(k, 1408, LANES)
    tn = _divisor_tile(n, 1408, LANES)
    tm = _divisor_tile(m, 1024, 8)

    def body(a_ref, g_ref, o_ref):
        @pl.when(pl.program_id(2) == 0)
        def _():
            o_ref[...] = jnp.zeros_like(o_ref)

        o_ref[...] += _dot_raw(a_ref[...], g_ref[...], ((0,), (0,)))

    return pl.pallas_call(
        body, name=name, grid=(k // tk, n // tn, m // tm),
        in_specs=[pl.BlockSpec((tm, tk), lambda i, j, t: (t, i)), pl.BlockSpec((tm, tn), lambda i, j, t: (t, j))],
        out_specs=pl.BlockSpec((tk, tn), lambda i, j, t: (i, j)),
        out_shape=jax.ShapeDtypeStruct((k, n), F32),
        compiler_params=pltpu.CompilerParams(dimension_semantics=("parallel", "parallel", "arbitrary")),
    )(a, g)


def _tile_call(name, fn, tiled, params, out_tiled, out_acc, tile, side=None):
    tiled = [t if isinstance(t, tuple) else (t, t.shape[1], 0) for t in tiled]
    t_rows = tiled[0][0].shape[0]
    tile = min(tile, t_rows)
    assert t_rows % tile == 0
    n_t, n_p, n_o = len(tiled), len(params), len(out_tiled)

    def body(*refs):
        vals = [r[...] for r in refs[: n_t + n_p]]
        touts, aouts = fn(*vals)
        for r, v in zip(refs[n_t + n_p: n_t + n_p + n_o], touts):
            r[...] = v.astype(r.dtype)
        acc_refs = refs[n_t + n_p + n_o:]
        if acc_refs:
            @pl.when(pl.program_id(0) == 0)
            def _():
                for r in acc_refs:
                    r[...] = jnp.zeros_like(r)

            for r, v in zip(acc_refs, aouts):
                r[...] += v

    in_specs = [pl.BlockSpec((tile, w), lambda i, cb=cb: (i, cb)) for _, w, cb in tiled]
    in_specs += [pl.BlockSpec(p.shape, _const_map(p.ndim)) for p in params]
    out_specs = [pl.BlockSpec((tile, w), lambda i: (i, 0)) for w, _ in out_tiled]
    out_specs += [pl.BlockSpec(s, _const_map(len(s))) for s in out_acc]
    out_shape = [jax.ShapeDtypeStruct((t_rows, w), dt) for w, dt in out_tiled]
    out_shape += [jax.ShapeDtypeStruct(s, F32) for s in out_acc]
    res = _pallas(
        body, side, name=name, grid=(t_rows // tile,), in_specs=in_specs, out_specs=out_specs, out_shape=out_shape,
        semantics=("arbitrary" if out_acc else "parallel",),
    )(*[t[0] for t in tiled], *params)
    return res[:n_o], res[n_o:]


def _rms_fwd(x, g, name):
    (h,), _ = _tile_call(name, lambda xv, gv: ([_rms(xv, gv)], []), [x], [g], [(x.shape[1], BF16)], [], 512)
    return h


def _rms_bwd(x, g, dh, dres, name, side=None):
    def fn(xv, dhv, drv, gv):
        _, vjp = jax.vjp(_rms, xv, gv)
        dx, dg = vjp(dhv.astype(F32))
        return [drv + dx], [dg]

    (dx,), (dg,) = _tile_call(name, fn, [x, dh, dres], [g], [(x.shape[1], F32)], [g.shape], 512, side)
    return dx, dg


def _colsum(x, name):
    _, (s,) = _tile_call(name, lambda xv: ([], [jnp.sum(xv.astype(F32), axis=0, keepdims=True)]), [x], [], [],
                         [(1, x.shape[1])], 512)
    return s


def _seq_flags(i, tiles_per_seq):
    pos = i % tiles_per_seq
    return pos == 0, pos == tiles_per_seq - 1


def _dwconv_fwd(x, w, b, seq, name, side=None):
    t_rows, ch = x.shape
    kw = w.shape[0]
    tile = min(512, seq)
    cb = _divisor_tile(ch, 256, LANES)
    tps, hb = seq // tile, tile // HALO

    def body(x_ref, halo_ref, w_ref, b_ref, y_ref, pad_ref):
        first, _ = _seq_flags(pl.program_id(0), tps)
        pad_ref[0:HALO, :] = jnp.where(first, 0.0, halo_ref[...])
        pad_ref[HALO:HALO + tile, :] = x_ref[...]
        for r0 in range(0, tile, HALO):
            acc = jnp.broadcast_to(b_ref[...], (HALO, cb))
            for k in range(kw):
                acc = acc + pad_ref[pl.ds(HALO - (kw - 1) + k + r0, HALO), :] * w_ref[k:k + 1, :]
            y_ref[pl.ds(r0, HALO), :] = acc

    return _pallas(
        body, side, name=name, grid=(t_rows // tile, ch // cb),
        in_specs=[pl.BlockSpec((tile, cb), lambda i, j: (i, j)),
                  pl.BlockSpec((HALO, cb), lambda i, j: (jnp.maximum(i * hb - 1, 0), j)),
                  pl.BlockSpec((kw, cb), lambda i, j: (0, j)), pl.BlockSpec((1, cb), lambda i, j: (0, j))],
        out_specs=pl.BlockSpec((tile, cb), lambda i, j: (i, j)),
        out_shape=jax.ShapeDtypeStruct((t_rows, ch), F32),
        scratch_shapes=[pltpu.VMEM((HALO + tile, cb), F32)],
        semantics=("parallel", "parallel"),
    )(x, x, w, b)


def _dwconv_bwd(x, dy, w, seq, name, side=None):
    t_rows, ch = x.shape
    kw = w.shape[0]
    tile = min(512, seq)
    cb = _divisor_tile(ch, 256, LANES)
    tps, hb, n_hb = seq // tile, tile // HALO, t_rows // HALO

    def body(x_ref, xh_ref, dy_ref, dyh_ref, w_ref, dx_ref, dw_ref, db_ref, xpad, dypad, sums):
        i = pl.program_id(1)
        first, last = _seq_flags(i, tps)

        @pl.when(i == 0)
        def _():
            sums[...] = jnp.zeros_like(sums)

        xpad[0:HALO, :] = jnp.where(first, 0.0, xh_ref[...])
        xpad[HALO:HALO + tile, :] = x_ref[...]
        dypad[0:tile, :] = dy_ref[...]
        dypad[tile:tile + HALO, :] = jnp.where(last, 0.0, dyh_ref[...])
        fold = lambda v: functools.reduce(jnp.add, [v[r:r + 8] for r in range(0, HALO, 8)])
        for r0 in range(0, tile, HALO):
            dyc = dy_ref[pl.ds(r0, HALO), :]
            acc = jnp.zeros((HALO, cb), F32)
            for k in range(kw):
                acc = acc + dypad[pl.ds(kw - 1 - k + r0, HALO), :] * w_ref[k:k + 1, :]
                sums[8 * k:8 * k + 8, :] += fold(dyc * xpad[pl.ds(HALO - (kw - 1) + k + r0, HALO), :])
            dx_ref[pl.ds(r0, HALO), :] = acc
            sums[8 * kw:8 * kw + 8, :] += fold(dyc)

        @pl.when(i == t_rows // tile - 1)
        def _():
            for k in range(kw):
                dw_ref[k:k + 1, :] = jnp.sum(sums[8 * k:8 * k + 8, :], axis=0, keepdims=True)
            db_ref[...] = jnp.sum(sums[8 * kw:8 * kw + 8, :], axis=0, keepdims=True)

    return _pallas(
        body, side, name=name, grid=(ch // cb, t_rows // tile),
        in_specs=[pl.BlockSpec((tile, cb), lambda j, i: (i, j)),
                  pl.BlockSpec((HALO, cb), lambda j, i: (jnp.maximum(i * hb - 1, 0), j)),
                  pl.BlockSpec((tile, cb), lambda j, i: (i, j)),
                  pl.BlockSpec((HALO, cb), lambda j, i: (jnp.minimum((i + 1) * hb, n_hb - 1), j)),
                  pl.BlockSpec((kw, cb), lambda j, i: (0, j))],
        out_specs=[pl.BlockSpec((tile, cb), lambda j, i: (i, j)), pl.BlockSpec((kw, cb), lambda j, i: (0, j)),
                   pl.BlockSpec((1, cb), lambda j, i: (0, j))],
        out_shape=[jax.ShapeDtypeStruct((t_rows, ch), F32), jax.ShapeDtypeStruct((kw, ch), F32),
                   jax.ShapeDtypeStruct((1, ch), F32)],
        scratch_shapes=[pltpu.VMEM((HALO + tile, cb), F32), pltpu.VMEM((tile + HALO, cb), F32),
                        pltpu.VMEM((8 * (kw + 1), cb), F32)],
        semantics=("parallel", "arbitrary"),
    )(x, x, dy, dy, w)


_ROWS = SUBLANES_BF16


def _lane_chunks(width, cap=6 * LANES):
    return [slice(c0, min(c0 + cap, width)) for c0 in range(0, width, cap)]


def _tap_rows(w_ref, cols):
    return [w_ref[k:k + 1, cols] for k in range(FFN_CONV_WIDTH)]


def _conv3_at(pad, taps, row, cols):
    z = pad[pl.ds(row, _ROWS), cols] * taps[2]
    z = z + pad[pl.ds(row - 1, _ROWS), cols] * taps[1]
    return z + pad[pl.ds(row - 2, _ROWS), cols] * taps[0]


def _ffn_mid_fwd(u, w, seq, name, side=None):
    t_rows, f2 = u.shape
    f = f2 // 2
    tile = min(256, seq)
    cb = _divisor_tile(f, 1408, LANES)
    nj, tps, hb, hl = f // cb, seq // tile, tile // SUBLANES_BF16, SUBLANES_BF16

    def body(ug_ref, uv_ref, hg_ref, hv_ref, wg_ref, wv_ref, a_ref, gpad, vpad):
        first, _ = _seq_flags(pl.program_id(0), tps)
        for t_ref, h_ref, pad in ((ug_ref, hg_ref, gpad), (uv_ref, hv_ref, vpad)):
            pad[0:hl, :] = jnp.where(first, 0.0, h_ref[...].astype(F32))
            pad[hl:hl + tile, :] = t_ref[...].astype(F32)
        for cols in _lane_chunks(cb):
            wg, wv = _tap_rows(wg_ref, cols), _tap_rows(wv_ref, cols)
            for r0 in range(0, tile, _ROWS):
                zg = _conv3_at(gpad, wg, hl + r0, cols)
                zv = _conv3_at(vpad, wv, hl + r0, cols)
                half = 0.5 * zg
                a_ref[pl.ds(r0, _ROWS), cols] = ((jnp.tanh(half) + 1.0) * half * zv).astype(a_ref.dtype)

    halo_map = lambda off: (lambda i, j: (jnp.maximum(i * hb - 1, 0), j + off))
    return _pallas(
        body, side, name=name, grid=(t_rows // tile, nj),
        in_specs=[pl.BlockSpec((tile, cb), lambda i, j: (i, j)), pl.BlockSpec((tile, cb), lambda i, j: (i, j + nj)),
                  pl.BlockSpec((hl, cb), halo_map(0)), pl.BlockSpec((hl, cb), halo_map(nj)),
                  pl.BlockSpec((3, cb), lambda i, j: (0, j)), pl.BlockSpec((3, cb), lambda i, j: (0, j + nj))],
        out_specs=pl.BlockSpec((tile, cb), lambda i, j: (i, j)),
        out_shape=jax.ShapeDtypeStruct((t_rows, f), BF16),
        scratch_shapes=[pltpu.VMEM((hl + tile, cb), F32), pltpu.VMEM((hl + tile, cb), F32)],
        semantics=("parallel", "parallel"),
    )(u, u, u, u, w, w)


def _ffn_mid_bwd(u, da, w, seq, name, side=None):
    t_rows, f2 = u.shape
    f = f2 // 2
    tile = min(256, seq)
    cb = _divisor_tile(f, 1408, LANES)
    hl = SUBLANES_BF16
    nj, tps, hb, n_hb, ext = f // cb, seq // tile, tile // hl, t_rows // hl, tile + hl

    def body(ug_ref, uv_ref, pg_ref, pv_ref, ng_ref, nv_ref, da_ref, dan_ref, wg_ref, wv_ref,
             dug_ref, duv_ref, dwg_ref, dwv_ref, gpad, vpad, dzg, dzv):
        i = pl.program_id(1)
        first, last = _seq_flags(i, tps)

        @pl.when(i == 0)
        def _():
            dwg_ref[...] = jnp.zeros_like(dwg_ref)
            dwv_ref[...] = jnp.zeros_like(dwv_ref)

        for t_ref, p_ref, n_ref, pad in ((ug_ref, pg_ref, ng_ref, gpad), (uv_ref, pv_ref, nv_ref, vpad)):
            pad[0:hl, :] = jnp.where(first, 0.0, p_ref[...].astype(F32))
            pad[hl:hl + tile, :] = t_ref[...].astype(F32)
            pad[hl + tile:hl + ext, :] = jnp.where(last, 0.0, n_ref[...].astype(F32))
        for cols in _lane_chunks(cb):
            wg, wv = _tap_rows(wg_ref, cols), _tap_rows(wv_ref, cols)
            for r0 in range(0, ext, _ROWS):
                zg = _conv3_at(gpad, wg, hl + r0, cols)
                zv = _conv3_at(vpad, wv, hl + r0, cols)
                if r0 < tile:
                    da = da_ref[pl.ds(r0, _ROWS), cols].astype(F32)
                else:
                    da = jnp.where(last, 0.0, dan_ref[:, cols].astype(F32))
                sg = _sigmoid(zg)
                dzg[pl.ds(r0, _ROWS), cols] = da * zv * (sg * (1.0 + zg * (1.0 - sg)))
                dzv[pl.ds(r0, _ROWS), cols] = da * (zg * sg)
        for dz, w_ref, pad, du_ref, dw_ref in ((dzg, wg_ref, gpad, dug_ref, dwg_ref), (dzv, wv_ref, vpad, duv_ref, dwv_ref)):
            for cols in _lane_chunks(cb):
                taps = _tap_rows(w_ref, cols)
                width = cols.stop - cols.start
                acc = [jnp.zeros((8, width), F32) for _ in range(FFN_CONV_WIDTH)]
                for r0 in range(0, tile, _ROWS):
                    d0 = dz[pl.ds(r0, _ROWS), cols]
                    du = dz[pl.ds(r0 + 2, _ROWS), cols] * taps[0] + dz[pl.ds(r0 + 1, _ROWS), cols] * taps[1] + d0 * taps[2]
                    du_ref[pl.ds(r0, _ROWS), cols] = du.astype(du_ref.dtype)
                    for k in range(FFN_CONV_WIDTH):
                        prod = d0 * pad[pl.ds(hl - 2 + k + r0, _ROWS), cols]
                        acc[k] = acc[k] + prod[0:8] + prod[8:16]
                for k in range(FFN_CONV_WIDTH):
                    dw_ref[k:k + 1, cols] += jnp.sum(acc[k], axis=0, keepdims=True)

    prev_map = lambda off: (lambda j, i: (jnp.maximum(i * hb - 1, 0), j + off))
    next_map = lambda off: (lambda j, i: (jnp.minimum((i + 1) * hb, n_hb - 1), j + off))
    tile_spec = lambda off: pl.BlockSpec((tile, cb), lambda j, i: (i, j + off))
    w_spec = lambda off: pl.BlockSpec((3, cb), lambda j, i: (0, j + off))
    return _pallas(
        body, side, name=name, grid=(nj, t_rows // tile),
        in_specs=[tile_spec(0), tile_spec(nj), pl.BlockSpec((hl, cb), prev_map(0)), pl.BlockSpec((hl, cb), prev_map(nj)),
                  pl.BlockSpec((hl, cb), next_map(0)), pl.BlockSpec((hl, cb), next_map(nj)),
                  tile_spec(0), pl.BlockSpec((hl, cb), next_map(0)), w_spec(0), w_spec(nj)],
        out_specs=[tile_spec(0), tile_spec(0), w_spec(0), w_spec(0)],
        out_shape=[jax.ShapeDtypeStruct((t_rows, f), BF16), jax.ShapeDtypeStruct((t_rows, f), BF16),
                   jax.ShapeDtypeStruct((3, f), F32), jax.ShapeDtypeStruct((3, f), F32)],
        scratch_shapes=[pltpu.VMEM((hl + ext, cb), F32), pltpu.VMEM((hl + ext, cb), F32),
                        pltpu.VMEM((ext, cb), F32), pltpu.VMEM((ext, cb), F32)],
        semantics=("parallel", "arbitrary"),
    )(u, u, u, u, u, u, da, da, w, w)


def _gla_chunk(q, k, v, lg, st, *, scale, chunk):
    row = lax.broadcasted_iota(jnp.int32, (chunk, chunk), 0)
    col = lax.broadcasted_iota(jnp.int32, (chunk, chunk), 1)
    causal = col <= row
    b = _cumsum_rows(lg)
    upto_mid = lax.broadcasted_iota(jnp.int32, lg.shape, 0) <= chunk // 2
    b_mid = jnp.sum(jnp.where(upto_mid, lg, 0.0), axis=0, keepdims=True)
    b_last = jnp.sum(lg, axis=0, keepdims=True)
    qs = q * scale
    scores = _bdot_nt(qs * jnp.exp(b - b_mid), k * jnp.exp(b_mid - b))
    o = _bdot(jnp.where(causal, scores, 0.0), v)
    o = o + _bdot_nt(qs * jnp.exp(b), st)
    st_new = st * jnp.exp(b_last) + _bdot_tn(v, k * jnp.exp(b_last - b))
    return o, st_new


_CHUNKS_PER_STEP = 2


def _gla_specs(specs, rows, n_blocks, reverse):
    if reverse:
        row = lambda bi, ci: bi * n_blocks + (n_blocks - 1 - ci)
    else:
        row = lambda bi, ci: bi * n_blocks + ci
    return [pl.BlockSpec((rows, w), lambda bi, ci, cb=cb: (row(bi, ci), cb)) for _, w, cb in specs], row


def _gla_fwd(q, k, v, lg, *, heads, dk, dv, scale, chunk, seq, name, side=None):
    t_rows = q[0].shape[0]
    per = _CHUNKS_PER_STEP if (seq // chunk) % _CHUNKS_PER_STEP == 0 else 1
    n_blocks = seq // (per * chunk)
    fn = functools.partial(_gla_chunk, scale=scale, chunk=chunk)

    def body(q_ref, k_ref, v_ref, lg_ref, o_ref, sts_ref, st_ref):
        @pl.when(pl.program_id(1) == 0)
        def _():
            st_ref[...] = jnp.zeros_like(st_ref)

        ks = [slice(h * dk, (h + 1) * dk) for h in range(heads)]
        vs = [slice(h * dv, (h + 1) * dv) for h in range(heads)]
        st = [st_ref[vs[h], :] for h in range(heads)]
        for s in range(per):
            rows = pl.ds(s * chunk, chunk)
            for h in range(heads):
                sts_ref[s, vs[h], :] = st[h]
                o, st[h] = fn(q_ref[rows, ks[h]].astype(F32), k_ref[rows, ks[h]].astype(F32),
                              v_ref[rows, vs[h]].astype(F32), lg_ref[rows, ks[h]], st[h])
                o_ref[rows, vs[h]] = o
        for h in range(heads):
            st_ref[vs[h], :] = st[h]

    in_specs, row = _gla_specs([q, k, v, lg], per * chunk, n_blocks, False)
    return _pallas(
        body, side, name=name, grid=(t_rows // seq, n_blocks), in_specs=in_specs,
        out_specs=[pl.BlockSpec((per * chunk, heads * dv), lambda bi, ci: (row(bi, ci), 0)),
                   pl.BlockSpec((per, heads * dv, dk), lambda bi, ci: (row(bi, ci), 0, 0))],
        out_shape=[jax.ShapeDtypeStruct((t_rows, heads * dv), F32),
                   jax.ShapeDtypeStruct((t_rows // chunk, heads * dv, dk), F32)],
        scratch_shapes=[pltpu.VMEM((heads * dv, dk), F32)],
        semantics=("arbitrary", "arbitrary"),
    )(q[0], k[0], v[0], lg[0])


def _gla_bwd(q, k, v, lg, states, do, *, heads, dk, dv, scale, chunk, seq, out_dtypes, name, side=None):
    t_rows = q[0].shape[0]
    per = _CHUNKS_PER_STEP if (seq // chunk) % _CHUNKS_PER_STEP == 0 else 1
    n_blocks = seq // (per * chunk)
    fn = functools.partial(_gla_chunk, scale=scale, chunk=chunk)

    def body(q_ref, k_ref, v_ref, lg_ref, do_ref, sts_ref, dq_ref, dk_ref, dv_ref, dlg_ref, dst_ref):
        @pl.when(pl.program_id(1) == 0)
        def _():
            dst_ref[...] = jnp.zeros_like(dst_ref)

        ks = [slice(h * dk, (h + 1) * dk) for h in range(heads)]
        vs = [slice(h * dv, (h + 1) * dv) for h in range(heads)]
        dst = [dst_ref[vs[h], :] for h in range(heads)]
        for s in reversed(range(per)):
            rows = pl.ds(s * chunk, chunk)
            for h in range(heads):
                _, vjp = jax.vjp(fn, q_ref[rows, ks[h]].astype(F32), k_ref[rows, ks[h]].astype(F32),
                                 v_ref[rows, vs[h]].astype(F32), lg_ref[rows, ks[h]], sts_ref[s, vs[h], :])
                dq, dkk, dvv, dlg, dst[h] = vjp((do_ref[rows, vs[h]].astype(F32), dst[h]))
                dq_ref[rows, ks[h]] = dq.astype(dq_ref.dtype)
                dk_ref[rows, ks[h]] = dkk.astype(dk_ref.dtype)
                dv_ref[rows, vs[h]] = dvv.astype(dv_ref.dtype)
                dlg_ref[rows, ks[h]] = dlg
        for h in range(heads):
            dst_ref[vs[h], :] = dst[h]

    do_view = (do, heads * dv, 0)
    in_specs, row = _gla_specs([q, k, v, lg, do_view], per * chunk, n_blocks, True)
    in_specs.append(pl.BlockSpec((per, heads * dv, dk), lambda bi, ci: (row(bi, ci), 0, 0)))
    wide = lambda w: pl.BlockSpec((per * chunk, w), lambda bi, ci: (row(bi, ci), 0))
    return _pallas(
        body, side, name=name, grid=(t_rows // seq, n_blocks), in_specs=in_specs,
        out_specs=[wide(heads * dk), wide(heads * dk), wide(heads * dv), wide(heads * dk)],
        out_shape=[jax.ShapeDtypeStruct((t_rows, heads * dk), out_dtypes[0]),
                   jax.ShapeDtypeStruct((t_rows, heads * dk), out_dtypes[1]),
                   jax.ShapeDtypeStruct((t_rows, heads * dv), out_dtypes[2]),
                   jax.ShapeDtypeStruct((t_rows, heads * dk), F32)],
        scratch_shapes=[pltpu.VMEM((heads * dv, dk), F32)],
        semantics=("arbitrary", "arbitrary"),
    )(q[0], k[0], v[0], lg[0], do, states)


def _head_rms_gate(o, r, g, heads):
    d = o.shape[1] // heads
    parts = [_rms(o[:, h * d:(h + 1) * d], g) for h in range(heads)]
    return jnp.concatenate(parts, axis=1) * _silu(r)


def _gla_gate(glr, w_g2p, b_g2):
    return _log_sigmoid(_bdot(glr, w_g2p) + b_g2) * (1.0 / GLA_GATE_NORM)


def _glu(a, gate, b_in):
    d = a.shape[1]
    return (a + b_in[:, :d]) * _sigmoid(gate + b_in[:, d:])


def _ln_silu(y, g, b):
    return _silu(_layer_norm(y, g, b))


def _sgu(pre, b_in, ln_g, ln_b, w_s, b_st):
    d = pre.shape[1] // 2
    gd = d // SGU_GROUPS
    uv = _gelu(pre + b_in)
    u, v = uv[:, :d], _layer_norm(uv[:, d:], ln_g, ln_b)
    row = lax.broadcasted_iota(jnp.int32, (SGU_CHUNK, SGU_CHUNK), 0)
    col = lax.broadcasted_iota(jnp.int32, (SGU_CHUNK, SGU_CHUNK), 1)
    lane = lax.broadcasted_iota(jnp.int32, b_st.shape, 1)
    rows = []
    for c in range(pre.shape[0] // SGU_CHUNK):
        rs = slice(c * SGU_CHUNK, (c + 1) * SGU_CHUNK)
        parts = []
        for g in range(SGU_GROUPS):
            wg = jnp.where(col <= row, w_s[g], 0.0)
            bias = jnp.sum(jnp.where(lane == g, b_st, 0.0), axis=1, keepdims=True)
            parts.append(_bdot(wg, v[rs, g * gd:(g + 1) * gd]) + bias)
        rows.append(jnp.concatenate(parts, axis=1))
    s = rows[0] if len(rows) == 1 else jnp.concatenate(rows, axis=0)
    return u * s


def _hgrn_pre(q, f, table, layer):
    t = table - jnp.max(table, axis=0, keepdims=True)
    e = jnp.exp(t)
    sm = e / jnp.sum(e, axis=0, keepdims=True)
    rows = lax.broadcasted_iota(jnp.int32, table.shape, 0)
    lb = jnp.sum(jnp.where((rows >= 1) & (rows <= layer), sm, 0.0), axis=0, keepdims=True)
    sf = _sigmoid(f)
    return _silu(q), (1.0 - lb) * (1.0 - sf), jnp.log(lb + (1.0 - lb) * sf)


def _ffn_fwd(x, w, seq, sv):
    sv["h2"] = _rms_fwd(x, w["norm"], "ffn_norm")
    late = w.get("late")
    sv["u"] = _mm(sv["h2"], w["w_up"], out_dtype=BF16, name="ffn_up", side=late[0] if late else None)
    sv["a"] = _ffn_mid_fwd(sv["u"], w["w_dw"], seq, "ffn_mid", sv.pop("side", None))
    if late:
        sv["late_w"] = late[1]()
        w = dict(w, **sv["late_w"])
    return _mm(sv["a"], w["w_down"], add=x, name="ffn_down")


def _ffn_bwd(x, dy, w, seq, sv):
    g = {}
    da = _mm(dy, w["w_down_t"], out_dtype=BF16, name="ffn_down_dx")
    g["w_down"] = _mm_tn(sv["a"], dy, name="ffn_down_dw")
    early = sv.pop("put_early", None)
    side = _Side.join([sv.pop("side", None), early(g["w_down"]) if early else None])
    dug, duv, dwg, dwv = _ffn_mid_bwd(sv["u"], da, w["w_dw"], seq, "ffn_mid_bwd", side)
    g["w_dw"] = jnp.concatenate([dwg, dwv], axis=1)
    g["w_up_gate"] = _mm_tn(sv["h2"], dug, name="ffn_up_dw")
    g["w_up_val"] = _mm_tn(sv["h2"], duv, name="ffn_up_dw")
    dh = _mm(dug, w["w_up_t_gate"], out_dtype=F32, name="ffn_up_dx")
    dh = _mm(duv, w["w_up_t_val"], add=dh, out_dtype=BF16, name="ffn_up_dx2")
    dx, g["norm"] = _rms_bwd(x, w["norm"], dh, dy, "ffn_norm_bwd")
    return dx, g


def _gla_layer_fwd(x, h, w, seq, sv):
    d = x.shape[1]
    dkt = d // 2
    dk, dv = dkt // GLA_HEADS, d // GLA_HEADS
    proj = _mm(h, w["w_main"], out_dtype=F32, name="gla_in")
    glr = _mm(h, w["w_glr"], out_dtype=BF16, name="gla_in_g")
    (lg,), _ = _tile_call("gla_gate", lambda a, b, c: ([_gla_gate(a.astype(F32), b, c)], []), [glr],
                          [w["w_g2p"], w["b_g2"]], [(dkt, F32)], [], 512)
    q, k, v, r = (proj, dkt, 0), (proj, dkt, 1), (proj, d, 1), (proj, d, 2)
    o, states = _gla_fwd(q, k, v, (lg, dkt, 0), heads=GLA_HEADS, dk=dk, dv=dv, scale=dk ** -0.5, chunk=GLA_CHUNK,
                         seq=seq, name="gla_core", side=sv.pop("side", None))
    (o2,), _ = _tile_call("gla_post", lambda ov, rv, gv: ([_head_rms_gate(ov, rv.astype(F32), gv, GLA_HEADS)], []),
                          [o, r], [w["norm"]], [(d, BF16)], [], 256)
    sv.update(proj=proj, glr=glr, lg=lg, o=o, states=states, o2=o2)
    return _mm(o2, w["w_out"], add=x, name="mix_out")


def _gla_layer_bwd(h, dy, w, seq, sv):
    d = dy.shape[1]
    dkt = d // 2
    dk, dv = dkt // GLA_HEADS, d // GLA_HEADS
    proj, glr, lg, o = sv["proj"], sv["glr"], sv["lg"], sv["o"]
    g = {}
    do2 = _mm(dy, w["w_out_t"], out_dtype=F32, name="gla_out_dx")
    g["w_out"] = _mm_tn(sv["o2"], dy, name="mix_out_dw")

    def post_bwd(ov, rv, ctv, gv):
        _, vjp = jax.vjp(functools.partial(_head_rms_gate, heads=GLA_HEADS), ov, rv.astype(F32), gv)
        d_o, d_r, d_g = vjp(ctv.astype(F32))
        return [d_o, d_r], [d_g]

    (d_o, d_r), (g["norm"],) = _tile_call("gla_post_bwd", post_bwd, [o, (proj, d, 2), do2], [w["norm"]],
                                          [(d, F32), (d, BF16)], [w["norm"].shape], 256)
    q, k, v = (proj, dkt, 0), (proj, dkt, 1), (proj, d, 1)
    dq, dkk, dvv, dlg = _gla_bwd(q, k, v, (lg, dkt, 0), sv["states"], d_o, heads=GLA_HEADS, dk=dk, dv=dv,
                                 scale=dk ** -0.5, chunk=GLA_CHUNK, seq=seq, out_dtypes=(BF16, BF16, BF16),
                                 name="gla_core_bwd", side=sv.pop("side", None))

    def gate_bwd(glrv, ctv, wv, bv):
        _, vjp = jax.vjp(_gla_gate, glrv.astype(F32), wv, bv)
        d_glr, d_w, d_b = vjp(ctv)
        return [d_glr], [d_w, d_b]

    (dglr,), (g["w_g2p"], g["b_g2"]) = _tile_call("gla_gate_bwd", gate_bwd, [glr, dlg], [w["w_g2p"], w["b_g2"]],
                                                  [(LANES, BF16)], [w["w_g2p"].shape, w["b_g2"].shape], 512)
    dproj = jnp.concatenate([dq, dkk, dvv, d_r], axis=1)
    g["w_main"] = _mm_tn(h, dproj, name="gla_in_dw")
    g["w_glr"] = _mm_tn(h, dglr, name="gla_in_g_dw")
    dh = _mm(dproj, w["w_main_t"], out_dtype=F32, name="gla_in_dx")
    dh = _mm(dglr, w["w_glr_t"], add=dh, out_dtype=BF16, name="gla_in_g_dx")
    return dh, g


def _cv_layer_fwd(x, h, w, seq, sv):
    d = x.shape[1]
    pre = _mm(h, w["w_in"], out_dtype=BF16, name="cv_in")
    (y1,), _ = _tile_call("cv_glu", lambda a, gt, b: ([_glu(a.astype(F32), gt.astype(F32), b)], []),
                          [(pre, d, 0), (pre, d, 1)], [w["b_in"]], [(d, F32)], [], 512)
    y2 = _dwconv_fwd(y1, w["w_dw"], w["b_dw"], seq, "cv_conv", sv.pop("side", None))
    (y3,), _ = _tile_call("cv_ln", lambda y, a, b: ([_ln_silu(y, a, b)], []), [y2], [w["ln_g"], w["ln_b"]],
                          [(d, BF16)], [], 512)
    sv.update(pre=pre, y1=y1, y2=y2, y3=y3)
    return _mm(y3, w["w_out"], bias=w["b_out"], add=x, name="mix_out_b")


def _cv_layer_bwd(h, dy, w, seq, sv):
    d = dy.shape[1]
    pre = sv["pre"]
    g = {}
    dy3 = _mm(dy, w["w_out_t"], out_dtype=BF16, name="mix_out_dx")
    g["w_out"] = _mm_tn(sv["y3"], dy, name="mix_out_dw")
    g["b_out"] = _colsum(dy, "bias_out_dw")

    def ln_bwd(yv, ctv, av, bv):
        _, vjp = jax.vjp(_ln_silu, yv, av, bv)
        d_y, d_a, d_b = vjp(ctv.astype(F32))
        return [d_y], [d_a, d_b]

    (dy2,), (g["ln_g"], g["ln_b"]) = _tile_call("cv_ln_bwd", ln_bwd, [sv["y2"], dy3], [w["ln_g"], w["ln_b"]],
                                                [(d, F32)], [w["ln_g"].shape, w["ln_b"].shape], 512)
    dy1, g["w_dw"], g["b_dw"] = _dwconv_bwd(sv["y1"], dy2, w["w_dw"], seq, "cv_conv_bwd", sv.pop("side", None))

    def glu_bwd(av, gv, ctv, bv):
        _, vjp = jax.vjp(_glu, av.astype(F32), gv.astype(F32), bv)
        d_a, d_g, d_b = vjp(ctv)
        return [jnp.concatenate([d_a, d_g], axis=1)], [d_b]

    (dpre,), (g["b_in"],) = _tile_call("cv_glu_bwd", glu_bwd, [(pre, d, 0), (pre, d, 1), dy1], [w["b_in"]],
                                       [(2 * d, BF16)], [w["b_in"].shape], 512)
    g["w_in"] = _mm_tn(h, dpre, name="in2_dw")
    dh = _mm(dpre, w["w_in_t"], out_dtype=BF16, name="in2_dx")
    return dh, g


def _sg_layer_fwd(x, h, w, seq, sv):
    d = x.shape[1]
    pre = _mm(h, w["w_in"], out_dtype=BF16, name="sg_in")
    pars = [w["b_in"], w["ln_g"], w["ln_b"], w["w_s"], w["b_st"]]
    (p,), _ = _tile_call("sg_gate", lambda pv, *ps: ([_sgu(pv.astype(F32), *ps)], []), [pre], pars, [(d, BF16)], [],
                         SGU_CHUNK, side=sv.pop("side", None))
    sv.update(pre=pre, p=p)
    return _mm(p, w["w_out"], bias=w["b_out"], add=x, name="mix_out_b")


def _sg_layer_bwd(h, dy, w, seq, sv):
    d = dy.shape[1]
    g = {}
    dp = _mm(dy, w["w_out_t"], out_dtype=BF16, name="mix_out_dx")
    g["w_out"] = _mm_tn(sv["p"], dy, name="mix_out_dw")
    g["b_out"] = _colsum(dy, "bias_out_dw")
    pars = [w["b_in"], w["ln_g"], w["ln_b"], w["w_s"], w["b_st"]]

    def sgu_bwd(pv, ctv, *ps):
        _, vjp = jax.vjp(_sgu, pv.astype(F32), *ps)
        grads = vjp(ctv.astype(F32))
        return [grads[0]], list(grads[1:])

    (dpre,), (g["b_in"], g["ln_g"], g["ln_b"], g["w_s"], g["b_st"]) = _tile_call(
        "sg_gate_bwd", sgu_bwd, [sv["pre"], dp], pars, [(2 * d, BF16)], [p.shape for p in pars], SGU_CHUNK,
        side=sv.pop("side", None))
    g["w_in"] = _mm_tn(h, dpre, name="in2_dw")
    dh = _mm(dpre, w["w_in_t"], out_dtype=BF16, name="in2_dx")
    return dh, g


def _hg_layer_fwd(x, h, w, seq, sv, layer):
    d = x.shape[1]
    heads = d // HGRN_EXPAND
    proj = _mm(h, w["w_in"], out_dtype=BF16, name="hg_in")
    pre = functools.partial(_hgrn_pre, layer=layer)
    (qs, kk, lg), _ = _tile_call("hg_pre", lambda qv, fv, tb: (list(pre(qv.astype(F32), fv.astype(F32), tb)), []),
                                 [(proj, d, 0), (proj, d, 1)], [w["lb_table"]], [(d, BF16), (d, F32), (d, F32)], [], 256)
    o, states = _gla_fwd((qs, d, 0), (kk, d, 0), (proj, d, 2), (lg, d, 0), heads=heads, dk=HGRN_EXPAND,
                         dv=HGRN_EXPAND, scale=1.0, chunk=HGRN_CHUNK, seq=seq, name="hg_core",
                         side=sv.pop("side", None))
    (o2,), _ = _tile_call("hg_post", lambda ov, gv, nv: ([_head_rms_gate(ov, gv.astype(F32), nv, heads)], []),
                          [o, (proj, d, 3)], [w["norm"]], [(d, BF16)], [], 256)
    sv.update(proj=proj, qs=qs, kk=kk, lg=lg, o=o, states=states, o2=o2)
    return _mm(o2, w["w_out"], add=x, name="mix_out")


def _hg_layer_bwd(h, dy, w, seq, sv, layer):
    d = dy.shape[1]
    heads = d // HGRN_EXPAND
    proj = sv["proj"]
    g = {}
    do2 = _mm(dy, w["w_out_t"], out_dtype=BF16, name="mix_out_dx")
    g["w_out"] = _mm_tn(sv["o2"], dy, name="mix_out_dw")

    def post_bwd(ov, gv, ctv, nv):
        _, vjp = jax.vjp(functools.partial(_head_rms_gate, heads=heads), ov, gv.astype(F32), nv)
        d_o, d_g, d_n = vjp(ctv.astype(F32))
        return [d_o, d_g], [d_n]

    (d_o, d_gate), (g["norm"],) = _tile_call("hg_post_bwd", post_bwd, [sv["o"], (proj, d, 3), do2], [w["norm"]],
                                             [(d, F32), (d, BF16)], [w["norm"].shape], 256)
    dqs, dkk, di, dlg = _gla_bwd((sv["qs"], d, 0), (sv["kk"], d, 0), (proj, d, 2), (sv["lg"], d, 0), sv["states"], d_o,
                                 heads=heads, dk=HGRN_EXPAND, dv=HGRN_EXPAND, scale=1.0, chunk=HGRN_CHUNK, seq=seq,
                                 out_dtypes=(F32, F32, BF16), name="hg_core_bwd", side=sv.pop("side", None))

    def pre_bwd(qv, fv, c1, c2, c3, tb):
        _, vjp = jax.vjp(functools.partial(_hgrn_pre, layer=layer), qv.astype(F32), fv.astype(F32), tb)
        d_q, d_f, d_t = vjp((c1, c2, c3))
        return [jnp.concatenate([d_q, d_f], axis=1)], [d_t]

    (dqf,), (g["lb_table"],) = _tile_call("hg_pre_bwd", pre_bwd, [(proj, d, 0), (proj, d, 1), dqs, dkk, dlg],
                                          [w["lb_table"]], [(2 * d, BF16)], [w["lb_table"].shape], 256)
    dproj = jnp.concatenate([dqf, di, d_gate], axis=1)
    g["w_in"] = _mm_tn(h, dproj, name="hg_in_dw")
    dh = _mm(dproj, w["w_in_t"], out_dtype=BF16, name="hg_in_dx")
    return dh, g


_MIXERS = ("gla", "cv", "sg", "hg")


_BIG_KEYS = {"gla": ("w_main", "w_glr", "w_out"), "cv": ("w_in", "w_out"), "sg": ("w_in", "w_out"), "hg": ("w_in", "w_out"),
             "ffn": ("w_up_gate", "w_up_val", "w_down")}


def _local_step(x, target, w, seq, get_big, put_big, ride=lambda kind, layer, forward: None, put_early=None):
    depth = w["norm_mix"].shape[0]
    d = x.shape[1]
    saved, big = [], {}
    for layer in range(depth):
        mixer = _MIXERS[layer % 4]
        sv = {"x_in": x, "side": ride(mixer, layer, True)}
        sv["h"] = _rms_fwd(x, w["norm_mix"][layer:layer + 1], "mix_norm")
        big[mixer, layer] = get_big(mixer, layer)
        wm = dict(w[mixer], **big[mixer, layer])
        if mixer == "gla":
            x = _gla_layer_fwd(x, sv["h"], wm, seq, sv)
        elif mixer == "cv":
            x = _cv_layer_fwd(x, sv["h"], wm, seq, sv)
        elif mixer == "sg":
            x = _sg_layer_fwd(x, sv["h"], wm, seq, sv)
        else:
            x = _hg_layer_fwd(x, sv["h"], wm, seq, sv, layer)
        sv["x_mid"] = x
        big["ffn", layer] = get_big("ffn", layer)
        sv["ffn"] = {"side": ride("ffn", layer, True)}
        wf = dict(w["ffn"][layer], norm=w["norm_ffn"][layer:layer + 1], **big["ffn", layer])
        x = _ffn_fwd(x, wf, seq, sv["ffn"])
        big["ffn", layer].pop("late", None)
        big["ffn", layer].update(sv["ffn"].pop("late_w", {}))
        saved.append(sv)

    def head(xv, tv, gv):
        y, vjp = jax.vjp(_rms, xv, gv)
        err = y - tv
        dx, dg = vjp(err * (1.0 / d))
        part = 0.5 * jnp.sum(jnp.mean(err * err, axis=-1, keepdims=True), axis=0, keepdims=True)
        return [dx], [jnp.broadcast_to(part, (1, LANES)), dg]

    (dx,), (loss, g_final) = _tile_call("loss_head", head, [x, target], [w["norm_final"]], [(d, F32)],
                                        [(1, LANES), (1, d)], 512)
    grads = {"norm_final": g_final, "norm_mix": [None] * depth, "norm_ffn": [None] * depth, "ffn": [None] * depth}
    for layer in reversed(range(depth)):
        mixer = _MIXERS[layer % 4]
        sv = saved[layer]
        wf = dict(w["ffn"][layer], norm=w["norm_ffn"][layer:layer + 1], **big["ffn", layer])
        sv["ffn"]["side"] = ride("ffn", layer, False)
        if put_early is not None:
            sv["ffn"]["put_early"] = functools.partial(put_early, layer)
        dx, gf = _ffn_bwd(sv["x_mid"], dx, wf, seq, sv["ffn"])
        put_big("ffn", layer, {k: gf.pop(k) for k in _BIG_KEYS["ffn"]})
        sv["side"] = ride(mixer, layer, False)
        grads["norm_ffn"][layer] = gf.pop("norm")
        grads["ffn"][layer] = gf
        wm = dict(w[mixer], **big[mixer, layer])
        if mixer == "gla":
            dh, gm = _gla_layer_bwd(sv["h"], dx, wm, seq, sv)
        elif mixer == "cv":
            dh, gm = _cv_layer_bwd(sv["h"], dx, wm, seq, sv)
        elif mixer == "sg":
            dh, gm = _sg_layer_bwd(sv["h"], dx, wm, seq, sv)
        else:
            dh, gm = _hg_layer_bwd(sv["h"], dx, wm, seq, sv, layer)
        put_big(mixer, layer, {k: gm.pop(k) for k in _BIG_KEYS[mixer]})
        grads[mixer] = gm
        last = ride(mixer, layer, False) if layer == 0 else None
        dx, grads["norm_mix"][layer] = _rms_bwd(sv["x_in"], w["norm_mix"][layer:layer + 1], dh, dx, "mix_norm_bwd", last)
    return loss, dx, grads


def _prep_small(p):
    row = lambda a: a.reshape(1, -1).astype(F32)
    w = {"norm_mix": p["norm_mix"].astype(F32), "norm_ffn": p["norm_ffn"].astype(F32), "norm_final": row(p["norm_final"])}
    w["gla"] = dict(w_g2p=jnp.pad(p["gla_w_g2"][0].astype(F32), ((0, LANES - GLA_RANK), (0, 0))), b_g2=row(p["gla_b_g2"]),
                    norm=row(p["gla_norm"]))
    w["cv"] = dict(b_in=row(p["cv_b_in"]), w_dw=p["cv_w_dw"][0].astype(F32), b_dw=row(p["cv_b_dw"]), ln_g=row(p["cv_ln_g"]),
                   ln_b=row(p["cv_ln_b"]), b_out=row(p["cv_b_out"]))
    b_st = jnp.pad(p["sg_b_s"][0].astype(F32).T, ((0, 0), (0, LANES - SGU_GROUPS)))
    w["sg"] = dict(b_in=row(p["sg_b_in"]), ln_g=row(p["sg_ln_g"]), ln_b=row(p["sg_ln_b"]), w_s=p["sg_w_s"][0].astype(F32),
                   b_st=b_st, b_out=row(p["sg_b_out"]))
    w["hg"] = dict(lb_table=p["hg_lb_table"].astype(F32), norm=row(p["hg_norm"]))
    w["ffn"] = [dict(w_dw=p["ffn_w_dw"][layer].astype(F32)) for layer in range(p["ffn_w_dw"].shape[0])]
    return w


def _small_grads(g):
    gla, cv, sg, hg = g["gla"], g["cv"], g["sg"], g["hg"]
    return {
        "norm_mix": jnp.concatenate(g["norm_mix"], axis=0), "norm_ffn": jnp.concatenate(g["norm_ffn"], axis=0),
        "norm_final": g["norm_final"][0],
        "gla_w_g2": gla["w_g2p"][:GLA_RANK][None], "gla_b_g2": gla["b_g2"], "gla_norm": gla["norm"],
        "cv_b_in": cv["b_in"], "cv_w_dw": cv["w_dw"][None], "cv_b_dw": cv["b_dw"], "cv_ln_g": cv["ln_g"],
        "cv_ln_b": cv["ln_b"], "cv_b_out": cv["b_out"],
        "sg_b_in": sg["b_in"], "sg_ln_g": sg["ln_g"], "sg_ln_b": sg["ln_b"], "sg_w_s": sg["w_s"][None],
        "sg_b_s": sg["b_st"][:, :SGU_GROUPS].T[None], "sg_b_out": sg["b_out"],
        "hg_lb_table": hg["lb_table"], "hg_norm": hg["norm"],
        "ffn_w_dw": jnp.stack([f["w_dw"] for f in g["ffn"]]),
    }


def _oriented(kind, mats):
    if kind == "ffn":
        (up, up_t), (down, down_t) = mats["w_up"], mats["w_down"]
        f = down.shape[0]
        return dict(w_up=up, w_up_t_gate=up_t[:f], w_up_t_val=up_t[f:], w_down=down, w_down_t=down_t)
    (w_in, w_in_t), (w_out, w_out_t) = mats["w_in"], mats["w_out"]
    if kind != "gla":
        return dict(w_in=w_in, w_in_t=w_in_t, w_out=w_out, w_out_t=w_out_t)
    n_main = w_in.shape[1] - GLA_RANK
    return dict(w_main=w_in[:, :n_main], w_glr=jnp.pad(w_in[:, n_main:], ((0, 0), (0, LANES - GLA_RANK))),
                w_main_t=w_in_t[:n_main], w_glr_t=jnp.pad(w_in_t[n_main:], ((0, LANES - GLA_RANK), (0, 0))),
                w_out=w_out, w_out_t=w_out_t)


def _all_gather(x, *, name):
    m_per, n = x.shape

    def body(x_ref, out_ref, send_sems, recv_sems, local_sem):
        mx, my, mc = lax.axis_index("x"), lax.axis_index("y"), lax.axis_index("c")
        me, sibling = (mx, my, mc), (mx, my, 1 - mc)
        chips = [(1 - mx, my), (mx, 1 - my), (1 - mx, 1 - my)]

        def rows(px, py, pc):
            return out_ref.at[pl.ds((4 * px + 2 * py + pc) * m_per, m_per), :]

        def copy(k, block, to, src=None):
            return pltpu.make_async_remote_copy(
                src_ref=rows(*block) if src is None else src, dst_ref=rows(*block), send_sem=send_sems.at[k],
                recv_sem=recv_sems.at[k], device_id=to, device_id_type=MESH)

        mine = pltpu.make_async_copy(x_ref, rows(*me), local_sem)
        mine.start()
        first = [copy(0, me, sibling, src=x_ref)]
        first += [copy(1 + j, me, (*chip, mc), src=x_ref) for j, chip in enumerate(chips)]
        for cp in first:
            cp.start()
        passed = [copy(4 + j, (*chip, mc), sibling) for j, chip in enumerate(chips)]
        for j, chip in enumerate(chips):
            copy(1 + j, (*chip, mc), me).wait_recv()
            passed[j].start()
        copy(0, sibling, me).wait_recv()
        for j, chip in enumerate(chips):
            copy(4 + j, (*chip, 1 - mc), me).wait_recv()
        for cp in first + passed:
            cp.wait_send()
        mine.wait()

    return pl.pallas_call(
        body, name=name, out_shape=jax.ShapeDtypeStruct((N_DEV * m_per, n), x.dtype),
        in_specs=[pl.BlockSpec(memory_space=pltpu.VMEM)], out_specs=pl.BlockSpec(memory_space=pltpu.VMEM),
        scratch_shapes=[pltpu.SemaphoreType.DMA((7,)), pltpu.SemaphoreType.DMA((7,)), pltpu.SemaphoreType.DMA],
    )(x)


def _my_index():
    return 4 * lax.axis_index("x") + 2 * lax.axis_index("y") + lax.axis_index("c")


def _gather_stage(srcs, *, name):
    n = len(srcs)

    def body(*refs):
        x_refs, out_refs = refs[:n], refs[n:2 * n]
        send_sems, recv_sems, local_sems = refs[2 * n:]
        mx, my, mc = lax.axis_index("x"), lax.axis_index("y"), lax.axis_index("c")
        me, sibling = (mx, my, mc), (mx, my, 1 - mc)
        chips = [(1 - mx, my), (mx, 1 - my), (1 - mx, 1 - my)]

        def slot(i, px, py, pc):
            return out_refs[i].at[4 * px + 2 * py + pc]

        def copy(i, k, block, to, src=None):
            return pltpu.make_async_remote_copy(
                src_ref=slot(i, *block) if src is None else src, dst_ref=slot(i, *block), send_sem=send_sems.at[7 * i + k],
                recv_sem=recv_sems.at[7 * i + k], device_id=to, device_id_type=MESH)

        mine = [pltpu.make_async_copy(x_refs[i], slot(i, *me), local_sems.at[i]) for i in range(n)]
        first = [copy(i, 0, me, sibling, src=x_refs[i]) for i in range(n)]
        first += [copy(i, 1 + j, me, (*chip, mc), src=x_refs[i]) for j, chip in enumerate(chips) for i in range(n)]
        for cp in mine + first:
            cp.start()
        passed = []
        for j, chip in enumerate(chips):
            for i in range(n):
                copy(i, 1 + j, (*chip, mc), me).wait_recv()
                passed.append(copy(i, 4 + j, (*chip, mc), sibling))
                passed[-1].start()
        for i in range(n):
            copy(i, 0, sibling, me).wait_recv()
            for j, chip in enumerate(chips):
                copy(i, 4 + j, (*chip, 1 - mc), me).wait_recv()
        for cp in first + passed:
            cp.wait_send()
        for cp in mine:
            cp.wait()

    any_space = pl.BlockSpec(memory_space=pl.ANY)
    return pl.pallas_call(
        body, name=name, out_shape=[jax.ShapeDtypeStruct((N_DEV,) + s.shape, s.dtype) for s in srcs],
        in_specs=[any_space] * n, out_specs=[any_space] * n,
        scratch_shapes=[pltpu.SemaphoreType.DMA((7 * n,)), pltpu.SemaphoreType.DMA((7 * n,)), pltpu.SemaphoreType.DMA((n,))],
    )(*srcs)


def _adamw_math(g, w, m, v):
    c1, c2 = 1.0 - ADAM_B1 ** ADAM_STEP, 1.0 - ADAM_B2 ** ADAM_STEP
    m_new = ADAM_B1 * m + (1.0 - ADAM_B1) * g
    v_new = ADAM_B2 * v + (1.0 - ADAM_B2) * (g * g)
    delta = -ADAM_LR * ((m_new / c1) / (jnp.sqrt(v_new / c2) + ADAM_EPS) + ADAM_WD * w)
    return delta, m_new, v_new


def _adamw_big(slots, w, m, v, layer, *, name):
    _, r, c = slots.shape
    tr = _divisor_tile(r, max(8, (200 * 1024) // c // 8 * 8), 8)

    def body(s_ref, w_ref, m_ref, v_ref, g_out, d_out, m_out, v_out):
        g = s_ref[0].astype(F32)
        for p in range(1, N_DEV):
            g = g + s_ref[p].astype(F32)
        g_out[...] = g
        d_out[...], m_out[...], v_out[...] = _adamw_math(g, w_ref[...], m_ref[...], v_ref[...])

    blk = pl.BlockSpec((tr, c), lambda i: (i, 0))
    lay = pl.BlockSpec((None, tr, c), lambda i: (layer, i, 0))
    return pl.pallas_call(
        body, name=name, grid=(r // tr,), in_specs=[pl.BlockSpec((N_DEV, tr, c), lambda i: (0, i, 0)), lay, lay, lay],
        out_specs=[blk] * 4, out_shape=[jax.ShapeDtypeStruct((r, c), F32)] * 4,
        compiler_params=pltpu.CompilerParams(dimension_semantics=("parallel",)),
    )(slots, w, m, v)


def _sum_small(got, r_re, r_sh, *, name):
    per_dev = r_re + N_DEV * r_sh

    def body(got_ref, re_ref, sh_ref):
        mine = r_re + _my_index() * r_sh
        acc_re = got_ref[0:r_re, :]
        acc_sh = got_ref[pl.ds(pl.multiple_of(mine, 8), r_sh), :]
        for p in range(1, N_DEV):
            acc_re = acc_re + got_ref[p * per_dev:p * per_dev + r_re, :]
            acc_sh = acc_sh + got_ref[pl.ds(pl.multiple_of(p * per_dev + mine, 8), r_sh), :]
        re_ref[...] = acc_re
        sh_ref[...] = acc_sh

    return pl.pallas_call(body, name=name, out_shape=[jax.ShapeDtypeStruct((r_re, LANES), F32),
                                                       jax.ShapeDtypeStruct((r_sh, LANES), F32)])(got)


def _adamw_small(gs, ws, ms, vs, *, name):
    n = len(gs)

    def body(*refs):
        ins, outs = refs[:4 * n], refs[4 * n:]
        for i in range(n):
            res = _adamw_math(ins[i][...], ins[n + i][...], ins[2 * n + i][...], ins[3 * n + i][...])
            for j in range(3):
                outs[j * n + i][...] = res[j]

    out = pl.pallas_call(body, name=name, out_shape=[jax.ShapeDtypeStruct(a.shape, F32) for a in ws] * 3)(*gs, *ws, *ms, *vs)
    return out[:n], out[n:2 * n], out[2 * n:]


def _layout(shapes, row_align, total_align):
    lay, off = {}, 0
    for name, shape in shapes.items():
        size = int(np.prod(shape))
        rows = -(-size // LANES)
        rows = -(-rows // row_align) * row_align
        lay[name] = (off, rows, size, tuple(shape))
        off += rows
    return lay, -(-off // total_align) * total_align


def _pack(arrs, lay, total, dtype, lead=()):
    parts = []
    nl = len(lead)
    for name, (off, rows, size, shape) in lay.items():
        flat = arrs[name].astype(dtype).reshape(*lead, size)
        parts.append(jnp.pad(flat, [(0, 0)] * nl + [(0, rows * LANES - size)]).reshape(*lead, rows, LANES))
    used = sum(v[1] for v in lay.values())
    if total > used:
        parts.append(jnp.zeros((*lead, total - used, LANES), dtype))
    return jnp.concatenate(parts, axis=nl)


def _unpack(buf, lay, lead=()):
    out = {}
    nl = len(lead)
    for name, (off, rows, size, shape) in lay.items():
        part = lax.slice_in_dim(buf, off, off + rows, axis=nl).reshape(*lead, rows * LANES)
        out[name] = lax.slice_in_dim(part, 0, size, axis=nl).reshape(*lead, *shape)
    return out


_SHARD_AXIS = {
    "norm_mix": None, "norm_ffn": None, "norm_final": None, "gla_w_in": 2, "gla_w_g2": 2, "gla_b_g2": None,
    "gla_norm": None, "gla_w_out": 1, "cv_w_in": 2, "cv_b_in": 1, "cv_w_dw": 2, "cv_b_dw": 1, "cv_ln_g": 1,
    "cv_ln_b": 1, "cv_w_out": 1, "cv_b_out": 1, "sg_w_in": 2, "sg_b_in": 1, "sg_ln_g": 1, "sg_ln_b": 1, "sg_w_s": None,
    "sg_b_s": None, "sg_w_out": 1, "sg_b_out": 1, "hg_w_in": 2, "hg_lb_table": None, "hg_norm": None, "hg_w_out": 1,
    "ffn_w_up": 2, "ffn_w_dw": 2, "ffn_w_down": 1,
}
_MATMUL_WEIGHTS = ("gla_w_in", "gla_w_out", "cv_w_in", "cv_w_out", "sg_w_in", "sg_w_out", "hg_w_in", "hg_w_out",
                   "ffn_w_up", "ffn_w_down")
_NAMES = tuple(_SHARD_AXIS)


def kernel(x, norm_mix, norm_ffn, norm_final, gla_w_in, gla_w_g2, gla_b_g2, gla_norm, gla_w_out, cv_w_in, cv_b_in, cv_w_dw, cv_b_dw, cv_ln_g, cv_ln_b, cv_w_out, cv_b_out, sg_w_in, sg_b_in, sg_ln_g, sg_ln_b, sg_w_s, sg_b_s, sg_w_out, sg_b_out, hg_w_in, hg_lb_table, hg_norm, hg_w_out, ffn_w_up, ffn_w_dw, ffn_w_down, loss_target, m_norm_mix, m_norm_ffn, m_norm_final, m_gla_w_in, m_gla_w_g2, m_gla_b_g2, m_gla_norm, m_gla_w_out, m_cv_w_in, m_cv_b_in, m_cv_w_dw, m_cv_b_dw, m_cv_ln_g, m_cv_ln_b, m_cv_w_out, m_cv_b_out, m_sg_w_in, m_sg_b_in, m_sg_ln_g, m_sg_ln_b, m_sg_w_s, m_sg_b_s, m_sg_w_out, m_sg_b_out, m_hg_w_in, m_hg_lb_table, m_hg_norm, m_hg_w_out, m_ffn_w_up, m_ffn_w_dw, m_ffn_w_down, v_norm_mix, v_norm_ffn, v_norm_final, v_gla_w_in, v_gla_w_g2, v_gla_b_g2, v_gla_norm, v_gla_w_out, v_cv_w_in, v_cv_b_in, v_cv_w_dw, v_cv_b_dw, v_cv_ln_g, v_cv_ln_b, v_cv_w_out, v_cv_b_out, v_sg_w_in, v_sg_b_in, v_sg_ln_g, v_sg_ln_b, v_sg_w_s, v_sg_b_s, v_sg_w_out, v_sg_b_out, v_hg_w_in, v_hg_lb_table, v_hg_norm, v_hg_w_out, v_ffn_w_up, v_ffn_w_dw, v_ffn_w_down):
    local = dict(locals())
    wts = {n: local[n] for n in _NAMES}
    mom = {n: local["m_" + n] for n in _NAMES}
    var = {n: local["v_" + n] for n in _NAMES}
    small_all = [n for n in _NAMES if n not in _MATMUL_WEIGHTS]
    small_sharded = [n for n in small_all if _SHARD_AXIS[n] is not None]
    bsz, seq, d = x.shape
    depth = norm_mix.shape[0]

    stages = {}
    for layer in range(depth):
        kind = _MIXERS[layer % 4]
        stages[kind, layer] = {"w_in": (kind + "_w_in", layer // 4), "w_out": (kind + "_w_out", layer // 4)}
        stages["ffn", layer] = {"w_up": ("ffn_w_up", layer), "w_down": ("ffn_w_down", layer)}

    order = list(stages)
    shards = lambda stage: [wts[nm][idx].astype(BF16) for nm, idx in stages[stage].values()]
    gathers = {order[0]: _Side(shards(order[0]), False)}
    gathers[order[0]].lands = _gather_stage(gathers[order[0]].srcs, name="gather_first")
    scatters, waiting, down_gathers, down_scatters = {}, [], {}, {}

    def ride(kind, layer, forward):
        if not forward:
            return waiting.pop() if waiting else None
        at = order.index((kind, layer)) + 1
        if at == len(order):
            return None
        srcs = shards(order[at])
        if order[at][0] == "ffn":
            down_gathers[order[at][1]] = _Side(srcs[1:], False)
            srcs = srcs[:1]
        gathers[order[at]] = _Side(srcs, False)
        return gathers[order[at]]

    lay_sw, r_sw = _layout({n: wts[n].shape for n in small_sharded}, 8, 8)
    got_sw = _all_gather(_pack(wts, lay_sw, r_sw, F32), name="gather_small_weights")
    parts = _unpack(got_sw.reshape(N_DEV, r_sw, LANES), lay_sw, (N_DEV,))
    full_small = {n: wts[n] for n in small_all if _SHARD_AXIS[n] is None}
    for n in small_sharded:
        ax, shape = _SHARD_AXIS[n], wts[n].shape
        full_small[n] = jnp.moveaxis(parts[n], 0, ax).reshape(shape[:ax] + (N_DEV * shape[ax],) + shape[ax + 1:])

    def full_size(nm, land):
        _, r, c = land.shape
        if _SHARD_AXIS[nm] == 2:
            return land.transpose(1, 0, 2).reshape(r, N_DEV * c), land.transpose(0, 2, 1).reshape(N_DEV * c, r)
        return land.reshape(N_DEV * r, c), land.reshape(N_DEV * r, c).T

    def get_big(kind, layer):
        names = [nm for nm, _ in stages[kind, layer].values()]
        lands = gathers[kind, layer].lands
        if kind != "ffn" or layer not in down_gathers:
            return _oriented(kind, {key: full_size(nm, land) for key, nm, land in zip(stages[kind, layer], names, lands)})
        up, up_t = full_size(names[0], lands[0])
        f = up.shape[1] // 2

        def down_landed():
            down, down_t = full_size(names[1], down_gathers[layer].lands[0])
            return dict(w_down=down, w_down_t=down_t)

        return dict(w_up=up, w_up_t_gate=up_t[:f], w_up_t_val=up_t[f:], late=(down_gathers[layer], down_landed))

    def put_big(kind, layer, g):
        if kind == "ffn":
            k, f = g["w_up_gate"].shape
            halves = [g[key].reshape(k, N_DEV // 2, 2 * f // N_DEV) for key in ("w_up_gate", "w_up_val")]
            w_in = jnp.concatenate(halves, axis=1)
        else:
            w_in = jnp.concatenate([g["w_main"], g["w_glr"][:, :GLA_RANK]], axis=1) if kind == "gla" else g["w_in"]
            w_in = w_in.reshape(w_in.shape[0], N_DEV, w_in.shape[1] // N_DEV)
        sends = [w_in.transpose(1, 0, 2).astype(BF16)]
        if kind != "ffn":
            sends.append(row_slots(g["w_out"]))
        scatters[kind, layer] = _Side(sends, True)
        waiting.append(scatters[kind, layer])

    def row_slots(grad):
        return grad.reshape(N_DEV, grad.shape[0] // N_DEV, grad.shape[1]).astype(BF16)

    def put_early(layer, grad_w_down):
        down_scatters[layer] = _Side([row_slots(grad_w_down)], True)
        return down_scatters[layer]

    loss, dx, grads = _local_step(x.reshape(bsz * seq, d), loss_target.reshape(bsz * seq, d), _prep_small(full_small), seq,
                                  get_big, put_big, ride, put_early)
    loss = lax.psum(loss[0, 0], ("x", "y", "c"))

    gs = _small_grads(grads)
    small_repl = [n for n in small_all if _SHARD_AXIS[n] is None]
    lay_re, r_re = _layout({n: wts[n].shape for n in small_repl}, 8, 8)
    slots = {}
    for n in small_sharded:
        ax, shape = _SHARD_AXIS[n], wts[n].shape
        slots[n] = jnp.moveaxis(gs[n].reshape(shape[:ax] + (N_DEV, shape[ax]) + shape[ax + 1:]), ax, 0)
    sent = jnp.concatenate([_pack(gs, lay_re, r_re, F32), _pack(slots, lay_sw, r_sw, F32, (N_DEV,)).reshape(-1, LANES)])
    sum_re, sum_sh = _sum_small(_all_gather(sent, name="gather_small_grads"), r_re, r_sw, name="sum_small_grads")
    g_own = _unpack(sum_re, lay_re)
    g_own.update(_unpack(sum_sh, lay_sw))
    two_d = lambda a: a.reshape(-1, a.shape[-1])
    upd = _adamw_small(*[[two_d(src[n]) for n in small_all] for src in (g_own, wts, mom, var)], name="adamw_small")
    results = {n: [g_own[n]] + [part[i].reshape(wts[n].shape) for part in upd] for i, n in enumerate(small_all)}

    per_layer = {}
    for (kind, layer), side in scatters.items():
        lands = side.lands
        if kind == "ffn":
            lands = list(lands) + down_scatters[layer].lands
        for (nm, idx), land in zip(stages[kind, layer].values(), lands):
            three_d = lambda a: a.reshape((a.shape[0],) + land.shape[1:])
            per_layer.setdefault(nm, {})[idx] = _adamw_big(land, three_d(wts[nm]), three_d(mom[nm]), three_d(var[nm]), idx,
                                                           name="adamw_" + nm)
    for nm, by_idx in per_layer.items():
        outs = [by_idx[i] for i in range(len(by_idx))]
        results[nm] = [(outs[0][j] if len(outs) == 1 else jnp.stack([o[j] for o in outs])).reshape(wts[nm].shape)
                       for j in range(4)]
    out = [loss, dx.reshape(bsz, seq, d)]
    for j in range(4):
        out += [results[n][j] for n in _NAMES]
    return tuple(out)
```

```python
import functools
import math

import jax
import jax.numpy as jnp
import numpy as np
from jax import lax
from jax.experimental import pallas as pl
from jax.experimental.pallas import tpu as pltpu

F32 = jnp.float32
BF16 = jnp.bfloat16
EPS = 1e-6
N_DEV = 8
LANES = 128
SUBLANES_BF16 = 16
HALO = 32
GLA_HEADS, GLA_RANK, GLA_GATE_NORM, GLA_CHUNK = 4, 16, 16.0, 64
SGU_CHUNK, SGU_GROUPS = 128, 8
HGRN_EXPAND, HGRN_CHUNK = 128, 64
FFN_CONV_WIDTH = 3
ADAM_LR, ADAM_B1, ADAM_B2, ADAM_EPS, ADAM_WD, ADAM_STEP = 0.001, 0.9, 0.999, 1e-08, 0.01, 10
MESH = pl.DeviceIdType.MESH


def _sigmoid(x):
    return 0.5 * (jnp.tanh(0.5 * x) + 1.0)


def _silu(x):
    return x * _sigmoid(x)


def _log_sigmoid(x):
    return jnp.minimum(x, 0.0) - jnp.log(1.0 + jnp.exp(-jnp.abs(x)))


def _gelu(x):
    return 0.5 * x * (1.0 + jnp.tanh(math.sqrt(2.0 / math.pi) * (x + 0.044715 * (x * x * x))))


def _rms(x, g):
    return x * lax.rsqrt(jnp.mean(x * x, axis=-1, keepdims=True) + EPS) * g


def _layer_norm(x, g, b):
    xc = x - jnp.mean(x, axis=-1, keepdims=True)
    return xc * lax.rsqrt(jnp.mean(xc * xc, axis=-1, keepdims=True) + EPS) * g + b


def _dot_raw(a, b, dims):
    return lax.dot_general(a.astype(BF16), b.astype(BF16), (dims, ((), ())), preferred_element_type=F32)


@jax.custom_vjp
def _bdot(a, b):
    return _dot_raw(a, b, ((1,), (0,)))


@jax.custom_vjp
def _bdot_nt(a, b):
    return _dot_raw(a, b, ((1,), (1,)))


@jax.custom_vjp
def _bdot_tn(a, b):
    return _dot_raw(a, b, ((0,), (0,)))


_bdot.defvjp(lambda a, b: (_bdot(a, b), (a, b)), lambda r, g: (_bdot_nt(g, r[1]), _bdot_tn(r[0], g)))
_bdot_nt.defvjp(lambda a, b: (_bdot_nt(a, b), (a, b)), lambda r, g: (_bdot(g, r[1]), _bdot_tn(g, r[0])))
_bdot_tn.defvjp(lambda a, b: (_bdot_tn(a, b), (a, b)), lambda r, g: (_bdot_nt(r[1], g), _bdot(r[0], g)))


def _scan_rows(x, reverse):
    n = x.shape[0]
    row = lax.broadcasted_iota(jnp.int32, x.shape, 0)
    step = 1
    while step < n:
        if reverse:
            x = x + jnp.where(row < n - step, pltpu.roll(x, n - step, 0), 0.0)
        else:
            x = x + jnp.where(row >= step, pltpu.roll(x, step, 0), 0.0)
        step *= 2
    return x


@jax.custom_vjp
def _cumsum_rows(x):
    return _scan_rows(x, False)


_cumsum_rows.defvjp(lambda x: (_scan_rows(x, False), None), lambda _, g: (_scan_rows(g, True),))


def _divisor_tile(n, cap, unit):
    if n <= cap:
        return n
    best = None
    for t in range(unit, cap + 1, unit):
        if n % t == 0:
            best = t
    assert best is not None, (n, cap, unit)
    return best


def _const_map(nd):
    return lambda *_: (0,) * nd


class _Side:
    def __init__(self, srcs, scatter, parts=()):
        self.srcs, self.scatter, self.lands, self.parts = list(srcs), scatter, None, list(parts)

    @staticmethod
    def join(sides):
        sides = [s for s in sides if s is not None]
        if len(sides) < 2:
            return sides[0] if sides else None
        assert len({s.scatter for s in sides}) == 1
        return _Side([a for s in sides for a in s.srcs], sides[0].scatter, sides)

    def landed(self, lands):
        self.lands = list(lands)
        at = 0
        for part in self.parts:
            part.landed(self.lands[at:at + len(part.srcs)])
            at += len(part.srcs)


def _pallas(body, side, *, name, grid, in_specs, out_specs, out_shape, scratch_shapes=(), semantics):
    if side is None:
        return pl.pallas_call(body, name=name, grid=grid, in_specs=in_specs, out_specs=out_specs, out_shape=out_shape,
                              scratch_shapes=list(scratch_shapes),
                              compiler_params=pltpu.CompilerParams(dimension_semantics=semantics))
    single = not isinstance(out_shape, (list, tuple))
    out_specs, out_shape = ([out_specs], [out_shape]) if single else (list(out_specs), list(out_shape))
    n, n_in, n_out, n_scr = len(side.srcs), len(in_specs), len(out_shape), len(scratch_shapes)
    lands = [jax.ShapeDtypeStruct((N_DEV,) + (s.shape[1:] if side.scatter else s.shape), s.dtype) for s in side.srcs]

    def body2(*refs):
        x_refs, land_refs = refs[n_in:n_in + n], refs[n_in + n + n_out:n_in + 2 * n + n_out]
        send_sems, recv_sems, local_sems = refs[-3:]
        steps = [pl.program_id(a) for a in range(len(grid))]
        first = functools.reduce(jnp.logical_and, [s == 0 for s in steps])
        last = functools.reduce(jnp.logical_and, [s == g - 1 for s, g in zip(steps, grid)])

        def copies():
            mx, my, mc = lax.axis_index("x"), lax.axis_index("y"), lax.axis_index("c")
            me = 4 * mx + 2 * my + mc
            mine = [pltpu.make_async_copy(x_refs[i].at[me] if side.scatter else x_refs[i], land_refs[i].at[me],
                                          local_sems.at[i]) for i in range(n)]
            sends, recvs = [], []
            for k in range(1, N_DEV):
                px = 1 - mx if k & 4 else mx
                py = 1 - my if k & 2 else my
                pc = 1 - mc if k & 1 else mc
                peer = 4 * px + 2 * py + pc
                for i in range(n):
                    sems = dict(send_sem=send_sems.at[7 * i + k - 1], recv_sem=recv_sems.at[7 * i + k - 1],
                                device_id=(px, py, pc), device_id_type=MESH)
                    src = x_refs[i].at[peer] if side.scatter else x_refs[i]
                    sends.append(pltpu.make_async_remote_copy(src_ref=src, dst_ref=land_refs[i].at[me], **sems))
                    recvs.append(pltpu.make_async_remote_copy(src_ref=src, dst_ref=land_refs[i].at[peer], **sems))
            return mine, sends, recvs

        @pl.when(first)
        def _():
            mine, sends, _ = copies()
            for cp in mine + sends:
                cp.start()

        body(*refs[:n_in], *refs[n_in + n:n_in + n + n_out], *refs[n_in + 2 * n + n_out:n_in + 2 * n + n_out + n_scr])

        @pl.when(last)
        def _():
            mine, sends, recvs = copies()
            for cp in recvs:
                cp.wait_recv()
            for cp in sends:
                cp.wait_send()
            for cp in mine:
                cp.wait()

    any_space = pl.BlockSpec(memory_space=pl.ANY)
    call = pl.pallas_call(
        body2, name=name, grid=grid, in_specs=list(in_specs) + [any_space] * n, out_specs=out_specs + [any_space] * n,
        out_shape=out_shape + lands,
        scratch_shapes=list(scratch_shapes) + [pltpu.SemaphoreType.DMA((7 * n,)), pltpu.SemaphoreType.DMA((7 * n,)),
                                               pltpu.SemaphoreType.DMA((n,))],
        compiler_params=pltpu.CompilerParams(dimension_semantics=("arbitrary",) * len(grid)))

    def run(*args):
        res = call(*args, *side.srcs)
        side.landed(res[n_out:])
        return res[0] if single else res[:n_out]

    return run


def _mm(a, b, *, add=None, bias=None, out_dtype=F32, name, side=None):
    m, k = a.shape
    k2, n = b.shape
    assert k == k2
    tn = _divisor_tile(n, max(LANES, min(1408, (6 << 20) // (2 * k) // LANES * LANES)), LANES)
    tm = _divisor_tile(m, max(256, min(1024, (4 << 20) // (a.dtype.itemsize * k) // 256 * 256)), 8)
    has_bias, has_add = bias is not None, add is not None

    def body(*refs):
        a_ref, b_ref = refs[0], refs[1]
        o_ref = refs[-1]
        acc = jnp.dot(a_ref[...].astype(BF16), b_ref[...], preferred_element_type=F32)
        pos = 2
        if has_bias:
            acc = acc + refs[pos][...]
            pos += 1
        if has_add:
            acc = acc + refs[pos][...].astype(F32)
        o_ref[...] = acc.astype(o_ref.dtype)

    in_specs = [pl.BlockSpec((tm, k), lambda i, j: (i, 0)), pl.BlockSpec((k, tn), lambda i, j: (0, j))]
    args = [a, b]
    if has_bias:
        in_specs.append(pl.BlockSpec((1, tn), lambda i, j: (0, j)))
        args.append(bias)
    if has_add:
        in_specs.append(pl.BlockSpec((tm, tn), lambda i, j: (i, j)))
        args.append(add)
    return _pallas(
        body, side, name=name, grid=(m // tm, n // tn), in_specs=in_specs,
        out_specs=pl.BlockSpec((tm, tn), lambda i, j: (i, j)),
        out_shape=jax.ShapeDtypeStruct((m, n), out_dtype),
        semantics=("parallel", "parallel"),
    )(*args)


def _mm_tn(a, g, *, name):
    m, k = a.shape
    m2, n = g.shape
    assert m == m2
    tk = _divisor_tile(k, 1408, LANES)
    tn = _divisor_tile(n, 1408, LANES)
    tm = _divisor_tile(m, 1024, 8)

    def body(a_ref, g_ref, o_ref):
        @pl.when(pl.program_id(2) == 0)
        def _():
            o_ref[...] = jnp.zeros_like(o_ref)

        o_ref[...] += _dot_raw(a_ref[...], g_ref[...], ((0,), (0,)))

    return pl.pallas_call(
        body, name=name, grid=(k // tk, n // tn, m // tm),
        in_specs=[pl.BlockSpec((tm, tk), lambda i, j, t: (t, i)), pl.BlockSpec((tm, tn), lambda i, j, t: (t, j))],
        out_specs=pl.BlockSpec((tk, tn), lambda i, j, t: (i, j)),
        out_shape=jax.ShapeDtypeStruct((k, n), F32),
        compiler_params=pltpu.CompilerParams(dimension_semantics=("parallel", "parallel", "arbitrary")),
    )(a, g)


def _tile_call(name, fn, tiled, params, out_tiled, out_acc, tile, side=None):
    tiled = [t if isinstance(t, tuple) else (t, t.shape[1], 0) for t in tiled]
    t_rows = tiled[0][0].shape[0]
    tile = min(tile, t_rows)
    assert t_rows % tile == 0
    n_t, n_p, n_o = len(tiled), len(params), len(out_tiled)

    def body(*refs):
        vals = [r[...] for r in refs[: n_t + n_p]]
        touts, aouts = fn(*vals)
        for r, v in zip(refs[n_t + n_p: n_t + n_p + n_o], touts):
            r[...] = v.astype(r.dtype)
        acc_refs = refs[n_t + n_p + n_o:]
        if acc_refs:
            @pl.when(pl.program_id(0) == 0)
            def _():
                for r in acc_refs:
                    r[...] = jnp.zeros_like(r)

            for r, v in zip(acc_refs, aouts):
                r[...] += v

    in_specs = [pl.BlockSpec((tile, w), lambda i, cb=cb: (i, cb)) for _, w, cb in tiled]
    in_specs += [pl.BlockSpec(p.shape, _const_map(p.ndim)) for p in params]
    out_specs = [pl.BlockSpec((tile, w), lambda i: (i, 0)) for w, _ in out_tiled]
    out_specs += [pl.BlockSpec(s, _const_map(len(s))) for s in out_acc]
    out_shape = [jax.ShapeDtypeStruct((t_rows, w), dt) for w, dt in out_tiled]
    out_shape += [jax.ShapeDtypeStruct(s, F32) for s in out_acc]
    res = _pallas(
        body, side, name=name, grid=(t_rows // tile,), in_specs=in_specs, out_specs=out_specs, out_shape=out_shape,
        semantics=("arbitrary" if out_acc else "parallel",),
    )(*[t[0] for t in tiled], *params)
    return res[:n_o], res[n_o:]


def _rms_fwd(x, g, name):
    (h,), _ = _tile_call(name, lambda xv, gv: ([_rms(xv, gv)], []), [x], [g], [(x.shape[1], BF16)], [], 512)
    return h


def _rms_bwd(x, g, dh, dres, name, side=None):
    def fn(xv, dhv, drv, gv):
        _, vjp = jax.vjp(_rms, xv, gv)
        dx, dg = vjp(dhv.astype(F32))
        return [drv + dx], [dg]

    (dx,), (dg,) = _tile_call(name, fn, [x, dh, dres], [g], [(x.shape[1], F32)], [g.shape], 512, side)
    return dx, dg


def _colsum(x, name):
    _, (s,) = _tile_call(name, lambda xv: ([], [jnp.sum(xv.astype(F32), axis=0, keepdims=True)]), [x], [], [],
                         [(1, x.shape[1])], 512)
    return s


def _seq_flags(i, tiles_per_seq):
    pos = i % tiles_per_seq
    return pos == 0, pos == tiles_per_seq - 1


def _dwconv_fwd(x, w, b, seq, name, side=None):
    t_rows, ch = x.shape
    kw = w.shape[0]
    tile = min(512, seq)
    cb = _divisor_tile(ch, 256, LANES)
    tps, hb = seq // tile, tile // HALO

    def body(x_ref, halo_ref, w_ref, b_ref, y_ref, pad_ref):
        first, _ = _seq_flags(pl.program_id(0), tps)
        pad_ref[0:HALO, :] = jnp.where(first, 0.0, halo_ref[...])
        pad_ref[HALO:HALO + tile, :] = x_ref[...]
        for r0 in range(0, tile, HALO):
            acc = jnp.broadcast_to(b_ref[...], (HALO, cb))
            for k in range(kw):
                acc = acc + pad_ref[pl.ds(HALO - (kw - 1) + k + r0, HALO), :] * w_ref[k:k + 1, :]
            y_ref[pl.ds(r0, HALO), :] = acc

    return _pallas(
        body, side, name=name, grid=(t_rows // tile, ch // cb),
        in_specs=[pl.BlockSpec((tile, cb), lambda i, j: (i, j)),
                  pl.BlockSpec((HALO, cb), lambda i, j: (jnp.maximum(i * hb - 1, 0), j)),
                  pl.BlockSpec((kw, cb), lambda i, j: (0, j)), pl.BlockSpec((1, cb), lambda i, j: (0, j))],
        out_specs=pl.BlockSpec((tile, cb), lambda i, j: (i, j)),
        out_shape=jax.ShapeDtypeStruct((t_rows, ch), F32),
        scratch_shapes=[pltpu.VMEM((HALO + tile, cb), F32)],
        semantics=("parallel", "parallel"),
    )(x, x, w, b)


def _dwconv_bwd(x, dy, w, seq, name, side=None):
    t_rows, ch = x.shape
    kw = w.shape[0]
    tile = min(512, seq)
    cb = _divisor_tile(ch, 256, LANES)
    tps, hb, n_hb = seq // tile, tile // HALO, t_rows // HALO

    def body(x_ref, xh_ref, dy_ref, dyh_ref, w_ref, dx_ref, dw_ref, db_ref, xpad, dypad, sums):
        i = pl.program_id(1)
        first, last = _seq_flags(i, tps)

        @pl.when(i == 0)
        def _():
            sums[...] = jnp.zeros_like(sums)

        xpad[0:HALO, :] = jnp.where(first, 0.0, xh_ref[...])
        xpad[HALO:HALO + tile, :] = x_ref[...]
        dypad[0:tile, :] = dy_ref[...]
        dypad[tile:tile + HALO, :] = jnp.where(last, 0.0, dyh_ref[...])
        fold = lambda v: functools.reduce(jnp.add, [v[r:r + 8] for r in range(0, HALO, 8)])
        for r0 in range(0, tile, HALO):
            dyc = dy_ref[pl.ds(r0, HALO), :]
            acc = jnp.zeros((HALO, cb), F32)
            for k in range(kw):
                acc = acc + dypad[pl.ds(kw - 1 - k + r0, HALO), :] * w_ref[k:k + 1, :]
                sums[8 * k:8 * k + 8, :] += fold(dyc * xpad[pl.ds(HALO - (kw - 1) + k + r0, HALO), :])
            dx_ref[pl.ds(r0, HALO), :] = acc
            sums[8 * kw:8 * kw + 8, :] += fold(dyc)

        @pl.when(i == t_rows // tile - 1)
        def _():
            for k in range(kw):
                dw_ref[k:k + 1, :] = jnp.sum(sums[8 * k:8 * k + 8, :], axis=0, keepdims=True)
            db_ref[...] = jnp.sum(sums[8 * kw:8 * kw + 8, :], axis=0, keepdims=True)

    return _pallas(
        body, side, name=name, grid=(ch // cb, t_rows // tile),
        in_specs=[pl.BlockSpec((tile, cb), lambda j, i: (i, j)),
                  pl.BlockSpec((HALO, cb), lambda j, i: (jnp.maximum(i * hb - 1, 0), j)),
                  pl.BlockSpec((tile, cb), lambda j, i: (i, j)),
                  pl.BlockSpec((HALO, cb), lambda j, i: (jnp.minimum((i + 1) * hb, n_hb - 1), j)),
                  pl.BlockSpec((kw, cb), lambda j, i: (0, j))],
        out_specs=[pl.BlockSpec((tile, cb), lambda j, i: (i, j)), pl.BlockSpec((kw, cb), lambda j, i: (0, j)),
                   pl.BlockSpec((1, cb), lambda j, i: (0, j))],
        out_shape=[jax.ShapeDtypeStruct((t_rows, ch), F32), jax.ShapeDtypeStruct((kw, ch), F32),
                   jax.ShapeDtypeStruct((1, ch), F32)],
        scratch_shapes=[pltpu.VMEM((HALO + tile, cb), F32), pltpu.VMEM((tile + HALO, cb), F32),
                        pltpu.VMEM((8 * (kw + 1), cb), F32)],
        semantics=("parallel", "arbitrary"),
    )(x, x, dy, dy, w)


_ROWS = SUBLANES_BF16


def _lane_chunks(width, cap=6 * LANES):
    return [slice(c0, min(c0 + cap, width)) for c0 in range(0, width, cap)]


def _tap_rows(w_ref, cols):
    return [w_ref[k:k + 1, cols] for k in range(FFN_CONV_WIDTH)]


def _conv3_at(pad, taps, row, cols):
    z = pad[pl.ds(row, _ROWS), cols] * taps[2]
    z = z + pad[pl.ds(row - 1, _ROWS), cols] * taps[1]
    return z + pad[pl.ds(row - 2, _ROWS), cols] * taps[0]


def _ffn_mid_fwd(u, w, seq, name, side=None):
    t_rows, f2 = u.shape
    f = f2 // 2
    tile = min(256, seq)
    cb = _divisor_tile(f, 1408, LANES)
    nj, tps, hb, hl = f // cb, seq // tile, tile // SUBLANES_BF16, SUBLANES_BF16

    def body(ug_ref, uv_ref, hg_ref, hv_ref, wg_ref, wv_ref, a_ref, gpad, vpad):
        first, _ = _seq_flags(pl.program_id(0), tps)
        for t_ref, h_ref, pad in ((ug_ref, hg_ref, gpad), (uv_ref, hv_ref, vpad)):
            pad[0:hl, :] = jnp.where(first, 0.0, h_ref[...].astype(F32))
            pad[hl:hl + tile, :] = t_ref[...].astype(F32)
        for cols in _lane_chunks(cb):
            wg, wv = _tap_rows(wg_ref, cols), _tap_rows(wv_ref, cols)
            for r0 in range(0, tile, _ROWS):
                zg = _conv3_at(gpad, wg, hl + r0, cols)
                zv = _conv3_at(vpad, wv, hl + r0, cols)
                half = 0.5 * zg
                a_ref[pl.ds(r0, _ROWS), cols] = ((jnp.tanh(half) + 1.0) * half * zv).astype(a_ref.dtype)

    halo_map = lambda off: (lambda i, j: (jnp.maximum(i * hb - 1, 0), j + off))
    return _pallas(
        body, side, name=name, grid=(t_rows // tile, nj),
        in_specs=[pl.BlockSpec((tile, cb), lambda i, j: (i, j)), pl.BlockSpec((tile, cb), lambda i, j: (i, j + nj)),
                  pl.BlockSpec((hl, cb), halo_map(0)), pl.BlockSpec((hl, cb), halo_map(nj)),
                  pl.BlockSpec((3, cb), lambda i, j: (0, j)), pl.BlockSpec((3, cb), lambda i, j: (0, j + nj))],
        out_specs=pl.BlockSpec((tile, cb), lambda i, j: (i, j)),
        out_shape=jax.ShapeDtypeStruct((t_rows, f), BF16),
        scratch_shapes=[pltpu.VMEM((hl + tile, cb), F32), pltpu.VMEM((hl + tile, cb), F32)],
        semantics=("parallel", "parallel"),
    )(u, u, u, u, w, w)


def _ffn_mid_bwd(u, da, w, seq, name, side=None):
    t_rows, f2 = u.shape
    f = f2 // 2
    tile = min(256, seq)
    cb = _divisor_tile(f, 1408, LANES)
    hl = SUBLANES_BF16
    nj, tps, hb, n_hb, ext = f // cb, seq // tile, tile // hl, t_rows // hl, tile + hl

    def body(ug_ref, uv_ref, pg_ref, pv_ref, ng_ref, nv_ref, da_ref, dan_ref, wg_ref, wv_ref,
             dug_ref, duv_ref, dwg_ref, dwv_ref, gpad, vpad, dzg, dzv):
        i = pl.program_id(1)
        first, last = _seq_flags(i, tps)

        @pl.when(i == 0)
        def _():
            dwg_ref[...] = jnp.zeros_like(dwg_ref)
            dwv_ref[...] = jnp.zeros_like(dwv_ref)

        for t_ref, p_ref, n_ref, pad in ((ug_ref, pg_ref, ng_ref, gpad), (uv_ref, pv_ref, nv_ref, vpad)):
            pad[0:hl, :] = jnp.where(first, 0.0, p_ref[...].astype(F32))
            pad[hl:hl + tile, :] = t_ref[...].astype(F32)
            pad[hl + tile:hl + ext, :] = jnp.where(last, 0.0, n_ref[...].astype(F32))
        for cols in _lane_chunks(cb):
            wg, wv = _tap_rows(wg_ref, cols), _tap_rows(wv_ref, cols)
            for r0 in range(0, ext, _ROWS):
                zg = _conv3_at(gpad, wg, hl + r0, cols)
                zv = _conv3_at(vpad, wv, hl + r0, cols)
                if r0 < tile:
                    da = da_ref[pl.ds(r0, _ROWS), cols].astype(F32)
                else:
                    da = jnp.where(last, 0.0, dan_ref[:, cols].astype(F32))
                sg = _sigmoid(zg)
                dzg[pl.ds(r0, _ROWS), cols] = da * zv * (sg * (1.0 + zg * (1.0 - sg)))
                dzv[pl.ds(r0, _ROWS), cols] = da * (zg * sg)
        for dz, w_ref, pad, du_ref, dw_ref in ((dzg, wg_ref, gpad, dug_ref, dwg_ref), (dzv, wv_ref, vpad, duv_ref, dwv_ref)):
            for cols in _lane_chunks(cb):
                taps = _tap_rows(w_ref, cols)
                width = cols.stop - cols.start
                acc = [jnp.zeros((8, width), F32) for _ in range(FFN_CONV_WIDTH)]
                for r0 in range(0, tile, _ROWS):
                    d0 = dz[pl.ds(r0, _ROWS), cols]
                    du = dz[pl.ds(r0 + 2, _ROWS), cols] * taps[0] + dz[pl.ds(r0 + 1, _ROWS), cols] * taps[1] + d0 * taps[2]
                    du_ref[pl.ds(r0, _ROWS), cols] = du.astype(du_ref.dtype)
                    for k in range(FFN_CONV_WIDTH):
                        prod = d0 * pad[pl.ds(hl - 2 + k + r0, _ROWS), cols]
                        acc[k] = acc[k] + prod[0:8] + prod[8:16]
                for k in range(FFN_CONV_WIDTH):
                    dw_ref[k:k + 1, cols] += jnp.sum(acc[k], axis=0, keepdims=True)

    prev_map = lambda off: (lambda j, i: (jnp.maximum(i * hb - 1, 0), j + off))
    next_map = lambda off: (lambda j, i: (jnp.minimum((i + 1) * hb, n_hb - 1), j + off))
    tile_spec = lambda off: pl.BlockSpec((tile, cb), lambda j, i: (i, j + off))
    w_spec = lambda off: pl.BlockSpec((3, cb), lambda j, i: (0, j + off))
    return _pallas(
        body, side, name=name, grid=(nj, t_rows // tile),
        in_specs=[tile_spec(0), tile_spec(nj), pl.BlockSpec((hl, cb), prev_map(0)), pl.BlockSpec((hl, cb), prev_map(nj)),
                  pl.BlockSpec((hl, cb), next_map(0)), pl.BlockSpec((hl, cb), next_map(nj)),
                  tile_spec(0), pl.BlockSpec((hl, cb), next_map(0)), w_spec(0), w_spec(nj)],
        out_specs=[tile_spec(0), tile_spec(0), w_spec(0), w_spec(0)],
        out_shape=[jax.ShapeDtypeStruct((t_rows, f), BF16), jax.ShapeDtypeStruct((t_rows, f), BF16),
                   jax.ShapeDtypeStruct((3, f), F32), jax.ShapeDtypeStruct((3, f), F32)],
        scratch_shapes=[pltpu.VMEM((hl + ext, cb), F32), pltpu.VMEM((hl + ext, cb), F32),
                        pltpu.VMEM((ext, cb), F32), pltpu.VMEM((ext, cb), F32)],
        semantics=("parallel", "arbitrary"),
    )(u, u, u, u, u, u, da, da, w, w)


def _gla_chunk(q, k, v, lg, st, *, scale, chunk):
    row = lax.broadcasted_iota(jnp.int32, (chunk, chunk), 0)
    col = lax.broadcasted_iota(jnp.int32, (chunk, chunk), 1)
    causal = col <= row
    b = _cumsum_rows(lg)
    upto_mid = lax.broadcasted_iota(jnp.int32, lg.shape, 0) <= chunk // 2
    b_mid = jnp.sum(jnp.where(upto_mid, lg, 0.0), axis=0, keepdims=True)
    b_last = jnp.sum(lg, axis=0, keepdims=True)
    qs = q * scale
    scores = _bdot_nt(qs * jnp.exp(b - b_mid), k * jnp.exp(b_mid - b))
    o = _bdot(jnp.where(causal, scores, 0.0), v)
    o = o + _bdot_nt(qs * jnp.exp(b), st)
    st_new = st * jnp.exp(b_last) + _bdot_tn(v, k * jnp.exp(b_last - b))
    return o, st_new


_CHUNKS_PER_STEP = 4


def _gla_specs(specs, rows, n_blocks, reverse):
    if reverse:
        row = lambda bi, ci: bi * n_blocks + (n_blocks - 1 - ci)
    else:
        row = lambda bi, ci: bi * n_blocks + ci
    return [pl.BlockSpec((rows, w), lambda bi, ci, cb=cb: (row(bi, ci), cb)) for _, w, cb in specs], row


def _gla_fwd(q, k, v, lg, *, heads, dk, dv, scale, chunk, seq, name, side=None):
    t_rows = q[0].shape[0]
    per = _CHUNKS_PER_STEP if (seq // chunk) % _CHUNKS_PER_STEP == 0 else 1
    n_blocks = seq // (per * chunk)
    fn = functools.partial(_gla_chunk, scale=scale, chunk=chunk)

    def body(q_ref, k_ref, v_ref, lg_ref, o_ref, sts_ref, st_ref):
        @pl.when(pl.program_id(1) == 0)
        def _():
            st_ref[...] = jnp.zeros_like(st_ref)

        ks = [slice(h * dk, (h + 1) * dk) for h in range(heads)]
        vs = [slice(h * dv, (h + 1) * dv) for h in range(heads)]
        st = [st_ref[vs[h], :] for h in range(heads)]
        for s in range(per):
            rows = pl.ds(s * chunk, chunk)
            for h in range(heads):
                sts_ref[s, vs[h], :] = st[h]
                o, st[h] = fn(q_ref[rows, ks[h]].astype(F32), k_ref[rows, ks[h]].astype(F32),
                              v_ref[rows, vs[h]].astype(F32), lg_ref[rows, ks[h]], st[h])
                o_ref[rows, vs[h]] = o
        for h in range(heads):
            st_ref[vs[h], :] = st[h]

    in_specs, row = _gla_specs([q, k, v, lg], per * chunk, n_blocks, False)
    return _pallas(
        body, side, name=name, grid=(t_rows // seq, n_blocks), in_specs=in_specs,
        out_specs=[pl.BlockSpec((per * chunk, heads * dv), lambda bi, ci: (row(bi, ci), 0)),
                   pl.BlockSpec((per, heads * dv, dk), lambda bi, ci: (row(bi, ci), 0, 0))],
        out_shape=[jax.ShapeDtypeStruct((t_rows, heads * dv), F32),
                   jax.ShapeDtypeStruct((t_rows // chunk, heads * dv, dk), F32)],
        scratch_shapes=[pltpu.VMEM((heads * dv, dk), F32)],
        semantics=("arbitrary", "arbitrary"),
    )(q[0], k[0], v[0], lg[0])


def _gla_bwd(q, k, v, lg, states, do, *, heads, dk, dv, scale, chunk, seq, out_dtypes, name, side=None):
    t_rows = q[0].shape[0]
    per = _CHUNKS_PER_STEP if (seq // chunk) % _CHUNKS_PER_STEP == 0 else 1
    n_blocks = seq // (per * chunk)
    fn = functools.partial(_gla_chunk, scale=scale, chunk=chunk)

    def body(q_ref, k_ref, v_ref, lg_ref, do_ref, sts_ref, dq_ref, dk_ref, dv_ref, dlg_ref, dst_ref):
        @pl.when(pl.program_id(1) == 0)
        def _():
            dst_ref[...] = jnp.zeros_like(dst_ref)

        ks = [slice(h * dk, (h + 1) * dk) for h in range(heads)]
        vs = [slice(h * dv, (h + 1) * dv) for h in range(heads)]
        dst = [dst_ref[vs[h], :] for h in range(heads)]
        for s in reversed(range(per)):
            rows = pl.ds(s * chunk, chunk)
            for h in range(heads):
                _, vjp = jax.vjp(fn, q_ref[rows, ks[h]].astype(F32), k_ref[rows, ks[h]].astype(F32),
                                 v_ref[rows, vs[h]].astype(F32), lg_ref[rows, ks[h]], sts_ref[s, vs[h], :])
                dq, dkk, dvv, dlg, dst[h] = vjp((do_ref[rows, vs[h]].astype(F32), dst[h]))
                dq_ref[rows, ks[h]] = dq.astype(dq_ref.dtype)
                dk_ref[rows, ks[h]] = dkk.astype(dk_ref.dtype)
                dv_ref[rows, vs[h]] = dvv.astype(dv_ref.dtype)
                dlg_ref[rows, ks[h]] = dlg
        for h in range(heads):
            dst_ref[vs[h], :] = dst[h]

    do_view = (do, heads * dv, 0)
    in_specs, row = _gla_specs([q, k, v, lg, do_view], per * chunk, n_blocks, True)
    in_specs.append(pl.BlockSpec((per, heads * dv, dk), lambda bi, ci: (row(bi, ci), 0, 0)))
    wide = lambda w: pl.BlockSpec((per * chunk, w), lambda bi, ci: (row(bi, ci), 0))
    return _pallas(
        body, side, name=name, grid=(t_rows // seq, n_blocks), in_specs=in_specs,
        out_specs=[wide(heads * dk), wide(heads * dk), wide(heads * dv), wide(heads * dk)],
        out_shape=[jax.ShapeDtypeStruct((t_rows, heads * dk), out_dtypes[0]),
                   jax.ShapeDtypeStruct((t_rows, heads * dk), out_dtypes[1]),
                   jax.ShapeDtypeStruct((t_rows, heads * dv), out_dtypes[2]),
                   jax.ShapeDtypeStruct((t_rows, heads * dk), F32)],
        scratch_shapes=[pltpu.VMEM((heads * dv, dk), F32)],
        semantics=("arbitrary", "arbitrary"),
    )(q[0], k[0], v[0], lg[0], do, states)


def _head_rms_gate(o, r, g, heads):
    d = o.shape[1] // heads
    parts = [_rms(o[:, h * d:(h + 1) * d], g) for h in range(heads)]
    return jnp.concatenate(parts, axis=1) * _silu(r)


def _gla_gate(glr, w_g2p, b_g2):
    return _log_sigmoid(_bdot(glr, w_g2p) + b_g2) * (1.0 / GLA_GATE_NORM)


def _glu(a, gate, b_in):
    d = a.shape[1]
    return (a + b_in[:, :d]) * _sigmoid(gate + b_in[:, d:])


def _ln_silu(y, g, b):
    return _silu(_layer_norm(y, g, b))


def _sgu(pre, b_in, ln_g, ln_b, w_s, b_st):
    d = pre.shape[1] // 2
    gd = d // SGU_GROUPS
    uv = _gelu(pre + b_in)
    u, v = uv[:, :d], _layer_norm(uv[:, d:], ln_g, ln_b)
    row = lax.broadcasted_iota(jnp.int32, (SGU_CHUNK, SGU_CHUNK), 0)
    col = lax.broadcasted_iota(jnp.int32, (SGU_CHUNK, SGU_CHUNK), 1)
    lane = lax.broadcasted_iota(jnp.int32, b_st.shape, 1)
    rows = []
    for c in range(pre.shape[0] // SGU_CHUNK):
        rs = slice(c * SGU_CHUNK, (c + 1) * SGU_CHUNK)
        parts = []
        for g in range(SGU_GROUPS):
            wg = jnp.where(col <= row, w_s[g], 0.0)
            bias = jnp.sum(jnp.where(lane == g, b_st, 0.0), axis=1, keepdims=True)
            parts.append(_bdot(wg, v[rs, g * gd:(g + 1) * gd]) + bias)
        rows.append(jnp.concatenate(parts, axis=1))
    s = rows[0] if len(rows) == 1 else jnp.concatenate(rows, axis=0)
    return u * s


def _hgrn_pre(q, f, table, layer):
    t = table - jnp.max(table, axis=0, keepdims=True)
    e = jnp.exp(t)
    sm = e / jnp.sum(e, axis=0, keepdims=True)
    rows = lax.broadcasted_iota(jnp.int32, table.shape, 0)
    lb = jnp.sum(jnp.where((rows >= 1) & (rows <= layer), sm, 0.0), axis=0, keepdims=True)
    sf = _sigmoid(f)
    return _silu(q), (1.0 - lb) * (1.0 - sf), jnp.log(lb + (1.0 - lb) * sf)


def _ffn_fwd(x, w, seq, sv):
    sv["h2"] = _rms_fwd(x, w["norm"], "ffn_norm")
    late = w.get("late")
    sv["u"] = _mm(sv["h2"], w["w_up"], out_dtype=BF16, name="ffn_up", side=late[0] if late else None)
    sv["a"] = _ffn_mid_fwd(sv["u"], w["w_dw"], seq, "ffn_mid", sv.pop("side", None))
    if late:
        sv["late_w"] = late[1]()
        w = dict(w, **sv["late_w"])
    return _mm(sv["a"], w["w_down"], add=x, name="ffn_down")


def _ffn_bwd(x, dy, w, seq, sv):
    g = {}
    da = _mm(dy, w["w_down_t"], out_dtype=BF16, name="ffn_down_dx")
    g["w_down"] = _mm_tn(sv["a"], dy, name="ffn_down_dw")
    early = sv.pop("put_early", None)
    side = _Side.join([sv.pop("side", None), early(g["w_down"]) if early else None])
    dug, duv, dwg, dwv = _ffn_mid_bwd(sv["u"], da, w["w_dw"], seq, "ffn_mid_bwd", side)
    g["w_dw"] = jnp.concatenate([dwg, dwv], axis=1)
    g["w_up_gate"] = _mm_tn(sv["h2"], dug, name="ffn_up_dw")
    g["w_up_val"] = _mm_tn(sv["h2"], duv, name="ffn_up_dw")
    dh = _mm(dug, w["w_up_t_gate"], out_dtype=F32, name="ffn_up_dx")
    dh = _mm(duv, w["w_up_t_val"], add=dh, out_dtype=BF16, name="ffn_up_dx2")
    dx, g["norm"] = _rms_bwd(x, w["norm"], dh, dy, "ffn_norm_bwd")
    return dx, g


def _gla_layer_fwd(x, h, w, seq, sv):
    d = x.shape[1]
    dkt = d // 2
    dk, dv = dkt // GLA_HEADS, d // GLA_HEADS
    proj = _mm(h, w["w_main"], out_dtype=F32, name="gla_in")
    glr = _mm(h, w["w_glr"], out_dtype=BF16, name="gla_in_g")
    (lg,), _ = _tile_call("gla_gate", lambda a, b, c: ([_gla_gate(a.astype(F32), b, c)], []), [glr],
                          [w["w_g2p"], w["b_g2"]], [(dkt, F32)], [], 512)
    q, k, v, r = (proj, dkt, 0), (proj, dkt, 1), (proj, d, 1), (proj, d, 2)
    o, states = _gla_fwd(q, k, v, (lg, dkt, 0), heads=GLA_HEADS, dk=dk, dv=dv, scale=dk ** -0.5, chunk=GLA_CHUNK,
                         seq=seq, name="gla_core", side=sv.pop("side", None))
    (o2,), _ = _tile_call("gla_post", lambda ov, rv, gv: ([_head_rms_gate(ov, rv.astype(F32), gv, GLA_HEADS)], []),
                          [o, r], [w["norm"]], [(d, BF16)], [], 256)
    sv.update(proj=proj, glr=glr, lg=lg, o=o, states=states, o2=o2)
    return _mm(o2, w["w_out"], add=x, name="mix_out")


def _gla_layer_bwd(h, dy, w, seq, sv):
    d = dy.shape[1]
    dkt = d // 2
    dk, dv = dkt // GLA_HEADS, d // GLA_HEADS
    proj, glr, lg, o = sv["proj"], sv["glr"], sv["lg"], sv["o"]
    g = {}
    do2 = _mm(dy, w["w_out_t"], out_dtype=F32, name="gla_out_dx")
    g["w_out"] = _mm_tn(sv["o2"], dy, name="mix_out_dw")

    def post_bwd(ov, rv, ctv, gv):
        _, vjp = jax.vjp(functools.partial(_head_rms_gate, heads=GLA_HEADS), ov, rv.astype(F32), gv)
        d_o, d_r, d_g = vjp(ctv.astype(F32))
        return [d_o, d_r], [d_g]

    (d_o, d_r), (g["norm"],) = _tile_call("gla_post_bwd", post_bwd, [o, (proj, d, 2), do2], [w["norm"]],
                                          [(d, F32), (d, BF16)], [w["norm"].shape], 256)
    q, k, v = (proj, dkt, 0), (proj, dkt, 1), (proj, d, 1)
    dq, dkk, dvv, dlg = _gla_bwd(q, k, v, (lg, dkt, 0), sv["states"], d_o, heads=GLA_HEADS, dk=dk, dv=dv,
                                 scale=dk ** -0.5, chunk=GLA_CHUNK, seq=seq, out_dtypes=(BF16, BF16, BF16),
                                 name="gla_core_bwd", side=sv.pop("side", None))

    def gate_bwd(glrv, ctv, wv, bv):
        _, vjp = jax.vjp(_gla_gate, glrv.astype(F32), wv, bv)
        d_glr, d_w, d_b = vjp(ctv)
        return [d_glr], [d_w, d_b]

    (dglr,), (g["w_g2p"], g["b_g2"]) = _tile_call("gla_gate_bwd", gate_bwd, [glr, dlg], [w["w_g2p"], w["b_g2"]],
                                                  [(LANES, BF16)], [w["w_g2p"].shape, w["b_g2"].shape], 512)
    dproj = jnp.concatenate([dq, dkk, dvv, d_r], axis=1)
    g["w_main"] = _mm_tn(h, dproj, name="gla_in_dw")
    g["w_glr"] = _mm_tn(h, dglr, name="gla_in_g_dw")
    dh = _mm(dproj, w["w_main_t"], out_dtype=F32, name="gla_in_dx")
    dh = _mm(dglr, w["w_glr_t"], add=dh, out_dtype=BF16, name="gla_in_g_dx")
    return dh, g


def _cv_layer_fwd(x, h, w, seq, sv):
    d = x.shape[1]
    pre = _mm(h, w["w_in"], out_dtype=BF16, name="cv_in")
    (y1,), _ = _tile_call("cv_glu", lambda a, gt, b: ([_glu(a.astype(F32), gt.astype(F32), b)], []),
                          [(pre, d, 0), (pre, d, 1)], [w["b_in"]], [(d, F32)], [], 512)
    y2 = _dwconv_fwd(y1, w["w_dw"], w["b_dw"], seq, "cv_conv", sv.pop("side", None))
    (y3,), _ = _tile_call("cv_ln", lambda y, a, b: ([_ln_silu(y, a, b)], []), [y2], [w["ln_g"], w["ln_b"]],
                          [(d, BF16)], [], 512)
    sv.update(pre=pre, y1=y1, y2=y2, y3=y3)
    return _mm(y3, w["w_out"], bias=w["b_out"], add=x, name="mix_out_b")


def _cv_layer_bwd(h, dy, w, seq, sv):
    d = dy.shape[1]
    pre = sv["pre"]
    g = {}
    dy3 = _mm(dy, w["w_out_t"], out_dtype=BF16, name="mix_out_dx")
    g["w_out"] = _mm_tn(sv["y3"], dy, name="mix_out_dw")
    g["b_out"] = _colsum(dy, "bias_out_dw")

    def ln_bwd(yv, ctv, av, bv):
        _, vjp = jax.vjp(_ln_silu, yv, av, bv)
        d_y, d_a, d_b = vjp(ctv.astype(F32))
        return [d_y], [d_a, d_b]

    (dy2,), (g["ln_g"], g["ln_b"]) = _tile_call("cv_ln_bwd", ln_bwd, [sv["y2"], dy3], [w["ln_g"], w["ln_b"]],
                                                [(d, F32)], [w["ln_g"].shape, w["ln_b"].shape], 512)
    dy1, g["w_dw"], g["b_dw"] = _dwconv_bwd(sv["y1"], dy2, w["w_dw"], seq, "cv_conv_bwd", sv.pop("side", None))

    def glu_bwd(av, gv, ctv, bv):
        _, vjp = jax.vjp(_glu, av.astype(F32), gv.astype(F32), bv)
        d_a, d_g, d_b = vjp(ctv)
        return [jnp.concatenate([d_a, d_g], axis=1)], [d_b]

    (dpre,), (g["b_in"],) = _tile_call("cv_glu_bwd", glu_bwd, [(pre, d, 0), (pre, d, 1), dy1], [w["b_in"]],
                                       [(2 * d, BF16)], [w["b_in"].shape], 512)
    g["w_in"] = _mm_tn(h, dpre, name="in2_dw")
    dh = _mm(dpre, w["w_in_t"], out_dtype=BF16, name="in2_dx")
    return dh, g


def _sg_layer_fwd(x, h, w, seq, sv):
    d = x.shape[1]
    pre = _mm(h, w["w_in"], out_dtype=BF16, name="sg_in")
    pars = [w["b_in"], w["ln_g"], w["ln_b"], w["w_s"], w["b_st"]]
    (p,), _ = _tile_call("sg_gate", lambda pv, *ps: ([_sgu(pv.astype(F32), *ps)], []), [pre], pars, [(d, BF16)], [],
                         SGU_CHUNK, side=sv.pop("side", None))
    sv.update(pre=pre, p=p)
    return _mm(p, w["w_out"], bias=w["b_out"], add=x, name="mix_out_b")


def _sg_layer_bwd(h, dy, w, seq, sv):
    d = dy.shape[1]
    g = {}
    dp = _mm(dy, w["w_out_t"], out_dtype=BF16, name="mix_out_dx")
    g["w_out"] = _mm_tn(sv["p"], dy, name="mix_out_dw")
    g["b_out"] = _colsum(dy, "bias_out_dw")
    pars = [w["b_in"], w["ln_g"], w["ln_b"], w["w_s"], w["b_st"]]

    def sgu_bwd(pv, ctv, *ps):
        _, vjp = jax.vjp(_sgu, pv.astype(F32), *ps)
        grads = vjp(ctv.astype(F32))
        return [grads[0]], list(grads[1:])

    (dpre,), (g["b_in"], g["ln_g"], g["ln_b"], g["w_s"], g["b_st"]) = _tile_call(
        "sg_gate_bwd", sgu_bwd, [sv["pre"], dp], pars, [(2 * d, BF16)], [p.shape for p in pars], SGU_CHUNK,
        side=sv.pop("side", None))
    g["w_in"] = _mm_tn(h, dpre, name="in2_dw")
    dh = _mm(dpre, w["w_in_t"], out_dtype=BF16, name="in2_dx")
    return dh, g


def _hg_layer_fwd(x, h, w, seq, sv, layer):
    d = x.shape[1]
    heads = d // HGRN_EXPAND
    proj = _mm(h, w["w_in"], out_dtype=BF16, name="hg_in")
    pre = functools.partial(_hgrn_pre, layer=layer)
    (qs, kk, lg), _ = _tile_call("hg_pre", lambda qv, fv, tb: (list(pre(qv.astype(F32), fv.astype(F32), tb)), []),
                                 [(proj, d, 0), (proj, d, 1)], [w["lb_table"]], [(d, BF16), (d, F32), (d, F32)], [], 256)
    o, states = _gla_fwd((qs, d, 0), (kk, d, 0), (proj, d, 2), (lg, d, 0), heads=heads, dk=HGRN_EXPAND,
                         dv=HGRN_EXPAND, scale=1.0, chunk=HGRN_CHUNK, seq=seq, name="hg_core",
                         side=sv.pop("side", None))
    (o2,), _ = _tile_call("hg_post", lambda ov, gv, nv: ([_head_rms_gate(ov, gv.astype(F32), nv, heads)], []),
                          [o, (proj, d, 3)], [w["norm"]], [(d, BF16)], [], 256)
    sv.update(proj=proj, qs=qs, kk=kk, lg=lg, o=o, states=states, o2=o2)
    return _mm(o2, w["w_out"], add=x, name="mix_out")


def _hg_layer_bwd(h, dy, w, seq, sv, layer):
    d = dy.shape[1]
    heads = d // HGRN_EXPAND
    proj = sv["proj"]
    g = {}
    do2 = _mm(dy, w["w_out_t"], out_dtype=BF16, name="mix_out_dx")
    g["w_out"] = _mm_tn(sv["o2"], dy, name="mix_out_dw")

    def post_bwd(ov, gv, ctv, nv):
        _, vjp = jax.vjp(functools.partial(_head_rms_gate, heads=heads), ov, gv.astype(F32), nv)
        d_o, d_g, d_n = vjp(ctv.astype(F32))
        return [d_o, d_g], [d_n]

    (d_o, d_gate), (g["norm"],) = _tile_call("hg_post_bwd", post_bwd, [sv["o"], (proj, d, 3), do2], [w["norm"]],
                                             [(d, F32), (d, BF16)], [w["norm"].shape], 256)
    dqs, dkk, di, dlg = _gla_bwd((sv["qs"], d, 0), (sv["kk"], d, 0), (proj, d, 2), (sv["lg"], d, 0), sv["states"], d_o,
                                 heads=heads, dk=HGRN_EXPAND, dv=HGRN_EXPAND, scale=1.0, chunk=HGRN_CHUNK, seq=seq,
                                 out_dtypes=(F32, F32, BF16), name="hg_core_bwd", side=sv.pop("side", None))

    def pre_bwd(qv, fv, c1, c2, c3, tb):
        _, vjp = jax.vjp(functools.partial(_hgrn_pre, layer=layer), qv.astype(F32), fv.astype(F32), tb)
        d_q, d_f, d_t = vjp((c1, c2, c3))
        return [jnp.concatenate([d_q, d_f], axis=1)], [d_t]

    (dqf,), (g["lb_table"],) = _tile_call("hg_pre_bwd", pre_bwd, [(proj, d, 0), (proj, d, 1), dqs, dkk, dlg],
                                          [w["lb_table"]], [(2 * d, BF16)], [w["lb_table"].shape], 256)
    dproj = jnp.concatenate([dqf, di, d_gate], axis=1)
    g["w_in"] = _mm_tn(h, dproj, name="hg_in_dw")
    dh = _mm(dproj, w["w_in_t"], out_dtype=BF16, name="hg_in_dx")
    return dh, g


_MIXERS = ("gla", "cv", "sg", "hg")


_BIG_KEYS = {"gla": ("w_main", "w_glr", "w_out"), "cv": ("w_in", "w_out"), "sg": ("w_in", "w_out"), "hg": ("w_in", "w_out"),
             "ffn": ("w_up_gate", "w_up_val", "w_down")}


def _local_step(x, target, w, seq, get_big, put_big, ride=lambda kind, layer, forward: None, put_early=None):
    depth = w["norm_mix"].shape[0]
    d = x.shape[1]
    saved, big = [], {}
    for layer in range(depth):
        mixer = _MIXERS[layer % 4]
        sv = {"x_in": x, "side": ride(mixer, layer, True)}
        sv["h"] = _rms_fwd(x, w["norm_mix"][layer:layer + 1], "mix_norm")
        big[mixer, layer] = get_big(mixer, layer)
        wm = dict(w[mixer], **big[mixer, layer])
        if mixer == "gla":
            x = _gla_layer_fwd(x, sv["h"], wm, seq, sv)
        elif mixer == "cv":
            x = _cv_layer_fwd(x, sv["h"], wm, seq, sv)
        elif mixer == "sg":
            x = _sg_layer_fwd(x, sv["h"], wm, seq, sv)
        else:
            x = _hg_layer_fwd(x, sv["h"], wm, seq, sv, layer)
        sv["x_mid"] = x
        big["ffn", layer] = get_big("ffn", layer)
        sv["ffn"] = {"side": ride("ffn", layer, True)}
        wf = dict(w["ffn"][layer], norm=w["norm_ffn"][layer:layer + 1], **big["ffn", layer])
        x = _ffn_fwd(x, wf, seq, sv["ffn"])
        big["ffn", layer].pop("late", None)
        big["ffn", layer].update(sv["ffn"].pop("late_w", {}))
        saved.append(sv)

    def head(xv, tv, gv):
        y, vjp = jax.vjp(_rms, xv, gv)
        err = y - tv
        dx, dg = vjp(err * (1.0 / d))
        part = 0.5 * jnp.sum(jnp.mean(err * err, axis=-1, keepdims=True), axis=0, keepdims=True)
        return [dx], [jnp.broadcast_to(part, (1, LANES)), dg]

    (dx,), (loss, g_final) = _tile_call("loss_head", head, [x, target], [w["norm_final"]], [(d, F32)],
                                        [(1, LANES), (1, d)], 512)
    grads = {"norm_final": g_final, "norm_mix": [None] * depth, "norm_ffn": [None] * depth, "ffn": [None] * depth}
    for layer in reversed(range(depth)):
        mixer = _MIXERS[layer % 4]
        sv = saved[layer]
        wf = dict(w["ffn"][layer], norm=w["norm_ffn"][layer:layer + 1], **big["ffn", layer])
        sv["ffn"]["side"] = ride("ffn", layer, False)
        if put_early is not None:
            sv["ffn"]["put_early"] = functools.partial(put_early, layer)
        dx, gf = _ffn_bwd(sv["x_mid"], dx, wf, seq, sv["ffn"])
        put_big("ffn", layer, {k: gf.pop(k) for k in _BIG_KEYS["ffn"]})
        sv["side"] = ride(mixer, layer, False)
        grads["norm_ffn"][layer] = gf.pop("norm")
        grads["ffn"][layer] = gf
        wm = dict(w[mixer], **big[mixer, layer])
        if mixer == "gla":
            dh, gm = _gla_layer_bwd(sv["h"], dx, wm, seq, sv)
        elif mixer == "cv":
            dh, gm = _cv_layer_bwd(sv["h"], dx, wm, seq, sv)
        elif mixer == "sg":
            dh, gm = _sg_layer_bwd(sv["h"], dx, wm, seq, sv)
        else:
            dh, gm = _hg_layer_bwd(sv["h"], dx, wm, seq, sv, layer)
        put_big(mixer, layer, {k: gm.pop(k) for k in _BIG_KEYS[mixer]})
        grads[mixer] = gm
        last = ride(mixer, layer, False) if layer == 0 else None
        dx, grads["norm_mix"][layer] = _rms_bwd(sv["x_in"], w["norm_mix"][layer:layer + 1], dh, dx, "mix_norm_bwd", last)
    return loss, dx, grads


def _prep_small(p):
    row = lambda a: a.reshape(1, -1).astype(F32)
    w = {"norm_mix": p["norm_mix"].astype(F32), "norm_ffn": p["norm_ffn"].astype(F32), "norm_final": row(p["norm_final"])}
    w["gla"] = dict(w_g2p=jnp.pad(p["gla_w_g2"][0].astype(F32), ((0, LANES - GLA_RANK), (0, 0))), b_g2=row(p["gla_b_g2"]),
                    norm=row(p["gla_norm"]))
    w["cv"] = dict(b_in=row(p["cv_b_in"]), w_dw=p["cv_w_dw"][0].astype(F32), b_dw=row(p["cv_b_dw"]), ln_g=row(p["cv_ln_g"]),
                   ln_b=row(p["cv_ln_b"]), b_out=row(p["cv_b_out"]))
    b_st = jnp.pad(p["sg_b_s"][0].astype(F32).T, ((0, 0), (0, LANES - SGU_GROUPS)))
    w["sg"] = dict(b_in=row(p["sg_b_in"]), ln_g=row(p["sg_ln_g"]), ln_b=row(p["sg_ln_b"]), w_s=p["sg_w_s"][0].astype(F32),
                   b_st=b_st, b_out=row(p["sg_b_out"]))
    w["hg"] = dict(lb_table=p["hg_lb_table"].astype(F32), norm=row(p["hg_norm"]))
    w["ffn"] = [dict(w_dw=p["ffn_w_dw"][layer].astype(F32)) for layer in range(p["ffn_w_dw"].shape[0])]
    return w


def _small_grads(g):
    gla, cv, sg, hg = g["gla"], g["cv"], g["sg"], g["hg"]
    return {
        "norm_mix": jnp.concatenate(g["norm_mix"], axis=0), "norm_ffn": jnp.concatenate(g["norm_ffn"], axis=0),
        "norm_final": g["norm_final"][0],
        "gla_w_g2": gla["w_g2p"][:GLA_RANK][None], "gla_b_g2": gla["b_g2"], "gla_norm": gla["norm"],
        "cv_b_in": cv["b_in"], "cv_w_dw": cv["w_dw"][None], "cv_b_dw": cv["b_dw"], "cv_ln_g": cv["ln_g"],
        "cv_ln_b": cv["ln_b"], "cv_b_out": cv["b_out"],
        "sg_b_in": sg["b_in"], "sg_ln_g": sg["ln_g"], "sg_ln_b": sg["ln_b"], "sg_w_s": sg["w_s"][None],
        "sg_b_s": sg["b_st"][:, :SGU_GROUPS].T[None], "sg_b_out": sg["b_out"],
        "hg_lb_table": hg["lb_table"], "hg_norm": hg["norm"],
        "ffn_w_dw": jnp.stack([f["w_dw"] for f in g["ffn"]]),
    }


def _oriented(kind, mats):
    if kind == "ffn":
        (up, up_t), (down, down_t) = mats["w_up"], mats["w_down"]
        f = down.shape[0]
        return dict(w_up=up, w_up_t_gate=up_t[:f], w_up_t_val=up_t[f:], w_down=down, w_down_t=down_t)
    (w_in, w_in_t), (w_out, w_out_t) = mats["w_in"], mats["w_out"]
    if kind != "gla":
        return dict(w_in=w_in, w_in_t=w_in_t, w_out=w_out, w_out_t=w_out_t)
    n_main = w_in.shape[1] - GLA_RANK
    return dict(w_main=w_in[:, :n_main], w_glr=jnp.pad(w_in[:, n_main:], ((0, 0), (0, LANES - GLA_RANK))),
                w_main_t=w_in_t[:n_main], w_glr_t=jnp.pad(w_in_t[n_main:], ((0, LANES - GLA_RANK), (0, 0))),
                w_out=w_out, w_out_t=w_out_t)


def _all_gather(x, *, name):
    m_per, n = x.shape

    def body(x_ref, out_ref, send_sems, recv_sems, local_sem):
        mx, my, mc = lax.axis_index("x"), lax.axis_index("y"), lax.axis_index("c")
        me, sibling = (mx, my, mc), (mx, my, 1 - mc)
        chips = [(1 - mx, my), (mx, 1 - my), (1 - mx, 1 - my)]

        def rows(px, py, pc):
            return out_ref.at[pl.ds((4 * px + 2 * py + pc) * m_per, m_per), :]

        def copy(k, block, to, src=None):
            return pltpu.make_async_remote_copy(
                src_ref=rows(*block) if src is None else src, dst_ref=rows(*block), send_sem=send_sems.at[k],
                recv_sem=recv_sems.at[k], device_id=to, device_id_type=MESH)

        mine = pltpu.make_async_copy(x_ref, rows(*me), local_sem)
        mine.start()
        first = [copy(0, me, sibling, src=x_ref)]
        first += [copy(1 + j, me, (*chip, mc), src=x_ref) for j, chip in enumerate(chips)]
        for cp in first:
            cp.start()
        passed = [copy(4 + j, (*chip, mc), sibling) for j, chip in enumerate(chips)]
        for j, chip in enumerate(chips):
            copy(1 + j, (*chip, mc), me).wait_recv()
            passed[j].start()
        copy(0, sibling, me).wait_recv()
        for j, chip in enumerate(chips):
            copy(4 + j, (*chip, 1 - mc), me).wait_recv()
        for cp in first + passed:
            cp.wait_send()
        mine.wait()

    return pl.pallas_call(
        body, name=name, out_shape=jax.ShapeDtypeStruct((N_DEV * m_per, n), x.dtype),
        in_specs=[pl.BlockSpec(memory_space=pltpu.VMEM)], out_specs=pl.BlockSpec(memory_space=pltpu.VMEM),
        scratch_shapes=[pltpu.SemaphoreType.DMA((7,)), pltpu.SemaphoreType.DMA((7,)), pltpu.SemaphoreType.DMA],
    )(x)


def _my_index():
    return 4 * lax.axis_index("x") + 2 * lax.axis_index("y") + lax.axis_index("c")


def _gather_stage(srcs, *, name):
    n = len(srcs)

    def body(*refs):
        x_refs, out_refs = refs[:n], refs[n:2 * n]
        send_sems, recv_sems, local_sems = refs[2 * n:]
        mx, my, mc = lax.axis_index("x"), lax.axis_index("y"), lax.axis_index("c")
        me, sibling = (mx, my, mc), (mx, my, 1 - mc)
        chips = [(1 - mx, my), (mx, 1 - my), (1 - mx, 1 - my)]

        def slot(i, px, py, pc):
            return out_refs[i].at[4 * px + 2 * py + pc]

        def copy(i, k, block, to, src=None):
            return pltpu.make_async_remote_copy(
                src_ref=slot(i, *block) if src is None else src, dst_ref=slot(i, *block), send_sem=send_sems.at[7 * i + k],
                recv_sem=recv_sems.at[7 * i + k], device_id=to, device_id_type=MESH)

        mine = [pltpu.make_async_copy(x_refs[i], slot(i, *me), local_sems.at[i]) for i in range(n)]
        first = [copy(i, 0, me, sibling, src=x_refs[i]) for i in range(n)]
        first += [copy(i, 1 + j, me, (*chip, mc), src=x_refs[i]) for j, chip in enumerate(chips) for i in range(n)]
        for cp in mine + first:
            cp.start()
        passed = []
        for j, chip in enumerate(chips):
            for i in range(n):
                copy(i, 1 + j, (*chip, mc), me).wait_recv()
                passed.append(copy(i, 4 + j, (*chip, mc), sibling))
                passed[-1].start()
        for i in range(n):
            copy(i, 0, sibling, me).wait_recv()
            for j, chip in enumerate(chips):
                copy(i, 4 + j, (*chip, 1 - mc), me).wait_recv()
        for cp in first + passed:
            cp.wait_send()
        for cp in mine:
            cp.wait()

    any_space = pl.BlockSpec(memory_space=pl.ANY)
    return pl.pallas_call(
        body, name=name, out_shape=[jax.ShapeDtypeStruct((N_DEV,) + s.shape, s.dtype) for s in srcs],
        in_specs=[any_space] * n, out_specs=[any_space] * n,
        scratch_shapes=[pltpu.SemaphoreType.DMA((7 * n,)), pltpu.SemaphoreType.DMA((7 * n,)), pltpu.SemaphoreType.DMA((n,))],
    )(*srcs)


def _adamw_math(g, w, m, v):
    c1, c2 = 1.0 - ADAM_B1 ** ADAM_STEP, 1.0 - ADAM_B2 ** ADAM_STEP
    m_new = ADAM_B1 * m + (1.0 - ADAM_B1) * g
    v_new = ADAM_B2 * v + (1.0 - ADAM_B2) * (g * g)
    delta = -ADAM_LR * ((m_new / c1) / (jnp.sqrt(v_new / c2) + ADAM_EPS) + ADAM_WD * w)
    return delta, m_new, v_new


def _adamw_big(slots, w, m, v, layer, *, name):
    _, r, c = slots.shape
    tr = _divisor_tile(r, max(8, (200 * 1024) // c // 8 * 8), 8)

    def body(s_ref, w_ref, m_ref, v_ref, g_out, d_out, m_out, v_out):
        g = s_ref[0].astype(F32)
        for p in range(1, N_DEV):
            g = g + s_ref[p].astype(F32)
        g_out[...] = g
        d_out[...], m_out[...], v_out[...] = _adamw_math(g, w_ref[...], m_ref[...], v_ref[...])

    blk = pl.BlockSpec((tr, c), lambda i: (i, 0))
    lay = pl.BlockSpec((None, tr, c), lambda i: (layer, i, 0))
    return pl.pallas_call(
        body, name=name, grid=(r // tr,), in_specs=[pl.BlockSpec((N_DEV, tr, c), lambda i: (0, i, 0)), lay, lay, lay],
        out_specs=[blk] * 4, out_shape=[jax.ShapeDtypeStruct((r, c), F32)] * 4,
        compiler_params=pltpu.CompilerParams(dimension_semantics=("parallel",)),
    )(slots, w, m, v)


def _sum_small(got, r_re, r_sh, *, name):
    per_dev = r_re + N_DEV * r_sh

    def body(got_ref, re_ref, sh_ref):
        mine = r_re + _my_index() * r_sh
        acc_re = got_ref[0:r_re, :]
        acc_sh = got_ref[pl.ds(pl.multiple_of(mine, 8), r_sh), :]
        for p in range(1, N_DEV):
            acc_re = acc_re + got_ref[p * per_dev:p * per_dev + r_re, :]
            acc_sh = acc_sh + got_ref[pl.ds(pl.multiple_of(p * per_dev + mine, 8), r_sh), :]
        re_ref[...] = acc_re
        sh_ref[...] = acc_sh

    return pl.pallas_call(body, name=name, out_shape=[jax.ShapeDtypeStruct((r_re, LANES), F32),
                                                       jax.ShapeDtypeStruct((r_sh, LANES), F32)])(got)


def _adamw_small(gs, ws, ms, vs, *, name):
    n = len(gs)

    def body(*refs):
        ins, outs = refs[:4 * n], refs[4 * n:]
        for i in range(n):
            res = _adamw_math(ins[i][...], ins[n + i][...], ins[2 * n + i][...], ins[3 * n + i][...])
            for j in range(3):
                outs[j * n + i][...] = res[j]

    out = pl.pallas_call(body, name=name, out_shape=[jax.ShapeDtypeStruct(a.shape, F32) for a in ws] * 3)(*gs, *ws, *ms, *vs)
    return out[:n], out[n:2 * n], out[2 * n:]


def _layout(shapes, row_align, total_align):
    lay, off = {}, 0
    for name, shape in shapes.items():
        size = int(np.prod(shape))
        rows = -(-size // LANES)
        rows = -(-rows // row_align) * row_align
        lay[name] = (off, rows, size, tuple(shape))
        off += rows
    return lay, -(-off // total_align) * total_align


def _pack(arrs, lay, total, dtype, lead=()):
    parts = []
    nl = len(lead)
    for name, (off, rows, size, shape) in lay.items():
        flat = arrs[name].astype(dtype).reshape(*lead, size)
        parts.append(jnp.pad(flat, [(0, 0)] * nl + [(0, rows * LANES - size)]).reshape(*lead, rows, LANES))
    used = sum(v[1] for v in lay.values())
    if total > used:
        parts.append(jnp.zeros((*lead, total - used, LANES), dtype))
    return jnp.concatenate(parts, axis=nl)


def _unpack(buf, lay, lead=()):
    out = {}
    nl = len(lead)
    for name, (off, rows, size, shape) in lay.items():
        part = lax.slice_in_dim(buf, off, off + rows, axis=nl).reshape(*lead, rows * LANES)
        out[name] = lax.slice_in_dim(part, 0, size, axis=nl).reshape(*lead, *shape)
    return out


_SHARD_AXIS = {
    "norm_mix": None, "norm_ffn": None, "norm_final": None, "gla_w_in": 2, "gla_w_g2": 2, "gla_b_g2": None,
    "gla_norm": None, "gla_w_out": 1, "cv_w_in": 2, "cv_b_in": 1, "cv_w_dw": 2, "cv_b_dw": 1, "cv_ln_g": 1,
    "cv_ln_b": 1, "cv_w_out": 1, "cv_b_out": 1, "sg_w_in": 2, "sg_b_in": 1, "sg_ln_g": 1, "sg_ln_b": 1, "sg_w_s": None,
    "sg_b_s": None, "sg_w_out": 1, "sg_b_out": 1, "hg_w_in": 2, "hg_lb_table": None, "hg_norm": None, "hg_w_out": 1,
    "ffn_w_up": 2, "ffn_w_dw": 2, "ffn_w_down": 1,
}
_MATMUL_WEIGHTS = ("gla_w_in", "gla_w_out", "cv_w_in", "cv_w_out", "sg_w_in", "sg_w_out", "hg_w_in", "hg_w_out",
                   "ffn_w_up", "ffn_w_down")
_NAMES = tuple(_SHARD_AXIS)


def kernel(x, norm_mix, norm_ffn, norm_final, gla_w_in, gla_w_g2, gla_b_g2, gla_norm, gla_w_out, cv_w_in, cv_b_in, cv_w_dw, cv_b_dw, cv_ln_g, cv_ln_b, cv_w_out, cv_b_out, sg_w_in, sg_b_in, sg_ln_g, sg_ln_b, sg_w_s, sg_b_s, sg_w_out, sg_b_out, hg_w_in, hg_lb_table, hg_norm, hg_w_out, ffn_w_up, ffn_w_dw, ffn_w_down, loss_target, m_norm_mix, m_norm_ffn, m_norm_final, m_gla_w_in, m_gla_w_g2, m_gla_b_g2, m_gla_norm, m_gla_w_out, m_cv_w_in, m_cv_b_in, m_cv_w_dw, m_cv_b_dw, m_cv_ln_g, m_cv_ln_b, m_cv_w_out, m_cv_b_out, m_sg_w_in, m_sg_b_in, m_sg_ln_g, m_sg_ln_b, m_sg_w_s, m_sg_b_s, m_sg_w_out, m_sg_b_out, m_hg_w_in, m_hg_lb_table, m_hg_norm, m_hg_w_out, m_ffn_w_up, m_ffn_w_dw, m_ffn_w_down, v_norm_mix, v_norm_ffn, v_norm_final, v_gla_w_in, v_gla_w_g2, v_gla_b_g2, v_gla_norm, v_gla_w_out, v_cv_w_in, v_cv_b_in, v_cv_w_dw, v_cv_b_dw, v_cv_ln_g, v_cv_ln_b, v_cv_w_out, v_cv_b_out, v_sg_w_in, v_sg_b_in, v_sg_ln_g, v_sg_ln_b, v_sg_w_s, v_sg_b_s, v_sg_w_out, v_sg_b_out, v_hg_w_in, v_hg_lb_table, v_hg_norm, v_hg_w_out, v_ffn_w_up, v_ffn_w_dw, v_ffn_w_down):
    local = dict(locals())
    wts = {n: local[n] for n in _NAMES}
    mom = {n: local["m_" + n] for n in _NAMES}
    var = {n: local["v_" + n] for n in _NAMES}
    small_all = [n for n in _NAMES if n not in _MATMUL_WEIGHTS]
    small_sharded = [n for n in small_all if _SHARD_AXIS[n] is not None]
    bsz, seq, d = x.shape
    depth = norm_mix.shape[0]

    stages = {}
    for layer in range(depth):
        kind = _MIXERS[layer % 4]
        stages[kind, layer] = {"w_in": (kind + "_w_in", layer // 4), "w_out": (kind + "_w_out", layer // 4)}
        stages["ffn", layer] = {"w_up": ("ffn_w_up", layer), "w_down": ("ffn_w_down", layer)}

    order = list(stages)
    shards = lambda stage: [wts[nm][idx].astype(BF16) for nm, idx in stages[stage].values()]
    gathers = {order[0]: _Side(shards(order[0]), False)}
    gathers[order[0]].lands = _gather_stage(gathers[order[0]].srcs, name="gather_first")
    scatters, waiting, down_gathers, down_scatters = {}, [], {}, {}

    def ride(kind, layer, forward):
        if not forward:
            return waiting.pop() if waiting else None
        at = order.index((kind, layer)) + 1
        if at == len(order):
            return None
        srcs = shards(order[at])
        if order[at][0] == "ffn":
            down_gathers[order[at][1]] = _Side(srcs[1:], False)
            srcs = srcs[:1]
        gathers[order[at]] = _Side(srcs, False)
        return gathers[order[at]]

    lay_sw, r_sw = _layout({n: wts[n].shape for n in small_sharded}, 8, 8)
    got_sw = _all_gather(_pack(wts, lay_sw, r_sw, F32), name="gather_small_weights")
    parts = _unpack(got_sw.reshape(N_DEV, r_sw, LANES), lay_sw, (N_DEV,))
    full_small = {n: wts[n] for n in small_all if _SHARD_AXIS[n] is None}
    for n in small_sharded:
        ax, shape = _SHARD_AXIS[n], wts[n].shape
        full_small[n] = jnp.moveaxis(parts[n], 0, ax).reshape(shape[:ax] + (N_DEV * shape[ax],) + shape[ax + 1:])

    def full_size(nm, land):
        _, r, c = land.shape
        if _SHARD_AXIS[nm] == 2:
            return land.transpose(1, 0, 2).reshape(r, N_DEV * c), land.transpose(0, 2, 1).reshape(N_DEV * c, r)
        return land.reshape(N_DEV * r, c), land.reshape(N_DEV * r, c).T

    def get_big(kind, layer):
        names = [nm for nm, _ in stages[kind, layer].values()]
        lands = gathers[kind, layer].lands
        if kind != "ffn" or layer not in down_gathers:
            return _oriented(kind, {key: full_size(nm, land) for key, nm, land in zip(stages[kind, layer], names, lands)})
        up, up_t = full_size(names[0], lands[0])
        f = up.shape[1] // 2

        def down_landed():
            down, down_t = full_size(names[1], down_gathers[layer].lands[0])
            return dict(w_down=down, w_down_t=down_t)

        return dict(w_up=up, w_up_t_gate=up_t[:f], w_up_t_val=up_t[f:], late=(down_gathers[layer], down_landed))

    def put_big(kind, layer, g):
        if kind == "ffn":
            k, f = g["w_up_gate"].shape
            halves = [g[key].reshape(k, N_DEV // 2, 2 * f // N_DEV) for key in ("w_up_gate", "w_up_val")]
            w_in = jnp.concatenate(halves, axis=1)
        else:
            w_in = jnp.concatenate([g["w_main"], g["w_glr"][:, :GLA_RANK]], axis=1) if kind == "gla" else g["w_in"]
            w_in = w_in.reshape(w_in.shape[0], N_DEV, w_in.shape[1] // N_DEV)
        sends = [w_in.transpose(1, 0, 2).astype(BF16)]
        if kind != "ffn":
            sends.append(row_slots(g["w_out"]))
        scatters[kind, layer] = _Side(sends, True)
        waiting.append(scatters[kind, layer])

    def row_slots(grad):
        return grad.reshape(N_DEV, grad.shape[0] // N_DEV, grad.shape[1]).astype(BF16)

    def put_early(layer, grad_w_down):
        down_scatters[layer] = _Side([row_slots(grad_w_down)], True)
        return down_scatters[layer]

    loss, dx, grads = _local_step(x.reshape(bsz * seq, d), loss_target.reshape(bsz * seq, d), _prep_small(full_small), seq,
                                  get_big, put_big, ride, put_early)
    loss = lax.psum(loss[0, 0], ("x", "y", "c"))

    gs = _small_grads(grads)
    small_repl = [n for n in small_all if _SHARD_AXIS[n] is None]
    lay_re, r_re = _layout({n: wts[n].shape for n in small_repl}, 8, 8)
    slots = {}
    for n in small_sharded:
        ax, shape = _SHARD_AXIS[n], wts[n].shape
        slots[n] = jnp.moveaxis(gs[n].reshape(shape[:ax] + (N_DEV, shape[ax]) + shape[ax + 1:]), ax, 0)
    sent = jnp.concatenate([_pack(gs, lay_re, r_re, F32), _pack(slots, lay_sw, r_sw, F32, (N_DEV,)).reshape(-1, LANES)])
    sum_re, sum_sh = _sum_small(_all_gather(sent, name="gather_small_grads"), r_re, r_sw, name="sum_small_grads")
    g_own = _unpack(sum_re, lay_re)
    g_own.update(_unpack(sum_sh, lay_sw))
    two_d = lambda a: a.reshape(-1, a.shape[-1])
    upd = _adamw_small(*[[two_d(src[n]) for n in small_all] for src in (g_own, wts, mom, var)], name="adamw_small")
    results = {n: [g_own[n]] + [part[i].reshape(wts[n].shape) for part in upd] for i, n in enumerate(small_all)}

    per_layer = {}
    for (kind, layer), side in scatters.items():
        lands = side.lands
        if kind == "ffn":
            lands = list(lands) + down_scatters[layer].lands
        for (nm, idx), land in zip(stages[kind, layer].values(), lands):
            three_d = lambda a: a.reshape((a.shape[0],) + land.shape[1:])
            per_layer.setdefault(nm, {})[idx] = _adamw_big(land, three_d(wts[nm]), three_d(mom[nm]), three_d(var[nm]), idx,
                                                           name="adamw_" + nm)
    for nm, by_idx in per_layer.items():
        outs = [by_idx[i] for i in range(len(by_idx))]
        results[nm] = [(outs[0][j] if len(outs) == 1 else jnp.stack([o[j] for o in outs])).reshape(wts[nm].shape)
                       for j in range(4)]
    out = [loss, dx.reshape(bsz, seq, d)]
    for j in range(4):
        out += [results[n][j] for n in _NAMES]
    return tuple(out)
```

```python
import functools
import math

import jax
import jax.numpy as jnp
import numpy as np
from jax import lax
from jax.experimental import pallas as pl
from jax.experimental.pallas import tpu as pltpu

F32 = jnp.float32
BF16 = jnp.bfloat16
EPS = 1e-6
N_DEV = 8
LANES = 128
SUBLANES_BF16 = 16
HALO = 32
GLA_HEADS, GLA_RANK, GLA_GATE_NORM, GLA_CHUNK = 4, 16, 16.0, 64
SGU_CHUNK, SGU_GROUPS = 128, 8
HGRN_EXPAND, HGRN_CHUNK = 128, 64
FFN_CONV_WIDTH = 3
ADAM_LR, ADAM_B1, ADAM_B2, ADAM_EPS, ADAM_WD, ADAM_STEP = 0.001, 0.9, 0.999, 1e-08, 0.01, 10
MESH = pl.DeviceIdType.MESH


def _sigmoid(x):
    return 0.5 * (jnp.tanh(0.5 * x) + 1.0)


def _silu(x):
    return x * _sigmoid(x)


def _log_sigmoid(x):
    return jnp.minimum(x, 0.0) - jnp.log(1.0 + jnp.exp(-jnp.abs(x)))


def _gelu(x):
    return 0.5 * x * (1.0 + jnp.tanh(math.sqrt(2.0 / math.pi) * (x + 0.044715 * (x * x * x))))


def _rms(x, g):
    return x * lax.rsqrt(jnp.mean(x * x, axis=-1, keepdims=True) + EPS) * g


def _layer_norm(x, g, b):
    xc = x - jnp.mean(x, axis=-1, keepdims=True)
    return xc * lax.rsqrt(jnp.mean(xc * xc, axis=-1, keepdims=True) + EPS) * g + b


def _dot_raw(a, b, dims):
    return lax.dot_general(a.astype(BF16), b.astype(BF16), (dims, ((), ())), preferred_element_type=F32)


@jax.custom_vjp
def _bdot(a, b):
    return _dot_raw(a, b, ((1,), (0,)))


@jax.custom_vjp
def _bdot_nt(a, b):
    return _dot_raw(a, b, ((1,), (1,)))


@jax.custom_vjp
def _bdot_tn(a, b):
    return _dot_raw(a, b, ((0,), (0,)))


_bdot.defvjp(lambda a, b: (_bdot(a, b), (a, b)), lambda r, g: (_bdot_nt(g, r[1]), _bdot_tn(r[0], g)))
_bdot_nt.defvjp(lambda a, b: (_bdot_nt(a, b), (a, b)), lambda r, g: (_bdot(g, r[1]), _bdot_tn(g, r[0])))
_bdot_tn.defvjp(lambda a, b: (_bdot_tn(a, b), (a, b)), lambda r, g: (_bdot_nt(r[1], g), _bdot(r[0], g)))


def _scan_rows(x, reverse):
    n = x.shape[0]
    row = lax.broadcasted_iota(jnp.int32, x.shape, 0)
    step = 1
    while step < n:
        if reverse:
            x = x + jnp.where(row < n - step, pltpu.roll(x, n - step, 0), 0.0)
        else:
            x = x + jnp.where(row >= step, pltpu.roll(x, step, 0), 0.0)
        step *= 2
    return x


@jax.custom_vjp
def _cumsum_rows(x):
    return _scan_rows(x, False)


_cumsum_rows.defvjp(lambda x: (_scan_rows(x, False), None), lambda _, g: (_scan_rows(g, True),))


def _divisor_tile(n, cap, unit):
    if n <= cap:
        return n
    best = None
    for t in range(unit, cap + 1, unit):
        if n % t == 0:
            best = t
    assert best is not None, (n, cap, unit)
    return best


def _const_map(nd):
    return lambda *_: (0,) * nd


class _Side:
    def __init__(self, srcs, scatter, parts=()):
        self.srcs, self.scatter, self.lands, self.parts = list(srcs), scatter, None, list(parts)

    @staticmethod
    def join(sides):
        sides = [s for s in sides if s is not None]
        if len(sides) < 2:
            return sides[0] if sides else None
        assert len({s.scatter for s in sides}) == 1
        return _Side([a for s in sides for a in s.srcs], sides[0].scatter, sides)

    def landed(self, lands):
        self.lands = list(lands)
        at = 0
        for part in self.parts:
            part.landed(self.lands[at:at + len(part.srcs)])
            at += len(part.srcs)


def _pallas(body, side, *, name, grid, in_specs, out_specs, out_shape, scratch_shapes=(), semantics):
    if side is None:
        return pl.pallas_call(body, name=name, grid=grid, in_specs=in_specs, out_specs=out_specs, out_shape=out_shape,
                              scratch_shapes=list(scratch_shapes),
                              compiler_params=pltpu.CompilerParams(dimension_semantics=semantics))
    single = not isinstance(out_shape, (list, tuple))
    out_specs, out_shape = ([out_specs], [out_shape]) if single else (list(out_specs), list(out_shape))
    n, n_in, n_out, n_scr = len(side.srcs), len(in_specs), len(out_shape), len(scratch_shapes)
    lands = [jax.ShapeDtypeStruct((N_DEV,) + (s.shape[1:] if side.scatter else s.shape), s.dtype) for s in side.srcs]

    def body2(*refs):
        x_refs, land_refs = refs[n_in:n_in + n], refs[n_in + n + n_out:n_in + 2 * n + n_out]
        send_sems, recv_sems, local_sems = refs[-3:]
        steps = [pl.program_id(a) for a in range(len(grid))]
        first = functools.reduce(jnp.logical_and, [s == 0 for s in steps])
        last = functools.reduce(jnp.logical_and, [s == g - 1 for s, g in zip(steps, grid)])

        def copies():
            mx, my, mc = lax.axis_index("x"), lax.axis_index("y"), lax.axis_index("c")
            me = 4 * mx + 2 * my + mc
            mine = [pltpu.make_async_copy(x_refs[i].at[me] if side.scatter else x_refs[i], land_refs[i].at[me],
                                          local_sems.at[i]) for i in range(n)]
            sends, recvs = [], []
            for k in range(1, N_DEV):
                px = 1 - mx if k & 4 else mx
                py = 1 - my if k & 2 else my
                pc = 1 - mc if k & 1 else mc
                peer = 4 * px + 2 * py + pc
                for i in range(n):
                    sems = dict(send_sem=send_sems.at[7 * i + k - 1], recv_sem=recv_sems.at[7 * i + k - 1],
                                device_id=(px, py, pc), device_id_type=MESH)
                    src = x_refs[i].at[peer] if side.scatter else x_refs[i]
                    sends.append(pltpu.make_async_remote_copy(src_ref=src, dst_ref=land_refs[i].at[me], **sems))
                    recvs.append(pltpu.make_async_remote_copy(src_ref=src, dst_ref=land_refs[i].at[peer], **sems))
            return mine, sends, recvs

        @pl.when(first)
        def _():
            mine, sends, _ = copies()
            for cp in mine + sends:
                cp.start()

        body(*refs[:n_in], *refs[n_in + n:n_in + n + n_out], *refs[n_in + 2 * n + n_out:n_in + 2 * n + n_out + n_scr])

        @pl.when(last)
        def _():
            mine, sends, recvs = copies()
            for cp in recvs:
                cp.wait_recv()
            for cp in sends:
                cp.wait_send()
            for cp in mine:
                cp.wait()

    any_space = pl.BlockSpec(memory_space=pl.ANY)
    call = pl.pallas_call(
        body2, name=name, grid=grid, in_specs=list(in_specs) + [any_space] * n, out_specs=out_specs + [any_space] * n,
        out_shape=out_shape + lands,
        scratch_shapes=list(scratch_shapes) + [pltpu.SemaphoreType.DMA((7 * n,)), pltpu.SemaphoreType.DMA((7 * n,)),
                                               pltpu.SemaphoreType.DMA((n,))],
        compiler_params=pltpu.CompilerParams(dimension_semantics=("arbitrary",) * len(grid)))

    def run(*args):
        res = call(*args, *side.srcs)
        side.landed(res[n_out:])
        return res[0] if single else res[:n_out]

    return run


def _mm(a, b, *, add=None, bias=None, out_dtype=F32, name, side=None):
    m, k = a.shape
    k2, n = b.shape
    assert k == k2
    tn = _divisor_tile(n, max(LANES, min(1408, (6 << 20) // (2 * k) // LANES * LANES)), LANES)
    tm = _divisor_tile(m, max(256, min(1024, (4 << 20) // (a.dtype.itemsize * k) // 256 * 256)), 8)
    has_bias, has_add = bias is not None, add is not None

    def body(*refs):
        a_ref, b_ref = refs[0], refs[1]
        o_ref = refs[-1]
        acc = jnp.dot(a_ref[...].astype(BF16), b_ref[...], preferred_element_type=F32)
        pos = 2
        if has_bias:
            acc = acc + refs[pos][...]
            pos += 1
        if has_add:
            acc = acc + refs[pos][...].astype(F32)
        o_ref[...] = acc.astype(o_ref.dtype)

    in_specs = [pl.BlockSpec((tm, k), lambda i, j: (i, 0)), pl.BlockSpec((k, tn), lambda i, j: (0, j))]
    args = [a, b]
    if has_bias:
        in_specs.append(pl.BlockSpec((1, tn), lambda i, j: (0, j)))
        args.append(bias)
    if has_add:
        in_specs.append(pl.BlockSpec((tm, tn), lambda i, j: (i, j)))
        args.append(add)
    return _pallas(
        body, side, name=name, grid=(m // tm, n // tn), in_specs=in_specs,
        out_specs=pl.BlockSpec((tm, tn), lambda i, j: (i, j)),
        out_shape=jax.ShapeDtypeStruct((m, n), out_dtype),
        semantics=("parallel", "parallel"),
    )(*args)


def _mm_tn(a, g, *, name):
    m, k = a.shape
    m2, n = g.shape
    assert m == m2
    tk = _divisor_tile(k, 1408, LANES)
    tn = _divisor_tile(n, 1408, LANES)
    tm = _divisor_tile(m, 1024, 8)

    def body(a_ref, g_ref, o_ref):
        @pl.when(pl.program_id(2) == 0)
        def _():
            o_ref[...] = jnp.zeros_like(o_ref)

        o_ref[...] += _dot_raw(a_ref[...], g_ref[...], ((0,), (0,)))

    return pl.pallas_call(
        body, name=name, grid=(k // tk, n // tn, m // tm),
        in_specs=[pl.BlockSpec((tm, tk), lambda i, j, t: (t, i)), pl.BlockSpec((tm, tn), lambda i, j, t: (t, j))],
        out_specs=pl.BlockSpec((tk, tn), lambda i, j, t: (i, j)),
        out_shape=jax.ShapeDtypeStruct((k, n), F32),
        compiler_params=pltpu.CompilerParams(dimension_semantics=("parallel", "parallel", "arbitrary")),
    )(a, g)


def _tile_call(name, fn, tiled, params, out_tiled, out_acc, tile, side=None):
    tiled = [t if isinstance(t, tuple) else (t, t.shape[1], 0) for t in tiled]
    t_rows = tiled[0][0].shape[0]
    tile = min(tile, t_rows)
    assert t_rows % tile == 0
    n_t, n_p, n_o = len(tiled), len(params), len(out_tiled)

    def body(*refs):
        vals = [r[...] for r in refs[: n_t + n_p]]
        touts, aouts = fn(*vals)
        for r, v in zip(refs[n_t + n_p: n_t + n_p + n_o], touts):
            r[...] = v.astype(r.dtype)
        acc_refs = refs[n_t + n_p + n_o:]
        if acc_refs:
            @pl.when(pl.program_id(0) == 0)
            def _():
                for r in acc_refs:
                    r[...] = jnp.zeros_like(r)

            for r, v in zip(acc_refs, aouts):
                r[...] += v

    in_specs = [pl.BlockSpec((tile, w), lambda i, cb=cb: (i, cb)) for _, w, cb in tiled]
    in_specs += [pl.BlockSpec(p.shape, _const_map(p.ndim)) for p in params]
    out_specs = [pl.BlockSpec((tile, w), lambda i: (i, 0)) for w, _ in out_tiled]
    out_specs += [pl.BlockSpec(s, _const_map(len(s))) for s in out_acc]
    out_shape = [jax.ShapeDtypeStruct((t_rows, w), dt) for w, dt in out_tiled]
    out_shape += [jax.ShapeDtypeStruct(s, F32) for s in out_acc]
    res = _pallas(
        body, side, name=name, grid=(t_rows // tile,), in_specs=in_specs, out_specs=out_specs, out_shape=out_shape,
        semantics=("arbitrary" if out_acc else "parallel",),
    )(*[t[0] for t in tiled], *params)
    return res[:n_o], res[n_o:]


def _rms_fwd(x, g, name):
    (h,), _ = _tile_call(name, lambda xv, gv: ([_rms(xv, gv)], []), [x], [g], [(x.shape[1], BF16)], [], 512)
    return h


def _rms_bwd(x, g, dh, dres, name, side=None):
    def fn(xv, dhv, drv, gv):
        _, vjp = jax.vjp(_rms, xv, gv)
        dx, dg = vjp(dhv.astype(F32))
        return [drv + dx], [dg]

    (dx,), (dg,) = _tile_call(name, fn, [x, dh, dres], [g], [(x.shape[1], F32)], [g.shape], 512, side)
    return dx, dg


def _colsum(x, name):
    _, (s,) = _tile_call(name, lambda xv: ([], [jnp.sum(xv.astype(F32), axis=0, keepdims=True)]), [x], [], [],
                         [(1, x.shape[1])], 512)
    return s


def _seq_flags(i, tiles_per_seq):
    pos = i % tiles_per_seq
    return pos == 0, pos == tiles_per_seq - 1


def _dwconv_fwd(x, w, b, seq, name, side=None):
    t_rows, ch = x.shape
    kw = w.shape[0]
    tile = min(512, seq)
    cb = _divisor_tile(ch, 256, LANES)
    tps, hb = seq // tile, tile // HALO

    def body(x_ref, halo_ref, w_ref, b_ref, y_ref, pad_ref):
        first, _ = _seq_flags(pl.program_id(0), tps)
        pad_ref[0:HALO, :] = jnp.where(first, 0.0, halo_ref[...])
        pad_ref[HALO:HALO + tile, :] = x_ref[...]
        for r0 in range(0, tile, HALO):
            acc = jnp.broadcast_to(b_ref[...], (HALO, cb))
            for k in range(kw):
                acc = acc + pad_ref[pl.ds(HALO - (kw - 1) + k + r0, HALO), :] * w_ref[k:k + 1, :]
            y_ref[pl.ds(r0, HALO), :] = acc

    return _pallas(
        body, side, name=name, grid=(t_rows // tile, ch // cb),
        in_specs=[pl.BlockSpec((tile, cb), lambda i, j: (i, j)),
                  pl.BlockSpec((HALO, cb), lambda i, j: (jnp.maximum(i * hb - 1, 0), j)),
                  pl.BlockSpec((kw, cb), lambda i, j: (0, j)), pl.BlockSpec((1, cb), lambda i, j: (0, j))],
        out_specs=pl.BlockSpec((tile, cb), lambda i, j: (i, j)),
        out_shape=jax.ShapeDtypeStruct((t_rows, ch), F32),
        scratch_shapes=[pltpu.VMEM((HALO + tile, cb), F32)],
        semantics=("parallel", "parallel"),
    )(x, x, w, b)


def _dwconv_bwd(x, dy, w, seq, name, side=None):
    t_rows, ch = x.shape
    kw = w.shape[0]
    tile = min(512, seq)
    cb = _divisor_tile(ch, 256, LANES)
    tps, hb, n_hb = seq // tile, tile // HALO, t_rows // HALO

    def body(x_ref, xh_ref, dy_ref, dyh_ref, w_ref, dx_ref, dw_ref, db_ref, xpad, dypad, sums):
        i = pl.program_id(1)
        first, last = _seq_flags(i, tps)

        @pl.when(i == 0)
        def _():
            sums[...] = jnp.zeros_like(sums)

        xpad[0:HALO, :] = jnp.where(first, 0.0, xh_ref[...])
        xpad[HALO:HALO + tile, :] = x_ref[...]
        dypad[0:tile, :] = dy_ref[...]
        dypad[tile:tile + HALO, :] = jnp.where(last, 0.0, dyh_ref[...])
        fold = lambda v: functools.reduce(jnp.add, [v[r:r + 8] for r in range(0, HALO, 8)])
        for r0 in range(0, tile, HALO):
            dyc = dy_ref[pl.ds(r0, HALO), :]
            acc = jnp.zeros((HALO, cb), F32)
            for k in range(kw):
                acc = acc + dypad[pl.ds(kw - 1 - k + r0, HALO), :] * w_ref[k:k + 1, :]
                sums[8 * k:8 * k + 8, :] += fold(dyc * xpad[pl.ds(HALO - (kw - 1) + k + r0, HALO), :])
            dx_ref[pl.ds(r0, HALO), :] = acc
            sums[8 * kw:8 * kw + 8, :] += fold(dyc)

        @pl.when(i == t_rows // tile - 1)
        def _():
            for k in range(kw):
                dw_ref[k:k + 1, :] = jnp.sum(sums[8 * k:8 * k + 8, :], axis=0, keepdims=True)
            db_ref[...] = jnp.sum(sums[8 * kw:8 * kw + 8, :], axis=0, keepdims=True)

    return _pallas(
        body, side, name=name, grid=(ch // cb, t_rows // tile),
        in_specs=[pl.BlockSpec((tile, cb), lambda j, i: (i, j)),
                  pl.BlockSpec((HALO, cb), lambda j, i: (jnp.maximum(i * hb - 1, 0), j)),
                  pl.BlockSpec((tile, cb), lambda j, i: (i, j)),
                  pl.BlockSpec((HALO, cb), lambda j, i: (jnp.minimum((i + 1) * hb, n_hb - 1), j)),
                  pl.BlockSpec((kw, cb), lambda j, i: (0, j))],
        out_specs=[pl.BlockSpec((tile, cb), lambda j, i: (i, j)), pl.BlockSpec((kw, cb), lambda j, i: (0, j)),
                   pl.BlockSpec((1, cb), lambda j, i: (0, j))],
        out_shape=[jax.ShapeDtypeStruct((t_rows, ch), F32), jax.ShapeDtypeStruct((kw, ch), F32),
                   jax.ShapeDtypeStruct((1, ch), F32)],
        scratch_shapes=[pltpu.VMEM((HALO + tile, cb), F32), pltpu.VMEM((tile + HALO, cb), F32),
                        pltpu.VMEM((8 * (kw + 1), cb), F32)],
        semantics=("parallel", "arbitrary"),
    )(x, x, dy, dy, w)


_ROWS = SUBLANES_BF16


def _lane_chunks(width, cap=6 * LANES):
    return [slice(c0, min(c0 + cap, width)) for c0 in range(0, width, cap)]


def _tap_rows(w_ref, cols):
    return [w_ref[k:k + 1, cols] for k in range(FFN_CONV_WIDTH)]


def _conv3_at(pad, taps, row, cols):
    z = pad[pl.ds(row, _ROWS), cols] * taps[2]
    z = z + pad[pl.ds(row - 1, _ROWS), cols] * taps[1]
    return z + pad[pl.ds(row - 2, _ROWS), cols] * taps[0]


def _ffn_mid_fwd(u, w, seq, name, side=None):
    t_rows, f2 = u.shape
    f = f2 // 2
    tile = min(256, seq)
    cb = _divisor_tile(f, 1408, LANES)
    nj, tps, hb, hl = f // cb, seq // tile, tile // SUBLANES_BF16, SUBLANES_BF16

    def body(ug_ref, uv_ref, hg_ref, hv_ref, wg_ref, wv_ref, a_ref, gpad, vpad):
        first, _ = _seq_flags(pl.program_id(0), tps)
        for t_ref, h_ref, pad in ((ug_ref, hg_ref, gpad), (uv_ref, hv_ref, vpad)):
            pad[0:hl, :] = jnp.where(first, 0.0, h_ref[...].astype(F32))
            pad[hl:hl + tile, :] = t_ref[...].astype(F32)
        for cols in _lane_chunks(cb):
            wg, wv = _tap_rows(wg_ref, cols), _tap_rows(wv_ref, cols)
            for r0 in range(0, tile, _ROWS):
                zg = _conv3_at(gpad, wg, hl + r0, cols)
                zv = _conv3_at(vpad, wv, hl + r0, cols)
                half = 0.5 * zg
                a_ref[pl.ds(r0, _ROWS), cols] = ((jnp.tanh(half) + 1.0) * half * zv).astype(a_ref.dtype)

    halo_map = lambda off: (lambda i, j: (jnp.maximum(i * hb - 1, 0), j + off))
    return _pallas(
        body, side, name=name, grid=(t_rows // tile, nj),
        in_specs=[pl.BlockSpec((tile, cb), lambda i, j: (i, j)), pl.BlockSpec((tile, cb), lambda i, j: (i, j + nj)),
                  pl.BlockSpec((hl, cb), halo_map(0)), pl.BlockSpec((hl, cb), halo_map(nj)),
                  pl.BlockSpec((3, cb), lambda i, j: (0, j)), pl.BlockSpec((3, cb), lambda i, j: (0, j + nj))],
        out_specs=pl.BlockSpec((tile, cb), lambda i, j: (i, j)),
        out_shape=jax.ShapeDtypeStruct((t_rows, f), BF16),
        scratch_shapes=[pltpu.VMEM((hl + tile, cb), F32), pltpu.VMEM((hl + tile, cb), F32)],
        semantics=("parallel", "parallel"),
    )(u, u, u, u, w, w)


def _ffn_mid_bwd(u, da, w, seq, name, side=None):
    t_rows, f2 = u.shape
    f = f2 // 2
    tile = min(256, seq)
    cb = _divisor_tile(f, 1408, LANES)
    hl = SUBLANES_BF16
    nj, tps, hb, n_hb, ext = f // cb, seq // tile, tile // hl, t_rows // hl, tile + hl

    def body(ug_ref, uv_ref, pg_ref, pv_ref, ng_ref, nv_ref, da_ref, dan_ref, wg_ref, wv_ref,
             dug_ref, duv_ref, dwg_ref, dwv_ref, gpad, vpad, dzg, dzv):
        i = pl.program_id(1)
        first, last = _seq_flags(i, tps)

        @pl.when(i == 0)
        def _():
            dwg_ref[...] = jnp.zeros_like(dwg_ref)
            dwv_ref[...] = jnp.zeros_like(dwv_ref)

        for t_ref, p_ref, n_ref, pad in ((ug_ref, pg_ref, ng_ref, gpad), (uv_ref, pv_ref, nv_ref, vpad)):
            pad[0:hl, :] = jnp.where(first, 0.0, p_ref[...].astype(F32))
            pad[hl:hl + tile, :] = t_ref[...].astype(F32)
            pad[hl + tile:hl + ext, :] = jnp.where(last, 0.0, n_ref[...].astype(F32))
        for cols in _lane_chunks(cb):
            wg, wv = _tap_rows(wg_ref, cols), _tap_rows(wv_ref, cols)
            for r0 in range(0, ext, _ROWS):
                zg = _conv3_at(gpad, wg, hl + r0, cols)
                zv = _conv3_at(vpad, wv, hl + r0, cols)
                if r0 < tile:
                    da = da_ref[pl.ds(r0, _ROWS), cols].astype(F32)
                else:
                    da = jnp.where(last, 0.0, dan_ref[:, cols].astype(F32))
                sg = _sigmoid(zg)
                dzg[pl.ds(r0, _ROWS), cols] = da * zv * (sg * (1.0 + zg * (1.0 - sg)))
                dzv[pl.ds(r0, _ROWS), cols] = da * (zg * sg)
        for dz, w_ref, pad, du_ref, dw_ref in ((dzg, wg_ref, gpad, dug_ref, dwg_ref), (dzv, wv_ref, vpad, duv_ref, dwv_ref)):
            for cols in _lane_chunks(cb):
                taps = _tap_rows(w_ref, cols)
                width = cols.stop - cols.start
                acc = [jnp.zeros((8, width), F32) for _ in range(FFN_CONV_WIDTH)]
                for r0 in range(0, tile, _ROWS):
                    d0 = dz[pl.ds(r0, _ROWS), cols]
                    du = dz[pl.ds(r0 + 2, _ROWS), cols] * taps[0] + dz[pl.ds(r0 + 1, _ROWS), cols] * taps[1] + d0 * taps[2]
                    du_ref[pl.ds(r0, _ROWS), cols] = du.astype(du_ref.dtype)
                    for k in range(FFN_CONV_WIDTH):
                        prod = d0 * pad[pl.ds(hl - 2 + k + r0, _ROWS), cols]
                        acc[k] = acc[k] + prod[0:8] + prod[8:16]
                for k in range(FFN_CONV_WIDTH):
                    dw_ref[k:k + 1, cols] += jnp.sum(acc[k], axis=0, keepdims=True)

    prev_map = lambda off: (lambda j, i: (jnp.maximum(i * hb - 1, 0), j + off))
    next_map = lambda off: (lambda j, i: (jnp.minimum((i + 1) * hb, n_hb - 1), j + off))
    tile_spec = lambda off: pl.BlockSpec((tile, cb), lambda j, i: (i, j + off))
    w_spec = lambda off: pl.BlockSpec((3, cb), lambda j, i: (0, j + off))
    return _pallas(
        body, side, name=name, grid=(nj, t_rows // tile),
        in_specs=[tile_spec(0), tile_spec(nj), pl.BlockSpec((hl, cb), prev_map(0)), pl.BlockSpec((hl, cb), prev_map(nj)),
                  pl.BlockSpec((hl, cb), next_map(0)), pl.BlockSpec((hl, cb), next_map(nj)),
                  tile_spec(0), pl.BlockSpec((hl, cb), next_map(0)), w_spec(0), w_spec(nj)],
        out_specs=[tile_spec(0), tile_spec(0), w_spec(0), w_spec(0)],
        out_shape=[jax.ShapeDtypeStruct((t_rows, f), BF16), jax.ShapeDtypeStruct((t_rows, f), BF16),
                   jax.ShapeDtypeStruct((3, f), F32), jax.ShapeDtypeStruct((3, f), F32)],
        scratch_shapes=[pltpu.VMEM((hl + ext, cb), F32), pltpu.VMEM((hl + ext, cb), F32),
                        pltpu.VMEM((ext, cb), F32), pltpu.VMEM((ext, cb), F32)],
        semantics=("parallel", "arbitrary"),
    )(u, u, u, u, u, u, da, da, w, w)


def _gla_chunk(q, k, v, lg, st, *, scale, chunk):
    row = lax.broadcasted_iota(jnp.int32, (chunk, chunk), 0)
    col = lax.broadcasted_iota(jnp.int32, (chunk, chunk), 1)
    causal = col <= row
    b = _cumsum_rows(lg)
    upto_mid = lax.broadcasted_iota(jnp.int32, lg.shape, 0) <= chunk // 2
    b_mid = jnp.sum(jnp.where(upto_mid, lg, 0.0), axis=0, keepdims=True)
    b_last = jnp.sum(lg, axis=0, keepdims=True)
    qs = q * scale
    scores = _bdot_nt(qs * jnp.exp(b - b_mid), k * jnp.exp(b_mid - b))
    o = _bdot(jnp.where(causal, scores, 0.0), v)
    o = o + _bdot_nt(qs * jnp.exp(b), st)
    st_new = st * jnp.exp(b_last) + _bdot_tn(v, k * jnp.exp(b_last - b))
    return o, st_new


_CHUNKS_PER_STEP = 4


def _gla_specs(specs, rows, n_blocks, reverse):
    if reverse:
        row = lambda bi, ci: bi * n_blocks + (n_blocks - 1 - ci)
    else:
        row = lambda bi, ci: bi * n_blocks + ci
    return [pl.BlockSpec((rows, w), lambda bi, ci, cb=cb: (row(bi, ci), cb)) for _, w, cb in specs], row


def _gla_fwd(q, k, v, lg, *, heads, dk, dv, scale, chunk, seq, name, side=None):
    t_rows = q[0].shape[0]
    per = _CHUNKS_PER_STEP if (seq // chunk) % _CHUNKS_PER_STEP == 0 else 1
    n_blocks = seq // (per * chunk)
    fn = functools.partial(_gla_chunk, scale=scale, chunk=chunk)

    def body(q_ref, k_ref, v_ref, lg_ref, o_ref, sts_ref, st_ref):
        @pl.when(pl.program_id(1) == 0)
        def _():
            st_ref[...] = jnp.zeros_like(st_ref)

        ks = [slice(h * dk, (h + 1) * dk) for h in range(heads)]
        vs = [slice(h * dv, (h + 1) * dv) for h in range(heads)]
        st = [st_ref[vs[h], :] for h in range(heads)]
        for s in range(per):
            rows = pl.ds(s * chunk, chunk)
            for h in range(heads):
                sts_ref[s, vs[h], :] = st[h]
                o, st[h] = fn(q_ref[rows, ks[h]].astype(F32), k_ref[rows, ks[h]].astype(F32),
                              v_ref[rows, vs[h]].astype(F32), lg_ref[rows, ks[h]], st[h])
                o_ref[rows, vs[h]] = o
        for h in range(heads):
            st_ref[vs[h], :] = st[h]

    in_specs, row = _gla_specs([q, k, v, lg], per * chunk, n_blocks, False)
    return _pallas(
        body, side, name=name, grid=(t_rows // seq, n_blocks), in_specs=in_specs,
        out_specs=[pl.BlockSpec((per * chunk, heads * dv), lambda bi, ci: (row(bi, ci), 0)),
                   pl.BlockSpec((per, heads * dv, dk), lambda bi, ci: (row(bi, ci), 0, 0))],
        out_shape=[jax.ShapeDtypeStruct((t_rows, heads * dv), F32),
                   jax.ShapeDtypeStruct((t_rows // chunk, heads * dv, dk), F32)],
        scratch_shapes=[pltpu.VMEM((heads * dv, dk), F32)],
        semantics=("arbitrary", "arbitrary"),
    )(q[0], k[0], v[0], lg[0])


def _gla_bwd(q, k, v, lg, states, do, *, heads, dk, dv, scale, chunk, seq, out_dtypes, name, side=None):
    t_rows = q[0].shape[0]
    per = _CHUNKS_PER_STEP if (seq // chunk) % _CHUNKS_PER_STEP == 0 else 1
    n_blocks = seq // (per * chunk)
    fn = functools.partial(_gla_chunk, scale=scale, chunk=chunk)

    def body(q_ref, k_ref, v_ref, lg_ref, do_ref, sts_ref, dq_ref, dk_ref, dv_ref, dlg_ref, dst_ref):
        @pl.when(pl.program_id(1) == 0)
        def _():
            dst_ref[...] = jnp.zeros_like(dst_ref)

        ks = [slice(h * dk, (h + 1) * dk) for h in range(heads)]
        vs = [slice(h * dv, (h + 1) * dv) for h in range(heads)]
        dst = [dst_ref[vs[h], :] for h in range(heads)]
        for s in reversed(range(per)):
            rows = pl.ds(s * chunk, chunk)
            for h in range(heads):
                _, vjp = jax.vjp(fn, q_ref[rows, ks[h]].astype(F32), k_ref[rows, ks[h]].astype(F32),
                                 v_ref[rows, vs[h]].astype(F32), lg_ref[rows, ks[h]], sts_ref[s, vs[h], :])
                dq, dkk, dvv, dlg, dst[h] = vjp((do_ref[rows, vs[h]].astype(F32), dst[h]))
                dq_ref[rows, ks[h]] = dq.astype(dq_ref.dtype)
                dk_ref[rows, ks[h]] = dkk.astype(dk_ref.dtype)
                dv_ref[rows, vs[h]] = dvv.astype(dv_ref.dtype)
                dlg_ref[rows, ks[h]] = dlg
        for h in range(heads):
            dst_ref[vs[h], :] = dst[h]

    do_view = (do, heads * dv, 0)
    in_specs, row = _gla_specs([q, k, v, lg, do_view], per * chunk, n_blocks, True)
    in_specs.append(pl.BlockSpec((per, heads * dv, dk), lambda bi, ci: (row(bi, ci), 0, 0)))
    wide = lambda w: pl.BlockSpec((per * chunk, w), lambda bi, ci: (row(bi, ci), 0))
    return _pallas(
        body, side, name=name, grid=(t_rows // seq, n_blocks), in_specs=in_specs,
        out_specs=[wide(heads * dk), wide(heads * dk), wide(heads * dv), wide(heads * dk)],
        out_shape=[jax.ShapeDtypeStruct((t_rows, heads * dk), out_dtypes[0]),
                   jax.ShapeDtypeStruct((t_rows, heads * dk), out_dtypes[1]),
                   jax.ShapeDtypeStruct((t_rows, heads * dv), out_dtypes[2]),
                   jax.ShapeDtypeStruct((t_rows, heads * dk), F32)],
        scratch_shapes=[pltpu.VMEM((heads * dv, dk), F32)],
        semantics=("arbitrary", "arbitrary"),
    )(q[0], k[0], v[0], lg[0], do, states)


def _head_rms_gate(o, r, g, heads):
    d = o.shape[1] // heads
    parts = [_rms(o[:, h * d:(h + 1) * d], g) for h in range(heads)]
    return jnp.concatenate(parts, axis=1) * _silu(r)


def _gla_gate(glr, w_g2p, b_g2):
    return _log_sigmoid(_bdot(glr, w_g2p) + b_g2) * (1.0 / GLA_GATE_NORM)


def _glu(a, gate, b_in):
    d = a.shape[1]
    return (a + b_in[:, :d]) * _sigmoid(gate + b_in[:, d:])


def _ln_silu(y, g, b):
    return _silu(_layer_norm(y, g, b))


def _sgu(pre, b_in, ln_g, ln_b, w_s, b_st):
    d = pre.shape[1] // 2
    gd = d // SGU_GROUPS
    uv = _gelu(pre + b_in)
    u, v = uv[:, :d], _layer_norm(uv[:, d:], ln_g, ln_b)
    row = lax.broadcasted_iota(jnp.int32, (SGU_CHUNK, SGU_CHUNK), 0)
    col = lax.broadcasted_iota(jnp.int32, (SGU_CHUNK, SGU_CHUNK), 1)
    lane = lax.broadcasted_iota(jnp.int32, b_st.shape, 1)
    rows = []
    for c in range(pre.shape[0] // SGU_CHUNK):
        rs = slice(c * SGU_CHUNK, (c + 1) * SGU_CHUNK)
        parts = []
        for g in range(SGU_GROUPS):
            wg = jnp.where(col <= row, w_s[g], 0.0)
            bias = jnp.sum(jnp.where(lane == g, b_st, 0.0), axis=1, keepdims=True)
            parts.append(_bdot(wg, v[rs, g * gd:(g + 1) * gd]) + bias)
        rows.append(jnp.concatenate(parts, axis=1))
    s = rows[0] if len(rows) == 1 else jnp.concatenate(rows, axis=0)
    return u * s


def _hgrn_pre(q, f, table, layer):
    t = table - jnp.max(table, axis=0, keepdims=True)
    e = jnp.exp(t)
    sm = e / jnp.sum(e, axis=0, keepdims=True)
    rows = lax.broadcasted_iota(jnp.int32, table.shape, 0)
    lb = jnp.sum(jnp.where((rows >= 1) & (rows <= layer), sm, 0.0), axis=0, keepdims=True)
    sf = _sigmoid(f)
    return _silu(q), (1.0 - lb) * (1.0 - sf), jnp.log(lb + (1.0 - lb) * sf)


def _ffn_fwd(x, w, seq, sv):
    sv["h2"] = _rms_fwd(x, w["norm"], "ffn_norm")
    late = w.get("late")
    sv["u"] = _mm(sv["h2"], w["w_up"], out_dtype=BF16, name="ffn_up", side=late[0] if late else None)
    sv["a"] = _ffn_mid_fwd(sv["u"], w["w_dw"], seq, "ffn_mid", sv.pop("side", None))
    if late:
        sv["late_w"] = late[1]()
        w = dict(w, **sv["late_w"])
    return _mm(sv["a"], w["w_down"], add=x, name="ffn_down")


def _ffn_bwd(x, dy, w, seq, sv):
    g = {}
    da = _mm(dy, w["w_down_t"], out_dtype=BF16, name="ffn_down_dx")
    g["w_down"] = _mm_tn(sv["a"], dy, name="ffn_down_dw")
    early = sv.pop("put_early", None)
    side = _Side.join([sv.pop("side", None), early(g["w_down"]) if early else None])
    dug, duv, dwg, dwv = _ffn_mid_bwd(sv["u"], da, w["w_dw"], seq, "ffn_mid_bwd", side)
    g["w_dw"] = jnp.concatenate([dwg, dwv], axis=1)
    g["w_up_gate"] = _mm_tn(sv["h2"], dug, name="ffn_up_dw")
    g["w_up_val"] = _mm_tn(sv["h2"], duv, name="ffn_up_dw")
    dh = _mm(dug, w["w_up_t_gate"], out_dtype=F32, name="ffn_up_dx")
    dh = _mm(duv, w["w_up_t_val"], add=dh, out_dtype=BF16, name="ffn_up_dx2")
    dx, g["norm"] = _rms_bwd(x, w["norm"], dh, dy, "ffn_norm_bwd")
    return dx, g


def _gla_layer_fwd(x, h, w, seq, sv):
    d = x.shape[1]
    dkt = d // 2
    dk, dv = dkt // GLA_HEADS, d // GLA_HEADS
    proj = _mm(h, w["w_main"], out_dtype=F32, name="gla_in", side=sv.pop("side_in", None))
    glr = _mm(h, w["w_glr"], out_dtype=BF16, name="gla_in_g")
    (lg,), _ = _tile_call("gla_gate", lambda a, b, c: ([_gla_gate(a.astype(F32), b, c)], []), [glr],
                          [w["w_g2p"], w["b_g2"]], [(dkt, F32)], [], 512)
    q, k, v, r = (proj, dkt, 0), (proj, dkt, 1), (proj, d, 1), (proj, d, 2)
    o, states = _gla_fwd(q, k, v, (lg, dkt, 0), heads=GLA_HEADS, dk=dk, dv=dv, scale=dk ** -0.5, chunk=GLA_CHUNK,
                         seq=seq, name="gla_core", side=sv.pop("side", None))
    (o2,), _ = _tile_call("gla_post", lambda ov, rv, gv: ([_head_rms_gate(ov, rv.astype(F32), gv, GLA_HEADS)], []),
                          [o, r], [w["norm"]], [(d, BF16)], [], 256)
    sv.update(proj=proj, glr=glr, lg=lg, o=o, states=states, o2=o2)
    return _mm(o2, w["w_out"], add=x, name="mix_out")


def _gla_layer_bwd(h, dy, w, seq, sv):
    d = dy.shape[1]
    dkt = d // 2
    dk, dv = dkt // GLA_HEADS, d // GLA_HEADS
    proj, glr, lg, o = sv["proj"], sv["glr"], sv["lg"], sv["o"]
    g = {}
    do2 = _mm(dy, w["w_out_t"], out_dtype=F32, name="gla_out_dx")
    g["w_out"] = _mm_tn(sv["o2"], dy, name="mix_out_dw")

    def post_bwd(ov, rv, ctv, gv):
        _, vjp = jax.vjp(functools.partial(_head_rms_gate, heads=GLA_HEADS), ov, rv.astype(F32), gv)
        d_o, d_r, d_g = vjp(ctv.astype(F32))
        return [d_o, d_r], [d_g]

    (d_o, d_r), (g["norm"],) = _tile_call("gla_post_bwd", post_bwd, [o, (proj, d, 2), do2], [w["norm"]],
                                          [(d, F32), (d, BF16)], [w["norm"].shape], 256)
    q, k, v = (proj, dkt, 0), (proj, dkt, 1), (proj, d, 1)
    dq, dkk, dvv, dlg = _gla_bwd(q, k, v, (lg, dkt, 0), sv["states"], d_o, heads=GLA_HEADS, dk=dk, dv=dv,
                                 scale=dk ** -0.5, chunk=GLA_CHUNK, seq=seq, out_dtypes=(BF16, BF16, BF16),
                                 name="gla_core_bwd", side=sv.pop("side", None))

    def gate_bwd(glrv, ctv, wv, bv):
        _, vjp = jax.vjp(_gla_gate, glrv.astype(F32), wv, bv)
        d_glr, d_w, d_b = vjp(ctv)
        return [d_glr], [d_w, d_b]

    (dglr,), (g["w_g2p"], g["b_g2"]) = _tile_call("gla_gate_bwd", gate_bwd, [glr, dlg], [w["w_g2p"], w["b_g2"]],
                                                  [(LANES, BF16)], [w["w_g2p"].shape, w["b_g2"].shape], 512)
    dproj = jnp.concatenate([dq, dkk, dvv, d_r], axis=1)
    g["w_main"] = _mm_tn(h, dproj, name="gla_in_dw")
    g["w_glr"] = _mm_tn(h, dglr, name="gla_in_g_dw")
    dh = _mm(dproj, w["w_main_t"], out_dtype=F32, name="gla_in_dx")
    dh = _mm(dglr, w["w_glr_t"], add=dh, out_dtype=BF16, name="gla_in_g_dx")
    return dh, g


def _cv_layer_fwd(x, h, w, seq, sv):
    d = x.shape[1]
    pre = _mm(h, w["w_in"], out_dtype=BF16, name="cv_in")
    (y1,), _ = _tile_call("cv_glu", lambda a, gt, b: ([_glu(a.astype(F32), gt.astype(F32), b)], []),
                          [(pre, d, 0), (pre, d, 1)], [w["b_in"]], [(d, F32)], [], 512)
    y2 = _dwconv_fwd(y1, w["w_dw"], w["b_dw"], seq, "cv_conv", sv.pop("side", None))
    (y3,), _ = _tile_call("cv_ln", lambda y, a, b: ([_ln_silu(y, a, b)], []), [y2], [w["ln_g"], w["ln_b"]],
                          [(d, BF16)], [], 512)
    sv.update(pre=pre, y1=y1, y2=y2, y3=y3)
    return _mm(y3, w["w_out"], bias=w["b_out"], add=x, name="mix_out_b")


def _cv_layer_bwd(h, dy, w, seq, sv):
    d = dy.shape[1]
    pre = sv["pre"]
    g = {}
    dy3 = _mm(dy, w["w_out_t"], out_dtype=BF16, name="mix_out_dx")
    g["w_out"] = _mm_tn(sv["y3"], dy, name="mix_out_dw")
    g["b_out"] = _colsum(dy, "bias_out_dw")

    def ln_bwd(yv, ctv, av, bv):
        _, vjp = jax.vjp(_ln_silu, yv, av, bv)
        d_y, d_a, d_b = vjp(ctv.astype(F32))
        return [d_y], [d_a, d_b]

    (dy2,), (g["ln_g"], g["ln_b"]) = _tile_call("cv_ln_bwd", ln_bwd, [sv["y2"], dy3], [w["ln_g"], w["ln_b"]],
                                                [(d, F32)], [w["ln_g"].shape, w["ln_b"].shape], 512)
    dy1, g["w_dw"], g["b_dw"] = _dwconv_bwd(sv["y1"], dy2, w["w_dw"], seq, "cv_conv_bwd", sv.pop("side", None))

    def glu_bwd(av, gv, ctv, bv):
        _, vjp = jax.vjp(_glu, av.astype(F32), gv.astype(F32), bv)
        d_a, d_g, d_b = vjp(ctv)
        return [jnp.concatenate([d_a, d_g], axis=1)], [d_b]

    (dpre,), (g["b_in"],) = _tile_call("cv_glu_bwd", glu_bwd, [(pre, d, 0), (pre, d, 1), dy1], [w["b_in"]],
                                       [(2 * d, BF16)], [w["b_in"].shape], 512)
    g["w_in"] = _mm_tn(h, dpre, name="in2_dw")
    dh = _mm(dpre, w["w_in_t"], out_dtype=BF16, name="in2_dx")
    return dh, g


def _sg_layer_fwd(x, h, w, seq, sv):
    d = x.shape[1]
    pre = _mm(h, w["w_in"], out_dtype=BF16, name="sg_in", side=sv.pop("side_in", None))
    pars = [w["b_in"], w["ln_g"], w["ln_b"], w["w_s"], w["b_st"]]
    (p,), _ = _tile_call("sg_gate", lambda pv, *ps: ([_sgu(pv.astype(F32), *ps)], []), [pre], pars, [(d, BF16)], [],
                         SGU_CHUNK, side=sv.pop("side", None))
    sv.update(pre=pre, p=p)
    return _mm(p, w["w_out"], bias=w["b_out"], add=x, name="mix_out_b")


def _sg_layer_bwd(h, dy, w, seq, sv):
    d = dy.shape[1]
    g = {}
    dp = _mm(dy, w["w_out_t"], out_dtype=BF16, name="mix_out_dx")
    g["w_out"] = _mm_tn(sv["p"], dy, name="mix_out_dw")
    g["b_out"] = _colsum(dy, "bias_out_dw")
    pars = [w["b_in"], w["ln_g"], w["ln_b"], w["w_s"], w["b_st"]]

    def sgu_bwd(pv, ctv, *ps):
        _, vjp = jax.vjp(_sgu, pv.astype(F32), *ps)
        grads = vjp(ctv.astype(F32))
        return [grads[0]], list(grads[1:])

    (dpre,), (g["b_in"], g["ln_g"], g["ln_b"], g["w_s"], g["b_st"]) = _tile_call(
        "sg_gate_bwd", sgu_bwd, [sv["pre"], dp], pars, [(2 * d, BF16)], [p.shape for p in pars], SGU_CHUNK,
        side=sv.pop("side", None))
    g["w_in"] = _mm_tn(h, dpre, name="in2_dw")
    dh = _mm(dpre, w["w_in_t"], out_dtype=BF16, name="in2_dx")
    return dh, g


def _hg_layer_fwd(x, h, w, seq, sv, layer):
    d = x.shape[1]
    heads = d // HGRN_EXPAND
    proj = _mm(h, w["w_in"], out_dtype=BF16, name="hg_in", side=sv.pop("side_in", None))
    pre = functools.partial(_hgrn_pre, layer=layer)
    (qs, kk, lg), _ = _tile_call("hg_pre", lambda qv, fv, tb: (list(pre(qv.astype(F32), fv.astype(F32), tb)), []),
                                 [(proj, d, 0), (proj, d, 1)], [w["lb_table"]], [(d, BF16), (d, F32), (d, F32)], [], 256)
    o, states = _gla_fwd((qs, d, 0), (kk, d, 0), (proj, d, 2), (lg, d, 0), heads=heads, dk=HGRN_EXPAND,
                         dv=HGRN_EXPAND, scale=1.0, chunk=HGRN_CHUNK, seq=seq, name="hg_core",
                         side=sv.pop("side", None))
    (o2,), _ = _tile_call("hg_post", lambda ov, gv, nv: ([_head_rms_gate(ov, gv.astype(F32), nv, heads)], []),
                          [o, (proj, d, 3)], [w["norm"]], [(d, BF16)], [], 256)
    sv.update(proj=proj, qs=qs, kk=kk, lg=lg, o=o, states=states, o2=o2)
    return _mm(o2, w["w_out"], add=x, name="mix_out")


def _hg_layer_bwd(h, dy, w, seq, sv, layer):
    d = dy.shape[1]
    heads = d // HGRN_EXPAND
    proj = sv["proj"]
    g = {}
    do2 = _mm(dy, w["w_out_t"], out_dtype=BF16, name="mix_out_dx")
    g["w_out"] = _mm_tn(sv["o2"], dy, name="mix_out_dw")

    def post_bwd(ov, gv, ctv, nv):
        _, vjp = jax.vjp(functools.partial(_head_rms_gate, heads=heads), ov, gv.astype(F32), nv)
        d_o, d_g, d_n = vjp(ctv.astype(F32))
        return [d_o, d_g], [d_n]

    (d_o, d_gate), (g["norm"],) = _tile_call("hg_post_bwd", post_bwd, [sv["o"], (proj, d, 3), do2], [w["norm"]],
                                             [(d, F32), (d, BF16)], [w["norm"].shape], 256)
    dqs, dkk, di, dlg = _gla_bwd((sv["qs"], d, 0), (sv["kk"], d, 0), (proj, d, 2), (sv["lg"], d, 0), sv["states"], d_o,
                                 heads=heads, dk=HGRN_EXPAND, dv=HGRN_EXPAND, scale=1.0, chunk=HGRN_CHUNK, seq=seq,
                                 out_dtypes=(F32, F32, BF16), name="hg_core_bwd", side=sv.pop("side", None))

    def pre_bwd(qv, fv, c1, c2, c3, tb):
        _, vjp = jax.vjp(functools.partial(_hgrn_pre, layer=layer), qv.astype(F32), fv.astype(F32), tb)
        d_q, d_f, d_t = vjp((c1, c2, c3))
        return [jnp.concatenate([d_q, d_f], axis=1)], [d_t]

    (dqf,), (g["lb_table"],) = _tile_call("hg_pre_bwd", pre_bwd, [(proj, d, 0), (proj, d, 1), dqs, dkk, dlg],
                                          [w["lb_table"]], [(2 * d, BF16)], [w["lb_table"].shape], 256)
    dproj = jnp.concatenate([dqf, di, d_gate], axis=1)
    g["w_in"] = _mm_tn(h, dproj, name="hg_in_dw")
    dh = _mm(dproj, w["w_in_t"], out_dtype=BF16, name="hg_in_dx")
    return dh, g


_MIXERS = ("gla", "cv", "sg", "hg")


_BIG_KEYS = {"gla": ("w_main", "w_glr", "w_out"), "cv": ("w_in", "w_out"), "sg": ("w_in", "w_out"), "hg": ("w_in", "w_out"),
             "ffn": ("w_up_gate", "w_up_val", "w_down")}


def _local_step(x, target, w, seq, get_big, put_big, ride=lambda kind, layer, forward: None, put_early=None):
    depth = w["norm_mix"].shape[0]
    d = x.shape[1]
    saved, big = [], {}
    for layer in range(depth):
        mixer = _MIXERS[layer % 4]
        rides = ride(mixer, layer, True)
        rides = rides if isinstance(rides, list) else [None, rides]
        sv = {"x_in": x, "side_in": rides[0], "side": rides[1]}
        sv["h"] = _rms_fwd(x, w["norm_mix"][layer:layer + 1], "mix_norm")
        big[mixer, layer] = get_big(mixer, layer)
        wm = dict(w[mixer], **big[mixer, layer])
        if mixer == "gla":
            x = _gla_layer_fwd(x, sv["h"], wm, seq, sv)
        elif mixer == "cv":
            x = _cv_layer_fwd(x, sv["h"], wm, seq, sv)
        elif mixer == "sg":
            x = _sg_layer_fwd(x, sv["h"], wm, seq, sv)
        else:
            x = _hg_layer_fwd(x, sv["h"], wm, seq, sv, layer)
        sv["x_mid"] = x
        big["ffn", layer] = get_big("ffn", layer)
        sv["ffn"] = {"side": ride("ffn", layer, True)}
        wf = dict(w["ffn"][layer], norm=w["norm_ffn"][layer:layer + 1], **big["ffn", layer])
        x = _ffn_fwd(x, wf, seq, sv["ffn"])
        big["ffn", layer].pop("late", None)
        big["ffn", layer].update(sv["ffn"].pop("late_w", {}))
        saved.append(sv)

    def head(xv, tv, gv):
        y, vjp = jax.vjp(_rms, xv, gv)
        err = y - tv
        dx, dg = vjp(err * (1.0 / d))
        part = 0.5 * jnp.sum(jnp.mean(err * err, axis=-1, keepdims=True), axis=0, keepdims=True)
        return [dx], [jnp.broadcast_to(part, (1, LANES)), dg]

    (dx,), (loss, g_final) = _tile_call("loss_head", head, [x, target], [w["norm_final"]], [(d, F32)],
                                        [(1, LANES), (1, d)], 512)
    grads = {"norm_final": g_final, "norm_mix": [None] * depth, "norm_ffn": [None] * depth, "ffn": [None] * depth}
    for layer in reversed(range(depth)):
        mixer = _MIXERS[layer % 4]
        sv = saved[layer]
        wf = dict(w["ffn"][layer], norm=w["norm_ffn"][layer:layer + 1], **big["ffn", layer])
        sv["ffn"]["side"] = ride("ffn", layer, False)
        if put_early is not None:
            sv["ffn"]["put_early"] = functools.partial(put_early, layer)
        dx, gf = _ffn_bwd(sv["x_mid"], dx, wf, seq, sv["ffn"])
        put_big("ffn", layer, {k: gf.pop(k) for k in _BIG_KEYS["ffn"]})
        sv["side"] = ride(mixer, layer, False)
        grads["norm_ffn"][layer] = gf.pop("norm")
        grads["ffn"][layer] = gf
        wm = dict(w[mixer], **big[mixer, layer])
        if mixer == "gla":
            dh, gm = _gla_layer_bwd(sv["h"], dx, wm, seq, sv)
        elif mixer == "cv":
            dh, gm = _cv_layer_bwd(sv["h"], dx, wm, seq, sv)
        elif mixer == "sg":
            dh, gm = _sg_layer_bwd(sv["h"], dx, wm, seq, sv)
        else:
            dh, gm = _hg_layer_bwd(sv["h"], dx, wm, seq, sv, layer)
        put_big(mixer, layer, {k: gm.pop(k) for k in _BIG_KEYS[mixer]})
        grads[mixer] = gm
        last = ride(mixer, layer, False) if layer == 0 else None
        dx, grads["norm_mix"][layer] = _rms_bwd(sv["x_in"], w["norm_mix"][layer:layer + 1], dh, dx, "mix_norm_bwd", last)
    return loss, dx, grads


def _prep_small(p):
    row = lambda a: a.reshape(1, -1).astype(F32)
    w = {"norm_mix": p["norm_mix"].astype(F32), "norm_ffn": p["norm_ffn"].astype(F32), "norm_final": row(p["norm_final"])}
    w["gla"] = dict(w_g2p=jnp.pad(p["gla_w_g2"][0].astype(F32), ((0, LANES - GLA_RANK), (0, 0))), b_g2=row(p["gla_b_g2"]),
                    norm=row(p["gla_norm"]))
    w["cv"] = dict(b_in=row(p["cv_b_in"]), w_dw=p["cv_w_dw"][0].astype(F32), b_dw=row(p["cv_b_dw"]), ln_g=row(p["cv_ln_g"]),
                   ln_b=row(p["cv_ln_b"]), b_out=row(p["cv_b_out"]))
    b_st = jnp.pad(p["sg_b_s"][0].astype(F32).T, ((0, 0), (0, LANES - SGU_GROUPS)))
    w["sg"] = dict(b_in=row(p["sg_b_in"]), ln_g=row(p["sg_ln_g"]), ln_b=row(p["sg_ln_b"]), w_s=p["sg_w_s"][0].astype(F32),
                   b_st=b_st, b_out=row(p["sg_b_out"]))
    w["hg"] = dict(lb_table=p["hg_lb_table"].astype(F32), norm=row(p["hg_norm"]))
    w["ffn"] = [dict(w_dw=p["ffn_w_dw"][layer].astype(F32)) for layer in range(p["ffn_w_dw"].shape[0])]
    return w


def _small_grads(g):
    gla, cv, sg, hg = g["gla"], g["cv"], g["sg"], g["hg"]
    return {
        "norm_mix": jnp.concatenate(g["norm_mix"], axis=0), "norm_ffn": jnp.concatenate(g["norm_ffn"], axis=0),
        "norm_final": g["norm_final"][0],
        "gla_w_g2": gla["w_g2p"][:GLA_RANK][None], "gla_b_g2": gla["b_g2"], "gla_norm": gla["norm"],
        "cv_b_in": cv["b_in"], "cv_w_dw": cv["w_dw"][None], "cv_b_dw": cv["b_dw"], "cv_ln_g": cv["ln_g"],
        "cv_ln_b": cv["ln_b"], "cv_b_out": cv["b_out"],
        "sg_b_in": sg["b_in"], "sg_ln_g": sg["ln_g"], "sg_ln_b": sg["ln_b"], "sg_w_s": sg["w_s"][None],
        "sg_b_s": sg["b_st"][:, :SGU_GROUPS].T[None], "sg_b_out": sg["b_out"],
        "hg_lb_table": hg["lb_table"], "hg_norm": hg["norm"],
        "ffn_w_dw": jnp.stack([f["w_dw"] for f in g["ffn"]]),
    }


def _oriented(kind, mats):
    if kind == "ffn":
        (up, up_t), (down, down_t) = mats["w_up"], mats["w_down"]
        f = down.shape[0]
        return dict(w_up=up, w_up_t_gate=up_t[:f], w_up_t_val=up_t[f:], w_down=down, w_down_t=down_t)
    (w_in, w_in_t), (w_out, w_out_t) = mats["w_in"], mats["w_out"]
    if kind != "gla":
        return dict(w_in=w_in, w_in_t=w_in_t, w_out=w_out, w_out_t=w_out_t)
    n_main = w_in.shape[1] - GLA_RANK
    return dict(w_main=w_in[:, :n_main], w_glr=jnp.pad(w_in[:, n_main:], ((0, 0), (0, LANES - GLA_RANK))),
                w_main_t=w_in_t[:n_main], w_glr_t=jnp.pad(w_in_t[n_main:], ((0, LANES - GLA_RANK), (0, 0))),
                w_out=w_out, w_out_t=w_out_t)


def _all_gather(x, *, name):
    m_per, n = x.shape

    def body(x_ref, out_ref, send_sems, recv_sems, local_sem):
        mx, my, mc = lax.axis_index("x"), lax.axis_index("y"), lax.axis_index("c")
        me, sibling = (mx, my, mc), (mx, my, 1 - mc)
        chips = [(1 - mx, my), (mx, 1 - my), (1 - mx, 1 - my)]

        def rows(px, py, pc):
            return out_ref.at[pl.ds((4 * px + 2 * py + pc) * m_per, m_per), :]

        def copy(k, block, to, src=None):
            return pltpu.make_async_remote_copy(
                src_ref=rows(*block) if src is None else src, dst_ref=rows(*block), send_sem=send_sems.at[k],
                recv_sem=recv_sems.at[k], device_id=to, device_id_type=MESH)

        mine = pltpu.make_async_copy(x_ref, rows(*me), local_sem)
        mine.start()
        first = [copy(0, me, sibling, src=x_ref)]
        first += [copy(1 + j, me, (*chip, mc), src=x_ref) for j, chip in enumerate(chips)]
        for cp in first:
            cp.start()
        passed = [copy(4 + j, (*chip, mc), sibling) for j, chip in enumerate(chips)]
        for j, chip in enumerate(chips):
            copy(1 + j, (*chip, mc), me).wait_recv()
            passed[j].start()
        copy(0, sibling, me).wait_recv()
        for j, chip in enumerate(chips):
            copy(4 + j, (*chip, 1 - mc), me).wait_recv()
        for cp in first + passed:
            cp.wait_send()
        mine.wait()

    return pl.pallas_call(
        body, name=name, out_shape=jax.ShapeDtypeStruct((N_DEV * m_per, n), x.dtype),
        in_specs=[pl.BlockSpec(memory_space=pltpu.VMEM)], out_specs=pl.BlockSpec(memory_space=pltpu.VMEM),
        scratch_shapes=[pltpu.SemaphoreType.DMA((7,)), pltpu.SemaphoreType.DMA((7,)), pltpu.SemaphoreType.DMA],
    )(x)


def _my_index():
    return 4 * lax.axis_index("x") + 2 * lax.axis_index("y") + lax.axis_index("c")


def _gather_stage(srcs, *, name):
    n = len(srcs)

    def body(*refs):
        x_refs, out_refs = refs[:n], refs[n:2 * n]
        send_sems, recv_sems, local_sems = refs[2 * n:]
        mx, my, mc = lax.axis_index("x"), lax.axis_index("y"), lax.axis_index("c")
        me, sibling = (mx, my, mc), (mx, my, 1 - mc)
        chips = [(1 - mx, my), (mx, 1 - my), (1 - mx, 1 - my)]

        def slot(i, px, py, pc):
            return out_refs[i].at[4 * px + 2 * py + pc]

        def copy(i, k, block, to, src=None):
            return pltpu.make_async_remote_copy(
                src_ref=slot(i, *block) if src is None else src, dst_ref=slot(i, *block), send_sem=send_sems.at[7 * i + k],
                recv_sem=recv_sems.at[7 * i + k], device_id=to, device_id_type=MESH)

        mine = [pltpu.make_async_copy(x_refs[i], slot(i, *me), local_sems.at[i]) for i in range(n)]
        first = [copy(i, 0, me, sibling, src=x_refs[i]) for i in range(n)]
        first += [copy(i, 1 + j, me, (*chip, mc), src=x_refs[i]) for j, chip in enumerate(chips) for i in range(n)]
        for cp in mine + first:
            cp.start()
        passed = []
        for j, chip in enumerate(chips):
            for i in range(n):
                copy(i, 1 + j, (*chip, mc), me).wait_recv()
                passed.append(copy(i, 4 + j, (*chip, mc), sibling))
                passed[-1].start()
        for i in range(n):
            copy(i, 0, sibling, me).wait_recv()
            for j, chip in enumerate(chips):
                copy(i, 4 + j, (*chip, 1 - mc), me).wait_recv()
        for cp in first + passed:
            cp.wait_send()
        for cp in mine:
            cp.wait()

    any_space = pl.BlockSpec(memory_space=pl.ANY)
    return pl.pallas_call(
        body, name=name, out_shape=[jax.ShapeDtypeStruct((N_DEV,) + s.shape, s.dtype) for s in srcs],
        in_specs=[any_space] * n, out_specs=[any_space] * n,
        scratch_shapes=[pltpu.SemaphoreType.DMA((7 * n,)), pltpu.SemaphoreType.DMA((7 * n,)), pltpu.SemaphoreType.DMA((n,))],
    )(*srcs)


def _adamw_math(g, w, m, v):
    c1, c2 = 1.0 - ADAM_B1 ** ADAM_STEP, 1.0 - ADAM_B2 ** ADAM_STEP
    m_new = ADAM_B1 * m + (1.0 - ADAM_B1) * g
    v_new = ADAM_B2 * v + (1.0 - ADAM_B2) * (g * g)
    delta = -ADAM_LR * ((m_new / c1) / (jnp.sqrt(v_new / c2) + ADAM_EPS) + ADAM_WD * w)
    return delta, m_new, v_new


def _adamw_big(slots, w, m, v, layer, *, name):
    _, r, c = slots.shape
    tr = _divisor_tile(r, max(8, (200 * 1024) // c // 8 * 8), 8)

    def body(s_ref, w_ref, m_ref, v_ref, g_out, d_out, m_out, v_out):
        g = s_ref[0].astype(F32)
        for p in range(1, N_DEV):
            g = g + s_ref[p].astype(F32)
        g_out[...] = g
        d_out[...], m_out[...], v_out[...] = _adamw_math(g, w_ref[...], m_ref[...], v_ref[...])

    blk = pl.BlockSpec((tr, c), lambda i: (i, 0))
    lay = pl.BlockSpec((None, tr, c), lambda i: (layer, i, 0))
    return pl.pallas_call(
        body, name=name, grid=(r // tr,), in_specs=[pl.BlockSpec((N_DEV, tr, c), lambda i: (0, i, 0)), lay, lay, lay],
        out_specs=[blk] * 4, out_shape=[jax.ShapeDtypeStruct((r, c), F32)] * 4,
        compiler_params=pltpu.CompilerParams(dimension_semantics=("parallel",)),
    )(slots, w, m, v)


def _sum_small(got, r_re, r_sh, *, name):
    per_dev = r_re + N_DEV * r_sh

    def body(got_ref, re_ref, sh_ref):
        mine = r_re + _my_index() * r_sh
        acc_re = got_ref[0:r_re, :]
        acc_sh = got_ref[pl.ds(pl.multiple_of(mine, 8), r_sh), :]
        for p in range(1, N_DEV):
            acc_re = acc_re + got_ref[p * per_dev:p * per_dev + r_re, :]
            acc_sh = acc_sh + got_ref[pl.ds(pl.multiple_of(p * per_dev + mine, 8), r_sh), :]
        re_ref[...] = acc_re
        sh_ref[...] = acc_sh

    return pl.pallas_call(body, name=name, out_shape=[jax.ShapeDtypeStruct((r_re, LANES), F32),
                                                       jax.ShapeDtypeStruct((r_sh, LANES), F32)])(got)


def _adamw_small(gs, ws, ms, vs, *, name):
    n = len(gs)

    def body(*refs):
        ins, outs = refs[:4 * n], refs[4 * n:]
        for i in range(n):
            res = _adamw_math(ins[i][...], ins[n + i][...], ins[2 * n + i][...], ins[3 * n + i][...])
            for j in range(3):
                outs[j * n + i][...] = res[j]

    out = pl.pallas_call(body, name=name, out_shape=[jax.ShapeDtypeStruct(a.shape, F32) for a in ws] * 3)(*gs, *ws, *ms, *vs)
    return out[:n], out[n:2 * n], out[2 * n:]


def _layout(shapes, row_align, total_align):
    lay, off = {}, 0
    for name, shape in shapes.items():
        size = int(np.prod(shape))
        rows = -(-size // LANES)
        rows = -(-rows // row_align) * row_align
        lay[name] = (off, rows, size, tuple(shape))
        off += rows
    return lay, -(-off // total_align) * total_align


def _pack(arrs, lay, total, dtype, lead=()):
    parts = []
    nl = len(lead)
    for name, (off, rows, size, shape) in lay.items():
        flat = arrs[name].astype(dtype).reshape(*lead, size)
        parts.append(jnp.pad(flat, [(0, 0)] * nl + [(0, rows * LANES - size)]).reshape(*lead, rows, LANES))
    used = sum(v[1] for v in lay.values())
    if total > used:
        parts.append(jnp.zeros((*lead, total - used, LANES), dtype))
    return jnp.concatenate(parts, axis=nl)


def _unpack(buf, lay, lead=()):
    out = {}
    nl = len(lead)
    for name, (off, rows, size, shape) in lay.items():
        part = lax.slice_in_dim(buf, off, off + rows, axis=nl).reshape(*lead, rows * LANES)
        out[name] = lax.slice_in_dim(part, 0, size, axis=nl).reshape(*lead, *shape)
    return out


_SHARD_AXIS = {
    "norm_mix": None, "norm_ffn": None, "norm_final": None, "gla_w_in": 2, "gla_w_g2": 2, "gla_b_g2": None,
    "gla_norm": None, "gla_w_out": 1, "cv_w_in": 2, "cv_b_in": 1, "cv_w_dw": 2, "cv_b_dw": 1, "cv_ln_g": 1,
    "cv_ln_b": 1, "cv_w_out": 1, "cv_b_out": 1, "sg_w_in": 2, "sg_b_in": 1, "sg_ln_g": 1, "sg_ln_b": 1, "sg_w_s": None,
    "sg_b_s": None, "sg_w_out": 1, "sg_b_out": 1, "hg_w_in": 2, "hg_lb_table": None, "hg_norm": None, "hg_w_out": 1,
    "ffn_w_up": 2, "ffn_w_dw": 2, "ffn_w_down": 1,
}
_MATMUL_WEIGHTS = ("gla_w_in", "gla_w_out", "cv_w_in", "cv_w_out", "sg_w_in", "sg_w_out", "hg_w_in", "hg_w_out",
                   "ffn_w_up", "ffn_w_down")
_NAMES = tuple(_SHARD_AXIS)


def kernel(x, norm_mix, norm_ffn, norm_final, gla_w_in, gla_w_g2, gla_b_g2, gla_norm, gla_w_out, cv_w_in, cv_b_in, cv_w_dw, cv_b_dw, cv_ln_g, cv_ln_b, cv_w_out, cv_b_out, sg_w_in, sg_b_in, sg_ln_g, sg_ln_b, sg_w_s, sg_b_s, sg_w_out, sg_b_out, hg_w_in, hg_lb_table, hg_norm, hg_w_out, ffn_w_up, ffn_w_dw, ffn_w_down, loss_target, m_norm_mix, m_norm_ffn, m_norm_final, m_gla_w_in, m_gla_w_g2, m_gla_b_g2, m_gla_norm, m_gla_w_out, m_cv_w_in, m_cv_b_in, m_cv_w_dw, m_cv_b_dw, m_cv_ln_g, m_cv_ln_b, m_cv_w_out, m_cv_b_out, m_sg_w_in, m_sg_b_in, m_sg_ln_g, m_sg_ln_b, m_sg_w_s, m_sg_b_s, m_sg_w_out, m_sg_b_out, m_hg_w_in, m_hg_lb_table, m_hg_norm, m_hg_w_out, m_ffn_w_up, m_ffn_w_dw, m_ffn_w_down, v_norm_mix, v_norm_ffn, v_norm_final, v_gla_w_in, v_gla_w_g2, v_gla_b_g2, v_gla_norm, v_gla_w_out, v_cv_w_in, v_cv_b_in, v_cv_w_dw, v_cv_b_dw, v_cv_ln_g, v_cv_ln_b, v_cv_w_out, v_cv_b_out, v_sg_w_in, v_sg_b_in, v_sg_ln_g, v_sg_ln_b, v_sg_w_s, v_sg_b_s, v_sg_w_out, v_sg_b_out, v_hg_w_in, v_hg_lb_table, v_hg_norm, v_hg_w_out, v_ffn_w_up, v_ffn_w_dw, v_ffn_w_down):
    local = dict(locals())
    wts = {n: local[n] for n in _NAMES}
    mom = {n: local["m_" + n] for n in _NAMES}
    var = {n: local["v_" + n] for n in _NAMES}
    small_all = [n for n in _NAMES if n not in _MATMUL_WEIGHTS]
    small_sharded = [n for n in small_all if _SHARD_AXIS[n] is not None]
    bsz, seq, d = x.shape
    depth = norm_mix.shape[0]

    stages = {}
    for layer in range(depth):
        kind = _MIXERS[layer % 4]
        stages[kind, layer] = {"w_in": (kind + "_w_in", layer // 4), "w_out": (kind + "_w_out", layer // 4)}
        stages["ffn", layer] = {"w_up": ("ffn_w_up", layer), "w_down": ("ffn_w_down", layer)}

    order = list(stages)
    shards = lambda stage: [wts[nm][idx].astype(BF16) for nm, idx in stages[stage].values()]
    gathers = {order[0]: [_Side(shards(order[0]), False)]}
    gathers[order[0]][0].lands = _gather_stage(gathers[order[0]][0].srcs, name="gather_first")
    scatters, waiting, down_gathers, down_scatters = {}, [], {}, {}

    def ride(kind, layer, forward):
        if not forward:
            return waiting.pop() if waiting else None
        at = order.index((kind, layer)) + 1
        if at == len(order):
            return None
        srcs = shards(order[at])
        if order[at][0] == "ffn":
            down_gathers[order[at][1]] = _Side(srcs[1:], False)
            srcs = srcs[:1]
            if kind != "cv":
                half = srcs[0].shape[0] // 2
                gathers[order[at]] = [_Side([srcs[0][:half]], False), _Side([srcs[0][half:]], False)]
                return gathers[order[at]]
        gathers[order[at]] = [_Side(srcs, False)]
        return gathers[order[at]][0]

    lay_sw, r_sw = _layout({n: wts[n].shape for n in small_sharded}, 8, 8)
    got_sw = _all_gather(_pack(wts, lay_sw, r_sw, F32), name="gather_small_weights")
    parts = _unpack(got_sw.reshape(N_DEV, r_sw, LANES), lay_sw, (N_DEV,))
    full_small = {n: wts[n] for n in small_all if _SHARD_AXIS[n] is None}
    for n in small_sharded:
        ax, shape = _SHARD_AXIS[n], wts[n].shape
        full_small[n] = jnp.moveaxis(parts[n], 0, ax).reshape(shape[:ax] + (N_DEV * shape[ax],) + shape[ax + 1:])

    def full_size(nm, land):
        _, r, c = land.shape
        if _SHARD_AXIS[nm] == 2:
            return land.transpose(1, 0, 2).reshape(r, N_DEV * c), land.transpose(0, 2, 1).reshape(N_DEV * c, r)
        return land.reshape(N_DEV * r, c), land.reshape(N_DEV * r, c).T

    def get_big(kind, layer):
        names = [nm for nm, _ in stages[kind, layer].values()]
        lands = [land for side in gathers[kind, layer] for land in side.lands]
        if kind != "ffn" or layer not in down_gathers:
            return _oriented(kind, {key: full_size(nm, land) for key, nm, land in zip(stages[kind, layer], names, lands)})
        parts = [full_size(names[0], land) for land in lands]
        up = parts[0][0] if len(parts) == 1 else jnp.concatenate([p[0] for p in parts], axis=0)
        up_t = parts[0][1] if len(parts) == 1 else jnp.concatenate([p[1] for p in parts], axis=1)
        f = up.shape[1] // 2

        def down_landed():
            down, down_t = full_size(names[1], down_gathers[layer].lands[0])
            return dict(w_down=down, w_down_t=down_t)

        return dict(w_up=up, w_up_t_gate=up_t[:f], w_up_t_val=up_t[f:], late=(down_gathers[layer], down_landed))

    def put_big(kind, layer, g):
        if kind == "ffn":
            k, f = g["w_up_gate"].shape
            halves = [g[key].reshape(k, N_DEV // 2, 2 * f // N_DEV) for key in ("w_up_gate", "w_up_val")]
            w_in = jnp.concatenate(halves, axis=1)
        else:
            w_in = jnp.concatenate([g["w_main"], g["w_glr"][:, :GLA_RANK]], axis=1) if kind == "gla" else g["w_in"]
            w_in = w_in.reshape(w_in.shape[0], N_DEV, w_in.shape[1] // N_DEV)
        sends = [w_in.transpose(1, 0, 2).astype(BF16)]
        if kind != "ffn":
            sends.append(row_slots(g["w_out"]))
        scatters[kind, layer] = _Side(sends, True)
        waiting.append(scatters[kind, layer])

    def row_slots(grad):
        return grad.reshape(N_DEV, grad.shape[0] // N_DEV, grad.shape[1]).astype(BF16)

    def put_early(layer, grad_w_down):
        down_scatters[layer] = _Side([row_slots(grad_w_down)], True)
        return down_scatters[layer]

    loss, dx, grads = _local_step(x.reshape(bsz * seq, d), loss_target.reshape(bsz * seq, d), _prep_small(full_small), seq,
                                  get_big, put_big, ride, put_early)
    loss = lax.psum(loss[0, 0], ("x", "y", "c"))

    gs = _small_grads(grads)
    small_repl = [n for n in small_all if _SHARD_AXIS[n] is None]
    lay_re, r_re = _layout({n: wts[n].shape for n in small_repl}, 8, 8)
    slots = {}
    for n in small_sharded:
        ax, shape = _SHARD_AXIS[n], wts[n].shape
        slots[n] = jnp.moveaxis(gs[n].reshape(shape[:ax] + (N_DEV, shape[ax]) + shape[ax + 1:]), ax, 0)
    sent = jnp.concatenate([_pack(gs, lay_re, r_re, F32), _pack(slots, lay_sw, r_sw, F32, (N_DEV,)).reshape(-1, LANES)])
    sum_re, sum_sh = _sum_small(_all_gather(sent, name="gather_small_grads"), r_re, r_sw, name="sum_small_grads")
    g_own = _unpack(sum_re, lay_re)
    g_own.update(_unpack(sum_sh, lay_sw))
    two_d = lambda a: a.reshape(-1, a.shape[-1])
    upd = _adamw_small(*[[two_d(src[n]) for n in small_all] for src in (g_own, wts, mom, var)], name="adamw_small")
    results = {n: [g_own[n]] + [part[i].reshape(wts[n].shape) for part in upd] for i, n in enumerate(small_all)}

    per_layer = {}
    for (kind, layer), side in scatters.items():
        lands = side.lands
        if kind == "ffn":
            lands = list(lands) + down_scatters[layer].lands
        for (nm, idx), land in zip(stages[kind, layer].values(), lands):
            three_d = lambda a: a.reshape((a.shape[0],) + land.shape[1:])
            per_layer.setdefault(nm, {})[idx] = _adamw_big(land, three_d(wts[nm]), three_d(mom[nm]), three_d(var[nm]), idx,
                                                           name="adamw_" + nm)
    for nm, by_idx in per_layer.items():
        outs = [by_idx[i] for i in range(len(by_idx))]
        results[nm] = [(outs[0][j] if len(outs) == 1 else jnp.stack([o[j] for o in outs])).reshape(wts[nm].shape)
                       for j in range(4)]
    out = [loss, dx.reshape(bsz, seq, d)]
    for j in range(4):
        out += [results[n][j] for n in _NAMES]
    return tuple(out)
```

```python
import functools
import math

import jax
import jax.numpy as jnp
import numpy as np
from jax import lax
from jax.experimental import pallas as pl
from jax.experimental.pallas import tpu as pltpu

F32 = jnp.float32
BF16 = jnp.bfloat16
EPS = 1e-6
N_DEV = 8
LANES = 128
SUBLANES_BF16 = 16
HALO = 32
GLA_HEADS, GLA_RANK, GLA_GATE_NORM, GLA_CHUNK = 4, 16, 16.0, 64
SGU_CHUNK, SGU_GROUPS = 128, 8
HGRN_EXPAND, HGRN_CHUNK = 128, 64
FFN_CONV_WIDTH = 3
ADAM_LR, ADAM_B1, ADAM_B2, ADAM_EPS, ADAM_WD, ADAM_STEP = 0.001, 0.9, 0.999, 1e-08, 0.01, 10
MESH = pl.DeviceIdType.MESH


def _sigmoid(x):
    return 0.5 * (jnp.tanh(0.5 * x) + 1.0)


def _silu(x):
    return x * _sigmoid(x)


def _log_sigmoid(x):
    return jnp.minimum(x, 0.0) - jnp.log(1.0 + jnp.exp(-jnp.abs(x)))


def _gelu(x):
    return 0.5 * x * (1.0 + jnp.tanh(math.sqrt(2.0 / math.pi) * (x + 0.044715 * (x * x * x))))


def _rms(x, g):
    return x * lax.rsqrt(jnp.mean(x * x, axis=-1, keepdims=True) + EPS) * g


def _layer_norm(x, g, b):
    xc = x - jnp.mean(x, axis=-1, keepdims=True)
    return xc * lax.rsqrt(jnp.mean(xc * xc, axis=-1, keepdims=True) + EPS) * g + b


def _dot_raw(a, b, dims):
    return lax.dot_general(a.astype(BF16), b.astype(BF16), (dims, ((), ())), preferred_element_type=F32)


@jax.custom_vjp
def _bdot(a, b):
    return _dot_raw(a, b, ((1,), (0,)))


@jax.custom_vjp
def _bdot_nt(a, b):
    return _dot_raw(a, b, ((1,), (1,)))


@jax.custom_vjp
def _bdot_tn(a, b):
    return _dot_raw(a, b, ((0,), (0,)))


_bdot.defvjp(lambda a, b: (_bdot(a, b), (a, b)), lambda r, g: (_bdot_nt(g, r[1]), _bdot_tn(r[0], g)))
_bdot_nt.defvjp(lambda a, b: (_bdot_nt(a, b), (a, b)), lambda r, g: (_bdot(g, r[1]), _bdot_tn(g, r[0])))
_bdot_tn.defvjp(lambda a, b: (_bdot_tn(a, b), (a, b)), lambda r, g: (_bdot_nt(r[1], g), _bdot(r[0], g)))


def _scan_rows(x, reverse):
    n = x.shape[0]
    row = lax.broadcasted_iota(jnp.int32, x.shape, 0)
    step = 1
    while step < n:
        if reverse:
            x = x + jnp.where(row < n - step, pltpu.roll(x, n - step, 0), 0.0)
        else:
            x = x + jnp.where(row >= step, pltpu.roll(x, step, 0), 0.0)
        step *= 2
    return x


@jax.custom_vjp
def _cumsum_rows(x):
    return _scan_rows(x, False)


_cumsum_rows.defvjp(lambda x: (_scan_rows(x, False), None), lambda _, g: (_scan_rows(g, True),))


def _divisor_tile(n, cap, unit):
    if n <= cap:
        return n
    best = None
    for t in range(unit, cap + 1, unit):
        if n % t == 0:
            best = t
    assert best is not None, (n, cap, unit)
    return best


def _const_map(nd):
    return lambda *_: (0,) * nd


class _Side:
    def __init__(self, srcs, scatter, parts=()):
        self.srcs, self.scatter, self.lands, self.parts = list(srcs), scatter, None, list(parts)

    @staticmethod
    def join(sides):
        sides = [s for s in sides if s is not None]
        if len(sides) < 2:
            return sides[0] if sides else None
        assert len({s.scatter for s in sides}) == 1
        return _Side([a for s in sides for a in s.srcs], sides[0].scatter, sides)

    def landed(self, lands):
        self.lands = list(lands)
        at = 0
        for part in self.parts:
            part.landed(self.lands[at:at + len(part.srcs)])
            at += len(part.srcs)


def _pallas(body, side, *, name, grid, in_specs, out_specs, out_shape, scratch_shapes=(), semantics):
    if side is None:
        return pl.pallas_call(body, name=name, grid=grid, in_specs=in_specs, out_specs=out_specs, out_shape=out_shape,
                              scratch_shapes=list(scratch_shapes),
                              compiler_params=pltpu.CompilerParams(dimension_semantics=semantics))
    single = not isinstance(out_shape, (list, tuple))
    out_specs, out_shape = ([out_specs], [out_shape]) if single else (list(out_specs), list(out_shape))
    n, n_in, n_out, n_scr = len(side.srcs), len(in_specs), len(out_shape), len(scratch_shapes)
    lands = [jax.ShapeDtypeStruct((N_DEV,) + (s.shape[1:] if side.scatter else s.shape), s.dtype) for s in side.srcs]

    def body2(*refs):
        x_refs, land_refs = refs[n_in:n_in + n], refs[n_in + n + n_out:n_in + 2 * n + n_out]
        send_sems, recv_sems, local_sems = refs[-3:]
        steps = [pl.program_id(a) for a in range(len(grid))]
        first = functools.reduce(jnp.logical_and, [s == 0 for s in steps])
        last = functools.reduce(jnp.logical_and, [s == g - 1 for s, g in zip(steps, grid)])

        def copies():
            mx, my, mc = lax.axis_index("x"), lax.axis_index("y"), lax.axis_index("c")
            me = 4 * mx + 2 * my + mc
            mine = [pltpu.make_async_copy(x_refs[i].at[me] if side.scatter else x_refs[i], land_refs[i].at[me],
                                          local_sems.at[i]) for i in range(n)]
            sends, recvs = [], []
            for k in range(1, N_DEV):
                px = 1 - mx if k & 4 else mx
                py = 1 - my if k & 2 else my
                pc = 1 - mc if k & 1 else mc
                peer = 4 * px + 2 * py + pc
                for i in range(n):
                    sems = dict(send_sem=send_sems.at[7 * i + k - 1], recv_sem=recv_sems.at[7 * i + k - 1],
                                device_id=(px, py, pc), device_id_type=MESH)
                    src = x_refs[i].at[peer] if side.scatter else x_refs[i]
                    sends.append(pltpu.make_async_remote_copy(src_ref=src, dst_ref=land_refs[i].at[me], **sems))
                    recvs.append(pltpu.make_async_remote_copy(src_ref=src, dst_ref=land_refs[i].at[peer], **sems))
            return mine, sends, recvs

        @pl.when(first)
        def _():
            mine, sends, _ = copies()
            for cp in mine + sends:
                cp.start()

        body(*refs[:n_in], *refs[n_in + n:n_in + n + n_out], *refs[n_in + 2 * n + n_out:n_in + 2 * n + n_out + n_scr])

        @pl.when(last)
        def _():
            mine, sends, recvs = copies()
            for cp in recvs:
                cp.wait_recv()
            for cp in sends:
                cp.wait_send()
            for cp in mine:
                cp.wait()

    any_space = pl.BlockSpec(memory_space=pl.ANY)
    call = pl.pallas_call(
        body2, name=name, grid=grid, in_specs=list(in_specs) + [any_space] * n, out_specs=out_specs + [any_space] * n,
        out_shape=out_shape + lands,
        scratch_shapes=list(scratch_shapes) + [pltpu.SemaphoreType.DMA((7 * n,)), pltpu.SemaphoreType.DMA((7 * n,)),
                                               pltpu.SemaphoreType.DMA((n,))],
        compiler_params=pltpu.CompilerParams(dimension_semantics=("arbitrary",) * len(grid)))

    def run(*args):
        res = call(*args, *side.srcs)
        side.landed(res[n_out:])
        return res[0] if single else res[:n_out]

    return run


def _mm(a, b, *, add=None, bias=None, out_dtype=F32, name, side=None):
    m, k = a.shape
    k2, n = b.shape
    assert k == k2
    tn = _divisor_tile(n, max(LANES, min(1408, (6 << 20) // (2 * k) // LANES * LANES)), LANES)
    tm = _divisor_tile(m, max(256, min(1024, (4 << 20) // (a.dtype.itemsize * k) // 256 * 256)), 8)
    has_bias, has_add = bias is not None, add is not None

    def body(*refs):
        a_ref, b_ref = refs[0], refs[1]
        o_ref = refs[-1]
        acc = jnp.dot(a_ref[...].astype(BF16), b_ref[...], preferred_element_type=F32)
        pos = 2
        if has_bias:
            acc = acc + refs[pos][...]
            pos += 1
        if has_add:
            acc = acc + refs[pos][...].astype(F32)
        o_ref[...] = acc.astype(o_ref.dtype)

    in_specs = [pl.BlockSpec((tm, k), lambda i, j: (i, 0)), pl.BlockSpec((k, tn), lambda i, j: (0, j))]
    args = [a, b]
    if has_bias:
        in_specs.append(pl.BlockSpec((1, tn), lambda i, j: (0, j)))
        args.append(bias)
    if has_add:
        in_specs.append(pl.BlockSpec((tm, tn), lambda i, j: (i, j)))
        args.append(add)
    return _pallas(
        body, side, name=name, grid=(m // tm, n // tn), in_specs=in_specs,
        out_specs=pl.BlockSpec((tm, tn), lambda i, j: (i, j)),
        out_shape=jax.ShapeDtypeStruct((m, n), out_dtype),
        semantics=("parallel", "parallel"),
    )(*args)


def _mm_tn(a, g, *, name):
    m, k = a.shape
    m2, n = g.shape
    assert m == m2
    tk = _divisor_tile(k, 1408, LANES)
    tn = _divisor_tile(n, 1408, LANES)
    tm = _divisor_tile(m, 1024, 8)

    def body(a_ref, g_ref, o_ref):
        @pl.when(pl.program_id(2) == 0)
        def _():
            o_ref[...] = jnp.zeros_like(o_ref)

        o_ref[...] += _dot_raw(a_ref[...], g_ref[...], ((0,), (0,)))

    return pl.pallas_call(
        body, name=name, grid=(k // tk, n // tn, m // tm),
        in_specs=[pl.BlockSpec((tm, tk), lambda i, j, t: (t, i)), pl.BlockSpec((tm, tn), lambda i, j, t: (t, j))],
        out_specs=pl.BlockSpec((tk, tn), lambda i, j, t: (i, j)),
        out_shape=jax.ShapeDtypeStruct((k, n), F32),
        compiler_params=pltpu.CompilerParams(dimension_semantics=("parallel", "parallel", "arbitrary")),
    )(a, g)


def _tile_call(name, fn, tiled, params, out_tiled, out_acc, tile, side=None):
    tiled = [t if isinstance(t, tuple) else (t, t.shape[1], 0) for t in tiled]
    t_rows = tiled[0][0].shape[0]
    tile = min(tile, t_rows)
    assert t_rows % tile == 0
    n_t, n_p, n_o = len(tiled), len(params), len(out_tiled)

    def body(*refs):
        vals = [r[...] for r in refs[: n_t + n_p]]
        touts, aouts = fn(*vals)
        for r, v in zip(refs[n_t + n_p: n_t + n_p + n_o], touts):
            r[...] = v.astype(r.dtype)
        acc_refs = refs[n_t + n_p + n_o:]
        if acc_refs:
            @pl.when(pl.program_id(0) == 0)
            def _():
                for r in acc_refs:
                    r[...] = jnp.zeros_like(r)

            for r, v in zip(acc_refs, aouts):
                r[...] += v

    in_specs = [pl.BlockSpec((tile, w), lambda i, cb=cb: (i, cb)) for _, w, cb in tiled]
    in_specs += [pl.BlockSpec(p.shape, _const_map(p.ndim)) for p in params]
    out_specs = [pl.BlockSpec((tile, w), lambda i: (i, 0)) for w, _ in out_tiled]
    out_specs += [pl.BlockSpec(s, _const_map(len(s))) for s in out_acc]
    out_shape = [jax.ShapeDtypeStruct((t_rows, w), dt) for w, dt in out_tiled]
    out_shape += [jax.ShapeDtypeStruct(s, F32) for s in out_acc]
    res = _pallas(
        body, side, name=name, grid=(t_rows // tile,), in_specs=in_specs, out_specs=out_specs, out_shape=out_shape,
        semantics=("arbitrary" if out_acc else "parallel",),
    )(*[t[0] for t in tiled], *params)
    return res[:n_o], res[n_o:]


def _rms_fwd(x, g, name):
    (h,), _ = _tile_call(name, lambda xv, gv: ([_rms(xv, gv)], []), [x], [g], [(x.shape[1], BF16)], [], 512)
    return h


def _rms_bwd(x, g, dh, dres, name, side=None):
    def fn(xv, dhv, drv, gv):
        _, vjp = jax.vjp(_rms, xv, gv)
        dx, dg = vjp(dhv.astype(F32))
        return [drv + dx, drv + dx], [dg]

    (dx, dxb), (dg,) = _tile_call(name, fn, [x, dh, dres], [g], [(x.shape[1], F32), (x.shape[1], BF16)], [g.shape], 512,
                                  side)
    return dx, dxb, dg


def _colsum(x, name):
    _, (s,) = _tile_call(name, lambda xv: ([], [jnp.sum(xv.astype(F32), axis=0, keepdims=True)]), [x], [], [],
                         [(1, x.shape[1])], 512)
    return s


def _seq_flags(i, tiles_per_seq):
    pos = i % tiles_per_seq
    return pos == 0, pos == tiles_per_seq - 1


def _dwconv_fwd(x, w, b, seq, name, side=None):
    t_rows, ch = x.shape
    kw = w.shape[0]
    tile = min(512, seq)
    cb = _divisor_tile(ch, 256, LANES)
    tps, hb = seq // tile, tile // HALO

    def body(x_ref, halo_ref, w_ref, b_ref, y_ref, pad_ref):
        first, _ = _seq_flags(pl.program_id(0), tps)
        pad_ref[0:HALO, :] = jnp.where(first, 0.0, halo_ref[...])
        pad_ref[HALO:HALO + tile, :] = x_ref[...]
        for r0 in range(0, tile, HALO):
            acc = jnp.broadcast_to(b_ref[...], (HALO, cb))
            for k in range(kw):
                acc = acc + pad_ref[pl.ds(HALO - (kw - 1) + k + r0, HALO), :] * w_ref[k:k + 1, :]
            y_ref[pl.ds(r0, HALO), :] = acc

    return _pallas(
        body, side, name=name, grid=(t_rows // tile, ch // cb),
        in_specs=[pl.BlockSpec((tile, cb), lambda i, j: (i, j)),
                  pl.BlockSpec((HALO, cb), lambda i, j: (jnp.maximum(i * hb - 1, 0), j)),
                  pl.BlockSpec((kw, cb), lambda i, j: (0, j)), pl.BlockSpec((1, cb), lambda i, j: (0, j))],
        out_specs=pl.BlockSpec((tile, cb), lambda i, j: (i, j)),
        out_shape=jax.ShapeDtypeStruct((t_rows, ch), F32),
        scratch_shapes=[pltpu.VMEM((HALO + tile, cb), F32)],
        semantics=("parallel", "parallel"),
    )(x, x, w, b)


def _dwconv_bwd(x, dy, w, seq, name, side=None):
    t_rows, ch = x.shape
    kw = w.shape[0]
    tile = min(512, seq)
    cb = _divisor_tile(ch, 256, LANES)
    tps, hb, n_hb = seq // tile, tile // HALO, t_rows // HALO

    def body(x_ref, xh_ref, dy_ref, dyh_ref, w_ref, dx_ref, dw_ref, db_ref, xpad, dypad, sums):
        i = pl.program_id(1)
        first, last = _seq_flags(i, tps)

        @pl.when(i == 0)
        def _():
            sums[...] = jnp.zeros_like(sums)

        xpad[0:HALO, :] = jnp.where(first, 0.0, xh_ref[...])
        xpad[HALO:HALO + tile, :] = x_ref[...]
        dypad[0:tile, :] = dy_ref[...]
        dypad[tile:tile + HALO, :] = jnp.where(last, 0.0, dyh_ref[...])
        fold = lambda v: functools.reduce(jnp.add, [v[r:r + 8] for r in range(0, HALO, 8)])
        for r0 in range(0, tile, HALO):
            dyc = dy_ref[pl.ds(r0, HALO), :]
            acc = jnp.zeros((HALO, cb), F32)
            for k in range(kw):
                acc = acc + dypad[pl.ds(kw - 1 - k + r0, HALO), :] * w_ref[k:k + 1, :]
                sums[8 * k:8 * k + 8, :] += fold(dyc * xpad[pl.ds(HALO - (kw - 1) + k + r0, HALO), :])
            dx_ref[pl.ds(r0, HALO), :] = acc
            sums[8 * kw:8 * kw + 8, :] += fold(dyc)

        @pl.when(i == t_rows // tile - 1)
        def _():
            for k in range(kw):
                dw_ref[k:k + 1, :] = jnp.sum(sums[8 * k:8 * k + 8, :], axis=0, keepdims=True)
            db_ref[...] = jnp.sum(sums[8 * kw:8 * kw + 8, :], axis=0, keepdims=True)

    return _pallas(
        body, side, name=name, grid=(ch // cb, t_rows // tile),
        in_specs=[pl.BlockSpec((tile, cb), lambda j, i: (i, j)),
                  pl.BlockSpec((HALO, cb), lambda j, i: (jnp.maximum(i * hb - 1, 0), j)),
                  pl.BlockSpec((tile, cb), lambda j, i: (i, j)),
                  pl.BlockSpec((HALO, cb), lambda j, i: (jnp.minimum((i + 1) * hb, n_hb - 1), j)),
                  pl.BlockSpec((kw, cb), lambda j, i: (0, j))],
        out_specs=[pl.BlockSpec((tile, cb), lambda j, i: (i, j)), pl.BlockSpec((kw, cb), lambda j, i: (0, j)),
                   pl.BlockSpec((1, cb), lambda j, i: (0, j))],
        out_shape=[jax.ShapeDtypeStruct((t_rows, ch), F32), jax.ShapeDtypeStruct((kw, ch), F32),
                   jax.ShapeDtypeStruct((1, ch), F32)],
        scratch_shapes=[pltpu.VMEM((HALO + tile, cb), F32), pltpu.VMEM((tile + HALO, cb), F32),
                        pltpu.VMEM((8 * (kw + 1), cb), F32)],
        semantics=("parallel", "arbitrary"),
    )(x, x, dy, dy, w)


_ROWS = SUBLANES_BF16


def _lane_chunks(width, cap=6 * LANES):
    return [slice(c0, min(c0 + cap, width)) for c0 in range(0, width, cap)]


def _tap_rows(w_ref, cols):
    return [w_ref[k:k + 1, cols] for k in range(FFN_CONV_WIDTH)]


def _conv3_at(pad, taps, row, cols):
    z = pad[pl.ds(row, _ROWS), cols] * taps[2]
    z = z + pad[pl.ds(row - 1, _ROWS), cols] * taps[1]
    return z + pad[pl.ds(row - 2, _ROWS), cols] * taps[0]


def _ffn_mid_fwd(u, w, seq, name, side=None):
    t_rows, f2 = u.shape
    f = f2 // 2
    tile = min(256, seq)
    cb = _divisor_tile(f, 1408, LANES)
    nj, tps, hb, hl = f // cb, seq // tile, tile // SUBLANES_BF16, SUBLANES_BF16

    def body(ug_ref, uv_ref, hg_ref, hv_ref, wg_ref, wv_ref, a_ref, gpad, vpad):
        first, _ = _seq_flags(pl.program_id(0), tps)
        for t_ref, h_ref, pad in ((ug_ref, hg_ref, gpad), (uv_ref, hv_ref, vpad)):
            pad[0:hl, :] = jnp.where(first, 0.0, h_ref[...].astype(F32))
            pad[hl:hl + tile, :] = t_ref[...].astype(F32)
        for cols in _lane_chunks(cb):
            wg, wv = _tap_rows(wg_ref, cols), _tap_rows(wv_ref, cols)
            for r0 in range(0, tile, _ROWS):
                zg = _conv3_at(gpad, wg, hl + r0, cols)
                zv = _conv3_at(vpad, wv, hl + r0, cols)
                half = 0.5 * zg
                a_ref[pl.ds(r0, _ROWS), cols] = ((jnp.tanh(half) + 1.0) * half * zv).astype(a_ref.dtype)

    halo_map = lambda off: (lambda i, j: (jnp.maximum(i * hb - 1, 0), j + off))
    return _pallas(
        body, side, name=name, grid=(t_rows // tile, nj),
        in_specs=[pl.BlockSpec((tile, cb), lambda i, j: (i, j)), pl.BlockSpec((tile, cb), lambda i, j: (i, j + nj)),
                  pl.BlockSpec((hl, cb), halo_map(0)), pl.BlockSpec((hl, cb), halo_map(nj)),
                  pl.BlockSpec((3, cb), lambda i, j: (0, j)), pl.BlockSpec((3, cb), lambda i, j: (0, j + nj))],
        out_specs=pl.BlockSpec((tile, cb), lambda i, j: (i, j)),
        out_shape=jax.ShapeDtypeStruct((t_rows, f), BF16),
        scratch_shapes=[pltpu.VMEM((hl + tile, cb), F32), pltpu.VMEM((hl + tile, cb), F32)],
        semantics=("parallel", "parallel"),
    )(u, u, u, u, w, w)


def _ffn_mid_bwd(u, da, w, seq, name, side=None):
    t_rows, f2 = u.shape
    f = f2 // 2
    tile = min(256, seq)
    cb = _divisor_tile(f, 1408, LANES)
    hl = SUBLANES_BF16
    nj, tps, hb, n_hb, ext = f // cb, seq // tile, tile // hl, t_rows // hl, tile + hl

    def body(ug_ref, uv_ref, pg_ref, pv_ref, ng_ref, nv_ref, da_ref, dan_ref, wg_ref, wv_ref,
             dug_ref, duv_ref, dwg_ref, dwv_ref, gpad, vpad, dzg, dzv):
        i = pl.program_id(1)
        first, last = _seq_flags(i, tps)

        @pl.when(i == 0)
        def _():
            dwg_ref[...] = jnp.zeros_like(dwg_ref)
            dwv_ref[...] = jnp.zeros_like(dwv_ref)

        for t_ref, p_ref, n_ref, pad in ((ug_ref, pg_ref, ng_ref, gpad), (uv_ref, pv_ref, nv_ref, vpad)):
            pad[0:hl, :] = jnp.where(first, 0.0, p_ref[...].astype(F32))
            pad[hl:hl + tile, :] = t_ref[...].astype(F32)
            pad[hl + tile:hl + ext, :] = jnp.where(last, 0.0, n_ref[...].astype(F32))
        for cols in _lane_chunks(cb):
            wg, wv = _tap_rows(wg_ref, cols), _tap_rows(wv_ref, cols)
            for r0 in range(0, ext, _ROWS):
                zg = _conv3_at(gpad, wg, hl + r0, cols)
                zv = _conv3_at(vpad, wv, hl + r0, cols)
                if r0 < tile:
                    da = da_ref[pl.ds(r0, _ROWS), cols].astype(F32)
                else:
                    da = jnp.where(last, 0.0, dan_ref[:, cols].astype(F32))
                sg = _sigmoid(zg)
                dzg[pl.ds(r0, _ROWS), cols] = da * zv * (sg * (1.0 + zg * (1.0 - sg)))
                dzv[pl.ds(r0, _ROWS), cols] = da * (zg * sg)
        for dz, w_ref, pad, du_ref, dw_ref in ((dzg, wg_ref, gpad, dug_ref, dwg_ref), (dzv, wv_ref, vpad, duv_ref, dwv_ref)):
            for cols in _lane_chunks(cb):
                taps = _tap_rows(w_ref, cols)
                width = cols.stop - cols.start
                acc = [jnp.zeros((8, width), F32) for _ in range(FFN_CONV_WIDTH)]
                for r0 in range(0, tile, _ROWS):
                    d0 = dz[pl.ds(r0, _ROWS), cols]
                    du = dz[pl.ds(r0 + 2, _ROWS), cols] * taps[0] + dz[pl.ds(r0 + 1, _ROWS), cols] * taps[1] + d0 * taps[2]
                    du_ref[pl.ds(r0, _ROWS), cols] = du.astype(du_ref.dtype)
                    for k in range(FFN_CONV_WIDTH):
                        prod = d0 * pad[pl.ds(hl - 2 + k + r0, _ROWS), cols]
                        acc[k] = acc[k] + prod[0:8] + prod[8:16]
                for k in range(FFN_CONV_WIDTH):
                    dw_ref[k:k + 1, cols] += jnp.sum(acc[k], axis=0, keepdims=True)

    prev_map = lambda off: (lambda j, i: (jnp.maximum(i * hb - 1, 0), j + off))
    next_map = lambda off: (lambda j, i: (jnp.minimum((i + 1) * hb, n_hb - 1), j + off))
    tile_spec = lambda off: pl.BlockSpec((tile, cb), lambda j, i: (i, j + off))
    w_spec = lambda off: pl.BlockSpec((3, cb), lambda j, i: (0, j + off))
    return _pallas(
        body, side, name=name, grid=(nj, t_rows // tile),
        in_specs=[tile_spec(0), tile_spec(nj), pl.BlockSpec((hl, cb), prev_map(0)), pl.BlockSpec((hl, cb), prev_map(nj)),
                  pl.BlockSpec((hl, cb), next_map(0)), pl.BlockSpec((hl, cb), next_map(nj)),
                  tile_spec(0), pl.BlockSpec((hl, cb), next_map(0)), w_spec(0), w_spec(nj)],
        out_specs=[tile_spec(0), tile_spec(0), w_spec(0), w_spec(0)],
        out_shape=[jax.ShapeDtypeStruct((t_rows, f), BF16), jax.ShapeDtypeStruct((t_rows, f), BF16),
                   jax.ShapeDtypeStruct((3, f), F32), jax.ShapeDtypeStruct((3, f), F32)],
        scratch_shapes=[pltpu.VMEM((hl + ext, cb), F32), pltpu.VMEM((hl + ext, cb), F32),
                        pltpu.VMEM((ext, cb), F32), pltpu.VMEM((ext, cb), F32)],
        semantics=("parallel", "arbitrary"),
    )(u, u, u, u, u, u, da, da, w, w)


def _gla_chunk(q, k, v, lg, st, *, scale, chunk):
    row = lax.broadcasted_iota(jnp.int32, (chunk, chunk), 0)
    col = lax.broadcasted_iota(jnp.int32, (chunk, chunk), 1)
    causal = col <= row
    b = _cumsum_rows(lg)
    upto_mid = lax.broadcasted_iota(jnp.int32, lg.shape, 0) <= chunk // 2
    b_mid = jnp.sum(jnp.where(upto_mid, lg, 0.0), axis=0, keepdims=True)
    b_last = jnp.sum(lg, axis=0, keepdims=True)
    qs = q * scale
    scores = _bdot_nt(qs * jnp.exp(b - b_mid), k * jnp.exp(b_mid - b))
    o = _bdot(jnp.where(causal, scores, 0.0), v)
    o = o + _bdot_nt(qs * jnp.exp(b), st)
    st_new = st * jnp.exp(b_last) + _bdot_tn(v, k * jnp.exp(b_last - b))
    return o, st_new


_CHUNKS_PER_STEP = 4


def _gla_specs(specs, rows, n_blocks, reverse):
    if reverse:
        row = lambda bi, ci: bi * n_blocks + (n_blocks - 1 - ci)
    else:
        row = lambda bi, ci: bi * n_blocks + ci
    return [pl.BlockSpec((rows, w), lambda bi, ci, cb=cb: (row(bi, ci), cb)) for _, w, cb in specs], row


def _gla_fwd(q, k, v, lg, *, heads, dk, dv, scale, chunk, seq, name, side=None):
    t_rows = q[0].shape[0]
    per = _CHUNKS_PER_STEP if (seq // chunk) % _CHUNKS_PER_STEP == 0 else 1
    n_blocks = seq // (per * chunk)
    fn = functools.partial(_gla_chunk, scale=scale, chunk=chunk)

    def body(q_ref, k_ref, v_ref, lg_ref, o_ref, sts_ref, st_ref):
        @pl.when(pl.program_id(1) == 0)
        def _():
            st_ref[...] = jnp.zeros_like(st_ref)

        ks = [slice(h * dk, (h + 1) * dk) for h in range(heads)]
        vs = [slice(h * dv, (h + 1) * dv) for h in range(heads)]
        st = [st_ref[vs[h], :] for h in range(heads)]
        for s in range(per):
            rows = pl.ds(s * chunk, chunk)
            for h in range(heads):
                sts_ref[s, vs[h], :] = st[h]
                o, st[h] = fn(q_ref[rows, ks[h]].astype(F32), k_ref[rows, ks[h]].astype(F32),
                              v_ref[rows, vs[h]].astype(F32), lg_ref[rows, ks[h]], st[h])
                o_ref[rows, vs[h]] = o
        for h in range(heads):
            st_ref[vs[h], :] = st[h]

    in_specs, row = _gla_specs([q, k, v, lg], per * chunk, n_blocks, False)
    return _pallas(
        body, side, name=name, grid=(t_rows // seq, n_blocks), in_specs=in_specs,
        out_specs=[pl.BlockSpec((per * chunk, heads * dv), lambda bi, ci: (row(bi, ci), 0)),
                   pl.BlockSpec((per, heads * dv, dk), lambda bi, ci: (row(bi, ci), 0, 0))],
        out_shape=[jax.ShapeDtypeStruct((t_rows, heads * dv), F32),
                   jax.ShapeDtypeStruct((t_rows // chunk, heads * dv, dk), F32)],
        scratch_shapes=[pltpu.VMEM((heads * dv, dk), F32)],
        semantics=("arbitrary", "arbitrary"),
    )(q[0], k[0], v[0], lg[0])


def _gla_bwd(q, k, v, lg, states, do, *, heads, dk, dv, scale, chunk, seq, out_dtypes, name, side=None):
    t_rows = q[0].shape[0]
    per = _CHUNKS_PER_STEP if (seq // chunk) % _CHUNKS_PER_STEP == 0 else 1
    n_blocks = seq // (per * chunk)
    fn = functools.partial(_gla_chunk, scale=scale, chunk=chunk)

    def body(q_ref, k_ref, v_ref, lg_ref, do_ref, sts_ref, dq_ref, dk_ref, dv_ref, dlg_ref, dst_ref):
        @pl.when(pl.program_id(1) == 0)
        def _():
            dst_ref[...] = jnp.zeros_like(dst_ref)

        ks = [slice(h * dk, (h + 1) * dk) for h in range(heads)]
        vs = [slice(h * dv, (h + 1) * dv) for h in range(heads)]
        dst = [dst_ref[vs[h], :] for h in range(heads)]
        for s in reversed(range(per)):
            rows = pl.ds(s * chunk, chunk)
            for h in range(heads):
                _, vjp = jax.vjp(fn, q_ref[rows, ks[h]].astype(F32), k_ref[rows, ks[h]].astype(F32),
                                 v_ref[rows, vs[h]].astype(F32), lg_ref[rows, ks[h]], sts_ref[s, vs[h], :])
                dq, dkk, dvv, dlg, dst[h] = vjp((do_ref[rows, vs[h]].astype(F32), dst[h]))
                dq_ref[rows, ks[h]] = dq.astype(dq_ref.dtype)
                dk_ref[rows, ks[h]] = dkk.astype(dk_ref.dtype)
                dv_ref[rows, vs[h]] = dvv.astype(dv_ref.dtype)
                dlg_ref[rows, ks[h]] = dlg
        for h in range(heads):
            dst_ref[vs[h], :] = dst[h]

    do_view = (do, heads * dv, 0)
    in_specs, row = _gla_specs([q, k, v, lg, do_view], per * chunk, n_blocks, True)
    in_specs.append(pl.BlockSpec((per, heads * dv, dk), lambda bi, ci: (row(bi, ci), 0, 0)))
    wide = lambda w: pl.BlockSpec((per * chunk, w), lambda bi, ci: (row(bi, ci), 0))
    return _pallas(
        body, side, name=name, grid=(t_rows // seq, n_blocks), in_specs=in_specs,
        out_specs=[wide(heads * dk), wide(heads * dk), wide(heads * dv), wide(heads * dk)],
        out_shape=[jax.ShapeDtypeStruct((t_rows, heads * dk), out_dtypes[0]),
                   jax.ShapeDtypeStruct((t_rows, heads * dk), out_dtypes[1]),
                   jax.ShapeDtypeStruct((t_rows, heads * dv), out_dtypes[2]),
                   jax.ShapeDtypeStruct((t_rows, heads * dk), F32)],
        scratch_shapes=[pltpu.VMEM((heads * dv, dk), F32)],
        semantics=("arbitrary", "arbitrary"),
    )(q[0], k[0], v[0], lg[0], do, states)


def _head_rms_gate(o, r, g, heads):
    d = o.shape[1] // heads
    parts = [_rms(o[:, h * d:(h + 1) * d], g) for h in range(heads)]
    return jnp.concatenate(parts, axis=1) * _silu(r)


def _gla_gate(glr, w_g2p, b_g2):
    return _log_sigmoid(_bdot(glr, w_g2p) + b_g2) * (1.0 / GLA_GATE_NORM)


def _glu(a, gate, b_in):
    d = a.shape[1]
    return (a + b_in[:, :d]) * _sigmoid(gate + b_in[:, d:])


def _ln_silu(y, g, b):
    return _silu(_layer_norm(y, g, b))


def _sgu(pre, b_in, ln_g, ln_b, w_s, b_st):
    d = pre.shape[1] // 2
    gd = d // SGU_GROUPS
    uv = _gelu(pre + b_in)
    u, v = uv[:, :d], _layer_norm(uv[:, d:], ln_g, ln_b)
    row = lax.broadcasted_iota(jnp.int32, (SGU_CHUNK, SGU_CHUNK), 0)
    col = lax.broadcasted_iota(jnp.int32, (SGU_CHUNK, SGU_CHUNK), 1)
    lane = lax.broadcasted_iota(jnp.int32, b_st.shape, 1)
    rows = []
    for c in range(pre.shape[0] // SGU_CHUNK):
        rs = slice(c * SGU_CHUNK, (c + 1) * SGU_CHUNK)
        parts = []
        for g in range(SGU_GROUPS):
            wg = jnp.where(col <= row, w_s[g], 0.0)
            bias = jnp.sum(jnp.where(lane == g, b_st, 0.0), axis=1, keepdims=True)
            parts.append(_bdot(wg, v[rs, g * gd:(g + 1) * gd]) + bias)
        rows.append(jnp.concatenate(parts, axis=1))
    s = rows[0] if len(rows) == 1 else jnp.concatenate(rows, axis=0)
    return u * s


def _hgrn_pre(q, f, table, layer):
    t = table - jnp.max(table, axis=0, keepdims=True)
    e = jnp.exp(t)
    sm = e / jnp.sum(e, axis=0, keepdims=True)
    rows = lax.broadcasted_iota(jnp.int32, table.shape, 0)
    lb = jnp.sum(jnp.where((rows >= 1) & (rows <= layer), sm, 0.0), axis=0, keepdims=True)
    sf = _sigmoid(f)
    return _silu(q), (1.0 - lb) * (1.0 - sf), jnp.log(lb + (1.0 - lb) * sf)


def _ffn_fwd(x, w, seq, sv):
    sv["h2"] = _rms_fwd(x, w["norm"], "ffn_norm")
    late = w.get("late")
    sv["u"] = _mm(sv["h2"], w["w_up"], out_dtype=BF16, name="ffn_up", side=late[0] if late else None)
    sv["a"] = _ffn_mid_fwd(sv["u"], w["w_dw"], seq, "ffn_mid", sv.pop("side", None))
    if late:
        sv["late_w"] = late[1]()
        w = dict(w, **sv["late_w"])
    return _mm(sv["a"], w["w_down"], add=x, name="ffn_down")


def _ffn_bwd(x, dy, dyb, w, seq, sv):
    g = {}
    da = _mm(dyb, w["w_down_t"], out_dtype=BF16, name="ffn_down_dx")
    g["w_down"] = _mm_tn(sv["a"], dyb, name="ffn_down_dw")
    early = sv.pop("put_early", None)
    side = _Side.join([sv.pop("side", None), early(g["w_down"]) if early else None])
    dug, duv, dwg, dwv = _ffn_mid_bwd(sv["u"], da, w["w_dw"], seq, "ffn_mid_bwd", side)
    g["w_dw"] = jnp.concatenate([dwg, dwv], axis=1)
    g["w_up_gate"] = _mm_tn(sv["h2"], dug, name="ffn_up_dw")
    g["w_up_val"] = _mm_tn(sv["h2"], duv, name="ffn_up_dw")
    dh = _mm(dug, w["w_up_t_gate"], out_dtype=F32, name="ffn_up_dx")
    dh = _mm(duv, w["w_up_t_val"], add=dh, out_dtype=BF16, name="ffn_up_dx2")
    dx, dxb, g["norm"] = _rms_bwd(x, w["norm"], dh, dy, "ffn_norm_bwd")
    return dx, dxb, g


def _gla_layer_fwd(x, h, w, seq, sv):
    d = x.shape[1]
    dkt = d // 2
    dk, dv = dkt // GLA_HEADS, d // GLA_HEADS
    proj = _mm(h, w["w_main"], out_dtype=F32, name="gla_in", side=sv.pop("side_in", None))
    glr = _mm(h, w["w_glr"], out_dtype=BF16, name="gla_in_g")
    (lg,), _ = _tile_call("gla_gate", lambda a, b, c: ([_gla_gate(a.astype(F32), b, c)], []), [glr],
                          [w["w_g2p"], w["b_g2"]], [(dkt, F32)], [], 512)
    q, k, v, r = (proj, dkt, 0), (proj, dkt, 1), (proj, d, 1), (proj, d, 2)
    o, states = _gla_fwd(q, k, v, (lg, dkt, 0), heads=GLA_HEADS, dk=dk, dv=dv, scale=dk ** -0.5, chunk=GLA_CHUNK,
                         seq=seq, name="gla_core", side=sv.pop("side", None))
    (o2,), _ = _tile_call("gla_post", lambda ov, rv, gv: ([_head_rms_gate(ov, rv.astype(F32), gv, GLA_HEADS)], []),
                          [o, r], [w["norm"]], [(d, BF16)], [], 256)
    sv.update(proj=proj, glr=glr, lg=lg, o=o, states=states, o2=o2)
    return _mm(o2, w["w_out"], add=x, name="mix_out")


def _gla_layer_bwd(h, dy, dyb, w, seq, sv):
    d = dy.shape[1]
    dkt = d // 2
    dk, dv = dkt // GLA_HEADS, d // GLA_HEADS
    proj, glr, lg, o = sv["proj"], sv["glr"], sv["lg"], sv["o"]
    g = {}
    do2 = _mm(dyb, w["w_out_t"], out_dtype=F32, name="gla_out_dx")
    g["w_out"] = _mm_tn(sv["o2"], dyb, name="mix_out_dw")

    def post_bwd(ov, rv, ctv, gv):
        _, vjp = jax.vjp(functools.partial(_head_rms_gate, heads=GLA_HEADS), ov, rv.astype(F32), gv)
        d_o, d_r, d_g = vjp(ctv.astype(F32))
        return [d_o, d_r], [d_g]

    (d_o, d_r), (g["norm"],) = _tile_call("gla_post_bwd", post_bwd, [o, (proj, d, 2), do2], [w["norm"]],
                                          [(d, F32), (d, BF16)], [w["norm"].shape], 256)
    q, k, v = (proj, dkt, 0), (proj, dkt, 1), (proj, d, 1)
    dq, dkk, dvv, dlg = _gla_bwd(q, k, v, (lg, dkt, 0), sv["states"], d_o, heads=GLA_HEADS, dk=dk, dv=dv,
                                 scale=dk ** -0.5, chunk=GLA_CHUNK, seq=seq, out_dtypes=(BF16, BF16, BF16),
                                 name="gla_core_bwd", side=sv.pop("side", None))

    def gate_bwd(glrv, ctv, wv, bv):
        _, vjp = jax.vjp(_gla_gate, glrv.astype(F32), wv, bv)
        d_glr, d_w, d_b = vjp(ctv)
        return [d_glr], [d_w, d_b]

    (dglr,), (g["w_g2p"], g["b_g2"]) = _tile_call("gla_gate_bwd", gate_bwd, [glr, dlg], [w["w_g2p"], w["b_g2"]],
                                                  [(LANES, BF16)], [w["w_g2p"].shape, w["b_g2"].shape], 512)
    dproj = jnp.concatenate([dq, dkk, dvv, d_r], axis=1)
    g["w_main"] = _mm_tn(h, dproj, name="gla_in_dw")
    g["w_glr"] = _mm_tn(h, dglr, name="gla_in_g_dw")
    dh = _mm(dproj, w["w_main_t"], out_dtype=F32, name="gla_in_dx")
    dh = _mm(dglr, w["w_glr_t"], add=dh, out_dtype=BF16, name="gla_in_g_dx")
    return dh, g


def _cv_layer_fwd(x, h, w, seq, sv):
    d = x.shape[1]
    pre = _mm(h, w["w_in"], out_dtype=BF16, name="cv_in")
    (y1,), _ = _tile_call("cv_glu", lambda a, gt, b: ([_glu(a.astype(F32), gt.astype(F32), b)], []),
                          [(pre, d, 0), (pre, d, 1)], [w["b_in"]], [(d, F32)], [], 512)
    y2 = _dwconv_fwd(y1, w["w_dw"], w["b_dw"], seq, "cv_conv", sv.pop("side", None))
    (y3,), _ = _tile_call("cv_ln", lambda y, a, b: ([_ln_silu(y, a, b)], []), [y2], [w["ln_g"], w["ln_b"]],
                          [(d, BF16)], [], 512)
    sv.update(pre=pre, y1=y1, y2=y2, y3=y3)
    return _mm(y3, w["w_out"], bias=w["b_out"], add=x, name="mix_out_b")


def _cv_layer_bwd(h, dy, dyb, w, seq, sv):
    d = dy.shape[1]
    pre = sv["pre"]
    g = {}
    dy3 = _mm(dyb, w["w_out_t"], out_dtype=BF16, name="mix_out_dx")
    g["w_out"] = _mm_tn(sv["y3"], dyb, name="mix_out_dw")
    g["b_out"] = _colsum(dy, "bias_out_dw")

    def ln_bwd(yv, ctv, av, bv):
        _, vjp = jax.vjp(_ln_silu, yv, av, bv)
        d_y, d_a, d_b = vjp(ctv.astype(F32))
        return [d_y], [d_a, d_b]

    (dy2,), (g["ln_g"], g["ln_b"]) = _tile_call("cv_ln_bwd", ln_bwd, [sv["y2"], dy3], [w["ln_g"], w["ln_b"]],
                                                [(d, F32)], [w["ln_g"].shape, w["ln_b"].shape], 512)
    dy1, g["w_dw"], g["b_dw"] = _dwconv_bwd(sv["y1"], dy2, w["w_dw"], seq, "cv_conv_bwd", sv.pop("side", None))

    def glu_bwd(av, gv, ctv, bv):
        _, vjp = jax.vjp(_glu, av.astype(F32), gv.astype(F32), bv)
        d_a, d_g, d_b = vjp(ctv)
        return [jnp.concatenate([d_a, d_g], axis=1)], [d_b]

    (dpre,), (g["b_in"],) = _tile_call("cv_glu_bwd", glu_bwd, [(pre, d, 0), (pre, d, 1), dy1], [w["b_in"]],
                                       [(2 * d, BF16)], [w["b_in"].shape], 512)
    g["w_in"] = _mm_tn(h, dpre, name="in2_dw")
    dh = _mm(dpre, w["w_in_t"], out_dtype=BF16, name="in2_dx")
    return dh, g


def _sg_layer_fwd(x, h, w, seq, sv):
    d = x.shape[1]
    pre = _mm(h, w["w_in"], out_dtype=BF16, name="sg_in", side=sv.pop("side_in", None))
    pars = [w["b_in"], w["ln_g"], w["ln_b"], w["w_s"], w["b_st"]]
    (p,), _ = _tile_call("sg_gate", lambda pv, *ps: ([_sgu(pv.astype(F32), *ps)], []), [pre], pars, [(d, BF16)], [],
                         SGU_CHUNK, side=sv.pop("side", None))
    sv.update(pre=pre, p=p)
    return _mm(p, w["w_out"], bias=w["b_out"], add=x, name="mix_out_b")


def _sg_layer_bwd(h, dy, dyb, w, seq, sv):
    d = dy.shape[1]
    g = {}
    dp = _mm(dyb, w["w_out_t"], out_dtype=BF16, name="mix_out_dx")
    g["w_out"] = _mm_tn(sv["p"], dyb, name="mix_out_dw")
    g["b_out"] = _colsum(dy, "bias_out_dw")
    pars = [w["b_in"], w["ln_g"], w["ln_b"], w["w_s"], w["b_st"]]

    def sgu_bwd(pv, ctv, *ps):
        _, vjp = jax.vjp(_sgu, pv.astype(F32), *ps)
        grads = vjp(ctv.astype(F32))
        return [grads[0]], list(grads[1:])

    (dpre,), (g["b_in"], g["ln_g"], g["ln_b"], g["w_s"], g["b_st"]) = _tile_call(
        "sg_gate_bwd", sgu_bwd, [sv["pre"], dp], pars, [(2 * d, BF16)], [p.shape for p in pars], SGU_CHUNK,
        side=sv.pop("side", None))
    g["w_in"] = _mm_tn(h, dpre, name="in2_dw")
    dh = _mm(dpre, w["w_in_t"], out_dtype=BF16, name="in2_dx")
    return dh, g


def _hg_layer_fwd(x, h, w, seq, sv, layer):
    d = x.shape[1]
    heads = d // HGRN_EXPAND
    proj = _mm(h, w["w_in"], out_dtype=BF16, name="hg_in", side=sv.pop("side_in", None))
    pre = functools.partial(_hgrn_pre, layer=layer)
    (qs, kk, lg), _ = _tile_call("hg_pre", lambda qv, fv, tb: (list(pre(qv.astype(F32), fv.astype(F32), tb)), []),
                                 [(proj, d, 0), (proj, d, 1)], [w["lb_table"]], [(d, BF16), (d, F32), (d, F32)], [], 256)
    o, states = _gla_fwd((qs, d, 0), (kk, d, 0), (proj, d, 2), (lg, d, 0), heads=heads, dk=HGRN_EXPAND,
                         dv=HGRN_EXPAND, scale=1.0, chunk=HGRN_CHUNK, seq=seq, name="hg_core",
                         side=sv.pop("side", None))
    (o2,), _ = _tile_call("hg_post", lambda ov, gv, nv: ([_head_rms_gate(ov, gv.astype(F32), nv, heads)], []),
                          [o, (proj, d, 3)], [w["norm"]], [(d, BF16)], [], 256)
    sv.update(proj=proj, qs=qs, kk=kk, lg=lg, o=o, states=states, o2=o2)
    return _mm(o2, w["w_out"], add=x, name="mix_out")


def _hg_layer_bwd(h, dy, dyb, w, seq, sv, layer):
    d = dy.shape[1]
    heads = d // HGRN_EXPAND
    proj = sv["proj"]
    g = {}
    do2 = _mm(dyb, w["w_out_t"], out_dtype=BF16, name="mix_out_dx")
    g["w_out"] = _mm_tn(sv["o2"], dyb, name="mix_out_dw")

    def post_bwd(ov, gv, ctv, nv):
        _, vjp = jax.vjp(functools.partial(_head_rms_gate, heads=heads), ov, gv.astype(F32), nv)
        d_o, d_g, d_n = vjp(ctv.astype(F32))
        return [d_o, d_g], [d_n]

    (d_o, d_gate), (g["norm"],) = _tile_call("hg_post_bwd", post_bwd, [sv["o"], (proj, d, 3), do2], [w["norm"]],
                                             [(d, F32), (d, BF16)], [w["norm"].shape], 256)
    dqs, dkk, di, dlg = _gla_bwd((sv["qs"], d, 0), (sv["kk"], d, 0), (proj, d, 2), (sv["lg"], d, 0), sv["states"], d_o,
                                 heads=heads, dk=HGRN_EXPAND, dv=HGRN_EXPAND, scale=1.0, chunk=HGRN_CHUNK, seq=seq,
                                 out_dtypes=(F32, F32, BF16), name="hg_core_bwd", side=sv.pop("side", None))

    def pre_bwd(qv, fv, c1, c2, c3, tb):
        _, vjp = jax.vjp(functools.partial(_hgrn_pre, layer=layer), qv.astype(F32), fv.astype(F32), tb)
        d_q, d_f, d_t = vjp((c1, c2, c3))
        return [jnp.concatenate([d_q, d_f], axis=1)], [d_t]

    (dqf,), (g["lb_table"],) = _tile_call("hg_pre_bwd", pre_bwd, [(proj, d, 0), (proj, d, 1), dqs, dkk, dlg],
                                          [w["lb_table"]], [(2 * d, BF16)], [w["lb_table"].shape], 256)
    dproj = jnp.concatenate([dqf, di, d_gate], axis=1)
    g["w_in"] = _mm_tn(h, dproj, name="hg_in_dw")
    dh = _mm(dproj, w["w_in_t"], out_dtype=BF16, name="hg_in_dx")
    return dh, g


_MIXERS = ("gla", "cv", "sg", "hg")


_BIG_KEYS = {"gla": ("w_main", "w_glr", "w_out"), "cv": ("w_in", "w_out"), "sg": ("w_in", "w_out"), "hg": ("w_in", "w_out"),
             "ffn": ("w_up_gate", "w_up_val", "w_down")}


def _local_step(x, target, w, seq, get_big, put_big, ride=lambda kind, layer, forward: None, put_early=None):
    depth = w["norm_mix"].shape[0]
    d = x.shape[1]
    saved, big = [], {}
    for layer in range(depth):
        mixer = _MIXERS[layer % 4]
        rides = ride(mixer, layer, True)
        rides = rides if isinstance(rides, list) else [None, rides]
        sv = {"x_in": x, "side_in": rides[0], "side": rides[1]}
        sv["h"] = _rms_fwd(x, w["norm_mix"][layer:layer + 1], "mix_norm")
        big[mixer, layer] = get_big(mixer, layer)
        wm = dict(w[mixer], **big[mixer, layer])
        if mixer == "gla":
            x = _gla_layer_fwd(x, sv["h"], wm, seq, sv)
        elif mixer == "cv":
            x = _cv_layer_fwd(x, sv["h"], wm, seq, sv)
        elif mixer == "sg":
            x = _sg_layer_fwd(x, sv["h"], wm, seq, sv)
        else:
            x = _hg_layer_fwd(x, sv["h"], wm, seq, sv, layer)
        sv["x_mid"] = x
        big["ffn", layer] = get_big("ffn", layer)
        sv["ffn"] = {"side": ride("ffn", layer, True)}
        wf = dict(w["ffn"][layer], norm=w["norm_ffn"][layer:layer + 1], **big["ffn", layer])
        x = _ffn_fwd(x, wf, seq, sv["ffn"])
        big["ffn", layer].pop("late", None)
        big["ffn", layer].update(sv["ffn"].pop("late_w", {}))
        saved.append(sv)

    def head(xv, tv, gv):
        y, vjp = jax.vjp(_rms, xv, gv)
        err = y - tv
        dx, dg = vjp(err * (1.0 / d))
        part = 0.5 * jnp.sum(jnp.mean(err * err, axis=-1, keepdims=True), axis=0, keepdims=True)
        return [dx, dx], [jnp.broadcast_to(part, (1, LANES)), dg]

    (dx, dxb), (loss, g_final) = _tile_call("loss_head", head, [x, target], [w["norm_final"]], [(d, F32), (d, BF16)],
                                            [(1, LANES), (1, d)], 512)
    grads = {"norm_final": g_final, "norm_mix": [None] * depth, "norm_ffn": [None] * depth, "ffn": [None] * depth}
    for layer in reversed(range(depth)):
        mixer = _MIXERS[layer % 4]
        sv = saved[layer]
        wf = dict(w["ffn"][layer], norm=w["norm_ffn"][layer:layer + 1], **big["ffn", layer])
        sv["ffn"]["side"] = ride("ffn", layer, False)
        if put_early is not None:
            sv["ffn"]["put_early"] = functools.partial(put_early, layer)
        dx, dxb, gf = _ffn_bwd(sv["x_mid"], dx, dxb, wf, seq, sv["ffn"])
        put_big("ffn", layer, {k: gf.pop(k) for k in _BIG_KEYS["ffn"]})
        sv["side"] = ride(mixer, layer, False)
        grads["norm_ffn"][layer] = gf.pop("norm")
        grads["ffn"][layer] = gf
        wm = dict(w[mixer], **big[mixer, layer])
        if mixer == "gla":
            dh, gm = _gla_layer_bwd(sv["h"], dx, dxb, wm, seq, sv)
        elif mixer == "cv":
            dh, gm = _cv_layer_bwd(sv["h"], dx, dxb, wm, seq, sv)
        elif mixer == "sg":
            dh, gm = _sg_layer_bwd(sv["h"], dx, dxb, wm, seq, sv)
        else:
            dh, gm = _hg_layer_bwd(sv["h"], dx, dxb, wm, seq, sv, layer)
        put_big(mixer, layer, {k: gm.pop(k) for k in _BIG_KEYS[mixer]})
        grads[mixer] = gm
        last = ride(mixer, layer, False) if layer == 0 else None
        dx, dxb, grads["norm_mix"][layer] = _rms_bwd(sv["x_in"], w["norm_mix"][layer:layer + 1], dh, dx, "mix_norm_bwd",
                                                     last)
    return loss, dx, grads


def _prep_small(p):
    row = lambda a: a.reshape(1, -1).astype(F32)
    w = {"norm_mix": p["norm_mix"].astype(F32), "norm_ffn": p["norm_ffn"].astype(F32), "norm_final": row(p["norm_final"])}
    w["gla"] = dict(w_g2p=jnp.pad(p["gla_w_g2"][0].astype(F32), ((0, LANES - GLA_RANK), (0, 0))), b_g2=row(p["gla_b_g2"]),
                    norm=row(p["gla_norm"]))
    w["cv"] = dict(b_in=row(p["cv_b_in"]), w_dw=p["cv_w_dw"][0].astype(F32), b_dw=row(p["cv_b_dw"]), ln_g=row(p["cv_ln_g"]),
                   ln_b=row(p["cv_ln_b"]), b_out=row(p["cv_b_out"]))
    b_st = jnp.pad(p["sg_b_s"][0].astype(F32).T, ((0, 0), (0, LANES - SGU_GROUPS)))
    w["sg"] = dict(b_in=row(p["sg_b_in"]), ln_g=row(p["sg_ln_g"]), ln_b=row(p["sg_ln_b"]), w_s=p["sg_w_s"][0].astype(F32),
                   b_st=b_st, b_out=row(p["sg_b_out"]))
    w["hg"] = dict(lb_table=p["hg_lb_table"].astype(F32), norm=row(p["hg_norm"]))
    w["ffn"] = [dict(w_dw=p["ffn_w_dw"][layer].astype(F32)) for layer in range(p["ffn_w_dw"].shape[0])]
    return w


def _small_grads(g):
    gla, cv, sg, hg = g["gla"], g["cv"], g["sg"], g["hg"]
    return {
        "norm_mix": jnp.concatenate(g["norm_mix"], axis=0), "norm_ffn": jnp.concatenate(g["norm_ffn"], axis=0),
        "norm_final": g["norm_final"][0],
        "gla_w_g2": gla["w_g2p"][:GLA_RANK][None], "gla_b_g2": gla["b_g2"], "gla_norm": gla["norm"],
        "cv_b_in": cv["b_in"], "cv_w_dw": cv["w_dw"][None], "cv_b_dw": cv["b_dw"], "cv_ln_g": cv["ln_g"],
        "cv_ln_b": cv["ln_b"], "cv_b_out": cv["b_out"],
        "sg_b_in": sg["b_in"], "sg_ln_g": sg["ln_g"], "sg_ln_b": sg["ln_b"], "sg_w_s": sg["w_s"][None],
        "sg_b_s": sg["b_st"][:, :SGU_GROUPS].T[None], "sg_b_out": sg["b_out"],
        "hg_lb_table": hg["lb_table"], "hg_norm": hg["norm"],
        "ffn_w_dw": jnp.stack([f["w_dw"] for f in g["ffn"]]),
    }


def _oriented(kind, mats):
    if kind == "ffn":
        (up, up_t), (down, down_t) = mats["w_up"], mats["w_down"]
        f = down.shape[0]
        return dict(w_up=up, w_up_t_gate=up_t[:f], w_up_t_val=up_t[f:], w_down=down, w_down_t=down_t)
    (w_in, w_in_t), (w_out, w_out_t) = mats["w_in"], mats["w_out"]
    if kind != "gla":
        return dict(w_in=w_in, w_in_t=w_in_t, w_out=w_out, w_out_t=w_out_t)
    n_main = w_in.shape[1] - GLA_RANK
    return dict(w_main=w_in[:, :n_main], w_glr=jnp.pad(w_in[:, n_main:], ((0, 0), (0, LANES - GLA_RANK))),
                w_main_t=w_in_t[:n_main], w_glr_t=jnp.pad(w_in_t[n_main:], ((0, LANES - GLA_RANK), (0, 0))),
                w_out=w_out, w_out_t=w_out_t)


def _all_gather(x, *, name):
    m_per, n = x.shape

    def body(x_ref, out_ref, send_sems, recv_sems, local_sem):
        mx, my, mc = lax.axis_index("x"), lax.axis_index("y"), lax.axis_index("c")
        me, sibling = (mx, my, mc), (mx, my, 1 - mc)
        chips = [(1 - mx, my), (mx, 1 - my), (1 - mx, 1 - my)]

        def rows(px, py, pc):
            return out_ref.at[pl.ds((4 * px + 2 * py + pc) * m_per, m_per), :]

        def copy(k, block, to, src=None):
            return pltpu.make_async_remote_copy(
                src_ref=rows(*block) if src is None else src, dst_ref=rows(*block), send_sem=send_sems.at[k],
                recv_sem=recv_sems.at[k], device_id=to, device_id_type=MESH)

        mine = pltpu.make_async_copy(x_ref, rows(*me), local_sem)
        mine.start()
        first = [copy(0, me, sibling, src=x_ref)]
        first += [copy(1 + j, me, (*chip, mc), src=x_ref) for j, chip in enumerate(chips)]
        for cp in first:
            cp.start()
        passed = [copy(4 + j, (*chip, mc), sibling) for j, chip in enumerate(chips)]
        for j, chip in enumerate(chips):
            copy(1 + j, (*chip, mc), me).wait_recv()
            passed[j].start()
        copy(0, sibling, me).wait_recv()
        for j, chip in enumerate(chips):
            copy(4 + j, (*chip, 1 - mc), me).wait_recv()
        for cp in first + passed:
            cp.wait_send()
        mine.wait()

    return pl.pallas_call(
        body, name=name, out_shape=jax.ShapeDtypeStruct((N_DEV * m_per, n), x.dtype),
        in_specs=[pl.BlockSpec(memory_space=pltpu.VMEM)], out_specs=pl.BlockSpec(memory_space=pltpu.VMEM),
        scratch_shapes=[pltpu.SemaphoreType.DMA((7,)), pltpu.SemaphoreType.DMA((7,)), pltpu.SemaphoreType.DMA],
    )(x)


def _my_index():
    return 4 * lax.axis_index("x") + 2 * lax.axis_index("y") + lax.axis_index("c")


def _gather_stage(srcs, *, name):
    n = len(srcs)

    def body(*refs):
        x_refs, out_refs = refs[:n], refs[n:2 * n]
        send_sems, recv_sems, local_sems = refs[2 * n:]
        mx, my, mc = lax.axis_index("x"), lax.axis_index("y"), lax.axis_index("c")
        me, sibling = (mx, my, mc), (mx, my, 1 - mc)
        chips = [(1 - mx, my), (mx, 1 - my), (1 - mx, 1 - my)]

        def slot(i, px, py, pc):
            return out_refs[i].at[4 * px + 2 * py + pc]

        def copy(i, k, block, to, src=None):
            return pltpu.make_async_remote_copy(
                src_ref=slot(i, *block) if src is None else src, dst_ref=slot(i, *block), send_sem=send_sems.at[7 * i + k],
                recv_sem=recv_sems.at[7 * i + k], device_id=to, device_id_type=MESH)

        mine = [pltpu.make_async_copy(x_refs[i], slot(i, *me), local_sems.at[i]) for i in range(n)]
        first = [copy(i, 0, me, sibling, src=x_refs[i]) for i in range(n)]
        first += [copy(i, 1 + j, me, (*chip, mc), src=x_refs[i]) for j, chip in enumerate(chips) for i in range(n)]
        for cp in mine + first:
            cp.start()
        passed = []
        for j, chip in enumerate(chips):
            for i in range(n):
                copy(i, 1 + j, (*chip, mc), me).wait_recv()
                passed.append(copy(i, 4 + j, (*chip, mc), sibling))
                passed[-1].start()
        for i in range(n):
            copy(i, 0, sibling, me).wait_recv()
            for j, chip in enumerate(chips):
                copy(i, 4 + j, (*chip, 1 - mc), me).wait_recv()
        for cp in first + passed:
            cp.wait_send()
        for cp in mine:
            cp.wait()

    any_space = pl.BlockSpec(memory_space=pl.ANY)
    return pl.pallas_call(
        body, name=name, out_shape=[jax.ShapeDtypeStruct((N_DEV,) + s.shape, s.dtype) for s in srcs],
        in_specs=[any_space] * n, out_specs=[any_space] * n,
        scratch_shapes=[pltpu.SemaphoreType.DMA((7 * n,)), pltpu.SemaphoreType.DMA((7 * n,)), pltpu.SemaphoreType.DMA((n,))],
    )(*srcs)


def _adamw_math(g, w, m, v):
    c1, c2 = 1.0 - ADAM_B1 ** ADAM_STEP, 1.0 - ADAM_B2 ** ADAM_STEP
    m_new = ADAM_B1 * m + (1.0 - ADAM_B1) * g
    v_new = ADAM_B2 * v + (1.0 - ADAM_B2) * (g * g)
    delta = -ADAM_LR * ((m_new / c1) / (jnp.sqrt(v_new / c2) + ADAM_EPS) + ADAM_WD * w)
    return delta, m_new, v_new


def _adamw_big(slots, w, m, v, layer, *, name):
    _, r, c = slots.shape
    tr = _divisor_tile(r, max(8, (200 * 1024) // c // 8 * 8), 8)

    def body(s_ref, w_ref, m_ref, v_ref, g_out, d_out, m_out, v_out):
        g = s_ref[0].astype(F32)
        for p in range(1, N_DEV):
            g = g + s_ref[p].astype(F32)
        g_out[...] = g
        d_out[...], m_out[...], v_out[...] = _adamw_math(g, w_ref[...], m_ref[...], v_ref[...])

    blk = pl.BlockSpec((tr, c), lambda i: (i, 0))
    lay = pl.BlockSpec((None, tr, c), lambda i: (layer, i, 0))
    return pl.pallas_call(
        body, name=name, grid=(r // tr,), in_specs=[pl.BlockSpec((N_DEV, tr, c), lambda i: (0, i, 0)), lay, lay, lay],
        out_specs=[blk] * 4, out_shape=[jax.ShapeDtypeStruct((r, c), F32)] * 4,
        compiler_params=pltpu.CompilerParams(dimension_semantics=("parallel",)),
    )(slots, w, m, v)


def _sum_small(got, r_re, r_sh, *, name):
    per_dev = r_re + N_DEV * r_sh

    def body(got_ref, re_ref, sh_ref):
        mine = r_re + _my_index() * r_sh
        acc_re = got_ref[0:r_re, :]
        acc_sh = got_ref[pl.ds(pl.multiple_of(mine, 8), r_sh), :]
        for p in range(1, N_DEV):
            acc_re = acc_re + got_ref[p * per_dev:p * per_dev + r_re, :]
            acc_sh = acc_sh + got_ref[pl.ds(pl.multiple_of(p * per_dev + mine, 8), r_sh), :]
        re_ref[...] = acc_re
        sh_ref[...] = acc_sh

    return pl.pallas_call(body, name=name, out_shape=[jax.ShapeDtypeStruct((r_re, LANES), F32),
                                                       jax.ShapeDtypeStruct((r_sh, LANES), F32)])(got)


def _adamw_small(gs, ws, ms, vs, *, name):
    n = len(gs)

    def body(*refs):
        ins, outs = refs[:4 * n], refs[4 * n:]
        for i in range(n):
            res = _adamw_math(ins[i][...], ins[n + i][...], ins[2 * n + i][...], ins[3 * n + i][...])
            for j in range(3):
                outs[j * n + i][...] = res[j]

    out = pl.pallas_call(body, name=name, out_shape=[jax.ShapeDtypeStruct(a.shape, F32) for a in ws] * 3)(*gs, *ws, *ms, *vs)
    return out[:n], out[n:2 * n], out[2 * n:]


def _layout(shapes, row_align, total_align):
    lay, off = {}, 0
    for name, shape in shapes.items():
        size = int(np.prod(shape))
        rows = -(-size // LANES)
        rows = -(-rows // row_align) * row_align
        lay[name] = (off, rows, size, tuple(shape))
        off += rows
    return lay, -(-off // total_align) * total_align


def _pack(arrs, lay, total, dtype, lead=()):
    parts = []
    nl = len(lead)
    for name, (off, rows, size, shape) in lay.items():
        flat = arrs[name].astype(dtype).reshape(*lead, size)
        parts.append(jnp.pad(flat, [(0, 0)] * nl + [(0, rows * LANES - size)]).reshape(*lead, rows, LANES))
    used = sum(v[1] for v in lay.values())
    if total > used:
        parts.append(jnp.zeros((*lead, total - used, LANES), dtype))
    return jnp.concatenate(parts, axis=nl)


def _unpack(buf, lay, lead=()):
    out = {}
    nl = len(lead)
    for name, (off, rows, size, shape) in lay.items():
        part = lax.slice_in_dim(buf, off, off + rows, axis=nl).reshape(*lead, rows * LANES)
        out[name] = lax.slice_in_dim(part, 0, size, axis=nl).reshape(*lead, *shape)
    return out


_SHARD_AXIS = {
    "norm_mix": None, "norm_ffn": None, "norm_final": None, "gla_w_in": 2, "gla_w_g2": 2, "gla_b_g2": None,
    "gla_norm": None, "gla_w_out": 1, "cv_w_in": 2, "cv_b_in": 1, "cv_w_dw": 2, "cv_b_dw": 1, "cv_ln_g": 1,
    "cv_ln_b": 1, "cv_w_out": 1, "cv_b_out": 1, "sg_w_in": 2, "sg_b_in": 1, "sg_ln_g": 1, "sg_ln_b": 1, "sg_w_s": None,
    "sg_b_s": None, "sg_w_out": 1, "sg_b_out": 1, "hg_w_in": 2, "hg_lb_table": None, "hg_norm": None, "hg_w_out": 1,
    "ffn_w_up": 2, "ffn_w_dw": 2, "ffn_w_down": 1,
}
_MATMUL_WEIGHTS = ("gla_w_in", "gla_w_out", "cv_w_in", "cv_w_out", "sg_w_in", "sg_w_out", "hg_w_in", "hg_w_out",
                   "ffn_w_up", "ffn_w_down")
_NAMES = tuple(_SHARD_AXIS)


def kernel(x, norm_mix, norm_ffn, norm_final, gla_w_in, gla_w_g2, gla_b_g2, gla_norm, gla_w_out, cv_w_in, cv_b_in, cv_w_dw, cv_b_dw, cv_ln_g, cv_ln_b, cv_w_out, cv_b_out, sg_w_in, sg_b_in, sg_ln_g, sg_ln_b, sg_w_s, sg_b_s, sg_w_out, sg_b_out, hg_w_in, hg_lb_table, hg_norm, hg_w_out, ffn_w_up, ffn_w_dw, ffn_w_down, loss_target, m_norm_mix, m_norm_ffn, m_norm_final, m_gla_w_in, m_gla_w_g2, m_gla_b_g2, m_gla_norm, m_gla_w_out, m_cv_w_in, m_cv_b_in, m_cv_w_dw, m_cv_b_dw, m_cv_ln_g, m_cv_ln_b, m_cv_w_out, m_cv_b_out, m_sg_w_in, m_sg_b_in, m_sg_ln_g, m_sg_ln_b, m_sg_w_s, m_sg_b_s, m_sg_w_out, m_sg_b_out, m_hg_w_in, m_hg_lb_table, m_hg_norm, m_hg_w_out, m_ffn_w_up, m_ffn_w_dw, m_ffn_w_down, v_norm_mix, v_norm_ffn, v_norm_final, v_gla_w_in, v_gla_w_g2, v_gla_b_g2, v_gla_norm, v_gla_w_out, v_cv_w_in, v_cv_b_in, v_cv_w_dw, v_cv_b_dw, v_cv_ln_g, v_cv_ln_b, v_cv_w_out, v_cv_b_out, v_sg_w_in, v_sg_b_in, v_sg_ln_g, v_sg_ln_b, v_sg_w_s, v_sg_b_s, v_sg_w_out, v_sg_b_out, v_hg_w_in, v_hg_lb_table, v_hg_norm, v_hg_w_out, v_ffn_w_up, v_ffn_w_dw, v_ffn_w_down):
    local = dict(locals())
    wts = {n: local[n] for n in _NAMES}
    mom = {n: local["m_" + n] for n in _NAMES}
    var = {n: local["v_" + n] for n in _NAMES}
    small_all = [n for n in _NAMES if n not in _MATMUL_WEIGHTS]
    small_sharded = [n for n in small_all if _SHARD_AXIS[n] is not None]
    bsz, seq, d = x.shape
    depth = norm_mix.shape[0]

    stages = {}
    for layer in range(depth):
        kind = _MIXERS[layer % 4]
        stages[kind, layer] = {"w_in": (kind + "_w_in", layer // 4), "w_out": (kind + "_w_out", layer // 4)}
        stages["ffn", layer] = {"w_up": ("ffn_w_up", layer), "w_down": ("ffn_w_down", layer)}

    order = list(stages)
    shards = lambda stage: [wts[nm][idx].astype(BF16) for nm, idx in stages[stage].values()]
    gathers = {order[0]: [_Side(shards(order[0]), False)]}
    gathers[order[0]][0].lands = _gather_stage(gathers[order[0]][0].srcs, name="gather_first")
    scatters, waiting, down_gathers, down_scatters = {}, [], {}, {}

    def ride(kind, layer, forward):
        if not forward:
            return waiting.pop() if waiting else None
        at = order.index((kind, layer)) + 1
        if at == len(order):
            return None
        srcs = shards(order[at])
        if order[at][0] == "ffn":
            down_gathers[order[at][1]] = _Side(srcs[1:], False)
            srcs = srcs[:1]
            if kind != "cv":
                half = srcs[0].shape[0] // 2
                gathers[order[at]] = [_Side([srcs[0][:half]], False), _Side([srcs[0][half:]], False)]
                return gathers[order[at]]
        gathers[order[at]] = [_Side(srcs, False)]
        return gathers[order[at]][0]

    lay_sw, r_sw = _layout({n: wts[n].shape for n in small_sharded}, 8, 8)
    got_sw = _all_gather(_pack(wts, lay_sw, r_sw, F32), name="gather_small_weights")
    parts = _unpack(got_sw.reshape(N_DEV, r_sw, LANES), lay_sw, (N_DEV,))
    full_small = {n: wts[n] for n in small_all if _SHARD_AXIS[n] is None}
    for n in small_sharded:
        ax, shape = _SHARD_AXIS[n], wts[n].shape
        full_small[n] = jnp.moveaxis(parts[n], 0, ax).reshape(shape[:ax] + (N_DEV * shape[ax],) + shape[ax + 1:])

    def full_size(nm, land):
        _, r, c = land.shape
        if _SHARD_AXIS[nm] == 2:
            return land.transpose(1, 0, 2).reshape(r, N_DEV * c), land.transpose(0, 2, 1).reshape(N_DEV * c, r)
        return land.reshape(N_DEV * r, c), land.reshape(N_DEV * r, c).T

    def get_big(kind, layer):
        names = [nm for nm, _ in stages[kind, layer].values()]
        lands = [land for side in gathers[kind, layer] for land in side.lands]
        if kind != "ffn" or layer not in down_gathers:
            return _oriented(kind, {key: full_size(nm, land) for key, nm, land in zip(stages[kind, layer], names, lands)})
        parts = [full_size(names[0], land) for land in lands]
        up = parts[0][0] if len(parts) == 1 else jnp.concatenate([p[0] for p in parts], axis=0)
        up_t = parts[0][1] if len(parts) == 1 else jnp.concatenate([p[1] for p in parts], axis=1)
        f = up.shape[1] // 2

        def down_landed():
            down, down_t = full_size(names[1], down_gathers[layer].lands[0])
            return dict(w_down=down, w_down_t=down_t)

        return dict(w_up=up, w_up_t_gate=up_t[:f], w_up_t_val=up_t[f:], late=(down_gathers[layer], down_landed))

    def put_big(kind, layer, g):
        if kind == "ffn":
            k, f = g["w_up_gate"].shape
            halves = [g[key].reshape(k, N_DEV // 2, 2 * f // N_DEV) for key in ("w_up_gate", "w_up_val")]
            w_in = jnp.concatenate(halves, axis=1)
        else:
            w_in = jnp.concatenate([g["w_main"], g["w_glr"][:, :GLA_RANK]], axis=1) if kind == "gla" else g["w_in"]
            w_in = w_in.reshape(w_in.shape[0], N_DEV, w_in.shape[1] // N_DEV)
        sends = [w_in.transpose(1, 0, 2).astype(BF16)]
        if kind != "ffn":
            sends.append(row_slots(g["w_out"]))
        scatters[kind, layer] = _Side(sends, True)
        waiting.append(scatters[kind, layer])

    def row_slots(grad):
        return grad.reshape(N_DEV, grad.shape[0] // N_DEV, grad.shape[1]).astype(BF16)

    def put_early(layer, grad_w_down):
        down_scatters[layer] = _Side([row_slots(grad_w_down)], True)
        return down_scatters[layer]

    loss, dx, grads = _local_step(x.reshape(bsz * seq, d), loss_target.reshape(bsz * seq, d), _prep_small(full_small), seq,
                                  get_big, put_big, ride, put_early)
    loss = lax.psum(loss[0, 0], ("x", "y", "c"))

    gs = _small_grads(grads)
    small_repl = [n for n in small_all if _SHARD_AXIS[n] is None]
    lay_re, r_re = _layout({n: wts[n].shape for n in small_repl}, 8, 8)
    slots = {}
    for n in small_sharded:
        ax, shape = _SHARD_AXIS[n], wts[n].shape
        slots[n] = jnp.moveaxis(gs[n].reshape(shape[:ax] + (N_DEV, shape[ax]) + shape[ax + 1:]), ax, 0)
    sent = jnp.concatenate([_pack(gs, lay_re, r_re, F32), _pack(slots, lay_sw, r_sw, F32, (N_DEV,)).reshape(-1, LANES)])
    sum_re, sum_sh = _sum_small(_all_gather(sent, name="gather_small_grads"), r_re, r_sw, name="sum_small_grads")
    g_own = _unpack(sum_re, lay_re)
    g_own.update(_unpack(sum_sh, lay_sw))
    two_d = lambda a: a.reshape(-1, a.shape[-1])
    upd = _adamw_small(*[[two_d(src[n]) for n in small_all] for src in (g_own, wts, mom, var)], name="adamw_small")
    results = {n: [g_own[n]] + [part[i].reshape(wts[n].shape) for part in upd] for i, n in enumerate(small_all)}

    per_layer = {}
    for (kind, layer), side in scatters.items():
        lands = side.lands
        if kind == "ffn":
            lands = list(lands) + down_scatters[layer].lands
        for (nm, idx), land in zip(stages[kind, layer].values(), lands):
            three_d = lambda a: a.reshape((a.shape[0],) + land.shape[1:])
            per_layer.setdefault(nm, {})[idx] = _adamw_big(land, three_d(wts[nm]), three_d(mom[nm]), three_d(var[nm]), idx,
                                                           name="adamw_" + nm)
    for nm, by_idx in per_layer.items():
        outs = [by_idx[i] for i in range(len(by_idx))]
        results[nm] = [(outs[0][j] if len(outs) == 1 else jnp.stack([o[j] for o in outs])).reshape(wts[nm].shape)
                       for j in range(4)]
    out = [loss, dx.reshape(bsz, seq, d)]
    for j in range(4):
        out += [results[n][j] for n in _NAMES]
    return tuple(out)
```
